```python
import jax, jax.numpy as jnp
from jax import lax
import numpy as np

D_MODEL = 1024
BATCH = 8
SEQ = 4096
DEPTH = 1

HEAD_DIM = 64
SSD_HEADS = 16
SSD_D_INNER = SSD_HEADS * HEAD_DIM
SSD_GROUPS = 4
SSD_STATE = 128
SSD_CONV = 4
SSD_CHUNK = 128
CONV_CH = SSD_D_INNER + 2 * SSD_GROUPS * SSD_STATE
ATT_HEADS = 16
ATT_D = ATT_HEADS * HEAD_DIM
DILATED_PATTERNS = ((128, 1), (512, 4), (2048, 16))
MIX_WIDTH = SSD_D_INNER + ATT_D
IN_PROJ_WIDTH = SSD_D_INNER + CONV_CH + SSD_HEADS + 3 * ATT_D
D_FF = 4 * D_MODEL
N_MOD = 6
EPS = 1e-6

kernel_name = 'hybrid_ssd_dilated_attn_block'


def _rms(t):
    tf = t.astype(jnp.float32)
    return tf * lax.rsqrt(jnp.mean(tf * tf, axis=-1, keepdims=True) + EPS)


def rms_norm(t, w):
    return (_rms(t) * w.astype(jnp.float32)).astype(t.dtype)


def causal_depthwise_conv(u, w, bias):
    k = w.shape[0]
    s = u.shape[1]
    up = jnp.pad(u, ((0, 0), (k - 1, 0), (0, 0)))
    out = bias
    for i in range(k):
        out = out + w[i] * up[:, i:i + s]
    return out


def ssd_chunked_scan(xs, dt, a, bm, cm):
    b, s, h, p = xs.shape
    g, n = bm.shape[2], bm.shape[3]
    e = h // g
    nc = s // SSD_CHUNK
    L = SSD_CHUNK
    xdt = (xs.astype(jnp.float32) * dt[..., None]).reshape(b, nc, L, g, e, p)
    bc = bm.reshape(b, nc, L, g, n)
    cc = cm.reshape(b, nc, L, g, n)
    a_cs = jnp.cumsum((dt * a).reshape(b, nc, L, g, e), axis=2)
    seg = a_cs[:, :, :, None] - a_cs[:, :, None, :]
    tril = jnp.tril(jnp.ones((L, L), dtype=bool))[None, None, :, :, None, None]
    lmat = jnp.exp(jnp.where(tril, seg, -jnp.inf))
    cb = jnp.einsum('bclgn,bcsgn->bclsg', cc, bc)
    y_diag = jnp.einsum('bclsge,bcsgep->bclgep', cb[..., None] * lmat, xdt)
    decay_states = jnp.exp(a_cs[:, :, -1:] - a_cs)
    states = jnp.einsum('bclgn,bclge,bclgep->bcgepn', bc, decay_states, xdt)
    chunk_decay = jnp.exp(a_cs[:, :, -1])

    def step(h_prev, inp):
        st, dec = inp
        return h_prev * dec[..., None, None] + st, h_prev

    init = jnp.zeros_like(states[:, 0])
    _, prev_states = lax.scan(step, init, (jnp.moveaxis(states, 1, 0), jnp.moveaxis(chunk_decay, 1, 0)))
    prev_states = jnp.moveaxis(prev_states, 0, 1)
    y_off = jnp.einsum('bclgn,bcgepn,bclge->bclgep', cc, prev_states, jnp.exp(a_cs))
    return (y_diag + y_off).reshape(b, s, h, p)


def dilated_window_attention(q, k, v, window, dilation):
    b, s, h, d = q.shape
    nw = window // dilation
    blk = nw
    L = s // dilation
    nb = -(-L // blk)
    Lp = nb * blk

    def to_classes(t):
        t = t.reshape(b, L, dilation, h, d).transpose(0, 2, 3, 1, 4)
        return jnp.pad(t, ((0, 0), (0, 0), (0, 0), (0, Lp - L), (0, 0)))

    def with_prev(t):
        tp = jnp.pad(t, ((0, 0), (0, 0), (0, 0), (blk, 0), (0, 0)))
        prev = tp[:, :, :, :Lp].reshape(b, dilation, h, nb, blk, d)
        cur = t.reshape(b, dilation, h, nb, blk, d)
        return jnp.concatenate([prev, cur], axis=-2)

    qb = to_classes(q).reshape(b, dilation, h, nb, blk, d)
    kb = with_prev(to_classes(k))
    vb = with_prev(to_classes(v))
    scores = jnp.einsum('brhiqd,brhikd->brhiqk', qb, kb).astype(jnp.float32)
    qi = jnp.arange(blk)[:, None]
    ki = jnp.arange(2 * blk)[None, :]
    dist = blk + qi - ki
    band = (dist >= 0) & (dist <= nw)
    valid = (jnp.arange(nb)[:, None, None] > 0) | (ki[None] >= blk)
    mask = band[None] & valid
    scores = jnp.where(mask, scores, -jnp.inf)
    m = jnp.max(scores, axis=-1, keepdims=True)
    pr = jnp.exp(scores - m)
    denom = jnp.sum(pr, axis=-1, keepdims=True)
    o = jnp.einsum('brhiqk,brhikd->brhiqd', pr, vb.astype(jnp.float32)) / denom
    lse = (m + jnp.log(denom))[..., 0]
    o = o.reshape(b, dilation, h, Lp, d)[:, :, :, :L].transpose(0, 3, 1, 2, 4).reshape(b, s, h, d)
    lse = lse.reshape(b, dilation, h, Lp)[:, :, :, :L].transpose(0, 3, 1, 2).reshape(b, s, h)
    return o, lse


def hybrid_mixer(hn, w_in, conv_w, conv_b, dt_bias, a_log, d_skip, ssd_norm_w,
                 q_norm_w, k_norm_w, attn_norm_w, w_out):
    b, s, _ = hn.shape
    proj = hn @ w_in
    o1 = SSD_D_INNER
    o2 = o1 + CONV_CH
    o3 = o2 + SSD_HEADS
    o4 = o3 + ATT_D
    o5 = o4 + ATT_D
    z, xbc, dt_raw, q, k, v = jnp.split(proj, [o1, o2, o3, o4, o5], axis=-1)

    xbc = jax.nn.silu(causal_depthwise_conv(xbc, conv_w, conv_b))
    xs, bm, cm = jnp.split(xbc, [SSD_D_INNER, SSD_D_INNER + SSD_GROUPS * SSD_STATE], axis=-1)
    xs = xs.reshape(b, s, SSD_HEADS, HEAD_DIM)
    bm = bm.reshape(b, s, SSD_GROUPS, SSD_STATE)
    cm = cm.reshape(b, s, SSD_GROUPS, SSD_STATE)
    dt = jax.nn.softplus(dt_raw.astype(jnp.float32) + dt_bias.astype(jnp.float32))
    a = -jnp.exp(a_log.astype(jnp.float32))
    y = ssd_chunked_scan(xs, dt, a, bm, cm) + d_skip.astype(jnp.float32)[:, None] * xs
    y = y.reshape(b, s, SSD_D_INNER).astype(hn.dtype) * jax.nn.silu(z)
    y_ssd = (_rms(y.reshape(b, s, SSD_GROUPS, SSD_D_INNER // SSD_GROUPS)).reshape(b, s, SSD_D_INNER)
             * ssd_norm_w.astype(jnp.float32)).astype(hn.dtype)

    q = rms_norm(q.reshape(b, s, ATT_HEADS, HEAD_DIM), q_norm_w) * HEAD_DIM ** -0.5
    k = rms_norm(k.reshape(b, s, ATT_HEADS, HEAD_DIM), k_norm_w)
    v = v.reshape(b, s, ATT_HEADS, HEAD_DIM)
    branches = [dilated_window_attention(q, k, v, w, r) for (w, r) in DILATED_PATTERNS]
    outs = jnp.stack([br[0] for br in branches])
    lses = jnp.stack([br[1] for br in branches])
    alpha = jax.nn.softmax(lses, axis=0)
    o = jnp.sum(alpha[..., None] * outs, axis=0).reshape(b, s, ATT_D)
    y_att = rms_norm(o, attn_norm_w).astype(hn.dtype)

    return jnp.concatenate([y_ssd, y_att], axis=-1) @ w_out


def _fwd_setup_inputs(seed: int = 0) -> dict:
    key = jax.random.key(seed)
    ks = jax.random.split(key, 20)
    f32 = jnp.float32

    def nrm(k, shape, scale):
        return jax.random.normal(k, shape, f32) * scale

    dt0 = jnp.exp(jax.random.uniform(ks[9], (DEPTH, SSD_HEADS), f32, np.log(1e-3), np.log(1e-1)))
    dt_bias = dt0 + jnp.log(-jnp.expm1(-dt0))
    return {
        'x': nrm(ks[0], (BATCH, SEQ, D_MODEL), 1.0),
        'c': nrm(ks[1], (BATCH, D_MODEL), 1.0),
        'norm1_w': 1.0 + nrm(ks[2], (DEPTH, D_MODEL), 0.02),
        'norm2_w': 1.0 + nrm(ks[3], (DEPTH, D_MODEL), 0.02),
        'w_ada': nrm(ks[4], (DEPTH, D_MODEL, N_MOD * D_MODEL), D_MODEL ** -0.5),
        'b_ada': nrm(ks[5], (DEPTH, N_MOD * D_MODEL), 0.01),
        'w_in': nrm(ks[6], (DEPTH, D_MODEL, IN_PROJ_WIDTH), D_MODEL ** -0.5),
        'conv_w': nrm(ks[7], (DEPTH, SSD_CONV, CONV_CH), SSD_CONV ** -0.5),
        'conv_b': nrm(ks[8], (DEPTH, CONV_CH), 0.01),
        'dt_bias': dt_bias,
        'a_log': jnp.log(jax.random.uniform(ks[10], (DEPTH, SSD_HEADS), f32, 1.0, 16.0)),
        'd_skip': 1.0 + nrm(ks[11], (DEPTH, SSD_HEADS), 0.1),
        'ssd_norm_w': 1.0 + nrm(ks[12], (DEPTH, SSD_D_INNER), 0.02),
        'q_norm_w': 1.0 + nrm(ks[13], (DEPTH, HEAD_DIM), 0.02),
        'k_norm_w': 1.0 + nrm(ks[14], (DEPTH, HEAD_DIM), 0.02),
        'attn_norm_w': 1.0 + nrm(ks[15], (DEPTH, ATT_D), 0.02),
        'w_out': nrm(ks[16], (DEPTH, MIX_WIDTH, D_MODEL), MIX_WIDTH ** -0.5),
        'w_ff1': nrm(ks[17], (DEPTH, D_MODEL, D_FF), D_MODEL ** -0.5),
        'w_ff2': nrm(ks[18], (DEPTH, D_FF, D_MODEL), D_FF ** -0.5),
    }


def _fwd_reference(x, c, norm1_w, norm2_w, w_ada, b_ada, w_in, conv_w, conv_b, dt_bias, a_log,
              d_skip, ssd_norm_w, q_norm_w, k_norm_w, attn_norm_w, w_out, w_ff1, w_ff2):
    c_act = jax.nn.silu(c)
    for l in range(DEPTH):
        mod = c_act @ w_ada[l] + b_ada[l]
        shift1, scale1, gate1, shift2, scale2, gate2 = [t[:, None, :] for t in jnp.split(mod, N_MOD, axis=-1)]
        h1 = rms_norm(x, norm1_w[l]) * (1.0 + scale1) + shift1
        mix = hybrid_mixer(h1, w_in[l], conv_w[l], conv_b[l], dt_bias[l], a_log[l], d_skip[l],
                           ssd_norm_w[l], q_norm_w[l], k_norm_w[l], attn_norm_w[l], w_out[l])
        x = x + gate1 * mix
        h2 = rms_norm(x, norm2_w[l]) * (1.0 + scale2) + shift2
        ff = jnp.square(jax.nn.relu(h2 @ w_ff1[l])) @ w_ff2[l]
        x = x + gate2 * ff
    return x.astype(c.dtype)


import jax as _jax
import jax.numpy as _jnp

TWIN_FORMAT = 'train_step'
FWD_PARAMS = ['x', 'c', 'norm1_w', 'norm2_w', 'w_ada', 'b_ada', 'w_in', 'conv_w', 'conv_b', 'dt_bias', 'a_log', 'd_skip', 'ssd_norm_w', 'q_norm_w', 'k_norm_w', 'attn_norm_w', 'w_out', 'w_ff1', 'w_ff2']
TWIN_WEIGHTS = ['norm1_w', 'norm2_w', 'w_ada', 'b_ada', 'w_in', 'conv_w', 'conv_b', 'dt_bias', 'a_log', 'd_skip', 'ssd_norm_w', 'q_norm_w', 'k_norm_w', 'attn_norm_w', 'w_out', 'w_ff1', 'w_ff2']
TWIN_DIFF_INPUT = 'x'
TWIN_INPUTS = ['x', 'c', 'norm1_w', 'norm2_w', 'w_ada', 'b_ada', 'w_in', 'conv_w', 'conv_b', 'dt_bias', 'a_log', 'd_skip', 'ssd_norm_w', 'q_norm_w', 'k_norm_w', 'attn_norm_w', 'w_out', 'w_ff1', 'w_ff2', 'loss_target', 'm_norm1_w', 'm_norm2_w', 'm_w_ada', 'm_b_ada', 'm_w_in', 'm_conv_w', 'm_conv_b', 'm_dt_bias', 'm_a_log', 'm_d_skip', 'm_ssd_norm_w', 'm_q_norm_w', 'm_k_norm_w', 'm_attn_norm_w', 'm_w_out', 'm_w_ff1', 'm_w_ff2', 'v_norm1_w', 'v_norm2_w', 'v_w_ada', 'v_b_ada', 'v_w_in', 'v_conv_w', 'v_conv_b', 'v_dt_bias', 'v_a_log', 'v_d_skip', 'v_ssd_norm_w', 'v_q_norm_w', 'v_k_norm_w', 'v_attn_norm_w', 'v_w_out', 'v_w_ff1', 'v_w_ff2']
TWIN_OUTPUTS = ['loss', 'grad_x', 'grad_norm1_w', 'grad_norm2_w', 'grad_w_ada', 'grad_b_ada', 'grad_w_in', 'grad_conv_w', 'grad_conv_b', 'grad_dt_bias', 'grad_a_log', 'grad_d_skip', 'grad_ssd_norm_w', 'grad_q_norm_w', 'grad_k_norm_w', 'grad_attn_norm_w', 'grad_w_out', 'grad_w_ff1', 'grad_w_ff2', 'delta_norm1_w', 'delta_norm2_w', 'delta_w_ada', 'delta_b_ada', 'delta_w_in', 'delta_conv_w', 'delta_conv_b', 'delta_dt_bias', 'delta_a_log', 'delta_d_skip', 'delta_ssd_norm_w', 'delta_q_norm_w', 'delta_k_norm_w', 'delta_attn_norm_w', 'delta_w_out', 'delta_w_ff1', 'delta_w_ff2', 'new_m_norm1_w', 'new_m_norm2_w', 'new_m_w_ada', 'new_m_b_ada', 'new_m_w_in', 'new_m_conv_w', 'new_m_conv_b', 'new_m_dt_bias', 'new_m_a_log', 'new_m_d_skip', 'new_m_ssd_norm_w', 'new_m_q_norm_w', 'new_m_k_norm_w', 'new_m_attn_norm_w', 'new_m_w_out', 'new_m_w_ff1', 'new_m_w_ff2', 'new_v_norm1_w', 'new_v_norm2_w', 'new_v_w_ada', 'new_v_b_ada', 'new_v_w_in', 'new_v_conv_w', 'new_v_conv_b', 'new_v_dt_bias', 'new_v_a_log', 'new_v_d_skip', 'new_v_ssd_norm_w', 'new_v_q_norm_w', 'new_v_k_norm_w', 'new_v_attn_norm_w', 'new_v_w_out', 'new_v_w_ff1', 'new_v_w_ff2']
TWIN_LEAF_KINDS = {'loss': 'loss', 'grad_x': 'grad_x', 'grad_norm1_w': 'grad_w', 'grad_norm2_w': 'grad_w', 'grad_w_ada': 'grad_w', 'grad_b_ada': 'grad_w', 'grad_w_in': 'grad_w', 'grad_conv_w': 'grad_w', 'grad_conv_b': 'grad_w', 'grad_dt_bias': 'grad_w', 'grad_a_log': 'grad_w', 'grad_d_skip': 'grad_w', 'grad_ssd_norm_w': 'grad_w', 'grad_q_norm_w': 'grad_w', 'grad_k_norm_w': 'grad_w', 'grad_attn_norm_w': 'grad_w', 'grad_w_out': 'grad_w', 'grad_w_ff1': 'grad_w', 'grad_w_ff2': 'grad_w', 'delta_norm1_w': 'delta_w', 'delta_norm2_w': 'delta_w', 'delta_w_ada': 'delta_w', 'delta_b_ada': 'delta_w', 'delta_w_in': 'delta_w', 'delta_conv_w': 'delta_w', 'delta_conv_b': 'delta_w', 'delta_dt_bias': 'delta_w', 'delta_a_log': 'delta_w', 'delta_d_skip': 'delta_w', 'delta_ssd_norm_w': 'delta_w', 'delta_q_norm_w': 'delta_w', 'delta_k_norm_w': 'delta_w', 'delta_attn_norm_w': 'delta_w', 'delta_w_out': 'delta_w', 'delta_w_ff1': 'delta_w', 'delta_w_ff2': 'delta_w', 'new_m_norm1_w': 'new_m', 'new_m_norm2_w': 'new_m', 'new_m_w_ada': 'new_m', 'new_m_b_ada': 'new_m', 'new_m_w_in': 'new_m', 'new_m_conv_w': 'new_m', 'new_m_conv_b': 'new_m', 'new_m_dt_bias': 'new_m', 'new_m_a_log': 'new_m', 'new_m_d_skip': 'new_m', 'new_m_ssd_norm_w': 'new_m', 'new_m_q_norm_w': 'new_m', 'new_m_k_norm_w': 'new_m', 'new_m_attn_norm_w': 'new_m', 'new_m_w_out': 'new_m', 'new_m_w_ff1': 'new_m', 'new_m_w_ff2': 'new_m', 'new_v_norm1_w': 'new_v', 'new_v_norm2_w': 'new_v', 'new_v_w_ada': 'new_v', 'new_v_b_ada': 'new_v', 'new_v_w_in': 'new_v', 'new_v_conv_w': 'new_v', 'new_v_conv_b': 'new_v', 'new_v_dt_bias': 'new_v', 'new_v_a_log': 'new_v', 'new_v_d_skip': 'new_v', 'new_v_ssd_norm_w': 'new_v', 'new_v_q_norm_w': 'new_v', 'new_v_k_norm_w': 'new_v', 'new_v_attn_norm_w': 'new_v', 'new_v_w_out': 'new_v', 'new_v_w_ff1': 'new_v', 'new_v_w_ff2': 'new_v'}


def _forward(args):
    return _fwd_reference(*[args[k] for k in FWD_PARAMS])


def _output_shape():
    out = _jax.eval_shape(lambda: _forward(_fwd_setup_inputs(0)))
    return out.shape, out.dtype

N_MICROBATCH = 1
ADAM_LR = 0.001
ADAM_B1 = 0.9
ADAM_B2 = 0.999
ADAM_EPS = 1e-08
ADAM_WD = 0.01
ADAM_STEP = 10
PER_EXAMPLE_BATCH_AXIS = {'x': 0, 'c': 0, 'loss_target': 0}
SHARED_INPUTS = []
_WEIGHT_DTYPES = {'norm1_w': _jnp.float32, 'norm2_w': _jnp.float32, 'w_ada': _jnp.float32, 'b_ada': _jnp.float32, 'w_in': _jnp.float32, 'conv_w': _jnp.float32, 'conv_b': _jnp.float32, 'dt_bias': _jnp.float32, 'a_log': _jnp.float32, 'd_skip': _jnp.float32, 'ssd_norm_w': _jnp.float32, 'q_norm_w': _jnp.float32, 'k_norm_w': _jnp.float32, 'attn_norm_w': _jnp.float32, 'w_out': _jnp.float32, 'w_ff1': _jnp.float32, 'w_ff2': _jnp.float32}
MOMENT_SCALE = {'norm1_w': 1.248160e+00, 'norm2_w': 1.081619e+02, 'w_ada': 2.386558e+01, 'b_ada': 4.722295e+01, 'w_in': 8.163409e+00, 'conv_w': 4.744491e+00, 'conv_b': 6.210360e+00, 'dt_bias': 8.110247e+00, 'a_log': 2.373347e+01, 'd_skip': 1.222288e+01, 'ssd_norm_w': 1.243769e+01, 'q_norm_w': 1.709348e+00, 'k_norm_w': 1.731890e+00, 'attn_norm_w': 1.984687e+01, 'w_out': 1.998710e+01, 'w_ff1': 1.091679e+01, 'w_ff2': 2.035355e+01}


def _to_microbatches(a, axis):
    t = _jnp.moveaxis(a, axis, 0)
    t = t.reshape((N_MICROBATCH, t.shape[0] // N_MICROBATCH) + t.shape[1:])
    return _jnp.moveaxis(t, 1, axis + 1)


def setup_inputs(seed: int = 0) -> dict:
    inp = _fwd_setup_inputs(seed)
    key = _jax.random.fold_in(_jax.random.key(seed), 7919)
    shape, _ = _output_shape()
    out = dict(inp)
    out["loss_target"] = _jax.random.normal(_jax.random.fold_in(key, 0), shape, _jnp.float32)
    for i, name in enumerate(TWIN_WEIGHTS):
        w = inp[name].astype(_jnp.float32)
        if MOMENT_SCALE is None:
            s = _jnp.sqrt(_jnp.mean(_jnp.square(w)) + 1e-30)
        else:
            s = MOMENT_SCALE[name]
        km, kv = _jax.random.split(_jax.random.fold_in(key, i + 1))
        out[name] = w
        out["m_" + name] = s * _jax.random.normal(km, w.shape, _jnp.float32)
        out["v_" + name] = (s * s) * _jax.random.uniform(kv, w.shape, _jnp.float32, 0.5, 1.5)
    if N_MICROBATCH > 1:
        for name, axis in PER_EXAMPLE_BATCH_AXIS.items():
            out[name] = _to_microbatches(out[name], axis)
    return {'x': out['x'], 'c': out['c'], 'norm1_w': out['norm1_w'], 'norm2_w': out['norm2_w'], 'w_ada': out['w_ada'], 'b_ada': out['b_ada'], 'w_in': out['w_in'], 'conv_w': out['conv_w'], 'conv_b': out['conv_b'], 'dt_bias': out['dt_bias'], 'a_log': out['a_log'], 'd_skip': out['d_skip'], 'ssd_norm_w': out['ssd_norm_w'], 'q_norm_w': out['q_norm_w'], 'k_norm_w': out['k_norm_w'], 'attn_norm_w': out['attn_norm_w'], 'w_out': out['w_out'], 'w_ff1': out['w_ff1'], 'w_ff2': out['w_ff2'], 'loss_target': out['loss_target'], 'm_norm1_w': out['m_norm1_w'], 'm_norm2_w': out['m_norm2_w'], 'm_w_ada': out['m_w_ada'], 'm_b_ada': out['m_b_ada'], 'm_w_in': out['m_w_in'], 'm_conv_w': out['m_conv_w'], 'm_conv_b': out['m_conv_b'], 'm_dt_bias': out['m_dt_bias'], 'm_a_log': out['m_a_log'], 'm_d_skip': out['m_d_skip'], 'm_ssd_norm_w': out['m_ssd_norm_w'], 'm_q_norm_w': out['m_q_norm_w'], 'm_k_norm_w': out['m_k_norm_w'], 'm_attn_norm_w': out['m_attn_norm_w'], 'm_w_out': out['m_w_out'], 'm_w_ff1': out['m_w_ff1'], 'm_w_ff2': out['m_w_ff2'], 'v_norm1_w': out['v_norm1_w'], 'v_norm2_w': out['v_norm2_w'], 'v_w_ada': out['v_w_ada'], 'v_b_ada': out['v_b_ada'], 'v_w_in': out['v_w_in'], 'v_conv_w': out['v_conv_w'], 'v_conv_b': out['v_conv_b'], 'v_dt_bias': out['v_dt_bias'], 'v_a_log': out['v_a_log'], 'v_d_skip': out['v_d_skip'], 'v_ssd_norm_w': out['v_ssd_norm_w'], 'v_q_norm_w': out['v_q_norm_w'], 'v_k_norm_w': out['v_k_norm_w'], 'v_attn_norm_w': out['v_attn_norm_w'], 'v_w_out': out['v_w_out'], 'v_w_ff1': out['v_w_ff1'], 'v_w_ff2': out['v_w_ff2']}


def _loss(weights, diff, rest, loss_target):
    with _jax.named_scope("forward"):
        args = {**rest, TWIN_DIFF_INPUT: diff, **{k: w.astype(_WEIGHT_DTYPES[k]) for k, w in weights.items()}}
        y = _forward(args)
    with _jax.named_scope("loss_head"):
        err = _jnp.square(y.astype(_jnp.float32) - loss_target)
        return 0.5 * _jnp.sum(_jnp.mean(err, axis=-1)) if err.ndim else 0.5 * err


def _adamw(w, g, m, v):
    m = ADAM_B1 * m + (1.0 - ADAM_B1) * g
    v = ADAM_B2 * v + (1.0 - ADAM_B2) * _jnp.square(g)
    m_hat = m / (1.0 - ADAM_B1 ** ADAM_STEP)
    v_hat = v / (1.0 - ADAM_B2 ** ADAM_STEP)
    delta = -ADAM_LR * (m_hat / (_jnp.sqrt(v_hat) + ADAM_EPS) + ADAM_WD * w)
    return delta, m, v


def reference(x, c, norm1_w, norm2_w, w_ada, b_ada, w_in, conv_w, conv_b, dt_bias, a_log, d_skip, ssd_norm_w, q_norm_w, k_norm_w, attn_norm_w, w_out, w_ff1, w_ff2, loss_target, m_norm1_w, m_norm2_w, m_w_ada, m_b_ada, m_w_in, m_conv_w, m_conv_b, m_dt_bias, m_a_log, m_d_skip, m_ssd_norm_w, m_q_norm_w, m_k_norm_w, m_attn_norm_w, m_w_out, m_w_ff1, m_w_ff2, v_norm1_w, v_norm2_w, v_w_ada, v_b_ada, v_w_in, v_conv_w, v_conv_b, v_dt_bias, v_a_log, v_d_skip, v_ssd_norm_w, v_q_norm_w, v_k_norm_w, v_attn_norm_w, v_w_out, v_w_ff1, v_w_ff2):
    given = dict(x=x, c=c, norm1_w=norm1_w, norm2_w=norm2_w, w_ada=w_ada, b_ada=b_ada, w_in=w_in, conv_w=conv_w, conv_b=conv_b, dt_bias=dt_bias, a_log=a_log, d_skip=d_skip, ssd_norm_w=ssd_norm_w, q_norm_w=q_norm_w, k_norm_w=k_norm_w, attn_norm_w=attn_norm_w, w_out=w_out, w_ff1=w_ff1, w_ff2=w_ff2, loss_target=loss_target, m_norm1_w=m_norm1_w, m_norm2_w=m_norm2_w, m_w_ada=m_w_ada, m_b_ada=m_b_ada, m_w_in=m_w_in, m_conv_w=m_conv_w, m_conv_b=m_conv_b, m_dt_bias=m_dt_bias, m_a_log=m_a_log, m_d_skip=m_d_skip, m_ssd_norm_w=m_ssd_norm_w, m_q_norm_w=m_q_norm_w, m_k_norm_w=m_k_norm_w, m_attn_norm_w=m_attn_norm_w, m_w_out=m_w_out, m_w_ff1=m_w_ff1, m_w_ff2=m_w_ff2, v_norm1_w=v_norm1_w, v_norm2_w=v_norm2_w, v_w_ada=v_w_ada, v_b_ada=v_b_ada, v_w_in=v_w_in, v_conv_w=v_conv_w, v_conv_b=v_conv_b, v_dt_bias=v_dt_bias, v_a_log=v_a_log, v_d_skip=v_d_skip, v_ssd_norm_w=v_ssd_norm_w, v_q_norm_w=v_q_norm_w, v_k_norm_w=v_k_norm_w, v_attn_norm_w=v_attn_norm_w, v_w_out=v_w_out, v_w_ff1=v_w_ff1, v_w_ff2=v_w_ff2)
    weights = {n: given[n] for n in TWIN_WEIGHTS}
    shared = {n: given[n] for n in SHARED_INPUTS}
    per_example = {n: given[n] for n in ['x', 'c']}
    grad_fn = _jax.value_and_grad(_loss, argnums=(0, 1))

    def one_microbatch(ex, loss_target):
        ex = dict(ex)
        diff = ex.pop(TWIN_DIFF_INPUT)
        return grad_fn(weights, diff, {**shared, **ex}, loss_target)

    if N_MICROBATCH == 1:
        loss, (grad_w, grad_x) = one_microbatch(per_example, given["loss_target"])
    else:
        def body(carry, xs):
            loss_sum, grad_sum = carry
            l_k, (gw_k, gx_k) = one_microbatch(xs[0], xs[1])
            with _jax.named_scope("update"):
                return (loss_sum + l_k, _jax.tree.map(_jnp.add, grad_sum, gw_k)), gx_k

        init = (_jnp.zeros((), _jnp.float32), _jax.tree.map(_jnp.zeros_like, weights))
        (loss, grad_w), grad_x = _jax.lax.scan(body, init, (per_example, given["loss_target"]))
    with _jax.named_scope("update"):
        delta_w, new_m, new_v = {}, {}, {}
        for n in TWIN_WEIGHTS:
            delta_w[n], new_m[n], new_v[n] = _adamw(weights[n], grad_w[n], given["m_" + n], given["v_" + n])
    return (loss, grad_x, *[grad_w[n] for n in TWIN_WEIGHTS], *[delta_w[n] for n in TWIN_WEIGHTS],
            *[new_m[n] for n in TWIN_WEIGHTS], *[new_v[n] for n in TWIN_WEIGHTS])
```

```python
import functools

import jax
import jax.numpy as jnp
from jax import lax
from jax.experimental import pallas as pl
from jax.experimental.pallas import tpu as pltpu

F32, BF16 = jnp.float32, jnp.bfloat16
EPS = 1e-6
HD = 64
NH_SSD = 16
NG = 4
NSTATE = 128
KCONV = 4
CHUNK = 128
NH_ATT = 16
PATTERNS = ((128, 1), (512, 4), (2048, 16))
ABLK = 128
LANES = 128
ADAM_LR, ADAM_B1, ADAM_B2, ADAM_EPS, ADAM_WD, ADAM_STEP = 0.001, 0.9, 0.999, 1e-08, 0.01, 10
VMEM_LIMIT = 56 * 1024 * 1024
MESH = pl.DeviceIdType.MESH
NEG = -1e30

_DN = {"nn": (((1,), (0,)), ((), ())), "nt": (((1,), (1,)), ((), ())), "tn": (((0,), (0,)), ((), ()))}


def _cparams(sem):
    return pltpu.CompilerParams(dimension_semantics=sem, vmem_limit_bytes=VMEM_LIMIT)


def _tile(n, cap):
    if n % LANES or n <= LANES:
        return n
    best = LANES
    for t in range(LANES, min(n, cap) + 1, LANES):
        if n % t == 0:
            best = t
    return best


def _silu(x):
    return x / (1.0 + jnp.exp(-x))


def _softplus(x):
    return jnp.maximum(x, 0.0) + jnp.log(1.0 + jnp.exp(-jnp.abs(x)))


def _dot(a, b, dims):
    return lax.dot_general(a.astype(BF16), b.astype(BF16), _DN[dims], preferred_element_type=F32)


def _matmul(a, b, dims, out_dtype, name, a_fn=None, epilogue=None, extras=(), tm=512, tn=1024, tk=512):
    if dims == "nn":
        (M, K), (_, N) = a.shape, b.shape
    elif dims == "nt":
        (M, K), (N, _) = a.shape, b.shape
    else:
        (K, M), (_, N) = a.shape, b.shape
    tm, tn, tk = _tile(M, tm), _tile(N, tn), _tile(K, tk)
    nk = K // tk
    ne = len(extras)

    def body(a_ref, b_ref, *rest):
        e_refs, o_ref, acc = rest[:ne], rest[ne], rest[ne + 1]
        k = pl.program_id(2)

        @pl.when(k == 0)
        def _():
            acc[...] = jnp.zeros_like(acc)

        av = a_ref[...]
        if a_fn is not None:
            av = a_fn(av)
        acc[...] += _dot(av, b_ref[...], dims)

        @pl.when(k == nk - 1)
        def _():
            r = acc[...]
            if epilogue is not None:
                r = epilogue(r, *[e[...] for e in e_refs])
            o_ref[...] = r.astype(out_dtype)

    a_spec = pl.BlockSpec((tk, tm), lambda i, j, k: (k, i)) if dims == "tn" else pl.BlockSpec((tm, tk), lambda i, j, k: (i, k))
    b_spec = pl.BlockSpec((tn, tk), lambda i, j, k: (j, k)) if dims == "nt" else pl.BlockSpec((tk, tn), lambda i, j, k: (k, j))
    o_spec = pl.BlockSpec((tm, tn), lambda i, j, k: (i, j))
    return pl.pallas_call(
        body, name=name, grid=(M // tm, N // tn, nk),
        in_specs=[a_spec, b_spec] + [o_spec] * ne, out_specs=o_spec,
        out_shape=jax.ShapeDtypeStruct((M, N), out_dtype),
        scratch_shapes=[pltpu.VMEM((tm, tn), F32)],
        compiler_params=_cparams(("parallel", "parallel", "arbitrary")),
    )(a, b, *extras)


def _rows(name, fn, rows, consts, outs, accs, n_rows, tm=256):
    tm = min(tm, n_rows)
    nr, nc, no, na = len(rows), len(consts), len(outs), len(accs)

    def body(*refs):
        r_refs, c_refs = refs[:nr], refs[nr:nr + nc]
        o_refs, a_refs = refs[nr + nc:nr + nc + no], refs[nr + nc + no:]
        o_vals, a_vals = fn([r[...] for r in r_refs], [c[...] for c in c_refs])
        for ref, val in zip(o_refs, o_vals):
            ref[...] = val.astype(ref.dtype)
        if na:
            @pl.when(pl.program_id(0) == 0)
            def _():
                for ref in a_refs:
                    ref[...] = jnp.zeros_like(ref)
            for ref, val in zip(a_refs, a_vals):
                ref[...] += val

    in_specs = [pl.BlockSpec((tm, w), lambda i, cb=cb: (i, cb)) for (_, cb, w) in rows]
    in_specs += [pl.BlockSpec(cst.shape, lambda i, nd=cst.ndim: (0,) * nd) for cst in consts]
    out_specs = [pl.BlockSpec((tm, w), lambda i: (i, 0)) for (w, _) in outs]
    out_specs += [pl.BlockSpec(s, lambda i: (0, 0)) for s in accs]
    out_shape = [jax.ShapeDtypeStruct((n_rows, w), dt) for (w, dt) in outs]
    out_shape += [jax.ShapeDtypeStruct(s, F32) for s in accs]
    res = pl.pallas_call(
        body, name=name, grid=(n_rows // tm,), in_specs=in_specs, out_specs=out_specs, out_shape=out_shape,
        compiler_params=_cparams(("arbitrary",)),
    )(*[r[0] for r in rows], *consts)
    return res


def _normmod(x, nw, sc, sh):
    r = lax.rsqrt(jnp.mean(x * x, axis=-1, keepdims=True) + EPS)
    return (x * r) * nw * (1.0 + sc) + sh


def _resid_normmod(x, mix, g, nw, sc, sh):
    x2 = x + g * mix
    return x2, _normmod(x2, nw, sc, sh)


def _rmsw(o, w):
    return o * lax.rsqrt(jnp.mean(o * o, axis=-1, keepdims=True) + EPS) * w


def _lane_mask():
    return lax.broadcasted_iota(jnp.int32, (1, LANES), 1) < HD


def _headnorm(t, w, scale):
    lo = _lane_mask()
    t2 = t * t
    s0 = jnp.sum(jnp.where(lo, t2, 0.0), axis=1, keepdims=True)
    s1 = jnp.sum(jnp.where(lo, 0.0, t2), axis=1, keepdims=True)
    ms = jnp.where(lo, s0, s1) * (1.0 / HD)
    return t * lax.rsqrt(ms + EPS) * w * scale


def _conv_cols(n_ch):
    return _tile(n_ch, 256)


def _conv_fwd(proj, col0, n_ch, conv_w, conv_b, name):
    S = proj.shape[0]
    tc = _conv_cols(n_ch)

    def body(u_ref, w_ref, b_ref, o_ref):
        u = u_ref[...]
        row = lax.broadcasted_iota(jnp.int32, u.shape, 0)
        acc = b_ref[...] + w_ref[KCONV - 1:KCONV, :] * u
        for i in range(KCONV - 1):
            sh = KCONV - 1 - i
            acc = acc + w_ref[i:i + 1, :] * jnp.where(row >= sh, pltpu.roll(u, sh, 0), 0.0)
        o_ref[...] = _silu(acc)

    return pl.pallas_call(
        body, name=name, grid=(n_ch // tc,),
        in_specs=[pl.BlockSpec((S, tc), lambda j: (0, j + col0 // tc)),
                  pl.BlockSpec((KCONV, tc), lambda j: (0, j)), pl.BlockSpec((1, tc), lambda j: (0, j))],
        out_specs=pl.BlockSpec((S, tc), lambda j: (0, j)),
        out_shape=jax.ShapeDtypeStruct((S, n_ch), F32),
        compiler_params=_cparams(("parallel",)),
    )(proj, conv_w, conv_b)


def _conv_bwd(proj, col0, n_ch, conv_w, conv_b, dxbc, name):
    S = proj.shape[0]
    tc = _conv_cols(n_ch)

    def body(u_ref, w_ref, b_ref, g_ref, du_ref, dw_ref, db_ref):
        u = u_ref[...]
        row = lax.broadcasted_iota(jnp.int32, u.shape, 0)
        shifted = [jnp.where(row >= s, pltpu.roll(u, s, 0), 0.0) for s in range(1, KCONV)]
        acc = b_ref[...] + w_ref[KCONV - 1:KCONV, :] * u
        for i in range(KCONV - 1):
            acc = acc + w_ref[i:i + 1, :] * shifted[KCONV - 2 - i]
        sig = 1.0 / (1.0 + jnp.exp(-acc))
        dacc = g_ref[...] * (sig * (1.0 + acc * (1.0 - sig)))
        db_ref[...] = jnp.sum(dacc, axis=0, keepdims=True)
        du = w_ref[KCONV - 1:KCONV, :] * dacc
        dw_ref[KCONV - 1:KCONV, :] = jnp.sum(dacc * u, axis=0, keepdims=True)
        for i in range(KCONV - 1):
            sh = KCONV - 1 - i
            dw_ref[i:i + 1, :] = jnp.sum(dacc * shifted[sh - 1], axis=0, keepdims=True)
            du = du + w_ref[i:i + 1, :] * jnp.where(row < S - sh, pltpu.roll(dacc, S - sh, 0), 0.0)
        du_ref[...] = du

    return pl.pallas_call(
        body, name=name, grid=(n_ch // tc,),
        in_specs=[pl.BlockSpec((S, tc), lambda j: (0, j + col0 // tc)),
                  pl.BlockSpec((KCONV, tc), lambda j: (0, j)), pl.BlockSpec((1, tc), lambda j: (0, j)),
                  pl.BlockSpec((S, tc), lambda j: (0, j))],
        out_specs=[pl.BlockSpec((S, tc), lambda j: (0, j)), pl.BlockSpec((KCONV, tc), lambda j: (0, j)),
                   pl.BlockSpec((1, tc), lambda j: (0, j))],
        out_shape=[jax.ShapeDtypeStruct((S, n_ch), F32), jax.ShapeDtypeStruct((KCONV, n_ch), F32),
                   jax.ShapeDtypeStruct((1, n_ch), F32)],
        compiler_params=_cparams(("parallel",)),
    )(proj, conv_w, conv_b, dxbc)


@functools.partial(jax.custom_vjp, nondiff_argnums=(2,))
def _mm(a, b, dims):
    return _dot(a, b, dims)


def _mm_fwd(a, b, dims):
    return _dot(a, b, dims), (a, b)


def _mm_bwd(dims, res, g):
    a, b = res
    if dims == "nn":
        return _dot(g, b, "nt"), _dot(a, g, "tn")
    if dims == "nt":
        return _dot(g, b, "nn"), _dot(g, a, "tn")
    return _dot(b, g, "nt"), _dot(a, g, "nn")


_mm.defvjp(_mm_fwd, _mm_bwd)


def _tri_dot(x, upper):
    n = x.shape[0]
    r = lax.broadcasted_iota(jnp.int32, (n, n), 0)
    c = lax.broadcasted_iota(jnp.int32, (n, n), 1)
    t = jnp.where((r <= c) if upper else (r >= c), 1.0, 0.0)
    return lax.dot_general(t, x, _DN["nn"], precision=lax.Precision.HIGHEST, preferred_element_type=F32)


@jax.custom_vjp
def _cumsum_rows(x):
    return _tri_dot(x, False)


_cumsum_rows.defvjp(lambda x: (_tri_dot(x, False), None), lambda _, g: (_tri_dot(g, True),))


def _ssd_chunk(xs_p, bm_g, cm_g, dtr, z_p, dtb, alog, dsk, nw_p, h_p):
    L = dtr.shape[0]
    n_pairs = len(xs_p)
    ppg = n_pairs // len(bm_g)
    lane = lax.broadcasted_iota(jnp.int32, (1, LANES), 1)
    sub = lax.broadcasted_iota(jnp.int32, (LANES, 1), 0)
    lo = lane < HD
    row_l = lax.broadcasted_iota(jnp.int32, (L, 1), 0)
    tri = lax.broadcasted_iota(jnp.int32, (L, L), 0) >= lax.broadcasted_iota(jnp.int32, (L, L), 1)

    dt = _softplus(dtr + dtb)
    acs = _cumsum_rows(dt * (-jnp.exp(alog)))
    acs_t = acs.T
    a_last = jnp.sum(jnp.where(row_l == L - 1, acs, 0.0), axis=0, keepdims=True)
    e_acs = jnp.exp(acs)
    dec = jnp.exp(a_last - acs)
    cdec = jnp.exp(a_last)

    def colv(m, h):
        return jnp.sum(jnp.where(lane == h, m, 0.0), axis=1, keepdims=True)

    def rowv(mt, h):
        return jnp.sum(jnp.where(sub == h, mt, 0.0), axis=0, keepdims=True)

    def pair(m, h0):
        return jnp.where(lo, colv(m, h0), colv(m, h0 + 1))

    ys, hs = [], []
    cb = None
    for p in range(n_pairs):
        g, h0 = p // ppg, 2 * p
        bmat, cmat = bm_g[g], cm_g[g]
        if p % ppg == 0:
            cb = _mm(cmat, bmat, "nt")
        x = xs_p[p]
        xdt = x * pair(dt, h0)
        yd = []
        for h in (h0, h0 + 1):
            seg = colv(acs, h) - rowv(acs_t, h)
            lm = jnp.where(tri, jnp.exp(jnp.where(tri, seg, 0.0)), 0.0)
            yd.append(_mm(cb * lm, xdt, "nn"))
        y = jnp.where(lo, yd[0], yd[1])
        y = y + _mm(cmat, h_p[p], "nt") * pair(e_acs, h0)
        st = _mm(xdt * pair(dec, h0), bmat, "tn")
        cd_col = jnp.where(sub < HD, colv(cdec, h0), colv(cdec, h0 + 1))
        hs.append(h_p[p] * cd_col + st)
        ys.append(y + pair(dsk, h0) * x)

    y2 = [ys[p] * _silu(z_p[p]) for p in range(n_pairs)]
    outs = []
    for g in range(len(bm_g)):
        ps = range(g * ppg, (g + 1) * ppg)
        ss = sum(jnp.sum(y2[p] * y2[p], axis=1, keepdims=True) for p in ps)
        rs = lax.rsqrt(ss * (1.0 / (ppg * LANES)) + EPS)
        outs += [y2[p] * rs * nw_p[p] for p in ps]
    return outs, hs


def _ssd_slices(xbc_ref, z_ref, nw_ref, di):
    n_pairs = di // LANES
    xs_p = [xbc_ref[:, p * LANES:(p + 1) * LANES] for p in range(n_pairs)]
    bm_g = [xbc_ref[:, di + g * NSTATE:di + (g + 1) * NSTATE] for g in range(NG)]
    cm_g = [xbc_ref[:, di + (NG + g) * NSTATE:di + (NG + g + 1) * NSTATE] for g in range(NG)]
    z_p = [z_ref[:, p * LANES:(p + 1) * LANES] for p in range(n_pairs)]
    nw_p = [nw_ref[:, p * LANES:(p + 1) * LANES] for p in range(n_pairs)]
    return xs_p, bm_g, cm_g, z_p, nw_p


def _ssd_fwd(xbc, proj, dt_cb, dtb, alog, dsk, nw, name):
    S, cc = xbc.shape
    di = NH_SSD * HD
    n_pairs = di // LANES
    nchunk = S // CHUNK

    def body(xbc_ref, z_ref, dtr_ref, dtb_ref, alog_ref, dsk_ref, nw_ref, y_ref, hs_ref, h_scr):
        @pl.when(pl.program_id(0) == 0)
        def _():
            h_scr[...] = jnp.zeros_like(h_scr)

        xs_p, bm_g, cm_g, z_p, nw_p = _ssd_slices(xbc_ref, z_ref, nw_ref, di)
        h_p = [h_scr[p * LANES:(p + 1) * LANES, :] for p in range(n_pairs)]
        hs_ref[...] = h_scr[...]
        outs, hs = _ssd_chunk(xs_p, bm_g, cm_g, dtr_ref[...], z_p, dtb_ref[...], alog_ref[...], dsk_ref[...], nw_p, h_p)
        for p in range(n_pairs):
            y_ref[:, p * LANES:(p + 1) * LANES] = outs[p].astype(y_ref.dtype)
            h_scr[p * LANES:(p + 1) * LANES, :] = hs[p]

    vec = pl.BlockSpec((1, LANES), lambda c: (0, 0))
    return pl.pallas_call(
        body, name=name, grid=(nchunk,),
        in_specs=[pl.BlockSpec((CHUNK, cc), lambda c: (c, 0)), pl.BlockSpec((CHUNK, di), lambda c: (c, 0)),
                  pl.BlockSpec((CHUNK, LANES), lambda c: (c, dt_cb)), vec, vec, vec,
                  pl.BlockSpec((1, di), lambda c: (0, 0))],
        out_specs=[pl.BlockSpec((CHUNK, di), lambda c: (c, 0)), pl.BlockSpec((None, di, NSTATE), lambda c: (c, 0, 0))],
        out_shape=[jax.ShapeDtypeStruct((S, di), BF16), jax.ShapeDtypeStruct((nchunk, di, NSTATE), F32)],
        scratch_shapes=[pltpu.VMEM((di, NSTATE), F32)],
        compiler_params=_cparams(("arbitrary",)),
    )(xbc, proj, proj, dtb, alog, dsk, nw)


def _ssd_bwd(xbc, proj, dt_cb, dtb, alog, dsk, nw, hsave, dy, name):
    S, cc = xbc.shape
    di = NH_SSD * HD
    n_pairs = di // LANES
    nchunk = S // CHUNK

    def body(xbc_ref, z_ref, dtr_ref, dtb_ref, alog_ref, dsk_ref, nw_ref, hs_ref, dy_ref,
             dxbc_ref, dz_ref, ddtr_ref, ddtb_ref, dalog_ref, ddsk_ref, dnw_ref, dh_scr):
        @pl.when(pl.program_id(0) == 0)
        def _():
            dh_scr[...] = jnp.zeros_like(dh_scr)
            ddtb_ref[...] = jnp.zeros_like(ddtb_ref)
            dalog_ref[...] = jnp.zeros_like(dalog_ref)
            ddsk_ref[...] = jnp.zeros_like(ddsk_ref)
            dnw_ref[...] = jnp.zeros_like(dnw_ref)

        xs_p, bm_g, cm_g, z_p, nw_p = _ssd_slices(xbc_ref, z_ref, nw_ref, di)
        h_p = [hs_ref[p * LANES:(p + 1) * LANES, :] for p in range(n_pairs)]
        dy_p = [dy_ref[:, p * LANES:(p + 1) * LANES].astype(F32) for p in range(n_pairs)]
        dh_p = [dh_scr[p * LANES:(p + 1) * LANES, :] for p in range(n_pairs)]
        _, vjp = jax.vjp(_ssd_chunk, xs_p, bm_g, cm_g, dtr_ref[...], z_p, dtb_ref[...], alog_ref[...], dsk_ref[...],
                         nw_p, h_p)
        dxs, dbm, dcm, ddtr, dz, ddtb, dalog, ddsk, dnw, dh = vjp((dy_p, dh_p))
        for p in range(n_pairs):
            sl = slice(p * LANES, (p + 1) * LANES)
            dxbc_ref[:, sl] = dxs[p]
            dz_ref[:, sl] = dz[p]
            dnw_ref[:, sl] += dnw[p]
            dh_scr[sl, :] = dh[p]
        for g in range(NG):
            dxbc_ref[:, di + g * NSTATE:di + (g + 1) * NSTATE] = dbm[g]
            dxbc_ref[:, di + (NG + g) * NSTATE:di + (NG + g + 1) * NSTATE] = dcm[g]
        ddtr_ref[...] = ddtr
        ddtb_ref[...] += ddtb
        dalog_ref[...] += dalog
        ddsk_ref[...] += ddsk

    last = nchunk - 1
    vec = pl.BlockSpec((1, LANES), lambda c: (0, 0))
    return pl.pallas_call(
        body, name=name, grid=(nchunk,),
        in_specs=[pl.BlockSpec((CHUNK, cc), lambda c: (last - c, 0)), pl.BlockSpec((CHUNK, di), lambda c: (last - c, 0)),
                  pl.BlockSpec((CHUNK, LANES), lambda c: (last - c, dt_cb)), vec, vec, vec,
                  pl.BlockSpec((1, di), lambda c: (0, 0)),
                  pl.BlockSpec((None, di, NSTATE), lambda c: (last - c, 0, 0)),
                  pl.BlockSpec((CHUNK, di), lambda c: (last - c, 0))],
        out_specs=[pl.BlockSpec((CHUNK, cc), lambda c: (last - c, 0)), pl.BlockSpec((CHUNK, di), lambda c: (last - c, 0)),
                   pl.BlockSpec((CHUNK, LANES), lambda c: (last - c, 0)), vec, vec, vec,
                   pl.BlockSpec((1, di), lambda c: (0, 0))],
        out_shape=[jax.ShapeDtypeStruct((S, cc), F32), jax.ShapeDtypeStruct((S, di), F32),
                   jax.ShapeDtypeStruct((S, LANES), F32), jax.ShapeDtypeStruct((1, LANES), F32),
                   jax.ShapeDtypeStruct((1, LANES), F32), jax.ShapeDtypeStruct((1, LANES), F32),
                   jax.ShapeDtypeStruct((1, di), F32)],
        scratch_shapes=[pltpu.VMEM((di, NSTATE), F32)],
        compiler_params=_cparams(("arbitrary",)),
    )(xbc, proj, proj, dtb, alog, dsk, nw, hsave, dy)


def _band_masks(rows_q, rows_k):
    qi = lax.broadcasted_iota(jnp.int32, (rows_q, rows_k), 0)
    ki = lax.broadcasted_iota(jnp.int32, (rows_q, rows_k), 1)
    return qi, ki


def _attn_fwd(q, k, v, nbc, name):
    S, ad = q.shape
    nb = S // ABLK

    def body(q_ref, kp_ref, kc_ref, vp_ref, vc_ref, o_ref, lse_ref):
        j = pl.program_id(1)
        has_prev = (j % nbc) != 0
        lo = _lane_mask()
        qv = q_ref[...]
        kk = jnp.concatenate([kp_ref[...], kc_ref[...]], axis=0)
        vv = jnp.concatenate([vp_ref[...], vc_ref[...]], axis=0)
        qi, ki = _band_masks(ABLK, 2 * ABLK)
        valid = (ki <= qi + ABLK) & ((ki >= ABLK) | ((ki >= qi) & has_prev))
        os_, ls_ = [], []
        for msk in (lo, jnp.logical_not(lo)):
            s = _dot(jnp.where(msk, qv, jnp.zeros_like(qv)), kk, "nt")
            s = jnp.where(valid, s, NEG)
            m = jnp.max(s, axis=1, keepdims=True)
            p = jnp.exp(s - m)
            l = jnp.sum(p, axis=1, keepdims=True)
            os_.append(_dot(p, vv, "nn") / l)
            ls_.append(m + jnp.log(l))
        o_ref[...] = jnp.where(lo, os_[0], os_[1])
        lse_ref[...] = jnp.where(lo, ls_[0], ls_[1])

    cur = pl.BlockSpec((ABLK, LANES), lambda h, j: (j, h))
    prev = pl.BlockSpec((ABLK, LANES), lambda h, j: (jnp.maximum(j - 1, 0), h))
    return pl.pallas_call(
        body, name=name, grid=(ad // LANES, nb),
        in_specs=[cur, prev, cur, prev, cur], out_specs=[cur, cur],
        out_shape=[jax.ShapeDtypeStruct((S, ad), F32), jax.ShapeDtypeStruct((S, ad), F32)],
        compiler_params=_cparams(("parallel", "parallel")),
    )(q, k, k, v, v)


def _head_col(t, msk):
    return jnp.max(jnp.where(msk, t, NEG), axis=1, keepdims=True)


def _attn_dq(q, k, v, do, lse, dd, nbc, name):
    S, ad = q.shape
    nb = S // ABLK

    def body(q_ref, kp_ref, kc_ref, vp_ref, vc_ref, do_ref, lse_ref, dd_ref, dq_ref):
        j = pl.program_id(1)
        has_prev = (j % nbc) != 0
        lo = _lane_mask()
        qv, dov = q_ref[...], do_ref[...]
        kk = jnp.concatenate([kp_ref[...], kc_ref[...]], axis=0)
        vv = jnp.concatenate([vp_ref[...], vc_ref[...]], axis=0)
        qi, ki = _band_masks(ABLK, 2 * ABLK)
        valid = (ki <= qi + ABLK) & ((ki >= ABLK) | ((ki >= qi) & has_prev))
        dqs = []
        for msk in (lo, jnp.logical_not(lo)):
            s = _dot(jnp.where(msk, qv, jnp.zeros_like(qv)), kk, "nt")
            p = jnp.where(valid, jnp.exp(jnp.where(valid, s, NEG) - _head_col(lse_ref[...], msk)), 0.0)
            dp = _dot(jnp.where(msk, dov, jnp.zeros_like(dov)), vv, "nt")
            ds = p * (dp - _head_col(dd_ref[...], msk))
            dqs.append(_dot(ds, kk, "nn"))
        dq_ref[...] = jnp.where(lo, dqs[0], dqs[1])

    cur = pl.BlockSpec((ABLK, LANES), lambda h, j: (j, h))
    prev = pl.BlockSpec((ABLK, LANES), lambda h, j: (jnp.maximum(j - 1, 0), h))
    return pl.pallas_call(
        body, name=name, grid=(ad // LANES, nb),
        in_specs=[cur, prev, cur, prev, cur, cur, cur, cur], out_specs=cur,
        out_shape=jax.ShapeDtypeStruct((S, ad), F32),
        compiler_params=_cparams(("parallel", "parallel")),
    )(q, k, k, v, v, do, lse, dd)


def _attn_dkv(q, k, v, do, lse, dd, nbc, name):
    S, ad = q.shape
    nb = S // ABLK

    def body(k_ref, v_ref, qc_ref, qn_ref, doc_ref, don_ref, lc_ref, ln_ref, dc_ref, dn_ref, dk_ref, dv_ref):
        j = pl.program_id(1)
        has_next = ((j + 1) % nbc) != 0
        lo = _lane_mask()
        kv, vv = k_ref[...], v_ref[...]
        qq = jnp.concatenate([qc_ref[...], qn_ref[...]], axis=0)
        dd_o = jnp.concatenate([doc_ref[...], don_ref[...]], axis=0)
        lse2 = jnp.concatenate([lc_ref[...], ln_ref[...]], axis=0)
        d2 = jnp.concatenate([dc_ref[...], dn_ref[...]], axis=0)
        qi, ki = _band_masks(2 * ABLK, ABLK)
        valid = ((qi < ABLK) & (ki <= qi)) | ((qi >= ABLK) & (ki >= qi - ABLK) & has_next)
        dk = jnp.zeros((ABLK, LANES), F32)
        dv = jnp.zeros((ABLK, LANES), F32)
        for msk in (lo, jnp.logical_not(lo)):
            qh = jnp.where(msk, qq, jnp.zeros_like(qq))
            doh = jnp.where(msk, dd_o, jnp.zeros_like(dd_o))
            s = _dot(qh, kv, "nt")
            p = jnp.where(valid, jnp.exp(jnp.where(valid, s, NEG) - _head_col(lse2, msk)), 0.0)
            dv = dv + _dot(p, doh, "tn")
            dp = _dot(doh, vv, "nt")
            ds = p * (dp - _head_col(d2, msk))
            dk = dk + _dot(ds, qh, "tn")
        dk_ref[...] = dk
        dv_ref[...] = dv

    cur = pl.BlockSpec((ABLK, LANES), lambda h, j: (j, h))
    nxt = pl.BlockSpec((ABLK, LANES), lambda h, j: (jnp.minimum(j + 1, nb - 1), h))
    return pl.pallas_call(
        body, name=name, grid=(ad // LANES, nb),
        in_specs=[cur, cur, cur, nxt, cur, nxt, cur, nxt, cur, nxt], out_specs=[cur, cur],
        out_shape=[jax.ShapeDtypeStruct((S, ad), F32), jax.ShapeDtypeStruct((S, ad), F32)],
        compiler_params=_cparams(("parallel", "parallel")),
    )(k, v, q, q, do, do, lse, lse, dd, dd)


def _to_classes(t, d):
    if d == 1:
        return t
    s, w = t.shape
    return t.reshape(s // d, d, w).transpose(1, 0, 2).reshape(s, w)


def _from_classes(t, d):
    if d == 1:
        return t
    s, w = t.shape
    return t.reshape(d, s // d, w).transpose(1, 0, 2).reshape(s, w)


def _coords():
    return lax.axis_index("x"), lax.axis_index("y"), lax.axis_index("c")


def _exchange8(xs, per_dest, name):
    n = len(xs)
    blk = [x.shape[1:] if per_dest else x.shape for x in xs]

    def body(*refs):
        ins, outs = refs[:n], refs[n:2 * n]
        send_sems, recv_sems, local_sems = refs[2 * n:]
        x, y, c = _coords()
        sibling = (x, y, 1 - c)
        chips = [(1 - x, y), (x, 1 - y), (1 - x, 1 - y)]
        first, passed, mine = [], [], []
        for a in range(n):
            def src_for(cx, cy, a=a):
                return ins[a].at[2 * cx + cy] if per_dest else ins[a]

            def slot(px, py, pc, a=a):
                return outs[a].at[4 * px + 2 * py + pc]

            def copy(k, src, dst, to, a=a):
                return pltpu.make_async_remote_copy(src_ref=src, dst_ref=dst, send_sem=send_sems.at[7 * a + k],
                                                    recv_sem=recv_sems.at[7 * a + k], device_id=to, device_id_type=MESH)

            m = pltpu.make_async_copy(src_for(x, y), slot(x, y, c), local_sems.at[a])
            m.start()
            mine.append(m)
            cps = [copy(0, src_for(x, y), slot(x, y, c), sibling)]
            cps += [copy(1 + j, src_for(*chip), slot(x, y, c), (*chip, c)) for j, chip in enumerate(chips)]
            for cp in cps:
                cp.start()
            first += cps
        for a in range(n):
            def slot(px, py, pc, a=a):
                return outs[a].at[4 * px + 2 * py + pc]

            def copy(k, src, dst, to, a=a):
                return pltpu.make_async_remote_copy(src_ref=src, dst_ref=dst, send_sem=send_sems.at[7 * a + k],
                                                    recv_sem=recv_sems.at[7 * a + k], device_id=to, device_id_type=MESH)

            for j, chip in enumerate(chips):
                copy(1 + j, slot(*chip, c), slot(*chip, c), (*chip, c)).wait_recv()
                fw = copy(4 + j, slot(*chip, c), slot(*chip, c), sibling)
                fw.start()
                passed.append(fw)
        for a in range(n):
            def slot(px, py, pc, a=a):
                return outs[a].at[4 * px + 2 * py + pc]

            def copy(k, src, dst, to, a=a):
                return pltpu.make_async_remote_copy(src_ref=src, dst_ref=dst, send_sem=send_sems.at[7 * a + k],
                                                    recv_sem=recv_sems.at[7 * a + k], device_id=to, device_id_type=MESH)

            copy(0, slot(x, y, 1 - c), slot(x, y, 1 - c), sibling).wait_recv()
            for j, chip in enumerate(chips):
                copy(4 + j, slot(*chip, 1 - c), slot(*chip, 1 - c), sibling).wait_recv()
        for cp in first + passed:
            cp.wait_send()
        for m in mine:
            m.wait()

    anyspec = pl.BlockSpec(memory_space=pl.ANY)
    res = pl.pallas_call(
        body, name=name, in_specs=[anyspec] * n, out_specs=[anyspec] * n,
        out_shape=[jax.ShapeDtypeStruct((8,) + tuple(b), x.dtype) for b, x in zip(blk, xs)],
        scratch_shapes=[pltpu.SemaphoreType.DMA((7 * n,)), pltpu.SemaphoreType.DMA((7 * n,)),
                        pltpu.SemaphoreType.DMA((n,))],
    )(*xs)
    return list(res)


def _pair_swap(xs, name):
    n = len(xs)

    def body(*refs):
        ins, outs = refs[:n], refs[n:2 * n]
        send_sems, recv_sems = refs[2 * n:]
        x, y, c = _coords()
        cps = [pltpu.make_async_remote_copy(src_ref=ins[a].at[1 - c], dst_ref=outs[a], send_sem=send_sems.at[a],
                                            recv_sem=recv_sems.at[a], device_id=(x, y, 1 - c), device_id_type=MESH)
               for a in range(n)]
        for cp in cps:
            cp.start()
        for cp in cps:
            cp.wait()

    anyspec = pl.BlockSpec(memory_space=pl.ANY)
    res = pl.pallas_call(
        body, name=name, in_specs=[anyspec] * n, out_specs=[anyspec] * n,
        out_shape=[jax.ShapeDtypeStruct(x.shape[1:], x.dtype) for x in xs],
        scratch_shapes=[pltpu.SemaphoreType.DMA((n,)), pltpu.SemaphoreType.DMA((n,))],
    )(*xs)
    return list(res)


def _adamw_math(w, g, m, v):
    m = ADAM_B1 * m + (1.0 - ADAM_B1) * g
    v = ADAM_B2 * v + (1.0 - ADAM_B2) * (g * g)
    m_hat = m / (1.0 - ADAM_B1 ** ADAM_STEP)
    v_hat = v / (1.0 - ADAM_B2 ** ADAM_STEP)
    delta = -ADAM_LR * (m_hat / (jnp.sqrt(v_hat) + ADAM_EPS) + ADAM_WD * w)
    return delta, m, v


def _adamw(parts, w, m, v, name, tm=128):
    npart, R, C = parts.shape
    tm = min(tm, R)

    def body(p_ref, w_ref, m_ref, v_ref, g_out, d_out, m_out, v_out):
        g = p_ref[0]
        for i in range(1, npart):
            g = g + p_ref[i]
        d, mm, vv = _adamw_math(w_ref[...], g, m_ref[...], v_ref[...])
        g_out[...] = g
        d_out[...] = d
        m_out[...] = mm
        v_out[...] = vv

    spec = pl.BlockSpec((tm, C), lambda i: (i, 0))
    return pl.pallas_call(
        body, name=name, grid=(R // tm,),
        in_specs=[pl.BlockSpec((npart, tm, C), lambda i: (0, i, 0)), spec, spec, spec], out_specs=[spec] * 4,
        out_shape=[jax.ShapeDtypeStruct((R, C), F32)] * 4,
        compiler_params=_cparams(("parallel",)),
    )(parts, w, m, v)


def _sum_parts(parts, name):
    npart, R, C = parts.shape

    def body(p_ref, o_ref):
        g = p_ref[0]
        for i in range(1, npart):
            g = g + p_ref[i]
        o_ref[...] = g

    return pl.pallas_call(body, name=name, out_shape=jax.ShapeDtypeStruct((R, C), F32))(parts)


def _mod_fwd(c_all, w_ada, b_sh, name):
    def body(c_ref, w_ref, b_ref, o_ref):
        o_ref[...] = _dot(_silu(c_ref[...]), w_ref[...], "nn") + b_ref[...]

    return pl.pallas_call(body, name=name, out_shape=jax.ShapeDtypeStruct((c_all.shape[0], w_ada.shape[1]), F32),
                          compiler_params=pltpu.CompilerParams(vmem_limit_bytes=VMEM_LIMIT))(c_all, w_ada, b_sh)


def _mod_wgrad(c_all, dmod_sh, name):
    def body(c_ref, d_ref, o_ref):
        o_ref[...] = _dot(_silu(c_ref[...]), d_ref[...], "tn")

    return pl.pallas_call(body, name=name, out_shape=jax.ShapeDtypeStruct((c_all.shape[1], dmod_sh.shape[1]), F32),
                          compiler_params=pltpu.CompilerParams(vmem_limit_bytes=VMEM_LIMIT))(c_all, dmod_sh)


def _pad_lanes(v):
    return jnp.pad(v, ((0, 0), (0, (-v.shape[1]) % LANES)))


def kernel(x, c, norm1_w, norm2_w, w_ada, b_ada, w_in, conv_w, conv_b, dt_bias, a_log, d_skip, ssd_norm_w, q_norm_w, k_norm_w, attn_norm_w, w_out, w_ff1, w_ff2, loss_target, m_norm1_w, m_norm2_w, m_w_ada, m_b_ada, m_w_in, m_conv_w, m_conv_b, m_dt_bias, m_a_log, m_d_skip, m_ssd_norm_w, m_q_norm_w, m_k_norm_w, m_attn_norm_w, m_w_out, m_w_ff1, m_w_ff2, v_norm1_w, v_norm2_w, v_w_ada, v_b_ada, v_w_in, v_conv_w, v_conv_b, v_dt_bias, v_a_log, v_d_skip, v_ssd_norm_w, v_q_norm_w, v_k_norm_w, v_attn_norm_w, v_w_out, v_w_ff1, v_w_ff2):
    xi, yi, ci = _coords()
    chip = 2 * xi + yi
    dev = 2 * chip + ci
    xs, tgt = x[0], loss_target[0]
    S, D = xs.shape
    DI, AD = NH_SSD * HD, NH_ATT * HD
    CC = DI + 2 * NG * NSTATE
    PW = DI + CC + 3 * AD + LANES
    DFF = w_ff1.shape[2] * 4
    MIX = DI + AD
    o_xbc, o_q, o_k, o_v, o_dt = DI, DI + CC, DI + CC + AD, DI + CC + 2 * AD, DI + CC + 3 * AD

    def half_rows(w):
        r = w.shape[0] // 2
        return lax.dynamic_slice_in_dim(w, ci * r, r, 0).astype(BF16)

    c_all, conv_w_all = _exchange8([c, conv_w[0]], False, "gather_c_conv_w")
    c_all = c_all.reshape(8, D)
    c_all = jnp.pad(c_all, ((0, 8), (0, 0)))
    nmod = w_ada.shape[2]
    b_sh = lax.dynamic_slice_in_dim(b_ada, chip * nmod, nmod, 1)
    mod_sh = _mod_fwd(c_all, w_ada[0], b_sh, "mod_fwd")
    mod_all = _exchange8([mod_sh[:8]], False, "gather_mod")[0]
    mod_me = lax.dynamic_index_in_dim(mod_all[0::2], dev, 1, keepdims=False).reshape(1, 4 * nmod)
    shift1, scale1, gate1, shift2, scale2, gate2 = [mod_me[:, i * D:(i + 1) * D] for i in range(6)]

    g_in, g_out, g_ff1, g_ff2 = _exchange8([half_rows(w_in[0]), half_rows(w_out[0]), half_rows(w_ff1[0]),
                                            half_rows(w_ff2[0])], False, "gather_weights")
    wsh = w_in.shape[2]
    w_in_f = g_in.reshape(4, D, wsh).transpose(1, 0, 2).reshape(D, 4 * wsh)
    n_zx = DI + CC
    w_proj = jnp.concatenate([w_in_f[:, :n_zx], w_in_f[:, n_zx + NH_SSD:], w_in_f[:, n_zx:n_zx + NH_SSD],
                              jnp.zeros((D, LANES - NH_SSD), BF16)], axis=1)
    w_out_f = g_out.reshape(MIX, D)
    w_out_a, w_out_b = w_out_f[:DI], w_out_f[DI:]
    w_ff1_f = g_ff1.reshape(4, D, DFF // 4).transpose(1, 0, 2).reshape(D, DFF)
    w_ff2_f = g_ff2.reshape(DFF, D)

    dtb, alog, dsk = _pad_lanes(dt_bias), _pad_lanes(a_log), _pad_lanes(d_skip)
    qw2 = jnp.concatenate([q_norm_w, q_norm_w], axis=1)
    kw2 = jnp.concatenate([k_norm_w, k_norm_w], axis=1)
    conv_w_f = conv_w_all[0::2].transpose(1, 0, 2).reshape(KCONV, CC)

    h1 = _rows("norm1", lambda r, k: ([_normmod(r[0], *k)], []), [(xs, 0, D)], [norm1_w, scale1, shift1],
               [(D, BF16)], [], S)[0]
    proj = _matmul(h1, w_proj, "nn", F32, "in_proj", tn=896)
    xbc = _conv_fwd(proj, o_xbc, CC, conv_w_f, conv_b, "conv_fwd")
    y_ssd, hsave = _ssd_fwd(xbc, proj, o_dt // LANES, dtb, alog, dsk, ssd_norm_w, "ssd_fwd")

    def qk_call(name, col0, w2, scale):
        def body(t_ref, w_ref, o_ref):
            o_ref[...] = _headnorm(t_ref[...], w_ref[...], scale).astype(BF16)
        tmq = min(512, S)
        return pl.pallas_call(
            body, name=name, grid=(S // tmq, AD // LANES),
            in_specs=[pl.BlockSpec((tmq, LANES), lambda i, j: (i, j + col0 // LANES)),
                      pl.BlockSpec((1, LANES), lambda i, j: (0, 0))],
            out_specs=pl.BlockSpec((tmq, LANES), lambda i, j: (i, j)),
            out_shape=jax.ShapeDtypeStruct((S, AD), BF16), compiler_params=_cparams(("parallel", "parallel")),
        )(proj, w2)

    qn = qk_call("q_norm", o_q, qw2, HD ** -0.5)
    kn = qk_call("k_norm", o_k, kw2, 1.0)
    vb = proj[:, o_v:o_v + AD].astype(BF16)
    qc = [_to_classes(qn, d) for _, d in PATTERNS]
    kc = [_to_classes(kn, d) for _, d in PATTERNS]
    vc = [_to_classes(vb, d) for _, d in PATTERNS]
    nbcs = [S // d // ABLK for _, d in PATTERNS]
    br = [_attn_fwd(qc[i], kc[i], vc[i], nbcs[i], "attn_fwd_%d" % i) for i in range(len(PATTERNS))]
    o_br = [_from_classes(br[i][0], PATTERNS[i][1]) for i in range(len(PATTERNS))]
    l_br = [_from_classes(br[i][1], PATTERNS[i][1]) for i in range(len(PATTERNS))]

    def combine(r, k):
        os_, ls_ = r[:3], r[3:]
        mx = jnp.maximum(jnp.maximum(ls_[0], ls_[1]), ls_[2])
        es = [jnp.exp(l - mx) for l in ls_]
        tot = es[0] + es[1] + es[2]
        o = (es[0] * os_[0] + es[1] * os_[1] + es[2] * os_[2]) / tot
        return [o, mx + jnp.log(tot), _rmsw(o, k[0])], []

    o_att, lse, y_att = _rows("attn_combine", combine, [(t, 0, AD) for t in o_br + l_br], [attn_norm_w],
                              [(AD, F32), (AD, F32), (AD, BF16)], [], S)
    mix_a = _matmul(y_ssd, w_out_a, "nn", F32, "out_proj_a")
    mix = _matmul(y_att, w_out_b, "nn", F32, "out_proj_b", epilogue=lambda r, e: r + e, extras=(mix_a,))
    x2, h2 = _rows("resid_norm2", lambda r, k: (list(_resid_normmod(r[0], r[1], *k)), []), [(xs, 0, D), (mix, 0, D)],
                   [gate1, norm2_w, scale2, shift2], [(D, F32), (D, BF16)], [], S)
    u = _matmul(h2, w_ff1_f, "nn", F32, "ff1")
    relu2 = lambda t: jnp.square(jnp.maximum(t, 0.0))
    ff = _matmul(u, w_ff2_f, "nn", F32, "ff2", a_fn=relu2)

    def loss_fn(r, k):
        x2_, ff_, t_ = r
        err = x2_ + k[0] * ff_ - t_
        dy_ = err * (1.0 / D)
        ls = jnp.sum(jnp.sum(0.5 * err * err, axis=1, keepdims=True), axis=0, keepdims=True) * (1.0 / D)
        return [dy_, dy_ * k[0]], [ls, jnp.sum(dy_ * ff_, axis=0, keepdims=True)]

    dy, dff, loss_p, dgate2 = _rows("loss", loss_fn, [(x2, 0, D), (ff, 0, D), (tgt, 0, D)], [gate2],
                                    [(D, F32), (D, BF16)], [(1, 1), (1, D)], S)
    du = _matmul(dff, w_ff2_f, "nt", BF16, "ff2_dx", epilogue=lambda r, e: r * (2.0 * jnp.maximum(e, 0.0)), extras=(u,))
    gw_ff2 = _matmul(u, dff, "tn", F32, "ff2_dw", a_fn=relu2)
    gw_ff1 = _matmul(h2, du, "tn", F32, "ff1_dw")
    dh2 = _matmul(du, w_ff1_f, "nt", F32, "ff1_dx")

    def resid_bwd(r, k):
        x_, mix_, dx2a, dh2_ = r
        _, vjp = jax.vjp(_resid_normmod, x_, mix_, *k)
        dx, dmix_, dg, dnw, dsc, dsh = vjp((dx2a, dh2_))
        return [dx, dmix_], [dg, dnw, dsc, dsh]

    dx2, dmix, dgate1, g_norm2, dscale2, dshift2 = _rows(
        "resid_norm2_bwd", resid_bwd, [(xs, 0, D), (mix, 0, D), (dy, 0, D), (dh2, 0, D)],
        [gate1, norm2_w, scale2, shift2], [(D, F32), (D, BF16)], [(1, D)] * 4, S)
    dy_ssd = _matmul(dmix, w_out_a, "nt", F32, "out_proj_dx_a")
    dy_att = _matmul(dmix, w_out_b, "nt", F32, "out_proj_dx_b")
    gw_out = jnp.concatenate([_matmul(y_ssd, dmix, "tn", F32, "out_proj_dw_a"),
                              _matmul(y_att, dmix, "tn", F32, "out_proj_dw_b")], axis=0)

    def attn_norm_bwd(r, k):
        o_, dyo = r
        _, vjp = jax.vjp(_rmsw, o_, k[0])
        do_, dw_ = vjp(dyo)
        lo = _lane_mask()
        dd_blocks = []
        for b in range(AD // LANES):
            t = (do_ * o_)[:, b * LANES:(b + 1) * LANES]
            s0 = jnp.sum(jnp.where(lo, t, 0.0), axis=1, keepdims=True)
            s1 = jnp.sum(jnp.where(lo, 0.0, t), axis=1, keepdims=True)
            dd_blocks.append(jnp.where(lo, s0, s1))
        return [do_, jnp.concatenate(dd_blocks, axis=1)], [dw_]

    do_att, dd_att, g_attn_norm = _rows("attn_norm_bwd", attn_norm_bwd, [(o_att, 0, AD), (dy_att, 0, AD)],
                                        [attn_norm_w], [(AD, BF16), (AD, F32)], [(1, AD)], S)
    dqs, dks, dvs = [], [], []
    for i, (_, d) in enumerate(PATTERNS):
        do_c, lse_c, dd_c = _to_classes(do_att, d), _to_classes(lse, d), _to_classes(dd_att, d)
        dq_c = _attn_dq(qc[i], kc[i], vc[i], do_c, lse_c, dd_c, nbcs[i], "attn_dq_%d" % i)
        dk_c, dv_c = _attn_dkv(qc[i], kc[i], vc[i], do_c, lse_c, dd_c, nbcs[i], "attn_dkv_%d" % i)
        dqs.append(_from_classes(dq_c, d))
        dks.append(_from_classes(dk_c, d))
        dvs.append(_from_classes(dv_c, d))

    def qk_bwd_call(name, col0, w2, scale, parts):
        def body(t_ref, w_ref, a_ref, b_ref, c_ref, o_ref, dw_ref):
            @pl.when((pl.program_id(0) == 0) & (pl.program_id(1) == 0))
            def _():
                dw_ref[...] = jnp.zeros_like(dw_ref)
            _, vjp = jax.vjp(lambda t, w: _headnorm(t, w, scale), t_ref[...], w_ref[...])
            dt_, dw_ = vjp(a_ref[...] + b_ref[...] + c_ref[...])
            o_ref[...] = dt_.astype(BF16)
            dw_ref[...] += dw_
        tmq = min(512, S)
        blk = pl.BlockSpec((tmq, LANES), lambda i, j: (i, j))
        return pl.pallas_call(
            body, name=name, grid=(S // tmq, AD // LANES),
            in_specs=[pl.BlockSpec((tmq, LANES), lambda i, j: (i, j + col0 // LANES)),
                      pl.BlockSpec((1, LANES), lambda i, j: (0, 0)), blk, blk, blk],
            out_specs=[blk, pl.BlockSpec((1, LANES), lambda i, j: (0, 0))],
            out_shape=[jax.ShapeDtypeStruct((S, AD), BF16), jax.ShapeDtypeStruct((1, LANES), F32)],
            compiler_params=_cparams(("arbitrary", "arbitrary")),
        )(proj, w2, *parts)

    dq, g_qw2 = qk_bwd_call("q_norm_bwd", o_q, qw2, HD ** -0.5, dqs)
    dk, g_kw2 = qk_bwd_call("k_norm_bwd", o_k, kw2, 1.0, dks)
    dv = _rows("dv_sum", lambda r, k: ([r[0] + r[1] + r[2]], []), [(t, 0, AD) for t in dvs], [], [(AD, BF16)], [], S)[0]
    g_q_norm = g_qw2[:, :HD] + g_qw2[:, HD:]
    g_k_norm = g_kw2[:, :HD] + g_kw2[:, HD:]

    dxbc, dz, ddtr, g_dtb, g_alog, g_dsk, g_ssd_norm = _ssd_bwd(
        xbc, proj, o_dt // LANES, dtb, alog, dsk, ssd_norm_w, hsave, dy_ssd, "ssd_bwd")
    dxbc_pre, g_conv_w, g_conv_b = _conv_bwd(proj, o_xbc, CC, conv_w_f, conv_b, dxbc, "conv_bwd")
    dproj = jnp.concatenate([dz.astype(BF16), dxbc_pre.astype(BF16), dq, dk, dv, ddtr.astype(BF16)], axis=1)
    gw_proj = _matmul(h1, dproj, "tn", F32, "in_proj_dw", tn=896)
    dh1 = _matmul(dproj, w_proj, "nt", F32, "in_proj_dx", tk=896)

    def norm1_bwd(r, k):
        x_, dh_, dres = r
        _, vjp = jax.vjp(_normmod, x_, *k)
        dx, dnw, dsc, dsh = vjp(dh_)
        return [dx + dres], [dnw, dsc, dsh]

    grad_x, g_norm1, dscale1, dshift1 = _rows("norm1_bwd", norm1_bwd, [(xs, 0, D), (dh1, 0, D), (dx2, 0, D)],
                                              [norm1_w, scale1, shift1], [(D, F32)], [(1, D)] * 3, S)
    gw_in = jnp.concatenate([gw_proj[:, :n_zx], gw_proj[:, o_dt:o_dt + NH_SSD], gw_proj[:, n_zx:o_dt]], axis=1)
    dmod = jnp.concatenate([dshift1, dscale1, dgate1, dshift2, dscale2, dgate2], axis=1)

    small = [g_norm1, g_norm2, dmod, g_conv_b, g_dtb, g_alog, g_dsk, g_ssd_norm, _pad_lanes(g_q_norm),
             _pad_lanes(g_k_norm), g_attn_norm, g_conv_w.reshape(1, KCONV * CC)]
    sizes = [t.shape[1] for t in small]
    packed = jnp.concatenate(small, axis=1)
    nrow = -(-packed.shape[1] // LANES // 8) * 8
    packed = jnp.pad(packed, ((0, 0), (0, nrow * LANES - packed.shape[1]))).reshape(nrow, LANES)
    packed_all = _exchange8([packed], False, "gather_small_grads")[0]
    tot = _sum_parts(packed_all, "sum_small_grads").reshape(1, nrow * LANES)
    offs = [sum(sizes[:i]) for i in range(len(sizes))]
    (g_norm1, g_norm2, g_b_ada, g_conv_b, g_dtb, g_alog, g_dsk, g_ssd_norm, g_q_norm, g_k_norm, g_attn_norm,
     g_conv_w) = [tot[:, o:o + n] for o, n in zip(offs, sizes)]
    g_dtb, g_alog, g_dsk = g_dtb[:, :NH_SSD], g_alog[:, :NH_SSD], g_dsk[:, :NH_SSD]
    g_q_norm, g_k_norm = g_q_norm[:, :HD], g_k_norm[:, :HD]
    ccs = CC // 4
    g_conv_w = lax.dynamic_slice_in_dim(g_conv_w.reshape(KCONV, CC), chip * ccs, ccs, 1)

    dmod_all = packed_all.reshape(8, nrow * LANES)[:, offs[2]:offs[2] + 6 * D]
    dmod_sh = jnp.pad(lax.dynamic_slice_in_dim(dmod_all, chip * nmod, nmod, 1), ((0, 8), (0, 0)))
    gw_ada = _mod_wgrad(c_all, dmod_sh, "mod_wgrad")

    def by_half_cols(g):
        r, c4 = g.shape
        return g.reshape(2, r // 2, 4, c4 // 4).transpose(0, 2, 1, 3)

    def by_half_rows(g):
        r4, cdim = g.shape
        return g.reshape(4, 2, r4 // 8, cdim).transpose(1, 0, 2, 3)

    big = [by_half_cols(gw_in), by_half_rows(gw_out), by_half_cols(gw_ff1), by_half_rows(gw_ff2)]
    theirs = _pair_swap(big, "pair_swap_grads")
    pair_sums = []
    for i, (g2, t) in enumerate(zip(big, theirs)):
        mine = lax.dynamic_index_in_dim(g2, ci, 0, keepdims=False)
        _, r2, cdim = t.shape
        sm = _rows("pair_add_%d" % i, lambda r, k: ([r[0] + r[1]], []),
                   [(mine.reshape(4 * r2, cdim), 0, cdim), (t.reshape(4 * r2, cdim), 0, cdim)], [], [(cdim, F32)], [],
                   4 * r2, tm=128)[0]
        pair_sums.append(sm.reshape(4, r2, cdim))
    scattered = _exchange8(pair_sums, True, "scatter_grads")
    parts = [s.reshape(4, 2 * s.shape[1], s.shape[2]) for s in scattered]

    res_in = _adamw(parts[0], w_in[0], m_w_in[0], v_w_in[0], "adamw_w_in")
    res_out = _adamw(parts[1], w_out[0], m_w_out[0], v_w_out[0], "adamw_w_out")
    res_ff1 = _adamw(parts[2], w_ff1[0], m_w_ff1[0], v_w_ff1[0], "adamw_w_ff1")
    res_ff2 = _adamw(parts[3], w_ff2[0], m_w_ff2[0], v_w_ff2[0], "adamw_w_ff2")
    res_ada = _adamw(gw_ada[None], w_ada[0], m_w_ada[0], v_w_ada[0], "adamw_w_ada")

    small_names = ["norm1_w", "norm2_w", "b_ada", "conv_w", "conv_b", "dt_bias", "a_log", "d_skip", "ssd_norm_w",
                   "q_norm_w", "k_norm_w", "attn_norm_w"]
    small_g = dict(norm1_w=g_norm1, norm2_w=g_norm2, b_ada=g_b_ada, conv_w=g_conv_w.reshape(1, KCONV * ccs),
                   conv_b=g_conv_b, dt_bias=g_dtb, a_log=g_alog, d_skip=g_dsk, ssd_norm_w=g_ssd_norm, q_norm_w=g_q_norm,
                   k_norm_w=g_k_norm, attn_norm_w=g_attn_norm)
    small_w = dict(norm1_w=(norm1_w, m_norm1_w, v_norm1_w), norm2_w=(norm2_w, m_norm2_w, v_norm2_w),
                   b_ada=(b_ada, m_b_ada, v_b_ada),
                   conv_w=tuple(t.reshape(1, KCONV * ccs) for t in (conv_w, m_conv_w, v_conv_w)),
                   conv_b=(conv_b, m_conv_b, v_conv_b), dt_bias=(dt_bias, m_dt_bias, v_dt_bias),
                   a_log=(a_log, m_a_log, v_a_log), d_skip=(d_skip, m_d_skip, v_d_skip),
                   ssd_norm_w=(ssd_norm_w, m_ssd_norm_w, v_ssd_norm_w), q_norm_w=(q_norm_w, m_q_norm_w, v_q_norm_w),
                   k_norm_w=(k_norm_w, m_k_norm_w, v_k_norm_w), attn_norm_w=(attn_norm_w, m_attn_norm_w, v_attn_norm_w))
    ssz = [_pad_lanes(small_g[n]).shape[1] for n in small_names]
    soff = [sum(ssz[:i]) for i in range(len(ssz))]
    srow = -(-sum(ssz) // LANES // 8) * 8

    def pack(ts, fill):
        t = jnp.concatenate([jnp.pad(t, ((0, 0), (0, (-t.shape[1]) % LANES)), constant_values=fill) for t in ts], axis=1)
        return jnp.pad(t, ((0, 0), (0, srow * LANES - t.shape[1])), constant_values=fill).reshape(srow, LANES)

    sg = pack([small_g[n] for n in small_names], 0.0)
    sw = pack([small_w[n][0] for n in small_names], 0.0)
    sm_ = pack([small_w[n][1] for n in small_names], 0.0)
    sv = pack([small_w[n][2] for n in small_names], 1.0)
    _, s_delta, s_m, s_v = _adamw(sg[None], sw, sm_, sv, "adamw_small", tm=srow)

    def unpack(t, n):
        i = small_names.index(n)
        return t.reshape(1, srow * LANES)[:, soff[i]:soff[i] + small_g[n].shape[1]].reshape(small_w[n][0].shape)

    loss = lax.psum(loss_p[0, 0], ("x", "y", "c"))
    big_res = dict(w_ada=res_ada, w_in=res_in, w_out=res_out, w_ff1=res_ff1, w_ff2=res_ff2)
    order = ["norm1_w", "norm2_w", "w_ada", "b_ada", "w_in", "conv_w", "conv_b", "dt_bias", "a_log", "d_skip",
             "ssd_norm_w", "q_norm_w", "k_norm_w", "attn_norm_w", "w_out", "w_ff1", "w_ff2"]
    grads, deltas, new_m, new_v = [], [], [], []
    for n in order:
        if n in big_res:
            g_, d_, m_, v_ = [t[None] for t in big_res[n]]
        else:
            g_ = small_g[n].reshape(small_w[n][0].shape)
            d_, m_, v_ = unpack(s_delta, n), unpack(s_m, n), unpack(s_v, n)
            if n == "conv_w":
                g_, d_, m_, v_ = [t.reshape(conv_w.shape) for t in (g_, d_, m_, v_)]
        grads.append(g_)
        deltas.append(d_)
        new_m.append(m_)
        new_v.append(v_)
    return (loss, grad_x[None], *grads, *deltas, *new_m, *new_v)
```

```python
import functools

import jax
import jax.numpy as jnp
from jax import lax
from jax.experimental import pallas as pl
from jax.experimental.pallas import tpu as pltpu

F32, BF16 = jnp.float32, jnp.bfloat16
EPS = 1e-6
HD = 64
NH_SSD = 16
NG = 4
NSTATE = 128
KCONV = 4
CHUNK = 128
NH_ATT = 16
PATTERNS = ((128, 1), (512, 4), (2048, 16))
ABLK = 128
LANES = 128
ADAM_LR, ADAM_B1, ADAM_B2, ADAM_EPS, ADAM_WD, ADAM_STEP = 0.001, 0.9, 0.999, 1e-08, 0.01, 10
VMEM_LIMIT = 56 * 1024 * 1024
MESH = pl.DeviceIdType.MESH
NEG = -1e30

_DN = {"nn": (((1,), (0,)), ((), ())), "nt": (((1,), (1,)), ((), ())), "tn": (((0,), (0,)), ((), ()))}


def _cparams(sem):
    return pltpu.CompilerParams(dimension_semantics=sem, vmem_limit_bytes=VMEM_LIMIT)


def _tile(n, cap):
    if n % LANES or n <= LANES:
        return n
    best = LANES
    for t in range(LANES, min(n, cap) + 1, LANES):
        if n % t == 0:
            best = t
    return best


def _silu(x):
    return x / (1.0 + jnp.exp(-x))


def _softplus(x):
    return jnp.maximum(x, 0.0) + jnp.log(1.0 + jnp.exp(-jnp.abs(x)))


def _dot(a, b, dims):
    return lax.dot_general(a.astype(BF16), b.astype(BF16), _DN[dims], preferred_element_type=F32)


def _matmul(a, b, dims, out_dtype, name, a_fn=None, epilogue=None, extras=(), tm=512, tn=1024, tk=512):
    if dims == "nn":
        (M, K), (_, N) = a.shape, b.shape
    elif dims == "nt":
        (M, K), (N, _) = a.shape, b.shape
    else:
        (K, M), (_, N) = a.shape, b.shape
    tm, tn, tk = _tile(M, tm), _tile(N, tn), _tile(K, tk)
    nk = K // tk
    ne = len(extras)

    def body(a_ref, b_ref, *rest):
        e_refs, o_ref, acc = rest[:ne], rest[ne], rest[ne + 1]
        k = pl.program_id(2)

        @pl.when(k == 0)
        def _():
            acc[...] = jnp.zeros_like(acc)

        av = a_ref[...]
        if a_fn is not None:
            av = a_fn(av)
        acc[...] += _dot(av, b_ref[...], dims)

        @pl.when(k == nk - 1)
        def _():
            r = acc[...]
            if epilogue is not None:
                r = epilogue(r, *[e[...] for e in e_refs])
            o_ref[...] = r.astype(out_dtype)

    a_spec = pl.BlockSpec((tk, tm), lambda i, j, k: (k, i)) if dims == "tn" else pl.BlockSpec((tm, tk), lambda i, j, k: (i, k))
    b_spec = pl.BlockSpec((tn, tk), lambda i, j, k: (j, k)) if dims == "nt" else pl.BlockSpec((tk, tn), lambda i, j, k: (k, j))
    o_spec = pl.BlockSpec((tm, tn), lambda i, j, k: (i, j))
    return pl.pallas_call(
        body, name=name, grid=(M // tm, N // tn, nk),
        in_specs=[a_spec, b_spec] + [o_spec] * ne, out_specs=o_spec,
        out_shape=jax.ShapeDtypeStruct((M, N), out_dtype),
        scratch_shapes=[pltpu.VMEM((tm, tn), F32)],
        compiler_params=_cparams(("parallel", "parallel", "arbitrary")),
    )(a, b, *extras)


def _rows(name, fn, rows, consts, outs, accs, n_rows, tm=256):
    tm = min(tm, n_rows)
    nr, nc, no, na = len(rows), len(consts), len(outs), len(accs)

    def body(*refs):
        r_refs, c_refs = refs[:nr], refs[nr:nr + nc]
        o_refs, a_refs = refs[nr + nc:nr + nc + no], refs[nr + nc + no:]
        o_vals, a_vals = fn([r[...] for r in r_refs], [c[...] for c in c_refs])
        for ref, val in zip(o_refs, o_vals):
            ref[...] = val.astype(ref.dtype)
        if na:
            @pl.when(pl.program_id(0) == 0)
            def _():
                for ref in a_refs:
                    ref[...] = jnp.zeros_like(ref)
            for ref, val in zip(a_refs, a_vals):
                ref[...] += val

    in_specs = [pl.BlockSpec((tm, w), lambda i, cb=cb: (i, cb)) for (_, cb, w) in rows]
    in_specs += [pl.BlockSpec(cst.shape, lambda i, nd=cst.ndim: (0,) * nd) for cst in consts]
    out_specs = [pl.BlockSpec((tm, w), lambda i: (i, 0)) for (w, _) in outs]
    out_specs += [pl.BlockSpec(s, lambda i: (0, 0)) for s in accs]
    out_shape = [jax.ShapeDtypeStruct((n_rows, w), dt) for (w, dt) in outs]
    out_shape += [jax.ShapeDtypeStruct(s, F32) for s in accs]
    res = pl.pallas_call(
        body, name=name, grid=(n_rows // tm,), in_specs=in_specs, out_specs=out_specs, out_shape=out_shape,
        compiler_params=_cparams(("arbitrary",)),
    )(*[r[0] for r in rows], *consts)
    return res


def _normmod(x, nw, sc, sh):
    r = lax.rsqrt(jnp.mean(x * x, axis=-1, keepdims=True) + EPS)
    return (x * r) * nw * (1.0 + sc) + sh


def _resid_normmod(x, mix, g, nw, sc, sh):
    x2 = x + g * mix
    return x2, _normmod(x2, nw, sc, sh)


def _rmsw(o, w):
    return o * lax.rsqrt(jnp.mean(o * o, axis=-1, keepdims=True) + EPS) * w


def _lane_mask():
    return lax.broadcasted_iota(jnp.int32, (1, LANES), 1) < HD


def _headnorm(t, w, scale):
    lo = _lane_mask()
    t2 = t * t
    s0 = jnp.sum(jnp.where(lo, t2, 0.0), axis=1, keepdims=True)
    s1 = jnp.sum(jnp.where(lo, 0.0, t2), axis=1, keepdims=True)
    ms = jnp.where(lo, s0, s1) * (1.0 / HD)
    return t * lax.rsqrt(ms + EPS) * w * scale


def _conv_cols(n_ch):
    return _tile(n_ch, 256)


def _conv_fwd(proj, col0, n_ch, conv_w, conv_b, name):
    S = proj.shape[0]
    tc = _conv_cols(n_ch)

    def body(u_ref, w_ref, b_ref, o_ref):
        u = u_ref[...]
        row = lax.broadcasted_iota(jnp.int32, u.shape, 0)
        acc = b_ref[...] + w_ref[KCONV - 1:KCONV, :] * u
        for i in range(KCONV - 1):
            sh = KCONV - 1 - i
            acc = acc + w_ref[i:i + 1, :] * jnp.where(row >= sh, pltpu.roll(u, sh, 0), 0.0)
        o_ref[...] = _silu(acc)

    return pl.pallas_call(
        body, name=name, grid=(n_ch // tc,),
        in_specs=[pl.BlockSpec((S, tc), lambda j: (0, j + col0 // tc)),
                  pl.BlockSpec((KCONV, tc), lambda j: (0, j)), pl.BlockSpec((1, tc), lambda j: (0, j))],
        out_specs=pl.BlockSpec((S, tc), lambda j: (0, j)),
        out_shape=jax.ShapeDtypeStruct((S, n_ch), F32),
        compiler_params=_cparams(("parallel",)),
    )(proj, conv_w, conv_b)


def _conv_bwd(proj, col0, n_ch, conv_w, conv_b, dxbc, name):
    S = proj.shape[0]
    tc = _conv_cols(n_ch)

    def body(u_ref, w_ref, b_ref, g_ref, du_ref, dw_ref, db_ref):
        u = u_ref[...]
        row = lax.broadcasted_iota(jnp.int32, u.shape, 0)
        shifted = [jnp.where(row >= s, pltpu.roll(u, s, 0), 0.0) for s in range(1, KCONV)]
        acc = b_ref[...] + w_ref[KCONV - 1:KCONV, :] * u
        for i in range(KCONV - 1):
            acc = acc + w_ref[i:i + 1, :] * shifted[KCONV - 2 - i]
        sig = 1.0 / (1.0 + jnp.exp(-acc))
        dacc = g_ref[...] * (sig * (1.0 + acc * (1.0 - sig)))
        db_ref[...] = jnp.sum(dacc, axis=0, keepdims=True)
        du = w_ref[KCONV - 1:KCONV, :] * dacc
        dw_ref[KCONV - 1:KCONV, :] = jnp.sum(dacc * u, axis=0, keepdims=True)
        for i in range(KCONV - 1):
            sh = KCONV - 1 - i
            dw_ref[i:i + 1, :] = jnp.sum(dacc * shifted[sh - 1], axis=0, keepdims=True)
            du = du + w_ref[i:i + 1, :] * jnp.where(row < S - sh, pltpu.roll(dacc, S - sh, 0), 0.0)
        du_ref[...] = du

    return pl.pallas_call(
        body, name=name, grid=(n_ch // tc,),
        in_specs=[pl.BlockSpec((S, tc), lambda j: (0, j + col0 // tc)),
                  pl.BlockSpec((KCONV, tc), lambda j: (0, j)), pl.BlockSpec((1, tc), lambda j: (0, j)),
                  pl.BlockSpec((S, tc), lambda j: (0, j))],
        out_specs=[pl.BlockSpec((S, tc), lambda j: (0, j)), pl.BlockSpec((KCONV, tc), lambda j: (0, j)),
                   pl.BlockSpec((1, tc), lambda j: (0, j))],
        out_shape=[jax.ShapeDtypeStruct((S, n_ch), F32), jax.ShapeDtypeStruct((KCONV, n_ch), F32),
                   jax.ShapeDtypeStruct((1, n_ch), F32)],
        compiler_params=_cparams(("parallel",)),
    )(proj, conv_w, conv_b, dxbc)


@functools.partial(jax.custom_vjp, nondiff_argnums=(2,))
def _mm(a, b, dims):
    return _dot(a, b, dims)


def _mm_fwd(a, b, dims):
    return _dot(a, b, dims), (a, b)


def _mm_bwd(dims, res, g):
    a, b = res
    if dims == "nn":
        return _dot(g, b, "nt"), _dot(a, g, "tn")
    if dims == "nt":
        return _dot(g, b, "nn"), _dot(g, a, "tn")
    return _dot(b, g, "nt"), _dot(a, g, "nn")


_mm.defvjp(_mm_fwd, _mm_bwd)


def _tri_dot(x, upper):
    n = x.shape[0]
    r = lax.broadcasted_iota(jnp.int32, (n, n), 0)
    c = lax.broadcasted_iota(jnp.int32, (n, n), 1)
    t = jnp.where((r <= c) if upper else (r >= c), 1.0, 0.0)
    return lax.dot_general(t, x, _DN["nn"], precision=lax.Precision.HIGHEST, preferred_element_type=F32)


@jax.custom_vjp
def _cumsum_rows(x):
    return _tri_dot(x, False)


_cumsum_rows.defvjp(lambda x: (_tri_dot(x, False), None), lambda _, g: (_tri_dot(g, True),))


def _ssd_chunk(xs_p, bm_g, cm_g, dtr, z_p, dtb, alog, dsk, nw_p, h_p):
    L = dtr.shape[0]
    n_pairs = len(xs_p)
    ppg = n_pairs // len(bm_g)
    lane = lax.broadcasted_iota(jnp.int32, (1, LANES), 1)
    sub = lax.broadcasted_iota(jnp.int32, (LANES, 1), 0)
    lo = lane < HD
    row_l = lax.broadcasted_iota(jnp.int32, (L, 1), 0)
    tri = lax.broadcasted_iota(jnp.int32, (L, L), 0) >= lax.broadcasted_iota(jnp.int32, (L, L), 1)

    dt = _softplus(dtr + dtb)
    acs = _cumsum_rows(dt * (-jnp.exp(alog)))
    acs_t = acs.T
    a_last = jnp.sum(jnp.where(row_l == L - 1, acs, 0.0), axis=0, keepdims=True)
    e_acs = jnp.exp(acs)
    dec = jnp.exp(a_last - acs)
    cdec = jnp.exp(a_last)

    def colv(m, h):
        return jnp.sum(jnp.where(lane == h, m, 0.0), axis=1, keepdims=True)

    def rowv(mt, h):
        return jnp.sum(jnp.where(sub == h, mt, 0.0), axis=0, keepdims=True)

    def pair(m, h0):
        return jnp.where(lo, colv(m, h0), colv(m, h0 + 1))

    ys, hs = [], []
    cb = None
    for p in range(n_pairs):
        g, h0 = p // ppg, 2 * p
        bmat, cmat = bm_g[g], cm_g[g]
        if p % ppg == 0:
            cb = _mm(cmat, bmat, "nt")
        x = xs_p[p]
        xdt = x * pair(dt, h0)
        yd = []
        for h in (h0, h0 + 1):
            seg = colv(acs, h) - rowv(acs_t, h)
            lm = jnp.where(tri, jnp.exp(jnp.where(tri, seg, 0.0)), 0.0)
            yd.append(_mm(cb * lm, xdt, "nn"))
        y = jnp.where(lo, yd[0], yd[1])
        y = y + _mm(cmat, h_p[p], "nt") * pair(e_acs, h0)
        st = _mm(xdt * pair(dec, h0), bmat, "tn")
        cd_col = jnp.where(sub < HD, colv(cdec, h0), colv(cdec, h0 + 1))
        hs.append(h_p[p] * cd_col + st)
        ys.append(y + pair(dsk, h0) * x)

    y2 = [ys[p] * _silu(z_p[p]) for p in range(n_pairs)]
    outs = []
    for g in range(len(bm_g)):
        ps = range(g * ppg, (g + 1) * ppg)
        ss = sum(jnp.sum(y2[p] * y2[p], axis=1, keepdims=True) for p in ps)
        rs = lax.rsqrt(ss * (1.0 / (ppg * LANES)) + EPS)
        outs += [y2[p] * rs * nw_p[p] for p in ps]
    return outs, hs


def _ssd_slices(xbc_ref, z_ref, nw_ref, di):
    n_pairs = di // LANES
    xs_p = [xbc_ref[:, p * LANES:(p + 1) * LANES] for p in range(n_pairs)]
    bm_g = [xbc_ref[:, di + g * NSTATE:di + (g + 1) * NSTATE] for g in range(NG)]
    cm_g = [xbc_ref[:, di + (NG + g) * NSTATE:di + (NG + g + 1) * NSTATE] for g in range(NG)]
    z_p = [z_ref[:, p * LANES:(p + 1) * LANES] for p in range(n_pairs)]
    nw_p = [nw_ref[:, p * LANES:(p + 1) * LANES] for p in range(n_pairs)]
    return xs_p, bm_g, cm_g, z_p, nw_p


def _ssd_fwd(xbc, proj, dt_cb, dtb, alog, dsk, nw, name):
    S, cc = xbc.shape
    di = NH_SSD * HD
    n_pairs = di // LANES
    nchunk = S // CHUNK

    def body(xbc_ref, z_ref, dtr_ref, dtb_ref, alog_ref, dsk_ref, nw_ref, y_ref, hs_ref, h_scr):
        @pl.when(pl.program_id(0) == 0)
        def _():
            h_scr[...] = jnp.zeros_like(h_scr)

        xs_p, bm_g, cm_g, z_p, nw_p = _ssd_slices(xbc_ref, z_ref, nw_ref, di)
        h_p = [h_scr[p * LANES:(p + 1) * LANES, :] for p in range(n_pairs)]
        hs_ref[...] = h_scr[...]
        outs, hs = _ssd_chunk(xs_p, bm_g, cm_g, dtr_ref[...], z_p, dtb_ref[...], alog_ref[...], dsk_ref[...], nw_p, h_p)
        for p in range(n_pairs):
            y_ref[:, p * LANES:(p + 1) * LANES] = outs[p].astype(y_ref.dtype)
            h_scr[p * LANES:(p + 1) * LANES, :] = hs[p]

    vec = pl.BlockSpec((1, LANES), lambda c: (0, 0))
    return pl.pallas_call(
        body, name=name, grid=(nchunk,),
        in_specs=[pl.BlockSpec((CHUNK, cc), lambda c: (c, 0)), pl.BlockSpec((CHUNK, di), lambda c: (c, 0)),
                  pl.BlockSpec((CHUNK, LANES), lambda c: (c, dt_cb)), vec, vec, vec,
                  pl.BlockSpec((1, di), lambda c: (0, 0))],
        out_specs=[pl.BlockSpec((CHUNK, di), lambda c: (c, 0)), pl.BlockSpec((None, di, NSTATE), lambda c: (c, 0, 0))],
        out_shape=[jax.ShapeDtypeStruct((S, di), BF16), jax.ShapeDtypeStruct((nchunk, di, NSTATE), F32)],
        scratch_shapes=[pltpu.VMEM((di, NSTATE), F32)],
        compiler_params=_cparams(("arbitrary",)),
    )(xbc, proj, proj, dtb, alog, dsk, nw)


def _ssd_bwd(xbc, proj, dt_cb, dtb, alog, dsk, nw, hsave, dy, name):
    S, cc = xbc.shape
    di = NH_SSD * HD
    n_pairs = di // LANES
    nchunk = S // CHUNK

    def body(xbc_ref, z_ref, dtr_ref, dtb_ref, alog_ref, dsk_ref, nw_ref, hs_ref, dy_ref,
             dxbc_ref, dz_ref, ddtr_ref, ddtb_ref, dalog_ref, ddsk_ref, dnw_ref, dh_scr):
        @pl.when(pl.program_id(0) == 0)
        def _():
            dh_scr[...] = jnp.zeros_like(dh_scr)
            ddtb_ref[...] = jnp.zeros_like(ddtb_ref)
            dalog_ref[...] = jnp.zeros_like(dalog_ref)
            ddsk_ref[...] = jnp.zeros_like(ddsk_ref)
            dnw_ref[...] = jnp.zeros_like(dnw_ref)

        xs_p, bm_g, cm_g, z_p, nw_p = _ssd_slices(xbc_ref, z_ref, nw_ref, di)
        h_p = [hs_ref[p * LANES:(p + 1) * LANES, :] for p in range(n_pairs)]
        dy_p = [dy_ref[:, p * LANES:(p + 1) * LANES].astype(F32) for p in range(n_pairs)]
        dh_p = [dh_scr[p * LANES:(p + 1) * LANES, :] for p in range(n_pairs)]
        _, vjp = jax.vjp(_ssd_chunk, xs_p, bm_g, cm_g, dtr_ref[...], z_p, dtb_ref[...], alog_ref[...], dsk_ref[...],
                         nw_p, h_p)
        dxs, dbm, dcm, ddtr, dz, ddtb, dalog, ddsk, dnw, dh = vjp((dy_p, dh_p))
        for p in range(n_pairs):
            sl = slice(p * LANES, (p + 1) * LANES)
            dxbc_ref[:, sl] = dxs[p]
            dz_ref[:, sl] = dz[p]
            dnw_ref[:, sl] += dnw[p]
            dh_scr[sl, :] = dh[p]
        for g in range(NG):
            dxbc_ref[:, di + g * NSTATE:di + (g + 1) * NSTATE] = dbm[g]
            dxbc_ref[:, di + (NG + g) * NSTATE:di + (NG + g + 1) * NSTATE] = dcm[g]
        ddtr_ref[...] = ddtr
        ddtb_ref[...] += ddtb
        dalog_ref[...] += dalog
        ddsk_ref[...] += ddsk

    last = nchunk - 1
    vec = pl.BlockSpec((1, LANES), lambda c: (0, 0))
    return pl.pallas_call(
        body, name=name, grid=(nchunk,),
        in_specs=[pl.BlockSpec((CHUNK, cc), lambda c: (last - c, 0)), pl.BlockSpec((CHUNK, di), lambda c: (last - c, 0)),
                  pl.BlockSpec((CHUNK, LANES), lambda c: (last - c, dt_cb)), vec, vec, vec,
                  pl.BlockSpec((1, di), lambda c: (0, 0)),
                  pl.BlockSpec((None, di, NSTATE), lambda c: (last - c, 0, 0)),
                  pl.BlockSpec((CHUNK, di), lambda c: (last - c, 0))],
        out_specs=[pl.BlockSpec((CHUNK, cc), lambda c: (last - c, 0)), pl.BlockSpec((CHUNK, di), lambda c: (last - c, 0)),
                   pl.BlockSpec((CHUNK, LANES), lambda c: (last - c, 0)), vec, vec, vec,
                   pl.BlockSpec((1, di), lambda c: (0, 0))],
        out_shape=[jax.ShapeDtypeStruct((S, cc), F32), jax.ShapeDtypeStruct((S, di), F32),
                   jax.ShapeDtypeStruct((S, LANES), F32), jax.ShapeDtypeStruct((1, LANES), F32),
                   jax.ShapeDtypeStruct((1, LANES), F32), jax.ShapeDtypeStruct((1, LANES), F32),
                   jax.ShapeDtypeStruct((1, di), F32)],
        scratch_shapes=[pltpu.VMEM((di, NSTATE), F32)],
        compiler_params=_cparams(("arbitrary",)),
    )(xbc, proj, proj, dtb, alog, dsk, nw, hsave, dy)


def _band_masks(rows_q, rows_k):
    qi = lax.broadcasted_iota(jnp.int32, (rows_q, rows_k), 0)
    ki = lax.broadcasted_iota(jnp.int32, (rows_q, rows_k), 1)
    return qi, ki


def _class_chunks(n_rows, d):
    per_class = n_rows // d
    ch = min(per_class, 256)
    out = []
    for r in range(d):
        for c0 in range(0, per_class, ch):
            tok = pl.ds(c0, ch) if d == 1 else pl.ds(r + d * c0, ch, stride=d)
            out.append((tok, pl.ds(r * per_class + c0, ch)))
    return out


def _to_class_order(src_ref, dst_ref, n_rows, d):
    for tok, cls in _class_chunks(n_rows, d):
        dst_ref[cls, :] = src_ref[tok, :].astype(dst_ref.dtype)


def _blk_rows(t):
    return pl.ds(pl.multiple_of(t * ABLK, ABLK), ABLK)


def _head_col(t, msk):
    return jnp.max(jnp.where(msk, t, NEG), axis=1, keepdims=True)


def _zero_unless(msk, t):
    return jnp.where(msk, t, jnp.zeros_like(t))


def _attn_fwd(qn, kn, proj, v_cb, name):
    S, ad = qn.shape
    nb = S // ABLK
    nbr = len(PATTERNS)

    def body(q_ref, k_ref, v_ref, o_ref, lse_ref, qc, kc, vc, ob, mb, lb, m_s, l_s):
        lo = _lane_mask()
        qi, ki = _band_masks(ABLK, 2 * ABLK)
        band, in_cur, prev_ok = ki <= qi + ABLK, ki >= ABLK, ki >= qi
        for bi, (_, d) in enumerate(PATTERNS):
            nbc = S // d // ABLK
            first, last = bi == 0, bi == nbr - 1
            for src, dst in ((q_ref, qc), (k_ref, kc), (v_ref, vc)):
                _to_class_order(src, dst, S, d)
            o_dst, m_dst, l_dst = (o_ref, m_s, l_s) if first else (ob, mb, lb)

            def blk(t, carry, nbc=nbc, o_dst=o_dst, m_dst=m_dst, l_dst=l_dst):
                rows, prow = _blk_rows(t), _blk_rows(jnp.maximum(t - 1, 0))
                valid = band & (in_cur | (prev_ok & ((t % nbc) != 0)))
                qv = qc[rows, :]
                kk = jnp.concatenate([kc[prow, :], kc[rows, :]], axis=0)
                vv = jnp.concatenate([vc[prow, :], vc[rows, :]], axis=0)
                os_, ms_, ls_ = [], [], []
                for msk in (lo, jnp.logical_not(lo)):
                    s = jnp.where(valid, _dot(_zero_unless(msk, qv), kk, "nt"), NEG)
                    m = jnp.max(s, axis=1, keepdims=True)
                    p = jnp.exp(s - m)
                    os_.append(_dot(p, vv, "nn"))
                    ms_.append(m)
                    ls_.append(jnp.sum(p, axis=1, keepdims=True))
                o_dst[rows, :] = jnp.where(lo, os_[0], os_[1])
                m_dst[rows, :] = jnp.where(lo, ms_[0], ms_[1])
                l_dst[rows, :] = jnp.where(lo, ls_[0], ls_[1])
                return carry

            lax.fori_loop(0, nb, blk, 0, unroll=2)
            if first:
                continue
            for tok, cls in _class_chunks(S, d):
                m_old, m_b = m_s[tok, :], mb[cls, :]
                m_new = jnp.maximum(m_old, m_b)
                a, b = jnp.exp(m_old - m_new), jnp.exp(m_b - m_new)
                l_new = a * l_s[tok, :] + b * lb[cls, :]
                o_new = a * o_ref[tok, :] + b * ob[cls, :]
                if last:
                    o_ref[tok, :] = o_new / l_new
                    lse_ref[tok, :] = m_new + jnp.log(l_new)
                else:
                    o_ref[tok, :] = o_new
                    m_s[tok, :] = m_new
                    l_s[tok, :] = l_new

    col = pl.BlockSpec((S, LANES), lambda h: (0, h))
    return pl.pallas_call(
        body, name=name, grid=(ad // LANES,),
        in_specs=[col, col, pl.BlockSpec((S, LANES), lambda h: (0, h + v_cb))], out_specs=[col, col],
        out_shape=[jax.ShapeDtypeStruct((S, ad), F32), jax.ShapeDtypeStruct((S, ad), F32)],
        scratch_shapes=[pltpu.VMEM((S, LANES), BF16)] * 3 + [pltpu.VMEM((S, LANES), F32)] * 5,
        compiler_params=_cparams(("parallel",)),
    )(qn, kn, proj)


def _attn_dq(qn, kn, proj, v_cb, do, lse, dd, name):
    S, ad = qn.shape
    nb = S // ABLK

    def body(q_ref, k_ref, v_ref, do_ref, lse_ref, dd_ref, dq_ref, qc, kc, vc, doc, lsec, ddc, dqc):
        lo = _lane_mask()
        qi, ki = _band_masks(ABLK, 2 * ABLK)
        band, in_cur, prev_ok = ki <= qi + ABLK, ki >= ABLK, ki >= qi
        for bi, (_, d) in enumerate(PATTERNS):
            nbc = S // d // ABLK
            first = bi == 0
            for src, dst in ((q_ref, qc), (k_ref, kc), (v_ref, vc), (do_ref, doc), (lse_ref, lsec), (dd_ref, ddc)):
                _to_class_order(src, dst, S, d)
            dq_dst = dq_ref if first else dqc

            def blk(t, carry, nbc=nbc, dq_dst=dq_dst):
                rows, prow = _blk_rows(t), _blk_rows(jnp.maximum(t - 1, 0))
                valid = band & (in_cur | (prev_ok & ((t % nbc) != 0)))
                qv, dov, lse_b, dd_b = qc[rows, :], doc[rows, :], lsec[rows, :], ddc[rows, :]
                kk = jnp.concatenate([kc[prow, :], kc[rows, :]], axis=0)
                vv = jnp.concatenate([vc[prow, :], vc[rows, :]], axis=0)
                dqs = []
                for msk in (lo, jnp.logical_not(lo)):
                    s = jnp.where(valid, _dot(_zero_unless(msk, qv), kk, "nt"), NEG)
                    p = jnp.exp(s - _head_col(lse_b, msk))
                    dp = _dot(_zero_unless(msk, dov), vv, "nt")
                    ds = p * (dp - _head_col(dd_b, msk))
                    dqs.append(_dot(ds, kk, "nn"))
                dq_dst[rows, :] = jnp.where(lo, dqs[0], dqs[1])
                return carry

            lax.fori_loop(0, nb, blk, 0, unroll=2)
            if not first:
                for tok, cls in _class_chunks(S, d):
                    dq_ref[tok, :] = dq_ref[tok, :] + dqc[cls, :]

    col = pl.BlockSpec((S, LANES), lambda h: (0, h))
    return pl.pallas_call(
        body, name=name, grid=(ad // LANES,),
        in_specs=[col, col, pl.BlockSpec((S, LANES), lambda h: (0, h + v_cb)), col, col, col], out_specs=col,
        out_shape=jax.ShapeDtypeStruct((S, ad), F32),
        scratch_shapes=[pltpu.VMEM((S, LANES), BF16)] * 4 + [pltpu.VMEM((S, LANES), F32)] * 3,
        compiler_params=_cparams(("parallel",)),
    )(qn, kn, proj, do, lse, dd)


def _attn_dkv(qn, kn, proj, v_cb, do, lse, dd, name):
    S, ad = qn.shape
    nb = S // ABLK

    def body(q_ref, k_ref, v_ref, do_ref, lse_ref, dd_ref, dk_ref, dv_ref, qc, kc, vc, doc, lsec, ddc, dkc, dvc):
        lo = _lane_mask()
        qi, ki = _band_masks(2 * ABLK, ABLK)
        cur_ok, nxt_ok = (qi < ABLK) & (ki <= qi), (qi >= ABLK) & (ki >= qi - ABLK)
        for bi, (_, d) in enumerate(PATTERNS):
            nbc = S // d // ABLK
            first = bi == 0
            for src, dst in ((q_ref, qc), (k_ref, kc), (v_ref, vc), (do_ref, doc), (lse_ref, lsec), (dd_ref, ddc)):
                _to_class_order(src, dst, S, d)
            dk_dst, dv_dst = (dk_ref, dv_ref) if first else (dkc, dvc)

            def blk(t, carry, nbc=nbc, dk_dst=dk_dst, dv_dst=dv_dst):
                rows, nrow = _blk_rows(t), _blk_rows(jnp.minimum(t + 1, nb - 1))
                valid = cur_ok | (nxt_ok & (((t + 1) % nbc) != 0))
                kv, vv = kc[rows, :], vc[rows, :]
                qq = jnp.concatenate([qc[rows, :], qc[nrow, :]], axis=0)
                do2 = jnp.concatenate([doc[rows, :], doc[nrow, :]], axis=0)
                lse2 = jnp.concatenate([lsec[rows, :], lsec[nrow, :]], axis=0)
                dd2 = jnp.concatenate([ddc[rows, :], ddc[nrow, :]], axis=0)
                dk = jnp.zeros((ABLK, LANES), F32)
                dv = jnp.zeros((ABLK, LANES), F32)
                for msk in (lo, jnp.logical_not(lo)):
                    qh, doh = _zero_unless(msk, qq), _zero_unless(msk, do2)
                    s = jnp.where(valid, _dot(qh, kv, "nt"), NEG)
                    p = jnp.exp(s - _head_col(lse2, msk))
                    dv = dv + _dot(p, doh, "tn")
                    ds = p * (_dot(doh, vv, "nt") - _head_col(dd2, msk))
                    dk = dk + _dot(ds, qh, "tn")
                dk_dst[rows, :] = dk
                dv_dst[rows, :] = dv
                return carry

            lax.fori_loop(0, nb, blk, 0, unroll=2)
            if not first:
                for tok, cls in _class_chunks(S, d):
                    dk_ref[tok, :] = dk_ref[tok, :] + dkc[cls, :]
                    dv_ref[tok, :] = dv_ref[tok, :] + dvc[cls, :]

    col = pl.BlockSpec((S, LANES), lambda h: (0, h))
    return pl.pallas_call(
        body, name=name, grid=(ad // LANES,),
        in_specs=[col, col, pl.BlockSpec((S, LANES), lambda h: (0, h + v_cb)), col, col, col], out_specs=[col, col],
        out_shape=[jax.ShapeDtypeStruct((S, ad), F32), jax.ShapeDtypeStruct((S, ad), F32)],
        scratch_shapes=[pltpu.VMEM((S, LANES), BF16)] * 4 + [pltpu.VMEM((S, LANES), F32)] * 4,
        compiler_params=_cparams(("parallel",)),
    )(qn, kn, proj, do, lse, dd)


def _coords():
    return lax.axis_index("x"), lax.axis_index("y"), lax.axis_index("c")


def _exchange8(xs, per_dest, name):
    n = len(xs)
    blk = [x.shape[1:] if per_dest else x.shape for x in xs]

    def body(*refs):
        ins, outs = refs[:n], refs[n:2 * n]
        send_sems, recv_sems, local_sems = refs[2 * n:]
        x, y, c = _coords()
        sibling = (x, y, 1 - c)
        chips = [(1 - x, y), (x, 1 - y), (1 - x, 1 - y)]
        first, passed, mine = [], [], []
        for a in range(n):
            def src_for(cx, cy, a=a):
                return ins[a].at[2 * cx + cy] if per_dest else ins[a]

            def slot(px, py, pc, a=a):
                return outs[a].at[4 * px + 2 * py + pc]

            def copy(k, src, dst, to, a=a):
                return pltpu.make_async_remote_copy(src_ref=src, dst_ref=dst, send_sem=send_sems.at[7 * a + k],
                                                    recv_sem=recv_sems.at[7 * a + k], device_id=to, device_id_type=MESH)

            m = pltpu.make_async_copy(src_for(x, y), slot(x, y, c), local_sems.at[a])
            m.start()
            mine.append(m)
            cps = [copy(0, src_for(x, y), slot(x, y, c), sibling)]
            cps += [copy(1 + j, src_for(*chip), slot(x, y, c), (*chip, c)) for j, chip in enumerate(chips)]
            for cp in cps:
                cp.start()
            first += cps
        for a in range(n):
            def slot(px, py, pc, a=a):
                return outs[a].at[4 * px + 2 * py + pc]

            def copy(k, src, dst, to, a=a):
                return pltpu.make_async_remote_copy(src_ref=src, dst_ref=dst, send_sem=send_sems.at[7 * a + k],
                                                    recv_sem=recv_sems.at[7 * a + k], device_id=to, device_id_type=MESH)

            for j, chip in enumerate(chips):
                copy(1 + j, slot(*chip, c), slot(*chip, c), (*chip, c)).wait_recv()
                fw = copy(4 + j, slot(*chip, c), slot(*chip, c), sibling)
                fw.start()
                passed.append(fw)
        for a in range(n):
            def slot(px, py, pc, a=a):
                return outs[a].at[4 * px + 2 * py + pc]

            def copy(k, src, dst, to, a=a):
                return pltpu.make_async_remote_copy(src_ref=src, dst_ref=dst, send_sem=send_sems.at[7 * a + k],
                                                    recv_sem=recv_sems.at[7 * a + k], device_id=to, device_id_type=MESH)

            copy(0, slot(x, y, 1 - c), slot(x, y, 1 - c), sibling).wait_recv()
            for j, chip in enumerate(chips):
                copy(4 + j, slot(*chip, 1 - c), slot(*chip, 1 - c), sibling).wait_recv()
        for cp in first + passed:
            cp.wait_send()
        for m in mine:
            m.wait()

    anyspec = pl.BlockSpec(memory_space=pl.ANY)
    res = pl.pallas_call(
        body, name=name, in_specs=[anyspec] * n, out_specs=[anyspec] * n,
        out_shape=[jax.ShapeDtypeStruct((8,) + tuple(b), x.dtype) for b, x in zip(blk, xs)],
        scratch_shapes=[pltpu.SemaphoreType.DMA((7 * n,)), pltpu.SemaphoreType.DMA((7 * n,)),
                        pltpu.SemaphoreType.DMA((n,))],
    )(*xs)
    return list(res)


def _pair_swap(xs, name):
    n = len(xs)

    def body(*refs):
        ins, outs = refs[:n], refs[n:2 * n]
        send_sems, recv_sems = refs[2 * n:]
        x, y, c = _coords()
        cps = [pltpu.make_async_remote_copy(src_ref=ins[a].at[1 - c], dst_ref=outs[a], send_sem=send_sems.at[a],
                                            recv_sem=recv_sems.at[a], device_id=(x, y, 1 - c), device_id_type=MESH)
               for a in range(n)]
        for cp in cps:
            cp.start()
        for cp in cps:
            cp.wait()

    anyspec = pl.BlockSpec(memory_space=pl.ANY)
    res = pl.pallas_call(
        body, name=name, in_specs=[anyspec] * n, out_specs=[anyspec] * n,
        out_shape=[jax.ShapeDtypeStruct(x.shape[1:], x.dtype) for x in xs],
        scratch_shapes=[pltpu.SemaphoreType.DMA((n,)), pltpu.SemaphoreType.DMA((n,))],
    )(*xs)
    return list(res)


def _adamw_math(w, g, m, v):
    m = ADAM_B1 * m + (1.0 - ADAM_B1) * g
    v = ADAM_B2 * v + (1.0 - ADAM_B2) * (g * g)
    m_hat = m / (1.0 - ADAM_B1 ** ADAM_STEP)
    v_hat = v / (1.0 - ADAM_B2 ** ADAM_STEP)
    delta = -ADAM_LR * (m_hat / (jnp.sqrt(v_hat) + ADAM_EPS) + ADAM_WD * w)
    return delta, m, v


def _adamw(parts, w, m, v, name, tm=128):
    npart, R, C = parts.shape
    tm = min(tm, R)

    def body(p_ref, w_ref, m_ref, v_ref, g_out, d_out, m_out, v_out):
        g = p_ref[0]
        for i in range(1, npart):
            g = g + p_ref[i]
        d, mm, vv = _adamw_math(w_ref[...], g, m_ref[...], v_ref[...])
        g_out[...] = g
        d_out[...] = d
        m_out[...] = mm
        v_out[...] = vv

    spec = pl.BlockSpec((tm, C), lambda i: (i, 0))
    return pl.pallas_call(
        body, name=name, grid=(R // tm,),
        in_specs=[pl.BlockSpec((npart, tm, C), lambda i: (0, i, 0)), spec, spec, spec], out_specs=[spec] * 4,
        out_shape=[jax.ShapeDtypeStruct((R, C), F32)] * 4,
        compiler_params=_cparams(("parallel",)),
    )(parts, w, m, v)


def _sum_parts(parts, name):
    npart, R, C = parts.shape

    def body(p_ref, o_ref):
        g = p_ref[0]
        for i in range(1, npart):
            g = g + p_ref[i]
        o_ref[...] = g

    return pl.pallas_call(body, name=name, out_shape=jax.ShapeDtypeStruct((R, C), F32))(parts)


def _mod_fwd(c_all, w_ada, b_sh, name):
    def body(c_ref, w_ref, b_ref, o_ref):
        o_ref[...] = _dot(_silu(c_ref[...]), w_ref[...], "nn") + b_ref[...]

    return pl.pallas_call(body, name=name, out_shape=jax.ShapeDtypeStruct((c_all.shape[0], w_ada.shape[1]), F32),
                          compiler_params=pltpu.CompilerParams(vmem_limit_bytes=VMEM_LIMIT))(c_all, w_ada, b_sh)


def _mod_wgrad(c_all, dmod_sh, name):
    def body(c_ref, d_ref, o_ref):
        o_ref[...] = _dot(_silu(c_ref[...]), d_ref[...], "tn")

    return pl.pallas_call(body, name=name, out_shape=jax.ShapeDtypeStruct((c_all.shape[1], dmod_sh.shape[1]), F32),
                          compiler_params=pltpu.CompilerParams(vmem_limit_bytes=VMEM_LIMIT))(c_all, dmod_sh)


def _pad_lanes(v):
    return jnp.pad(v, ((0, 0), (0, (-v.shape[1]) % LANES)))


def kernel(x, c, norm1_w, norm2_w, w_ada, b_ada, w_in, conv_w, conv_b, dt_bias, a_log, d_skip, ssd_norm_w, q_norm_w, k_norm_w, attn_norm_w, w_out, w_ff1, w_ff2, loss_target, m_norm1_w, m_norm2_w, m_w_ada, m_b_ada, m_w_in, m_conv_w, m_conv_b, m_dt_bias, m_a_log, m_d_skip, m_ssd_norm_w, m_q_norm_w, m_k_norm_w, m_attn_norm_w, m_w_out, m_w_ff1, m_w_ff2, v_norm1_w, v_norm2_w, v_w_ada, v_b_ada, v_w_in, v_conv_w, v_conv_b, v_dt_bias, v_a_log, v_d_skip, v_ssd_norm_w, v_q_norm_w, v_k_norm_w, v_attn_norm_w, v_w_out, v_w_ff1, v_w_ff2):
    xi, yi, ci = _coords()
    chip = 2 * xi + yi
    dev = 2 * chip + ci
    xs, tgt = x[0], loss_target[0]
    S, D = xs.shape
    DI, AD = NH_SSD * HD, NH_ATT * HD
    CC = DI + 2 * NG * NSTATE
    PW = DI + CC + 3 * AD + LANES
    DFF = w_ff1.shape[2] * 4
    MIX = DI + AD
    o_xbc, o_q, o_k, o_v, o_dt = DI, DI + CC, DI + CC + AD, DI + CC + 2 * AD, DI + CC + 3 * AD

    def half_rows(w):
        r = w.shape[0] // 2
        return lax.dynamic_slice_in_dim(w, ci * r, r, 0).astype(BF16)

    c_all, conv_w_all = _exchange8([c, conv_w[0]], False, "gather_c_conv_w")
    c_all = c_all.reshape(8, D)
    c_all = jnp.pad(c_all, ((0, 8), (0, 0)))
    nmod = w_ada.shape[2]
    b_sh = lax.dynamic_slice_in_dim(b_ada, chip * nmod, nmod, 1)
    mod_sh = _mod_fwd(c_all, w_ada[0], b_sh, "mod_fwd")
    mod_all = _exchange8([mod_sh[:8]], False, "gather_mod")[0]
    mod_me = lax.dynamic_index_in_dim(mod_all[0::2], dev, 1, keepdims=False).reshape(1, 4 * nmod)
    shift1, scale1, gate1, shift2, scale2, gate2 = [mod_me[:, i * D:(i + 1) * D] for i in range(6)]

    g_in, g_out, g_ff1, g_ff2 = _exchange8([half_rows(w_in[0]), half_rows(w_out[0]), half_rows(w_ff1[0]),
                                            half_rows(w_ff2[0])], False, "gather_weights")
    wsh = w_in.shape[2]
    w_in_f = g_in.reshape(4, D, wsh).transpose(1, 0, 2).reshape(D, 4 * wsh)
    n_zx = DI + CC
    w_proj = jnp.concatenate([w_in_f[:, :n_zx], w_in_f[:, n_zx + NH_SSD:], w_in_f[:, n_zx:n_zx + NH_SSD],
                              jnp.zeros((D, LANES - NH_SSD), BF16)], axis=1)
    w_out_f = g_out.reshape(MIX, D)
    w_out_a, w_out_b = w_out_f[:DI], w_out_f[DI:]
    w_ff1_f = g_ff1.reshape(4, D, DFF // 4).transpose(1, 0, 2).reshape(D, DFF)
    w_ff2_f = g_ff2.reshape(DFF, D)

    dtb, alog, dsk = _pad_lanes(dt_bias), _pad_lanes(a_log), _pad_lanes(d_skip)
    qw2 = jnp.concatenate([q_norm_w, q_norm_w], axis=1)
    kw2 = jnp.concatenate([k_norm_w, k_norm_w], axis=1)
    conv_w_f = conv_w_all[0::2].transpose(1, 0, 2).reshape(KCONV, CC)

    h1 = _rows("norm1", lambda r, k: ([_normmod(r[0], *k)], []), [(xs, 0, D)], [norm1_w, scale1, shift1],
               [(D, BF16)], [], S)[0]
    proj = _matmul(h1, w_proj, "nn", F32, "in_proj", tn=896)
    xbc = _conv_fwd(proj, o_xbc, CC, conv_w_f, conv_b, "conv_fwd")
    y_ssd, hsave = _ssd_fwd(xbc, proj, o_dt // LANES, dtb, alog, dsk, ssd_norm_w, "ssd_fwd")

    def qk_call(name, col0, w2, scale):
        def body(t_ref, w_ref, o_ref):
            o_ref[...] = _headnorm(t_ref[...], w_ref[...], scale)
        return pl.pallas_call(
            body, name=name, grid=(AD // LANES,),
            in_specs=[pl.BlockSpec((S, LANES), lambda j: (0, j + col0 // LANES)),
                      pl.BlockSpec((1, LANES), lambda j: (0, 0))],
            out_specs=pl.BlockSpec((S, LANES), lambda j: (0, j)),
            out_shape=jax.ShapeDtypeStruct((S, AD), F32), compiler_params=_cparams(("parallel",)),
        )(proj, w2)

    qn = qk_call("q_norm", o_q, qw2, HD ** -0.5)
    kn = qk_call("k_norm", o_k, kw2, 1.0)
    o_att, lse = _attn_fwd(qn, kn, proj, o_v // LANES, "attn_fwd")
    y_att = _rows("attn_out_norm", lambda r, k: ([_rmsw(r[0], k[0])], []), [(o_att, 0, AD)], [attn_norm_w],
                  [(AD, BF16)], [], S)[0]
    mix_a = _matmul(y_ssd, w_out_a, "nn", F32, "out_proj_a")
    mix = _matmul(y_att, w_out_b, "nn", F32, "out_proj_b", epilogue=lambda r, e: r + e, extras=(mix_a,))
    x2, h2 = _rows("resid_norm2", lambda r, k: (list(_resid_normmod(r[0], r[1], *k)), []), [(xs, 0, D), (mix, 0, D)],
                   [gate1, norm2_w, scale2, shift2], [(D, F32), (D, BF16)], [], S)
    u = _matmul(h2, w_ff1_f, "nn", F32, "ff1")
    relu2 = lambda t: jnp.square(jnp.maximum(t, 0.0))
    ff = _matmul(u, w_ff2_f, "nn", F32, "ff2", a_fn=relu2)

    def loss_fn(r, k):
        x2_, ff_, t_ = r
        err = x2_ + k[0] * ff_ - t_
        dy_ = err * (1.0 / D)
        ls = jnp.sum(jnp.sum(0.5 * err * err, axis=1, keepdims=True), axis=0, keepdims=True) * (1.0 / D)
        return [dy_, dy_ * k[0]], [ls, jnp.sum(dy_ * ff_, axis=0, keepdims=True)]

    dy, dff, loss_p, dgate2 = _rows("loss", loss_fn, [(x2, 0, D), (ff, 0, D), (tgt, 0, D)], [gate2],
                                    [(D, F32), (D, BF16)], [(1, 1), (1, D)], S)
    du = _matmul(dff, w_ff2_f, "nt", BF16, "ff2_dx", epilogue=lambda r, e: r * (2.0 * jnp.maximum(e, 0.0)), extras=(u,))
    gw_ff2 = _matmul(u, dff, "tn", F32, "ff2_dw", a_fn=relu2)
    gw_ff1 = _matmul(h2, du, "tn", F32, "ff1_dw")
    dh2 = _matmul(du, w_ff1_f, "nt", F32, "ff1_dx")

    def resid_bwd(r, k):
        x_, mix_, dx2a, dh2_ = r
        _, vjp = jax.vjp(_resid_normmod, x_, mix_, *k)
        dx, dmix_, dg, dnw, dsc, dsh = vjp((dx2a, dh2_))
        return [dx, dmix_], [dg, dnw, dsc, dsh]

    dx2, dmix, dgate1, g_norm2, dscale2, dshift2 = _rows(
        "resid_norm2_bwd", resid_bwd, [(xs, 0, D), (mix, 0, D), (dy, 0, D), (dh2, 0, D)],
        [gate1, norm2_w, scale2, shift2], [(D, F32), (D, BF16)], [(1, D)] * 4, S)
    dy_ssd = _matmul(dmix, w_out_a, "nt", F32, "out_proj_dx_a")
    dy_att = _matmul(dmix, w_out_b, "nt", F32, "out_proj_dx_b")
    gw_out = jnp.concatenate([_matmul(y_ssd, dmix, "tn", F32, "out_proj_dw_a"),
                              _matmul(y_att, dmix, "tn", F32, "out_proj_dw_b")], axis=0)

    def attn_norm_bwd(r, k):
        o_, dyo = r
        _, vjp = jax.vjp(_rmsw, o_, k[0])
        do_, dw_ = vjp(dyo)
        lo = _lane_mask()
        dd_blocks = []
        for b in range(AD // LANES):
            t = (do_ * o_)[:, b * LANES:(b + 1) * LANES]
            s0 = jnp.sum(jnp.where(lo, t, 0.0), axis=1, keepdims=True)
            s1 = jnp.sum(jnp.where(lo, 0.0, t), axis=1, keepdims=True)
            dd_blocks.append(jnp.where(lo, s0, s1))
        return [do_, jnp.concatenate(dd_blocks, axis=1)], [dw_]

    do_att, dd_att, g_attn_norm = _rows("attn_norm_bwd", attn_norm_bwd, [(o_att, 0, AD), (dy_att, 0, AD)],
                                        [attn_norm_w], [(AD, F32), (AD, F32)], [(1, AD)], S)
    dq_n = _attn_dq(qn, kn, proj, o_v // LANES, do_att, lse, dd_att, "attn_dq")
    dk_n, dv = _attn_dkv(qn, kn, proj, o_v // LANES, do_att, lse, dd_att, "attn_dkv")

    def qk_bwd_call(name, col0, w2, scale, g):
        def body(t_ref, w_ref, g_ref, o_ref, dw_ref):
            @pl.when(pl.program_id(0) == 0)
            def _():
                dw_ref[...] = jnp.zeros_like(dw_ref)
            _, vjp = jax.vjp(lambda t, w: _headnorm(t, w, scale), t_ref[...], w_ref[...])
            dt_, dw_ = vjp(g_ref[...])
            o_ref[...] = dt_.astype(BF16)
            dw_ref[...] += dw_
        blk = pl.BlockSpec((S, LANES), lambda j: (0, j))
        return pl.pallas_call(
            body, name=name, grid=(AD // LANES,),
            in_specs=[pl.BlockSpec((S, LANES), lambda j: (0, j + col0 // LANES)),
                      pl.BlockSpec((1, LANES), lambda j: (0, 0)), blk],
            out_specs=[blk, pl.BlockSpec((1, LANES), lambda j: (0, 0))],
            out_shape=[jax.ShapeDtypeStruct((S, AD), BF16), jax.ShapeDtypeStruct((1, LANES), F32)],
            compiler_params=_cparams(("arbitrary",)),
        )(proj, w2, g)

    dq, g_qw2 = qk_bwd_call("q_norm_bwd", o_q, qw2, HD ** -0.5, dq_n)
    dk, g_kw2 = qk_bwd_call("k_norm_bwd", o_k, kw2, 1.0, dk_n)
    g_q_norm = g_qw2[:, :HD] + g_qw2[:, HD:]
    g_k_norm = g_kw2[:, :HD] + g_kw2[:, HD:]

    dxbc, dz, ddtr, g_dtb, g_alog, g_dsk, g_ssd_norm = _ssd_bwd(
        xbc, proj, o_dt // LANES, dtb, alog, dsk, ssd_norm_w, hsave, dy_ssd, "ssd_bwd")
    dxbc_pre, g_conv_w, g_conv_b = _conv_bwd(proj, o_xbc, CC, conv_w_f, conv_b, dxbc, "conv_bwd")
    dproj = jnp.concatenate([dz.astype(BF16), dxbc_pre.astype(BF16), dq, dk, dv.astype(BF16), ddtr.astype(BF16)], axis=1)
    gw_proj = _matmul(h1, dproj, "tn", F32, "in_proj_dw", tn=896)
    dh1 = _matmul(dproj, w_proj, "nt", F32, "in_proj_dx", tk=896)

    def norm1_bwd(r, k):
        x_, dh_, dres = r
        _, vjp = jax.vjp(_normmod, x_, *k)
        dx, dnw, dsc, dsh = vjp(dh_)
        return [dx + dres], [dnw, dsc, dsh]

    grad_x, g_norm1, dscale1, dshift1 = _rows("norm1_bwd", norm1_bwd, [(xs, 0, D), (dh1, 0, D), (dx2, 0, D)],
                                              [norm1_w, scale1, shift1], [(D, F32)], [(1, D)] * 3, S)
    gw_in = jnp.concatenate([gw_proj[:, :n_zx], gw_proj[:, o_dt:o_dt + NH_SSD], gw_proj[:, n_zx:o_dt]], axis=1)
    dmod = jnp.concatenate([dshift1, dscale1, dgate1, dshift2, dscale2, dgate2], axis=1)

    small = [g_norm1, g_norm2, dmod, g_conv_b, g_dtb, g_alog, g_dsk, g_ssd_norm, _pad_lanes(g_q_norm),
             _pad_lanes(g_k_norm), g_attn_norm, g_conv_w.reshape(1, KCONV * CC)]
    sizes = [t.shape[1] for t in small]
    packed = jnp.concatenate(small, axis=1)
    nrow = -(-packed.shape[1] // LANES // 8) * 8
    packed = jnp.pad(packed, ((0, 0), (0, nrow * LANES - packed.shape[1]))).reshape(nrow, LANES)
    packed_all = _exchange8([packed], False, "gather_small_grads")[0]
    tot = _sum_parts(packed_all, "sum_small_grads").reshape(1, nrow * LANES)
    offs = [sum(sizes[:i]) for i in range(len(sizes))]
    (g_norm1, g_norm2, g_b_ada, g_conv_b, g_dtb, g_alog, g_dsk, g_ssd_norm, g_q_norm, g_k_norm, g_attn_norm,
     g_conv_w) = [tot[:, o:o + n] for o, n in zip(offs, sizes)]
    g_dtb, g_alog, g_dsk = g_dtb[:, :NH_SSD], g_alog[:, :NH_SSD], g_dsk[:, :NH_SSD]
    g_q_norm, g_k_norm = g_q_norm[:, :HD], g_k_norm[:, :HD]
    ccs = CC // 4
    g_conv_w = lax.dynamic_slice_in_dim(g_conv_w.reshape(KCONV, CC), chip * ccs, ccs, 1)

    dmod_all = packed_all.reshape(8, nrow * LANES)[:, offs[2]:offs[2] + 6 * D]
    dmod_sh = jnp.pad(lax.dynamic_slice_in_dim(dmod_all, chip * nmod, nmod, 1), ((0, 8), (0, 0)))
    gw_ada = _mod_wgrad(c_all, dmod_sh, "mod_wgrad")

    def by_half_cols(g):
        r, c4 = g.shape
        return g.reshape(2, r // 2, 4, c4 // 4).transpose(0, 2, 1, 3)

    def by_half_rows(g):
        r4, cdim = g.shape
        return g.reshape(4, 2, r4 // 8, cdim).transpose(1, 0, 2, 3)

    big = [by_half_cols(gw_in), by_half_rows(gw_out), by_half_cols(gw_ff1), by_half_rows(gw_ff2)]
    theirs = _pair_swap(big, "pair_swap_grads")
    pair_sums = []
    for i, (g2, t) in enumerate(zip(big, theirs)):
        mine = lax.dynamic_index_in_dim(g2, ci, 0, keepdims=False)
        _, r2, cdim = t.shape
        sm = _rows("pair_add_%d" % i, lambda r, k: ([r[0] + r[1]], []),
                   [(mine.reshape(4 * r2, cdim), 0, cdim), (t.reshape(4 * r2, cdim), 0, cdim)], [], [(cdim, F32)], [],
                   4 * r2, tm=128)[0]
        pair_sums.append(sm.reshape(4, r2, cdim))
    scattered = _exchange8(pair_sums, True, "scatter_grads")
    parts = [s.reshape(4, 2 * s.shape[1], s.shape[2]) for s in scattered]

    res_in = _adamw(parts[0], w_in[0], m_w_in[0], v_w_in[0], "adamw_w_in")
    res_out = _adamw(parts[1], w_out[0], m_w_out[0], v_w_out[0], "adamw_w_out")
    res_ff1 = _adamw(parts[2], w_ff1[0], m_w_ff1[0], v_w_ff1[0], "adamw_w_ff1")
    res_ff2 = _adamw(parts[3], w_ff2[0], m_w_ff2[0], v_w_ff2[0], "adamw_w_ff2")
    res_ada = _adamw(gw_ada[None], w_ada[0], m_w_ada[0], v_w_ada[0], "adamw_w_ada")

    small_names = ["norm1_w", "norm2_w", "b_ada", "conv_w", "conv_b", "dt_bias", "a_log", "d_skip", "ssd_norm_w",
                   "q_norm_w", "k_norm_w", "attn_norm_w"]
    small_g = dict(norm1_w=g_norm1, norm2_w=g_norm2, b_ada=g_b_ada, conv_w=g_conv_w.reshape(1, KCONV * ccs),
                   conv_b=g_conv_b, dt_bias=g_dtb, a_log=g_alog, d_skip=g_dsk, ssd_norm_w=g_ssd_norm, q_norm_w=g_q_norm,
                   k_norm_w=g_k_norm, attn_norm_w=g_attn_norm)
    small_w = dict(norm1_w=(norm1_w, m_norm1_w, v_norm1_w), norm2_w=(norm2_w, m_norm2_w, v_norm2_w),
                   b_ada=(b_ada, m_b_ada, v_b_ada),
                   conv_w=tuple(t.reshape(1, KCONV * ccs) for t in (conv_w, m_conv_w, v_conv_w)),
                   conv_b=(conv_b, m_conv_b, v_conv_b), dt_bias=(dt_bias, m_dt_bias, v_dt_bias),
                   a_log=(a_log, m_a_log, v_a_log), d_skip=(d_skip, m_d_skip, v_d_skip),
                   ssd_norm_w=(ssd_norm_w, m_ssd_norm_w, v_ssd_norm_w), q_norm_w=(q_norm_w, m_q_norm_w, v_q_norm_w),
                   k_norm_w=(k_norm_w, m_k_norm_w, v_k_norm_w), attn_norm_w=(attn_norm_w, m_attn_norm_w, v_attn_norm_w))
    ssz = [_pad_lanes(small_g[n]).shape[1] for n in small_names]
    soff = [sum(ssz[:i]) for i in range(len(ssz))]
    srow = -(-sum(ssz) // LANES // 8) * 8

    def pack(ts, fill):
        t = jnp.concatenate([jnp.pad(t, ((0, 0), (0, (-t.shape[1]) % LANES)), constant_values=fill) for t in ts], axis=1)
        return jnp.pad(t, ((0, 0), (0, srow * LANES - t.shape[1])), constant_values=fill).reshape(srow, LANES)

    sg = pack([small_g[n] for n in small_names], 0.0)
    sw = pack([small_w[n][0] for n in small_names], 0.0)
    sm_ = pack([small_w[n][1] for n in small_names], 0.0)
    sv = pack([small_w[n][2] for n in small_names], 1.0)
    _, s_delta, s_m, s_v = _adamw(sg[None], sw, sm_, sv, "adamw_small", tm=srow)

    def unpack(t, n):
        i = small_names.index(n)
        return t.reshape(1, srow * LANES)[:, soff[i]:soff[i] + small_g[n].shape[1]].reshape(small_w[n][0].shape)

    loss = lax.psum(loss_p[0, 0], ("x", "y", "c"))
    big_res = dict(w_ada=res_ada, w_in=res_in, w_out=res_out, w_ff1=res_ff1, w_ff2=res_ff2)
    order = ["norm1_w", "norm2_w", "w_ada", "b_ada", "w_in", "conv_w", "conv_b", "dt_bias", "a_log", "d_skip",
             "ssd_norm_w", "q_norm_w", "k_norm_w", "attn_norm_w", "w_out", "w_ff1", "w_ff2"]
    grads, deltas, new_m, new_v = [], [], [], []
    for n in order:
        if n in big_res:
            g_, d_, m_, v_ = [t[None] for t in big_res[n]]
        else:
            g_ = small_g[n].reshape(small_w[n][0].shape)
            d_, m_, v_ = unpack(s_delta, n), unpack(s_m, n), unpack(s_v, n)
            if n == "conv_w":
                g_, d_, m_, v_ = [t.reshape(conv_w.shape) for t in (g_, d_, m_, v_)]
        grads.append(g_)
        deltas.append(d_)
        new_m.append(m_)
        new_v.append(v_)
    return (loss, grad_x[None], *grads, *deltas, *new_m, *new_v)
```

```python
import functools

import jax
import jax.numpy as jnp
from jax import lax
from jax.experimental import pallas as pl
from jax.experimental.pallas import tpu as pltpu

F32, BF16 = jnp.float32, jnp.bfloat16
EPS = 1e-6
HD = 64
NH_SSD = 16
NG = 4
NSTATE = 128
KCONV = 4
CHUNK = 128
NH_ATT = 16
PATTERNS = ((128, 1), (512, 4), (2048, 16))
ABLK = 128
QTILE = 128
LANES = 128
ADAM_LR, ADAM_B1, ADAM_B2, ADAM_EPS, ADAM_WD, ADAM_STEP = 0.001, 0.9, 0.999, 1e-08, 0.01, 10
VMEM_LIMIT = 56 * 1024 * 1024
MESH = pl.DeviceIdType.MESH
NEG = -1e30

_DN = {"nn": (((1,), (0,)), ((), ())), "nt": (((1,), (1,)), ((), ())), "tn": (((0,), (0,)), ((), ()))}


def _cparams(sem):
    return pltpu.CompilerParams(dimension_semantics=sem, vmem_limit_bytes=VMEM_LIMIT)


def _tile(n, cap):
    if n % LANES or n <= LANES:
        return n
    best = LANES
    for t in range(LANES, min(n, cap) + 1, LANES):
        if n % t == 0:
            best = t
    return best


def _silu(x):
    return x / (1.0 + jnp.exp(-x))


def _softplus(x):
    return jnp.maximum(x, 0.0) + jnp.log(1.0 + jnp.exp(-jnp.abs(x)))


def _dot(a, b, dims):
    return lax.dot_general(a.astype(BF16), b.astype(BF16), _DN[dims], preferred_element_type=F32)


def _matmul(a, b, dims, out_dtype, name, a_fn=None, epilogue=None, extras=(), tm=1024, tn=1024, tk=1024):
    if dims == "nn":
        (M, K), (_, N) = a.shape, b.shape
    elif dims == "nt":
        (M, K), (N, _) = a.shape, b.shape
    else:
        (K, M), (_, N) = a.shape, b.shape
    tm, tn, tk = _tile(M, tm), _tile(N, tn), _tile(K, tk)
    nk = K // tk
    ne = len(extras)

    def body(a_ref, b_ref, *rest):
        e_refs, o_ref = rest[:ne], rest[ne]
        av = a_ref[...]
        if a_fn is not None:
            av = a_fn(av)
        part = _dot(av, b_ref[...], dims)

        def finish(r):
            if epilogue is not None:
                r = epilogue(r, *[e[...] for e in e_refs])
            o_ref[...] = r.astype(out_dtype)

        if nk == 1:
            finish(part)
            return
        acc = rest[ne + 1]
        k = pl.program_id(2)

        @pl.when(k == 0)
        def _():
            acc[...] = part

        @pl.when(k > 0)
        def _():
            acc[...] += part

        @pl.when(k == nk - 1)
        def _():
            finish(acc[...])

    a_spec = pl.BlockSpec((tk, tm), lambda i, j, k: (k, i)) if dims == "tn" else pl.BlockSpec((tm, tk), lambda i, j, k: (i, k))
    b_spec = pl.BlockSpec((tn, tk), lambda i, j, k: (j, k)) if dims == "nt" else pl.BlockSpec((tk, tn), lambda i, j, k: (k, j))
    o_spec = pl.BlockSpec((tm, tn), lambda i, j, k: (i, j))
    return pl.pallas_call(
        body, name=name, grid=(M // tm, N // tn, nk),
        in_specs=[a_spec, b_spec] + [o_spec] * ne, out_specs=o_spec,
        out_shape=jax.ShapeDtypeStruct((M, N), out_dtype),
        scratch_shapes=[pltpu.VMEM((tm, tn), F32)] if nk > 1 else [],
        compiler_params=_cparams(("parallel", "parallel", "arbitrary")),
    )(a, b, *extras)


def _rows(name, fn, rows, consts, outs, accs, n_rows, tm=256):
    tm = min(tm, n_rows)
    nr, nc, no, na = len(rows), len(consts), len(outs), len(accs)

    def body(*refs):
        r_refs, c_refs = refs[:nr], refs[nr:nr + nc]
        o_refs, a_refs = refs[nr + nc:nr + nc + no], refs[nr + nc + no:]
        o_vals, a_vals = fn([r[...] for r in r_refs], [c[...] for c in c_refs])
        for ref, val in zip(o_refs, o_vals):
            ref[...] = val.astype(ref.dtype)
        if na:
            @pl.when(pl.program_id(0) == 0)
            def _():
                for ref in a_refs:
                    ref[...] = jnp.zeros_like(ref)
            for ref, val in zip(a_refs, a_vals):
                ref[...] += val

    in_specs = [pl.BlockSpec((tm, w), lambda i, cb=cb: (i, cb)) for (_, cb, w) in rows]
    in_specs += [pl.BlockSpec(cst.shape, lambda i, nd=cst.ndim: (0,) * nd) for cst in consts]
    out_specs = [pl.BlockSpec((tm, w), lambda i: (i, 0)) for (w, _) in outs]
    out_specs += [pl.BlockSpec(s, lambda i: (0, 0)) for s in accs]
    out_shape = [jax.ShapeDtypeStruct((n_rows, w), dt) for (w, dt) in outs]
    out_shape += [jax.ShapeDtypeStruct(s, F32) for s in accs]
    res = pl.pallas_call(
        body, name=name, grid=(n_rows // tm,), in_specs=in_specs, out_specs=out_specs, out_shape=out_shape,
        compiler_params=_cparams(("arbitrary",)),
    )(*[r[0] for r in rows], *consts)
    return res


def _normmod(x, nw, sc, sh):
    r = lax.rsqrt(jnp.mean(x * x, axis=-1, keepdims=True) + EPS)
    return (x * r) * nw * (1.0 + sc) + sh


def _resid_normmod(x, mix, g, nw, sc, sh):
    x2 = x + g * mix
    return x2, _normmod(x2, nw, sc, sh)


def _rmsw(o, w):
    return o * lax.rsqrt(jnp.mean(o * o, axis=-1, keepdims=True) + EPS) * w


def _lane_mask():
    return lax.broadcasted_iota(jnp.int32, (1, LANES), 1) < HD


def _headnorm(t, w, scale):
    lo = _lane_mask()
    t2 = t * t
    s0 = jnp.sum(jnp.where(lo, t2, 0.0), axis=1, keepdims=True)
    s1 = jnp.sum(jnp.where(lo, 0.0, t2), axis=1, keepdims=True)
    ms = jnp.where(lo, s0, s1) * (1.0 / HD)
    return t * lax.rsqrt(ms + EPS) * w * scale


def _conv_cols(n_ch):
    return _tile(n_ch, 256)


def _conv_fwd(proj, col0, n_ch, conv_w, conv_b, name):
    S = proj.shape[0]
    tc = _conv_cols(n_ch)

    def body(u_ref, w_ref, b_ref, o_ref):
        u = u_ref[...]
        row = lax.broadcasted_iota(jnp.int32, u.shape, 0)
        acc = b_ref[...] + w_ref[KCONV - 1:KCONV, :] * u
        for i in range(KCONV - 1):
            sh = KCONV - 1 - i
            acc = acc + w_ref[i:i + 1, :] * jnp.where(row >= sh, pltpu.roll(u, sh, 0), 0.0)
        o_ref[...] = _silu(acc)

    return pl.pallas_call(
        body, name=name, grid=(n_ch // tc,),
        in_specs=[pl.BlockSpec((S, tc), lambda j: (0, j + col0 // tc)),
                  pl.BlockSpec((KCONV, tc), lambda j: (0, j)), pl.BlockSpec((1, tc), lambda j: (0, j))],
        out_specs=pl.BlockSpec((S, tc), lambda j: (0, j)),
        out_shape=jax.ShapeDtypeStruct((S, n_ch), F32),
        compiler_params=_cparams(("parallel",)),
    )(proj, conv_w, conv_b)


def _conv_bwd(proj, col0, n_ch, conv_w, conv_b, dxbc, name):
    S = proj.shape[0]
    tc = _conv_cols(n_ch)

    def body(u_ref, w_ref, b_ref, g_ref, du_ref, dw_ref, db_ref):
        u = u_ref[...]
        row = lax.broadcasted_iota(jnp.int32, u.shape, 0)
        shifted = [jnp.where(row >= s, pltpu.roll(u, s, 0), 0.0) for s in range(1, KCONV)]
        acc = b_ref[...] + w_ref[KCONV - 1:KCONV, :] * u
        for i in range(KCONV - 1):
            acc = acc + w_ref[i:i + 1, :] * shifted[KCONV - 2 - i]
        sig = 1.0 / (1.0 + jnp.exp(-acc))
        dacc = g_ref[...] * (sig * (1.0 + acc * (1.0 - sig)))
        db_ref[...] = jnp.sum(dacc, axis=0, keepdims=True)
        du = w_ref[KCONV - 1:KCONV, :] * dacc
        dw_ref[KCONV - 1:KCONV, :] = jnp.sum(dacc * u, axis=0, keepdims=True)
        for i in range(KCONV - 1):
            sh = KCONV - 1 - i
            dw_ref[i:i + 1, :] = jnp.sum(dacc * shifted[sh - 1], axis=0, keepdims=True)
            du = du + w_ref[i:i + 1, :] * jnp.where(row < S - sh, pltpu.roll(dacc, S - sh, 0), 0.0)
        du_ref[...] = du

    return pl.pallas_call(
        body, name=name, grid=(n_ch // tc,),
        in_specs=[pl.BlockSpec((S, tc), lambda j: (0, j + col0 // tc)),
                  pl.BlockSpec((KCONV, tc), lambda j: (0, j)), pl.BlockSpec((1, tc), lambda j: (0, j)),
                  pl.BlockSpec((S, tc), lambda j: (0, j))],
        out_specs=[pl.BlockSpec((S, tc), lambda j: (0, j)), pl.BlockSpec((KCONV, tc), lambda j: (0, j)),
                   pl.BlockSpec((1, tc), lambda j: (0, j))],
        out_shape=[jax.ShapeDtypeStruct((S, n_ch), F32), jax.ShapeDtypeStruct((KCONV, n_ch), F32),
                   jax.ShapeDtypeStruct((1, n_ch), F32)],
        compiler_params=_cparams(("parallel",)),
    )(proj, conv_w, conv_b, dxbc)


@functools.partial(jax.custom_vjp, nondiff_argnums=(2,))
def _mm(a, b, dims):
    return _dot(a, b, dims)


def _mm_fwd(a, b, dims):
    return _dot(a, b, dims), (a, b)


def _mm_bwd(dims, res, g):
    a, b = res
    if dims == "nn":
        return _dot(g, b, "nt"), _dot(a, g, "tn")
    if dims == "nt":
        return _dot(g, b, "nn"), _dot(g, a, "tn")
    return _dot(b, g, "nt"), _dot(a, g, "nn")


_mm.defvjp(_mm_fwd, _mm_bwd)


def _tri_dot(x, upper):
    n = x.shape[0]
    r = lax.broadcasted_iota(jnp.int32, (n, n), 0)
    c = lax.broadcasted_iota(jnp.int32, (n, n), 1)
    t = jnp.where((r <= c) if upper else (r >= c), 1.0, 0.0)
    return lax.dot_general(t, x, _DN["nn"], precision=lax.Precision.HIGHEST, preferred_element_type=F32)


@jax.custom_vjp
def _cumsum_rows(x):
    return _tri_dot(x, False)


_cumsum_rows.defvjp(lambda x: (_tri_dot(x, False), None), lambda _, g: (_tri_dot(g, True),))


def _ssd_chunk(xs_p, bm_g, cm_g, dtr, z_p, dtb, alog, dsk, nw_p, h_p):
    L = dtr.shape[0]
    n_pairs = len(xs_p)
    ppg = n_pairs // len(bm_g)
    lane = lax.broadcasted_iota(jnp.int32, (1, LANES), 1)
    sub = lax.broadcasted_iota(jnp.int32, (LANES, 1), 0)
    lo = lane < HD
    row_l = lax.broadcasted_iota(jnp.int32, (L, 1), 0)
    tri = lax.broadcasted_iota(jnp.int32, (L, L), 0) >= lax.broadcasted_iota(jnp.int32, (L, L), 1)

    dt = _softplus(dtr + dtb)
    acs = _cumsum_rows(dt * (-jnp.exp(alog)))
    acs_t = acs.T
    a_last = jnp.sum(jnp.where(row_l == L - 1, acs, 0.0), axis=0, keepdims=True)
    e_acs = jnp.exp(acs)
    dec = jnp.exp(a_last - acs)
    cdec = jnp.exp(a_last)

    def colv(m, h):
        return jnp.sum(jnp.where(lane == h, m, 0.0), axis=1, keepdims=True)

    def rowv(mt, h):
        return jnp.sum(jnp.where(sub == h, mt, 0.0), axis=0, keepdims=True)

    def pair(m, h0):
        return jnp.where(lo, colv(m, h0), colv(m, h0 + 1))

    ys, hs = [], []
    cb = None
    for p in range(n_pairs):
        g, h0 = p // ppg, 2 * p
        bmat, cmat = bm_g[g], cm_g[g]
        if p % ppg == 0:
            cb = _mm(cmat, bmat, "nt")
        x = xs_p[p]
        xdt = x * pair(dt, h0)
        yd = []
        for h in (h0, h0 + 1):
            seg = colv(acs, h) - rowv(acs_t, h)
            lm = jnp.where(tri, jnp.exp(jnp.where(tri, seg, 0.0)), 0.0)
            yd.append(_mm(cb * lm, xdt, "nn"))
        y = jnp.where(lo, yd[0], yd[1])
        y = y + _mm(cmat, h_p[p], "nt") * pair(e_acs, h0)
        st = _mm(xdt * pair(dec, h0), bmat, "tn")
        cd_col = jnp.where(sub < HD, colv(cdec, h0), colv(cdec, h0 + 1))
        hs.append(h_p[p] * cd_col + st)
        ys.append(y + pair(dsk, h0) * x)

    y2 = [ys[p] * _silu(z_p[p]) for p in range(n_pairs)]
    outs = []
    for g in range(len(bm_g)):
        ps = range(g * ppg, (g + 1) * ppg)
        ss = sum(jnp.sum(y2[p] * y2[p], axis=1, keepdims=True) for p in ps)
        rs = lax.rsqrt(ss * (1.0 / (ppg * LANES)) + EPS)
        outs += [y2[p] * rs * nw_p[p] for p in ps]
    return outs, hs


def _ssd_slices(xbc_ref, z_ref, nw_ref, di):
    n_pairs = di // LANES
    xs_p = [xbc_ref[:, p * LANES:(p + 1) * LANES] for p in range(n_pairs)]
    bm_g = [xbc_ref[:, di + g * NSTATE:di + (g + 1) * NSTATE] for g in range(NG)]
    cm_g = [xbc_ref[:, di + (NG + g) * NSTATE:di + (NG + g + 1) * NSTATE] for g in range(NG)]
    z_p = [z_ref[:, p * LANES:(p + 1) * LANES] for p in range(n_pairs)]
    nw_p = [nw_ref[:, p * LANES:(p + 1) * LANES] for p in range(n_pairs)]
    return xs_p, bm_g, cm_g, z_p, nw_p


def _ssd_fwd(xbc, proj, dt_cb, dtb, alog, dsk, nw, name):
    S, cc = xbc.shape
    di = NH_SSD * HD
    n_pairs = di // LANES
    nchunk = S // CHUNK

    def body(xbc_ref, z_ref, dtr_ref, dtb_ref, alog_ref, dsk_ref, nw_ref, y_ref, hs_ref, h_scr):
        @pl.when(pl.program_id(0) == 0)
        def _():
            h_scr[...] = jnp.zeros_like(h_scr)

        xs_p, bm_g, cm_g, z_p, nw_p = _ssd_slices(xbc_ref, z_ref, nw_ref, di)
        h_p = [h_scr[p * LANES:(p + 1) * LANES, :] for p in range(n_pairs)]
        hs_ref[...] = h_scr[...]
        outs, hs = _ssd_chunk(xs_p, bm_g, cm_g, dtr_ref[...], z_p, dtb_ref[...], alog_ref[...], dsk_ref[...], nw_p, h_p)
        for p in range(n_pairs):
            y_ref[:, p * LANES:(p + 1) * LANES] = outs[p].astype(y_ref.dtype)
            h_scr[p * LANES:(p + 1) * LANES, :] = hs[p]

    vec = pl.BlockSpec((1, LANES), lambda c: (0, 0))
    return pl.pallas_call(
        body, name=name, grid=(nchunk,),
        in_specs=[pl.BlockSpec((CHUNK, cc), lambda c: (c, 0)), pl.BlockSpec((CHUNK, di), lambda c: (c, 0)),
                  pl.BlockSpec((CHUNK, LANES), lambda c: (c, dt_cb)), vec, vec, vec,
                  pl.BlockSpec((1, di), lambda c: (0, 0))],
        out_specs=[pl.BlockSpec((CHUNK, di), lambda c: (c, 0)), pl.BlockSpec((None, di, NSTATE), lambda c: (c, 0, 0))],
        out_shape=[jax.ShapeDtypeStruct((S, di), BF16), jax.ShapeDtypeStruct((nchunk, di, NSTATE), F32)],
        scratch_shapes=[pltpu.VMEM((di, NSTATE), F32)],
        compiler_params=_cparams(("arbitrary",)),
    )(xbc, proj, proj, dtb, alog, dsk, nw)


def _ssd_bwd(xbc, proj, dt_cb, dtb, alog, dsk, nw, hsave, dy, name):
    S, cc = xbc.shape
    di = NH_SSD * HD
    n_pairs = di // LANES
    nchunk = S // CHUNK

    def body(xbc_ref, z_ref, dtr_ref, dtb_ref, alog_ref, dsk_ref, nw_ref, hs_ref, dy_ref,
             dxbc_ref, dz_ref, ddtr_ref, ddtb_ref, dalog_ref, ddsk_ref, dnw_ref, dh_scr):
        @pl.when(pl.program_id(0) == 0)
        def _():
            dh_scr[...] = jnp.zeros_like(dh_scr)
            ddtb_ref[...] = jnp.zeros_like(ddtb_ref)
            dalog_ref[...] = jnp.zeros_like(dalog_ref)
            ddsk_ref[...] = jnp.zeros_like(ddsk_ref)
            dnw_ref[...] = jnp.zeros_like(dnw_ref)

        xs_p, bm_g, cm_g, z_p, nw_p = _ssd_slices(xbc_ref, z_ref, nw_ref, di)
        h_p = [hs_ref[p * LANES:(p + 1) * LANES, :] for p in range(n_pairs)]
        dy_p = [dy_ref[:, p * LANES:(p + 1) * LANES].astype(F32) for p in range(n_pairs)]
        dh_p = [dh_scr[p * LANES:(p + 1) * LANES, :] for p in range(n_pairs)]
        _, vjp = jax.vjp(_ssd_chunk, xs_p, bm_g, cm_g, dtr_ref[...], z_p, dtb_ref[...], alog_ref[...], dsk_ref[...],
                         nw_p, h_p)
        dxs, dbm, dcm, ddtr, dz, ddtb, dalog, ddsk, dnw, dh = vjp((dy_p, dh_p))
        for p in range(n_pairs):
            sl = slice(p * LANES, (p + 1) * LANES)
            dxbc_ref[:, sl] = dxs[p]
            dz_ref[:, sl] = dz[p]
            dnw_ref[:, sl] += dnw[p]
            dh_scr[sl, :] = dh[p]
        for g in range(NG):
            dxbc_ref[:, di + g * NSTATE:di + (g + 1) * NSTATE] = dbm[g]
            dxbc_ref[:, di + (NG + g) * NSTATE:di + (NG + g + 1) * NSTATE] = dcm[g]
        ddtr_ref[...] = ddtr
        ddtb_ref[...] += ddtb
        dalog_ref[...] += dalog
        ddsk_ref[...] += ddsk

    last = nchunk - 1
    vec = pl.BlockSpec((1, LANES), lambda c: (0, 0))
    return pl.pallas_call(
        body, name=name, grid=(nchunk,),
        in_specs=[pl.BlockSpec((CHUNK, cc), lambda c: (last - c, 0)), pl.BlockSpec((CHUNK, di), lambda c: (last - c, 0)),
                  pl.BlockSpec((CHUNK, LANES), lambda c: (last - c, dt_cb)), vec, vec, vec,
                  pl.BlockSpec((1, di), lambda c: (0, 0)),
                  pl.BlockSpec((None, di, NSTATE), lambda c: (last - c, 0, 0)),
                  pl.BlockSpec((CHUNK, di), lambda c: (last - c, 0))],
        out_specs=[pl.BlockSpec((CHUNK, cc), lambda c: (last - c, 0)), pl.BlockSpec((CHUNK, di), lambda c: (last - c, 0)),
                   pl.BlockSpec((CHUNK, LANES), lambda c: (last - c, 0)), vec, vec, vec,
                   pl.BlockSpec((1, di), lambda c: (0, 0))],
        out_shape=[jax.ShapeDtypeStruct((S, cc), F32), jax.ShapeDtypeStruct((S, di), F32),
                   jax.ShapeDtypeStruct((S, LANES), F32), jax.ShapeDtypeStruct((1, LANES), F32),
                   jax.ShapeDtypeStruct((1, LANES), F32), jax.ShapeDtypeStruct((1, LANES), F32),
                   jax.ShapeDtypeStruct((1, di), F32)],
        scratch_shapes=[pltpu.VMEM((di, NSTATE), F32)],
        compiler_params=_cparams(("arbitrary",)),
    )(xbc, proj, proj, dtb, alog, dsk, nw, hsave, dy)


def _band_masks(rows_q, rows_k):
    qi = lax.broadcasted_iota(jnp.int32, (rows_q, rows_k), 0)
    ki = lax.broadcasted_iota(jnp.int32, (rows_q, rows_k), 1)
    return qi, ki


def _class_chunks(n_rows, d):
    per_class = n_rows // d
    ch = min(per_class, 256)
    out = []
    for r in range(d):
        for c0 in range(0, per_class, ch):
            tok = pl.ds(c0, ch) if d == 1 else pl.ds(r + d * c0, ch, stride=d)
            out.append((tok, pl.ds(r * per_class + c0, ch)))
    return out


def _to_class_order(src_ref, dst_ref, n_rows, d):
    for tok, cls in _class_chunks(n_rows, d):
        dst_ref[cls, :] = src_ref[tok, :].astype(dst_ref.dtype)


def _blk_rows(t):
    return pl.ds(pl.multiple_of(t * ABLK, ABLK), ABLK)


def _head_lanes(msk, t, t_rolled):
    return jnp.where(msk, t, t_rolled)


def _zero_unless(msk, t):
    return jnp.where(msk, t, jnp.zeros_like(t))


def _attn_fwd(qn, kn, proj, v_cb, name):
    S, ad = qn.shape
    nb = S // ABLK
    nbr = len(PATTERNS)

    def body(q_ref, k_ref, v_ref, o_ref, lse_ref, qc, kc, vc, ob, mb, lb, m_s, l_s):
        lo = _lane_mask()
        qi, ki = _band_masks(ABLK, 2 * ABLK)
        band, in_cur, prev_ok = ki <= qi + ABLK, ki >= ABLK, ki >= qi
        for bi, (_, d) in enumerate(PATTERNS):
            nbc = S // d // ABLK
            first, last = bi == 0, bi == nbr - 1
            for src, dst in ((q_ref, qc), (k_ref, kc), (v_ref, vc)):
                _to_class_order(src, dst, S, d)
            o_dst, m_dst, l_dst = (o_ref, m_s, l_s) if first else (ob, mb, lb)

            def blk(t, carry, nbc=nbc, o_dst=o_dst, m_dst=m_dst, l_dst=l_dst):
                rows, prow = _blk_rows(t), _blk_rows(jnp.maximum(t - 1, 0))
                has_prev = (t % nbc) != 0
                kk = jnp.concatenate([kc[prow, :], kc[rows, :]], axis=0)
                vv = jnp.concatenate([vc[prow, :], vc[rows, :]], axis=0)
                for u in range(ABLK // QTILE):
                    sub = pl.ds(pl.multiple_of(t * ABLK + u * QTILE, QTILE), QTILE)
                    sl = slice(u * QTILE, (u + 1) * QTILE)
                    valid = band[sl] & (in_cur[sl] | (prev_ok[sl] & has_prev))
                    qv = qc[sub, :]
                    os_, ms_, ls_ = [], [], []
                    for msk in (lo, jnp.logical_not(lo)):
                        s = jnp.where(valid, _dot(_zero_unless(msk, qv), kk, "nt"), NEG)
                        m = jnp.max(s, axis=1, keepdims=True)
                        p = jnp.exp(s - m)
                        os_.append(_dot(p, vv, "nn"))
                        ms_.append(m)
                        ls_.append(jnp.sum(p, axis=1, keepdims=True))
                    o_dst[sub, :] = jnp.where(lo, os_[0], os_[1])
                    m_dst[sub, :] = jnp.where(lo, ms_[0], ms_[1])
                    l_dst[sub, :] = jnp.where(lo, ls_[0], ls_[1])
                return carry

            lax.fori_loop(0, nb, blk, 0, unroll=8)
            if first:
                continue
            for tok, cls in _class_chunks(S, d):
                m_old, m_b = m_s[tok, :], mb[cls, :]
                m_new = jnp.maximum(m_old, m_b)
                a, b = jnp.exp(m_old - m_new), jnp.exp(m_b - m_new)
                l_new = a * l_s[tok, :] + b * lb[cls, :]
                o_new = a * o_ref[tok, :] + b * ob[cls, :]
                if last:
                    o_ref[tok, :] = o_new / l_new
                    lse_ref[tok, :] = m_new + jnp.log(l_new)
                else:
                    o_ref[tok, :] = o_new
                    m_s[tok, :] = m_new
                    l_s[tok, :] = l_new

    col = pl.BlockSpec((S, LANES), lambda h: (0, h))
    return pl.pallas_call(
        body, name=name, grid=(ad // LANES,),
        in_specs=[col, col, pl.BlockSpec((S, LANES), lambda h: (0, h + v_cb))], out_specs=[col, col],
        out_shape=[jax.ShapeDtypeStruct((S, ad), F32), jax.ShapeDtypeStruct((S, ad), F32)],
        scratch_shapes=[pltpu.VMEM((S, LANES), BF16)] * 3 + [pltpu.VMEM((S, LANES), F32)] * 5,
        compiler_params=_cparams(("parallel",)),
    )(qn, kn, proj)


def _attn_dq(qn, kn, proj, v_cb, do, lse, dd, name):
    S, ad = qn.shape
    nb = S // ABLK

    def body(q_ref, k_ref, v_ref, do_ref, lse_ref, dd_ref, dq_ref, qc, kc, vc, doc, lsec, ddc, dqc):
        lo = _lane_mask()
        qi, ki = _band_masks(ABLK, 2 * ABLK)
        band, in_cur, prev_ok = ki <= qi + ABLK, ki >= ABLK, ki >= qi
        for bi, (_, d) in enumerate(PATTERNS):
            nbc = S // d // ABLK
            first = bi == 0
            for src, dst in ((q_ref, qc), (k_ref, kc), (v_ref, vc), (do_ref, doc), (lse_ref, lsec), (dd_ref, ddc)):
                _to_class_order(src, dst, S, d)
            dq_dst = dq_ref if first else dqc

            def blk(t, carry, nbc=nbc, dq_dst=dq_dst):
                rows, prow = _blk_rows(t), _blk_rows(jnp.maximum(t - 1, 0))
                valid = band & (in_cur | (prev_ok & ((t % nbc) != 0)))
                qv, dov, lse_b, dd_b = qc[rows, :], doc[rows, :], lsec[rows, :], ddc[rows, :]
                kk = jnp.concatenate([kc[prow, :], kc[rows, :]], axis=0)
                vv = jnp.concatenate([vc[prow, :], vc[rows, :]], axis=0)
                lse_r, dd_r = pltpu.roll(lse_b, HD, 1), pltpu.roll(dd_b, HD, 1)
                dqs = []
                for msk in (lo, jnp.logical_not(lo)):
                    lse_h, dd_h = _head_lanes(msk, lse_b, lse_r), _head_lanes(msk, dd_b, dd_r)
                    s = jnp.where(valid, _dot(_zero_unless(msk, qv), kk, "nt"), NEG)
                    p = jnp.exp(s - jnp.concatenate([lse_h, lse_h], axis=1))
                    dp = _dot(_zero_unless(msk, dov), vv, "nt")
                    ds = p * (dp - jnp.concatenate([dd_h, dd_h], axis=1))
                    dqs.append(_dot(ds, kk, "nn"))
                dq_dst[rows, :] = jnp.where(lo, dqs[0], dqs[1])
                return carry

            lax.fori_loop(0, nb, blk, 0, unroll=4)
            if not first:
                for tok, cls in _class_chunks(S, d):
                    dq_ref[tok, :] = dq_ref[tok, :] + dqc[cls, :]

    col = pl.BlockSpec((S, LANES), lambda h: (0, h))
    return pl.pallas_call(
        body, name=name, grid=(ad // LANES,),
        in_specs=[col, col, pl.BlockSpec((S, LANES), lambda h: (0, h + v_cb)), col, col, col], out_specs=col,
        out_shape=jax.ShapeDtypeStruct((S, ad), F32),
        scratch_shapes=[pltpu.VMEM((S, LANES), BF16)] * 4 + [pltpu.VMEM((S, LANES), F32)] * 3,
        compiler_params=_cparams(("parallel",)),
    )(qn, kn, proj, do, lse, dd)


def _attn_dkv(qn, kn, proj, v_cb, do, lse, dd, name):
    S, ad = qn.shape
    nb = S // ABLK

    def body(q_ref, k_ref, v_ref, do_ref, lse_ref, dd_ref, dk_ref, dv_ref, qc, kc, vc, doc, lsec, ddc, dkc, dvc):
        lo = _lane_mask()
        qi, ki = _band_masks(2 * ABLK, ABLK)
        cur_ok, nxt_ok = (qi < ABLK) & (ki <= qi), (qi >= ABLK) & (ki >= qi - ABLK)
        for bi, (_, d) in enumerate(PATTERNS):
            nbc = S // d // ABLK
            first = bi == 0
            for src, dst in ((q_ref, qc), (k_ref, kc), (v_ref, vc), (do_ref, doc), (lse_ref, lsec), (dd_ref, ddc)):
                _to_class_order(src, dst, S, d)
            dk_dst, dv_dst = (dk_ref, dv_ref) if first else (dkc, dvc)

            def blk(t, carry, nbc=nbc, dk_dst=dk_dst, dv_dst=dv_dst):
                rows, nrow = _blk_rows(t), _blk_rows(jnp.minimum(t + 1, nb - 1))
                valid = cur_ok | (nxt_ok & (((t + 1) % nbc) != 0))
                kv, vv = kc[rows, :], vc[rows, :]
                qq = jnp.concatenate([qc[rows, :], qc[nrow, :]], axis=0)
                do2 = jnp.concatenate([doc[rows, :], doc[nrow, :]], axis=0)
                lse2 = jnp.concatenate([lsec[rows, :], lsec[nrow, :]], axis=0)
                dd2 = jnp.concatenate([ddc[rows, :], ddc[nrow, :]], axis=0)
                lse_r, dd_r = pltpu.roll(lse2, HD, 1), pltpu.roll(dd2, HD, 1)
                dk = jnp.zeros((ABLK, LANES), F32)
                dv = jnp.zeros((ABLK, LANES), F32)
                for msk in (lo, jnp.logical_not(lo)):
                    qh, doh = _zero_unless(msk, qq), _zero_unless(msk, do2)
                    s = jnp.where(valid, _dot(qh, kv, "nt"), NEG)
                    p = jnp.exp(s - _head_lanes(msk, lse2, lse_r))
                    dv = dv + _dot(p, doh, "tn")
                    ds = p * (_dot(doh, vv, "nt") - _head_lanes(msk, dd2, dd_r))
                    dk = dk + _dot(ds, qh, "tn")
                dk_dst[rows, :] = dk
                dv_dst[rows, :] = dv
                return carry

            lax.fori_loop(0, nb, blk, 0, unroll=4)
            if not first:
                for tok, cls in _class_chunks(S, d):
                    dk_ref[tok, :] = dk_ref[tok, :] + dkc[cls, :]
                    dv_ref[tok, :] = dv_ref[tok, :] + dvc[cls, :]

    col = pl.BlockSpec((S, LANES), lambda h: (0, h))
    return pl.pallas_call(
        body, name=name, grid=(ad // LANES,),
        in_specs=[col, col, pl.BlockSpec((S, LANES), lambda h: (0, h + v_cb)), col, col, col], out_specs=[col, col],
        out_shape=[jax.ShapeDtypeStruct((S, ad), F32), jax.ShapeDtypeStruct((S, ad), F32)],
        scratch_shapes=[pltpu.VMEM((S, LANES), BF16)] * 4 + [pltpu.VMEM((S, LANES), F32)] * 4,
        compiler_params=_cparams(("parallel",)),
    )(qn, kn, proj, do, lse, dd)


def _coords():
    return lax.axis_index("x"), lax.axis_index("y"), lax.axis_index("c")


def _exchange8(xs, per_dest, name):
    n = len(xs)
    blk = [x.shape[1:] if per_dest else x.shape for x in xs]

    def body(*refs):
        ins, outs = refs[:n], refs[n:2 * n]
        send_sems, recv_sems, local_sems = refs[2 * n:]
        x, y, c = _coords()
        sibling = (x, y, 1 - c)
        chips = [(1 - x, y), (x, 1 - y), (1 - x, 1 - y)]
        first, passed, mine = [], [], []
        for a in range(n):
            def src_for(cx, cy, a=a):
                return ins[a].at[2 * cx + cy] if per_dest else ins[a]

            def slot(px, py, pc, a=a):
                return outs[a].at[4 * px + 2 * py + pc]

            def copy(k, src, dst, to, a=a):
                return pltpu.make_async_remote_copy(src_ref=src, dst_ref=dst, send_sem=send_sems.at[7 * a + k],
                                                    recv_sem=recv_sems.at[7 * a + k], device_id=to, device_id_type=MESH)

            m = pltpu.make_async_copy(src_for(x, y), slot(x, y, c), local_sems.at[a])
            m.start()
            mine.append(m)
            cps = [copy(0, src_for(x, y), slot(x, y, c), sibling)]
            cps += [copy(1 + j, src_for(*chip), slot(x, y, c), (*chip, c)) for j, chip in enumerate(chips)]
            for cp in cps:
                cp.start()
            first += cps
        for a in range(n):
            def slot(px, py, pc, a=a):
                return outs[a].at[4 * px + 2 * py + pc]

            def copy(k, src, dst, to, a=a):
                return pltpu.make_async_remote_copy(src_ref=src, dst_ref=dst, send_sem=send_sems.at[7 * a + k],
                                                    recv_sem=recv_sems.at[7 * a + k], device_id=to, device_id_type=MESH)

            for j, chip in enumerate(chips):
                copy(1 + j, slot(*chip, c), slot(*chip, c), (*chip, c)).wait_recv()
                fw = copy(4 + j, slot(*chip, c), slot(*chip, c), sibling)
                fw.start()
                passed.append(fw)
        for a in range(n):
            def slot(px, py, pc, a=a):
                return outs[a].at[4 * px + 2 * py + pc]

            def copy(k, src, dst, to, a=a):
                return pltpu.make_async_remote_copy(src_ref=src, dst_ref=dst, send_sem=send_sems.at[7 * a + k],
                                                    recv_sem=recv_sems.at[7 * a + k], device_id=to, device_id_type=MESH)

            copy(0, slot(x, y, 1 - c), slot(x, y, 1 - c), sibling).wait_recv()
            for j, chip in enumerate(chips):
                copy(4 + j, slot(*chip, 1 - c), slot(*chip, 1 - c), sibling).wait_recv()
        for cp in first + passed:
            cp.wait_send()
        for m in mine:
            m.wait()

    anyspec = pl.BlockSpec(memory_space=pl.ANY)
    res = pl.pallas_call(
        body, name=name, in_specs=[anyspec] * n, out_specs=[anyspec] * n,
        out_shape=[jax.ShapeDtypeStruct((8,) + tuple(b), x.dtype) for b, x in zip(blk, xs)],
        scratch_shapes=[pltpu.SemaphoreType.DMA((7 * n,)), pltpu.SemaphoreType.DMA((7 * n,)),
                        pltpu.SemaphoreType.DMA((n,))],
    )(*xs)
    return list(res)


def _pair_swap(xs, name):
    n = len(xs)

    def body(*refs):
        ins, outs = refs[:n], refs[n:2 * n]
        send_sems, recv_sems = refs[2 * n:]
        x, y, c = _coords()
        cps = [pltpu.make_async_remote_copy(src_ref=ins[a].at[1 - c], dst_ref=outs[a], send_sem=send_sems.at[a],
                                            recv_sem=recv_sems.at[a], device_id=(x, y, 1 - c), device_id_type=MESH)
               for a in range(n)]
        for cp in cps:
            cp.start()
        for cp in cps:
            cp.wait()

    anyspec = pl.BlockSpec(memory_space=pl.ANY)
    res = pl.pallas_call(
        body, name=name, in_specs=[anyspec] * n, out_specs=[anyspec] * n,
        out_shape=[jax.ShapeDtypeStruct(x.shape[1:], x.dtype) for x in xs],
        scratch_shapes=[pltpu.SemaphoreType.DMA((n,)), pltpu.SemaphoreType.DMA((n,))],
    )(*xs)
    return list(res)


def _pair_add(g2, theirs, half, name, tm=256):
    _, n, cdim = g2.shape
    tm = min(tm, n)

    def body(h_ref, a_ref, b_ref, o_ref):
        o_ref[...] = (a_ref[...] + b_ref[...]).astype(o_ref.dtype)

    grid_spec = pltpu.PrefetchScalarGridSpec(
        num_scalar_prefetch=1, grid=(n // tm,),
        in_specs=[pl.BlockSpec((None, tm, cdim), lambda i, h: (h[0], i, 0)), pl.BlockSpec((tm, cdim), lambda i, h: (i, 0))],
        out_specs=pl.BlockSpec((tm, cdim), lambda i, h: (i, 0)))
    return pl.pallas_call(body, name=name, grid_spec=grid_spec, out_shape=jax.ShapeDtypeStruct((n, cdim), BF16),
                          compiler_params=_cparams(("parallel",)))(half.reshape(1).astype(jnp.int32), g2, theirs)


def _adamw_math(w, g, m, v):
    m = ADAM_B1 * m + (1.0 - ADAM_B1) * g
    v = ADAM_B2 * v + (1.0 - ADAM_B2) * (g * g)
    m_hat = m / (1.0 - ADAM_B1 ** ADAM_STEP)
    v_hat = v / (1.0 - ADAM_B2 ** ADAM_STEP)
    delta = -ADAM_LR * (m_hat / (jnp.sqrt(v_hat) + ADAM_EPS) + ADAM_WD * w)
    return delta, m, v


def _adamw(parts, w, m, v, name, tm=128):
    npart, R, C = parts.shape
    tm = min(tm, R)

    def body(p_ref, w_ref, m_ref, v_ref, g_out, d_out, m_out, v_out):
        g = p_ref[0].astype(F32)
        for i in range(1, npart):
            g = g + p_ref[i].astype(F32)
        d, mm, vv = _adamw_math(w_ref[...], g, m_ref[...], v_ref[...])
        g_out[...] = g
        d_out[...] = d
        m_out[...] = mm
        v_out[...] = vv

    spec = pl.BlockSpec((tm, C), lambda i: (i, 0))
    return pl.pallas_call(
        body, name=name, grid=(R // tm,),
        in_specs=[pl.BlockSpec((npart, tm, C), lambda i: (0, i, 0)), spec, spec, spec], out_specs=[spec] * 4,
        out_shape=[jax.ShapeDtypeStruct((R, C), F32)] * 4,
        compiler_params=_cparams(("parallel",)),
    )(parts, w, m, v)


def _sum_parts(parts, name):
    npart, R, C = parts.shape

    def body(p_ref, o_ref):
        g = p_ref[0]
        for i in range(1, npart):
            g = g + p_ref[i]
        o_ref[...] = g

    return pl.pallas_call(body, name=name, out_shape=jax.ShapeDtypeStruct((R, C), F32))(parts)


def _mod_fwd(c_all, w_ada, b_sh, name):
    def body(c_ref, w_ref, b_ref, o_ref):
        o_ref[...] = _dot(_silu(c_ref[...]), w_ref[...], "nn") + b_ref[...]

    return pl.pallas_call(body, name=name, out_shape=jax.ShapeDtypeStruct((c_all.shape[0], w_ada.shape[1]), F32),
                          compiler_params=pltpu.CompilerParams(vmem_limit_bytes=VMEM_LIMIT))(c_all, w_ada, b_sh)


def _mod_wgrad(c_all, dmod_sh, name):
    def body(c_ref, d_ref, o_ref):
        o_ref[...] = _dot(_silu(c_ref[...]), d_ref[...], "tn")

    return pl.pallas_call(body, name=name, out_shape=jax.ShapeDtypeStruct((c_all.shape[1], dmod_sh.shape[1]), F32),
                          compiler_params=pltpu.CompilerParams(vmem_limit_bytes=VMEM_LIMIT))(c_all, dmod_sh)


def _pad_lanes(v):
    return jnp.pad(v, ((0, 0), (0, (-v.shape[1]) % LANES)))


def kernel(x, c, norm1_w, norm2_w, w_ada, b_ada, w_in, conv_w, conv_b, dt_bias, a_log, d_skip, ssd_norm_w, q_norm_w, k_norm_w, attn_norm_w, w_out, w_ff1, w_ff2, loss_target, m_norm1_w, m_norm2_w, m_w_ada, m_b_ada, m_w_in, m_conv_w, m_conv_b, m_dt_bias, m_a_log, m_d_skip, m_ssd_norm_w, m_q_norm_w, m_k_norm_w, m_attn_norm_w, m_w_out, m_w_ff1, m_w_ff2, v_norm1_w, v_norm2_w, v_w_ada, v_b_ada, v_w_in, v_conv_w, v_conv_b, v_dt_bias, v_a_log, v_d_skip, v_ssd_norm_w, v_q_norm_w, v_k_norm_w, v_attn_norm_w, v_w_out, v_w_ff1, v_w_ff2):
    xi, yi, ci = _coords()
    chip = 2 * xi + yi
    dev = 2 * chip + ci
    xs, tgt = x[0], loss_target[0]
    S, D = xs.shape
    DI, AD = NH_SSD * HD, NH_ATT * HD
    CC = DI + 2 * NG * NSTATE
    PW = DI + CC + 3 * AD + LANES
    DFF = w_ff1.shape[2] * 4
    MIX = DI + AD
    o_xbc, o_q, o_k, o_v, o_dt = DI, DI + CC, DI + CC + AD, DI + CC + 2 * AD, DI + CC + 3 * AD

    def half_rows(w):
        r = w.shape[0] // 2
        return lax.dynamic_slice_in_dim(w, ci * r, r, 0).astype(BF16)

    c_all, conv_w_all = _exchange8([c, conv_w[0]], False, "gather_c_conv_w")
    c_all = c_all.reshape(8, D)
    c_all = jnp.pad(c_all, ((0, 8), (0, 0)))
    nmod = w_ada.shape[2]
    b_sh = lax.dynamic_slice_in_dim(b_ada, chip * nmod, nmod, 1)
    mod_sh = _mod_fwd(c_all, w_ada[0], b_sh, "mod_fwd")
    mod_all = _exchange8([mod_sh[:8]], False, "gather_mod")[0]
    mod_me = lax.dynamic_index_in_dim(mod_all[0::2], dev, 1, keepdims=False).reshape(1, 4 * nmod)
    shift1, scale1, gate1, shift2, scale2, gate2 = [mod_me[:, i * D:(i + 1) * D] for i in range(6)]

    g_in, g_out, g_ff1, g_ff2 = _exchange8([half_rows(w_in[0]), half_rows(w_out[0]), half_rows(w_ff1[0]),
                                            half_rows(w_ff2[0])], False, "gather_weights")
    wsh = w_in.shape[2]
    w_in_f = g_in.reshape(4, D, wsh).transpose(1, 0, 2).reshape(D, 4 * wsh)
    n_zx = DI + CC
    w_proj = jnp.concatenate([w_in_f[:, :n_zx], w_in_f[:, n_zx + NH_SSD:], w_in_f[:, n_zx:n_zx + NH_SSD],
                              jnp.zeros((D, LANES - NH_SSD), BF16)], axis=1)
    w_out_f = g_out.reshape(MIX, D)
    w_out_a, w_out_b = w_out_f[:DI], w_out_f[DI:]
    w_ff1_f = g_ff1.reshape(4, D, DFF // 4).transpose(1, 0, 2).reshape(D, DFF)
    w_ff2_f = g_ff2.reshape(DFF, D)

    dtb, alog, dsk = _pad_lanes(dt_bias), _pad_lanes(a_log), _pad_lanes(d_skip)
    qw2 = jnp.concatenate([q_norm_w, q_norm_w], axis=1)
    kw2 = jnp.concatenate([k_norm_w, k_norm_w], axis=1)
    conv_w_f = conv_w_all[0::2].transpose(1, 0, 2).reshape(KCONV, CC)

    h1 = _rows("norm1", lambda r, k: ([_normmod(r[0], *k)], []), [(xs, 0, D)], [norm1_w, scale1, shift1],
               [(D, BF16)], [], S)[0]
    proj = _matmul(h1, w_proj, "nn", F32, "in_proj", tn=896)
    xbc = _conv_fwd(proj, o_xbc, CC, conv_w_f, conv_b, "conv_fwd")
    y_ssd, hsave = _ssd_fwd(xbc, proj, o_dt // LANES, dtb, alog, dsk, ssd_norm_w, "ssd_fwd")

    def qk_call(name, col0, w2, scale):
        def body(t_ref, w_ref, o_ref):
            o_ref[...] = _headnorm(t_ref[...], w_ref[...], scale)
        return pl.pallas_call(
            body, name=name, grid=(AD // LANES,),
            in_specs=[pl.BlockSpec((S, LANES), lambda j: (0, j + col0 // LANES)),
                      pl.BlockSpec((1, LANES), lambda j: (0, 0))],
            out_specs=pl.BlockSpec((S, LANES), lambda j: (0, j)),
            out_shape=jax.ShapeDtypeStruct((S, AD), F32), compiler_params=_cparams(("parallel",)),
        )(proj, w2)

    qn = qk_call("q_norm", o_q, qw2, HD ** -0.5)
    kn = qk_call("k_norm", o_k, kw2, 1.0)
    o_att, lse = _attn_fwd(qn, kn, proj, o_v // LANES, "attn_fwd")
    y_att = _rows("attn_out_norm", lambda r, k: ([_rmsw(r[0], k[0])], []), [(o_att, 0, AD)], [attn_norm_w],
                  [(AD, BF16)], [], S)[0]
    mix_a = _matmul(y_ssd, w_out_a, "nn", F32, "out_proj_a")
    mix = _matmul(y_att, w_out_b, "nn", F32, "out_proj_b", epilogue=lambda r, e: r + e, extras=(mix_a,))
    x2, h2 = _rows("resid_norm2", lambda r, k: (list(_resid_normmod(r[0], r[1], *k)), []), [(xs, 0, D), (mix, 0, D)],
                   [gate1, norm2_w, scale2, shift2], [(D, F32), (D, BF16)], [], S)
    u = _matmul(h2, w_ff1_f, "nn", F32, "ff1")
    relu2 = lambda t: jnp.square(jnp.maximum(t, 0.0))
    ff = _matmul(u, w_ff2_f, "nn", F32, "ff2", a_fn=relu2)

    def loss_fn(r, k):
        x2_, ff_, t_ = r
        err = x2_ + k[0] * ff_ - t_
        dy_ = err * (1.0 / D)
        ls = jnp.sum(jnp.sum(0.5 * err * err, axis=1, keepdims=True), axis=0, keepdims=True) * (1.0 / D)
        return [dy_, dy_ * k[0]], [ls, jnp.sum(dy_ * ff_, axis=0, keepdims=True)]

    dy, dff, loss_p, dgate2 = _rows("loss", loss_fn, [(x2, 0, D), (ff, 0, D), (tgt, 0, D)], [gate2],
                                    [(D, F32), (D, BF16)], [(1, 1), (1, D)], S)
    du = _matmul(dff, w_ff2_f, "nt", BF16, "ff2_dx", epilogue=lambda r, e: r * (2.0 * jnp.maximum(e, 0.0)), extras=(u,))
    gw_ff2 = _matmul(u, dff, "tn", F32, "ff2_dw", a_fn=relu2)
    gw_ff1 = _matmul(h2, du, "tn", F32, "ff1_dw")
    dh2 = _matmul(du, w_ff1_f, "nt", F32, "ff1_dx")

    def resid_bwd(r, k):
        x_, mix_, dx2a, dh2_ = r
        _, vjp = jax.vjp(_resid_normmod, x_, mix_, *k)
        dx, dmix_, dg, dnw, dsc, dsh = vjp((dx2a, dh2_))
        return [dx, dmix_], [dg, dnw, dsc, dsh]

    dx2, dmix, dgate1, g_norm2, dscale2, dshift2 = _rows(
        "resid_norm2_bwd", resid_bwd, [(xs, 0, D), (mix, 0, D), (dy, 0, D), (dh2, 0, D)],
        [gate1, norm2_w, scale2, shift2], [(D, F32), (D, BF16)], [(1, D)] * 4, S)
    dy_ssd = _matmul(dmix, w_out_a, "nt", F32, "out_proj_dx_a")
    dy_att = _matmul(dmix, w_out_b, "nt", F32, "out_proj_dx_b")
    gw_out = jnp.concatenate([_matmul(y_ssd, dmix, "tn", F32, "out_proj_dw_a"),
                              _matmul(y_att, dmix, "tn", F32, "out_proj_dw_b")], axis=0)

    def attn_norm_bwd(r, k):
        o_, dyo = r
        _, vjp = jax.vjp(_rmsw, o_, k[0])
        do_, dw_ = vjp(dyo)
        lo = _lane_mask()
        dd_blocks = []
        for b in range(AD // LANES):
            t = (do_ * o_)[:, b * LANES:(b + 1) * LANES]
            s0 = jnp.sum(jnp.where(lo, t, 0.0), axis=1, keepdims=True)
            s1 = jnp.sum(jnp.where(lo, 0.0, t), axis=1, keepdims=True)
            dd_blocks.append(jnp.where(lo, s0, s1))
        return [do_, jnp.concatenate(dd_blocks, axis=1)], [dw_]

    do_att, dd_att, g_attn_norm = _rows("attn_norm_bwd", attn_norm_bwd, [(o_att, 0, AD), (dy_att, 0, AD)],
                                        [attn_norm_w], [(AD, F32), (AD, F32)], [(1, AD)], S)
    dq_n = _attn_dq(qn, kn, proj, o_v // LANES, do_att, lse, dd_att, "attn_dq")
    dk_n, dv = _attn_dkv(qn, kn, proj, o_v // LANES, do_att, lse, dd_att, "attn_dkv")

    def qk_bwd_call(name, col0, w2, scale, g):
        def body(t_ref, w_ref, g_ref, o_ref, dw_ref):
            @pl.when(pl.program_id(0) == 0)
            def _():
                dw_ref[...] = jnp.zeros_like(dw_ref)
            _, vjp = jax.vjp(lambda t, w: _headnorm(t, w, scale), t_ref[...], w_ref[...])
            dt_, dw_ = vjp(g_ref[...])
            o_ref[...] = dt_.astype(BF16)
            dw_ref[...] += dw_
        blk = pl.BlockSpec((S, LANES), lambda j: (0, j))
        return pl.pallas_call(
            body, name=name, grid=(AD // LANES,),
            in_specs=[pl.BlockSpec((S, LANES), lambda j: (0, j + col0 // LANES)),
                      pl.BlockSpec((1, LANES), lambda j: (0, 0)), blk],
            out_specs=[blk, pl.BlockSpec((1, LANES), lambda j: (0, 0))],
            out_shape=[jax.ShapeDtypeStruct((S, AD), BF16), jax.ShapeDtypeStruct((1, LANES), F32)],
            compiler_params=_cparams(("arbitrary",)),
        )(proj, w2, g)

    dq, g_qw2 = qk_bwd_call("q_norm_bwd", o_q, qw2, HD ** -0.5, dq_n)
    dk, g_kw2 = qk_bwd_call("k_norm_bwd", o_k, kw2, 1.0, dk_n)
    g_q_norm = g_qw2[:, :HD] + g_qw2[:, HD:]
    g_k_norm = g_kw2[:, :HD] + g_kw2[:, HD:]

    dxbc, dz, ddtr, g_dtb, g_alog, g_dsk, g_ssd_norm = _ssd_bwd(
        xbc, proj, o_dt // LANES, dtb, alog, dsk, ssd_norm_w, hsave, dy_ssd, "ssd_bwd")
    dxbc_pre, g_conv_w, g_conv_b = _conv_bwd(proj, o_xbc, CC, conv_w_f, conv_b, dxbc, "conv_bwd")
    dproj = jnp.concatenate([dz.astype(BF16), dxbc_pre.astype(BF16), dq, dk, dv.astype(BF16), ddtr.astype(BF16)], axis=1)
    gw_proj = _matmul(h1, dproj, "tn", F32, "in_proj_dw", tn=896)
    dh1 = _matmul(dproj, w_proj, "nt", F32, "in_proj_dx", tk=896)

    def norm1_bwd(r, k):
        x_, dh_, dres = r
        _, vjp = jax.vjp(_normmod, x_, *k)
        dx, dnw, dsc, dsh = vjp(dh_)
        return [dx + dres], [dnw, dsc, dsh]

    grad_x, g_norm1, dscale1, dshift1 = _rows("norm1_bwd", norm1_bwd, [(xs, 0, D), (dh1, 0, D), (dx2, 0, D)],
                                              [norm1_w, scale1, shift1], [(D, F32)], [(1, D)] * 3, S)
    gw_in = jnp.concatenate([gw_proj[:, :n_zx], gw_proj[:, o_dt:o_dt + NH_SSD], gw_proj[:, n_zx:o_dt]], axis=1)
    dmod = jnp.concatenate([dshift1, dscale1, dgate1, dshift2, dscale2, dgate2], axis=1)

    small = [g_norm1, g_norm2, dmod, g_conv_b, g_dtb, g_alog, g_dsk, g_ssd_norm, _pad_lanes(g_q_norm),
             _pad_lanes(g_k_norm), g_attn_norm, g_conv_w.reshape(1, KCONV * CC)]
    sizes = [t.shape[1] for t in small]
    packed = jnp.concatenate(small, axis=1)
    nrow = -(-packed.shape[1] // LANES // 8) * 8
    packed = jnp.pad(packed, ((0, 0), (0, nrow * LANES - packed.shape[1]))).reshape(nrow, LANES)
    packed_all = _exchange8([packed], False, "gather_small_grads")[0]
    tot = _sum_parts(packed_all, "sum_small_grads").reshape(1, nrow * LANES)
    offs = [sum(sizes[:i]) for i in range(len(sizes))]
    (g_norm1, g_norm2, g_b_ada, g_conv_b, g_dtb, g_alog, g_dsk, g_ssd_norm, g_q_norm, g_k_norm, g_attn_norm,
     g_conv_w) = [tot[:, o:o + n] for o, n in zip(offs, sizes)]
    g_dtb, g_alog, g_dsk = g_dtb[:, :NH_SSD], g_alog[:, :NH_SSD], g_dsk[:, :NH_SSD]
    g_q_norm, g_k_norm = g_q_norm[:, :HD], g_k_norm[:, :HD]
    ccs = CC // 4
    g_conv_w = lax.dynamic_slice_in_dim(g_conv_w.reshape(KCONV, CC), chip * ccs, ccs, 1)

    dmod_all = packed_all.reshape(8, nrow * LANES)[:, offs[2]:offs[2] + 6 * D]
    dmod_sh = jnp.pad(lax.dynamic_slice_in_dim(dmod_all, chip * nmod, nmod, 1), ((0, 8), (0, 0)))
    gw_ada = _mod_wgrad(c_all, dmod_sh, "mod_wgrad")

    def by_half_cols(g):
        r, c4 = g.shape
        return g.reshape(2, r // 2, 4, c4 // 4).transpose(0, 2, 1, 3)

    def by_half_rows(g):
        r4, cdim = g.shape
        return g.reshape(4, 2, r4 // 8, cdim).transpose(1, 0, 2, 3)

    big = [by_half_cols(gw_in), by_half_rows(gw_out), by_half_cols(gw_ff1), by_half_rows(gw_ff2)]
    theirs = _pair_swap(big, "pair_swap_grads")
    pair_sums = []
    for i, (g2, t) in enumerate(zip(big, theirs)):
        _, r2, cdim = t.shape
        sm = _pair_add(g2.reshape(2, 4 * r2, cdim), t.reshape(4 * r2, cdim), ci, "pair_add_%d" % i)
        pair_sums.append(sm.reshape(4, r2, cdim))
    scattered = _exchange8(pair_sums, True, "scatter_grads")
    parts = [s.reshape(4, 2 * s.shape[1], s.shape[2]) for s in scattered]

    res_in = _adamw(parts[0], w_in[0], m_w_in[0], v_w_in[0], "adamw_w_in")
    res_out = _adamw(parts[1], w_out[0], m_w_out[0], v_w_out[0], "adamw_w_out")
    res_ff1 = _adamw(parts[2], w_ff1[0], m_w_ff1[0], v_w_ff1[0], "adamw_w_ff1")
    res_ff2 = _adamw(parts[3], w_ff2[0], m_w_ff2[0], v_w_ff2[0], "adamw_w_ff2")
    res_ada = _adamw(gw_ada[None], w_ada[0], m_w_ada[0], v_w_ada[0], "adamw_w_ada")

    small_names = ["norm1_w", "norm2_w", "b_ada", "conv_w", "conv_b", "dt_bias", "a_log", "d_skip", "ssd_norm_w",
                   "q_norm_w", "k_norm_w", "attn_norm_w"]
    small_g = dict(norm1_w=g_norm1, norm2_w=g_norm2, b_ada=g_b_ada, conv_w=g_conv_w.reshape(1, KCONV * ccs),
                   conv_b=g_conv_b, dt_bias=g_dtb, a_log=g_alog, d_skip=g_dsk, ssd_norm_w=g_ssd_norm, q_norm_w=g_q_norm,
                   k_norm_w=g_k_norm, attn_norm_w=g_attn_norm)
    small_w = dict(norm1_w=(norm1_w, m_norm1_w, v_norm1_w), norm2_w=(norm2_w, m_norm2_w, v_norm2_w),
                   b_ada=(b_ada, m_b_ada, v_b_ada),
                   conv_w=tuple(t.reshape(1, KCONV * ccs) for t in (conv_w, m_conv_w, v_conv_w)),
                   conv_b=(conv_b, m_conv_b, v_conv_b), dt_bias=(dt_bias, m_dt_bias, v_dt_bias),
                   a_log=(a_log, m_a_log, v_a_log), d_skip=(d_skip, m_d_skip, v_d_skip),
                   ssd_norm_w=(ssd_norm_w, m_ssd_norm_w, v_ssd_norm_w), q_norm_w=(q_norm_w, m_q_norm_w, v_q_norm_w),
                   k_norm_w=(k_norm_w, m_k_norm_w, v_k_norm_w), attn_norm_w=(attn_norm_w, m_attn_norm_w, v_attn_norm_w))
    ssz = [_pad_lanes(small_g[n]).shape[1] for n in small_names]
    soff = [sum(ssz[:i]) for i in range(len(ssz))]
    srow = -(-sum(ssz) // LANES // 8) * 8

    def pack(ts, fill):
        t = jnp.concatenate([jnp.pad(t, ((0, 0), (0, (-t.shape[1]) % LANES)), constant_values=fill) for t in ts], axis=1)
        return jnp.pad(t, ((0, 0), (0, srow * LANES - t.shape[1])), constant_values=fill).reshape(srow, LANES)

    sg = pack([small_g[n] for n in small_names], 0.0)
    sw = pack([small_w[n][0] for n in small_names], 0.0)
    sm_ = pack([small_w[n][1] for n in small_names], 0.0)
    sv = pack([small_w[n][2] for n in small_names], 1.0)
    _, s_delta, s_m, s_v = _adamw(sg[None], sw, sm_, sv, "adamw_small", tm=srow)

    def unpack(t, n):
        i = small_names.index(n)
        return t.reshape(1, srow * LANES)[:, soff[i]:soff[i] + small_g[n].shape[1]].reshape(small_w[n][0].shape)

    loss = lax.psum(loss_p[0, 0], ("x", "y", "c"))
    big_res = dict(w_ada=res_ada, w_in=res_in, w_out=res_out, w_ff1=res_ff1, w_ff2=res_ff2)
    order = ["norm1_w", "norm2_w", "w_ada", "b_ada", "w_in", "conv_w", "conv_b", "dt_bias", "a_log", "d_skip",
             "ssd_norm_w", "q_norm_w", "k_norm_w", "attn_norm_w", "w_out", "w_ff1", "w_ff2"]
    grads, deltas, new_m, new_v = [], [], [], []
    for n in order:
        if n in big_res:
            g_, d_, m_, v_ = [t[None] for t in big_res[n]]
        else:
            g_ = small_g[n].reshape(small_w[n][0].shape)
            d_, m_, v_ = unpack(s_delta, n), unpack(s_m, n), unpack(s_v, n)
            if n == "conv_w":
                g_, d_, m_, v_ = [t.reshape(conv_w.shape) for t in (g_, d_, m_, v_)]
        grads.append(g_)
        deltas.append(d_)
        new_m.append(m_)
        new_v.append(v_)
    return (loss, grad_x[None], *grads, *deltas, *new_m, *new_v)
```

```python
import functools

import jax
import jax.numpy as jnp
from jax import lax
from jax.experimental import pallas as pl
from jax.experimental.pallas import tpu as pltpu

F32, BF16 = jnp.float32, jnp.bfloat16
EPS = 1e-6
HD = 64
NH_SSD = 16
NG = 4
NSTATE = 128
KCONV = 4
CHUNK = 128
NH_ATT = 16
PATTERNS = ((128, 1), (512, 4), (2048, 16))
ABLK = 128
QTILE = 128
LANES = 128
ADAM_LR, ADAM_B1, ADAM_B2, ADAM_EPS, ADAM_WD, ADAM_STEP = 0.001, 0.9, 0.999, 1e-08, 0.01, 10
VMEM_LIMIT = 56 * 1024 * 1024
MESH = pl.DeviceIdType.MESH
NEG = -1e30

_DN = {"nn": (((1,), (0,)), ((), ())), "nt": (((1,), (1,)), ((), ())), "tn": (((0,), (0,)), ((), ()))}


def _cparams(sem):
    return pltpu.CompilerParams(dimension_semantics=sem, vmem_limit_bytes=VMEM_LIMIT)


def _tile(n, cap):
    if n % LANES or n <= LANES:
        return n
    best = LANES
    for t in range(LANES, min(n, cap) + 1, LANES):
        if n % t == 0:
            best = t
    return best


def _silu(x):
    return x / (1.0 + jnp.exp(-x))


def _softplus(x):
    return jnp.maximum(x, 0.0) + jnp.log(1.0 + jnp.exp(-jnp.abs(x)))


def _dot(a, b, dims):
    return lax.dot_general(a.astype(BF16), b.astype(BF16), _DN[dims], preferred_element_type=F32)


def _matmul(a, b, dims, out_dtype, name, a_fn=None, epilogue=None, extras=(), tm=1024, tn=1024, tk=1024):
    if dims == "nn":
        (M, K), (_, N) = a.shape, b.shape
    elif dims == "nt":
        (M, K), (N, _) = a.shape, b.shape
    else:
        (K, M), (_, N) = a.shape, b.shape
    tm, tn, tk = _tile(M, tm), _tile(N, tn), _tile(K, tk)
    nk = K // tk
    ne = len(extras)

    def body(a_ref, b_ref, *rest):
        e_refs, o_ref = rest[:ne], rest[ne]
        av = a_ref[...]
        if a_fn is not None:
            av = a_fn(av)
        part = _dot(av, b_ref[...], dims)

        def finish(r):
            if epilogue is not None:
                r = epilogue(r, *[e[...] for e in e_refs])
            o_ref[...] = r.astype(out_dtype)

        if nk == 1:
            finish(part)
            return
        acc = rest[ne + 1]
        k = pl.program_id(2)

        @pl.when(k == 0)
        def _():
            acc[...] = part

        @pl.when(k > 0)
        def _():
            acc[...] += part

        @pl.when(k == nk - 1)
        def _():
            finish(acc[...])

    a_spec = pl.BlockSpec((tk, tm), lambda i, j, k: (k, i)) if dims == "tn" else pl.BlockSpec((tm, tk), lambda i, j, k: (i, k))
    b_spec = pl.BlockSpec((tn, tk), lambda i, j, k: (j, k)) if dims == "nt" else pl.BlockSpec((tk, tn), lambda i, j, k: (k, j))
    o_spec = pl.BlockSpec((tm, tn), lambda i, j, k: (i, j))
    return pl.pallas_call(
        body, name=name, grid=(M // tm, N // tn, nk),
        in_specs=[a_spec, b_spec] + [o_spec] * ne, out_specs=o_spec,
        out_shape=jax.ShapeDtypeStruct((M, N), out_dtype),
        scratch_shapes=[pltpu.VMEM((tm, tn), F32)] if nk > 1 else [],
        compiler_params=_cparams(("parallel", "parallel", "arbitrary")),
    )(a, b, *extras)


def _rows(name, fn, rows, consts, outs, accs, n_rows, tm=256):
    tm = min(tm, n_rows)
    nr, nc, no, na = len(rows), len(consts), len(outs), len(accs)

    def body(*refs):
        r_refs, c_refs = refs[:nr], refs[nr:nr + nc]
        o_refs, a_refs = refs[nr + nc:nr + nc + no], refs[nr + nc + no:]
        o_vals, a_vals = fn([r[...] for r in r_refs], [c[...] for c in c_refs])
        for ref, val in zip(o_refs, o_vals):
            ref[...] = val.astype(ref.dtype)
        if na:
            @pl.when(pl.program_id(0) == 0)
            def _():
                for ref in a_refs:
                    ref[...] = jnp.zeros_like(ref)
            for ref, val in zip(a_refs, a_vals):
                ref[...] += val

    in_specs = [pl.BlockSpec((tm, w), lambda i, cb=cb: (i, cb)) for (_, cb, w) in rows]
    in_specs += [pl.BlockSpec(cst.shape, lambda i, nd=cst.ndim: (0,) * nd) for cst in consts]
    out_specs = [pl.BlockSpec((tm, w), lambda i: (i, 0)) for (w, _) in outs]
    out_specs += [pl.BlockSpec(s, lambda i: (0, 0)) for s in accs]
    out_shape = [jax.ShapeDtypeStruct((n_rows, w), dt) for (w, dt) in outs]
    out_shape += [jax.ShapeDtypeStruct(s, F32) for s in accs]
    res = pl.pallas_call(
        body, name=name, grid=(n_rows // tm,), in_specs=in_specs, out_specs=out_specs, out_shape=out_shape,
        compiler_params=_cparams(("arbitrary",)),
    )(*[r[0] for r in rows], *consts)
    return res


def _normmod(x, nw, sc, sh):
    r = lax.rsqrt(jnp.mean(x * x, axis=-1, keepdims=True) + EPS)
    return (x * r) * nw * (1.0 + sc) + sh


def _resid_normmod(x, mix, g, nw, sc, sh):
    x2 = x + g * mix
    return x2, _normmod(x2, nw, sc, sh)


def _rmsw(o, w):
    return o * lax.rsqrt(jnp.mean(o * o, axis=-1, keepdims=True) + EPS) * w


def _lane_mask():
    return lax.broadcasted_iota(jnp.int32, (1, LANES), 1) < HD


def _headnorm(t, w, scale):
    lo = _lane_mask()
    t2 = t * t
    s0 = jnp.sum(jnp.where(lo, t2, 0.0), axis=1, keepdims=True)
    s1 = jnp.sum(jnp.where(lo, 0.0, t2), axis=1, keepdims=True)
    ms = jnp.where(lo, s0, s1) * (1.0 / HD)
    return t * lax.rsqrt(ms + EPS) * w * scale


def _conv_cols(n_ch):
    return _tile(n_ch, 256)


def _conv_fwd(proj, col0, n_ch, conv_w, conv_b, name):
    S = proj.shape[0]
    tc = _conv_cols(n_ch)

    def body(u_ref, w_ref, b_ref, o_ref):
        u = u_ref[...]
        row = lax.broadcasted_iota(jnp.int32, u.shape, 0)
        acc = b_ref[...] + w_ref[KCONV - 1:KCONV, :] * u
        for i in range(KCONV - 1):
            sh = KCONV - 1 - i
            acc = acc + w_ref[i:i + 1, :] * jnp.where(row >= sh, pltpu.roll(u, sh, 0), 0.0)
        o_ref[...] = _silu(acc)

    return pl.pallas_call(
        body, name=name, grid=(n_ch // tc,),
        in_specs=[pl.BlockSpec((S, tc), lambda j: (0, j + col0 // tc)),
                  pl.BlockSpec((KCONV, tc), lambda j: (0, j)), pl.BlockSpec((1, tc), lambda j: (0, j))],
        out_specs=pl.BlockSpec((S, tc), lambda j: (0, j)),
        out_shape=jax.ShapeDtypeStruct((S, n_ch), F32),
        compiler_params=_cparams(("parallel",)),
    )(proj, conv_w, conv_b)


def _conv_bwd(proj, col0, n_ch, conv_w, conv_b, dxbc, name):
    S = proj.shape[0]
    tc = _conv_cols(n_ch)

    def body(u_ref, w_ref, b_ref, g_ref, du_ref, dw_ref, db_ref):
        u = u_ref[...]
        row = lax.broadcasted_iota(jnp.int32, u.shape, 0)
        shifted = [jnp.where(row >= s, pltpu.roll(u, s, 0), 0.0) for s in range(1, KCONV)]
        acc = b_ref[...] + w_ref[KCONV - 1:KCONV, :] * u
        for i in range(KCONV - 1):
            acc = acc + w_ref[i:i + 1, :] * shifted[KCONV - 2 - i]
        sig = 1.0 / (1.0 + jnp.exp(-acc))
        dacc = g_ref[...] * (sig * (1.0 + acc * (1.0 - sig)))
        db_ref[...] = jnp.sum(dacc, axis=0, keepdims=True)
        du = w_ref[KCONV - 1:KCONV, :] * dacc
        dw_ref[KCONV - 1:KCONV, :] = jnp.sum(dacc * u, axis=0, keepdims=True)
        for i in range(KCONV - 1):
            sh = KCONV - 1 - i
            dw_ref[i:i + 1, :] = jnp.sum(dacc * shifted[sh - 1], axis=0, keepdims=True)
            du = du + w_ref[i:i + 1, :] * jnp.where(row < S - sh, pltpu.roll(dacc, S - sh, 0), 0.0)
        du_ref[...] = du

    return pl.pallas_call(
        body, name=name, grid=(n_ch // tc,),
        in_specs=[pl.BlockSpec((S, tc), lambda j: (0, j + col0 // tc)),
                  pl.BlockSpec((KCONV, tc), lambda j: (0, j)), pl.BlockSpec((1, tc), lambda j: (0, j)),
                  pl.BlockSpec((S, tc), lambda j: (0, j))],
        out_specs=[pl.BlockSpec((S, tc), lambda j: (0, j)), pl.BlockSpec((KCONV, tc), lambda j: (0, j)),
                   pl.BlockSpec((1, tc), lambda j: (0, j))],
        out_shape=[jax.ShapeDtypeStruct((S, n_ch), F32), jax.ShapeDtypeStruct((KCONV, n_ch), F32),
                   jax.ShapeDtypeStruct((1, n_ch), F32)],
        compiler_params=_cparams(("parallel",)),
    )(proj, conv_w, conv_b, dxbc)


@functools.partial(jax.custom_vjp, nondiff_argnums=(2,))
def _mm(a, b, dims):
    return _dot(a, b, dims)


def _mm_fwd(a, b, dims):
    return _dot(a, b, dims), (a, b)


def _mm_bwd(dims, res, g):
    a, b = res
    if dims == "nn":
        return _dot(g, b, "nt"), _dot(a, g, "tn")
    if dims == "nt":
        return _dot(g, b, "nn"), _dot(g, a, "tn")
    return _dot(b, g, "nt"), _dot(a, g, "nn")


_mm.defvjp(_mm_fwd, _mm_bwd)


def _tri_dot(x, upper):
    n = x.shape[0]
    r = lax.broadcasted_iota(jnp.int32, (n, n), 0)
    c = lax.broadcasted_iota(jnp.int32, (n, n), 1)
    t = jnp.where((r <= c) if upper else (r >= c), 1.0, 0.0)
    return lax.dot_general(t, x, _DN["nn"], precision=lax.Precision.HIGHEST, preferred_element_type=F32)


@jax.custom_vjp
def _cumsum_rows(x):
    return _tri_dot(x, False)


_cumsum_rows.defvjp(lambda x: (_tri_dot(x, False), None), lambda _, g: (_tri_dot(g, True),))


def _ssd_chunk(xs_p, bm_g, cm_g, dtr, z_p, dtb, alog, dsk, nw_p, h_p):
    L = dtr.shape[0]
    n_pairs = len(xs_p)
    ppg = n_pairs // len(bm_g)
    lane = lax.broadcasted_iota(jnp.int32, (1, LANES), 1)
    sub = lax.broadcasted_iota(jnp.int32, (LANES, 1), 0)
    lo = lane < HD
    row_l = lax.broadcasted_iota(jnp.int32, (L, 1), 0)
    tri = lax.broadcasted_iota(jnp.int32, (L, L), 0) >= lax.broadcasted_iota(jnp.int32, (L, L), 1)

    dt = _softplus(dtr + dtb)
    acs = _cumsum_rows(dt * (-jnp.exp(alog)))
    acs_t = acs.T
    a_last = jnp.sum(jnp.where(row_l == L - 1, acs, 0.0), axis=0, keepdims=True)
    e_acs = jnp.exp(acs)
    dec = jnp.exp(a_last - acs)
    cdec = jnp.exp(a_last)

    def colv(m, h):
        return jnp.sum(jnp.where(lane == h, m, 0.0), axis=1, keepdims=True)

    def rowv(mt, h):
        return jnp.sum(jnp.where(sub == h, mt, 0.0), axis=0, keepdims=True)

    def pair(m, h0):
        return jnp.where(lo, colv(m, h0), colv(m, h0 + 1))

    ys, hs = [], []
    cb = None
    for p in range(n_pairs):
        g, h0 = p // ppg, 2 * p
        bmat, cmat = bm_g[g], cm_g[g]
        if p % ppg == 0:
            cb = _mm(cmat, bmat, "nt")
        x = xs_p[p]
        xdt = x * pair(dt, h0)
        yd = []
        for h in (h0, h0 + 1):
            seg = colv(acs, h) - rowv(acs_t, h)
            lm = jnp.where(tri, jnp.exp(jnp.where(tri, seg, 0.0)), 0.0)
            yd.append(_mm(cb * lm, xdt, "nn"))
        y = jnp.where(lo, yd[0], yd[1])
        y = y + _mm(cmat, h_p[p], "nt") * pair(e_acs, h0)
        st = _mm(xdt * pair(dec, h0), bmat, "tn")
        cd_col = jnp.where(sub < HD, colv(cdec, h0), colv(cdec, h0 + 1))
        hs.append(h_p[p] * cd_col + st)
        ys.append(y + pair(dsk, h0) * x)

    y2 = [ys[p] * _silu(z_p[p]) for p in range(n_pairs)]
    outs = []
    for g in range(len(bm_g)):
        ps = range(g * ppg, (g + 1) * ppg)
        ss = sum(jnp.sum(y2[p] * y2[p], axis=1, keepdims=True) for p in ps)
        rs = lax.rsqrt(ss * (1.0 / (ppg * LANES)) + EPS)
        outs += [y2[p] * rs * nw_p[p] for p in ps]
    return outs, hs


def _ssd_slices(xbc_ref, z_ref, nw_ref, di):
    n_pairs = di // LANES
    xs_p = [xbc_ref[:, p * LANES:(p + 1) * LANES] for p in range(n_pairs)]
    bm_g = [xbc_ref[:, di + g * NSTATE:di + (g + 1) * NSTATE] for g in range(NG)]
    cm_g = [xbc_ref[:, di + (NG + g) * NSTATE:di + (NG + g + 1) * NSTATE] for g in range(NG)]
    z_p = [z_ref[:, p * LANES:(p + 1) * LANES] for p in range(n_pairs)]
    nw_p = [nw_ref[:, p * LANES:(p + 1) * LANES] for p in range(n_pairs)]
    return xs_p, bm_g, cm_g, z_p, nw_p


def _ssd_fwd(xbc, proj, dt_cb, dtb, alog, dsk, nw, name):
    S, cc = xbc.shape
    di = NH_SSD * HD
    n_pairs = di // LANES
    nchunk = S // CHUNK

    def body(xbc_ref, z_ref, dtr_ref, dtb_ref, alog_ref, dsk_ref, nw_ref, y_ref, hs_ref, h_scr):
        @pl.when(pl.program_id(0) == 0)
        def _():
            h_scr[...] = jnp.zeros_like(h_scr)

        xs_p, bm_g, cm_g, z_p, nw_p = _ssd_slices(xbc_ref, z_ref, nw_ref, di)
        h_p = [h_scr[p * LANES:(p + 1) * LANES, :] for p in range(n_pairs)]
        hs_ref[...] = h_scr[...]
        outs, hs = _ssd_chunk(xs_p, bm_g, cm_g, dtr_ref[...], z_p, dtb_ref[...], alog_ref[...], dsk_ref[...], nw_p, h_p)
        for p in range(n_pairs):
            y_ref[:, p * LANES:(p + 1) * LANES] = outs[p].astype(y_ref.dtype)
            h_scr[p * LANES:(p + 1) * LANES, :] = hs[p]

    vec = pl.BlockSpec((1, LANES), lambda c: (0, 0))
    return pl.pallas_call(
        body, name=name, grid=(nchunk,),
        in_specs=[pl.BlockSpec((CHUNK, cc), lambda c: (c, 0)), pl.BlockSpec((CHUNK, di), lambda c: (c, 0)),
                  pl.BlockSpec((CHUNK, LANES), lambda c: (c, dt_cb)), vec, vec, vec,
                  pl.BlockSpec((1, di), lambda c: (0, 0))],
        out_specs=[pl.BlockSpec((CHUNK, di), lambda c: (c, 0)), pl.BlockSpec((None, di, NSTATE), lambda c: (c, 0, 0))],
        out_shape=[jax.ShapeDtypeStruct((S, di), BF16), jax.ShapeDtypeStruct((nchunk, di, NSTATE), F32)],
        scratch_shapes=[pltpu.VMEM((di, NSTATE), F32)],
        compiler_params=_cparams(("arbitrary",)),
    )(xbc, proj, proj, dtb, alog, dsk, nw)


def _ssd_bwd(xbc, proj, dt_cb, dtb, alog, dsk, nw, hsave, dy, name):
    S, cc = xbc.shape
    di = NH_SSD * HD
    n_pairs = di // LANES
    nchunk = S // CHUNK

    def body(xbc_ref, z_ref, dtr_ref, dtb_ref, alog_ref, dsk_ref, nw_ref, hs_ref, dy_ref,
             dxbc_ref, dz_ref, ddtr_ref, ddtb_ref, dalog_ref, ddsk_ref, dnw_ref, dh_scr):
        @pl.when(pl.program_id(0) == 0)
        def _():
            dh_scr[...] = jnp.zeros_like(dh_scr)
            ddtb_ref[...] = jnp.zeros_like(ddtb_ref)
            dalog_ref[...] = jnp.zeros_like(dalog_ref)
            ddsk_ref[...] = jnp.zeros_like(ddsk_ref)
            dnw_ref[...] = jnp.zeros_like(dnw_ref)

        xs_p, bm_g, cm_g, z_p, nw_p = _ssd_slices(xbc_ref, z_ref, nw_ref, di)
        h_p = [hs_ref[p * LANES:(p + 1) * LANES, :] for p in range(n_pairs)]
        dy_p = [dy_ref[:, p * LANES:(p + 1) * LANES].astype(F32) for p in range(n_pairs)]
        dh_p = [dh_scr[p * LANES:(p + 1) * LANES, :] for p in range(n_pairs)]
        _, vjp = jax.vjp(_ssd_chunk, xs_p, bm_g, cm_g, dtr_ref[...], z_p, dtb_ref[...], alog_ref[...], dsk_ref[...],
                         nw_p, h_p)
        dxs, dbm, dcm, ddtr, dz, ddtb, dalog, ddsk, dnw, dh = vjp((dy_p, dh_p))
        for p in range(n_pairs):
            sl = slice(p * LANES, (p + 1) * LANES)
            dxbc_ref[:, sl] = dxs[p]
            dz_ref[:, sl] = dz[p]
            dnw_ref[:, sl] += dnw[p]
            dh_scr[sl, :] = dh[p]
        for g in range(NG):
            dxbc_ref[:, di + g * NSTATE:di + (g + 1) * NSTATE] = dbm[g]
            dxbc_ref[:, di + (NG + g) * NSTATE:di + (NG + g + 1) * NSTATE] = dcm[g]
        ddtr_ref[...] = ddtr
        ddtb_ref[...] += ddtb
        dalog_ref[...] += dalog
        ddsk_ref[...] += ddsk

    last = nchunk - 1
    vec = pl.BlockSpec((1, LANES), lambda c: (0, 0))
    return pl.pallas_call(
        body, name=name, grid=(nchunk,),
        in_specs=[pl.BlockSpec((CHUNK, cc), lambda c: (last - c, 0)), pl.BlockSpec((CHUNK, di), lambda c: (last - c, 0)),
                  pl.BlockSpec((CHUNK, LANES), lambda c: (last - c, dt_cb)), vec, vec, vec,
                  pl.BlockSpec((1, di), lambda c: (0, 0)),
                  pl.BlockSpec((None, di, NSTATE), lambda c: (last - c, 0, 0)),
                  pl.BlockSpec((CHUNK, di), lambda c: (last - c, 0))],
        out_specs=[pl.BlockSpec((CHUNK, cc), lambda c: (last - c, 0)), pl.BlockSpec((CHUNK, di), lambda c: (last - c, 0)),
                   pl.BlockSpec((CHUNK, LANES), lambda c: (last - c, 0)), vec, vec, vec,
                   pl.BlockSpec((1, di), lambda c: (0, 0))],
        out_shape=[jax.ShapeDtypeStruct((S, cc), F32), jax.ShapeDtypeStruct((S, di), F32),
                   jax.ShapeDtypeStruct((S, LANES), F32), jax.ShapeDtypeStruct((1, LANES), F32),
                   jax.ShapeDtypeStruct((1, LANES), F32), jax.ShapeDtypeStruct((1, LANES), F32),
                   jax.ShapeDtypeStruct((1, di), F32)],
        scratch_shapes=[pltpu.VMEM((di, NSTATE), F32)],
        compiler_params=_cparams(("arbitrary",)),
    )(xbc, proj, proj, dtb, alog, dsk, nw, hsave, dy)


def _band_masks(rows_q, rows_k):
    qi = lax.broadcasted_iota(jnp.int32, (rows_q, rows_k), 0)
    ki = lax.broadcasted_iota(jnp.int32, (rows_q, rows_k), 1)
    return qi, ki


def _class_chunks(n_rows, d):
    per_class = n_rows // d
    ch = min(per_class, 256)
    out = []
    for r in range(d):
        for c0 in range(0, per_class, ch):
            tok = pl.ds(c0, ch) if d == 1 else pl.ds(r + d * c0, ch, stride=d)
            out.append((tok, pl.ds(r * per_class + c0, ch)))
    return out


def _to_class_order(src_ref, dst_ref, n_rows, d):
    for tok, cls in _class_chunks(n_rows, d):
        dst_ref[cls, :] = src_ref[tok, :].astype(dst_ref.dtype)


def _blk_rows(t):
    return pl.ds(pl.multiple_of(t * ABLK, ABLK), ABLK)


def _head_lanes(msk, t, t_rolled):
    return jnp.where(msk, t, t_rolled)


def _zero_unless(msk, t):
    return jnp.where(msk, t, jnp.zeros_like(t))


def _attn_fwd(qn, kn, proj, v_cb, name):
    S, ad = qn.shape
    nb = S // ABLK
    nbr = len(PATTERNS)

    def body(q_ref, k_ref, v_ref, o_ref, lse_ref, qc, kc, vc, ob, mb, lb, m_s, l_s):
        lo = _lane_mask()
        qi, ki = _band_masks(ABLK, 2 * ABLK)
        band, in_cur, prev_ok = ki <= qi + ABLK, ki >= ABLK, ki >= qi
        for bi, (_, d) in enumerate(PATTERNS):
            nbc = S // d // ABLK
            first, last = bi == 0, bi == nbr - 1
            qs, ks, vs = q_ref, k_ref, v_ref
            if d > 1:
                qs, ks, vs = qc, kc, vc
                for src, dst in ((q_ref, qc), (k_ref, kc), (v_ref, vc)):
                    _to_class_order(src, dst, S, d)
            o_dst, m_dst, l_dst = (o_ref, m_s, l_s) if first else (ob, mb, lb)

            def blk(t, carry, nbc=nbc, qs=qs, ks=ks, vs=vs, o_dst=o_dst, m_dst=m_dst, l_dst=l_dst):
                rows, prow = _blk_rows(t), _blk_rows(jnp.maximum(t - 1, 0))
                has_prev = (t % nbc) != 0
                kk = jnp.concatenate([ks[prow, :], ks[rows, :]], axis=0)
                vv = jnp.concatenate([vs[prow, :], vs[rows, :]], axis=0)
                for u in range(ABLK // QTILE):
                    sub = pl.ds(pl.multiple_of(t * ABLK + u * QTILE, QTILE), QTILE)
                    sl = slice(u * QTILE, (u + 1) * QTILE)
                    valid = band[sl] & (in_cur[sl] | (prev_ok[sl] & has_prev))
                    qv = qs[sub, :]
                    os_, ms_, ls_ = [], [], []
                    for msk in (lo, jnp.logical_not(lo)):
                        s = jnp.where(valid, _dot(_zero_unless(msk, qv), kk, "nt"), NEG)
                        m = jnp.max(s, axis=1, keepdims=True)
                        p = jnp.exp(s - m)
                        os_.append(_dot(p, vv, "nn"))
                        ms_.append(m)
                        ls_.append(jnp.sum(p, axis=1, keepdims=True))
                    o_dst[sub, :] = jnp.where(lo, os_[0], os_[1])
                    m_dst[sub, :] = jnp.where(lo, ms_[0], ms_[1])
                    l_dst[sub, :] = jnp.where(lo, ls_[0], ls_[1])
                return carry

            lax.fori_loop(0, nb, blk, 0, unroll=8)
            if first:
                continue
            for tok, cls in _class_chunks(S, d):
                m_old, m_b = m_s[tok, :], mb[cls, :]
                m_new = jnp.maximum(m_old, m_b)
                a, b = jnp.exp(m_old - m_new), jnp.exp(m_b - m_new)
                l_new = a * l_s[tok, :] + b * lb[cls, :]
                o_new = a * o_ref[tok, :] + b * ob[cls, :]
                if last:
                    o_ref[tok, :] = o_new / l_new
                    lse_ref[tok, :] = m_new + jnp.log(l_new)
                else:
                    o_ref[tok, :] = o_new
                    m_s[tok, :] = m_new
                    l_s[tok, :] = l_new

    col = pl.BlockSpec((S, LANES), lambda h: (0, h))
    return pl.pallas_call(
        body, name=name, grid=(ad // LANES,),
        in_specs=[col, col, pl.BlockSpec((S, LANES), lambda h: (0, h + v_cb))], out_specs=[col, col],
        out_shape=[jax.ShapeDtypeStruct((S, ad), F32), jax.ShapeDtypeStruct((S, ad), F32)],
        scratch_shapes=[pltpu.VMEM((S, LANES), BF16)] * 3 + [pltpu.VMEM((S, LANES), F32)] * 5,
        compiler_params=_cparams(("parallel",)),
    )(qn, kn, proj)


def _attn_bwd(qn, kn, proj, v_cb, do, lse, dd, name):
    S, ad = qn.shape
    nb = S // ABLK

    def body(q_ref, k_ref, v_ref, do_ref, lse_ref, dd_ref, dq_ref, dk_ref, dv_ref,
             qc, kc, vc, doc, lsec, ddc, dqc, dkc, dvc):
        lo = _lane_mask()
        qi, ki = _band_masks(ABLK, ABLK)
        cur_ok, prev_ok = ki <= qi, ki >= qi
        for bi, (_, d) in enumerate(PATTERNS):
            nbc = S // d // ABLK
            first = bi == 0
            token_order = (q_ref, k_ref, v_ref, do_ref, lse_ref, dd_ref)
            class_order = (qc, kc, vc, doc, lsec, ddc)
            if d > 1:
                for src, dst in zip(token_order, class_order):
                    _to_class_order(src, dst, S, d)
            qs, ks, vs, dos, lses, dds = class_order if d > 1 else token_order
            dq_dst, dk_dst, dv_dst = (dq_ref, dk_ref, dv_ref) if first else (dqc, dkc, dvc)
            dk_dst[...] = jnp.zeros_like(dk_dst)
            dv_dst[...] = jnp.zeros_like(dv_dst)

            def blk(t, carry, nbc=nbc, qs=qs, ks=ks, vs=vs, dos=dos, lses=lses, dds=dds,
                    dq_dst=dq_dst, dk_dst=dk_dst, dv_dst=dv_dst):
                rows, prow = _blk_rows(t), _blk_rows(jnp.maximum(t - 1, 0))
                has_prev = (t % nbc) != 0
                qv, dov, lse_b, dd_b = qs[rows, :], dos[rows, :], lses[rows, :], dds[rows, :]
                lse_r, dd_r = pltpu.roll(lse_b, HD, 1), pltpu.roll(dd_b, HD, 1)
                heads = []
                for msk in (lo, jnp.logical_not(lo)):
                    heads.append((_zero_unless(msk, qv), _zero_unless(msk, dov), _head_lanes(msk, lse_b, lse_r),
                                  _head_lanes(msk, dd_b, dd_r)))
                dqs = [None, None]
                for krows, vmask in ((rows, cur_ok), (prow, prev_ok & has_prev)):
                    kv, vv = ks[krows, :], vs[krows, :]
                    dk = jnp.zeros((ABLK, LANES), F32)
                    dv = jnp.zeros((ABLK, LANES), F32)
                    for hi, (qh, doh, lse_h, dd_h) in enumerate(heads):
                        s = jnp.where(vmask, _dot(qh, kv, "nt"), NEG)
                        p = jnp.exp(s - lse_h)
                        ds = p * (_dot(doh, vv, "nt") - dd_h)
                        dqh = _dot(ds, kv, "nn")
                        dqs[hi] = dqh if dqs[hi] is None else dqs[hi] + dqh
                        dv = dv + _dot(p, doh, "tn")
                        dk = dk + _dot(ds, qh, "tn")
                    dk_dst[krows, :] += dk
                    dv_dst[krows, :] += dv
                dq_dst[rows, :] = jnp.where(lo, dqs[0], dqs[1])
                return carry

            lax.fori_loop(0, nb, blk, 0, unroll=4)
            if not first:
                for tok, cls in _class_chunks(S, d):
                    dq_ref[tok, :] = dq_ref[tok, :] + dqc[cls, :]
                    dk_ref[tok, :] = dk_ref[tok, :] + dkc[cls, :]
                    dv_ref[tok, :] = dv_ref[tok, :] + dvc[cls, :]

    col = pl.BlockSpec((S, LANES), lambda h: (0, h))
    col1 = pl.BlockSpec((S, LANES), lambda h: (0, h), pipeline_mode=pl.Buffered(1))
    vcol1 = pl.BlockSpec((S, LANES), lambda h: (0, h + v_cb), pipeline_mode=pl.Buffered(1))
    return pl.pallas_call(
        body, name=name, grid=(ad // LANES,),
        in_specs=[col, col, vcol1, col1, col1, col1], out_specs=[col, col, col],
        out_shape=[jax.ShapeDtypeStruct((S, ad), F32)] * 3,
        scratch_shapes=[pltpu.VMEM((S, LANES), BF16)] * 4 + [pltpu.VMEM((S, LANES), F32)] * 5,
        compiler_params=_cparams(("parallel",)),
    )(qn, kn, proj, do, lse, dd)


def _coords():
    return lax.axis_index("x"), lax.axis_index("y"), lax.axis_index("c")


def _exchange8(xs, per_dest, name):
    n = len(xs)
    blk = [x.shape[1:] if per_dest else x.shape for x in xs]

    def body(*refs):
        ins, outs = refs[:n], refs[n:2 * n]
        send_sems, recv_sems, local_sems = refs[2 * n:]
        x, y, c = _coords()
        sibling = (x, y, 1 - c)
        chips = [(1 - x, y), (x, 1 - y), (1 - x, 1 - y)]
        first, passed, mine = [], [], []
        for a in range(n):
            def src_for(cx, cy, a=a):
                return ins[a].at[2 * cx + cy] if per_dest else ins[a]

            def slot(px, py, pc, a=a):
                return outs[a].at[4 * px + 2 * py + pc]

            def copy(k, src, dst, to, a=a):
                return pltpu.make_async_remote_copy(src_ref=src, dst_ref=dst, send_sem=send_sems.at[7 * a + k],
                                                    recv_sem=recv_sems.at[7 * a + k], device_id=to, device_id_type=MESH)

            m = pltpu.make_async_copy(src_for(x, y), slot(x, y, c), local_sems.at[a])
            m.start()
            mine.append(m)
            cps = [copy(0, src_for(x, y), slot(x, y, c), sibling)]
            cps += [copy(1 + j, src_for(*chip), slot(x, y, c), (*chip, c)) for j, chip in enumerate(chips)]
            for cp in cps:
                cp.start()
            first += cps
        for a in range(n):
            def slot(px, py, pc, a=a):
                return outs[a].at[4 * px + 2 * py + pc]

            def copy(k, src, dst, to, a=a):
                return pltpu.make_async_remote_copy(src_ref=src, dst_ref=dst, send_sem=send_sems.at[7 * a + k],
                                                    recv_sem=recv_sems.at[7 * a + k], device_id=to, device_id_type=MESH)

            for j, chip in enumerate(chips):
                copy(1 + j, slot(*chip, c), slot(*chip, c), (*chip, c)).wait_recv()
                fw = copy(4 + j, slot(*chip, c), slot(*chip, c), sibling)
                fw.start()
                passed.append(fw)
        for a in range(n):
            def slot(px, py, pc, a=a):
                return outs[a].at[4 * px + 2 * py + pc]

            def copy(k, src, dst, to, a=a):
                return pltpu.make_async_remote_copy(src_ref=src, dst_ref=dst, send_sem=send_sems.at[7 * a + k],
                                                    recv_sem=recv_sems.at[7 * a + k], device_id=to, device_id_type=MESH)

            copy(0, slot(x, y, 1 - c), slot(x, y, 1 - c), sibling).wait_recv()
            for j, chip in enumerate(chips):
                copy(4 + j, slot(*chip, 1 - c), slot(*chip, 1 - c), sibling).wait_recv()
        for cp in first + passed:
            cp.wait_send()
        for m in mine:
            m.wait()

    anyspec = pl.BlockSpec(memory_space=pl.ANY)
    res = pl.pallas_call(
        body, name=name, in_specs=[anyspec] * n, out_specs=[anyspec] * n,
        out_shape=[jax.ShapeDtypeStruct((8,) + tuple(b), x.dtype) for b, x in zip(blk, xs)],
        scratch_shapes=[pltpu.SemaphoreType.DMA((7 * n,)), pltpu.SemaphoreType.DMA((7 * n,)),
                        pltpu.SemaphoreType.DMA((n,))],
    )(*xs)
    return list(res)


def _pair_swap(xs, name):
    n = len(xs)

    def body(*refs):
        ins, outs = refs[:n], refs[n:2 * n]
        send_sems, recv_sems = refs[2 * n:]
        x, y, c = _coords()
        cps = [pltpu.make_async_remote_copy(src_ref=ins[a].at[1 - c], dst_ref=outs[a], send_sem=send_sems.at[a],
                                            recv_sem=recv_sems.at[a], device_id=(x, y, 1 - c), device_id_type=MESH)
               for a in range(n)]
        for cp in cps:
            cp.start()
        for cp in cps:
            cp.wait()

    anyspec = pl.BlockSpec(memory_space=pl.ANY)
    res = pl.pallas_call(
        body, name=name, in_specs=[anyspec] * n, out_specs=[anyspec] * n,
        out_shape=[jax.ShapeDtypeStruct(x.shape[1:], x.dtype) for x in xs],
        scratch_shapes=[pltpu.SemaphoreType.DMA((n,)), pltpu.SemaphoreType.DMA((n,))],
    )(*xs)
    return list(res)


def _pair_add(g2, theirs, half, name, tm=256):
    _, n, cdim = g2.shape
    tm = min(tm, n)

    def body(h_ref, a_ref, b_ref, o_ref):
        o_ref[...] = (a_ref[...] + b_ref[...]).astype(o_ref.dtype)

    grid_spec = pltpu.PrefetchScalarGridSpec(
        num_scalar_prefetch=1, grid=(n // tm,),
        in_specs=[pl.BlockSpec((None, tm, cdim), lambda i, h: (h[0], i, 0)), pl.BlockSpec((tm, cdim), lambda i, h: (i, 0))],
        out_specs=pl.BlockSpec((tm, cdim), lambda i, h: (i, 0)))
    return pl.pallas_call(body, name=name, grid_spec=grid_spec, out_shape=jax.ShapeDtypeStruct((n, cdim), BF16),
                          compiler_params=_cparams(("parallel",)))(half.reshape(1).astype(jnp.int32), g2, theirs)


def _adamw_math(w, g, m, v):
    m = ADAM_B1 * m + (1.0 - ADAM_B1) * g
    v = ADAM_B2 * v + (1.0 - ADAM_B2) * (g * g)
    m_hat = m / (1.0 - ADAM_B1 ** ADAM_STEP)
    v_hat = v / (1.0 - ADAM_B2 ** ADAM_STEP)
    delta = -ADAM_LR * (m_hat / (jnp.sqrt(v_hat) + ADAM_EPS) + ADAM_WD * w)
    return delta, m, v


def _adamw(parts, w, m, v, name, tm=128):
    npart, R, C = parts.shape
    tm = min(tm, R)

    def body(p_ref, w_ref, m_ref, v_ref, g_out, d_out, m_out, v_out):
        g = p_ref[0].astype(F32)
        for i in range(1, npart):
            g = g + p_ref[i].astype(F32)
        d, mm, vv = _adamw_math(w_ref[...], g, m_ref[...], v_ref[...])
        g_out[...] = g
        d_out[...] = d
        m_out[...] = mm
        v_out[...] = vv

    spec = pl.BlockSpec((tm, C), lambda i: (i, 0))
    return pl.pallas_call(
        body, name=name, grid=(R // tm,),
        in_specs=[pl.BlockSpec((npart, tm, C), lambda i: (0, i, 0)), spec, spec, spec], out_specs=[spec] * 4,
        out_shape=[jax.ShapeDtypeStruct((R, C), F32)] * 4,
        compiler_params=_cparams(("parallel",)),
    )(parts, w, m, v)


def _sum_parts(parts, name):
    npart, R, C = parts.shape

    def body(p_ref, o_ref):
        g = p_ref[0]
        for i in range(1, npart):
            g = g + p_ref[i]
        o_ref[...] = g

    return pl.pallas_call(body, name=name, out_shape=jax.ShapeDtypeStruct((R, C), F32))(parts)


def _mod_fwd(c_all, w_ada, b_sh, name):
    def body(c_ref, w_ref, b_ref, o_ref):
        o_ref[...] = _dot(_silu(c_ref[...]), w_ref[...], "nn") + b_ref[...]

    return pl.pallas_call(body, name=name, out_shape=jax.ShapeDtypeStruct((c_all.shape[0], w_ada.shape[1]), F32),
                          compiler_params=pltpu.CompilerParams(vmem_limit_bytes=VMEM_LIMIT))(c_all, w_ada, b_sh)


def _mod_wgrad(c_all, dmod_sh, name):
    def body(c_ref, d_ref, o_ref):
        o_ref[...] = _dot(_silu(c_ref[...]), d_ref[...], "tn")

    return pl.pallas_call(body, name=name, out_shape=jax.ShapeDtypeStruct((c_all.shape[1], dmod_sh.shape[1]), F32),
                          compiler_params=pltpu.CompilerParams(vmem_limit_bytes=VMEM_LIMIT))(c_all, dmod_sh)


def _pad_lanes(v):
    return jnp.pad(v, ((0, 0), (0, (-v.shape[1]) % LANES)))


def kernel(x, c, norm1_w, norm2_w, w_ada, b_ada, w_in, conv_w, conv_b, dt_bias, a_log, d_skip, ssd_norm_w, q_norm_w, k_norm_w, attn_norm_w, w_out, w_ff1, w_ff2, loss_target, m_norm1_w, m_norm2_w, m_w_ada, m_b_ada, m_w_in, m_conv_w, m_conv_b, m_dt_bias, m_a_log, m_d_skip, m_ssd_norm_w, m_q_norm_w, m_k_norm_w, m_attn_norm_w, m_w_out, m_w_ff1, m_w_ff2, v_norm1_w, v_norm2_w, v_w_ada, v_b_ada, v_w_in, v_conv_w, v_conv_b, v_dt_bias, v_a_log, v_d_skip, v_ssd_norm_w, v_q_norm_w, v_k_norm_w, v_attn_norm_w, v_w_out, v_w_ff1, v_w_ff2):
    xi, yi, ci = _coords()
    chip = 2 * xi + yi
    dev = 2 * chip + ci
    xs, tgt = x[0], loss_target[0]
    S, D = xs.shape
    DI, AD = NH_SSD * HD, NH_ATT * HD
    CC = DI + 2 * NG * NSTATE
    PW = DI + CC + 3 * AD + LANES
    DFF = w_ff1.shape[2] * 4
    MIX = DI + AD
    o_xbc, o_q, o_k, o_v, o_dt = DI, DI + CC, DI + CC + AD, DI + CC + 2 * AD, DI + CC + 3 * AD

    def half_rows(w):
        r = w.shape[0] // 2
        return lax.dynamic_slice_in_dim(w, ci * r, r, 0).astype(BF16)

    c_all, conv_w_all = _exchange8([c, conv_w[0]], False, "gather_c_conv_w")
    c_all = c_all.reshape(8, D)
    c_all = jnp.pad(c_all, ((0, 8), (0, 0)))
    nmod = w_ada.shape[2]
    b_sh = lax.dynamic_slice_in_dim(b_ada, chip * nmod, nmod, 1)
    mod_sh = _mod_fwd(c_all, w_ada[0], b_sh, "mod_fwd")
    mod_all = _exchange8([mod_sh[:8]], False, "gather_mod")[0]
    mod_me = lax.dynamic_index_in_dim(mod_all[0::2], dev, 1, keepdims=False).reshape(1, 4 * nmod)
    shift1, scale1, gate1, shift2, scale2, gate2 = [mod_me[:, i * D:(i + 1) * D] for i in range(6)]

    g_in, g_out, g_ff1, g_ff2 = _exchange8([half_rows(w_in[0]), half_rows(w_out[0]), half_rows(w_ff1[0]),
                                            half_rows(w_ff2[0])], False, "gather_weights")
    wsh = w_in.shape[2]
    w_in_f = g_in.reshape(4, D, wsh).transpose(1, 0, 2).reshape(D, 4 * wsh)
    n_zx = DI + CC
    w_proj = jnp.concatenate([w_in_f[:, :n_zx], w_in_f[:, n_zx + NH_SSD:], w_in_f[:, n_zx:n_zx + NH_SSD],
                              jnp.zeros((D, LANES - NH_SSD), BF16)], axis=1)
    w_out_f = g_out.reshape(MIX, D)
    w_out_a, w_out_b = w_out_f[:DI], w_out_f[DI:]
    w_ff1_f = g_ff1.reshape(4, D, DFF // 4).transpose(1, 0, 2).reshape(D, DFF)
    w_ff2_f = g_ff2.reshape(DFF, D)

    dtb, alog, dsk = _pad_lanes(dt_bias), _pad_lanes(a_log), _pad_lanes(d_skip)
    qw2 = jnp.concatenate([q_norm_w, q_norm_w], axis=1)
    kw2 = jnp.concatenate([k_norm_w, k_norm_w], axis=1)
    conv_w_f = conv_w_all[0::2].transpose(1, 0, 2).reshape(KCONV, CC)

    h1 = _rows("norm1", lambda r, k: ([_normmod(r[0], *k)], []), [(xs, 0, D)], [norm1_w, scale1, shift1],
               [(D, BF16)], [], S)[0]
    proj = _matmul(h1, w_proj, "nn", F32, "in_proj", tn=896)
    xbc = _conv_fwd(proj, o_xbc, CC, conv_w_f, conv_b, "conv_fwd")
    y_ssd, hsave = _ssd_fwd(xbc, proj, o_dt // LANES, dtb, alog, dsk, ssd_norm_w, "ssd_fwd")

    def qk_call(name, col0, w2, scale):
        def body(t_ref, w_ref, o_ref):
            o_ref[...] = _headnorm(t_ref[...], w_ref[...], scale)
        return pl.pallas_call(
            body, name=name, grid=(AD // LANES,),
            in_specs=[pl.BlockSpec((S, LANES), lambda j: (0, j + col0 // LANES)),
                      pl.BlockSpec((1, LANES), lambda j: (0, 0))],
            out_specs=pl.BlockSpec((S, LANES), lambda j: (0, j)),
            out_shape=jax.ShapeDtypeStruct((S, AD), F32), compiler_params=_cparams(("parallel",)),
        )(proj, w2)

    qn = qk_call("q_norm", o_q, qw2, HD ** -0.5)
    kn = qk_call("k_norm", o_k, kw2, 1.0)
    o_att, lse = _attn_fwd(qn, kn, proj, o_v // LANES, "attn_fwd")
    y_att = _rows("attn_out_norm", lambda r, k: ([_rmsw(r[0], k[0])], []), [(o_att, 0, AD)], [attn_norm_w],
                  [(AD, BF16)], [], S)[0]
    mix_a = _matmul(y_ssd, w_out_a, "nn", F32, "out_proj_a")
    mix = _matmul(y_att, w_out_b, "nn", F32, "out_proj_b", epilogue=lambda r, e: r + e, extras=(mix_a,))
    x2, h2 = _rows("resid_norm2", lambda r, k: (list(_resid_normmod(r[0], r[1], *k)), []), [(xs, 0, D), (mix, 0, D)],
                   [gate1, norm2_w, scale2, shift2], [(D, F32), (D, BF16)], [], S)
    u = _matmul(h2, w_ff1_f, "nn", F32, "ff1")
    relu2 = lambda t: jnp.square(jnp.maximum(t, 0.0))
    ff = _matmul(u, w_ff2_f, "nn", F32, "ff2", a_fn=relu2)

    def loss_fn(r, k):
        x2_, ff_, t_ = r
        err = x2_ + k[0] * ff_ - t_
        dy_ = err * (1.0 / D)
        ls = jnp.sum(jnp.sum(0.5 * err * err, axis=1, keepdims=True), axis=0, keepdims=True) * (1.0 / D)
        return [dy_, dy_ * k[0]], [ls, jnp.sum(dy_ * ff_, axis=0, keepdims=True)]

    dy, dff, loss_p, dgate2 = _rows("loss", loss_fn, [(x2, 0, D), (ff, 0, D), (tgt, 0, D)], [gate2],
                                    [(D, F32), (D, BF16)], [(1, 1), (1, D)], S)
    du = _matmul(dff, w_ff2_f, "nt", BF16, "ff2_dx", epilogue=lambda r, e: r * (2.0 * jnp.maximum(e, 0.0)), extras=(u,))
    gw_ff2 = _matmul(u, dff, "tn", F32, "ff2_dw", a_fn=relu2)
    gw_ff1 = _matmul(h2, du, "tn", F32, "ff1_dw")
    dh2 = _matmul(du, w_ff1_f, "nt", F32, "ff1_dx")

    def resid_bwd(r, k):
        x_, mix_, dx2a, dh2_ = r
        _, vjp = jax.vjp(_resid_normmod, x_, mix_, *k)
        dx, dmix_, dg, dnw, dsc, dsh = vjp((dx2a, dh2_))
        return [dx, dmix_], [dg, dnw, dsc, dsh]

    dx2, dmix, dgate1, g_norm2, dscale2, dshift2 = _rows(
        "resid_norm2_bwd", resid_bwd, [(xs, 0, D), (mix, 0, D), (dy, 0, D), (dh2, 0, D)],
        [gate1, norm2_w, scale2, shift2], [(D, F32), (D, BF16)], [(1, D)] * 4, S)
    dy_ssd = _matmul(dmix, w_out_a, "nt", F32, "out_proj_dx_a")
    dy_att = _matmul(dmix, w_out_b, "nt", F32, "out_proj_dx_b")
    gw_out = jnp.concatenate([_matmul(y_ssd, dmix, "tn", F32, "out_proj_dw_a"),
                              _matmul(y_att, dmix, "tn", F32, "out_proj_dw_b")], axis=0)

    def attn_norm_bwd(r, k):
        o_, dyo = r
        _, vjp = jax.vjp(_rmsw, o_, k[0])
        do_, dw_ = vjp(dyo)
        lo = _lane_mask()
        dd_blocks = []
        for b in range(AD // LANES):
            t = (do_ * o_)[:, b * LANES:(b + 1) * LANES]
            s0 = jnp.sum(jnp.where(lo, t, 0.0), axis=1, keepdims=True)
            s1 = jnp.sum(jnp.where(lo, 0.0, t), axis=1, keepdims=True)
            dd_blocks.append(jnp.where(lo, s0, s1))
        return [do_, jnp.concatenate(dd_blocks, axis=1)], [dw_]

    do_att, dd_att, g_attn_norm = _rows("attn_norm_bwd", attn_norm_bwd, [(o_att, 0, AD), (dy_att, 0, AD)],
                                        [attn_norm_w], [(AD, F32), (AD, F32)], [(1, AD)], S)
    dq_n, dk_n, dv = _attn_bwd(qn, kn, proj, o_v // LANES, do_att, lse, dd_att, "attn_bwd")

    def qk_bwd_call(name, col0, w2, scale, g):
        def body(t_ref, w_ref, g_ref, o_ref, dw_ref):
            @pl.when(pl.program_id(0) == 0)
            def _():
                dw_ref[...] = jnp.zeros_like(dw_ref)
            _, vjp = jax.vjp(lambda t, w: _headnorm(t, w, scale), t_ref[...], w_ref[...])
            dt_, dw_ = vjp(g_ref[...])
            o_ref[...] = dt_.astype(BF16)
            dw_ref[...] += dw_
        blk = pl.BlockSpec((S, LANES), lambda j: (0, j))
        return pl.pallas_call(
            body, name=name, grid=(AD // LANES,),
            in_specs=[pl.BlockSpec((S, LANES), lambda j: (0, j + col0 // LANES)),
                      pl.BlockSpec((1, LANES), lambda j: (0, 0)), blk],
            out_specs=[blk, pl.BlockSpec((1, LANES), lambda j: (0, 0))],
            out_shape=[jax.ShapeDtypeStruct((S, AD), BF16), jax.ShapeDtypeStruct((1, LANES), F32)],
            compiler_params=_cparams(("arbitrary",)),
        )(proj, w2, g)

    dq, g_qw2 = qk_bwd_call("q_norm_bwd", o_q, qw2, HD ** -0.5, dq_n)
    dk, g_kw2 = qk_bwd_call("k_norm_bwd", o_k, kw2, 1.0, dk_n)
    g_q_norm = g_qw2[:, :HD] + g_qw2[:, HD:]
    g_k_norm = g_kw2[:, :HD] + g_kw2[:, HD:]

    dxbc, dz, ddtr, g_dtb, g_alog, g_dsk, g_ssd_norm = _ssd_bwd(
        xbc, proj, o_dt // LANES, dtb, alog, dsk, ssd_norm_w, hsave, dy_ssd, "ssd_bwd")
    dxbc_pre, g_conv_w, g_conv_b = _conv_bwd(proj, o_xbc, CC, conv_w_f, conv_b, dxbc, "conv_bwd")
    dproj = jnp.concatenate([dz.astype(BF16), dxbc_pre.astype(BF16), dq, dk, dv.astype(BF16), ddtr.astype(BF16)], axis=1)
    gw_proj = _matmul(h1, dproj, "tn", F32, "in_proj_dw", tn=896)
    dh1 = _matmul(dproj, w_proj, "nt", F32, "in_proj_dx", tk=896)

    def norm1_bwd(r, k):
        x_, dh_, dres = r
        _, vjp = jax.vjp(_normmod, x_, *k)
        dx, dnw, dsc, dsh = vjp(dh_)
        return [dx + dres], [dnw, dsc, dsh]

    grad_x, g_norm1, dscale1, dshift1 = _rows("norm1_bwd", norm1_bwd, [(xs, 0, D), (dh1, 0, D), (dx2, 0, D)],
                                              [norm1_w, scale1, shift1], [(D, F32)], [(1, D)] * 3, S)
    gw_in = jnp.concatenate([gw_proj[:, :n_zx], gw_proj[:, o_dt:o_dt + NH_SSD], gw_proj[:, n_zx:o_dt]], axis=1)
    dmod = jnp.concatenate([dshift1, dscale1, dgate1, dshift2, dscale2, dgate2], axis=1)

    small = [g_norm1, g_norm2, dmod, g_conv_b, g_dtb, g_alog, g_dsk, g_ssd_norm, _pad_lanes(g_q_norm),
             _pad_lanes(g_k_norm), g_attn_norm, g_conv_w.reshape(1, KCONV * CC)]
    sizes = [t.shape[1] for t in small]
    packed = jnp.concatenate(small, axis=1)
    nrow = -(-packed.shape[1] // LANES // 8) * 8
    packed = jnp.pad(packed, ((0, 0), (0, nrow * LANES - packed.shape[1]))).reshape(nrow, LANES)
    packed_all = _exchange8([packed], False, "gather_small_grads")[0]
    tot = _sum_parts(packed_all, "sum_small_grads").reshape(1, nrow * LANES)
    offs = [sum(sizes[:i]) for i in range(len(sizes))]
    (g_norm1, g_norm2, g_b_ada, g_conv_b, g_dtb, g_alog, g_dsk, g_ssd_norm, g_q_norm, g_k_norm, g_attn_norm,
     g_conv_w) = [tot[:, o:o + n] for o, n in zip(offs, sizes)]
    g_dtb, g_alog, g_dsk = g_dtb[:, :NH_SSD], g_alog[:, :NH_SSD], g_dsk[:, :NH_SSD]
    g_q_norm, g_k_norm = g_q_norm[:, :HD], g_k_norm[:, :HD]
    ccs = CC // 4
    g_conv_w = lax.dynamic_slice_in_dim(g_conv_w.reshape(KCONV, CC), chip * ccs, ccs, 1)

    dmod_all = packed_all.reshape(8, nrow * LANES)[:, offs[2]:offs[2] + 6 * D]
    dmod_sh = jnp.pad(lax.dynamic_slice_in_dim(dmod_all, chip * nmod, nmod, 1), ((0, 8), (0, 0)))
    gw_ada = _mod_wgrad(c_all, dmod_sh, "mod_wgrad")

    def by_half_cols(g):
        r, c4 = g.shape
        return g.reshape(2, r // 2, 4, c4 // 4).transpose(0, 2, 1, 3)

    def by_half_rows(g):
        r4, cdim = g.shape
        return g.reshape(4, 2, r4 // 8, cdim).transpose(1, 0, 2, 3)

    big = [by_half_cols(gw_in), by_half_rows(gw_out), by_half_cols(gw_ff1), by_half_rows(gw_ff2)]
    theirs = _pair_swap(big, "pair_swap_grads")
    pair_sums = []
    for i, (g2, t) in enumerate(zip(big, theirs)):
        _, r2, cdim = t.shape
        sm = _pair_add(g2.reshape(2, 4 * r2, cdim), t.reshape(4 * r2, cdim), ci, "pair_add_%d" % i)
        pair_sums.append(sm.reshape(4, r2, cdim))
    scattered = _exchange8(pair_sums, True, "scatter_grads")
    parts = [s.reshape(4, 2 * s.shape[1], s.shape[2]) for s in scattered]

    res_in = _adamw(parts[0], w_in[0], m_w_in[0], v_w_in[0], "adamw_w_in")
    res_out = _adamw(parts[1], w_out[0], m_w_out[0], v_w_out[0], "adamw_w_out")
    res_ff1 = _adamw(parts[2], w_ff1[0], m_w_ff1[0], v_w_ff1[0], "adamw_w_ff1")
    res_ff2 = _adamw(parts[3], w_ff2[0], m_w_ff2[0], v_w_ff2[0], "adamw_w_ff2")
    res_ada = _adamw(gw_ada[None], w_ada[0], m_w_ada[0], v_w_ada[0], "adamw_w_ada")

    small_names = ["norm1_w", "norm2_w", "b_ada", "conv_w", "conv_b", "dt_bias", "a_log", "d_skip", "ssd_norm_w",
                   "q_norm_w", "k_norm_w", "attn_norm_w"]
    small_g = dict(norm1_w=g_norm1, norm2_w=g_norm2, b_ada=g_b_ada, conv_w=g_conv_w.reshape(1, KCONV * ccs),
                   conv_b=g_conv_b, dt_bias=g_dtb, a_log=g_alog, d_skip=g_dsk, ssd_norm_w=g_ssd_norm, q_norm_w=g_q_norm,
                   k_norm_w=g_k_norm, attn_norm_w=g_attn_norm)
    small_w = dict(norm1_w=(norm1_w, m_norm1_w, v_norm1_w), norm2_w=(norm2_w, m_norm2_w, v_norm2_w),
                   b_ada=(b_ada, m_b_ada, v_b_ada),
                   conv_w=tuple(t.reshape(1, KCONV * ccs) for t in (conv_w, m_conv_w, v_conv_w)),
                   conv_b=(conv_b, m_conv_b, v_conv_b), dt_bias=(dt_bias, m_dt_bias, v_dt_bias),
                   a_log=(a_log, m_a_log, v_a_log), d_skip=(d_skip, m_d_skip, v_d_skip),
                   ssd_norm_w=(ssd_norm_w, m_ssd_norm_w, v_ssd_norm_w), q_norm_w=(q_norm_w, m_q_norm_w, v_q_norm_w),
                   k_norm_w=(k_norm_w, m_k_norm_w, v_k_norm_w), attn_norm_w=(attn_norm_w, m_attn_norm_w, v_attn_norm_w))
    ssz = [_pad_lanes(small_g[n]).shape[1] for n in small_names]
    soff = [sum(ssz[:i]) for i in range(len(ssz))]
    srow = -(-sum(ssz) // LANES // 8) * 8

    def pack(ts, fill):
        t = jnp.concatenate([jnp.pad(t, ((0, 0), (0, (-t.shape[1]) % LANES)), constant_values=fill) for t in ts], axis=1)
        return jnp.pad(t, ((0, 0), (0, srow * LANES - t.shape[1])), constant_values=fill).reshape(srow, LANES)

    sg = pack([small_g[n] for n in small_names], 0.0)
    sw = pack([small_w[n][0] for n in small_names], 0.0)
    sm_ = pack([small_w[n][1] for n in small_names], 0.0)
    sv = pack([small_w[n][2] for n in small_names], 1.0)
    _, s_delta, s_m, s_v = _adamw(sg[None], sw, sm_, sv, "adamw_small", tm=srow)

    def unpack(t, n):
        i = small_names.index(n)
        return t.reshape(1, srow * LANES)[:, soff[i]:soff[i] + small_g[n].shape[1]].reshape(small_w[n][0].shape)

    loss = lax.psum(loss_p[0, 0], ("x", "y", "c"))
    big_res = dict(w_ada=res_ada, w_in=res_in, w_out=res_out, w_ff1=res_ff1, w_ff2=res_ff2)
    order = ["norm1_w", "norm2_w", "w_ada", "b_ada", "w_in", "conv_w", "conv_b", "dt_bias", "a_log", "d_skip",
             "ssd_norm_w", "q_norm_w", "k_norm_w", "attn_norm_w", "w_out", "w_ff1", "w_ff2"]
    grads, deltas, new_m, new_v = [], [], [], []
    for n in order:
        if n in big_res:
            g_, d_, m_, v_ = [t[None] for t in big_res[n]]
        else:
            g_ = small_g[n].reshape(small_w[n][0].shape)
            d_, m_, v_ = unpack(s_delta, n), unpack(s_m, n), unpack(s_v, n)
            if n == "conv_w":
                g_, d_, m_, v_ = [t.reshape(conv_w.shape) for t in (g_, d_, m_, v_)]
        grads.append(g_)
        deltas.append(d_)
        new_m.append(m_)
        new_v.append(v_)
    return (loss, grad_x[None], *grads, *deltas, *new_m, *new_v)
```

```python
import functools

import jax
import jax.numpy as jnp
from jax import lax
from jax.experimental import pallas as pl
from jax.experimental.pallas import tpu as pltpu

F32, BF16 = jnp.float32, jnp.bfloat16
EPS = 1e-6
HD = 64
NH_SSD = 16
NG = 4
NSTATE = 128
KCONV = 4
CHUNK = 128
NH_ATT = 16
PATTERNS = ((128, 1), (512, 4), (2048, 16))
ABLK = 128
QTILE = 128
LANES = 128
ADAM_LR, ADAM_B1, ADAM_B2, ADAM_EPS, ADAM_WD, ADAM_STEP = 0.001, 0.9, 0.999, 1e-08, 0.01, 10
VMEM_LIMIT = 56 * 1024 * 1024
MESH = pl.DeviceIdType.MESH
NEG = -1e30

_DN = {"nn": (((1,), (0,)), ((), ())), "nt": (((1,), (1,)), ((), ())), "tn": (((0,), (0,)), ((), ()))}


def _cparams(sem):
    return pltpu.CompilerParams(dimension_semantics=sem, vmem_limit_bytes=VMEM_LIMIT)


def _tile(n, cap):
    if n % LANES or n <= LANES:
        return n
    best = LANES
    for t in range(LANES, min(n, cap) + 1, LANES):
        if n % t == 0:
            best = t
    return best


def _silu(x):
    return x / (1.0 + jnp.exp(-x))


def _softplus(x):
    return jnp.maximum(x, 0.0) + jnp.log(1.0 + jnp.exp(-jnp.abs(x)))


def _dot(a, b, dims):
    return lax.dot_general(a.astype(BF16), b.astype(BF16), _DN[dims], preferred_element_type=F32)


def _matmul(a, b, dims, out_dtype, name, a_fn=None, epilogue=None, extras=(), tm=1024, tn=1024, tk=1024):
    if dims == "nn":
        (M, K), (_, N) = a.shape, b.shape
    elif dims == "nt":
        (M, K), (N, _) = a.shape, b.shape
    else:
        (K, M), (_, N) = a.shape, b.shape
    tm, tn, tk = _tile(M, tm), _tile(N, tn), _tile(K, tk)
    nk = K // tk
    ne = len(extras)

    def body(a_ref, b_ref, *rest):
        e_refs, o_ref = rest[:ne], rest[ne]
        av = a_ref[...]
        if a_fn is not None:
            av = a_fn(av)
        part = _dot(av, b_ref[...], dims)

        def finish(r):
            if epilogue is not None:
                r = epilogue(r, *[e[...] for e in e_refs])
            o_ref[...] = r.astype(out_dtype)

        if nk == 1:
            finish(part)
            return
        acc = rest[ne + 1]
        k = pl.program_id(2)

        @pl.when(k == 0)
        def _():
            acc[...] = part

        @pl.when(k > 0)
        def _():
            acc[...] += part

        @pl.when(k == nk - 1)
        def _():
            finish(acc[...])

    a_spec = pl.BlockSpec((tk, tm), lambda i, j, k: (k, i)) if dims == "tn" else pl.BlockSpec((tm, tk), lambda i, j, k: (i, k))
    b_spec = pl.BlockSpec((tn, tk), lambda i, j, k: (j, k)) if dims == "nt" else pl.BlockSpec((tk, tn), lambda i, j, k: (k, j))
    o_spec = pl.BlockSpec((tm, tn), lambda i, j, k: (i, j))
    return pl.pallas_call(
        body, name=name, grid=(M // tm, N // tn, nk),
        in_specs=[a_spec, b_spec] + [o_spec] * ne, out_specs=o_spec,
        out_shape=jax.ShapeDtypeStruct((M, N), out_dtype),
        scratch_shapes=[pltpu.VMEM((tm, tn), F32)] if nk > 1 else [],
        compiler_params=_cparams(("parallel", "parallel", "arbitrary")),
    )(a, b, *extras)


def _rows(name, fn, rows, consts, outs, accs, n_rows, tm=256):
    tm = min(tm, n_rows)
    nr, nc, no, na = len(rows), len(consts), len(outs), len(accs)

    def body(*refs):
        r_refs, c_refs = refs[:nr], refs[nr:nr + nc]
        o_refs, a_refs = refs[nr + nc:nr + nc + no], refs[nr + nc + no:]
        o_vals, a_vals = fn([r[...] for r in r_refs], [c[...] for c in c_refs])
        for ref, val in zip(o_refs, o_vals):
            ref[...] = val.astype(ref.dtype)
        if na:
            @pl.when(pl.program_id(0) == 0)
            def _():
                for ref in a_refs:
                    ref[...] = jnp.zeros_like(ref)
            for ref, val in zip(a_refs, a_vals):
                ref[...] += val

    in_specs = [pl.BlockSpec((tm, w), lambda i, cb=cb: (i, cb)) for (_, cb, w) in rows]
    in_specs += [pl.BlockSpec(cst.shape, lambda i, nd=cst.ndim: (0,) * nd) for cst in consts]
    out_specs = [pl.BlockSpec((tm, w), lambda i: (i, 0)) for (w, _) in outs]
    out_specs += [pl.BlockSpec(s, lambda i: (0, 0)) for s in accs]
    out_shape = [jax.ShapeDtypeStruct((n_rows, w), dt) for (w, dt) in outs]
    out_shape += [jax.ShapeDtypeStruct(s, F32) for s in accs]
    res = pl.pallas_call(
        body, name=name, grid=(n_rows // tm,), in_specs=in_specs, out_specs=out_specs, out_shape=out_shape,
        compiler_params=_cparams(("arbitrary",)),
    )(*[r[0] for r in rows], *consts)
    return res


def _normmod(x, nw, sc, sh):
    r = lax.rsqrt(jnp.mean(x * x, axis=-1, keepdims=True) + EPS)
    return (x * r) * nw * (1.0 + sc) + sh


def _resid_normmod(x, mix, g, nw, sc, sh):
    x2 = x + g * mix
    return x2, _normmod(x2, nw, sc, sh)


def _rmsw(o, w):
    return o * lax.rsqrt(jnp.mean(o * o, axis=-1, keepdims=True) + EPS) * w


def _lane_mask():
    return lax.broadcasted_iota(jnp.int32, (1, LANES), 1) < HD


def _headnorm(t, w, scale):
    lo = _lane_mask()
    t2 = t * t
    s0 = jnp.sum(jnp.where(lo, t2, 0.0), axis=1, keepdims=True)
    s1 = jnp.sum(jnp.where(lo, 0.0, t2), axis=1, keepdims=True)
    ms = jnp.where(lo, s0, s1) * (1.0 / HD)
    return t * lax.rsqrt(ms + EPS) * w * scale


def _conv_cols(n_ch):
    return _tile(n_ch, 256)


def _conv_fwd(proj, col0, n_ch, conv_w, conv_b, name):
    S = proj.shape[0]
    tc = _conv_cols(n_ch)

    def body(u_ref, w_ref, b_ref, o_ref):
        u = u_ref[...]
        row = lax.broadcasted_iota(jnp.int32, u.shape, 0)
        acc = b_ref[...] + w_ref[KCONV - 1:KCONV, :] * u
        for i in range(KCONV - 1):
            sh = KCONV - 1 - i
            acc = acc + w_ref[i:i + 1, :] * jnp.where(row >= sh, pltpu.roll(u, sh, 0), 0.0)
        o_ref[...] = _silu(acc)

    return pl.pallas_call(
        body, name=name, grid=(n_ch // tc,),
        in_specs=[pl.BlockSpec((S, tc), lambda j: (0, j + col0 // tc)),
                  pl.BlockSpec((KCONV, tc), lambda j: (0, j)), pl.BlockSpec((1, tc), lambda j: (0, j))],
        out_specs=pl.BlockSpec((S, tc), lambda j: (0, j)),
        out_shape=jax.ShapeDtypeStruct((S, n_ch), F32),
        compiler_params=_cparams(("parallel",)),
    )(proj, conv_w, conv_b)


def _conv_bwd(proj, col0, n_ch, conv_w, conv_b, dxbc, name):
    S = proj.shape[0]
    tc = _conv_cols(n_ch)

    def body(u_ref, w_ref, b_ref, g_ref, du_ref, dw_ref, db_ref):
        u = u_ref[...]
        row = lax.broadcasted_iota(jnp.int32, u.shape, 0)
        shifted = [jnp.where(row >= s, pltpu.roll(u, s, 0), 0.0) for s in range(1, KCONV)]
        acc = b_ref[...] + w_ref[KCONV - 1:KCONV, :] * u
        for i in range(KCONV - 1):
            acc = acc + w_ref[i:i + 1, :] * shifted[KCONV - 2 - i]
        sig = 1.0 / (1.0 + jnp.exp(-acc))
        dacc = g_ref[...] * (sig * (1.0 + acc * (1.0 - sig)))
        db_ref[...] = jnp.sum(dacc, axis=0, keepdims=True)
        du = w_ref[KCONV - 1:KCONV, :] * dacc
        dw_ref[KCONV - 1:KCONV, :] = jnp.sum(dacc * u, axis=0, keepdims=True)
        for i in range(KCONV - 1):
            sh = KCONV - 1 - i
            dw_ref[i:i + 1, :] = jnp.sum(dacc * shifted[sh - 1], axis=0, keepdims=True)
            du = du + w_ref[i:i + 1, :] * jnp.where(row < S - sh, pltpu.roll(dacc, S - sh, 0), 0.0)
        du_ref[...] = du

    return pl.pallas_call(
        body, name=name, grid=(n_ch // tc,),
        in_specs=[pl.BlockSpec((S, tc), lambda j: (0, j + col0 // tc)),
                  pl.BlockSpec((KCONV, tc), lambda j: (0, j)), pl.BlockSpec((1, tc), lambda j: (0, j)),
                  pl.BlockSpec((S, tc), lambda j: (0, j))],
        out_specs=[pl.BlockSpec((S, tc), lambda j: (0, j)), pl.BlockSpec((KCONV, tc), lambda j: (0, j)),
                   pl.BlockSpec((1, tc), lambda j: (0, j))],
        out_shape=[jax.ShapeDtypeStruct((S, n_ch), F32), jax.ShapeDtypeStruct((KCONV, n_ch), F32),
                   jax.ShapeDtypeStruct((1, n_ch), F32)],
        compiler_params=_cparams(("parallel",)),
    )(proj, conv_w, conv_b, dxbc)


@functools.partial(jax.custom_vjp, nondiff_argnums=(2,))
def _mm(a, b, dims):
    return _dot(a, b, dims)


def _mm_fwd(a, b, dims):
    return _dot(a, b, dims), (a, b)


def _mm_bwd(dims, res, g):
    a, b = res
    if dims == "nn":
        return _dot(g, b, "nt"), _dot(a, g, "tn")
    if dims == "nt":
        return _dot(g, b, "nn"), _dot(g, a, "tn")
    return _dot(b, g, "nt"), _dot(a, g, "nn")


_mm.defvjp(_mm_fwd, _mm_bwd)


def _tri_dot(x, upper):
    n = x.shape[0]
    r = lax.broadcasted_iota(jnp.int32, (n, n), 0)
    c = lax.broadcasted_iota(jnp.int32, (n, n), 1)
    t = jnp.where((r <= c) if upper else (r >= c), 1.0, 0.0)
    return lax.dot_general(t, x, _DN["nn"], precision=lax.Precision.HIGHEST, preferred_element_type=F32)


@jax.custom_vjp
def _cumsum_rows(x):
    return _tri_dot(x, False)


_cumsum_rows.defvjp(lambda x: (_tri_dot(x, False), None), lambda _, g: (_tri_dot(g, True),))


def _ssd_chunk(xs_p, bm_g, cm_g, dtr, z_p, dtb, alog, dsk, nw_p, h_p):
    L = dtr.shape[0]
    n_pairs = len(xs_p)
    ppg = n_pairs // len(bm_g)
    lane = lax.broadcasted_iota(jnp.int32, (1, LANES), 1)
    sub = lax.broadcasted_iota(jnp.int32, (LANES, 1), 0)
    lo = lane < HD
    row_l = lax.broadcasted_iota(jnp.int32, (L, 1), 0)
    tri = lax.broadcasted_iota(jnp.int32, (L, L), 0) >= lax.broadcasted_iota(jnp.int32, (L, L), 1)

    dt = _softplus(dtr + dtb)
    acs = _cumsum_rows(dt * (-jnp.exp(alog)))
    acs_t = acs.T
    a_last = jnp.sum(jnp.where(row_l == L - 1, acs, 0.0), axis=0, keepdims=True)
    e_acs = jnp.exp(acs)
    dec = jnp.exp(a_last - acs)
    cdec = jnp.exp(a_last)

    def colv(m, h):
        return jnp.sum(jnp.where(lane == h, m, 0.0), axis=1, keepdims=True)

    def rowv(mt, h):
        return jnp.sum(jnp.where(sub == h, mt, 0.0), axis=0, keepdims=True)

    def pair(m, h0):
        return jnp.where(lo, colv(m, h0), colv(m, h0 + 1))

    ys, hs = [], []
    cb = None
    for p in range(n_pairs):
        g, h0 = p // ppg, 2 * p
        bmat, cmat = bm_g[g], cm_g[g]
        if p % ppg == 0:
            cb = _mm(cmat, bmat, "nt")
        x = xs_p[p]
        xdt = x * pair(dt, h0)
        yd = []
        for h in (h0, h0 + 1):
            seg = colv(acs, h) - rowv(acs_t, h)
            lm = jnp.where(tri, jnp.exp(jnp.where(tri, seg, 0.0)), 0.0)
            yd.append(_mm(cb * lm, xdt, "nn"))
        y = jnp.where(lo, yd[0], yd[1])
        y = y + _mm(cmat, h_p[p], "nt") * pair(e_acs, h0)
        st = _mm(xdt * pair(dec, h0), bmat, "tn")
        cd_col = jnp.where(sub < HD, colv(cdec, h0), colv(cdec, h0 + 1))
        hs.append(h_p[p] * cd_col + st)
        ys.append(y + pair(dsk, h0) * x)

    y2 = [ys[p] * _silu(z_p[p]) for p in range(n_pairs)]
    outs = []
    for g in range(len(bm_g)):
        ps = range(g * ppg, (g + 1) * ppg)
        ss = sum(jnp.sum(y2[p] * y2[p], axis=1, keepdims=True) for p in ps)
        rs = lax.rsqrt(ss * (1.0 / (ppg * LANES)) + EPS)
        outs += [y2[p] * rs * nw_p[p] for p in ps]
    return outs, hs


def _ssd_slices(xbc_ref, z_ref, nw_ref, di):
    n_pairs = di // LANES
    xs_p = [xbc_ref[:, p * LANES:(p + 1) * LANES] for p in range(n_pairs)]
    bm_g = [xbc_ref[:, di + g * NSTATE:di + (g + 1) * NSTATE] for g in range(NG)]
    cm_g = [xbc_ref[:, di + (NG + g) * NSTATE:di + (NG + g + 1) * NSTATE] for g in range(NG)]
    z_p = [z_ref[:, p * LANES:(p + 1) * LANES] for p in range(n_pairs)]
    nw_p = [nw_ref[:, p * LANES:(p + 1) * LANES] for p in range(n_pairs)]
    return xs_p, bm_g, cm_g, z_p, nw_p


def _ssd_fwd(xbc, proj, dt_cb, dtb, alog, dsk, nw, name):
    S, cc = xbc.shape
    di = NH_SSD * HD
    n_pairs = di // LANES
    nchunk = S // CHUNK

    def body(xbc_ref, z_ref, dtr_ref, dtb_ref, alog_ref, dsk_ref, nw_ref, y_ref, hs_ref, h_scr):
        @pl.when(pl.program_id(0) == 0)
        def _():
            h_scr[...] = jnp.zeros_like(h_scr)

        xs_p, bm_g, cm_g, z_p, nw_p = _ssd_slices(xbc_ref, z_ref, nw_ref, di)
        h_p = [h_scr[p * LANES:(p + 1) * LANES, :] for p in range(n_pairs)]
        hs_ref[...] = h_scr[...]
        outs, hs = _ssd_chunk(xs_p, bm_g, cm_g, dtr_ref[...], z_p, dtb_ref[...], alog_ref[...], dsk_ref[...], nw_p, h_p)
        for p in range(n_pairs):
            y_ref[:, p * LANES:(p + 1) * LANES] = outs[p].astype(y_ref.dtype)
            h_scr[p * LANES:(p + 1) * LANES, :] = hs[p]

    vec = pl.BlockSpec((1, LANES), lambda c: (0, 0))
    return pl.pallas_call(
        body, name=name, grid=(nchunk,),
        in_specs=[pl.BlockSpec((CHUNK, cc), lambda c: (c, 0)), pl.BlockSpec((CHUNK, di), lambda c: (c, 0)),
                  pl.BlockSpec((CHUNK, LANES), lambda c: (c, dt_cb)), vec, vec, vec,
                  pl.BlockSpec((1, di), lambda c: (0, 0))],
        out_specs=[pl.BlockSpec((CHUNK, di), lambda c: (c, 0)), pl.BlockSpec((None, di, NSTATE), lambda c: (c, 0, 0))],
        out_shape=[jax.ShapeDtypeStruct((S, di), BF16), jax.ShapeDtypeStruct((nchunk, di, NSTATE), F32)],
        scratch_shapes=[pltpu.VMEM((di, NSTATE), F32)],
        compiler_params=_cparams(("arbitrary",)),
    )(xbc, proj, proj, dtb, alog, dsk, nw)


def _ssd_bwd(xbc, proj, dt_cb, dtb, alog, dsk, nw, hsave, dy, name):
    S, cc = xbc.shape
    di = NH_SSD * HD
    n_pairs = di // LANES
    nchunk = S // CHUNK

    def body(xbc_ref, z_ref, dtr_ref, dtb_ref, alog_ref, dsk_ref, nw_ref, hs_ref, dy_ref,
             dxbc_ref, dz_ref, ddtr_ref, ddtb_ref, dalog_ref, ddsk_ref, dnw_ref, dh_scr):
        @pl.when(pl.program_id(0) == 0)
        def _():
            dh_scr[...] = jnp.zeros_like(dh_scr)
            ddtb_ref[...] = jnp.zeros_like(ddtb_ref)
            dalog_ref[...] = jnp.zeros_like(dalog_ref)
            ddsk_ref[...] = jnp.zeros_like(ddsk_ref)
            dnw_ref[...] = jnp.zeros_like(dnw_ref)

        xs_p, bm_g, cm_g, z_p, nw_p = _ssd_slices(xbc_ref, z_ref, nw_ref, di)
        h_p = [hs_ref[p * LANES:(p + 1) * LANES, :] for p in range(n_pairs)]
        dy_p = [dy_ref[:, p * LANES:(p + 1) * LANES].astype(F32) for p in range(n_pairs)]
        dh_p = [dh_scr[p * LANES:(p + 1) * LANES, :] for p in range(n_pairs)]
        _, vjp = jax.vjp(_ssd_chunk, xs_p, bm_g, cm_g, dtr_ref[...], z_p, dtb_ref[...], alog_ref[...], dsk_ref[...],
                         nw_p, h_p)
        dxs, dbm, dcm, ddtr, dz, ddtb, dalog, ddsk, dnw, dh = vjp((dy_p, dh_p))
        for p in range(n_pairs):
            sl = slice(p * LANES, (p + 1) * LANES)
            dxbc_ref[:, sl] = dxs[p]
            dz_ref[:, sl] = dz[p]
            dnw_ref[:, sl] += dnw[p]
            dh_scr[sl, :] = dh[p]
        for g in range(NG):
            dxbc_ref[:, di + g * NSTATE:di + (g + 1) * NSTATE] = dbm[g]
            dxbc_ref[:, di + (NG + g) * NSTATE:di + (NG + g + 1) * NSTATE] = dcm[g]
        ddtr_ref[...] = ddtr
        ddtb_ref[...] += ddtb
        dalog_ref[...] += dalog
        ddsk_ref[...] += ddsk

    last = nchunk - 1
    vec = pl.BlockSpec((1, LANES), lambda c: (0, 0))
    return pl.pallas_call(
        body, name=name, grid=(nchunk,),
        in_specs=[pl.BlockSpec((CHUNK, cc), lambda c: (last - c, 0)), pl.BlockSpec((CHUNK, di), lambda c: (last - c, 0)),
                  pl.BlockSpec((CHUNK, LANES), lambda c: (last - c, dt_cb)), vec, vec, vec,
                  pl.BlockSpec((1, di), lambda c: (0, 0)),
                  pl.BlockSpec((None, di, NSTATE), lambda c: (last - c, 0, 0)),
                  pl.BlockSpec((CHUNK, di), lambda c: (last - c, 0))],
        out_specs=[pl.BlockSpec((CHUNK, cc), lambda c: (last - c, 0)), pl.BlockSpec((CHUNK, di), lambda c: (last - c, 0)),
                   pl.BlockSpec((CHUNK, LANES), lambda c: (last - c, 0)), vec, vec, vec,
                   pl.BlockSpec((1, di), lambda c: (0, 0))],
        out_shape=[jax.ShapeDtypeStruct((S, cc), F32), jax.ShapeDtypeStruct((S, di), F32),
                   jax.ShapeDtypeStruct((S, LANES), F32), jax.ShapeDtypeStruct((1, LANES), F32),
                   jax.ShapeDtypeStruct((1, LANES), F32), jax.ShapeDtypeStruct((1, LANES), F32),
                   jax.ShapeDtypeStruct((1, di), F32)],
        scratch_shapes=[pltpu.VMEM((di, NSTATE), F32)],
        compiler_params=_cparams(("arbitrary",)),
    )(xbc, proj, proj, dtb, alog, dsk, nw, hsave, dy)


def _band_masks(rows_q, rows_k):
    qi = lax.broadcasted_iota(jnp.int32, (rows_q, rows_k), 0)
    ki = lax.broadcasted_iota(jnp.int32, (rows_q, rows_k), 1)
    return qi, ki


def _class_chunks(n_rows, d):
    per_class = n_rows // d
    ch = min(per_class, 256)
    out = []
    for r in range(d):
        for c0 in range(0, per_class, ch):
            tok = pl.ds(c0, ch) if d == 1 else pl.ds(r + d * c0, ch, stride=d)
            out.append((tok, pl.ds(r * per_class + c0, ch)))
    return out


def _to_class_order(src_ref, dst_ref, n_rows, d):
    for tok, cls in _class_chunks(n_rows, d):
        dst_ref[cls, :] = src_ref[tok, :].astype(dst_ref.dtype)


def _blk_rows(t):
    return pl.ds(pl.multiple_of(t * ABLK, ABLK), ABLK)


def _head_lanes(msk, t, t_rolled):
    return jnp.where(msk, t, t_rolled)


def _zero_unless(msk, t):
    return jnp.where(msk, t, jnp.zeros_like(t))


def _attn_fwd(qn, kn, proj, v_cb, name):
    S, ad = qn.shape
    nb = S // ABLK
    nbr = len(PATTERNS)

    def body(q_ref, k_ref, v_ref, o_ref, lse_ref, qc, kc, vc, ob, mb, lb, m_s, l_s):
        lo = _lane_mask()
        qi, ki = _band_masks(ABLK, 2 * ABLK)
        band, in_cur, prev_ok = ki <= qi + ABLK, ki >= ABLK, ki >= qi
        for bi, (_, d) in enumerate(PATTERNS):
            nbc = S // d // ABLK
            first, last = bi == 0, bi == nbr - 1
            qs, ks, vs = q_ref, k_ref, v_ref
            if d > 1:
                qs, ks, vs = qc, kc, vc
                for src, dst in ((q_ref, qc), (k_ref, kc), (v_ref, vc)):
                    _to_class_order(src, dst, S, d)
            o_dst, m_dst, l_dst = (o_ref, m_s, l_s) if first else (ob, mb, lb)

            def blk(t, carry, nbc=nbc, qs=qs, ks=ks, vs=vs, o_dst=o_dst, m_dst=m_dst, l_dst=l_dst):
                rows, prow = _blk_rows(t), _blk_rows(jnp.maximum(t - 1, 0))
                has_prev = (t % nbc) != 0
                kk = jnp.concatenate([ks[prow, :], ks[rows, :]], axis=0)
                vv = jnp.concatenate([vs[prow, :], vs[rows, :]], axis=0)
                for u in range(ABLK // QTILE):
                    sub = pl.ds(pl.multiple_of(t * ABLK + u * QTILE, QTILE), QTILE)
                    sl = slice(u * QTILE, (u + 1) * QTILE)
                    valid = band[sl] & (in_cur[sl] | (prev_ok[sl] & has_prev))
                    qv = qs[sub, :]
                    os_, ms_, ls_ = [], [], []
                    for msk in (lo, jnp.logical_not(lo)):
                        s = jnp.where(valid, _dot(_zero_unless(msk, qv), kk, "nt"), NEG)
                        m = jnp.max(s, axis=1, keepdims=True)
                        p = jnp.exp(s - m)
                        os_.append(_dot(p, vv, "nn"))
                        ms_.append(m)
                        ls_.append(jnp.sum(p, axis=1, keepdims=True))
                    o_dst[sub, :] = jnp.where(lo, os_[0], os_[1])
                    m_dst[sub, :] = jnp.where(lo, ms_[0], ms_[1])
                    l_dst[sub, :] = jnp.where(lo, ls_[0], ls_[1])
                return carry

            lax.fori_loop(0, nb, blk, 0, unroll=8)
            if first:
                continue
            for tok, cls in _class_chunks(S, d):
                m_old, m_b = m_s[tok, :], mb[cls, :]
                m_new = jnp.maximum(m_old, m_b)
                a, b = jnp.exp(m_old - m_new), jnp.exp(m_b - m_new)
                l_new = a * l_s[tok, :] + b * lb[cls, :]
                o_new = a * o_ref[tok, :] + b * ob[cls, :]
                if last:
                    o_ref[tok, :] = o_new / l_new
                    lse_ref[tok, :] = m_new + jnp.log(l_new)
                else:
                    o_ref[tok, :] = o_new
                    m_s[tok, :] = m_new
                    l_s[tok, :] = l_new

    col = pl.BlockSpec((S, LANES), lambda h: (0, h))
    return pl.pallas_call(
        body, name=name, grid=(ad // LANES,),
        in_specs=[col, col, pl.BlockSpec((S, LANES), lambda h: (0, h + v_cb))], out_specs=[col, col],
        out_shape=[jax.ShapeDtypeStruct((S, ad), F32), jax.ShapeDtypeStruct((S, ad), F32)],
        scratch_shapes=[pltpu.VMEM((S, LANES), BF16)] * 3 + [pltpu.VMEM((S, LANES), F32)] * 5,
        compiler_params=_cparams(("parallel",)),
    )(qn, kn, proj)


def _attn_bwd(qn, kn, proj, v_cb, do, lse, dd, name):
    S, ad = qn.shape
    nb = S // ABLK

    def body(q_ref, k_ref, v_ref, do_ref, lse_ref, dd_ref, dq_ref, dk_ref, dv_ref,
             qc, kc, vc, doc, lsec, ddc, dqc, dkc, dvc):
        lo = _lane_mask()
        qi, ki = _band_masks(ABLK, ABLK)
        cur_ok, prev_ok = ki <= qi, ki >= qi
        for bi, (_, d) in enumerate(PATTERNS):
            nbc = S // d // ABLK
            first = bi == 0
            token_order = (q_ref, k_ref, v_ref, do_ref, lse_ref, dd_ref)
            class_order = (qc, kc, vc, doc, lsec, ddc)
            if d > 1:
                for src, dst in zip(token_order, class_order):
                    _to_class_order(src, dst, S, d)
            qs, ks, vs, dos, lses, dds = class_order if d > 1 else token_order
            dq_dst, dk_dst, dv_dst = (dq_ref, dk_ref, dv_ref) if first else (dqc, dkc, dvc)
            dk_dst[...] = jnp.zeros_like(dk_dst)
            dv_dst[...] = jnp.zeros_like(dv_dst)

            def blk(t, carry, nbc=nbc, qs=qs, ks=ks, vs=vs, dos=dos, lses=lses, dds=dds,
                    dq_dst=dq_dst, dk_dst=dk_dst, dv_dst=dv_dst):
                rows, prow = _blk_rows(t), _blk_rows(jnp.maximum(t - 1, 0))
                has_prev = (t % nbc) != 0
                qv, dov, lse_b, dd_b = qs[rows, :], dos[rows, :], lses[rows, :], dds[rows, :]
                lse_r, dd_r = pltpu.roll(lse_b, HD, 1), pltpu.roll(dd_b, HD, 1)
                heads = []
                for msk in (lo, jnp.logical_not(lo)):
                    heads.append((_zero_unless(msk, qv), _zero_unless(msk, dov), _head_lanes(msk, lse_b, lse_r),
                                  _head_lanes(msk, dd_b, dd_r)))
                dqs = [None, None]
                for krows, vmask in ((rows, cur_ok), (prow, prev_ok & has_prev)):
                    kv, vv = ks[krows, :], vs[krows, :]
                    dk = jnp.zeros((ABLK, LANES), F32)
                    dv = jnp.zeros((ABLK, LANES), F32)
                    for hi, (qh, doh, lse_h, dd_h) in enumerate(heads):
                        s = jnp.where(vmask, _dot(qh, kv, "nt"), NEG)
                        p = jnp.exp(s - lse_h)
                        ds = p * (_dot(doh, vv, "nt") - dd_h)
                        dqh = _dot(ds, kv, "nn")
                        dqs[hi] = dqh if dqs[hi] is None else dqs[hi] + dqh
                        dv = dv + _dot(p, doh, "tn")
                        dk = dk + _dot(ds, qh, "tn")
                    dk_dst[krows, :] += dk
                    dv_dst[krows, :] += dv
                dq_dst[rows, :] = jnp.where(lo, dqs[0], dqs[1])
                return carry

            lax.fori_loop(0, nb, blk, 0, unroll=4)
            if not first:
                for tok, cls in _class_chunks(S, d):
                    dq_ref[tok, :] = dq_ref[tok, :] + dqc[cls, :]
                    dk_ref[tok, :] = dk_ref[tok, :] + dkc[cls, :]
                    dv_ref[tok, :] = dv_ref[tok, :] + dvc[cls, :]

    col = pl.BlockSpec((S, LANES), lambda h: (0, h))
    col1 = pl.BlockSpec((S, LANES), lambda h: (0, h), pipeline_mode=pl.Buffered(1))
    vcol1 = pl.BlockSpec((S, LANES), lambda h: (0, h + v_cb), pipeline_mode=pl.Buffered(1))
    return pl.pallas_call(
        body, name=name, grid=(ad // LANES,),
        in_specs=[col, col, vcol1, col1, col1, col1], out_specs=[col, col, col],
        out_shape=[jax.ShapeDtypeStruct((S, ad), F32)] * 3,
        scratch_shapes=[pltpu.VMEM((S, LANES), BF16)] * 4 + [pltpu.VMEM((S, LANES), F32)] * 5,
        compiler_params=_cparams(("parallel",)),
    )(qn, kn, proj, do, lse, dd)


def _coords():
    return lax.axis_index("x"), lax.axis_index("y"), lax.axis_index("c")


def _exchange8(xs, per_dest, name):
    n = len(xs)
    blk = [x.shape[1:] if per_dest else x.shape for x in xs]

    def body(*refs):
        ins, outs = refs[:n], refs[n:2 * n]
        send_sems, recv_sems, local_sems = refs[2 * n:]
        x, y, c = _coords()
        sibling = (x, y, 1 - c)
        chips = [(1 - x, y), (x, 1 - y), (1 - x, 1 - y)]
        first, passed, mine = [], [], []
        for a in range(n):
            def src_for(cx, cy, a=a):
                return ins[a].at[2 * cx + cy] if per_dest else ins[a]

            def slot(px, py, pc, a=a):
                return outs[a].at[4 * px + 2 * py + pc]

            def copy(k, src, dst, to, a=a):
                return pltpu.make_async_remote_copy(src_ref=src, dst_ref=dst, send_sem=send_sems.at[7 * a + k],
                                                    recv_sem=recv_sems.at[7 * a + k], device_id=to, device_id_type=MESH)

            m = pltpu.make_async_copy(src_for(x, y), slot(x, y, c), local_sems.at[a])
            m.start()
            mine.append(m)
            cps = [copy(0, src_for(x, y), slot(x, y, c), sibling)]
            cps += [copy(1 + j, src_for(*chip), slot(x, y, c), (*chip, c)) for j, chip in enumerate(chips)]
            for cp in cps:
                cp.start()
            first += cps
        for a in range(n):
            def slot(px, py, pc, a=a):
                return outs[a].at[4 * px + 2 * py + pc]

            def copy(k, src, dst, to, a=a):
                return pltpu.make_async_remote_copy(src_ref=src, dst_ref=dst, send_sem=send_sems.at[7 * a + k],
                                                    recv_sem=recv_sems.at[7 * a + k], device_id=to, device_id_type=MESH)

            for j, chip in enumerate(chips):
                copy(1 + j, slot(*chip, c), slot(*chip, c), (*chip, c)).wait_recv()
                fw = copy(4 + j, slot(*chip, c), slot(*chip, c), sibling)
                fw.start()
                passed.append(fw)
        for a in range(n):
            def slot(px, py, pc, a=a):
                return outs[a].at[4 * px + 2 * py + pc]

            def copy(k, src, dst, to, a=a):
                return pltpu.make_async_remote_copy(src_ref=src, dst_ref=dst, send_sem=send_sems.at[7 * a + k],
                                                    recv_sem=recv_sems.at[7 * a + k], device_id=to, device_id_type=MESH)

            copy(0, slot(x, y, 1 - c), slot(x, y, 1 - c), sibling).wait_recv()
            for j, chip in enumerate(chips):
                copy(4 + j, slot(*chip, 1 - c), slot(*chip, 1 - c), sibling).wait_recv()
        for cp in first + passed:
            cp.wait_send()
        for m in mine:
            m.wait()

    anyspec = pl.BlockSpec(memory_space=pl.ANY)
    res = pl.pallas_call(
        body, name=name, in_specs=[anyspec] * n, out_specs=[anyspec] * n,
        out_shape=[jax.ShapeDtypeStruct((8,) + tuple(b), x.dtype) for b, x in zip(blk, xs)],
        scratch_shapes=[pltpu.SemaphoreType.DMA((7 * n,)), pltpu.SemaphoreType.DMA((7 * n,)),
                        pltpu.SemaphoreType.DMA((n,))],
    )(*xs)
    return list(res)


def _pair_swap(xs, name):
    n = len(xs)

    def body(*refs):
        ins, outs = refs[:n], refs[n:2 * n]
        send_sems, recv_sems = refs[2 * n:]
        x, y, c = _coords()
        cps = [pltpu.make_async_remote_copy(src_ref=ins[a].at[1 - c], dst_ref=outs[a], send_sem=send_sems.at[a],
                                            recv_sem=recv_sems.at[a], device_id=(x, y, 1 - c), device_id_type=MESH)
               for a in range(n)]
        for cp in cps:
            cp.start()
        for cp in cps:
            cp.wait()

    anyspec = pl.BlockSpec(memory_space=pl.ANY)
    res = pl.pallas_call(
        body, name=name, in_specs=[anyspec] * n, out_specs=[anyspec] * n,
        out_shape=[jax.ShapeDtypeStruct(x.shape[1:], x.dtype) for x in xs],
        scratch_shapes=[pltpu.SemaphoreType.DMA((n,)), pltpu.SemaphoreType.DMA((n,))],
    )(*xs)
    return list(res)


_HBM = pl.BlockSpec(memory_space=pltpu.HBM)
_SEM = pl.BlockSpec(memory_space=pltpu.SEMAPHORE)
_EFFECT = pltpu.SideEffectType.DATAFLOW_SIDE_EFFECTING


def _peer(x, y, c, j):
    dx, dy, dc = (j + 1) >> 2 & 1, (j + 1) >> 1 & 1, (j + 1) & 1
    return (1 - x if dx else x, 1 - y if dy else y, 1 - c if dc else c)


def _spread_copies(s_refs, l_refs, send_sems, recv_sems, per_dest):
    x, y, c = _coords()
    me = 4 * x + 2 * y + c
    cps = []
    for a in range(len(s_refs)):
        for j in range(7):
            tx, ty, tc = _peer(x, y, c, j)
            src = s_refs[a].at[2 * tx + ty] if per_dest else s_refs[a]
            cps.append(pltpu.make_async_remote_copy(src_ref=src, dst_ref=l_refs[a].at[me], send_sem=send_sems.at[7 * a + j],
                                                    recv_sem=recv_sems.at[7 * a + j], device_id=(tx, ty, tc),
                                                    device_id_type=MESH))
    return cps


def _spread_start(srcs, per_dest, dev, chip, name):
    n = len(srcs)
    lands = []
    for s in srcs:
        own = lax.dynamic_index_in_dim(s, chip, 0, keepdims=False) if per_dest else s
        lands.append(lax.dynamic_update_index_in_dim(lax.empty((8,) + own.shape, own.dtype), own, dev, 0))

    def body(*refs):
        s_refs, l_refs, send_sems, recv_sems, token = refs[:n], refs[n:2 * n], refs[2 * n], refs[2 * n + 1], refs[-1]
        for cp in _spread_copies(s_refs, l_refs, send_sems, recv_sems, per_dest):
            cp.start()
        token[...] = jnp.zeros_like(token)

    hbm_in = [pltpu.with_memory_space_constraint(t, pltpu.HBM) for t in list(srcs) + lands]
    outs = pl.pallas_call(
        body, name=name,
        out_shape=(pltpu.SemaphoreType.DMA((7 * n,)), pltpu.SemaphoreType.DMA((7 * n,)),
                   *[pltpu.HBM(t.shape, t.dtype) for t in hbm_in], jax.ShapeDtypeStruct((8, LANES), F32)),
        in_specs=[_HBM] * (2 * n), out_specs=(_SEM, _SEM, *[_HBM] * (2 * n), pl.BlockSpec(memory_space=pltpu.VMEM)),
        input_output_aliases={i: 2 + i for i in range(2 * n)},
        compiler_params=pltpu.CompilerParams(has_side_effects=_EFFECT),
    )(*hbm_in)
    return (outs[0], outs[1], list(outs[2:2 + n]), list(outs[2 + n:2 + 2 * n])), outs[-1]


def _spread_wait(handle, per_dest, after, name):
    send_sems, recv_sems, srcs, lands = handle
    n = len(srcs)

    def body(*refs):
        s_refs, l_refs, send_ref, recv_ref = refs[:n], refs[n:2 * n], refs[2 * n], refs[2 * n + 1]
        for cp in _spread_copies(s_refs, l_refs, send_ref, recv_ref, per_dest):
            cp.wait_send()
            cp.wait_recv()

    outs = pl.pallas_call(
        body, name=name, out_shape=tuple(pltpu.HBM(t.shape, t.dtype) for t in srcs + lands),
        in_specs=[_HBM] * (2 * n) + [_SEM, _SEM, pl.BlockSpec(memory_space=pl.ANY)], out_specs=tuple([_HBM] * (2 * n)),
        input_output_aliases={i: i for i in range(2 * n)},
        compiler_params=pltpu.CompilerParams(has_side_effects=_EFFECT),
    )(*srcs, *lands, send_sems, recv_sems, after)
    return list(outs[n:])


def _pair_add(g2, theirs, half, name, tm=256):
    _, n, cdim = g2.shape
    tm = min(tm, n)

    def body(h_ref, a_ref, b_ref, o_ref):
        o_ref[...] = (a_ref[...] + b_ref[...]).astype(o_ref.dtype)

    grid_spec = pltpu.PrefetchScalarGridSpec(
        num_scalar_prefetch=1, grid=(n // tm,),
        in_specs=[pl.BlockSpec((None, tm, cdim), lambda i, h: (h[0], i, 0)), pl.BlockSpec((tm, cdim), lambda i, h: (i, 0))],
        out_specs=pl.BlockSpec((tm, cdim), lambda i, h: (i, 0)))
    return pl.pallas_call(body, name=name, grid_spec=grid_spec, out_shape=jax.ShapeDtypeStruct((n, cdim), BF16),
                          compiler_params=_cparams(("parallel",)))(half.reshape(1).astype(jnp.int32), g2, theirs)


def _adamw_math(w, g, m, v):
    m = ADAM_B1 * m + (1.0 - ADAM_B1) * g
    v = ADAM_B2 * v + (1.0 - ADAM_B2) * (g * g)
    m_hat = m / (1.0 - ADAM_B1 ** ADAM_STEP)
    v_hat = v / (1.0 - ADAM_B2 ** ADAM_STEP)
    delta = -ADAM_LR * (m_hat / (jnp.sqrt(v_hat) + ADAM_EPS) + ADAM_WD * w)
    return delta, m, v


def _adamw(parts, w, m, v, name, tm=128):
    npart, R, C = parts.shape
    tm = min(tm, R)

    def body(p_ref, w_ref, m_ref, v_ref, g_out, d_out, m_out, v_out):
        g = p_ref[0].astype(F32)
        for i in range(1, npart):
            g = g + p_ref[i].astype(F32)
        d, mm, vv = _adamw_math(w_ref[...], g, m_ref[...], v_ref[...])
        g_out[...] = g
        d_out[...] = d
        m_out[...] = mm
        v_out[...] = vv

    spec = pl.BlockSpec((tm, C), lambda i: (i, 0))
    return pl.pallas_call(
        body, name=name, grid=(R // tm,),
        in_specs=[pl.BlockSpec((npart, tm, C), lambda i: (0, i, 0)), spec, spec, spec], out_specs=[spec] * 4,
        out_shape=[jax.ShapeDtypeStruct((R, C), F32)] * 4,
        compiler_params=_cparams(("parallel",)),
    )(parts, w, m, v)


def _sum_parts(parts, name):
    npart, R, C = parts.shape

    def body(p_ref, o_ref):
        g = p_ref[0]
        for i in range(1, npart):
            g = g + p_ref[i]
        o_ref[...] = g

    return pl.pallas_call(body, name=name, out_shape=jax.ShapeDtypeStruct((R, C), F32))(parts)


def _mod_fwd(c_all, w_ada, b_sh, name):
    def body(c_ref, w_ref, b_ref, o_ref):
        o_ref[...] = _dot(_silu(c_ref[...]), w_ref[...], "nn") + b_ref[...]

    return pl.pallas_call(body, name=name, out_shape=jax.ShapeDtypeStruct((c_all.shape[0], w_ada.shape[1]), F32),
                          compiler_params=pltpu.CompilerParams(vmem_limit_bytes=VMEM_LIMIT))(c_all, w_ada, b_sh)


def _mod_wgrad(c_all, dmod_sh, name):
    def body(c_ref, d_ref, o_ref):
        o_ref[...] = _dot(_silu(c_ref[...]), d_ref[...], "tn")

    return pl.pallas_call(body, name=name, out_shape=jax.ShapeDtypeStruct((c_all.shape[1], dmod_sh.shape[1]), F32),
                          compiler_params=pltpu.CompilerParams(vmem_limit_bytes=VMEM_LIMIT))(c_all, dmod_sh)


def _pad_lanes(v):
    return jnp.pad(v, ((0, 0), (0, (-v.shape[1]) % LANES)))


def kernel(x, c, norm1_w, norm2_w, w_ada, b_ada, w_in, conv_w, conv_b, dt_bias, a_log, d_skip, ssd_norm_w, q_norm_w, k_norm_w, attn_norm_w, w_out, w_ff1, w_ff2, loss_target, m_norm1_w, m_norm2_w, m_w_ada, m_b_ada, m_w_in, m_conv_w, m_conv_b, m_dt_bias, m_a_log, m_d_skip, m_ssd_norm_w, m_q_norm_w, m_k_norm_w, m_attn_norm_w, m_w_out, m_w_ff1, m_w_ff2, v_norm1_w, v_norm2_w, v_w_ada, v_b_ada, v_w_in, v_conv_w, v_conv_b, v_dt_bias, v_a_log, v_d_skip, v_ssd_norm_w, v_q_norm_w, v_k_norm_w, v_attn_norm_w, v_w_out, v_w_ff1, v_w_ff2):
    xi, yi, ci = _coords()
    chip = 2 * xi + yi
    dev = 2 * chip + ci
    xs, tgt = x[0], loss_target[0]
    S, D = xs.shape
    DI, AD = NH_SSD * HD, NH_ATT * HD
    CC = DI + 2 * NG * NSTATE
    PW = DI + CC + 3 * AD + LANES
    DFF = w_ff1.shape[2] * 4
    MIX = DI + AD
    o_xbc, o_q, o_k, o_v, o_dt = DI, DI + CC, DI + CC + AD, DI + CC + 2 * AD, DI + CC + 3 * AD

    def half_rows(w):
        r = w.shape[0] // 2
        return lax.dynamic_slice_in_dim(w, ci * r, r, 0).astype(BF16)

    c_all, conv_w_all = _exchange8([c, conv_w[0]], False, "gather_c_conv_w")
    c_all = c_all.reshape(8, D)
    c_all = jnp.pad(c_all, ((0, 8), (0, 0)))
    nmod = w_ada.shape[2]
    b_sh = lax.dynamic_slice_in_dim(b_ada, chip * nmod, nmod, 1)
    mod_sh = _mod_fwd(c_all, w_ada[0], b_sh, "mod_fwd")
    mod_all = _exchange8([mod_sh[:8]], False, "gather_mod")[0]
    mod_me = lax.dynamic_index_in_dim(mod_all[0::2], dev, 1, keepdims=False).reshape(1, 4 * nmod)
    shift1, scale1, gate1, shift2, scale2, gate2 = [mod_me[:, i * D:(i + 1) * D] for i in range(6)]

    g_in = _exchange8([half_rows(w_in[0])], False, "gather_w_in")[0]
    rest_handle, rest_token = _spread_start([half_rows(w_out[0]), half_rows(w_ff1[0]), half_rows(w_ff2[0])], False, dev,
                                            chip, "gather_rest_start")
    shift1 = shift1 + rest_token[0, 0]
    wsh = w_in.shape[2]
    w_in_f = g_in.reshape(4, D, wsh).transpose(1, 0, 2).reshape(D, 4 * wsh)
    n_zx = DI + CC
    w_proj = jnp.concatenate([w_in_f[:, :n_zx], w_in_f[:, n_zx + NH_SSD:], w_in_f[:, n_zx:n_zx + NH_SSD],
                              jnp.zeros((D, LANES - NH_SSD), BF16)], axis=1)

    dtb, alog, dsk = _pad_lanes(dt_bias), _pad_lanes(a_log), _pad_lanes(d_skip)
    qw2 = jnp.concatenate([q_norm_w, q_norm_w], axis=1)
    kw2 = jnp.concatenate([k_norm_w, k_norm_w], axis=1)
    conv_w_f = conv_w_all[0::2].transpose(1, 0, 2).reshape(KCONV, CC)

    h1 = _rows("norm1", lambda r, k: ([_normmod(r[0], *k)], []), [(xs, 0, D)], [norm1_w, scale1, shift1],
               [(D, BF16)], [], S)[0]
    proj = _matmul(h1, w_proj, "nn", F32, "in_proj", tn=896)
    xbc = _conv_fwd(proj, o_xbc, CC, conv_w_f, conv_b, "conv_fwd")
    y_ssd, hsave = _ssd_fwd(xbc, proj, o_dt // LANES, dtb, alog, dsk, ssd_norm_w, "ssd_fwd")

    def qk_call(name, col0, w2, scale):
        def body(t_ref, w_ref, o_ref):
            o_ref[...] = _headnorm(t_ref[...], w_ref[...], scale)
        return pl.pallas_call(
            body, name=name, grid=(AD // LANES,),
            in_specs=[pl.BlockSpec((S, LANES), lambda j: (0, j + col0 // LANES)),
                      pl.BlockSpec((1, LANES), lambda j: (0, 0))],
            out_specs=pl.BlockSpec((S, LANES), lambda j: (0, j)),
            out_shape=jax.ShapeDtypeStruct((S, AD), F32), compiler_params=_cparams(("parallel",)),
        )(proj, w2)

    qn = qk_call("q_norm", o_q, qw2, HD ** -0.5)
    kn = qk_call("k_norm", o_k, kw2, 1.0)
    o_att, lse = _attn_fwd(qn, kn, proj, o_v // LANES, "attn_fwd")
    y_att = _rows("attn_out_norm", lambda r, k: ([_rmsw(r[0], k[0])], []), [(o_att, 0, AD)], [attn_norm_w],
                  [(AD, BF16)], [], S)[0]
    g_out, g_ff1, g_ff2 = _spread_wait(rest_handle, False, o_att, "gather_rest_wait")
    w_out_f = g_out.reshape(MIX, D)
    w_out_a, w_out_b = w_out_f[:DI], w_out_f[DI:]
    w_ff1_f = g_ff1.reshape(4, D, DFF // 4).transpose(1, 0, 2).reshape(D, DFF)
    w_ff2_f = g_ff2.reshape(DFF, D)
    mix_a = _matmul(y_ssd, w_out_a, "nn", F32, "out_proj_a")
    mix = _matmul(y_att, w_out_b, "nn", F32, "out_proj_b", epilogue=lambda r, e: r + e, extras=(mix_a,))
    x2, h2 = _rows("resid_norm2", lambda r, k: (list(_resid_normmod(r[0], r[1], *k)), []), [(xs, 0, D), (mix, 0, D)],
                   [gate1, norm2_w, scale2, shift2], [(D, F32), (D, BF16)], [], S)
    u = _matmul(h2, w_ff1_f, "nn", F32, "ff1")
    relu2 = lambda t: jnp.square(jnp.maximum(t, 0.0))
    ff = _matmul(u, w_ff2_f, "nn", F32, "ff2", a_fn=relu2)

    def loss_fn(r, k):
        x2_, ff_, t_ = r
        err = x2_ + k[0] * ff_ - t_
        dy_ = err * (1.0 / D)
        ls = jnp.sum(jnp.sum(0.5 * err * err, axis=1, keepdims=True), axis=0, keepdims=True) * (1.0 / D)
        return [dy_, dy_ * k[0]], [ls, jnp.sum(dy_ * ff_, axis=0, keepdims=True)]

    dy, dff, loss_p, dgate2 = _rows("loss", loss_fn, [(x2, 0, D), (ff, 0, D), (tgt, 0, D)], [gate2],
                                    [(D, F32), (D, BF16)], [(1, 1), (1, D)], S)
    du = _matmul(dff, w_ff2_f, "nt", BF16, "ff2_dx", epilogue=lambda r, e: r * (2.0 * jnp.maximum(e, 0.0)), extras=(u,))
    gw_ff2 = _matmul(u, dff, "tn", F32, "ff2_dw", a_fn=relu2)
    gw_ff1 = _matmul(h2, du, "tn", F32, "ff1_dw")

    def by_half_cols(g):
        r, c4 = g.shape
        return g.reshape(2, r // 2, 4, c4 // 4).transpose(0, 2, 1, 3)

    def by_half_rows(g):
        r4, cdim = g.shape
        return g.reshape(4, 2, r4 // 8, cdim).transpose(1, 0, 2, 3)

    def scatter_start(layouts, tag):
        theirs = _pair_swap(layouts, "pair_swap_" + tag)
        sums = []
        for i, (g2, t) in enumerate(zip(layouts, theirs)):
            _, r2, cdim = t.shape
            sm = _pair_add(g2.reshape(2, 4 * r2, cdim), t.reshape(4 * r2, cdim), ci, "pair_add_%s_%d" % (tag, i))
            sums.append(sm.reshape(4, r2, cdim))
        return _spread_start(sums, True, dev, chip, "scatter_%s_start" % tag)

    def scatter_wait(handle, after, tag):
        return [s.reshape(4, 2 * s.shape[1], s.shape[2]) for s in _spread_wait(handle, True, after, "scatter_%s_wait" % tag)]

    ff_handle, ff_token = scatter_start([by_half_cols(gw_ff1), by_half_rows(gw_ff2)], "ff")
    dh2 = _matmul(du, w_ff1_f, "nt", F32, "ff1_dx")

    def resid_bwd(r, k):
        x_, mix_, dx2a, dh2_ = r
        _, vjp = jax.vjp(_resid_normmod, x_, mix_, *k)
        dx, dmix_, dg, dnw, dsc, dsh = vjp((dx2a, dh2_))
        return [dx, dmix_], [dg, dnw, dsc, dsh]

    dx2, dmix, dgate1, g_norm2, dscale2, dshift2 = _rows(
        "resid_norm2_bwd", resid_bwd, [(xs, 0, D), (mix, 0, D), (dy, 0, D), (dh2, 0, D)],
        [gate1 + ff_token[0, 0], norm2_w, scale2, shift2], [(D, F32), (D, BF16)], [(1, D)] * 4, S)
    gw_out = jnp.concatenate([_matmul(y_ssd, dmix, "tn", F32, "out_proj_dw_a"),
                              _matmul(y_att, dmix, "tn", F32, "out_proj_dw_b")], axis=0)
    out_handle, out_token = scatter_start([by_half_rows(gw_out)], "out")
    dy_ssd = _matmul(dmix, w_out_a, "nt", F32, "out_proj_dx_a")
    dy_att = _matmul(dmix, w_out_b, "nt", F32, "out_proj_dx_b")

    def attn_norm_bwd(r, k):
        o_, dyo = r
        _, vjp = jax.vjp(_rmsw, o_, k[0])
        do_, dw_ = vjp(dyo)
        lo = _lane_mask()
        dd_blocks = []
        for b in range(AD // LANES):
            t = (do_ * o_)[:, b * LANES:(b + 1) * LANES]
            s0 = jnp.sum(jnp.where(lo, t, 0.0), axis=1, keepdims=True)
            s1 = jnp.sum(jnp.where(lo, 0.0, t), axis=1, keepdims=True)
            dd_blocks.append(jnp.where(lo, s0, s1))
        return [do_, jnp.concatenate(dd_blocks, axis=1)], [dw_]

    do_att, dd_att, g_attn_norm = _rows("attn_norm_bwd", attn_norm_bwd, [(o_att, 0, AD), (dy_att, 0, AD)],
                                        [attn_norm_w + out_token[0, 0]], [(AD, F32), (AD, F32)], [(1, AD)], S)
    dq_n, dk_n, dv = _attn_bwd(qn, kn, proj, o_v // LANES, do_att, lse, dd_att, "attn_bwd")

    def qk_bwd_call(name, col0, w2, scale, g):
        def body(t_ref, w_ref, g_ref, o_ref, dw_ref):
            @pl.when(pl.program_id(0) == 0)
            def _():
                dw_ref[...] = jnp.zeros_like(dw_ref)
            _, vjp = jax.vjp(lambda t, w: _headnorm(t, w, scale), t_ref[...], w_ref[...])
            dt_, dw_ = vjp(g_ref[...])
            o_ref[...] = dt_.astype(BF16)
            dw_ref[...] += dw_
        blk = pl.BlockSpec((S, LANES), lambda j: (0, j))
        return pl.pallas_call(
            body, name=name, grid=(AD // LANES,),
            in_specs=[pl.BlockSpec((S, LANES), lambda j: (0, j + col0 // LANES)),
                      pl.BlockSpec((1, LANES), lambda j: (0, 0)), blk],
            out_specs=[blk, pl.BlockSpec((1, LANES), lambda j: (0, 0))],
            out_shape=[jax.ShapeDtypeStruct((S, AD), BF16), jax.ShapeDtypeStruct((1, LANES), F32)],
            compiler_params=_cparams(("arbitrary",)),
        )(proj, w2, g)

    dq, g_qw2 = qk_bwd_call("q_norm_bwd", o_q, qw2, HD ** -0.5, dq_n)
    dk, g_kw2 = qk_bwd_call("k_norm_bwd", o_k, kw2, 1.0, dk_n)
    g_q_norm = g_qw2[:, :HD] + g_qw2[:, HD:]
    g_k_norm = g_kw2[:, :HD] + g_kw2[:, HD:]

    dxbc, dz, ddtr, g_dtb, g_alog, g_dsk, g_ssd_norm = _ssd_bwd(
        xbc, proj, o_dt // LANES, dtb, alog, dsk, ssd_norm_w, hsave, dy_ssd, "ssd_bwd")
    dxbc_pre, g_conv_w, g_conv_b = _conv_bwd(proj, o_xbc, CC, conv_w_f, conv_b, dxbc, "conv_bwd")
    dproj = jnp.concatenate([dz.astype(BF16), dxbc_pre.astype(BF16), dq, dk, dv.astype(BF16), ddtr.astype(BF16)], axis=1)
    gw_proj = _matmul(h1, dproj, "tn", F32, "in_proj_dw", tn=896)
    gw_in = jnp.concatenate([gw_proj[:, :n_zx], gw_proj[:, o_dt:o_dt + NH_SSD], gw_proj[:, n_zx:o_dt]], axis=1)
    in_handle, in_token = scatter_start([by_half_cols(gw_in)], "in")
    dh1 = _matmul(dproj, w_proj, "nt", F32, "in_proj_dx", tk=896)

    def norm1_bwd(r, k):
        x_, dh_, dres = r
        _, vjp = jax.vjp(_normmod, x_, *k)
        dx, dnw, dsc, dsh = vjp(dh_)
        return [dx + dres], [dnw, dsc, dsh]

    grad_x, g_norm1, dscale1, dshift1 = _rows("norm1_bwd", norm1_bwd, [(xs, 0, D), (dh1, 0, D), (dx2, 0, D)],
                                              [norm1_w + in_token[0, 0], scale1, shift1], [(D, F32)], [(1, D)] * 3, S)
    dmod =jnp.concatenate([dshift1, dscale1, dgate1, dshift2, dscale2, dgate2], axis=1)

    small = [g_norm1, g_norm2, dmod, g_conv_b, g_dtb, g_alog, g_dsk, g_ssd_norm, _pad_lanes(g_q_norm),
             _pad_lanes(g_k_norm), g_attn_norm, g_conv_w.reshape(1, KCONV * CC)]
    sizes = [t.shape[1] for t in small]
    packed = jnp.concatenate(small, axis=1)
    nrow = -(-packed.shape[1] // LANES // 8) * 8
    packed = jnp.pad(packed, ((0, 0), (0, nrow * LANES - packed.shape[1]))).reshape(nrow, LANES)
    packed_all = _exchange8([packed], False, "gather_small_grads")[0]
    tot = _sum_parts(packed_all, "sum_small_grads").reshape(1, nrow * LANES)
    offs = [sum(sizes[:i]) for i in range(len(sizes))]
    (g_norm1, g_norm2, g_b_ada, g_conv_b, g_dtb, g_alog, g_dsk, g_ssd_norm, g_q_norm, g_k_norm, g_attn_norm,
     g_conv_w) = [tot[:, o:o + n] for o, n in zip(offs, sizes)]
    g_dtb, g_alog, g_dsk = g_dtb[:, :NH_SSD], g_alog[:, :NH_SSD], g_dsk[:, :NH_SSD]
    g_q_norm, g_k_norm = g_q_norm[:, :HD], g_k_norm[:, :HD]
    ccs = CC // 4
    g_conv_w = lax.dynamic_slice_in_dim(g_conv_w.reshape(KCONV, CC), chip * ccs, ccs, 1)

    dmod_all = packed_all.reshape(8, nrow * LANES)[:, offs[2]:offs[2] + 6 * D]
    dmod_sh = jnp.pad(lax.dynamic_slice_in_dim(dmod_all, chip * nmod, nmod, 1), ((0, 8), (0, 0)))
    gw_ada = _mod_wgrad(c_all, dmod_sh, "mod_wgrad")

    parts_ff = scatter_wait(ff_handle, grad_x, "ff")
    parts = scatter_wait(in_handle, grad_x, "in") + scatter_wait(out_handle, grad_x, "out") + parts_ff

    res_in = _adamw(parts[0], w_in[0], m_w_in[0], v_w_in[0], "adamw_w_in")
    res_out = _adamw(parts[1], w_out[0], m_w_out[0], v_w_out[0], "adamw_w_out")
    res_ff1 = _adamw(parts[2], w_ff1[0], m_w_ff1[0], v_w_ff1[0], "adamw_w_ff1")
    res_ff2 = _adamw(parts[3], w_ff2[0], m_w_ff2[0], v_w_ff2[0], "adamw_w_ff2")
    res_ada = _adamw(gw_ada[None], w_ada[0], m_w_ada[0], v_w_ada[0], "adamw_w_ada")

    small_names = ["norm1_w", "norm2_w", "b_ada", "conv_w", "conv_b", "dt_bias", "a_log", "d_skip", "ssd_norm_w",
                   "q_norm_w", "k_norm_w", "attn_norm_w"]
    small_g = dict(norm1_w=g_norm1, norm2_w=g_norm2, b_ada=g_b_ada, conv_w=g_conv_w.reshape(1, KCONV * ccs),
                   conv_b=g_conv_b, dt_bias=g_dtb, a_log=g_alog, d_skip=g_dsk, ssd_norm_w=g_ssd_norm, q_norm_w=g_q_norm,
                   k_norm_w=g_k_norm, attn_norm_w=g_attn_norm)
    small_w = dict(norm1_w=(norm1_w, m_norm1_w, v_norm1_w), norm2_w=(norm2_w, m_norm2_w, v_norm2_w),
                   b_ada=(b_ada, m_b_ada, v_b_ada),
                   conv_w=tuple(t.reshape(1, KCONV * ccs) for t in (conv_w, m_conv_w, v_conv_w)),
                   conv_b=(conv_b, m_conv_b, v_conv_b), dt_bias=(dt_bias, m_dt_bias, v_dt_bias),
                   a_log=(a_log, m_a_log, v_a_log), d_skip=(d_skip, m_d_skip, v_d_skip),
                   ssd_norm_w=(ssd_norm_w, m_ssd_norm_w, v_ssd_norm_w), q_norm_w=(q_norm_w, m_q_norm_w, v_q_norm_w),
                   k_norm_w=(k_norm_w, m_k_norm_w, v_k_norm_w), attn_norm_w=(attn_norm_w, m_attn_norm_w, v_attn_norm_w))
    ssz = [_pad_lanes(small_g[n]).shape[1] for n in small_names]
    soff = [sum(ssz[:i]) for i in range(len(ssz))]
    srow = -(-sum(ssz) // LANES // 8) * 8

    def pack(ts, fill):
        t = jnp.concatenate([jnp.pad(t, ((0, 0), (0, (-t.shape[1]) % LANES)), constant_values=fill) for t in ts], axis=1)
        return jnp.pad(t, ((0, 0), (0, srow * LANES - t.shape[1])), constant_values=fill).reshape(srow, LANES)

    sg = pack([small_g[n] for n in small_names], 0.0)
    sw = pack([small_w[n][0] for n in small_names], 0.0)
    sm_ = pack([small_w[n][1] for n in small_names], 0.0)
    sv = pack([small_w[n][2] for n in small_names], 1.0)
    _, s_delta, s_m, s_v = _adamw(sg[None], sw, sm_, sv, "adamw_small", tm=srow)

    def unpack(t, n):
        i = small_names.index(n)
        return t.reshape(1, srow * LANES)[:, soff[i]:soff[i] + small_g[n].shape[1]].reshape(small_w[n][0].shape)

    loss = lax.psum(loss_p[0, 0], ("x", "y", "c"))
    big_res = dict(w_ada=res_ada, w_in=res_in, w_out=res_out, w_ff1=res_ff1, w_ff2=res_ff2)
    order = ["norm1_w", "norm2_w", "w_ada", "b_ada", "w_in", "conv_w", "conv_b", "dt_bias", "a_log", "d_skip",
             "ssd_norm_w", "q_norm_w", "k_norm_w", "attn_norm_w", "w_out", "w_ff1", "w_ff2"]
    grads, deltas, new_m, new_v = [], [], [], []
    for n in order:
        if n in big_res:
            g_, d_, m_, v_ = [t[None] for t in big_res[n]]
        else:
            g_ = small_g[n].reshape(small_w[n][0].shape)
            d_, m_, v_ = unpack(s_delta, n), unpack(s_m, n), unpack(s_v, n)
            if n == "conv_w":
                g_, d_, m_, v_ = [t.reshape(conv_w.shape) for t in (g_, d_, m_, v_)]
        grads.append(g_)
        deltas.append(d_)
        new_m.append(m_)
        new_v.append(v_)
    return (loss, grad_x[None], *grads, *deltas, *new_m, *new_v)
```

```python
import functools

import jax
import jax.numpy as jnp
from jax import lax
from jax.experimental import pallas as pl
from jax.experimental.pallas import tpu as pltpu

F32, BF16 = jnp.float32, jnp.bfloat16
EPS = 1e-6
HD = 64
NH_SSD = 16
NG = 4
NSTATE = 128
KCONV = 4
CHUNK = 128
NH_ATT = 16
PATTERNS = ((128, 1), (512, 4), (2048, 16))
ABLK = 128
QTILE = 128
LANES = 128
ADAM_LR, ADAM_B1, ADAM_B2, ADAM_EPS, ADAM_WD, ADAM_STEP = 0.001, 0.9, 0.999, 1e-08, 0.01, 10
VMEM_LIMIT = 56 * 1024 * 1024
MESH = pl.DeviceIdType.MESH
NEG = -1e30

_DN = {"nn": (((1,), (0,)), ((), ())), "nt": (((1,), (1,)), ((), ())), "tn": (((0,), (0,)), ((), ()))}


def _cparams(sem):
    return pltpu.CompilerParams(dimension_semantics=sem, vmem_limit_bytes=VMEM_LIMIT)


def _tile(n, cap):
    if n % LANES or n <= LANES:
        return n
    best = LANES
    for t in range(LANES, min(n, cap) + 1, LANES):
        if n % t == 0:
            best = t
    return best


def _silu(x):
    return x / (1.0 + jnp.exp(-x))


def _softplus(x):
    return jnp.maximum(x, 0.0) + jnp.log(1.0 + jnp.exp(-jnp.abs(x)))


def _dot(a, b, dims):
    return lax.dot_general(a.astype(BF16), b.astype(BF16), _DN[dims], preferred_element_type=F32)


def _matmul(a, b, dims, out_dtype, name, a_fn=None, epilogue=None, extras=(), tm=1024, tn=1024, tk=1024):
    if dims == "nn":
        (M, K), (_, N) = a.shape, b.shape
    elif dims == "nt":
        (M, K), (N, _) = a.shape, b.shape
    else:
        (K, M), (_, N) = a.shape, b.shape
    tm, tn, tk = _tile(M, tm), _tile(N, tn), _tile(K, tk)
    nk = K // tk
    ne = len(extras)

    def body(a_ref, b_ref, *rest):
        e_refs, o_ref = rest[:ne], rest[ne]
        av = a_ref[...]
        if a_fn is not None:
            av = a_fn(av)
        part = _dot(av, b_ref[...], dims)

        def finish(r):
            if epilogue is not None:
                r = epilogue(r, *[e[...] for e in e_refs])
            o_ref[...] = r.astype(out_dtype)

        if nk == 1:
            finish(part)
            return
        acc = rest[ne + 1]
        k = pl.program_id(2)

        @pl.when(k == 0)
        def _():
            acc[...] = part

        @pl.when(k > 0)
        def _():
            acc[...] += part

        @pl.when(k == nk - 1)
        def _():
            finish(acc[...])

    a_spec = pl.BlockSpec((tk, tm), lambda i, j, k: (k, i)) if dims == "tn" else pl.BlockSpec((tm, tk), lambda i, j, k: (i, k))
    b_spec = pl.BlockSpec((tn, tk), lambda i, j, k: (j, k)) if dims == "nt" else pl.BlockSpec((tk, tn), lambda i, j, k: (k, j))
    o_spec = pl.BlockSpec((tm, tn), lambda i, j, k: (i, j))
    return pl.pallas_call(
        body, name=name, grid=(M // tm, N // tn, nk),
        in_specs=[a_spec, b_spec] + [o_spec] * ne, out_specs=o_spec,
        out_shape=jax.ShapeDtypeStruct((M, N), out_dtype),
        scratch_shapes=[pltpu.VMEM((tm, tn), F32)] if nk > 1 else [],
        compiler_params=_cparams(("parallel", "parallel", "arbitrary")),
    )(a, b, *extras)


def _rows(name, fn, rows, consts, outs, accs, n_rows, tm=256):
    tm = min(tm, n_rows)
    nr, nc, no, na = len(rows), len(consts), len(outs), len(accs)

    def body(*refs):
        r_refs, c_refs = refs[:nr], refs[nr:nr + nc]
        o_refs, a_refs = refs[nr + nc:nr + nc + no], refs[nr + nc + no:]
        o_vals, a_vals = fn([r[...] for r in r_refs], [c[...] for c in c_refs])
        for ref, val in zip(o_refs, o_vals):
            ref[...] = val.astype(ref.dtype)
        if na:
            @pl.when(pl.program_id(0) == 0)
            def _():
                for ref in a_refs:
                    ref[...] = jnp.zeros_like(ref)
            for ref, val in zip(a_refs, a_vals):
                ref[...] += val

    in_specs = [pl.BlockSpec((tm, w), lambda i, cb=cb: (i, cb)) for (_, cb, w) in rows]
    in_specs += [pl.BlockSpec(cst.shape, lambda i, nd=cst.ndim: (0,) * nd) for cst in consts]
    out_specs = [pl.BlockSpec((tm, w), lambda i: (i, 0)) for (w, _) in outs]
    out_specs += [pl.BlockSpec(s, lambda i: (0, 0)) for s in accs]
    out_shape = [jax.ShapeDtypeStruct((n_rows, w), dt) for (w, dt) in outs]
    out_shape += [jax.ShapeDtypeStruct(s, F32) for s in accs]
    res = pl.pallas_call(
        body, name=name, grid=(n_rows // tm,), in_specs=in_specs, out_specs=out_specs, out_shape=out_shape,
        compiler_params=_cparams(("arbitrary",)),
    )(*[r[0] for r in rows], *consts)
    return res


def _normmod(x, nw, sc, sh):
    r = lax.rsqrt(jnp.mean(x * x, axis=-1, keepdims=True) + EPS)
    return (x * r) * nw * (1.0 + sc) + sh


def _resid_normmod(x, mix, g, nw, sc, sh):
    x2 = x + g * mix
    return x2, _normmod(x2, nw, sc, sh)


def _rmsw(o, w):
    return o * lax.rsqrt(jnp.mean(o * o, axis=-1, keepdims=True) + EPS) * w


def _lane_mask():
    return lax.broadcasted_iota(jnp.int32, (1, LANES), 1) < HD


def _headnorm(t, w, scale):
    lo = _lane_mask()
    t2 = t * t
    s0 = jnp.sum(jnp.where(lo, t2, 0.0), axis=1, keepdims=True)
    s1 = jnp.sum(jnp.where(lo, 0.0, t2), axis=1, keepdims=True)
    ms = jnp.where(lo, s0, s1) * (1.0 / HD)
    return t * lax.rsqrt(ms + EPS) * w * scale


def _conv_cols(n_ch):
    return _tile(n_ch, 256)


def _conv_fwd(proj, col0, n_ch, conv_w, conv_b, name):
    S = proj.shape[0]
    tc = _conv_cols(n_ch)

    def body(u_ref, w_ref, b_ref, o_ref):
        u = u_ref[...]
        row = lax.broadcasted_iota(jnp.int32, u.shape, 0)
        acc = b_ref[...] + w_ref[KCONV - 1:KCONV, :] * u
        for i in range(KCONV - 1):
            sh = KCONV - 1 - i
            acc = acc + w_ref[i:i + 1, :] * jnp.where(row >= sh, pltpu.roll(u, sh, 0), 0.0)
        o_ref[...] = _silu(acc)

    return pl.pallas_call(
        body, name=name, grid=(n_ch // tc,),
        in_specs=[pl.BlockSpec((S, tc), lambda j: (0, j + col0 // tc)),
                  pl.BlockSpec((KCONV, tc), lambda j: (0, j)), pl.BlockSpec((1, tc), lambda j: (0, j))],
        out_specs=pl.BlockSpec((S, tc), lambda j: (0, j)),
        out_shape=jax.ShapeDtypeStruct((S, n_ch), F32),
        compiler_params=_cparams(("parallel",)),
    )(proj, conv_w, conv_b)


def _conv_bwd(proj, col0, n_ch, conv_w, conv_b, dxbc, name):
    S = proj.shape[0]
    tc = _conv_cols(n_ch)

    def body(u_ref, w_ref, b_ref, g_ref, du_ref, dw_ref, db_ref):
        u = u_ref[...]
        row = lax.broadcasted_iota(jnp.int32, u.shape, 0)
        shifted = [jnp.where(row >= s, pltpu.roll(u, s, 0), 0.0) for s in range(1, KCONV)]
        acc = b_ref[...] + w_ref[KCONV - 1:KCONV, :] * u
        for i in range(KCONV - 1):
            acc = acc + w_ref[i:i + 1, :] * shifted[KCONV - 2 - i]
        sig = 1.0 / (1.0 + jnp.exp(-acc))
        dacc = g_ref[...] * (sig * (1.0 + acc * (1.0 - sig)))
        db_ref[...] = jnp.sum(dacc, axis=0, keepdims=True)
        du = w_ref[KCONV - 1:KCONV, :] * dacc
        dw_ref[KCONV - 1:KCONV, :] = jnp.sum(dacc * u, axis=0, keepdims=True)
        for i in range(KCONV - 1):
            sh = KCONV - 1 - i
            dw_ref[i:i + 1, :] = jnp.sum(dacc * shifted[sh - 1], axis=0, keepdims=True)
            du = du + w_ref[i:i + 1, :] * jnp.where(row < S - sh, pltpu.roll(dacc, S - sh, 0), 0.0)
        du_ref[...] = du

    return pl.pallas_call(
        body, name=name, grid=(n_ch // tc,),
        in_specs=[pl.BlockSpec((S, tc), lambda j: (0, j + col0 // tc)),
                  pl.BlockSpec((KCONV, tc), lambda j: (0, j)), pl.BlockSpec((1, tc), lambda j: (0, j)),
                  pl.BlockSpec((S, tc), lambda j: (0, j))],
        out_specs=[pl.BlockSpec((S, tc), lambda j: (0, j)), pl.BlockSpec((KCONV, tc), lambda j: (0, j)),
                   pl.BlockSpec((1, tc), lambda j: (0, j))],
        out_shape=[jax.ShapeDtypeStruct((S, n_ch), F32), jax.ShapeDtypeStruct((KCONV, n_ch), F32),
                   jax.ShapeDtypeStruct((1, n_ch), F32)],
        compiler_params=_cparams(("parallel",)),
    )(proj, conv_w, conv_b, dxbc)


@functools.partial(jax.custom_vjp, nondiff_argnums=(2,))
def _mm(a, b, dims):
    return _dot(a, b, dims)


def _mm_fwd(a, b, dims):
    return _dot(a, b, dims), (a, b)


def _mm_bwd(dims, res, g):
    a, b = res
    if dims == "nn":
        return _dot(g, b, "nt"), _dot(a, g, "tn")
    if dims == "nt":
        return _dot(g, b, "nn"), _dot(g, a, "tn")
    return _dot(b, g, "nt"), _dot(a, g, "nn")


_mm.defvjp(_mm_fwd, _mm_bwd)


def _tri_dot(x, upper):
    n = x.shape[0]
    r = lax.broadcasted_iota(jnp.int32, (n, n), 0)
    c = lax.broadcasted_iota(jnp.int32, (n, n), 1)
    t = jnp.where((r <= c) if upper else (r >= c), 1.0, 0.0)
    return lax.dot_general(t, x, _DN["nn"], precision=lax.Precision.HIGHEST, preferred_element_type=F32)


@jax.custom_vjp
def _cumsum_rows(x):
    return _tri_dot(x, False)


_cumsum_rows.defvjp(lambda x: (_tri_dot(x, False), None), lambda _, g: (_tri_dot(g, True),))


def _ssd_chunk(xs_p, bm_g, cm_g, dtr, z_p, dtb, alog, dsk, nw_p, h_p):
    L = dtr.shape[0]
    n_pairs = len(xs_p)
    ppg = n_pairs // len(bm_g)
    lane = lax.broadcasted_iota(jnp.int32, (1, LANES), 1)
    sub = lax.broadcasted_iota(jnp.int32, (LANES, 1), 0)
    lo = lane < HD
    row_l = lax.broadcasted_iota(jnp.int32, (L, 1), 0)
    tri = lax.broadcasted_iota(jnp.int32, (L, L), 0) >= lax.broadcasted_iota(jnp.int32, (L, L), 1)

    dt = _softplus(dtr + dtb)
    acs = _cumsum_rows(dt * (-jnp.exp(alog)))
    acs_t = acs.T
    a_last = jnp.sum(jnp.where(row_l == L - 1, acs, 0.0), axis=0, keepdims=True)
    e_acs = jnp.exp(acs)
    dec = jnp.exp(a_last - acs)
    cdec = jnp.exp(a_last)

    def colv(m, h):
        return jnp.sum(jnp.where(lane == h, m, 0.0), axis=1, keepdims=True)

    def rowv(mt, h):
        return jnp.sum(jnp.where(sub == h, mt, 0.0), axis=0, keepdims=True)

    def pair(m, h0):
        return jnp.where(lo, colv(m, h0), colv(m, h0 + 1))

    ys, hs = [], []
    cb = None
    for p in range(n_pairs):
        g, h0 = p // ppg, 2 * p
        bmat, cmat = bm_g[g], cm_g[g]
        if p % ppg == 0:
            cb = _mm(cmat, bmat, "nt")
        x = xs_p[p]
        xdt = x * pair(dt, h0)
        yd = []
        for h in (h0, h0 + 1):
            seg = colv(acs, h) - rowv(acs_t, h)
            lm = jnp.where(tri, jnp.exp(jnp.where(tri, seg, 0.0)), 0.0)
            yd.append(_mm(cb * lm, xdt, "nn"))
        y = jnp.where(lo, yd[0], yd[1])
        y = y + _mm(cmat, h_p[p], "nt") * pair(e_acs, h0)
        st = _mm(xdt * pair(dec, h0), bmat, "tn")
        cd_col = jnp.where(sub < HD, colv(cdec, h0), colv(cdec, h0 + 1))
        hs.append(h_p[p] * cd_col + st)
        ys.append(y + pair(dsk, h0) * x)

    y2 = [ys[p] * _silu(z_p[p]) for p in range(n_pairs)]
    outs = []
    for g in range(len(bm_g)):
        ps = range(g * ppg, (g + 1) * ppg)
        ss = sum(jnp.sum(y2[p] * y2[p], axis=1, keepdims=True) for p in ps)
        rs = lax.rsqrt(ss * (1.0 / (ppg * LANES)) + EPS)
        outs += [y2[p] * rs * nw_p[p] for p in ps]
    return outs, hs


def _ssd_slices(xbc_ref, z_ref, nw_ref, di):
    n_pairs = di // LANES
    xs_p = [xbc_ref[:, p * LANES:(p + 1) * LANES] for p in range(n_pairs)]
    bm_g = [xbc_ref[:, di + g * NSTATE:di + (g + 1) * NSTATE] for g in range(NG)]
    cm_g = [xbc_ref[:, di + (NG + g) * NSTATE:di + (NG + g + 1) * NSTATE] for g in range(NG)]
    z_p = [z_ref[:, p * LANES:(p + 1) * LANES] for p in range(n_pairs)]
    nw_p = [nw_ref[:, p * LANES:(p + 1) * LANES] for p in range(n_pairs)]
    return xs_p, bm_g, cm_g, z_p, nw_p


def _ssd_fwd(xbc, proj, dt_cb, dtb, alog, dsk, nw, name):
    S, cc = xbc.shape
    di = NH_SSD * HD
    n_pairs = di // LANES
    nchunk = S // CHUNK

    def body(xbc_ref, z_ref, dtr_ref, dtb_ref, alog_ref, dsk_ref, nw_ref, y_ref, hs_ref, h_scr):
        @pl.when(pl.program_id(0) == 0)
        def _():
            h_scr[...] = jnp.zeros_like(h_scr)

        xs_p, bm_g, cm_g, z_p, nw_p = _ssd_slices(xbc_ref, z_ref, nw_ref, di)
        h_p = [h_scr[p * LANES:(p + 1) * LANES, :] for p in range(n_pairs)]
        hs_ref[...] = h_scr[...]
        outs, hs = _ssd_chunk(xs_p, bm_g, cm_g, dtr_ref[...], z_p, dtb_ref[...], alog_ref[...], dsk_ref[...], nw_p, h_p)
        for p in range(n_pairs):
            y_ref[:, p * LANES:(p + 1) * LANES] = outs[p].astype(y_ref.dtype)
            h_scr[p * LANES:(p + 1) * LANES, :] = hs[p]

    vec = pl.BlockSpec((1, LANES), lambda c: (0, 0))
    return pl.pallas_call(
        body, name=name, grid=(nchunk,),
        in_specs=[pl.BlockSpec((CHUNK, cc), lambda c: (c, 0)), pl.BlockSpec((CHUNK, di), lambda c: (c, 0)),
                  pl.BlockSpec((CHUNK, LANES), lambda c: (c, dt_cb)), vec, vec, vec,
                  pl.BlockSpec((1, di), lambda c: (0, 0))],
        out_specs=[pl.BlockSpec((CHUNK, di), lambda c: (c, 0)), pl.BlockSpec((None, di, NSTATE), lambda c: (c, 0, 0))],
        out_shape=[jax.ShapeDtypeStruct((S, di), BF16), jax.ShapeDtypeStruct((nchunk, di, NSTATE), F32)],
        scratch_shapes=[pltpu.VMEM((di, NSTATE), F32)],
        compiler_params=_cparams(("arbitrary",)),
    )(xbc, proj, proj, dtb, alog, dsk, nw)


def _ssd_bwd(xbc, proj, dt_cb, dtb, alog, dsk, nw, hsave, dy, name):
    S, cc = xbc.shape
    di = NH_SSD * HD
    n_pairs = di // LANES
    nchunk = S // CHUNK

    def body(xbc_ref, z_ref, dtr_ref, dtb_ref, alog_ref, dsk_ref, nw_ref, hs_ref, dy_ref,
             dxbc_ref, dz_ref, ddtr_ref, ddtb_ref, dalog_ref, ddsk_ref, dnw_ref, dh_scr):
        @pl.when(pl.program_id(0) == 0)
        def _():
            dh_scr[...] = jnp.zeros_like(dh_scr)
            ddtb_ref[...] = jnp.zeros_like(ddtb_ref)
            dalog_ref[...] = jnp.zeros_like(dalog_ref)
            ddsk_ref[...] = jnp.zeros_like(ddsk_ref)
            dnw_ref[...] = jnp.zeros_like(dnw_ref)

        xs_p, bm_g, cm_g, z_p, nw_p = _ssd_slices(xbc_ref, z_ref, nw_ref, di)
        h_p = [hs_ref[p * LANES:(p + 1) * LANES, :] for p in range(n_pairs)]
        dy_p = [dy_ref[:, p * LANES:(p + 1) * LANES].astype(F32) for p in range(n_pairs)]
        dh_p = [dh_scr[p * LANES:(p + 1) * LANES, :] for p in range(n_pairs)]
        _, vjp = jax.vjp(_ssd_chunk, xs_p, bm_g, cm_g, dtr_ref[...], z_p, dtb_ref[...], alog_ref[...], dsk_ref[...],
                         nw_p, h_p)
        dxs, dbm, dcm, ddtr, dz, ddtb, dalog, ddsk, dnw, dh = vjp((dy_p, dh_p))
        for p in range(n_pairs):
            sl = slice(p * LANES, (p + 1) * LANES)
            dxbc_ref[:, sl] = dxs[p]
            dz_ref[:, sl] = dz[p]
            dnw_ref[:, sl] += dnw[p]
            dh_scr[sl, :] = dh[p]
        for g in range(NG):
            dxbc_ref[:, di + g * NSTATE:di + (g + 1) * NSTATE] = dbm[g]
            dxbc_ref[:, di + (NG + g) * NSTATE:di + (NG + g + 1) * NSTATE] = dcm[g]
        ddtr_ref[...] = ddtr
        ddtb_ref[...] += ddtb
        dalog_ref[...] += dalog
        ddsk_ref[...] += ddsk

    last = nchunk - 1
    vec = pl.BlockSpec((1, LANES), lambda c: (0, 0))
    return pl.pallas_call(
        body, name=name, grid=(nchunk,),
        in_specs=[pl.BlockSpec((CHUNK, cc), lambda c: (last - c, 0)), pl.BlockSpec((CHUNK, di), lambda c: (last - c, 0)),
                  pl.BlockSpec((CHUNK, LANES), lambda c: (last - c, dt_cb)), vec, vec, vec,
                  pl.BlockSpec((1, di), lambda c: (0, 0)),
                  pl.BlockSpec((None, di, NSTATE), lambda c: (last - c, 0, 0)),
                  pl.BlockSpec((CHUNK, di), lambda c: (last - c, 0))],
        out_specs=[pl.BlockSpec((CHUNK, cc), lambda c: (last - c, 0)), pl.BlockSpec((CHUNK, di), lambda c: (last - c, 0)),
                   pl.BlockSpec((CHUNK, LANES), lambda c: (last - c, 0)), vec, vec, vec,
                   pl.BlockSpec((1, di), lambda c: (0, 0))],
        out_shape=[jax.ShapeDtypeStruct((S, cc), F32), jax.ShapeDtypeStruct((S, di), F32),
                   jax.ShapeDtypeStruct((S, LANES), F32), jax.ShapeDtypeStruct((1, LANES), F32),
                   jax.ShapeDtypeStruct((1, LANES), F32), jax.ShapeDtypeStruct((1, LANES), F32),
                   jax.ShapeDtypeStruct((1, di), F32)],
        scratch_shapes=[pltpu.VMEM((di, NSTATE), F32)],
        compiler_params=_cparams(("arbitrary",)),
    )(xbc, proj, proj, dtb, alog, dsk, nw, hsave, dy)


def _band_masks(rows_q, rows_k):
    qi = lax.broadcasted_iota(jnp.int32, (rows_q, rows_k), 0)
    ki = lax.broadcasted_iota(jnp.int32, (rows_q, rows_k), 1)
    return qi, ki


def _class_chunks(n_rows, d):
    per_class = n_rows // d
    ch = min(per_class, 256)
    out = []
    for r in range(d):
        for c0 in range(0, per_class, ch):
            tok = pl.ds(c0, ch) if d == 1 else pl.ds(r + d * c0, ch, stride=d)
            out.append((tok, pl.ds(r * per_class + c0, ch)))
    return out


def _to_class_order(src_ref, dst_ref, n_rows, d):
    for tok, cls in _class_chunks(n_rows, d):
        dst_ref[cls, :] = src_ref[tok, :].astype(dst_ref.dtype)


def _blk_rows(t):
    return pl.ds(pl.multiple_of(t * ABLK, ABLK), ABLK)


def _head_lanes(msk, t, t_rolled):
    return jnp.where(msk, t, t_rolled)


def _zero_unless(msk, t):
    return jnp.where(msk, t, jnp.zeros_like(t))


def _attn_fwd(qn, kn, proj, v_cb, name):
    S, ad = qn.shape
    nb = S // ABLK
    nbr = len(PATTERNS)

    def body(q_ref, k_ref, v_ref, o_ref, lse_ref, qc, kc, vc, ob, mb, lb, m_s, l_s):
        lo = _lane_mask()
        qi, ki = _band_masks(ABLK, 2 * ABLK)
        band, in_cur, prev_ok = ki <= qi + ABLK, ki >= ABLK, ki >= qi
        for bi, (_, d) in enumerate(PATTERNS):
            nbc = S // d // ABLK
            first, last = bi == 0, bi == nbr - 1
            qs, ks, vs = q_ref, k_ref, v_ref
            if d > 1:
                qs, ks, vs = qc, kc, vc
                for src, dst in ((q_ref, qc), (k_ref, kc), (v_ref, vc)):
                    _to_class_order(src, dst, S, d)
            o_dst, m_dst, l_dst = (o_ref, m_s, l_s) if first else (ob, mb, lb)

            def blk(t, carry, nbc=nbc, qs=qs, ks=ks, vs=vs, o_dst=o_dst, m_dst=m_dst, l_dst=l_dst):
                rows, prow = _blk_rows(t), _blk_rows(jnp.maximum(t - 1, 0))
                has_prev = (t % nbc) != 0
                kk = jnp.concatenate([ks[prow, :], ks[rows, :]], axis=0)
                vv = jnp.concatenate([vs[prow, :], vs[rows, :]], axis=0)
                for u in range(ABLK // QTILE):
                    sub = pl.ds(pl.multiple_of(t * ABLK + u * QTILE, QTILE), QTILE)
                    sl = slice(u * QTILE, (u + 1) * QTILE)
                    valid = band[sl] & (in_cur[sl] | (prev_ok[sl] & has_prev))
                    qv = qs[sub, :]
                    os_, ms_, ls_ = [], [], []
                    for msk in (lo, jnp.logical_not(lo)):
                        s = jnp.where(valid, _dot(_zero_unless(msk, qv), kk, "nt"), NEG)
                        m = jnp.max(s, axis=1, keepdims=True)
                        p = jnp.exp(s - m)
                        os_.append(_dot(p, vv, "nn"))
                        ms_.append(m)
                        ls_.append(jnp.sum(p, axis=1, keepdims=True))
                    o_dst[sub, :] = jnp.where(lo, os_[0], os_[1])
                    m_dst[sub, :] = jnp.where(lo, ms_[0], ms_[1])
                    l_dst[sub, :] = jnp.where(lo, ls_[0], ls_[1])
                return carry

            lax.fori_loop(0, nb, blk, 0, unroll=8)
            if first:
                continue
            for tok, cls in _class_chunks(S, d):
                m_old, m_b = m_s[tok, :], mb[cls, :]
                m_new = jnp.maximum(m_old, m_b)
                a, b = jnp.exp(m_old - m_new), jnp.exp(m_b - m_new)
                l_new = a * l_s[tok, :] + b * lb[cls, :]
                o_new = a * o_ref[tok, :] + b * ob[cls, :]
                if last:
                    o_ref[tok, :] = o_new / l_new
                    lse_ref[tok, :] = m_new + jnp.log(l_new)
                else:
                    o_ref[tok, :] = o_new
                    m_s[tok, :] = m_new
                    l_s[tok, :] = l_new

    col = pl.BlockSpec((S, LANES), lambda h: (0, h))
    return pl.pallas_call(
        body, name=name, grid=(ad // LANES,),
        in_specs=[col, col, pl.BlockSpec((S, LANES), lambda h: (0, h + v_cb))], out_specs=[col, col],
        out_shape=[jax.ShapeDtypeStruct((S, ad), F32), jax.ShapeDtypeStruct((S, ad), F32)],
        scratch_shapes=[pltpu.VMEM((S, LANES), BF16)] * 3 + [pltpu.VMEM((S, LANES), F32)] * 5,
        compiler_params=_cparams(("parallel",)),
    )(qn, kn, proj)


def _attn_bwd(qn, kn, proj, v_cb, do, lse, dd, name):
    S, ad = qn.shape
    nb = S // ABLK

    def body(q_ref, k_ref, v_ref, do_ref, lse_ref, dd_ref, dq_ref, dk_ref, dv_ref,
             qc, kc, vc, doc, lsec, ddc, dqc, dkc, dvc):
        lo = _lane_mask()
        qi, ki = _band_masks(ABLK, ABLK)
        cur_ok, prev_ok = ki <= qi, ki >= qi
        for bi, (_, d) in enumerate(PATTERNS):
            nbc = S // d // ABLK
            first = bi == 0
            token_order = (q_ref, k_ref, v_ref, do_ref, lse_ref, dd_ref)
            class_order = (qc, kc, vc, doc, lsec, ddc)
            if d > 1:
                for src, dst in zip(token_order, class_order):
                    _to_class_order(src, dst, S, d)
            qs, ks, vs, dos, lses, dds = class_order if d > 1 else token_order
            dq_dst, dk_dst, dv_dst = (dq_ref, dk_ref, dv_ref) if first else (dqc, dkc, dvc)
            dk_dst[...] = jnp.zeros_like(dk_dst)
            dv_dst[...] = jnp.zeros_like(dv_dst)

            def blk(t, carry, nbc=nbc, qs=qs, ks=ks, vs=vs, dos=dos, lses=lses, dds=dds,
                    dq_dst=dq_dst, dk_dst=dk_dst, dv_dst=dv_dst):
                rows, prow = _blk_rows(t), _blk_rows(jnp.maximum(t - 1, 0))
                has_prev = (t % nbc) != 0
                qv, dov, lse_b, dd_b = qs[rows, :], dos[rows, :], lses[rows, :], dds[rows, :]
                lse_r, dd_r = pltpu.roll(lse_b, HD, 1), pltpu.roll(dd_b, HD, 1)
                heads = []
                for msk in (lo, jnp.logical_not(lo)):
                    heads.append((_zero_unless(msk, qv), _zero_unless(msk, dov), _head_lanes(msk, lse_b, lse_r),
                                  _head_lanes(msk, dd_b, dd_r)))
                dqs = [None, None]
                for krows, vmask in ((rows, cur_ok), (prow, prev_ok & has_prev)):
                    kv, vv = ks[krows, :], vs[krows, :]
                    dk = jnp.zeros((ABLK, LANES), F32)
                    dv = jnp.zeros((ABLK, LANES), F32)
                    for hi, (qh, doh, lse_h, dd_h) in enumerate(heads):
                        s = jnp.where(vmask, _dot(qh, kv, "nt"), NEG)
                        p = jnp.exp(s - lse_h)
                        ds = p * (_dot(doh, vv, "nt") - dd_h)
                        dqh = _dot(ds, kv, "nn")
                        dqs[hi] = dqh if dqs[hi] is None else dqs[hi] + dqh
                        dv = dv + _dot(p, doh, "tn")
                        dk = dk + _dot(ds, qh, "tn")
                    dk_dst[krows, :] += dk
                    dv_dst[krows, :] += dv
                dq_dst[rows, :] = jnp.where(lo, dqs[0], dqs[1])
                return carry

            lax.fori_loop(0, nb, blk, 0, unroll=4)
            if not first:
                for tok, cls in _class_chunks(S, d):
                    dq_ref[tok, :] = dq_ref[tok, :] + dqc[cls, :]
                    dk_ref[tok, :] = dk_ref[tok, :] + dkc[cls, :]
                    dv_ref[tok, :] = dv_ref[tok, :] + dvc[cls, :]

    col = pl.BlockSpec((S, LANES), lambda h: (0, h))
    col1 = pl.BlockSpec((S, LANES), lambda h: (0, h), pipeline_mode=pl.Buffered(1))
    vcol1 = pl.BlockSpec((S, LANES), lambda h: (0, h + v_cb), pipeline_mode=pl.Buffered(1))
    return pl.pallas_call(
        body, name=name, grid=(ad // LANES,),
        in_specs=[col, col, vcol1, col1, col1, col1], out_specs=[col, col, col],
        out_shape=[jax.ShapeDtypeStruct((S, ad), F32)] * 3,
        scratch_shapes=[pltpu.VMEM((S, LANES), BF16)] * 4 + [pltpu.VMEM((S, LANES), F32)] * 5,
        compiler_params=_cparams(("parallel",)),
    )(qn, kn, proj, do, lse, dd)


def _coords():
    return lax.axis_index("x"), lax.axis_index("y"), lax.axis_index("c")


def _exchange8(xs, per_dest, name):
    n = len(xs)
    blk = [x.shape[1:] if per_dest else x.shape for x in xs]

    def body(*refs):
        ins, outs = refs[:n], refs[n:2 * n]
        send_sems, recv_sems, local_sems = refs[2 * n:]
        x, y, c = _coords()
        sibling = (x, y, 1 - c)
        chips = [(1 - x, y), (x, 1 - y), (1 - x, 1 - y)]
        first, passed, mine = [], [], []
        for a in range(n):
            def src_for(cx, cy, a=a):
                return ins[a].at[2 * cx + cy] if per_dest else ins[a]

            def slot(px, py, pc, a=a):
                return outs[a].at[4 * px + 2 * py + pc]

            def copy(k, src, dst, to, a=a):
                return pltpu.make_async_remote_copy(src_ref=src, dst_ref=dst, send_sem=send_sems.at[7 * a + k],
                                                    recv_sem=recv_sems.at[7 * a + k], device_id=to, device_id_type=MESH)

            m = pltpu.make_async_copy(src_for(x, y), slot(x, y, c), local_sems.at[a])
            m.start()
            mine.append(m)
            cps = [copy(0, src_for(x, y), slot(x, y, c), sibling)]
            cps += [copy(1 + j, src_for(*chip), slot(x, y, c), (*chip, c)) for j, chip in enumerate(chips)]
            for cp in cps:
                cp.start()
            first += cps
        for a in range(n):
            def slot(px, py, pc, a=a):
                return outs[a].at[4 * px + 2 * py + pc]

            def copy(k, src, dst, to, a=a):
                return pltpu.make_async_remote_copy(src_ref=src, dst_ref=dst, send_sem=send_sems.at[7 * a + k],
                                                    recv_sem=recv_sems.at[7 * a + k], device_id=to, device_id_type=MESH)

            for j, chip in enumerate(chips):
                copy(1 + j, slot(*chip, c), slot(*chip, c), (*chip, c)).wait_recv()
                fw = copy(4 + j, slot(*chip, c), slot(*chip, c), sibling)
                fw.start()
                passed.append(fw)
        for a in range(n):
            def slot(px, py, pc, a=a):
                return outs[a].at[4 * px + 2 * py + pc]

            def copy(k, src, dst, to, a=a):
                return pltpu.make_async_remote_copy(src_ref=src, dst_ref=dst, send_sem=send_sems.at[7 * a + k],
                                                    recv_sem=recv_sems.at[7 * a + k], device_id=to, device_id_type=MESH)

            copy(0, slot(x, y, 1 - c), slot(x, y, 1 - c), sibling).wait_recv()
            for j, chip in enumerate(chips):
                copy(4 + j, slot(*chip, 1 - c), slot(*chip, 1 - c), sibling).wait_recv()
        for cp in first + passed:
            cp.wait_send()
        for m in mine:
            m.wait()

    anyspec = pl.BlockSpec(memory_space=pl.ANY)
    res = pl.pallas_call(
        body, name=name, in_specs=[anyspec] * n, out_specs=[anyspec] * n,
        out_shape=[jax.ShapeDtypeStruct((8,) + tuple(b), x.dtype) for b, x in zip(blk, xs)],
        scratch_shapes=[pltpu.SemaphoreType.DMA((7 * n,)), pltpu.SemaphoreType.DMA((7 * n,)),
                        pltpu.SemaphoreType.DMA((n,))],
    )(*xs)
    return list(res)


def _pair_swap(xs, name):
    n = len(xs)

    def body(*refs):
        ins, outs = refs[:n], refs[n:2 * n]
        send_sems, recv_sems = refs[2 * n:]
        x, y, c = _coords()
        cps = [pltpu.make_async_remote_copy(src_ref=ins[a].at[1 - c], dst_ref=outs[a], send_sem=send_sems.at[a],
                                            recv_sem=recv_sems.at[a], device_id=(x, y, 1 - c), device_id_type=MESH)
               for a in range(n)]
        for cp in cps:
            cp.start()
        for cp in cps:
            cp.wait()

    anyspec = pl.BlockSpec(memory_space=pl.ANY)
    res = pl.pallas_call(
        body, name=name, in_specs=[anyspec] * n, out_specs=[anyspec] * n,
        out_shape=[jax.ShapeDtypeStruct(x.shape[1:], x.dtype) for x in xs],
        scratch_shapes=[pltpu.SemaphoreType.DMA((n,)), pltpu.SemaphoreType.DMA((n,))],
    )(*xs)
    return list(res)


_HBM = pl.BlockSpec(memory_space=pltpu.HBM)
_SEM = pl.BlockSpec(memory_space=pltpu.SEMAPHORE)
_EFFECT = pltpu.SideEffectType.DATAFLOW_SIDE_EFFECTING


def _peer(x, y, c, j):
    dx, dy, dc = (j + 1) >> 2 & 1, (j + 1) >> 1 & 1, (j + 1) & 1
    return (1 - x if dx else x, 1 - y if dy else y, 1 - c if dc else c)


def _spread_copies(s_refs, l_refs, send_sems, recv_sems, per_dest):
    x, y, c = _coords()
    me = 4 * x + 2 * y + c
    cps = []
    for a in range(len(s_refs)):
        for j in range(7):
            tx, ty, tc = _peer(x, y, c, j)
            src = s_refs[a].at[2 * tx + ty] if per_dest else s_refs[a]
            cps.append(pltpu.make_async_remote_copy(src_ref=src, dst_ref=l_refs[a].at[me], send_sem=send_sems.at[7 * a + j],
                                                    recv_sem=recv_sems.at[7 * a + j], device_id=(tx, ty, tc),
                                                    device_id_type=MESH))
    return cps


def _spread_start(srcs, per_dest, dev, chip, name):
    n = len(srcs)
    lands = []
    for s in srcs:
        own = lax.dynamic_index_in_dim(s, chip, 0, keepdims=False) if per_dest else s
        lands.append(lax.dynamic_update_index_in_dim(lax.empty((8,) + own.shape, own.dtype), own, dev, 0))

    def body(*refs):
        s_refs, l_refs, send_sems, recv_sems, token = refs[:n], refs[n:2 * n], refs[2 * n], refs[2 * n + 1], refs[-1]
        for cp in _spread_copies(s_refs, l_refs, send_sems, recv_sems, per_dest):
            cp.start()
        token[...] = jnp.zeros_like(token)

    hbm_in = [pltpu.with_memory_space_constraint(t, pltpu.HBM) for t in list(srcs) + lands]
    outs = pl.pallas_call(
        body, name=name,
        out_shape=(pltpu.SemaphoreType.DMA((7 * n,)), pltpu.SemaphoreType.DMA((7 * n,)),
                   *[pltpu.HBM(t.shape, t.dtype) for t in hbm_in], jax.ShapeDtypeStruct((8, LANES), F32)),
        in_specs=[_HBM] * (2 * n), out_specs=(_SEM, _SEM, *[_HBM] * (2 * n), pl.BlockSpec(memory_space=pltpu.VMEM)),
        input_output_aliases={i: 2 + i for i in range(2 * n)},
        compiler_params=pltpu.CompilerParams(has_side_effects=_EFFECT),
    )(*hbm_in)
    return (outs[0], outs[1], list(outs[2:2 + n]), list(outs[2 + n:2 + 2 * n])), outs[-1]


def _spread_wait(handle, per_dest, after, name):
    send_sems, recv_sems, srcs, lands = handle
    n = len(srcs)

    def body(*refs):
        s_refs, l_refs, send_ref, recv_ref = refs[:n], refs[n:2 * n], refs[2 * n], refs[2 * n + 1]
        for cp in _spread_copies(s_refs, l_refs, send_ref, recv_ref, per_dest):
            cp.wait_send()
            cp.wait_recv()

    outs = pl.pallas_call(
        body, name=name, out_shape=tuple(pltpu.HBM(t.shape, t.dtype) for t in srcs + lands),
        in_specs=[_HBM] * (2 * n) + [_SEM, _SEM, pl.BlockSpec(memory_space=pl.ANY)], out_specs=tuple([_HBM] * (2 * n)),
        input_output_aliases={i: i for i in range(2 * n)},
        compiler_params=pltpu.CompilerParams(has_side_effects=_EFFECT),
    )(*srcs, *lands, send_sems, recv_sems, after)
    return list(outs[n:])


def _pair_add(g2, theirs, half, name, tm=256):
    _, n, cdim = g2.shape
    tm = min(tm, n)

    def body(h_ref, a_ref, b_ref, o_ref):
        o_ref[...] = (a_ref[...] + b_ref[...]).astype(o_ref.dtype)

    grid_spec = pltpu.PrefetchScalarGridSpec(
        num_scalar_prefetch=1, grid=(n // tm,),
        in_specs=[pl.BlockSpec((None, tm, cdim), lambda i, h: (h[0], i, 0)), pl.BlockSpec((tm, cdim), lambda i, h: (i, 0))],
        out_specs=pl.BlockSpec((tm, cdim), lambda i, h: (i, 0)))
    return pl.pallas_call(body, name=name, grid_spec=grid_spec, out_shape=jax.ShapeDtypeStruct((n, cdim), BF16),
                          compiler_params=_cparams(("parallel",)))(half.reshape(1).astype(jnp.int32), g2, theirs)


def _adamw_math(w, g, m, v):
    m = ADAM_B1 * m + (1.0 - ADAM_B1) * g
    v = ADAM_B2 * v + (1.0 - ADAM_B2) * (g * g)
    m_hat = m / (1.0 - ADAM_B1 ** ADAM_STEP)
    v_hat = v / (1.0 - ADAM_B2 ** ADAM_STEP)
    delta = -ADAM_LR * (m_hat / (jnp.sqrt(v_hat) + ADAM_EPS) + ADAM_WD * w)
    return delta, m, v


def _adamw(parts, w, m, v, name, tm=128):
    npart, R, C = parts.shape
    tm = min(tm, R)

    def body(p_ref, w_ref, m_ref, v_ref, g_out, d_out, m_out, v_out):
        g = p_ref[0].astype(F32)
        for i in range(1, npart):
            g = g + p_ref[i].astype(F32)
        d, mm, vv = _adamw_math(w_ref[...], g, m_ref[...], v_ref[...])
        g_out[...] = g
        d_out[...] = d
        m_out[...] = mm
        v_out[...] = vv

    spec = pl.BlockSpec((tm, C), lambda i: (i, 0))
    return pl.pallas_call(
        body, name=name, grid=(R // tm,),
        in_specs=[pl.BlockSpec((npart, tm, C), lambda i: (0, i, 0)), spec, spec, spec], out_specs=[spec] * 4,
        out_shape=[jax.ShapeDtypeStruct((R, C), F32)] * 4,
        compiler_params=_cparams(("parallel",)),
    )(parts, w, m, v)


def _sum_parts(parts, name):
    npart, R, C = parts.shape

    def body(p_ref, o_ref):
        g = p_ref[0]
        for i in range(1, npart):
            g = g + p_ref[i]
        o_ref[...] = g

    return pl.pallas_call(body, name=name, out_shape=jax.ShapeDtypeStruct((R, C), F32))(parts)


def _mod_fwd(c_all, w_ada, b_sh, name):
    def body(c_ref, w_ref, b_ref, o_ref):
        o_ref[...] = _dot(_silu(c_ref[...]), w_ref[...], "nn") + b_ref[...]

    return pl.pallas_call(body, name=name, out_shape=jax.ShapeDtypeStruct((c_all.shape[0], w_ada.shape[1]), F32),
                          compiler_params=pltpu.CompilerParams(vmem_limit_bytes=VMEM_LIMIT))(c_all, w_ada, b_sh)


def _mod_wgrad(c_all, dmod_sh, name):
    def body(c_ref, d_ref, o_ref):
        o_ref[...] = _dot(_silu(c_ref[...]), d_ref[...], "tn")

    return pl.pallas_call(body, name=name, out_shape=jax.ShapeDtypeStruct((c_all.shape[1], dmod_sh.shape[1]), F32),
                          compiler_params=pltpu.CompilerParams(vmem_limit_bytes=VMEM_LIMIT))(c_all, dmod_sh)


def _pad_lanes(v):
    return jnp.pad(v, ((0, 0), (0, (-v.shape[1]) % LANES)))


def kernel(x, c, norm1_w, norm2_w, w_ada, b_ada, w_in, conv_w, conv_b, dt_bias, a_log, d_skip, ssd_norm_w, q_norm_w, k_norm_w, attn_norm_w, w_out, w_ff1, w_ff2, loss_target, m_norm1_w, m_norm2_w, m_w_ada, m_b_ada, m_w_in, m_conv_w, m_conv_b, m_dt_bias, m_a_log, m_d_skip, m_ssd_norm_w, m_q_norm_w, m_k_norm_w, m_attn_norm_w, m_w_out, m_w_ff1, m_w_ff2, v_norm1_w, v_norm2_w, v_w_ada, v_b_ada, v_w_in, v_conv_w, v_conv_b, v_dt_bias, v_a_log, v_d_skip, v_ssd_norm_w, v_q_norm_w, v_k_norm_w, v_attn_norm_w, v_w_out, v_w_ff1, v_w_ff2):
    xi, yi, ci = _coords()
    chip = 2 * xi + yi
    dev = 2 * chip + ci
    xs, tgt = x[0], loss_target[0]
    S, D = xs.shape
    DI, AD = NH_SSD * HD, NH_ATT * HD
    CC = DI + 2 * NG * NSTATE
    PW = DI + CC + 3 * AD + LANES
    DFF = w_ff1.shape[2] * 4
    MIX = DI + AD
    o_xbc, o_q, o_k, o_v, o_dt = DI, DI + CC, DI + CC + AD, DI + CC + 2 * AD, DI + CC + 3 * AD

    def half_rows(w):
        r = w.shape[0] // 2
        return lax.dynamic_slice_in_dim(w, ci * r, r, 0).astype(BF16)

    c_all, conv_w_all = _exchange8([c, conv_w[0]], False, "gather_c_conv_w")
    c_all = c_all.reshape(8, D)
    c_all = jnp.pad(c_all, ((0, 8), (0, 0)))
    nmod = w_ada.shape[2]
    b_sh = lax.dynamic_slice_in_dim(b_ada, chip * nmod, nmod, 1)
    mod_sh = _mod_fwd(c_all, w_ada[0], b_sh, "mod_fwd")
    mod_all = _exchange8([mod_sh[:8]], False, "gather_mod")[0]
    mod_me = lax.dynamic_index_in_dim(mod_all[0::2], dev, 1, keepdims=False).reshape(1, 4 * nmod)
    shift1, scale1, gate1, shift2, scale2, gate2 = [mod_me[:, i * D:(i + 1) * D] for i in range(6)]

    g_in = _exchange8([half_rows(w_in[0])], False, "gather_w_in")[0]
    rest_handle, rest_token = _spread_start([half_rows(w_out[0]), half_rows(w_ff1[0]), half_rows(w_ff2[0])], False, dev,
                                            chip, "gather_rest_start")
    shift1 = shift1 + rest_token[0, 0]
    wsh = w_in.shape[2]
    w_in_f = g_in.reshape(4, D, wsh).transpose(1, 0, 2).reshape(D, 4 * wsh)
    n_zx = DI + CC
    w_proj = jnp.concatenate([w_in_f[:, :n_zx], w_in_f[:, n_zx + NH_SSD:], w_in_f[:, n_zx:n_zx + NH_SSD],
                              jnp.zeros((D, LANES - NH_SSD), BF16)], axis=1)

    dtb, alog, dsk = _pad_lanes(dt_bias), _pad_lanes(a_log), _pad_lanes(d_skip)
    qw2 = jnp.concatenate([q_norm_w, q_norm_w], axis=1)
    kw2 = jnp.concatenate([k_norm_w, k_norm_w], axis=1)
    conv_w_f = conv_w_all[0::2].transpose(1, 0, 2).reshape(KCONV, CC)

    h1 = _rows("norm1", lambda r, k: ([_normmod(r[0], *k)], []), [(xs, 0, D)], [norm1_w, scale1, shift1],
               [(D, BF16)], [], S)[0]
    proj = _matmul(h1, w_proj, "nn", F32, "in_proj", tn=896)
    xbc = _conv_fwd(proj, o_xbc, CC, conv_w_f, conv_b, "conv_fwd")
    y_ssd, hsave = _ssd_fwd(xbc, proj, o_dt // LANES, dtb, alog, dsk, ssd_norm_w, "ssd_fwd")

    def qk_call(name, col0, w2, scale):
        def body(t_ref, w_ref, o_ref):
            o_ref[...] = _headnorm(t_ref[...], w_ref[...], scale)
        return pl.pallas_call(
            body, name=name, grid=(AD // LANES,),
            in_specs=[pl.BlockSpec((S, LANES), lambda j: (0, j + col0 // LANES)),
                      pl.BlockSpec((1, LANES), lambda j: (0, 0))],
            out_specs=pl.BlockSpec((S, LANES), lambda j: (0, j)),
            out_shape=jax.ShapeDtypeStruct((S, AD), F32), compiler_params=_cparams(("parallel",)),
        )(proj, w2)

    qn = qk_call("q_norm", o_q, qw2, HD ** -0.5)
    kn = qk_call("k_norm", o_k, kw2, 1.0)
    o_att, lse = _attn_fwd(qn, kn, proj, o_v // LANES, "attn_fwd")
    y_att = _rows("attn_out_norm", lambda r, k: ([_rmsw(r[0], k[0])], []), [(o_att, 0, AD)], [attn_norm_w],
                  [(AD, BF16)], [], S)[0]
    g_out, g_ff1, g_ff2 = _spread_wait(rest_handle, False, o_att, "gather_rest_wait")
    w_out_f = g_out.reshape(MIX, D)
    w_out_a, w_out_b = w_out_f[:DI], w_out_f[DI:]
    w_ff1_f = g_ff1.reshape(4, D, DFF // 4).transpose(1, 0, 2).reshape(D, DFF)
    w_ff2_f = g_ff2.reshape(DFF, D)
    mix_a = _matmul(y_ssd, w_out_a, "nn", F32, "out_proj_a")
    mix = _matmul(y_att, w_out_b, "nn", F32, "out_proj_b", epilogue=lambda r, e: r + e, extras=(mix_a,))
    x2, h2 = _rows("resid_norm2", lambda r, k: (list(_resid_normmod(r[0], r[1], *k)), []), [(xs, 0, D), (mix, 0, D)],
                   [gate1, norm2_w, scale2, shift2], [(D, F32), (D, BF16)], [], S)
    u = _matmul(h2, w_ff1_f, "nn", F32, "ff1")
    relu2 = lambda t: jnp.square(jnp.maximum(t, 0.0))
    ff = _matmul(u, w_ff2_f, "nn", F32, "ff2", a_fn=relu2)

    def loss_fn(r, k):
        x2_, ff_, t_ = r
        err = x2_ + k[0] * ff_ - t_
        dy_ = err * (1.0 / D)
        ls = jnp.sum(jnp.sum(0.5 * err * err, axis=1, keepdims=True), axis=0, keepdims=True) * (1.0 / D)
        return [dy_, dy_ * k[0]], [ls, jnp.sum(dy_ * ff_, axis=0, keepdims=True)]

    dy, dff, loss_p, dgate2 = _rows("loss", loss_fn, [(x2, 0, D), (ff, 0, D), (tgt, 0, D)], [gate2],
                                    [(D, F32), (D, BF16)], [(1, 1), (1, D)], S)
    du = _matmul(dff, w_ff2_f, "nt", BF16, "ff2_dx", epilogue=lambda r, e: r * (2.0 * jnp.maximum(e, 0.0)), extras=(u,))
    gw_ff2 = _matmul(u, dff, "tn", F32, "ff2_dw", a_fn=relu2)
    gw_ff1 = _matmul(h2, du, "tn", F32, "ff1_dw")

    def by_half_cols(g):
        r, c4 = g.shape
        return g.reshape(2, r // 2, 4, c4 // 4).transpose(0, 2, 1, 3)

    def by_half_rows(g):
        r4, cdim = g.shape
        return g.reshape(4, 2, r4 // 8, cdim).transpose(1, 0, 2, 3)

    def scatter_start(layouts, tag):
        theirs = _pair_swap(layouts, "pair_swap_" + tag)
        sums = []
        for i, (g2, t) in enumerate(zip(layouts, theirs)):
            _, r2, cdim = t.shape
            sm = _pair_add(g2.reshape(2, 4 * r2, cdim), t.reshape(4 * r2, cdim), ci, "pair_add_%s_%d" % (tag, i))
            sums.append(sm.reshape(4, r2, cdim))
        return _spread_start(sums, True, dev, chip, "scatter_%s_start" % tag)

    def scatter_wait(handle, after, tag):
        return [s.reshape(4, 2 * s.shape[1], s.shape[2]) for s in _spread_wait(handle, True, after, "scatter_%s_wait" % tag)]

    ff_handle, ff_token = scatter_start([by_half_cols(gw_ff1), by_half_rows(gw_ff2)], "ff")
    dh2 = _matmul(du, w_ff1_f, "nt", F32, "ff1_dx")

    def resid_bwd(r, k):
        x_, mix_, dx2a, dh2_ = r
        _, vjp = jax.vjp(_resid_normmod, x_, mix_, *k)
        dx, dmix_, dg, dnw, dsc, dsh = vjp((dx2a, dh2_))
        return [dx, dmix_], [dg, dnw, dsc, dsh]

    dx2, dmix, dgate1, g_norm2, dscale2, dshift2 = _rows(
        "resid_norm2_bwd", resid_bwd, [(xs, 0, D), (mix, 0, D), (dy, 0, D), (dh2, 0, D)],
        [gate1 + ff_token[0, 0], norm2_w, scale2, shift2], [(D, F32), (D, BF16)], [(1, D)] * 4, S)
    gw_out = jnp.concatenate([_matmul(y_ssd, dmix, "tn", F32, "out_proj_dw_a"),
                              _matmul(y_att, dmix, "tn", F32, "out_proj_dw_b")], axis=0)
    out_handle, out_token = scatter_start([by_half_rows(gw_out)], "out")
    dy_ssd = _matmul(dmix, w_out_a, "nt", F32, "out_proj_dx_a")
    dy_att = _matmul(dmix, w_out_b, "nt", F32, "out_proj_dx_b")

    def attn_norm_bwd(r, k):
        o_, dyo = r
        _, vjp = jax.vjp(_rmsw, o_, k[0])
        do_, dw_ = vjp(dyo)
        lo = _lane_mask()
        dd_blocks = []
        for b in range(AD // LANES):
            t = (do_ * o_)[:, b * LANES:(b + 1) * LANES]
            s0 = jnp.sum(jnp.where(lo, t, 0.0), axis=1, keepdims=True)
            s1 = jnp.sum(jnp.where(lo, 0.0, t), axis=1, keepdims=True)
            dd_blocks.append(jnp.where(lo, s0, s1))
        return [do_, jnp.concatenate(dd_blocks, axis=1)], [dw_]

    do_att, dd_att, g_attn_norm = _rows("attn_norm_bwd", attn_norm_bwd, [(o_att, 0, AD), (dy_att, 0, AD)],
                                        [attn_norm_w + out_token[0, 0]], [(AD, F32), (AD, F32)], [(1, AD)], S)
    dq_n, dk_n, dv = _attn_bwd(qn, kn, proj, o_v // LANES, do_att, lse, dd_att, "attn_bwd")

    def qk_bwd_call(name, col0, w2, scale, g):
        def body(t_ref, w_ref, g_ref, o_ref, dw_ref):
            @pl.when(pl.program_id(0) == 0)
            def _():
                dw_ref[...] = jnp.zeros_like(dw_ref)
            _, vjp = jax.vjp(lambda t, w: _headnorm(t, w, scale), t_ref[...], w_ref[...])
            dt_, dw_ = vjp(g_ref[...])
            o_ref[...] = dt_.astype(BF16)
            dw_ref[...] += dw_
        blk = pl.BlockSpec((S, LANES), lambda j: (0, j))
        return pl.pallas_call(
            body, name=name, grid=(AD // LANES,),
            in_specs=[pl.BlockSpec((S, LANES), lambda j: (0, j + col0 // LANES)),
                      pl.BlockSpec((1, LANES), lambda j: (0, 0)), blk],
            out_specs=[blk, pl.BlockSpec((1, LANES), lambda j: (0, 0))],
            out_shape=[jax.ShapeDtypeStruct((S, AD), BF16), jax.ShapeDtypeStruct((1, LANES), F32)],
            compiler_params=_cparams(("arbitrary",)),
        )(proj, w2, g)

    dq, g_qw2 = qk_bwd_call("q_norm_bwd", o_q, qw2, HD ** -0.5, dq_n)
    dk, g_kw2 = qk_bwd_call("k_norm_bwd", o_k, kw2, 1.0, dk_n)
    g_q_norm = g_qw2[:, :HD] + g_qw2[:, HD:]
    g_k_norm = g_kw2[:, :HD] + g_kw2[:, HD:]

    dxbc, dz, ddtr, g_dtb, g_alog, g_dsk, g_ssd_norm = _ssd_bwd(
        xbc, proj, o_dt // LANES, dtb, alog, dsk, ssd_norm_w, hsave, dy_ssd, "ssd_bwd")
    dxbc_pre, g_conv_w, g_conv_b = _conv_bwd(proj, o_xbc, CC, conv_w_f, conv_b, dxbc, "conv_bwd")
    dproj = jnp.concatenate([dz.astype(BF16), dxbc_pre.astype(BF16), dq, dk, dv.astype(BF16), ddtr.astype(BF16)], axis=1)
    dh1 = _matmul(dproj, w_proj, "nt", F32, "in_proj_dx", tk=896)

    def norm1_bwd(r, k):
        x_, dh_, dres = r
        _, vjp = jax.vjp(_normmod, x_, *k)
        dx, dnw, dsc, dsh = vjp(dh_)
        return [dx + dres], [dnw, dsc, dsh]

    grad_x, g_norm1, dscale1, dshift1 = _rows("norm1_bwd", norm1_bwd, [(xs, 0, D), (dh1, 0, D), (dx2, 0, D)],
                                              [norm1_w, scale1, shift1], [(D, F32)], [(1, D)] * 3, S)
    dmod =jnp.concatenate([dshift1, dscale1, dgate1, dshift2, dscale2, dgate2], axis=1)

    small = [g_norm1, g_norm2, dmod, g_conv_b, g_dtb, g_alog, g_dsk, g_ssd_norm, _pad_lanes(g_q_norm),
             _pad_lanes(g_k_norm), g_attn_norm, g_conv_w.reshape(1, KCONV * CC)]
    sizes = [t.shape[1] for t in small]
    packed = jnp.concatenate(small, axis=1)
    nrow = -(-packed.shape[1] // LANES // 8) * 8
    packed = jnp.pad(packed, ((0, 0), (0, nrow * LANES - packed.shape[1]))).reshape(nrow, LANES)
    packed_all = _exchange8([packed], False, "gather_small_grads")[0]
    tot = _sum_parts(packed_all, "sum_small_grads").reshape(1, nrow * LANES)
    offs = [sum(sizes[:i]) for i in range(len(sizes))]
    (g_norm1, g_norm2, g_b_ada, g_conv_b, g_dtb, g_alog, g_dsk, g_ssd_norm, g_q_norm, g_k_norm, g_attn_norm,
     g_conv_w) = [tot[:, o:o + n] for o, n in zip(offs, sizes)]
    g_dtb, g_alog, g_dsk = g_dtb[:, :NH_SSD], g_alog[:, :NH_SSD], g_dsk[:, :NH_SSD]
    g_q_norm, g_k_norm = g_q_norm[:, :HD], g_k_norm[:, :HD]
    ccs = CC // 4
    g_conv_w = lax.dynamic_slice_in_dim(g_conv_w.reshape(KCONV, CC), chip * ccs, ccs, 1)

    dmod_all = packed_all.reshape(8, nrow * LANES)[:, offs[2]:offs[2] + 6 * D]
    dmod_sh = jnp.pad(lax.dynamic_slice_in_dim(dmod_all, chip * nmod, nmod, 1), ((0, 8), (0, 0)))
    gw_ada = _mod_wgrad(c_all, dmod_sh, "mod_wgrad")

    gw_proj = _matmul(h1, dproj, "tn", F32, "in_proj_dw", tn=896)
    gw_in = jnp.concatenate([gw_proj[:, :n_zx], gw_proj[:, o_dt:o_dt + NH_SSD], gw_proj[:, n_zx:o_dt]], axis=1)
    in_handle, in_token = scatter_start([by_half_cols(gw_in)], "in")

    parts_ff1, parts_ff2 = scatter_wait(ff_handle, in_token, "ff")
    res_ff1 = _adamw(parts_ff1, w_ff1[0], m_w_ff1[0], v_w_ff1[0], "adamw_w_ff1")
    res_ff2 = _adamw(parts_ff2, w_ff2[0], m_w_ff2[0], v_w_ff2[0], "adamw_w_ff2")
    res_out = _adamw(scatter_wait(out_handle, in_token, "out")[0], w_out[0], m_w_out[0], v_w_out[0], "adamw_w_out")
    res_ada = _adamw(gw_ada[None], w_ada[0], m_w_ada[0], v_w_ada[0], "adamw_w_ada")
    res_in = _adamw(scatter_wait(in_handle, res_ada[0], "in")[0], w_in[0], m_w_in[0], v_w_in[0], "adamw_w_in")

    small_names = ["norm1_w", "norm2_w", "b_ada", "conv_w", "conv_b", "dt_bias", "a_log", "d_skip", "ssd_norm_w",
                   "q_norm_w", "k_norm_w", "attn_norm_w"]
    small_g = dict(norm1_w=g_norm1, norm2_w=g_norm2, b_ada=g_b_ada, conv_w=g_conv_w.reshape(1, KCONV * ccs),
                   conv_b=g_conv_b, dt_bias=g_dtb, a_log=g_alog, d_skip=g_dsk, ssd_norm_w=g_ssd_norm, q_norm_w=g_q_norm,
                   k_norm_w=g_k_norm, attn_norm_w=g_attn_norm)
    small_w = dict(norm1_w=(norm1_w, m_norm1_w, v_norm1_w), norm2_w=(norm2_w, m_norm2_w, v_norm2_w),
                   b_ada=(b_ada, m_b_ada, v_b_ada),
                   conv_w=tuple(t.reshape(1, KCONV * ccs) for t in (conv_w, m_conv_w, v_conv_w)),
                   conv_b=(conv_b, m_conv_b, v_conv_b), dt_bias=(dt_bias, m_dt_bias, v_dt_bias),
                   a_log=(a_log, m_a_log, v_a_log), d_skip=(d_skip, m_d_skip, v_d_skip),
                   ssd_norm_w=(ssd_norm_w, m_ssd_norm_w, v_ssd_norm_w), q_norm_w=(q_norm_w, m_q_norm_w, v_q_norm_w),
                   k_norm_w=(k_norm_w, m_k_norm_w, v_k_norm_w), attn_norm_w=(attn_norm_w, m_attn_norm_w, v_attn_norm_w))
    ssz = [_pad_lanes(small_g[n]).shape[1] for n in small_names]
    soff = [sum(ssz[:i]) for i in range(len(ssz))]
    srow = -(-sum(ssz) // LANES // 8) * 8

    def pack(ts, fill):
        t = jnp.concatenate([jnp.pad(t, ((0, 0), (0, (-t.shape[1]) % LANES)), constant_values=fill) for t in ts], axis=1)
        return jnp.pad(t, ((0, 0), (0, srow * LANES - t.shape[1])), constant_values=fill).reshape(srow, LANES)

    sg = pack([small_g[n] for n in small_names], 0.0)
    sw = pack([small_w[n][0] for n in small_names], 0.0)
    sm_ = pack([small_w[n][1] for n in small_names], 0.0)
    sv = pack([small_w[n][2] for n in small_names], 1.0)
    _, s_delta, s_m, s_v = _adamw(sg[None], sw, sm_, sv, "adamw_small", tm=srow)

    def unpack(t, n):
        i = small_names.index(n)
        return t.reshape(1, srow * LANES)[:, soff[i]:soff[i] + small_g[n].shape[1]].reshape(small_w[n][0].shape)

    loss = lax.psum(loss_p[0, 0], ("x", "y", "c"))
    big_res = dict(w_ada=res_ada, w_in=res_in, w_out=res_out, w_ff1=res_ff1, w_ff2=res_ff2)
    order = ["norm1_w", "norm2_w", "w_ada", "b_ada", "w_in", "conv_w", "conv_b", "dt_bias", "a_log", "d_skip",
             "ssd_norm_w", "q_norm_w", "k_norm_w", "attn_norm_w", "w_out", "w_ff1", "w_ff2"]
    grads, deltas, new_m, new_v = [], [], [], []
    for n in order:
        if n in big_res:
            g_, d_, m_, v_ = [t[None] for t in big_res[n]]
        else:
            g_ = small_g[n].reshape(small_w[n][0].shape)
            d_, m_, v_ = unpack(s_delta, n), unpack(s_m, n), unpack(s_v, n)
            if n == "conv_w":
                g_, d_, m_, v_ = [t.reshape(conv_w.shape) for t in (g_, d_, m_, v_)]
        grads.append(g_)
        deltas.append(d_)
        new_m.append(m_)
        new_v.append(v_)
    return (loss, grad_x[None], *grads, *deltas, *new_m, *new_v)
```

```python
import functools

import jax
import jax.numpy as jnp
from jax import lax
from jax.experimental import pallas as pl
from jax.experimental.pallas import tpu as pltpu

F32, BF16 = jnp.float32, jnp.bfloat16
EPS = 1e-6
HD = 64
NH_SSD = 16
NG = 4
NSTATE = 128
KCONV = 4
CHUNK = 128
NH_ATT = 16
PATTERNS = ((128, 1), (512, 4), (2048, 16))
ABLK = 128
QTILE = 128
LANES = 128
ADAM_LR, ADAM_B1, ADAM_B2, ADAM_EPS, ADAM_WD, ADAM_STEP = 0.001, 0.9, 0.999, 1e-08, 0.01, 10
VMEM_LIMIT = 56 * 1024 * 1024
MESH = pl.DeviceIdType.MESH
NEG = -1e30

_DN = {"nn": (((1,), (0,)), ((), ())), "nt": (((1,), (1,)), ((), ())), "tn": (((0,), (0,)), ((), ()))}


def _cparams(sem):
    return pltpu.CompilerParams(dimension_semantics=sem, vmem_limit_bytes=VMEM_LIMIT)


def _tile(n, cap):
    if n % LANES or n <= LANES:
        return n
    best = LANES
    for t in range(LANES, min(n, cap) + 1, LANES):
        if n % t == 0:
            best = t
    return best


def _silu(x):
    return x / (1.0 + jnp.exp(-x))


def _softplus(x):
    return jnp.maximum(x, 0.0) + jnp.log(1.0 + jnp.exp(-jnp.abs(x)))


def _dot(a, b, dims):
    return lax.dot_general(a.astype(BF16), b.astype(BF16), _DN[dims], preferred_element_type=F32)


def _matmul(a, b, dims, out_dtype, name, a_fn=None, epilogue=None, extras=(), tm=1024, tn=1024, tk=1024):
    if dims == "nn":
        (M, K), (_, N) = a.shape, b.shape
    elif dims == "nt":
        (M, K), (N, _) = a.shape, b.shape
    else:
        (K, M), (_, N) = a.shape, b.shape
    tm, tn, tk = _tile(M, tm), _tile(N, tn), _tile(K, tk)
    nk = K // tk
    ne = len(extras)

    def body(a_ref, b_ref, *rest):
        e_refs, o_ref = rest[:ne], rest[ne]
        av = a_ref[...]
        if a_fn is not None:
            av = a_fn(av)
        part = _dot(av, b_ref[...], dims)

        def finish(r):
            if epilogue is not None:
                r = epilogue(r, *[e[...] for e in e_refs])
            o_ref[...] = r.astype(out_dtype)

        if nk == 1:
            finish(part)
            return
        acc = rest[ne + 1]
        k = pl.program_id(2)

        @pl.when(k == 0)
        def _():
            acc[...] = part

        @pl.when(k > 0)
        def _():
            acc[...] += part

        @pl.when(k == nk - 1)
        def _():
            finish(acc[...])

    a_spec = pl.BlockSpec((tk, tm), lambda i, j, k: (k, i)) if dims == "tn" else pl.BlockSpec((tm, tk), lambda i, j, k: (i, k))
    b_spec = pl.BlockSpec((tn, tk), lambda i, j, k: (j, k)) if dims == "nt" else pl.BlockSpec((tk, tn), lambda i, j, k: (k, j))
    o_spec = pl.BlockSpec((tm, tn), lambda i, j, k: (i, j))
    return pl.pallas_call(
        body, name=name, grid=(M // tm, N // tn, nk),
        in_specs=[a_spec, b_spec] + [o_spec] * ne, out_specs=o_spec,
        out_shape=jax.ShapeDtypeStruct((M, N), out_dtype),
        scratch_shapes=[pltpu.VMEM((tm, tn), F32)] if nk > 1 else [],
        compiler_params=_cparams(("parallel", "parallel", "arbitrary")),
    )(a, b, *extras)


def _rows(name, fn, rows, consts, outs, accs, n_rows, tm=256):
    tm = min(tm, n_rows)
    nr, nc, no, na = len(rows), len(consts), len(outs), len(accs)

    def body(*refs):
        r_refs, c_refs = refs[:nr], refs[nr:nr + nc]
        o_refs, a_refs = refs[nr + nc:nr + nc + no], refs[nr + nc + no:]
        o_vals, a_vals = fn([r[...] for r in r_refs], [c[...] for c in c_refs])
        for ref, val in zip(o_refs, o_vals):
            ref[...] = val.astype(ref.dtype)
        if na:
            @pl.when(pl.program_id(0) == 0)
            def _():
                for ref in a_refs:
                    ref[...] = jnp.zeros_like(ref)
            for ref, val in zip(a_refs, a_vals):
                ref[...] += val

    in_specs = [pl.BlockSpec((tm, w), lambda i, cb=cb: (i, cb)) for (_, cb, w) in rows]
    in_specs += [pl.BlockSpec(cst.shape, lambda i, nd=cst.ndim: (0,) * nd) for cst in consts]
    out_specs = [pl.BlockSpec((tm, w), lambda i: (i, 0)) for (w, _) in outs]
    out_specs += [pl.BlockSpec(s, lambda i: (0, 0)) for s in accs]
    out_shape = [jax.ShapeDtypeStruct((n_rows, w), dt) for (w, dt) in outs]
    out_shape += [jax.ShapeDtypeStruct(s, F32) for s in accs]
    res = pl.pallas_call(
        body, name=name, grid=(n_rows // tm,), in_specs=in_specs, out_specs=out_specs, out_shape=out_shape,
        compiler_params=_cparams(("arbitrary",)),
    )(*[r[0] for r in rows], *consts)
    return res


def _normmod(x, nw, sc, sh):
    r = lax.rsqrt(jnp.mean(x * x, axis=-1, keepdims=True) + EPS)
    return (x * r) * nw * (1.0 + sc) + sh


def _resid_normmod(x, mix, g, nw, sc, sh):
    x2 = x + g * mix
    return x2, _normmod(x2, nw, sc, sh)


def _rmsw(o, w):
    return o * lax.rsqrt(jnp.mean(o * o, axis=-1, keepdims=True) + EPS) * w


def _lane_mask():
    return lax.broadcasted_iota(jnp.int32, (1, LANES), 1) < HD


def _headnorm(t, w, scale):
    lo = _lane_mask()
    t2 = t * t
    s0 = jnp.sum(jnp.where(lo, t2, 0.0), axis=1, keepdims=True)
    s1 = jnp.sum(jnp.where(lo, 0.0, t2), axis=1, keepdims=True)
    ms = jnp.where(lo, s0, s1) * (1.0 / HD)
    return t * lax.rsqrt(ms + EPS) * w * scale


def _conv_cols(n_ch):
    return _tile(n_ch, 256)


def _conv_fwd(proj, col0, n_ch, conv_w, conv_b, name):
    S = proj.shape[0]
    tc = _conv_cols(n_ch)

    def body(u_ref, w_ref, b_ref, o_ref):
        u = u_ref[...]
        row = lax.broadcasted_iota(jnp.int32, u.shape, 0)
        acc = b_ref[...] + w_ref[KCONV - 1:KCONV, :] * u
        for i in range(KCONV - 1):
            sh = KCONV - 1 - i
            acc = acc + w_ref[i:i + 1, :] * jnp.where(row >= sh, pltpu.roll(u, sh, 0), 0.0)
        o_ref[...] = _silu(acc)

    return pl.pallas_call(
        body, name=name, grid=(n_ch // tc,),
        in_specs=[pl.BlockSpec((S, tc), lambda j: (0, j + col0 // tc)),
                  pl.BlockSpec((KCONV, tc), lambda j: (0, j)), pl.BlockSpec((1, tc), lambda j: (0, j))],
        out_specs=pl.BlockSpec((S, tc), lambda j: (0, j)),
        out_shape=jax.ShapeDtypeStruct((S, n_ch), F32),
        compiler_params=_cparams(("parallel",)),
    )(proj, conv_w, conv_b)


def _conv_bwd(proj, col0, n_ch, conv_w, conv_b, dxbc, name):
    S = proj.shape[0]
    tc = _conv_cols(n_ch)

    def body(u_ref, w_ref, b_ref, g_ref, du_ref, dw_ref, db_ref):
        u = u_ref[...]
        row = lax.broadcasted_iota(jnp.int32, u.shape, 0)
        shifted = [jnp.where(row >= s, pltpu.roll(u, s, 0), 0.0) for s in range(1, KCONV)]
        acc = b_ref[...] + w_ref[KCONV - 1:KCONV, :] * u
        for i in range(KCONV - 1):
            acc = acc + w_ref[i:i + 1, :] * shifted[KCONV - 2 - i]
        sig = 1.0 / (1.0 + jnp.exp(-acc))
        dacc = g_ref[...] * (sig * (1.0 + acc * (1.0 - sig)))
        db_ref[...] = jnp.sum(dacc, axis=0, keepdims=True)
        du = w_ref[KCONV - 1:KCONV, :] * dacc
        dw_ref[KCONV - 1:KCONV, :] = jnp.sum(dacc * u, axis=0, keepdims=True)
        for i in range(KCONV - 1):
            sh = KCONV - 1 - i
            dw_ref[i:i + 1, :] = jnp.sum(dacc * shifted[sh - 1], axis=0, keepdims=True)
            du = du + w_ref[i:i + 1, :] * jnp.where(row < S - sh, pltpu.roll(dacc, S - sh, 0), 0.0)
        du_ref[...] = du

    return pl.pallas_call(
        body, name=name, grid=(n_ch // tc,),
        in_specs=[pl.BlockSpec((S, tc), lambda j: (0, j + col0 // tc)),
                  pl.BlockSpec((KCONV, tc), lambda j: (0, j)), pl.BlockSpec((1, tc), lambda j: (0, j)),
                  pl.BlockSpec((S, tc), lambda j: (0, j))],
        out_specs=[pl.BlockSpec((S, tc), lambda j: (0, j)), pl.BlockSpec((KCONV, tc), lambda j: (0, j)),
                   pl.BlockSpec((1, tc), lambda j: (0, j))],
        out_shape=[jax.ShapeDtypeStruct((S, n_ch), F32), jax.ShapeDtypeStruct((KCONV, n_ch), F32),
                   jax.ShapeDtypeStruct((1, n_ch), F32)],
        compiler_params=_cparams(("parallel",)),
    )(proj, conv_w, conv_b, dxbc)


@functools.partial(jax.custom_vjp, nondiff_argnums=(2,))
def _mm(a, b, dims):
    return _dot(a, b, dims)


def _mm_fwd(a, b, dims):
    return _dot(a, b, dims), (a, b)


def _mm_bwd(dims, res, g):
    a, b = res
    if dims == "nn":
        return _dot(g, b, "nt"), _dot(a, g, "tn")
    if dims == "nt":
        return _dot(g, b, "nn"), _dot(g, a, "tn")
    return _dot(b, g, "nt"), _dot(a, g, "nn")


_mm.defvjp(_mm_fwd, _mm_bwd)


def _tri_dot(x, upper):
    n = x.shape[0]
    r = lax.broadcasted_iota(jnp.int32, (n, n), 0)
    c = lax.broadcasted_iota(jnp.int32, (n, n), 1)
    t = jnp.where((r <= c) if upper else (r >= c), 1.0, 0.0)
    return lax.dot_general(t, x, _DN["nn"], precision=lax.Precision.HIGHEST, preferred_element_type=F32)


@jax.custom_vjp
def _cumsum_rows(x):
    return _tri_dot(x, False)


_cumsum_rows.defvjp(lambda x: (_tri_dot(x, False), None), lambda _, g: (_tri_dot(g, True),))


def _ssd_chunk(xs_p, bm_g, cm_g, dtr, z_p, dtb, alog, dsk, nw_p, h_p):
    L = dtr.shape[0]
    n_pairs = len(xs_p)
    ppg = n_pairs // len(bm_g)
    lane = lax.broadcasted_iota(jnp.int32, (1, LANES), 1)
    sub = lax.broadcasted_iota(jnp.int32, (LANES, 1), 0)
    lo = lane < HD
    row_l = lax.broadcasted_iota(jnp.int32, (L, 1), 0)
    tri = lax.broadcasted_iota(jnp.int32, (L, L), 0) >= lax.broadcasted_iota(jnp.int32, (L, L), 1)

    dt = _softplus(dtr + dtb)
    acs = _cumsum_rows(dt * (-jnp.exp(alog)))
    acs_t = acs.T
    a_last = jnp.sum(jnp.where(row_l == L - 1, acs, 0.0), axis=0, keepdims=True)
    e_acs = jnp.exp(acs)
    dec = jnp.exp(a_last - acs)
    cdec = jnp.exp(a_last)

    def colv(m, h):
        return jnp.sum(jnp.where(lane == h, m, 0.0), axis=1, keepdims=True)

    def rowv(mt, h):
        return jnp.sum(jnp.where(sub == h, mt, 0.0), axis=0, keepdims=True)

    def pair(m, h0):
        return jnp.where(lo, colv(m, h0), colv(m, h0 + 1))

    ys, hs = [], []
    cb = None
    for p in range(n_pairs):
        g, h0 = p // ppg, 2 * p
        bmat, cmat = bm_g[g], cm_g[g]
        if p % ppg == 0:
            cb = _mm(cmat, bmat, "nt")
        x = xs_p[p]
        xdt = x * pair(dt, h0)
        yd = []
        for h in (h0, h0 + 1):
            seg = colv(acs, h) - rowv(acs_t, h)
            lm = jnp.where(tri, jnp.exp(jnp.where(tri, seg, 0.0)), 0.0)
            yd.append(_mm(cb * lm, xdt, "nn"))
        y = jnp.where(lo, yd[0], yd[1])
        y = y + _mm(cmat, h_p[p], "nt") * pair(e_acs, h0)
        st = _mm(xdt * pair(dec, h0), bmat, "tn")
        cd_col = jnp.where(sub < HD, colv(cdec, h0), colv(cdec, h0 + 1))
        hs.append(h_p[p] * cd_col + st)
        ys.append(y + pair(dsk, h0) * x)

    y2 = [ys[p] * _silu(z_p[p]) for p in range(n_pairs)]
    outs = []
    for g in range(len(bm_g)):
        ps = range(g * ppg, (g + 1) * ppg)
        ss = sum(jnp.sum(y2[p] * y2[p], axis=1, keepdims=True) for p in ps)
        rs = lax.rsqrt(ss * (1.0 / (ppg * LANES)) + EPS)
        outs += [y2[p] * rs * nw_p[p] for p in ps]
    return outs, hs


def _ssd_slices(xbc_ref, z_ref, nw_ref, di):
    n_pairs = di // LANES
    xs_p = [xbc_ref[:, p * LANES:(p + 1) * LANES] for p in range(n_pairs)]
    bm_g = [xbc_ref[:, di + g * NSTATE:di + (g + 1) * NSTATE] for g in range(NG)]
    cm_g = [xbc_ref[:, di + (NG + g) * NSTATE:di + (NG + g + 1) * NSTATE] for g in range(NG)]
    z_p = [z_ref[:, p * LANES:(p + 1) * LANES] for p in range(n_pairs)]
    nw_p = [nw_ref[:, p * LANES:(p + 1) * LANES] for p in range(n_pairs)]
    return xs_p, bm_g, cm_g, z_p, nw_p


def _ssd_fwd(xbc, proj, dt_cb, dtb, alog, dsk, nw, name):
    S, cc = xbc.shape
    di = NH_SSD * HD
    n_pairs = di // LANES
    nchunk = S // CHUNK

    def body(xbc_ref, z_ref, dtr_ref, dtb_ref, alog_ref, dsk_ref, nw_ref, y_ref, hs_ref, h_scr):
        @pl.when(pl.program_id(0) == 0)
        def _():
            h_scr[...] = jnp.zeros_like(h_scr)

        xs_p, bm_g, cm_g, z_p, nw_p = _ssd_slices(xbc_ref, z_ref, nw_ref, di)
        h_p = [h_scr[p * LANES:(p + 1) * LANES, :] for p in range(n_pairs)]
        hs_ref[...] = h_scr[...]
        outs, hs = _ssd_chunk(xs_p, bm_g, cm_g, dtr_ref[...], z_p, dtb_ref[...], alog_ref[...], dsk_ref[...], nw_p, h_p)
        for p in range(n_pairs):
            y_ref[:, p * LANES:(p + 1) * LANES] = outs[p].astype(y_ref.dtype)
            h_scr[p * LANES:(p + 1) * LANES, :] = hs[p]

    vec = pl.BlockSpec((1, LANES), lambda c: (0, 0))
    return pl.pallas_call(
        body, name=name, grid=(nchunk,),
        in_specs=[pl.BlockSpec((CHUNK, cc), lambda c: (c, 0)), pl.BlockSpec((CHUNK, di), lambda c: (c, 0)),
                  pl.BlockSpec((CHUNK, LANES), lambda c: (c, dt_cb)), vec, vec, vec,
                  pl.BlockSpec((1, di), lambda c: (0, 0))],
        out_specs=[pl.BlockSpec((CHUNK, di), lambda c: (c, 0)), pl.BlockSpec((None, di, NSTATE), lambda c: (c, 0, 0))],
        out_shape=[jax.ShapeDtypeStruct((S, di), BF16), jax.ShapeDtypeStruct((nchunk, di, NSTATE), F32)],
        scratch_shapes=[pltpu.VMEM((di, NSTATE), F32)],
        compiler_params=_cparams(("arbitrary",)),
    )(xbc, proj, proj, dtb, alog, dsk, nw)


def _ssd_bwd(xbc, proj, dt_cb, dtb, alog, dsk, nw, hsave, dy, name):
    S, cc = xbc.shape
    di = NH_SSD * HD
    n_pairs = di // LANES
    nchunk = S // CHUNK

    def body(xbc_ref, z_ref, dtr_ref, dtb_ref, alog_ref, dsk_ref, nw_ref, hs_ref, dy_ref,
             dxbc_ref, dz_ref, ddtr_ref, ddtb_ref, dalog_ref, ddsk_ref, dnw_ref, dh_scr):
        @pl.when(pl.program_id(0) == 0)
        def _():
            dh_scr[...] = jnp.zeros_like(dh_scr)
            ddtb_ref[...] = jnp.zeros_like(ddtb_ref)
            dalog_ref[...] = jnp.zeros_like(dalog_ref)
            ddsk_ref[...] = jnp.zeros_like(ddsk_ref)
            dnw_ref[...] = jnp.zeros_like(dnw_ref)

        xs_p, bm_g, cm_g, z_p, nw_p = _ssd_slices(xbc_ref, z_ref, nw_ref, di)
        h_p = [hs_ref[p * LANES:(p + 1) * LANES, :] for p in range(n_pairs)]
        dy_p = [dy_ref[:, p * LANES:(p + 1) * LANES].astype(F32) for p in range(n_pairs)]
        dh_p = [dh_scr[p * LANES:(p + 1) * LANES, :] for p in range(n_pairs)]
        _, vjp = jax.vjp(_ssd_chunk, xs_p, bm_g, cm_g, dtr_ref[...], z_p, dtb_ref[...], alog_ref[...], dsk_ref[...],
                         nw_p, h_p)
        dxs, dbm, dcm, ddtr, dz, ddtb, dalog, ddsk, dnw, dh = vjp((dy_p, dh_p))
        for p in range(n_pairs):
            sl = slice(p * LANES, (p + 1) * LANES)
            dxbc_ref[:, sl] = dxs[p]
            dz_ref[:, sl] = dz[p]
            dnw_ref[:, sl] += dnw[p]
            dh_scr[sl, :] = dh[p]
        for g in range(NG):
            dxbc_ref[:, di + g * NSTATE:di + (g + 1) * NSTATE] = dbm[g]
            dxbc_ref[:, di + (NG + g) * NSTATE:di + (NG + g + 1) * NSTATE] = dcm[g]
        ddtr_ref[...] = ddtr
        ddtb_ref[...] += ddtb
        dalog_ref[...] += dalog
        ddsk_ref[...] += ddsk

    last = nchunk - 1
    vec = pl.BlockSpec((1, LANES), lambda c: (0, 0))
    return pl.pallas_call(
        body, name=name, grid=(nchunk,),
        in_specs=[pl.BlockSpec((CHUNK, cc), lambda c: (last - c, 0)), pl.BlockSpec((CHUNK, di), lambda c: (last - c, 0)),
                  pl.BlockSpec((CHUNK, LANES), lambda c: (last - c, dt_cb)), vec, vec, vec,
                  pl.BlockSpec((1, di), lambda c: (0, 0)),
                  pl.BlockSpec((None, di, NSTATE), lambda c: (last - c, 0, 0)),
                  pl.BlockSpec((CHUNK, di), lambda c: (last - c, 0))],
        out_specs=[pl.BlockSpec((CHUNK, cc), lambda c: (last - c, 0)), pl.BlockSpec((CHUNK, di), lambda c: (last - c, 0)),
                   pl.BlockSpec((CHUNK, LANES), lambda c: (last - c, 0)), vec, vec, vec,
                   pl.BlockSpec((1, di), lambda c: (0, 0))],
        out_shape=[jax.ShapeDtypeStruct((S, cc), F32), jax.ShapeDtypeStruct((S, di), F32),
                   jax.ShapeDtypeStruct((S, LANES), F32), jax.ShapeDtypeStruct((1, LANES), F32),
                   jax.ShapeDtypeStruct((1, LANES), F32), jax.ShapeDtypeStruct((1, LANES), F32),
                   jax.ShapeDtypeStruct((1, di), F32)],
        scratch_shapes=[pltpu.VMEM((di, NSTATE), F32)],
        compiler_params=_cparams(("arbitrary",)),
    )(xbc, proj, proj, dtb, alog, dsk, nw, hsave, dy)


def _band_masks(rows_q, rows_k):
    qi = lax.broadcasted_iota(jnp.int32, (rows_q, rows_k), 0)
    ki = lax.broadcasted_iota(jnp.int32, (rows_q, rows_k), 1)
    return qi, ki


def _class_chunks(n_rows, d):
    per_class = n_rows // d
    ch = min(per_class, 256)
    out = []
    for r in range(d):
        for c0 in range(0, per_class, ch):
            tok = pl.ds(c0, ch) if d == 1 else pl.ds(r + d * c0, ch, stride=d)
            out.append((tok, pl.ds(r * per_class + c0, ch)))
    return out


def _to_class_order(src_ref, dst_ref, n_rows, d):
    for tok, cls in _class_chunks(n_rows, d):
        dst_ref[cls, :] = src_ref[tok, :].astype(dst_ref.dtype)


def _blk_rows(t):
    return pl.ds(pl.multiple_of(t * ABLK, ABLK), ABLK)


def _head_lanes(msk, t, t_rolled):
    return jnp.where(msk, t, t_rolled)


def _zero_unless(msk, t):
    return jnp.where(msk, t, jnp.zeros_like(t))


def _attn_fwd(qn, kn, proj, v_cb, name):
    S, ad = qn.shape
    nb = S // ABLK
    nbr = len(PATTERNS)

    def body(q_ref, k_ref, v_ref, o_ref, lse_ref, qc, kc, vc, ob, mb, lb, m_s, l_s):
        lo = _lane_mask()
        qi, ki = _band_masks(ABLK, 2 * ABLK)
        band, in_cur, prev_ok = ki <= qi + ABLK, ki >= ABLK, ki >= qi
        for bi, (_, d) in enumerate(PATTERNS):
            nbc = S // d // ABLK
            first, last = bi == 0, bi == nbr - 1
            qs, ks, vs = q_ref, k_ref, v_ref
            if d > 1:
                qs, ks, vs = qc, kc, vc
                for src, dst in ((q_ref, qc), (k_ref, kc), (v_ref, vc)):
                    _to_class_order(src, dst, S, d)
            o_dst, m_dst, l_dst = (o_ref, m_s, l_s) if first else (ob, mb, lb)

            def blk(t, carry, nbc=nbc, qs=qs, ks=ks, vs=vs, o_dst=o_dst, m_dst=m_dst, l_dst=l_dst):
                rows, prow = _blk_rows(t), _blk_rows(jnp.maximum(t - 1, 0))
                has_prev = (t % nbc) != 0
                kk = jnp.concatenate([ks[prow, :], ks[rows, :]], axis=0)
                vv = jnp.concatenate([vs[prow, :], vs[rows, :]], axis=0)
                for u in range(ABLK // QTILE):
                    sub = pl.ds(pl.multiple_of(t * ABLK + u * QTILE, QTILE), QTILE)
                    sl = slice(u * QTILE, (u + 1) * QTILE)
                    valid = band[sl] & (in_cur[sl] | (prev_ok[sl] & has_prev))
                    qv = qs[sub, :]
                    os_, ms_, ls_ = [], [], []
                    for msk in (lo, jnp.logical_not(lo)):
                        s = jnp.where(valid, _dot(_zero_unless(msk, qv), kk, "nt"), NEG)
                        m = jnp.max(s, axis=1, keepdims=True)
                        p = jnp.exp(s - m)
                        os_.append(_dot(p, vv, "nn"))
                        ms_.append(m)
                        ls_.append(jnp.sum(p, axis=1, keepdims=True))
                    o_dst[sub, :] = jnp.where(lo, os_[0], os_[1])
                    m_dst[sub, :] = jnp.where(lo, ms_[0], ms_[1])
                    l_dst[sub, :] = jnp.where(lo, ls_[0], ls_[1])
                return carry

            lax.fori_loop(0, nb, blk, 0, unroll=8)
            if first:
                continue
            for tok, cls in _class_chunks(S, d):
                m_old, m_b = m_s[tok, :], mb[cls, :]
                m_new = jnp.maximum(m_old, m_b)
                a, b = jnp.exp(m_old - m_new), jnp.exp(m_b - m_new)
                l_new = a * l_s[tok, :] + b * lb[cls, :]
                o_new = a * o_ref[tok, :] + b * ob[cls, :]
                if last:
                    o_ref[tok, :] = o_new / l_new
                    lse_ref[tok, :] = m_new + jnp.log(l_new)
                else:
                    o_ref[tok, :] = o_new
                    m_s[tok, :] = m_new
                    l_s[tok, :] = l_new

    col = pl.BlockSpec((S, LANES), lambda h: (0, h))
    return pl.pallas_call(
        body, name=name, grid=(ad // LANES,),
        in_specs=[col, col, pl.BlockSpec((S, LANES), lambda h: (0, h + v_cb))], out_specs=[col, col],
        out_shape=[jax.ShapeDtypeStruct((S, ad), F32), jax.ShapeDtypeStruct((S, ad), F32)],
        scratch_shapes=[pltpu.VMEM((S, LANES), BF16)] * 3 + [pltpu.VMEM((S, LANES), F32)] * 5,
        compiler_params=_cparams(("parallel",)),
    )(qn, kn, proj)


def _attn_bwd(qn, kn, proj, v_cb, do, lse, dd, name):
    S, ad = qn.shape
    nb = S // ABLK

    def body(q_ref, k_ref, v_ref, do_ref, lse_ref, dd_ref, dq_ref, dk_ref, dv_ref,
             qc, kc, vc, doc, lsec, ddc, dqc, dkc, dvc):
        lo = _lane_mask()
        qi, ki = _band_masks(ABLK, ABLK)
        cur_ok, prev_ok = ki <= qi, ki >= qi
        for bi, (_, d) in enumerate(PATTERNS):
            nbc = S // d // ABLK
            first = bi == 0
            token_order = (q_ref, k_ref, v_ref, do_ref, lse_ref, dd_ref)
            class_order = (qc, kc, vc, doc, lsec, ddc)
            if d > 1:
                for src, dst in zip(token_order, class_order):
                    _to_class_order(src, dst, S, d)
            qs, ks, vs, dos, lses, dds = class_order if d > 1 else token_order
            dq_dst, dk_dst, dv_dst = (dq_ref, dk_ref, dv_ref) if first else (dqc, dkc, dvc)
            dk_dst[...] = jnp.zeros_like(dk_dst)
            dv_dst[...] = jnp.zeros_like(dv_dst)

            def blk(t, carry, nbc=nbc, qs=qs, ks=ks, vs=vs, dos=dos, lses=lses, dds=dds,
                    dq_dst=dq_dst, dk_dst=dk_dst, dv_dst=dv_dst):
                rows, prow = _blk_rows(t), _blk_rows(jnp.maximum(t - 1, 0))
                has_prev = (t % nbc) != 0
                qv, dov, lse_b, dd_b = qs[rows, :], dos[rows, :], lses[rows, :], dds[rows, :]
                lse_r, dd_r = pltpu.roll(lse_b, HD, 1), pltpu.roll(dd_b, HD, 1)
                heads = []
                for msk in (lo, jnp.logical_not(lo)):
                    heads.append((_zero_unless(msk, qv), _zero_unless(msk, dov), _head_lanes(msk, lse_b, lse_r),
                                  _head_lanes(msk, dd_b, dd_r)))
                dqs = [None, None]
                for krows, vmask in ((rows, cur_ok), (prow, prev_ok & has_prev)):
                    kv, vv = ks[krows, :], vs[krows, :]
                    dk = jnp.zeros((ABLK, LANES), F32)
                    dv = jnp.zeros((ABLK, LANES), F32)
                    for hi, (qh, doh, lse_h, dd_h) in enumerate(heads):
                        s = jnp.where(vmask, _dot(qh, kv, "nt"), NEG)
                        p = jnp.exp(s - lse_h)
                        ds = p * (_dot(doh, vv, "nt") - dd_h)
                        dqh = _dot(ds, kv, "nn")
                        dqs[hi] = dqh if dqs[hi] is None else dqs[hi] + dqh
                        dv = dv + _dot(p, doh, "tn")
                        dk = dk + _dot(ds, qh, "tn")
                    dk_dst[krows, :] += dk
                    dv_dst[krows, :] += dv
                dq_dst[rows, :] = jnp.where(lo, dqs[0], dqs[1])
                return carry

            lax.fori_loop(0, nb, blk, 0, unroll=4)
            if not first:
                for tok, cls in _class_chunks(S, d):
                    dq_ref[tok, :] = dq_ref[tok, :] + dqc[cls, :]
                    dk_ref[tok, :] = dk_ref[tok, :] + dkc[cls, :]
                    dv_ref[tok, :] = dv_ref[tok, :] + dvc[cls, :]

    col = pl.BlockSpec((S, LANES), lambda h: (0, h))
    col1 = pl.BlockSpec((S, LANES), lambda h: (0, h), pipeline_mode=pl.Buffered(1))
    vcol1 = pl.BlockSpec((S, LANES), lambda h: (0, h + v_cb), pipeline_mode=pl.Buffered(1))
    return pl.pallas_call(
        body, name=name, grid=(ad // LANES,),
        in_specs=[col, col, vcol1, col1, col1, col1], out_specs=[col, col, col],
        out_shape=[jax.ShapeDtypeStruct((S, ad), F32)] * 3,
        scratch_shapes=[pltpu.VMEM((S, LANES), BF16)] * 4 + [pltpu.VMEM((S, LANES), F32)] * 5,
        compiler_params=_cparams(("parallel",)),
    )(qn, kn, proj, do, lse, dd)


def _coords():
    return lax.axis_index("x"), lax.axis_index("y"), lax.axis_index("c")


def _exchange8(xs, per_dest, name):
    n = len(xs)
    blk = [x.shape[1:] if per_dest else x.shape for x in xs]

    def body(*refs):
        ins, outs = refs[:n], refs[n:2 * n]
        send_sems, recv_sems, local_sems = refs[2 * n:]
        x, y, c = _coords()
        sibling = (x, y, 1 - c)
        chips = [(1 - x, y), (x, 1 - y), (1 - x, 1 - y)]
        first, passed, mine = [], [], []
        for a in range(n):
            def src_for(cx, cy, a=a):
                return ins[a].at[2 * cx + cy] if per_dest else ins[a]

            def slot(px, py, pc, a=a):
                return outs[a].at[4 * px + 2 * py + pc]

            def copy(k, src, dst, to, a=a):
                return pltpu.make_async_remote_copy(src_ref=src, dst_ref=dst, send_sem=send_sems.at[7 * a + k],
                                                    recv_sem=recv_sems.at[7 * a + k], device_id=to, device_id_type=MESH)

            m = pltpu.make_async_copy(src_for(x, y), slot(x, y, c), local_sems.at[a])
            m.start()
            mine.append(m)
            cps = [copy(0, src_for(x, y), slot(x, y, c), sibling)]
            cps += [copy(1 + j, src_for(*chip), slot(x, y, c), (*chip, c)) for j, chip in enumerate(chips)]
            for cp in cps:
                cp.start()
            first += cps
        for a in range(n):
            def slot(px, py, pc, a=a):
                return outs[a].at[4 * px + 2 * py + pc]

            def copy(k, src, dst, to, a=a):
                return pltpu.make_async_remote_copy(src_ref=src, dst_ref=dst, send_sem=send_sems.at[7 * a + k],
                                                    recv_sem=recv_sems.at[7 * a + k], device_id=to, device_id_type=MESH)

            for j, chip in enumerate(chips):
                copy(1 + j, slot(*chip, c), slot(*chip, c), (*chip, c)).wait_recv()
                fw = copy(4 + j, slot(*chip, c), slot(*chip, c), sibling)
                fw.start()
                passed.append(fw)
        for a in range(n):
            def slot(px, py, pc, a=a):
                return outs[a].at[4 * px + 2 * py + pc]

            def copy(k, src, dst, to, a=a):
                return pltpu.make_async_remote_copy(src_ref=src, dst_ref=dst, send_sem=send_sems.at[7 * a + k],
                                                    recv_sem=recv_sems.at[7 * a + k], device_id=to, device_id_type=MESH)

            copy(0, slot(x, y, 1 - c), slot(x, y, 1 - c), sibling).wait_recv()
            for j, chip in enumerate(chips):
                copy(4 + j, slot(*chip, 1 - c), slot(*chip, 1 - c), sibling).wait_recv()
        for cp in first + passed:
            cp.wait_send()
        for m in mine:
            m.wait()

    anyspec = pl.BlockSpec(memory_space=pl.ANY)
    res = pl.pallas_call(
        body, name=name, in_specs=[anyspec] * n, out_specs=[anyspec] * n,
        out_shape=[jax.ShapeDtypeStruct((8,) + tuple(b), x.dtype) for b, x in zip(blk, xs)],
        scratch_shapes=[pltpu.SemaphoreType.DMA((7 * n,)), pltpu.SemaphoreType.DMA((7 * n,)),
                        pltpu.SemaphoreType.DMA((n,))],
    )(*xs)
    return list(res)


def _pair_swap(xs, name):
    n = len(xs)

    def body(*refs):
        ins, outs = refs[:n], refs[n:2 * n]
        send_sems, recv_sems = refs[2 * n:]
        x, y, c = _coords()
        cps = [pltpu.make_async_remote_copy(src_ref=ins[a].at[1 - c], dst_ref=outs[a], send_sem=send_sems.at[a],
                                            recv_sem=recv_sems.at[a], device_id=(x, y, 1 - c), device_id_type=MESH)
               for a in range(n)]
        for cp in cps:
            cp.start()
        for cp in cps:
            cp.wait()

    anyspec = pl.BlockSpec(memory_space=pl.ANY)
    res = pl.pallas_call(
        body, name=name, in_specs=[anyspec] * n, out_specs=[anyspec] * n,
        out_shape=[jax.ShapeDtypeStruct(x.shape[1:], x.dtype) for x in xs],
        scratch_shapes=[pltpu.SemaphoreType.DMA((n,)), pltpu.SemaphoreType.DMA((n,))],
    )(*xs)
    return list(res)


_HBM = pl.BlockSpec(memory_space=pltpu.HBM)
_SEM = pl.BlockSpec(memory_space=pltpu.SEMAPHORE)
_EFFECT = pltpu.SideEffectType.DATAFLOW_SIDE_EFFECTING


N_PEER = 3


def _peer(x, y, j):
    dx, dy = (j + 1) >> 1 & 1, (j + 1) & 1
    return (1 - x if dx else x, 1 - y if dy else y)


def _spread_copies(s_refs, l_refs, send_sems, recv_sems, per_dest):
    x, y, c = _coords()
    me = 4 * x + 2 * y + c
    cps = []
    for a in range(len(s_refs)):
        for j in range(N_PEER):
            tx, ty = _peer(x, y, j)
            src = s_refs[a].at[2 * tx + ty] if per_dest else s_refs[a]
            cps.append(pltpu.make_async_remote_copy(src_ref=src, dst_ref=l_refs[a].at[me],
                                                    send_sem=send_sems.at[N_PEER * a + j],
                                                    recv_sem=recv_sems.at[N_PEER * a + j], device_id=(tx, ty, c),
                                                    device_id_type=MESH))
    return cps


def _sibling_fill(lands, name):
    n = len(lands)

    def body(*refs):
        outs, send_sems, recv_sems = refs[n:2 * n], refs[2 * n], refs[2 * n + 1]
        x, y, c = _coords()
        cps = [pltpu.make_async_remote_copy(src_ref=outs[a].at[2 * k + c], dst_ref=outs[a].at[2 * k + c],
                                            send_sem=send_sems.at[4 * a + k], recv_sem=recv_sems.at[4 * a + k],
                                            device_id=(x, y, 1 - c), device_id_type=MESH)
               for a in range(n) for k in range(4)]
        for cp in cps:
            cp.start()
        for cp in cps:
            cp.wait()

    anyspec = pl.BlockSpec(memory_space=pl.ANY)
    res = pl.pallas_call(
        body, name=name, in_specs=[anyspec] * n, out_specs=[anyspec] * n,
        out_shape=[jax.ShapeDtypeStruct(t.shape, t.dtype) for t in lands], input_output_aliases={i: i for i in range(n)},
        scratch_shapes=[pltpu.SemaphoreType.DMA((4 * n,)), pltpu.SemaphoreType.DMA((4 * n,))],
    )(*lands)
    return list(res)


def _spread_start(srcs, per_dest, dev, chip, name):
    n = len(srcs)
    lands = []
    for s in srcs:
        own = lax.dynamic_index_in_dim(s, chip, 0, keepdims=False) if per_dest else s
        lands.append(lax.dynamic_update_index_in_dim(lax.empty((8,) + own.shape, own.dtype), own, dev, 0))

    def body(*refs):
        s_refs, l_refs, send_sems, recv_sems, token = refs[:n], refs[n:2 * n], refs[2 * n], refs[2 * n + 1], refs[-1]
        for cp in _spread_copies(s_refs, l_refs, send_sems, recv_sems, per_dest):
            cp.start()
        token[...] = jnp.zeros_like(token)

    hbm_in = [pltpu.with_memory_space_constraint(t, pltpu.HBM) for t in list(srcs) + lands]
    outs = pl.pallas_call(
        body, name=name,
        out_shape=(pltpu.SemaphoreType.DMA((N_PEER * n,)), pltpu.SemaphoreType.DMA((N_PEER * n,)),
                   *[pltpu.HBM(t.shape, t.dtype) for t in hbm_in], jax.ShapeDtypeStruct((8, LANES), F32)),
        in_specs=[_HBM] * (2 * n), out_specs=(_SEM, _SEM, *[_HBM] * (2 * n), pl.BlockSpec(memory_space=pltpu.VMEM)),
        input_output_aliases={i: 2 + i for i in range(2 * n)},
        compiler_params=pltpu.CompilerParams(has_side_effects=_EFFECT),
    )(*hbm_in)
    return (outs[0], outs[1], list(outs[2:2 + n]), list(outs[2 + n:2 + 2 * n])), outs[-1]


def _spread_wait(handle, per_dest, after, name):
    send_sems, recv_sems, srcs, lands = handle
    n = len(srcs)

    def body(*refs):
        s_refs, l_refs, send_ref, recv_ref = refs[:n], refs[n:2 * n], refs[2 * n], refs[2 * n + 1]
        for cp in _spread_copies(s_refs, l_refs, send_ref, recv_ref, per_dest):
            cp.wait_send()
            cp.wait_recv()

    outs = pl.pallas_call(
        body, name=name, out_shape=tuple(pltpu.HBM(t.shape, t.dtype) for t in srcs + lands),
        in_specs=[_HBM] * (2 * n) + [_SEM, _SEM, pl.BlockSpec(memory_space=pl.ANY)], out_specs=tuple([_HBM] * (2 * n)),
        input_output_aliases={i: i for i in range(2 * n)},
        compiler_params=pltpu.CompilerParams(has_side_effects=_EFFECT),
    )(*srcs, *lands, send_sems, recv_sems, after)
    return list(outs[n:])


def _pair_add(g2, theirs, half, name, tm=256):
    _, n, cdim = g2.shape
    tm = min(tm, n)

    def body(h_ref, a_ref, b_ref, o_ref):
        o_ref[...] = (a_ref[...] + b_ref[...]).astype(o_ref.dtype)

    grid_spec = pltpu.PrefetchScalarGridSpec(
        num_scalar_prefetch=1, grid=(n // tm,),
        in_specs=[pl.BlockSpec((None, tm, cdim), lambda i, h: (h[0], i, 0)), pl.BlockSpec((tm, cdim), lambda i, h: (i, 0))],
        out_specs=pl.BlockSpec((tm, cdim), lambda i, h: (i, 0)))
    return pl.pallas_call(body, name=name, grid_spec=grid_spec, out_shape=jax.ShapeDtypeStruct((n, cdim), BF16),
                          compiler_params=_cparams(("parallel",)))(half.reshape(1).astype(jnp.int32), g2, theirs)


def _adamw_math(w, g, m, v):
    m = ADAM_B1 * m + (1.0 - ADAM_B1) * g
    v = ADAM_B2 * v + (1.0 - ADAM_B2) * (g * g)
    m_hat = m / (1.0 - ADAM_B1 ** ADAM_STEP)
    v_hat = v / (1.0 - ADAM_B2 ** ADAM_STEP)
    delta = -ADAM_LR * (m_hat / (jnp.sqrt(v_hat) + ADAM_EPS) + ADAM_WD * w)
    return delta, m, v


def _adamw(parts, w, m, v, name, tm=128):
    npart, R, C = parts.shape
    tm = min(tm, R)

    def body(p_ref, w_ref, m_ref, v_ref, g_out, d_out, m_out, v_out):
        g = p_ref[0].astype(F32)
        for i in range(1, npart):
            g = g + p_ref[i].astype(F32)
        d, mm, vv = _adamw_math(w_ref[...], g, m_ref[...], v_ref[...])
        g_out[...] = g
        d_out[...] = d
        m_out[...] = mm
        v_out[...] = vv

    spec = pl.BlockSpec((tm, C), lambda i: (i, 0))
    return pl.pallas_call(
        body, name=name, grid=(R // tm,),
        in_specs=[pl.BlockSpec((npart, tm, C), lambda i: (0, i, 0)), spec, spec, spec], out_specs=[spec] * 4,
        out_shape=[jax.ShapeDtypeStruct((R, C), F32)] * 4,
        compiler_params=_cparams(("parallel",)),
    )(parts, w, m, v)


def _sum_parts(parts, name):
    npart, R, C = parts.shape

    def body(p_ref, o_ref):
        g = p_ref[0]
        for i in range(1, npart):
            g = g + p_ref[i]
        o_ref[...] = g

    return pl.pallas_call(body, name=name, out_shape=jax.ShapeDtypeStruct((R, C), F32))(parts)


def _mod_fwd(c_all, w_ada, b_sh, name):
    def body(c_ref, w_ref, b_ref, o_ref):
        o_ref[...] = _dot(_silu(c_ref[...]), w_ref[...], "nn") + b_ref[...]

    return pl.pallas_call(body, name=name, out_shape=jax.ShapeDtypeStruct((c_all.shape[0], w_ada.shape[1]), F32),
                          compiler_params=pltpu.CompilerParams(vmem_limit_bytes=VMEM_LIMIT))(c_all, w_ada, b_sh)


def _mod_wgrad(c_all, dmod_sh, name):
    def body(c_ref, d_ref, o_ref):
        o_ref[...] = _dot(_silu(c_ref[...]), d_ref[...], "tn")

    return pl.pallas_call(body, name=name, out_shape=jax.ShapeDtypeStruct((c_all.shape[1], dmod_sh.shape[1]), F32),
                          compiler_params=pltpu.CompilerParams(vmem_limit_bytes=VMEM_LIMIT))(c_all, dmod_sh)


def _pad_lanes(v):
    return jnp.pad(v, ((0, 0), (0, (-v.shape[1]) % LANES)))


def kernel(x, c, norm1_w, norm2_w, w_ada, b_ada, w_in, conv_w, conv_b, dt_bias, a_log, d_skip, ssd_norm_w, q_norm_w, k_norm_w, attn_norm_w, w_out, w_ff1, w_ff2, loss_target, m_norm1_w, m_norm2_w, m_w_ada, m_b_ada, m_w_in, m_conv_w, m_conv_b, m_dt_bias, m_a_log, m_d_skip, m_ssd_norm_w, m_q_norm_w, m_k_norm_w, m_attn_norm_w, m_w_out, m_w_ff1, m_w_ff2, v_norm1_w, v_norm2_w, v_w_ada, v_b_ada, v_w_in, v_conv_w, v_conv_b, v_dt_bias, v_a_log, v_d_skip, v_ssd_norm_w, v_q_norm_w, v_k_norm_w, v_attn_norm_w, v_w_out, v_w_ff1, v_w_ff2):
    xi, yi, ci = _coords()
    chip = 2 * xi + yi
    dev = 2 * chip + ci
    xs, tgt = x[0], loss_target[0]
    S, D = xs.shape
    DI, AD = NH_SSD * HD, NH_ATT * HD
    CC = DI + 2 * NG * NSTATE
    PW = DI + CC + 3 * AD + LANES
    DFF = w_ff1.shape[2] * 4
    MIX = DI + AD
    o_xbc, o_q, o_k, o_v, o_dt = DI, DI + CC, DI + CC + AD, DI + CC + 2 * AD, DI + CC + 3 * AD

    def half_rows(w):
        r = w.shape[0] // 2
        return lax.dynamic_slice_in_dim(w, ci * r, r, 0).astype(BF16)

    c_all, conv_w_all = _exchange8([c, conv_w[0]], False, "gather_c_conv_w")
    c_all = c_all.reshape(8, D)
    c_all = jnp.pad(c_all, ((0, 8), (0, 0)))
    nmod = w_ada.shape[2]
    b_sh = lax.dynamic_slice_in_dim(b_ada, chip * nmod, nmod, 1)
    mod_sh = _mod_fwd(c_all, w_ada[0], b_sh, "mod_fwd")
    mod_all = _exchange8([mod_sh[:8]], False, "gather_mod")[0]
    mod_me = lax.dynamic_index_in_dim(mod_all[0::2], dev, 1, keepdims=False).reshape(1, 4 * nmod)
    shift1, scale1, gate1, shift2, scale2, gate2 = [mod_me[:, i * D:(i + 1) * D] for i in range(6)]

    g_in = _exchange8([half_rows(w_in[0])], False, "gather_w_in")[0]
    rest_handle, rest_token = _spread_start([half_rows(w_out[0]), half_rows(w_ff1[0]), half_rows(w_ff2[0])], False, dev,
                                            chip, "gather_rest_start")
    shift1 = shift1 + rest_token[0, 0]
    wsh = w_in.shape[2]
    w_in_f = g_in.reshape(4, D, wsh).transpose(1, 0, 2).reshape(D, 4 * wsh)
    n_zx = DI + CC
    w_proj = jnp.concatenate([w_in_f[:, :n_zx], w_in_f[:, n_zx + NH_SSD:], w_in_f[:, n_zx:n_zx + NH_SSD],
                              jnp.zeros((D, LANES - NH_SSD), BF16)], axis=1)

    dtb, alog, dsk = _pad_lanes(dt_bias), _pad_lanes(a_log), _pad_lanes(d_skip)
    qw2 = jnp.concatenate([q_norm_w, q_norm_w], axis=1)
    kw2 = jnp.concatenate([k_norm_w, k_norm_w], axis=1)
    conv_w_f = conv_w_all[0::2].transpose(1, 0, 2).reshape(KCONV, CC)

    h1 = _rows("norm1", lambda r, k: ([_normmod(r[0], *k)], []), [(xs, 0, D)], [norm1_w, scale1, shift1],
               [(D, BF16)], [], S)[0]
    proj = _matmul(h1, w_proj, "nn", F32, "in_proj", tn=896)
    xbc = _conv_fwd(proj, o_xbc, CC, conv_w_f, conv_b, "conv_fwd")
    y_ssd, hsave = _ssd_fwd(xbc, proj, o_dt // LANES, dtb, alog, dsk, ssd_norm_w, "ssd_fwd")

    def qk_call(name, col0, w2, scale):
        def body(t_ref, w_ref, o_ref):
            o_ref[...] = _headnorm(t_ref[...], w_ref[...], scale)
        return pl.pallas_call(
            body, name=name, grid=(AD // LANES,),
            in_specs=[pl.BlockSpec((S, LANES), lambda j: (0, j + col0 // LANES)),
                      pl.BlockSpec((1, LANES), lambda j: (0, 0))],
            out_specs=pl.BlockSpec((S, LANES), lambda j: (0, j)),
            out_shape=jax.ShapeDtypeStruct((S, AD), F32), compiler_params=_cparams(("parallel",)),
        )(proj, w2)

    qn = qk_call("q_norm", o_q, qw2, HD ** -0.5)
    kn = qk_call("k_norm", o_k, kw2, 1.0)
    o_att, lse = _attn_fwd(qn, kn, proj, o_v // LANES, "attn_fwd")
    y_att = _rows("attn_out_norm", lambda r, k: ([_rmsw(r[0], k[0])], []), [(o_att, 0, AD)], [attn_norm_w],
                  [(AD, BF16)], [], S)[0]
    g_out, g_ff1, g_ff2 = _sibling_fill(_spread_wait(rest_handle, False, o_att, "gather_rest_wait"), "gather_rest_fill")
    w_out_f = g_out.reshape(MIX, D)
    w_out_a, w_out_b = w_out_f[:DI], w_out_f[DI:]
    w_ff1_f = g_ff1.reshape(4, D, DFF // 4).transpose(1, 0, 2).reshape(D, DFF)
    w_ff2_f = g_ff2.reshape(DFF, D)
    mix_a = _matmul(y_ssd, w_out_a, "nn", F32, "out_proj_a")
    mix = _matmul(y_att, w_out_b, "nn", F32, "out_proj_b", epilogue=lambda r, e: r + e, extras=(mix_a,))
    x2, h2 = _rows("resid_norm2", lambda r, k: (list(_resid_normmod(r[0], r[1], *k)), []), [(xs, 0, D), (mix, 0, D)],
                   [gate1, norm2_w, scale2, shift2], [(D, F32), (D, BF16)], [], S)
    u = _matmul(h2, w_ff1_f, "nn", F32, "ff1")
    relu2 = lambda t: jnp.square(jnp.maximum(t, 0.0))
    ff = _matmul(u, w_ff2_f, "nn", F32, "ff2", a_fn=relu2)

    def loss_fn(r, k):
        x2_, ff_, t_ = r
        err = x2_ + k[0] * ff_ - t_
        dy_ = err * (1.0 / D)
        ls = jnp.sum(jnp.sum(0.5 * err * err, axis=1, keepdims=True), axis=0, keepdims=True) * (1.0 / D)
        return [dy_, dy_ * k[0]], [ls, jnp.sum(dy_ * ff_, axis=0, keepdims=True)]

    dy, dff, loss_p, dgate2 = _rows("loss", loss_fn, [(x2, 0, D), (ff, 0, D), (tgt, 0, D)], [gate2],
                                    [(D, F32), (D, BF16)], [(1, 1), (1, D)], S)
    du = _matmul(dff, w_ff2_f, "nt", BF16, "ff2_dx", epilogue=lambda r, e: r * (2.0 * jnp.maximum(e, 0.0)), extras=(u,))
    gw_ff2 = _matmul(u, dff, "tn", F32, "ff2_dw", a_fn=relu2)
    gw_ff1 = _matmul(h2, du, "tn", F32, "ff1_dw")

    def by_half_cols(g):
        r, c4 = g.shape
        return g.reshape(2, r // 2, 4, c4 // 4).transpose(0, 2, 1, 3)

    def by_half_rows(g):
        r4, cdim = g.shape
        return g.reshape(4, 2, r4 // 8, cdim).transpose(1, 0, 2, 3)

    def scatter_start(layouts, tag):
        theirs = _pair_swap(layouts, "pair_swap_" + tag)
        sums = []
        for i, (g2, t) in enumerate(zip(layouts, theirs)):
            _, r2, cdim = t.shape
            sm = _pair_add(g2.reshape(2, 4 * r2, cdim), t.reshape(4 * r2, cdim), ci, "pair_add_%s_%d" % (tag, i))
            sums.append(sm.reshape(4, r2, cdim))
        return _spread_start(sums, True, dev, chip, "scatter_%s_start" % tag)

    def scatter_wait(handle, after, tag):
        lands = _sibling_fill(_spread_wait(handle, True, after, "scatter_%s_wait" % tag), "scatter_%s_fill" % tag)
        return [s.reshape(4, 2 * s.shape[1], s.shape[2]) for s in lands]

    ff_handle, ff_token = scatter_start([by_half_cols(gw_ff1), by_half_rows(gw_ff2)], "ff")
    dh2 = _matmul(du, w_ff1_f, "nt", F32, "ff1_dx")

    def resid_bwd(r, k):
        x_, mix_, dx2a, dh2_ = r
        _, vjp = jax.vjp(_resid_normmod, x_, mix_, *k)
        dx, dmix_, dg, dnw, dsc, dsh = vjp((dx2a, dh2_))
        return [dx, dmix_], [dg, dnw, dsc, dsh]

    dx2, dmix, dgate1, g_norm2, dscale2, dshift2 = _rows(
        "resid_norm2_bwd", resid_bwd, [(xs, 0, D), (mix, 0, D), (dy, 0, D), (dh2, 0, D)],
        [gate1 + ff_token[0, 0], norm2_w, scale2, shift2], [(D, F32), (D, BF16)], [(1, D)] * 4, S)
    gw_out = jnp.concatenate([_matmul(y_ssd, dmix, "tn", F32, "out_proj_dw_a"),
                              _matmul(y_att, dmix, "tn", F32, "out_proj_dw_b")], axis=0)
    out_handle, out_token = scatter_start([by_half_rows(gw_out)], "out")
    dy_ssd = _matmul(dmix, w_out_a, "nt", F32, "out_proj_dx_a")
    dy_att = _matmul(dmix, w_out_b, "nt", F32, "out_proj_dx_b")

    def attn_norm_bwd(r, k):
        o_, dyo = r
        _, vjp = jax.vjp(_rmsw, o_, k[0])
        do_, dw_ = vjp(dyo)
        lo = _lane_mask()
        dd_blocks = []
        for b in range(AD // LANES):
            t = (do_ * o_)[:, b * LANES:(b + 1) * LANES]
            s0 = jnp.sum(jnp.where(lo, t, 0.0), axis=1, keepdims=True)
            s1 = jnp.sum(jnp.where(lo, 0.0, t), axis=1, keepdims=True)
            dd_blocks.append(jnp.where(lo, s0, s1))
        return [do_, jnp.concatenate(dd_blocks, axis=1)], [dw_]

    do_att, dd_att, g_attn_norm = _rows("attn_norm_bwd", attn_norm_bwd, [(o_att, 0, AD), (dy_att, 0, AD)],
                                        [attn_norm_w + out_token[0, 0]], [(AD, F32), (AD, F32)], [(1, AD)], S)
    dq_n, dk_n, dv = _attn_bwd(qn, kn, proj, o_v // LANES, do_att, lse, dd_att, "attn_bwd")

    def qk_bwd_call(name, col0, w2, scale, g):
        def body(t_ref, w_ref, g_ref, o_ref, dw_ref):
            @pl.when(pl.program_id(0) == 0)
            def _():
                dw_ref[...] = jnp.zeros_like(dw_ref)
            _, vjp = jax.vjp(lambda t, w: _headnorm(t, w, scale), t_ref[...], w_ref[...])
            dt_, dw_ = vjp(g_ref[...])
            o_ref[...] = dt_.astype(BF16)
            dw_ref[...] += dw_
        blk = pl.BlockSpec((S, LANES), lambda j: (0, j))
        return pl.pallas_call(
            body, name=name, grid=(AD // LANES,),
            in_specs=[pl.BlockSpec((S, LANES), lambda j: (0, j + col0 // LANES)),
                      pl.BlockSpec((1, LANES), lambda j: (0, 0)), blk],
            out_specs=[blk, pl.BlockSpec((1, LANES), lambda j: (0, 0))],
            out_shape=[jax.ShapeDtypeStruct((S, AD), BF16), jax.ShapeDtypeStruct((1, LANES), F32)],
            compiler_params=_cparams(("arbitrary",)),
        )(proj, w2, g)

    dq, g_qw2 = qk_bwd_call("q_norm_bwd", o_q, qw2, HD ** -0.5, dq_n)
    dk, g_kw2 = qk_bwd_call("k_norm_bwd", o_k, kw2, 1.0, dk_n)
    g_q_norm = g_qw2[:, :HD] + g_qw2[:, HD:]
    g_k_norm = g_kw2[:, :HD] + g_kw2[:, HD:]

    dxbc, dz, ddtr, g_dtb, g_alog, g_dsk, g_ssd_norm = _ssd_bwd(
        xbc, proj, o_dt // LANES, dtb, alog, dsk, ssd_norm_w, hsave, dy_ssd, "ssd_bwd")
    dxbc_pre, g_conv_w, g_conv_b = _conv_bwd(proj, o_xbc, CC, conv_w_f, conv_b, dxbc, "conv_bwd")
    dproj = jnp.concatenate([dz.astype(BF16), dxbc_pre.astype(BF16), dq, dk, dv.astype(BF16), ddtr.astype(BF16)], axis=1)
    dh1 = _matmul(dproj, w_proj, "nt", F32, "in_proj_dx", tk=896)

    def norm1_bwd(r, k):
        x_, dh_, dres = r
        _, vjp = jax.vjp(_normmod, x_, *k)
        dx, dnw, dsc, dsh = vjp(dh_)
        return [dx + dres], [dnw, dsc, dsh]

    grad_x, g_norm1, dscale1, dshift1 = _rows("norm1_bwd", norm1_bwd, [(xs, 0, D), (dh1, 0, D), (dx2, 0, D)],
                                              [norm1_w, scale1, shift1], [(D, F32)], [(1, D)] * 3, S)
    dmod =jnp.concatenate([dshift1, dscale1, dgate1, dshift2, dscale2, dgate2], axis=1)

    small = [g_norm1, g_norm2, dmod, g_conv_b, g_dtb, g_alog, g_dsk, g_ssd_norm, _pad_lanes(g_q_norm),
             _pad_lanes(g_k_norm), g_attn_norm, g_conv_w.reshape(1, KCONV * CC)]
    sizes = [t.shape[1] for t in small]
    packed = jnp.concatenate(small, axis=1)
    nrow = -(-packed.shape[1] // LANES // 8) * 8
    packed = jnp.pad(packed, ((0, 0), (0, nrow * LANES - packed.shape[1]))).reshape(nrow, LANES)
    packed_all = _exchange8([packed], False, "gather_small_grads")[0]
    tot = _sum_parts(packed_all, "sum_small_grads").reshape(1, nrow * LANES)
    offs = [sum(sizes[:i]) for i in range(len(sizes))]
    (g_norm1, g_norm2, g_b_ada, g_conv_b, g_dtb, g_alog, g_dsk, g_ssd_norm, g_q_norm, g_k_norm, g_attn_norm,
     g_conv_w) = [tot[:, o:o + n] for o, n in zip(offs, sizes)]
    g_dtb, g_alog, g_dsk = g_dtb[:, :NH_SSD], g_alog[:, :NH_SSD], g_dsk[:, :NH_SSD]
    g_q_norm, g_k_norm = g_q_norm[:, :HD], g_k_norm[:, :HD]
    ccs = CC // 4
    g_conv_w = lax.dynamic_slice_in_dim(g_conv_w.reshape(KCONV, CC), chip * ccs, ccs, 1)

    dmod_all = packed_all.reshape(8, nrow * LANES)[:, offs[2]:offs[2] + 6 * D]
    dmod_sh = jnp.pad(lax.dynamic_slice_in_dim(dmod_all, chip * nmod, nmod, 1), ((0, 8), (0, 0)))
    gw_ada = _mod_wgrad(c_all, dmod_sh, "mod_wgrad")

    gw_proj = _matmul(h1, dproj, "tn", F32, "in_proj_dw", tn=896)
    gw_in = jnp.concatenate([gw_proj[:, :n_zx], gw_proj[:, o_dt:o_dt + NH_SSD], gw_proj[:, n_zx:o_dt]], axis=1)
    in_handle, in_token = scatter_start([by_half_cols(gw_in)], "in")

    parts_ff1, parts_ff2 = scatter_wait(ff_handle, in_token, "ff")
    res_ff1 = _adamw(parts_ff1, w_ff1[0], m_w_ff1[0], v_w_ff1[0], "adamw_w_ff1")
    res_ff2 = _adamw(parts_ff2, w_ff2[0], m_w_ff2[0], v_w_ff2[0], "adamw_w_ff2")
    res_out = _adamw(scatter_wait(out_handle, in_token, "out")[0], w_out[0], m_w_out[0], v_w_out[0], "adamw_w_out")
    res_ada = _adamw(gw_ada[None], w_ada[0], m_w_ada[0], v_w_ada[0], "adamw_w_ada")
    res_in = _adamw(scatter_wait(in_handle, res_ada[0], "in")[0], w_in[0], m_w_in[0], v_w_in[0], "adamw_w_in")

    small_names = ["norm1_w", "norm2_w", "b_ada", "conv_w", "conv_b", "dt_bias", "a_log", "d_skip", "ssd_norm_w",
                   "q_norm_w", "k_norm_w", "attn_norm_w"]
    small_g = dict(norm1_w=g_norm1, norm2_w=g_norm2, b_ada=g_b_ada, conv_w=g_conv_w.reshape(1, KCONV * ccs),
                   conv_b=g_conv_b, dt_bias=g_dtb, a_log=g_alog, d_skip=g_dsk, ssd_norm_w=g_ssd_norm, q_norm_w=g_q_norm,
                   k_norm_w=g_k_norm, attn_norm_w=g_attn_norm)
    small_w = dict(norm1_w=(norm1_w, m_norm1_w, v_norm1_w), norm2_w=(norm2_w, m_norm2_w, v_norm2_w),
                   b_ada=(b_ada, m_b_ada, v_b_ada),
                   conv_w=tuple(t.reshape(1, KCONV * ccs) for t in (conv_w, m_conv_w, v_conv_w)),
                   conv_b=(conv_b, m_conv_b, v_conv_b), dt_bias=(dt_bias, m_dt_bias, v_dt_bias),
                   a_log=(a_log, m_a_log, v_a_log), d_skip=(d_skip, m_d_skip, v_d_skip),
                   ssd_norm_w=(ssd_norm_w, m_ssd_norm_w, v_ssd_norm_w), q_norm_w=(q_norm_w, m_q_norm_w, v_q_norm_w),
                   k_norm_w=(k_norm_w, m_k_norm_w, v_k_norm_w), attn_norm_w=(attn_norm_w, m_attn_norm_w, v_attn_norm_w))
    ssz = [_pad_lanes(small_g[n]).shape[1] for n in small_names]
    soff = [sum(ssz[:i]) for i in range(len(ssz))]
    srow = -(-sum(ssz) // LANES // 8) * 8

    def pack(ts, fill):
        t = jnp.concatenate([jnp.pad(t, ((0, 0), (0, (-t.shape[1]) % LANES)), constant_values=fill) for t in ts], axis=1)
        return jnp.pad(t, ((0, 0), (0, srow * LANES - t.shape[1])), constant_values=fill).reshape(srow, LANES)

    sg = pack([small_g[n] for n in small_names], 0.0)
    sw = pack([small_w[n][0] for n in small_names], 0.0)
    sm_ = pack([small_w[n][1] for n in small_names], 0.0)
    sv = pack([small_w[n][2] for n in small_names], 1.0)
    _, s_delta, s_m, s_v = _adamw(sg[None], sw, sm_, sv, "adamw_small", tm=srow)

    def unpack(t, n):
        i = small_names.index(n)
        return t.reshape(1, srow * LANES)[:, soff[i]:soff[i] + small_g[n].shape[1]].reshape(small_w[n][0].shape)

    loss = lax.psum(loss_p[0, 0], ("x", "y", "c"))
    big_res = dict(w_ada=res_ada, w_in=res_in, w_out=res_out, w_ff1=res_ff1, w_ff2=res_ff2)
    order = ["norm1_w", "norm2_w", "w_ada", "b_ada", "w_in", "conv_w", "conv_b", "dt_bias", "a_log", "d_skip",
             "ssd_norm_w", "q_norm_w", "k_norm_w", "attn_norm_w", "w_out", "w_ff1", "w_ff2"]
    grads, deltas, new_m, new_v = [], [], [], []
    for n in order:
        if n in big_res:
            g_, d_, m_, v_ = [t[None] for t in big_res[n]]
        else:
            g_ = small_g[n].reshape(small_w[n][0].shape)
            d_, m_, v_ = unpack(s_delta, n), unpack(s_m, n), unpack(s_v, n)
            if n == "conv_w":
                g_, d_, m_, v_ = [t.reshape(conv_w.shape) for t in (g_, d_, m_, v_)]
        grads.append(g_)
        deltas.append(d_)
        new_m.append(m_)
        new_v.append(v_)
    return (loss, grad_x[None], *grads, *deltas, *new_m, *new_v)
```

```python
import functools

import jax
import jax.numpy as jnp
from jax import lax
from jax.experimental import pallas as pl
from jax.experimental.pallas import tpu as pltpu

F32, BF16 = jnp.float32, jnp.bfloat16
EPS = 1e-6
HD = 64
NH_SSD = 16
NG = 4
NSTATE = 128
KCONV = 4
CHUNK = 128
NH_ATT = 16
PATTERNS = ((128, 1), (512, 4), (2048, 16))
ABLK = 128
QTILE = 128
LANES = 128
ADAM_LR, ADAM_B1, ADAM_B2, ADAM_EPS, ADAM_WD, ADAM_STEP = 0.001, 0.9, 0.999, 1e-08, 0.01, 10
VMEM_LIMIT = 56 * 1024 * 1024
MESH = pl.DeviceIdType.MESH
NEG = -1e30

_DN = {"nn": (((1,), (0,)), ((), ())), "nt": (((1,), (1,)), ((), ())), "tn": (((0,), (0,)), ((), ()))}


def _cparams(sem):
    return pltpu.CompilerParams(dimension_semantics=sem, vmem_limit_bytes=VMEM_LIMIT)


def _tile(n, cap):
    if n % LANES or n <= LANES:
        return n
    best = LANES
    for t in range(LANES, min(n, cap) + 1, LANES):
        if n % t == 0:
            best = t
    return best


def _silu(x):
    return x / (1.0 + jnp.exp(-x))


def _softplus(x):
    return jnp.maximum(x, 0.0) + jnp.log(1.0 + jnp.exp(-jnp.abs(x)))


def _dot(a, b, dims):
    return lax.dot_general(a.astype(BF16), b.astype(BF16), _DN[dims], preferred_element_type=F32)


def _matmul(a, b, dims, out_dtype, name, a_fn=None, epilogue=None, extras=(), tm=1024, tn=1024, tk=1024):
    if dims == "nn":
        (M, K), (_, N) = a.shape, b.shape
    elif dims == "nt":
        (M, K), (N, _) = a.shape, b.shape
    else:
        (K, M), (_, N) = a.shape, b.shape
    tm, tn, tk = _tile(M, tm), _tile(N, tn), _tile(K, tk)
    nk = K // tk
    ne = len(extras)

    def body(a_ref, b_ref, *rest):
        e_refs, o_ref = rest[:ne], rest[ne]
        av = a_ref[...]
        if a_fn is not None:
            av = a_fn(av)
        part = _dot(av, b_ref[...], dims)

        def finish(r):
            if epilogue is not None:
                r = epilogue(r, *[e[...] for e in e_refs])
            o_ref[...] = r.astype(out_dtype)

        if nk == 1:
            finish(part)
            return
        acc = rest[ne + 1]
        k = pl.program_id(2)

        @pl.when(k == 0)
        def _():
            acc[...] = part

        @pl.when(k > 0)
        def _():
            acc[...] += part

        @pl.when(k == nk - 1)
        def _():
            finish(acc[...])

    a_spec = pl.BlockSpec((tk, tm), lambda i, j, k: (k, i)) if dims == "tn" else pl.BlockSpec((tm, tk), lambda i, j, k: (i, k))
    b_spec = pl.BlockSpec((tn, tk), lambda i, j, k: (j, k)) if dims == "nt" else pl.BlockSpec((tk, tn), lambda i, j, k: (k, j))
    o_spec = pl.BlockSpec((tm, tn), lambda i, j, k: (i, j))
    return pl.pallas_call(
        body, name=name, grid=(M // tm, N // tn, nk),
        in_specs=[a_spec, b_spec] + [o_spec] * ne, out_specs=o_spec,
        out_shape=jax.ShapeDtypeStruct((M, N), out_dtype),
        scratch_shapes=[pltpu.VMEM((tm, tn), F32)] if nk > 1 else [],
        compiler_params=_cparams(("parallel", "parallel", "arbitrary")),
    )(a, b, *extras)


def _rows(name, fn, rows, consts, outs, accs, n_rows, tm=256):
    tm = min(tm, n_rows)
    nr, nc, no, na = len(rows), len(consts), len(outs), len(accs)

    def body(*refs):
        r_refs, c_refs = refs[:nr], refs[nr:nr + nc]
        o_refs, a_refs = refs[nr + nc:nr + nc + no], refs[nr + nc + no:]
        o_vals, a_vals = fn([r[...] for r in r_refs], [c[...] for c in c_refs])
        for ref, val in zip(o_refs, o_vals):
            ref[...] = val.astype(ref.dtype)
        if na:
            @pl.when(pl.program_id(0) == 0)
            def _():
                for ref in a_refs:
                    ref[...] = jnp.zeros_like(ref)
            for ref, val in zip(a_refs, a_vals):
                ref[...] += val

    in_specs = [pl.BlockSpec((tm, w), lambda i, cb=cb: (i, cb)) for (_, cb, w) in rows]
    in_specs += [pl.BlockSpec(cst.shape, lambda i, nd=cst.ndim: (0,) * nd) for cst in consts]
    out_specs = [pl.BlockSpec((tm, w), lambda i: (i, 0)) for (w, _) in outs]
    out_specs += [pl.BlockSpec(s, lambda i: (0, 0)) for s in accs]
    out_shape = [jax.ShapeDtypeStruct((n_rows, w), dt) for (w, dt) in outs]
    out_shape += [jax.ShapeDtypeStruct(s, F32) for s in accs]
    res = pl.pallas_call(
        body, name=name, grid=(n_rows // tm,), in_specs=in_specs, out_specs=out_specs, out_shape=out_shape,
        compiler_params=_cparams(("arbitrary",)),
    )(*[r[0] for r in rows], *consts)
    return res


def _normmod(x, nw, sc, sh):
    r = lax.rsqrt(jnp.mean(x * x, axis=-1, keepdims=True) + EPS)
    return (x * r) * nw * (1.0 + sc) + sh


def _resid_normmod(x, mix, g, nw, sc, sh):
    x2 = x + g * mix
    return x2, _normmod(x2, nw, sc, sh)


def _rmsw(o, w):
    return o * lax.rsqrt(jnp.mean(o * o, axis=-1, keepdims=True) + EPS) * w


def _lane_mask():
    return lax.broadcasted_iota(jnp.int32, (1, LANES), 1) < HD


def _headnorm(t, w, scale):
    lo = _lane_mask()
    t2 = t * t
    s0 = jnp.sum(jnp.where(lo, t2, 0.0), axis=1, keepdims=True)
    s1 = jnp.sum(jnp.where(lo, 0.0, t2), axis=1, keepdims=True)
    ms = jnp.where(lo, s0, s1) * (1.0 / HD)
    return t * lax.rsqrt(ms + EPS) * w * scale


CONV_ROWS = 128
CONV_HALO = 8


def _conv_cols(n_ch):
    return _tile(n_ch, LANES)


def _conv_fwd(proj, col0, n_ch, conv_w, conv_b, name):
    S = proj.shape[0]
    tc = _conv_cols(n_ch)

    R, H = CONV_ROWS, CONV_HALO

    def body(u_ref, w_ref, b_ref, o_ref):
        w = [w_ref[i:i + 1, :] for i in range(KCONV)]
        b = b_ref[...]

        def chunk(ext):
            acc = b + w[KCONV - 1] * ext[H:]
            for i in range(KCONV - 1):
                acc = acc + w[i] * pltpu.roll(ext, KCONV - 1 - i, 0)[H:]
            return _silu(acc)

        o_ref[0:R, :] = chunk(jnp.concatenate([jnp.zeros((H, tc), F32), u_ref[0:R, :]], axis=0))

        def step(c, carry):
            r0 = pl.multiple_of(c * R, R)
            o_ref[pl.ds(r0, R), :] = chunk(u_ref[pl.ds(pl.multiple_of(r0 - H, H), R + H), :])
            return carry

        lax.fori_loop(1, S // R, step, 0)

    return pl.pallas_call(
        body, name=name, grid=(n_ch // tc,),
        in_specs=[pl.BlockSpec((S, tc), lambda j: (0, j + col0 // tc)),
                  pl.BlockSpec((KCONV, tc), lambda j: (0, j)), pl.BlockSpec((1, tc), lambda j: (0, j))],
        out_specs=pl.BlockSpec((S, tc), lambda j: (0, j)),
        out_shape=jax.ShapeDtypeStruct((S, n_ch), F32),
        compiler_params=_cparams(("parallel",)),
    )(proj, conv_w, conv_b)


def _conv_bwd(proj, col0, n_ch, conv_w, conv_b, dxbc, name):
    S = proj.shape[0]
    tc = _conv_cols(n_ch)

    R, H = CONV_ROWS, CONV_HALO

    def body(u_ref, w_ref, b_ref, g_ref, du_ref, dw_ref, db_ref):
        w = [w_ref[i:i + 1, :] for i in range(KCONV)]
        b = b_ref[...]
        pad = jnp.zeros((H, tc), F32)

        def chunk(u_ext, g_ext):
            taps = [pltpu.roll(u_ext, KCONV - 1 - i, 0)[H:] for i in range(KCONV - 1)] + [u_ext[H:]]
            acc = b
            for i in range(KCONV):
                acc = acc + w[i] * taps[i]
            sig = 1.0 / (1.0 + jnp.exp(-acc))
            dacc = g_ext * (sig * (1.0 + acc * (1.0 - sig)))
            du = w[KCONV - 1] * dacc[:R]
            for i in range(KCONV - 1):
                du = du + w[i] * pltpu.roll(dacc, R + H - (KCONV - 1 - i), 0)[:R]
            d = dacc[:R]
            return du, [jnp.sum(d * t[:R], axis=0, keepdims=True) for t in taps], jnp.sum(d, axis=0, keepdims=True)

        du, dws, db = chunk(jnp.concatenate([pad, u_ref[0:R + H, :]], axis=0), g_ref[0:R + H, :])
        du_ref[0:R, :] = du

        def step(c, carry):
            r0 = pl.multiple_of(c * R, R)
            du_c, dws_c, db_c = chunk(u_ref[pl.ds(pl.multiple_of(r0 - H, H), R + 2 * H), :], g_ref[pl.ds(r0, R + H), :])
            du_ref[pl.ds(r0, R), :] = du_c
            return [a + b_ for a, b_ in zip(carry[0], dws_c)], carry[1] + db_c

        dws, db = lax.fori_loop(1, S // R - 1, step, (dws, db))
        du, dws_l, db_l = chunk(jnp.concatenate([u_ref[S - R - H:S, :], pad], axis=0),
                                jnp.concatenate([g_ref[S - R:S, :], pad], axis=0))
        du_ref[S - R:S, :] = du
        for i in range(KCONV):
            dw_ref[i:i + 1, :] = dws[i] + dws_l[i]
        db_ref[...] = db + db_l

    return pl.pallas_call(
        body, name=name, grid=(n_ch // tc,),
        in_specs=[pl.BlockSpec((S, tc), lambda j: (0, j + col0 // tc)),
                  pl.BlockSpec((KCONV, tc), lambda j: (0, j)), pl.BlockSpec((1, tc), lambda j: (0, j)),
                  pl.BlockSpec((S, tc), lambda j: (0, j))],
        out_specs=[pl.BlockSpec((S, tc), lambda j: (0, j)), pl.BlockSpec((KCONV, tc), lambda j: (0, j)),
                   pl.BlockSpec((1, tc), lambda j: (0, j))],
        out_shape=[jax.ShapeDtypeStruct((S, n_ch), F32), jax.ShapeDtypeStruct((KCONV, n_ch), F32),
                   jax.ShapeDtypeStruct((1, n_ch), F32)],
        compiler_params=_cparams(("parallel",)),
    )(proj, conv_w, conv_b, dxbc)


@functools.partial(jax.custom_vjp, nondiff_argnums=(2,))
def _mm(a, b, dims):
    return _dot(a, b, dims)


def _mm_fwd(a, b, dims):
    return _dot(a, b, dims), (a, b)


def _mm_bwd(dims, res, g):
    a, b = res
    if dims == "nn":
        return _dot(g, b, "nt"), _dot(a, g, "tn")
    if dims == "nt":
        return _dot(g, b, "nn"), _dot(g, a, "tn")
    return _dot(b, g, "nt"), _dot(a, g, "nn")


_mm.defvjp(_mm_fwd, _mm_bwd)


def _tri_dot(x, upper):
    n = x.shape[0]
    r = lax.broadcasted_iota(jnp.int32, (n, n), 0)
    c = lax.broadcasted_iota(jnp.int32, (n, n), 1)
    t = jnp.where((r <= c) if upper else (r >= c), 1.0, 0.0)
    return lax.dot_general(t, x, _DN["nn"], precision=lax.Precision.HIGHEST, preferred_element_type=F32)


@jax.custom_vjp
def _cumsum_rows(x):
    return _tri_dot(x, False)


_cumsum_rows.defvjp(lambda x: (_tri_dot(x, False), None), lambda _, g: (_tri_dot(g, True),))


def _ssd_chunk(xs_p, bm_g, cm_g, dtr, z_p, dtb, alog, dsk, nw_p, h_p):
    L = dtr.shape[0]
    n_pairs = len(xs_p)
    ppg = n_pairs // len(bm_g)
    lane = lax.broadcasted_iota(jnp.int32, (1, LANES), 1)
    sub = lax.broadcasted_iota(jnp.int32, (LANES, 1), 0)
    lo = lane < HD
    row_l = lax.broadcasted_iota(jnp.int32, (L, 1), 0)
    tri = lax.broadcasted_iota(jnp.int32, (L, L), 0) >= lax.broadcasted_iota(jnp.int32, (L, L), 1)

    dt = _softplus(dtr + dtb)
    acs = _cumsum_rows(dt * (-jnp.exp(alog)))
    acs_t = acs.T
    a_last = jnp.sum(jnp.where(row_l == L - 1, acs, 0.0), axis=0, keepdims=True)
    e_acs = jnp.exp(acs)
    dec = jnp.exp(a_last - acs)
    cdec = jnp.exp(a_last)

    def colv(m, h):
        return jnp.sum(jnp.where(lane == h, m, 0.0), axis=1, keepdims=True)

    def rowv(mt, h):
        return jnp.sum(jnp.where(sub == h, mt, 0.0), axis=0, keepdims=True)

    def pair(m, h0):
        return jnp.where(lo, colv(m, h0), colv(m, h0 + 1))

    ys, hs = [], []
    cb = None
    for p in range(n_pairs):
        g, h0 = p // ppg, 2 * p
        bmat, cmat = bm_g[g], cm_g[g]
        if p % ppg == 0:
            cb = _mm(cmat, bmat, "nt")
        x = xs_p[p]
        xdt = x * pair(dt, h0)
        yd = []
        for h in (h0, h0 + 1):
            seg = colv(acs, h) - rowv(acs_t, h)
            lm = jnp.where(tri, jnp.exp(jnp.where(tri, seg, 0.0)), 0.0)
            yd.append(_mm(cb * lm, xdt, "nn"))
        y = jnp.where(lo, yd[0], yd[1])
        y = y + _mm(cmat, h_p[p], "nt") * pair(e_acs, h0)
        st = _mm(xdt * pair(dec, h0), bmat, "tn")
        cd_col = jnp.where(sub < HD, colv(cdec, h0), colv(cdec, h0 + 1))
        hs.append(h_p[p] * cd_col + st)
        ys.append(y + pair(dsk, h0) * x)

    y2 = [ys[p] * _silu(z_p[p]) for p in range(n_pairs)]
    outs = []
    for g in range(len(bm_g)):
        ps = range(g * ppg, (g + 1) * ppg)
        ss = sum(jnp.sum(y2[p] * y2[p], axis=1, keepdims=True) for p in ps)
        rs = lax.rsqrt(ss * (1.0 / (ppg * LANES)) + EPS)
        outs += [y2[p] * rs * nw_p[p] for p in ps]
    return outs, hs


def _ssd_slices(xbc_ref, z_ref, nw_ref, di):
    n_pairs = di // LANES
    xs_p = [xbc_ref[:, p * LANES:(p + 1) * LANES] for p in range(n_pairs)]
    bm_g = [xbc_ref[:, di + g * NSTATE:di + (g + 1) * NSTATE] for g in range(NG)]
    cm_g = [xbc_ref[:, di + (NG + g) * NSTATE:di + (NG + g + 1) * NSTATE] for g in range(NG)]
    z_p = [z_ref[:, p * LANES:(p + 1) * LANES] for p in range(n_pairs)]
    nw_p = [nw_ref[:, p * LANES:(p + 1) * LANES] for p in range(n_pairs)]
    return xs_p, bm_g, cm_g, z_p, nw_p


def _ssd_fwd(xbc, proj, dt_cb, dtb, alog, dsk, nw, name):
    S, cc = xbc.shape
    di = NH_SSD * HD
    n_pairs = di // LANES
    nchunk = S // CHUNK

    def body(xbc_ref, z_ref, dtr_ref, dtb_ref, alog_ref, dsk_ref, nw_ref, y_ref, hs_ref, h_scr):
        @pl.when(pl.program_id(0) == 0)
        def _():
            h_scr[...] = jnp.zeros_like(h_scr)

        xs_p, bm_g, cm_g, z_p, nw_p = _ssd_slices(xbc_ref, z_ref, nw_ref, di)
        h_p = [h_scr[p * LANES:(p + 1) * LANES, :] for p in range(n_pairs)]
        hs_ref[...] = h_scr[...]
        outs, hs = _ssd_chunk(xs_p, bm_g, cm_g, dtr_ref[...], z_p, dtb_ref[...], alog_ref[...], dsk_ref[...], nw_p, h_p)
        for p in range(n_pairs):
            y_ref[:, p * LANES:(p + 1) * LANES] = outs[p].astype(y_ref.dtype)
            h_scr[p * LANES:(p + 1) * LANES, :] = hs[p]

    vec = pl.BlockSpec((1, LANES), lambda c: (0, 0))
    return pl.pallas_call(
        body, name=name, grid=(nchunk,),
        in_specs=[pl.BlockSpec((CHUNK, cc), lambda c: (c, 0)), pl.BlockSpec((CHUNK, di), lambda c: (c, 0)),
                  pl.BlockSpec((CHUNK, LANES), lambda c: (c, dt_cb)), vec, vec, vec,
                  pl.BlockSpec((1, di), lambda c: (0, 0))],
        out_specs=[pl.BlockSpec((CHUNK, di), lambda c: (c, 0)), pl.BlockSpec((None, di, NSTATE), lambda c: (c, 0, 0))],
        out_shape=[jax.ShapeDtypeStruct((S, di), BF16), jax.ShapeDtypeStruct((nchunk, di, NSTATE), F32)],
        scratch_shapes=[pltpu.VMEM((di, NSTATE), F32)],
        compiler_params=_cparams(("arbitrary",)),
    )(xbc, proj, proj, dtb, alog, dsk, nw)


def _ssd_bwd(xbc, proj, dt_cb, dtb, alog, dsk, nw, hsave, dy, name):
    S, cc = xbc.shape
    di = NH_SSD * HD
    n_pairs = di // LANES
    nchunk = S // CHUNK

    def body(xbc_ref, z_ref, dtr_ref, dtb_ref, alog_ref, dsk_ref, nw_ref, hs_ref, dy_ref,
             dxbc_ref, dz_ref, ddtr_ref, ddtb_ref, dalog_ref, ddsk_ref, dnw_ref, dh_scr):
        @pl.when(pl.program_id(0) == 0)
        def _():
            dh_scr[...] = jnp.zeros_like(dh_scr)
            ddtb_ref[...] = jnp.zeros_like(ddtb_ref)
            dalog_ref[...] = jnp.zeros_like(dalog_ref)
            ddsk_ref[...] = jnp.zeros_like(ddsk_ref)
            dnw_ref[...] = jnp.zeros_like(dnw_ref)

        xs_p, bm_g, cm_g, z_p, nw_p = _ssd_slices(xbc_ref, z_ref, nw_ref, di)
        h_p = [hs_ref[p * LANES:(p + 1) * LANES, :] for p in range(n_pairs)]
        dy_p = [dy_ref[:, p * LANES:(p + 1) * LANES].astype(F32) for p in range(n_pairs)]
        dh_p = [dh_scr[p * LANES:(p + 1) * LANES, :] for p in range(n_pairs)]
        _, vjp = jax.vjp(_ssd_chunk, xs_p, bm_g, cm_g, dtr_ref[...], z_p, dtb_ref[...], alog_ref[...], dsk_ref[...],
                         nw_p, h_p)
        dxs, dbm, dcm, ddtr, dz, ddtb, dalog, ddsk, dnw, dh = vjp((dy_p, dh_p))
        for p in range(n_pairs):
            sl = slice(p * LANES, (p + 1) * LANES)
            dxbc_ref[:, sl] = dxs[p]
            dz_ref[:, sl] = dz[p]
            dnw_ref[:, sl] += dnw[p]
            dh_scr[sl, :] = dh[p]
        for g in range(NG):
            dxbc_ref[:, di + g * NSTATE:di + (g + 1) * NSTATE] = dbm[g]
            dxbc_ref[:, di + (NG + g) * NSTATE:di + (NG + g + 1) * NSTATE] = dcm[g]
        ddtr_ref[...] = ddtr
        ddtb_ref[...] += ddtb
        dalog_ref[...] += dalog
        ddsk_ref[...] += ddsk

    last = nchunk - 1
    vec = pl.BlockSpec((1, LANES), lambda c: (0, 0))
    return pl.pallas_call(
        body, name=name, grid=(nchunk,),
        in_specs=[pl.BlockSpec((CHUNK, cc), lambda c: (last - c, 0)), pl.BlockSpec((CHUNK, di), lambda c: (last - c, 0)),
                  pl.BlockSpec((CHUNK, LANES), lambda c: (last - c, dt_cb)), vec, vec, vec,
                  pl.BlockSpec((1, di), lambda c: (0, 0)),
                  pl.BlockSpec((None, di, NSTATE), lambda c: (last - c, 0, 0)),
                  pl.BlockSpec((CHUNK, di), lambda c: (last - c, 0))],
        out_specs=[pl.BlockSpec((CHUNK, cc), lambda c: (last - c, 0)), pl.BlockSpec((CHUNK, di), lambda c: (last - c, 0)),
                   pl.BlockSpec((CHUNK, LANES), lambda c: (last - c, 0)), vec, vec, vec,
                   pl.BlockSpec((1, di), lambda c: (0, 0))],
        out_shape=[jax.ShapeDtypeStruct((S, cc), F32), jax.ShapeDtypeStruct((S, di), F32),
                   jax.ShapeDtypeStruct((S, LANES), F32), jax.ShapeDtypeStruct((1, LANES), F32),
                   jax.ShapeDtypeStruct((1, LANES), F32), jax.ShapeDtypeStruct((1, LANES), F32),
                   jax.ShapeDtypeStruct((1, di), F32)],
        scratch_shapes=[pltpu.VMEM((di, NSTATE), F32)],
        compiler_params=_cparams(("arbitrary",)),
    )(xbc, proj, proj, dtb, alog, dsk, nw, hsave, dy)


def _band_masks(rows_q, rows_k):
    qi = lax.broadcasted_iota(jnp.int32, (rows_q, rows_k), 0)
    ki = lax.broadcasted_iota(jnp.int32, (rows_q, rows_k), 1)
    return qi, ki


def _class_chunks(n_rows, d):
    per_class = n_rows // d
    ch = min(per_class, 256)
    out = []
    for r in range(d):
        for c0 in range(0, per_class, ch):
            tok = pl.ds(c0, ch) if d == 1 else pl.ds(r + d * c0, ch, stride=d)
            out.append((tok, pl.ds(r * per_class + c0, ch)))
    return out


def _to_class_order(src_ref, dst_ref, n_rows, d):
    for tok, cls in _class_chunks(n_rows, d):
        dst_ref[cls, :] = src_ref[tok, :].astype(dst_ref.dtype)


def _blk_rows(t):
    return pl.ds(pl.multiple_of(t * ABLK, ABLK), ABLK)


def _head_lanes(msk, t, t_rolled):
    return jnp.where(msk, t, t_rolled)


def _zero_unless(msk, t):
    return jnp.where(msk, t, jnp.zeros_like(t))


def _attn_fwd(qn, kn, proj, v_cb, name):
    S, ad = qn.shape
    nb = S // ABLK
    nbr = len(PATTERNS)

    def body(q_ref, k_ref, v_ref, o_ref, lse_ref, qc, kc, vc, ob, mb, lb, m_s, l_s):
        lo = _lane_mask()
        qi, ki = _band_masks(ABLK, 2 * ABLK)
        band, in_cur, prev_ok = ki <= qi + ABLK, ki >= ABLK, ki >= qi
        for bi, (_, d) in enumerate(PATTERNS):
            nbc = S // d // ABLK
            first, last = bi == 0, bi == nbr - 1
            qs, ks, vs = q_ref, k_ref, v_ref
            if d > 1:
                qs, ks, vs = qc, kc, vc
                for src, dst in ((q_ref, qc), (k_ref, kc), (v_ref, vc)):
                    _to_class_order(src, dst, S, d)
            o_dst, m_dst, l_dst = (o_ref, m_s, l_s) if first else (ob, mb, lb)

            def blk(t, carry, nbc=nbc, qs=qs, ks=ks, vs=vs, o_dst=o_dst, m_dst=m_dst, l_dst=l_dst):
                rows, prow = _blk_rows(t), _blk_rows(jnp.maximum(t - 1, 0))
                has_prev = (t % nbc) != 0
                kk = jnp.concatenate([ks[prow, :], ks[rows, :]], axis=0)
                vv = jnp.concatenate([vs[prow, :], vs[rows, :]], axis=0)
                for u in range(ABLK // QTILE):
                    sub = pl.ds(pl.multiple_of(t * ABLK + u * QTILE, QTILE), QTILE)
                    sl = slice(u * QTILE, (u + 1) * QTILE)
                    valid = band[sl] & (in_cur[sl] | (prev_ok[sl] & has_prev))
                    qv = qs[sub, :]
                    os_, ms_, ls_ = [], [], []
                    for msk in (lo, jnp.logical_not(lo)):
                        s = jnp.where(valid, _dot(_zero_unless(msk, qv), kk, "nt"), NEG)
                        m = jnp.max(s, axis=1, keepdims=True)
                        p = jnp.exp(s - m)
                        os_.append(_dot(p, vv, "nn"))
                        ms_.append(m)
                        ls_.append(jnp.sum(p, axis=1, keepdims=True))
                    o_dst[sub, :] = jnp.where(lo, os_[0], os_[1])
                    m_dst[sub, :] = jnp.where(lo, ms_[0], ms_[1])
                    l_dst[sub, :] = jnp.where(lo, ls_[0], ls_[1])
                return carry

            lax.fori_loop(0, nb, blk, 0, unroll=8)
            if first:
                continue
            for tok, cls in _class_chunks(S, d):
                m_old, m_b = m_s[tok, :], mb[cls, :]
                m_new = jnp.maximum(m_old, m_b)
                a, b = jnp.exp(m_old - m_new), jnp.exp(m_b - m_new)
                l_new = a * l_s[tok, :] + b * lb[cls, :]
                o_new = a * o_ref[tok, :] + b * ob[cls, :]
                if last:
                    o_ref[tok, :] = o_new / l_new
                    lse_ref[tok, :] = m_new + jnp.log(l_new)
                else:
                    o_ref[tok, :] = o_new
                    m_s[tok, :] = m_new
                    l_s[tok, :] = l_new

    col = pl.BlockSpec((S, LANES), lambda h: (0, h))
    return pl.pallas_call(
        body, name=name, grid=(ad // LANES,),
        in_specs=[col, col, pl.BlockSpec((S, LANES), lambda h: (0, h + v_cb))], out_specs=[col, col],
        out_shape=[jax.ShapeDtypeStruct((S, ad), F32), jax.ShapeDtypeStruct((S, ad), F32)],
        scratch_shapes=[pltpu.VMEM((S, LANES), BF16)] * 3 + [pltpu.VMEM((S, LANES), F32)] * 5,
        compiler_params=_cparams(("parallel",)),
    )(qn, kn, proj)


def _attn_bwd(qn, kn, proj, v_cb, do, lse, dd, name):
    S, ad = qn.shape
    nb = S // ABLK

    def body(q_ref, k_ref, v_ref, do_ref, lse_ref, dd_ref, dq_ref, dk_ref, dv_ref,
             qc, kc, vc, doc, lsec, ddc, dqc, dkc, dvc):
        lo = _lane_mask()
        qi, ki = _band_masks(ABLK, ABLK)
        cur_ok, prev_ok = ki <= qi, ki >= qi
        for bi, (_, d) in enumerate(PATTERNS):
            nbc = S // d // ABLK
            first = bi == 0
            token_order = (q_ref, k_ref, v_ref, do_ref, lse_ref, dd_ref)
            class_order = (qc, kc, vc, doc, lsec, ddc)
            if d > 1:
                for src, dst in zip(token_order, class_order):
                    _to_class_order(src, dst, S, d)
            qs, ks, vs, dos, lses, dds = class_order if d > 1 else token_order
            dq_dst, dk_dst, dv_dst = (dq_ref, dk_ref, dv_ref) if first else (dqc, dkc, dvc)
            dk_dst[...] = jnp.zeros_like(dk_dst)
            dv_dst[...] = jnp.zeros_like(dv_dst)

            def blk(t, carry, nbc=nbc, qs=qs, ks=ks, vs=vs, dos=dos, lses=lses, dds=dds,
                    dq_dst=dq_dst, dk_dst=dk_dst, dv_dst=dv_dst):
                rows, prow = _blk_rows(t), _blk_rows(jnp.maximum(t - 1, 0))
                has_prev = (t % nbc) != 0
                qv, dov, lse_b, dd_b = qs[rows, :], dos[rows, :], lses[rows, :], dds[rows, :]
                lse_r, dd_r = pltpu.roll(lse_b, HD, 1), pltpu.roll(dd_b, HD, 1)
                heads = []
                for msk in (lo, jnp.logical_not(lo)):
                    heads.append((_zero_unless(msk, qv), _zero_unless(msk, dov), _head_lanes(msk, lse_b, lse_r),
                                  _head_lanes(msk, dd_b, dd_r)))
                dqs = [None, None]
                for krows, vmask in ((rows, cur_ok), (prow, prev_ok & has_prev)):
                    kv, vv = ks[krows, :], vs[krows, :]
                    dk = jnp.zeros((ABLK, LANES), F32)
                    dv = jnp.zeros((ABLK, LANES), F32)
                    for hi, (qh, doh, lse_h, dd_h) in enumerate(heads):
                        s = jnp.where(vmask, _dot(qh, kv, "nt"), NEG)
                        p = jnp.exp(s - lse_h)
                        ds = p * (_dot(doh, vv, "nt") - dd_h)
                        dqh = _dot(ds, kv, "nn")
                        dqs[hi] = dqh if dqs[hi] is None else dqs[hi] + dqh
                        dv = dv + _dot(p, doh, "tn")
                        dk = dk + _dot(ds, qh, "tn")
                    dk_dst[krows, :] += dk
                    dv_dst[krows, :] += dv
                dq_dst[rows, :] = jnp.where(lo, dqs[0], dqs[1])
                return carry

            lax.fori_loop(0, nb, blk, 0, unroll=4)
            if not first:
                for tok, cls in _class_chunks(S, d):
                    dq_ref[tok, :] = dq_ref[tok, :] + dqc[cls, :]
                    dk_ref[tok, :] = dk_ref[tok, :] + dkc[cls, :]
                    dv_ref[tok, :] = dv_ref[tok, :] + dvc[cls, :]

    col = pl.BlockSpec((S, LANES), lambda h: (0, h))
    col1 = pl.BlockSpec((S, LANES), lambda h: (0, h), pipeline_mode=pl.Buffered(1))
    vcol1 = pl.BlockSpec((S, LANES), lambda h: (0, h + v_cb), pipeline_mode=pl.Buffered(1))
    return pl.pallas_call(
        body, name=name, grid=(ad // LANES,),
        in_specs=[col, col, vcol1, col1, col1, col1], out_specs=[col, col, col],
        out_shape=[jax.ShapeDtypeStruct((S, ad), F32)] * 3,
        scratch_shapes=[pltpu.VMEM((S, LANES), BF16)] * 4 + [pltpu.VMEM((S, LANES), F32)] * 5,
        compiler_params=_cparams(("parallel",)),
    )(qn, kn, proj, do, lse, dd)


def _coords():
    return lax.axis_index("x"), lax.axis_index("y"), lax.axis_index("c")


def _exchange8(xs, per_dest, name):
    n = len(xs)
    blk = [x.shape[1:] if per_dest else x.shape for x in xs]

    def body(*refs):
        ins, outs = refs[:n], refs[n:2 * n]
        send_sems, recv_sems, local_sems = refs[2 * n:]
        x, y, c = _coords()
        sibling = (x, y, 1 - c)
        chips = [(1 - x, y), (x, 1 - y), (1 - x, 1 - y)]
        first, passed, mine = [], [], []
        for a in range(n):
            def src_for(cx, cy, a=a):
                return ins[a].at[2 * cx + cy] if per_dest else ins[a]

            def slot(px, py, pc, a=a):
                return outs[a].at[4 * px + 2 * py + pc]

            def copy(k, src, dst, to, a=a):
                return pltpu.make_async_remote_copy(src_ref=src, dst_ref=dst, send_sem=send_sems.at[7 * a + k],
                                                    recv_sem=recv_sems.at[7 * a + k], device_id=to, device_id_type=MESH)

            m = pltpu.make_async_copy(src_for(x, y), slot(x, y, c), local_sems.at[a])
            m.start()
            mine.append(m)
            cps = [copy(0, src_for(x, y), slot(x, y, c), sibling)]
            cps += [copy(1 + j, src_for(*chip), slot(x, y, c), (*chip, c)) for j, chip in enumerate(chips)]
            for cp in cps:
                cp.start()
            first += cps
        for a in range(n):
            def slot(px, py, pc, a=a):
                return outs[a].at[4 * px + 2 * py + pc]

            def copy(k, src, dst, to, a=a):
                return pltpu.make_async_remote_copy(src_ref=src, dst_ref=dst, send_sem=send_sems.at[7 * a + k],
                                                    recv_sem=recv_sems.at[7 * a + k], device_id=to, device_id_type=MESH)

            for j, chip in enumerate(chips):
                copy(1 + j, slot(*chip, c), slot(*chip, c), (*chip, c)).wait_recv()
                fw = copy(4 + j, slot(*chip, c), slot(*chip, c), sibling)
                fw.start()
                passed.append(fw)
        for a in range(n):
            def slot(px, py, pc, a=a):
                return outs[a].at[4 * px + 2 * py + pc]

            def copy(k, src, dst, to, a=a):
                return pltpu.make_async_remote_copy(src_ref=src, dst_ref=dst, send_sem=send_sems.at[7 * a + k],
                                                    recv_sem=recv_sems.at[7 * a + k], device_id=to, device_id_type=MESH)

            copy(0, slot(x, y, 1 - c), slot(x, y, 1 - c), sibling).wait_recv()
            for j, chip in enumerate(chips):
                copy(4 + j, slot(*chip, 1 - c), slot(*chip, 1 - c), sibling).wait_recv()
        for cp in first + passed:
            cp.wait_send()
        for m in mine:
            m.wait()

    anyspec = pl.BlockSpec(memory_space=pl.ANY)
    res = pl.pallas_call(
        body, name=name, in_specs=[anyspec] * n, out_specs=[anyspec] * n,
        out_shape=[jax.ShapeDtypeStruct((8,) + tuple(b), x.dtype) for b, x in zip(blk, xs)],
        scratch_shapes=[pltpu.SemaphoreType.DMA((7 * n,)), pltpu.SemaphoreType.DMA((7 * n,)),
                        pltpu.SemaphoreType.DMA((n,))],
    )(*xs)
    return list(res)


def _pair_swap(xs, name):
    n = len(xs)

    def body(*refs):
        ins, outs = refs[:n], refs[n:2 * n]
        send_sems, recv_sems = refs[2 * n:]
        x, y, c = _coords()
        cps = [pltpu.make_async_remote_copy(src_ref=ins[a].at[1 - c], dst_ref=outs[a], send_sem=send_sems.at[a],
                                            recv_sem=recv_sems.at[a], device_id=(x, y, 1 - c), device_id_type=MESH)
               for a in range(n)]
        for cp in cps:
            cp.start()
        for cp in cps:
            cp.wait()

    anyspec = pl.BlockSpec(memory_space=pl.ANY)
    res = pl.pallas_call(
        body, name=name, in_specs=[anyspec] * n, out_specs=[anyspec] * n,
        out_shape=[jax.ShapeDtypeStruct(x.shape[1:], x.dtype) for x in xs],
        scratch_shapes=[pltpu.SemaphoreType.DMA((n,)), pltpu.SemaphoreType.DMA((n,))],
    )(*xs)
    return list(res)


_HBM = pl.BlockSpec(memory_space=pltpu.HBM)
_SEM = pl.BlockSpec(memory_space=pltpu.SEMAPHORE)
_EFFECT = pltpu.SideEffectType.DATAFLOW_SIDE_EFFECTING


N_PEER = 3


def _peer(x, y, j):
    dx, dy = (j + 1) >> 1 & 1, (j + 1) & 1
    return (1 - x if dx else x, 1 - y if dy else y)


def _spread_copies(s_refs, l_refs, send_sems, recv_sems, per_dest):
    x, y, c = _coords()
    me = 4 * x + 2 * y + c
    cps = []
    for a in range(len(s_refs)):
        for j in range(N_PEER):
            tx, ty = _peer(x, y, j)
            src = s_refs[a].at[2 * tx + ty] if per_dest else s_refs[a]
            cps.append(pltpu.make_async_remote_copy(src_ref=src, dst_ref=l_refs[a].at[me],
                                                    send_sem=send_sems.at[N_PEER * a + j],
                                                    recv_sem=recv_sems.at[N_PEER * a + j], device_id=(tx, ty, c),
                                                    device_id_type=MESH))
    return cps


def _sibling_fill(lands, name):
    n = len(lands)

    def body(*refs):
        outs, send_sems, recv_sems = refs[n:2 * n], refs[2 * n], refs[2 * n + 1]
        x, y, c = _coords()
        cps = [pltpu.make_async_remote_copy(src_ref=outs[a].at[2 * k + c], dst_ref=outs[a].at[2 * k + c],
                                            send_sem=send_sems.at[4 * a + k], recv_sem=recv_sems.at[4 * a + k],
                                            device_id=(x, y, 1 - c), device_id_type=MESH)
               for a in range(n) for k in range(4)]
        for cp in cps:
            cp.start()
        for cp in cps:
            cp.wait()

    anyspec = pl.BlockSpec(memory_space=pl.ANY)
    res = pl.pallas_call(
        body, name=name, in_specs=[anyspec] * n, out_specs=[anyspec] * n,
        out_shape=[jax.ShapeDtypeStruct(t.shape, t.dtype) for t in lands], input_output_aliases={i: i for i in range(n)},
        scratch_shapes=[pltpu.SemaphoreType.DMA((4 * n,)), pltpu.SemaphoreType.DMA((4 * n,))],
    )(*lands)
    return list(res)


def _spread_start(srcs, per_dest, dev, chip, name):
    n = len(srcs)
    lands = []
    for s in srcs:
        own = lax.dynamic_index_in_dim(s, chip, 0, keepdims=False) if per_dest else s
        lands.append(lax.dynamic_update_index_in_dim(lax.empty((8,) + own.shape, own.dtype), own, dev, 0))

    def body(*refs):
        s_refs, l_refs, send_sems, recv_sems, token = refs[:n], refs[n:2 * n], refs[2 * n], refs[2 * n + 1], refs[-1]
        for cp in _spread_copies(s_refs, l_refs, send_sems, recv_sems, per_dest):
            cp.start()
        token[...] = jnp.zeros_like(token)

    hbm_in = [pltpu.with_memory_space_constraint(t, pltpu.HBM) for t in list(srcs) + lands]
    outs = pl.pallas_call(
        body, name=name,
        out_shape=(pltpu.SemaphoreType.DMA((N_PEER * n,)), pltpu.SemaphoreType.DMA((N_PEER * n,)),
                   *[pltpu.HBM(t.shape, t.dtype) for t in hbm_in], jax.ShapeDtypeStruct((8, LANES), F32)),
        in_specs=[_HBM] * (2 * n), out_specs=(_SEM, _SEM, *[_HBM] * (2 * n), pl.BlockSpec(memory_space=pltpu.VMEM)),
        input_output_aliases={i: 2 + i for i in range(2 * n)},
        compiler_params=pltpu.CompilerParams(has_side_effects=_EFFECT),
    )(*hbm_in)
    return (outs[0], outs[1], list(outs[2:2 + n]), list(outs[2 + n:2 + 2 * n])), outs[-1]


def _spread_wait(handle, per_dest, after, name):
    send_sems, recv_sems, srcs, lands = handle
    n = len(srcs)

    def body(*refs):
        s_refs, l_refs, send_ref, recv_ref = refs[:n], refs[n:2 * n], refs[2 * n], refs[2 * n + 1]
        for cp in _spread_copies(s_refs, l_refs, send_ref, recv_ref, per_dest):
            cp.wait_send()
            cp.wait_recv()

    outs = pl.pallas_call(
        body, name=name, out_shape=tuple(pltpu.HBM(t.shape, t.dtype) for t in srcs + lands),
        in_specs=[_HBM] * (2 * n) + [_SEM, _SEM, pl.BlockSpec(memory_space=pl.ANY)], out_specs=tuple([_HBM] * (2 * n)),
        input_output_aliases={i: i for i in range(2 * n)},
        compiler_params=pltpu.CompilerParams(has_side_effects=_EFFECT),
    )(*srcs, *lands, send_sems, recv_sems, after)
    return list(outs[n:])


def _pair_add(g2, theirs, half, name, tm=256):
    _, n, cdim = g2.shape
    tm = min(tm, n)

    def body(h_ref, a_ref, b_ref, o_ref):
        o_ref[...] = (a_ref[...] + b_ref[...]).astype(o_ref.dtype)

    grid_spec = pltpu.PrefetchScalarGridSpec(
        num_scalar_prefetch=1, grid=(n // tm,),
        in_specs=[pl.BlockSpec((None, tm, cdim), lambda i, h: (h[0], i, 0)), pl.BlockSpec((tm, cdim), lambda i, h: (i, 0))],
        out_specs=pl.BlockSpec((tm, cdim), lambda i, h: (i, 0)))
    return pl.pallas_call(body, name=name, grid_spec=grid_spec, out_shape=jax.ShapeDtypeStruct((n, cdim), BF16),
                          compiler_params=_cparams(("parallel",)))(half.reshape(1).astype(jnp.int32), g2, theirs)


def _adamw_math(w, g, m, v):
    m = ADAM_B1 * m + (1.0 - ADAM_B1) * g
    v = ADAM_B2 * v + (1.0 - ADAM_B2) * (g * g)
    m_hat = m / (1.0 - ADAM_B1 ** ADAM_STEP)
    v_hat = v / (1.0 - ADAM_B2 ** ADAM_STEP)
    delta = -ADAM_LR * (m_hat / (jnp.sqrt(v_hat) + ADAM_EPS) + ADAM_WD * w)
    return delta, m, v


def _adamw(parts, w, m, v, name, tm=128):
    npart, R, C = parts.shape
    tm = min(tm, R)

    def body(p_ref, w_ref, m_ref, v_ref, g_out, d_out, m_out, v_out):
        g = p_ref[0].astype(F32)
        for i in range(1, npart):
            g = g + p_ref[i].astype(F32)
        d, mm, vv = _adamw_math(w_ref[...], g, m_ref[...], v_ref[...])
        g_out[...] = g
        d_out[...] = d
        m_out[...] = mm
        v_out[...] = vv

    spec = pl.BlockSpec((tm, C), lambda i: (i, 0))
    return pl.pallas_call(
        body, name=name, grid=(R // tm,),
        in_specs=[pl.BlockSpec((npart, tm, C), lambda i: (0, i, 0)), spec, spec, spec], out_specs=[spec] * 4,
        out_shape=[jax.ShapeDtypeStruct((R, C), F32)] * 4,
        compiler_params=_cparams(("parallel",)),
    )(parts, w, m, v)


def _sum_parts(parts, name):
    npart, R, C = parts.shape

    def body(p_ref, o_ref):
        g = p_ref[0]
        for i in range(1, npart):
            g = g + p_ref[i]
        o_ref[...] = g

    return pl.pallas_call(body, name=name, out_shape=jax.ShapeDtypeStruct((R, C), F32))(parts)


def _mod_fwd(c_all, w_ada, b_sh, name):
    def body(c_ref, w_ref, b_ref, o_ref):
        o_ref[...] = _dot(_silu(c_ref[...]), w_ref[...], "nn") + b_ref[...]

    return pl.pallas_call(body, name=name, out_shape=jax.ShapeDtypeStruct((c_all.shape[0], w_ada.shape[1]), F32),
                          compiler_params=pltpu.CompilerParams(vmem_limit_bytes=VMEM_LIMIT))(c_all, w_ada, b_sh)


def _mod_wgrad(c_all, dmod_sh, name):
    def body(c_ref, d_ref, o_ref):
        o_ref[...] = _dot(_silu(c_ref[...]), d_ref[...], "tn")

    return pl.pallas_call(body, name=name, out_shape=jax.ShapeDtypeStruct((c_all.shape[1], dmod_sh.shape[1]), F32),
                          compiler_params=pltpu.CompilerParams(vmem_limit_bytes=VMEM_LIMIT))(c_all, dmod_sh)


def _pad_lanes(v):
    return jnp.pad(v, ((0, 0), (0, (-v.shape[1]) % LANES)))


def kernel(x, c, norm1_w, norm2_w, w_ada, b_ada, w_in, conv_w, conv_b, dt_bias, a_log, d_skip, ssd_norm_w, q_norm_w, k_norm_w, attn_norm_w, w_out, w_ff1, w_ff2, loss_target, m_norm1_w, m_norm2_w, m_w_ada, m_b_ada, m_w_in, m_conv_w, m_conv_b, m_dt_bias, m_a_log, m_d_skip, m_ssd_norm_w, m_q_norm_w, m_k_norm_w, m_attn_norm_w, m_w_out, m_w_ff1, m_w_ff2, v_norm1_w, v_norm2_w, v_w_ada, v_b_ada, v_w_in, v_conv_w, v_conv_b, v_dt_bias, v_a_log, v_d_skip, v_ssd_norm_w, v_q_norm_w, v_k_norm_w, v_attn_norm_w, v_w_out, v_w_ff1, v_w_ff2):
    xi, yi, ci = _coords()
    chip = 2 * xi + yi
    dev = 2 * chip + ci
    xs, tgt = x[0], loss_target[0]
    S, D = xs.shape
    DI, AD = NH_SSD * HD, NH_ATT * HD
    CC = DI + 2 * NG * NSTATE
    PW = DI + CC + 3 * AD + LANES
    DFF = w_ff1.shape[2] * 4
    MIX = DI + AD
    o_xbc, o_q, o_k, o_v, o_dt = DI, DI + CC, DI + CC + AD, DI + CC + 2 * AD, DI + CC + 3 * AD

    def half_rows(w):
        r = w.shape[0] // 2
        return lax.dynamic_slice_in_dim(w, ci * r, r, 0).astype(BF16)

    c_all, conv_w_all = _exchange8([c, conv_w[0]], False, "gather_c_conv_w")
    c_all = c_all.reshape(8, D)
    c_all = jnp.pad(c_all, ((0, 8), (0, 0)))
    nmod = w_ada.shape[2]
    b_sh = lax.dynamic_slice_in_dim(b_ada, chip * nmod, nmod, 1)
    mod_sh = _mod_fwd(c_all, w_ada[0], b_sh, "mod_fwd")
    mod_all = _exchange8([mod_sh[:8]], False, "gather_mod")[0]
    mod_me = lax.dynamic_index_in_dim(mod_all[0::2], dev, 1, keepdims=False).reshape(1, 4 * nmod)
    shift1, scale1, gate1, shift2, scale2, gate2 = [mod_me[:, i * D:(i + 1) * D] for i in range(6)]

    g_in = _exchange8([half_rows(w_in[0])], False, "gather_w_in")[0]
    rest_handle, rest_token = _spread_start([half_rows(w_out[0]), half_rows(w_ff1[0]), half_rows(w_ff2[0])], False, dev,
                                            chip, "gather_rest_start")
    shift1 = shift1 + rest_token[0, 0]
    wsh = w_in.shape[2]
    w_in_f = g_in.reshape(4, D, wsh).transpose(1, 0, 2).reshape(D, 4 * wsh)
    n_zx = DI + CC
    w_proj = jnp.concatenate([w_in_f[:, :n_zx], w_in_f[:, n_zx + NH_SSD:], w_in_f[:, n_zx:n_zx + NH_SSD],
                              jnp.zeros((D, LANES - NH_SSD), BF16)], axis=1)

    dtb, alog, dsk = _pad_lanes(dt_bias), _pad_lanes(a_log), _pad_lanes(d_skip)
    qw2 = jnp.concatenate([q_norm_w, q_norm_w], axis=1)
    kw2 = jnp.concatenate([k_norm_w, k_norm_w], axis=1)
    conv_w_f = conv_w_all[0::2].transpose(1, 0, 2).reshape(KCONV, CC)

    h1 = _rows("norm1", lambda r, k: ([_normmod(r[0], *k)], []), [(xs, 0, D)], [norm1_w, scale1, shift1],
               [(D, BF16)], [], S)[0]
    proj = _matmul(h1, w_proj, "nn", F32, "in_proj", tn=896)
    xbc = _conv_fwd(proj, o_xbc, CC, conv_w_f, conv_b, "conv_fwd")
    y_ssd, hsave = _ssd_fwd(xbc, proj, o_dt // LANES, dtb, alog, dsk, ssd_norm_w, "ssd_fwd")

    def qk_call(name, col0, w2, scale):
        def body(t_ref, w_ref, o_ref):
            o_ref[...] = _headnorm(t_ref[...], w_ref[...], scale)
        return pl.pallas_call(
            body, name=name, grid=(AD // LANES,),
            in_specs=[pl.BlockSpec((S, LANES), lambda j: (0, j + col0 // LANES)),
                      pl.BlockSpec((1, LANES), lambda j: (0, 0))],
            out_specs=pl.BlockSpec((S, LANES), lambda j: (0, j)),
            out_shape=jax.ShapeDtypeStruct((S, AD), F32), compiler_params=_cparams(("parallel",)),
        )(proj, w2)

    qn = qk_call("q_norm", o_q, qw2, HD ** -0.5)
    kn = qk_call("k_norm", o_k, kw2, 1.0)
    o_att, lse = _attn_fwd(qn, kn, proj, o_v // LANES, "attn_fwd")
    y_att = _rows("attn_out_norm", lambda r, k: ([_rmsw(r[0], k[0])], []), [(o_att, 0, AD)], [attn_norm_w],
                  [(AD, BF16)], [], S)[0]
    g_out, g_ff1, g_ff2 = _sibling_fill(_spread_wait(rest_handle, False, o_att, "gather_rest_wait"), "gather_rest_fill")
    w_out_f = g_out.reshape(MIX, D)
    w_out_a, w_out_b = w_out_f[:DI], w_out_f[DI:]
    w_ff1_f = g_ff1.reshape(4, D, DFF // 4).transpose(1, 0, 2).reshape(D, DFF)
    w_ff2_f = g_ff2.reshape(DFF, D)
    mix_a = _matmul(y_ssd, w_out_a, "nn", F32, "out_proj_a")
    mix = _matmul(y_att, w_out_b, "nn", F32, "out_proj_b", epilogue=lambda r, e: r + e, extras=(mix_a,))
    x2, h2 = _rows("resid_norm2", lambda r, k: (list(_resid_normmod(r[0], r[1], *k)), []), [(xs, 0, D), (mix, 0, D)],
                   [gate1, norm2_w, scale2, shift2], [(D, F32), (D, BF16)], [], S)
    u = _matmul(h2, w_ff1_f, "nn", F32, "ff1")
    relu2 = lambda t: jnp.square(jnp.maximum(t, 0.0))
    ff = _matmul(u, w_ff2_f, "nn", F32, "ff2", a_fn=relu2)

    def loss_fn(r, k):
        x2_, ff_, t_ = r
        err = x2_ + k[0] * ff_ - t_
        dy_ = err * (1.0 / D)
        ls = jnp.sum(jnp.sum(0.5 * err * err, axis=1, keepdims=True), axis=0, keepdims=True) * (1.0 / D)
        return [dy_, dy_ * k[0]], [ls, jnp.sum(dy_ * ff_, axis=0, keepdims=True)]

    dy, dff, loss_p, dgate2 = _rows("loss", loss_fn, [(x2, 0, D), (ff, 0, D), (tgt, 0, D)], [gate2],
                                    [(D, F32), (D, BF16)], [(1, 1), (1, D)], S)
    du = _matmul(dff, w_ff2_f, "nt", BF16, "ff2_dx", epilogue=lambda r, e: r * (2.0 * jnp.maximum(e, 0.0)), extras=(u,))
    gw_ff2 = _matmul(u, dff, "tn", F32, "ff2_dw", a_fn=relu2)
    gw_ff1 = _matmul(h2, du, "tn", F32, "ff1_dw")

    def by_half_cols(g):
        r, c4 = g.shape
        return g.reshape(2, r // 2, 4, c4 // 4).transpose(0, 2, 1, 3)

    def by_half_rows(g):
        r4, cdim = g.shape
        return g.reshape(4, 2, r4 // 8, cdim).transpose(1, 0, 2, 3)

    def scatter_start(layouts, tag):
        theirs = _pair_swap(layouts, "pair_swap_" + tag)
        sums = []
        for i, (g2, t) in enumerate(zip(layouts, theirs)):
            _, r2, cdim = t.shape
            sm = _pair_add(g2.reshape(2, 4 * r2, cdim), t.reshape(4 * r2, cdim), ci, "pair_add_%s_%d" % (tag, i))
            sums.append(sm.reshape(4, r2, cdim))
        return _spread_start(sums, True, dev, chip, "scatter_%s_start" % tag)

    def scatter_wait(handle, after, tag):
        lands = _sibling_fill(_spread_wait(handle, True, after, "scatter_%s_wait" % tag), "scatter_%s_fill" % tag)
        return [s.reshape(4, 2 * s.shape[1], s.shape[2]) for s in lands]

    ff_handle, ff_token = scatter_start([by_half_cols(gw_ff1), by_half_rows(gw_ff2)], "ff")
    dh2 = _matmul(du, w_ff1_f, "nt", F32, "ff1_dx")

    def resid_bwd(r, k):
        x_, mix_, dx2a, dh2_ = r
        _, vjp = jax.vjp(_resid_normmod, x_, mix_, *k)
        dx, dmix_, dg, dnw, dsc, dsh = vjp((dx2a, dh2_))
        return [dx, dmix_], [dg, dnw, dsc, dsh]

    dx2, dmix, dgate1, g_norm2, dscale2, dshift2 = _rows(
        "resid_norm2_bwd", resid_bwd, [(xs, 0, D), (mix, 0, D), (dy, 0, D), (dh2, 0, D)],
        [gate1 + ff_token[0, 0], norm2_w, scale2, shift2], [(D, F32), (D, BF16)], [(1, D)] * 4, S)
    gw_out = jnp.concatenate([_matmul(y_ssd, dmix, "tn", F32, "out_proj_dw_a"),
                              _matmul(y_att, dmix, "tn", F32, "out_proj_dw_b")], axis=0)
    out_handle, out_token = scatter_start([by_half_rows(gw_out)], "out")
    dy_ssd = _matmul(dmix, w_out_a, "nt", F32, "out_proj_dx_a")
    dy_att = _matmul(dmix, w_out_b, "nt", F32, "out_proj_dx_b")

    def attn_norm_bwd(r, k):
        o_, dyo = r
        _, vjp = jax.vjp(_rmsw, o_, k[0])
        do_, dw_ = vjp(dyo)
        lo = _lane_mask()
        dd_blocks = []
        for b in range(AD // LANES):
            t = (do_ * o_)[:, b * LANES:(b + 1) * LANES]
            s0 = jnp.sum(jnp.where(lo, t, 0.0), axis=1, keepdims=True)
            s1 = jnp.sum(jnp.where(lo, 0.0, t), axis=1, keepdims=True)
            dd_blocks.append(jnp.where(lo, s0, s1))
        return [do_, jnp.concatenate(dd_blocks, axis=1)], [dw_]

    do_att, dd_att, g_attn_norm = _rows("attn_norm_bwd", attn_norm_bwd, [(o_att, 0, AD), (dy_att, 0, AD)],
                                        [attn_norm_w + out_token[0, 0]], [(AD, F32), (AD, F32)], [(1, AD)], S)
    dq_n, dk_n, dv = _attn_bwd(qn, kn, proj, o_v // LANES, do_att, lse, dd_att, "attn_bwd")

    def qk_bwd_call(name, col0, w2, scale, g):
        def body(t_ref, w_ref, g_ref, o_ref, dw_ref):
            @pl.when(pl.program_id(0) == 0)
            def _():
                dw_ref[...] = jnp.zeros_like(dw_ref)
            _, vjp = jax.vjp(lambda t, w: _headnorm(t, w, scale), t_ref[...], w_ref[...])
            dt_, dw_ = vjp(g_ref[...])
            o_ref[...] = dt_.astype(BF16)
            dw_ref[...] += dw_
        blk = pl.BlockSpec((S, LANES), lambda j: (0, j))
        return pl.pallas_call(
            body, name=name, grid=(AD // LANES,),
            in_specs=[pl.BlockSpec((S, LANES), lambda j: (0, j + col0 // LANES)),
                      pl.BlockSpec((1, LANES), lambda j: (0, 0)), blk],
            out_specs=[blk, pl.BlockSpec((1, LANES), lambda j: (0, 0))],
            out_shape=[jax.ShapeDtypeStruct((S, AD), BF16), jax.ShapeDtypeStruct((1, LANES), F32)],
            compiler_params=_cparams(("arbitrary",)),
        )(proj, w2, g)

    dq, g_qw2 = qk_bwd_call("q_norm_bwd", o_q, qw2, HD ** -0.5, dq_n)
    dk, g_kw2 = qk_bwd_call("k_norm_bwd", o_k, kw2, 1.0, dk_n)
    g_q_norm = g_qw2[:, :HD] + g_qw2[:, HD:]
    g_k_norm = g_kw2[:, :HD] + g_kw2[:, HD:]

    dxbc, dz, ddtr, g_dtb, g_alog, g_dsk, g_ssd_norm = _ssd_bwd(
        xbc, proj, o_dt // LANES, dtb, alog, dsk, ssd_norm_w, hsave, dy_ssd, "ssd_bwd")
    dxbc_pre, g_conv_w, g_conv_b = _conv_bwd(proj, o_xbc, CC, conv_w_f, conv_b, dxbc, "conv_bwd")
    dproj = jnp.concatenate([dz.astype(BF16), dxbc_pre.astype(BF16), dq, dk, dv.astype(BF16), ddtr.astype(BF16)], axis=1)
    gw_proj = _matmul(h1, dproj, "tn", F32, "in_proj_dw", tn=896)
    gw_in = jnp.concatenate([gw_proj[:, :n_zx], gw_proj[:, o_dt:o_dt + NH_SSD], gw_proj[:, n_zx:o_dt]], axis=1)
    in_handle, in_token = scatter_start([by_half_cols(gw_in)], "in")
    dh1 = _matmul(dproj, w_proj, "nt", F32, "in_proj_dx", tk=896)

    def norm1_bwd(r, k):
        x_, dh_, dres = r
        _, vjp = jax.vjp(_normmod, x_, *k)
        dx, dnw, dsc, dsh = vjp(dh_)
        return [dx + dres], [dnw, dsc, dsh]

    grad_x, g_norm1, dscale1, dshift1 = _rows("norm1_bwd", norm1_bwd, [(xs, 0, D), (dh1, 0, D), (dx2, 0, D)],
                                              [norm1_w + in_token[0, 0], scale1, shift1], [(D, F32)], [(1, D)] * 3, S)
    dmod =jnp.concatenate([dshift1, dscale1, dgate1, dshift2, dscale2, dgate2], axis=1)

    small = [g_norm1, g_norm2, dmod, g_conv_b, g_dtb, g_alog, g_dsk, g_ssd_norm, _pad_lanes(g_q_norm),
             _pad_lanes(g_k_norm), g_attn_norm, g_conv_w.reshape(1, KCONV * CC)]
    sizes = [t.shape[1] for t in small]
    packed = jnp.concatenate(small, axis=1)
    nrow = -(-packed.shape[1] // LANES // 8) * 8
    packed = jnp.pad(packed, ((0, 0), (0, nrow * LANES - packed.shape[1]))).reshape(nrow, LANES)
    packed_all = _exchange8([packed], False, "gather_small_grads")[0]
    tot = _sum_parts(packed_all, "sum_small_grads").reshape(1, nrow * LANES)
    offs = [sum(sizes[:i]) for i in range(len(sizes))]
    (g_norm1, g_norm2, g_b_ada, g_conv_b, g_dtb, g_alog, g_dsk, g_ssd_norm, g_q_norm, g_k_norm, g_attn_norm,
     g_conv_w) = [tot[:, o:o + n] for o, n in zip(offs, sizes)]
    g_dtb, g_alog, g_dsk = g_dtb[:, :NH_SSD], g_alog[:, :NH_SSD], g_dsk[:, :NH_SSD]
    g_q_norm, g_k_norm = g_q_norm[:, :HD], g_k_norm[:, :HD]
    ccs = CC // 4
    g_conv_w = lax.dynamic_slice_in_dim(g_conv_w.reshape(KCONV, CC), chip * ccs, ccs, 1)

    dmod_all = packed_all.reshape(8, nrow * LANES)[:, offs[2]:offs[2] + 6 * D]
    dmod_sh = jnp.pad(lax.dynamic_slice_in_dim(dmod_all, chip * nmod, nmod, 1), ((0, 8), (0, 0)))
    gw_ada = _mod_wgrad(c_all, dmod_sh, "mod_wgrad")

    parts_ff1, parts_ff2 = scatter_wait(ff_handle, in_token, "ff")
    res_ff1 = _adamw(parts_ff1, w_ff1[0], m_w_ff1[0], v_w_ff1[0], "adamw_w_ff1")
    res_ff2 = _adamw(parts_ff2, w_ff2[0], m_w_ff2[0], v_w_ff2[0], "adamw_w_ff2")
    res_out = _adamw(scatter_wait(out_handle, in_token, "out")[0], w_out[0], m_w_out[0], v_w_out[0], "adamw_w_out")
    res_ada = _adamw(gw_ada[None], w_ada[0], m_w_ada[0], v_w_ada[0], "adamw_w_ada")
    res_in = _adamw(scatter_wait(in_handle, res_ada[0], "in")[0], w_in[0], m_w_in[0], v_w_in[0], "adamw_w_in")

    small_names = ["norm1_w", "norm2_w", "b_ada", "conv_w", "conv_b", "dt_bias", "a_log", "d_skip", "ssd_norm_w",
                   "q_norm_w", "k_norm_w", "attn_norm_w"]
    small_g = dict(norm1_w=g_norm1, norm2_w=g_norm2, b_ada=g_b_ada, conv_w=g_conv_w.reshape(1, KCONV * ccs),
                   conv_b=g_conv_b, dt_bias=g_dtb, a_log=g_alog, d_skip=g_dsk, ssd_norm_w=g_ssd_norm, q_norm_w=g_q_norm,
                   k_norm_w=g_k_norm, attn_norm_w=g_attn_norm)
    small_w = dict(norm1_w=(norm1_w, m_norm1_w, v_norm1_w), norm2_w=(norm2_w, m_norm2_w, v_norm2_w),
                   b_ada=(b_ada, m_b_ada, v_b_ada),
                   conv_w=tuple(t.reshape(1, KCONV * ccs) for t in (conv_w, m_conv_w, v_conv_w)),
                   conv_b=(conv_b, m_conv_b, v_conv_b), dt_bias=(dt_bias, m_dt_bias, v_dt_bias),
                   a_log=(a_log, m_a_log, v_a_log), d_skip=(d_skip, m_d_skip, v_d_skip),
                   ssd_norm_w=(ssd_norm_w, m_ssd_norm_w, v_ssd_norm_w), q_norm_w=(q_norm_w, m_q_norm_w, v_q_norm_w),
                   k_norm_w=(k_norm_w, m_k_norm_w, v_k_norm_w), attn_norm_w=(attn_norm_w, m_attn_norm_w, v_attn_norm_w))
    ssz = [_pad_lanes(small_g[n]).shape[1] for n in small_names]
    soff = [sum(ssz[:i]) for i in range(len(ssz))]
    srow = -(-sum(ssz) // LANES // 8) * 8

    def pack(ts, fill):
        t = jnp.concatenate([jnp.pad(t, ((0, 0), (0, (-t.shape[1]) % LANES)), constant_values=fill) for t in ts], axis=1)
        return jnp.pad(t, ((0, 0), (0, srow * LANES - t.shape[1])), constant_values=fill).reshape(srow, LANES)

    sg = pack([small_g[n] for n in small_names], 0.0)
    sw = pack([small_w[n][0] for n in small_names], 0.0)
    sm_ = pack([small_w[n][1] for n in small_names], 0.0)
    sv = pack([small_w[n][2] for n in small_names], 1.0)
    _, s_delta, s_m, s_v = _adamw(sg[None], sw, sm_, sv, "adamw_small", tm=srow)

    def unpack(t, n):
        i = small_names.index(n)
        return t.reshape(1, srow * LANES)[:, soff[i]:soff[i] + small_g[n].shape[1]].reshape(small_w[n][0].shape)

    loss = lax.psum(loss_p[0, 0], ("x", "y", "c"))
    big_res = dict(w_ada=res_ada, w_in=res_in, w_out=res_out, w_ff1=res_ff1, w_ff2=res_ff2)
    order = ["norm1_w", "norm2_w", "w_ada", "b_ada", "w_in", "conv_w", "conv_b", "dt_bias", "a_log", "d_skip",
             "ssd_norm_w", "q_norm_w", "k_norm_w", "attn_norm_w", "w_out", "w_ff1", "w_ff2"]
    grads, deltas, new_m, new_v = [], [], [], []
    for n in order:
        if n in big_res:
            g_, d_, m_, v_ = [t[None] for t in big_res[n]]
        else:
            g_ = small_g[n].reshape(small_w[n][0].shape)
            d_, m_, v_ = unpack(s_delta, n), unpack(s_m, n), unpack(s_v, n)
            if n == "conv_w":
                g_, d_, m_, v_ = [t.reshape(conv_w.shape) for t in (g_, d_, m_, v_)]
        grads.append(g_)
        deltas.append(d_)
        new_m.append(m_)
        new_v.append(v_)
    return (loss, grad_x[None], *grads, *deltas, *new_m, *new_v)
```

```python
import functools

import jax
import jax.numpy as jnp
from jax import lax
from jax.experimental import pallas as pl
from jax.experimental.pallas import tpu as pltpu

F32, BF16 = jnp.float32, jnp.bfloat16
EPS = 1e-6
HD = 64
NH_SSD = 16
NG = 4
NSTATE = 128
KCONV = 4
CHUNK = 128
NH_ATT = 16
PATTERNS = ((128, 1), (512, 4), (2048, 16))
ABLK = 128
QTILE = 128
LANES = 128
ADAM_LR, ADAM_B1, ADAM_B2, ADAM_EPS, ADAM_WD, ADAM_STEP = 0.001, 0.9, 0.999, 1e-08, 0.01, 10
VMEM_LIMIT = 56 * 1024 * 1024
MESH = pl.DeviceIdType.MESH
NEG = -1e30

_DN = {"nn": (((1,), (0,)), ((), ())), "nt": (((1,), (1,)), ((), ())), "tn": (((0,), (0,)), ((), ()))}


def _cparams(sem):
    return pltpu.CompilerParams(dimension_semantics=sem, vmem_limit_bytes=VMEM_LIMIT)


def _tile(n, cap):
    if n % LANES or n <= LANES:
        return n
    best = LANES
    for t in range(LANES, min(n, cap) + 1, LANES):
        if n % t == 0:
            best = t
    return best


def _silu(x):
    return x / (1.0 + jnp.exp(-x))


def _softplus(x):
    return jnp.maximum(x, 0.0) + jnp.log(1.0 + jnp.exp(-jnp.abs(x)))


def _dot(a, b, dims):
    return lax.dot_general(a.astype(BF16), b.astype(BF16), _DN[dims], preferred_element_type=F32)


def _matmul(a, b, dims, out_dtype, name, a_fn=None, epilogue=None, extras=(), tm=1024, tn=1024, tk=1024):
    if dims == "nn":
        (M, K), (_, N) = a.shape, b.shape
    elif dims == "nt":
        (M, K), (N, _) = a.shape, b.shape
    else:
        (K, M), (_, N) = a.shape, b.shape
    tm, tn, tk = _tile(M, tm), _tile(N, tn), _tile(K, tk)
    nk = K // tk
    ne = len(extras)

    def body(a_ref, b_ref, *rest):
        e_refs, o_ref = rest[:ne], rest[ne]
        av = a_ref[...]
        if a_fn is not None:
            av = a_fn(av)
        part = _dot(av, b_ref[...], dims)

        def finish(r):
            if epilogue is not None:
                r = epilogue(r, *[e[...] for e in e_refs])
            o_ref[...] = r.astype(out_dtype)

        if nk == 1:
            finish(part)
            return
        acc = rest[ne + 1]
        k = pl.program_id(2)

        @pl.when(k == 0)
        def _():
            acc[...] = part

        @pl.when(k > 0)
        def _():
            acc[...] += part

        @pl.when(k == nk - 1)
        def _():
            finish(acc[...])

    a_spec = pl.BlockSpec((tk, tm), lambda i, j, k: (k, i)) if dims == "tn" else pl.BlockSpec((tm, tk), lambda i, j, k: (i, k))
    b_spec = pl.BlockSpec((tn, tk), lambda i, j, k: (j, k)) if dims == "nt" else pl.BlockSpec((tk, tn), lambda i, j, k: (k, j))
    o_spec = pl.BlockSpec((tm, tn), lambda i, j, k: (i, j))
    return pl.pallas_call(
        body, name=name, grid=(M // tm, N // tn, nk),
        in_specs=[a_spec, b_spec] + [o_spec] * ne, out_specs=o_spec,
        out_shape=jax.ShapeDtypeStruct((M, N), out_dtype),
        scratch_shapes=[pltpu.VMEM((tm, tn), F32)] if nk > 1 else [],
        compiler_params=_cparams(("parallel", "parallel", "arbitrary")),
    )(a, b, *extras)


def _rows(name, fn, rows, consts, outs, accs, n_rows, tm=256):
    tm = min(tm, n_rows)
    nr, nc, no, na = len(rows), len(consts), len(outs), len(accs)

    def body(*refs):
        r_refs, c_refs = refs[:nr], refs[nr:nr + nc]
        o_refs, a_refs = refs[nr + nc:nr + nc + no], refs[nr + nc + no:]
        o_vals, a_vals = fn([r[...] for r in r_refs], [c[...] for c in c_refs])
        for ref, val in zip(o_refs, o_vals):
            ref[...] = val.astype(ref.dtype)
        if na:
            @pl.when(pl.program_id(0) == 0)
            def _():
                for ref in a_refs:
                    ref[...] = jnp.zeros_like(ref)
            for ref, val in zip(a_refs, a_vals):
                ref[...] += val

    in_specs = [pl.BlockSpec((tm, w), lambda i, cb=cb: (i, cb)) for (_, cb, w) in rows]
    in_specs += [pl.BlockSpec(cst.shape, lambda i, nd=cst.ndim: (0,) * nd) for cst in consts]
    out_specs = [pl.BlockSpec((tm, w), lambda i: (i, 0)) for (w, _) in outs]
    out_specs += [pl.BlockSpec(s, lambda i: (0, 0)) for s in accs]
    out_shape = [jax.ShapeDtypeStruct((n_rows, w), dt) for (w, dt) in outs]
    out_shape += [jax.ShapeDtypeStruct(s, F32) for s in accs]
    res = pl.pallas_call(
        body, name=name, grid=(n_rows // tm,), in_specs=in_specs, out_specs=out_specs, out_shape=out_shape,
        compiler_params=_cparams(("arbitrary",)),
    )(*[r[0] for r in rows], *consts)
    return res


def _normmod(x, nw, sc, sh):
    r = lax.rsqrt(jnp.mean(x * x, axis=-1, keepdims=True) + EPS)
    return (x * r) * nw * (1.0 + sc) + sh


def _resid_normmod(x, mix, g, nw, sc, sh):
    x2 = x + g * mix
    return x2, _normmod(x2, nw, sc, sh)


def _rmsw(o, w):
    return o * lax.rsqrt(jnp.mean(o * o, axis=-1, keepdims=True) + EPS) * w


def _lane_mask():
    return lax.broadcasted_iota(jnp.int32, (1, LANES), 1) < HD


def _headnorm(t, w, scale):
    lo = _lane_mask()
    t2 = t * t
    s0 = jnp.sum(jnp.where(lo, t2, 0.0), axis=1, keepdims=True)
    s1 = jnp.sum(jnp.where(lo, 0.0, t2), axis=1, keepdims=True)
    ms = jnp.where(lo, s0, s1) * (1.0 / HD)
    return t * lax.rsqrt(ms + EPS) * w * scale


CONV_ROWS = 128
CONV_HALO = 8


def _conv_cols(n_ch):
    return _tile(n_ch, LANES)


def _conv_fwd(proj, col0, n_ch, conv_w, conv_b, name):
    S = proj.shape[0]
    tc = _conv_cols(n_ch)

    R, H = CONV_ROWS, CONV_HALO

    def body(u_ref, w_ref, b_ref, o_ref):
        w = [w_ref[i:i + 1, :] for i in range(KCONV)]
        b = b_ref[...]

        def chunk(ext):
            acc = b + w[KCONV - 1] * ext[H:]
            for i in range(KCONV - 1):
                acc = acc + w[i] * pltpu.roll(ext, KCONV - 1 - i, 0)[H:]
            return _silu(acc)

        o_ref[0:R, :] = chunk(jnp.concatenate([jnp.zeros((H, tc), F32), u_ref[0:R, :]], axis=0))

        def step(c, carry):
            r0 = pl.multiple_of(c * R, R)
            o_ref[pl.ds(r0, R), :] = chunk(u_ref[pl.ds(pl.multiple_of(r0 - H, H), R + H), :])
            return carry

        lax.fori_loop(1, S // R, step, 0)

    return pl.pallas_call(
        body, name=name, grid=(n_ch // tc,),
        in_specs=[pl.BlockSpec((S, tc), lambda j: (0, j + col0 // tc)),
                  pl.BlockSpec((KCONV, tc), lambda j: (0, j)), pl.BlockSpec((1, tc), lambda j: (0, j))],
        out_specs=pl.BlockSpec((S, tc), lambda j: (0, j)),
        out_shape=jax.ShapeDtypeStruct((S, n_ch), F32),
        compiler_params=_cparams(("parallel",)),
    )(proj, conv_w, conv_b)


def _conv_bwd(proj, col0, n_ch, conv_w, conv_b, dxbc, name):
    S = proj.shape[0]
    tc = _conv_cols(n_ch)

    R, H = CONV_ROWS, CONV_HALO

    def body(u_ref, w_ref, b_ref, g_ref, du_ref, dw_ref, db_ref):
        w = [w_ref[i:i + 1, :] for i in range(KCONV)]
        b = b_ref[...]
        pad = jnp.zeros((H, tc), F32)

        def chunk(u_ext, g_ext):
            taps = [pltpu.roll(u_ext, KCONV - 1 - i, 0)[H:] for i in range(KCONV - 1)] + [u_ext[H:]]
            acc = b
            for i in range(KCONV):
                acc = acc + w[i] * taps[i]
            sig = 1.0 / (1.0 + jnp.exp(-acc))
            dacc = g_ext * (sig * (1.0 + acc * (1.0 - sig)))
            du = w[KCONV - 1] * dacc[:R]
            for i in range(KCONV - 1):
                du = du + w[i] * pltpu.roll(dacc, R + H - (KCONV - 1 - i), 0)[:R]
            d = dacc[:R]
            return du, [jnp.sum(d * t[:R], axis=0, keepdims=True) for t in taps], jnp.sum(d, axis=0, keepdims=True)

        du, dws, db = chunk(jnp.concatenate([pad, u_ref[0:R + H, :]], axis=0), g_ref[0:R + H, :])
        du_ref[0:R, :] = du

        def step(c, carry):
            r0 = pl.multiple_of(c * R, R)
            du_c, dws_c, db_c = chunk(u_ref[pl.ds(pl.multiple_of(r0 - H, H), R + 2 * H), :], g_ref[pl.ds(r0, R + H), :])
            du_ref[pl.ds(r0, R), :] = du_c
            return [a + b_ for a, b_ in zip(carry[0], dws_c)], carry[1] + db_c

        dws, db = lax.fori_loop(1, S // R - 1, step, (dws, db))
        du, dws_l, db_l = chunk(jnp.concatenate([u_ref[S - R - H:S, :], pad], axis=0),
                                jnp.concatenate([g_ref[S - R:S, :], pad], axis=0))
        du_ref[S - R:S, :] = du
        for i in range(KCONV):
            dw_ref[i:i + 1, :] = dws[i] + dws_l[i]
        db_ref[...] = db + db_l

    return pl.pallas_call(
        body, name=name, grid=(n_ch // tc,),
        in_specs=[pl.BlockSpec((S, tc), lambda j: (0, j + col0 // tc)),
                  pl.BlockSpec((KCONV, tc), lambda j: (0, j)), pl.BlockSpec((1, tc), lambda j: (0, j)),
                  pl.BlockSpec((S, tc), lambda j: (0, j))],
        out_specs=[pl.BlockSpec((S, tc), lambda j: (0, j)), pl.BlockSpec((KCONV, tc), lambda j: (0, j)),
                   pl.BlockSpec((1, tc), lambda j: (0, j))],
        out_shape=[jax.ShapeDtypeStruct((S, n_ch), F32), jax.ShapeDtypeStruct((KCONV, n_ch), F32),
                   jax.ShapeDtypeStruct((1, n_ch), F32)],
        compiler_params=_cparams(("parallel",)),
    )(proj, conv_w, conv_b, dxbc)


@functools.partial(jax.custom_vjp, nondiff_argnums=(2,))
def _mm(a, b, dims):
    return _dot(a, b, dims)


def _mm_fwd(a, b, dims):
    return _dot(a, b, dims), (a, b)


def _mm_bwd(dims, res, g):
    a, b = res
    if dims == "nn":
        return _dot(g, b, "nt"), _dot(a, g, "tn")
    if dims == "nt":
        return _dot(g, b, "nn"), _dot(g, a, "tn")
    return _dot(b, g, "nt"), _dot(a, g, "nn")


_mm.defvjp(_mm_fwd, _mm_bwd)


def _tri_dot(x, upper):
    n = x.shape[0]
    r = lax.broadcasted_iota(jnp.int32, (n, n), 0)
    c = lax.broadcasted_iota(jnp.int32, (n, n), 1)
    t = jnp.where((r <= c) if upper else (r >= c), 1.0, 0.0)
    return lax.dot_general(t, x, _DN["nn"], precision=lax.Precision.HIGHEST, preferred_element_type=F32)


@jax.custom_vjp
def _cumsum_rows(x):
    return _tri_dot(x, False)


_cumsum_rows.defvjp(lambda x: (_tri_dot(x, False), None), lambda _, g: (_tri_dot(g, True),))


def _ssd_chunk(xs_p, bm_g, cm_g, dtr, z_p, dtb, alog, dsk, nw_p, h_p):
    L = dtr.shape[0]
    n_pairs = len(xs_p)
    ppg = n_pairs // len(bm_g)
    lane = lax.broadcasted_iota(jnp.int32, (1, LANES), 1)
    sub = lax.broadcasted_iota(jnp.int32, (LANES, 1), 0)
    lo = lane < HD
    row_l = lax.broadcasted_iota(jnp.int32, (L, 1), 0)
    tri = lax.broadcasted_iota(jnp.int32, (L, L), 0) >= lax.broadcasted_iota(jnp.int32, (L, L), 1)

    dt = _softplus(dtr + dtb)
    acs = _cumsum_rows(dt * (-jnp.exp(alog)))
    acs_t = acs.T
    a_last = jnp.sum(jnp.where(row_l == L - 1, acs, 0.0), axis=0, keepdims=True)
    e_acs = jnp.exp(acs)
    dec = jnp.exp(a_last - acs)
    cdec = jnp.exp(a_last)

    def colv(m, h):
        return jnp.sum(jnp.where(lane == h, m, 0.0), axis=1, keepdims=True)

    def rowv(mt, h):
        return jnp.sum(jnp.where(sub == h, mt, 0.0), axis=0, keepdims=True)

    def pair(m, h0):
        return jnp.where(lo, colv(m, h0), colv(m, h0 + 1))

    ys, hs = [], []
    cb = None
    for p in range(n_pairs):
        g, h0 = p // ppg, 2 * p
        bmat, cmat = bm_g[g], cm_g[g]
        if p % ppg == 0:
            cb = _mm(cmat, bmat, "nt")
        x = xs_p[p]
        xdt = x * pair(dt, h0)
        yd = []
        for h in (h0, h0 + 1):
            seg = colv(acs, h) - rowv(acs_t, h)
            lm = jnp.where(tri, jnp.exp(jnp.where(tri, seg, 0.0)), 0.0)
            yd.append(_mm(cb * lm, xdt, "nn"))
        y = jnp.where(lo, yd[0], yd[1])
        y = y + _mm(cmat, h_p[p], "nt") * pair(e_acs, h0)
        st = _mm(xdt * pair(dec, h0), bmat, "tn")
        cd_col = jnp.where(sub < HD, colv(cdec, h0), colv(cdec, h0 + 1))
        hs.append(h_p[p] * cd_col + st)
        ys.append(y + pair(dsk, h0) * x)

    y2 = [ys[p] * _silu(z_p[p]) for p in range(n_pairs)]
    outs = []
    for g in range(len(bm_g)):
        ps = range(g * ppg, (g + 1) * ppg)
        ss = sum(jnp.sum(y2[p] * y2[p], axis=1, keepdims=True) for p in ps)
        rs = lax.rsqrt(ss * (1.0 / (ppg * LANES)) + EPS)
        outs += [y2[p] * rs * nw_p[p] for p in ps]
    return outs, hs


def _ssd_slices(xbc_ref, z_ref, nw_ref, di):
    n_pairs = di // LANES
    xs_p = [xbc_ref[:, p * LANES:(p + 1) * LANES] for p in range(n_pairs)]
    bm_g = [xbc_ref[:, di + g * NSTATE:di + (g + 1) * NSTATE] for g in range(NG)]
    cm_g = [xbc_ref[:, di + (NG + g) * NSTATE:di + (NG + g + 1) * NSTATE] for g in range(NG)]
    z_p = [z_ref[:, p * LANES:(p + 1) * LANES] for p in range(n_pairs)]
    nw_p = [nw_ref[:, p * LANES:(p + 1) * LANES] for p in range(n_pairs)]
    return xs_p, bm_g, cm_g, z_p, nw_p


def _ssd_fwd(xbc, proj, dt_cb, dtb, alog, dsk, nw, name):
    S, cc = xbc.shape
    di = NH_SSD * HD
    n_pairs = di // LANES
    nchunk = S // CHUNK

    def body(xbc_ref, z_ref, dtr_ref, dtb_ref, alog_ref, dsk_ref, nw_ref, y_ref, hs_ref, h_scr):
        @pl.when(pl.program_id(0) == 0)
        def _():
            h_scr[...] = jnp.zeros_like(h_scr)

        xs_p, bm_g, cm_g, z_p, nw_p = _ssd_slices(xbc_ref, z_ref, nw_ref, di)
        h_p = [h_scr[p * LANES:(p + 1) * LANES, :] for p in range(n_pairs)]
        hs_ref[...] = h_scr[...]
        outs, hs = _ssd_chunk(xs_p, bm_g, cm_g, dtr_ref[...], z_p, dtb_ref[...], alog_ref[...], dsk_ref[...], nw_p, h_p)
        for p in range(n_pairs):
            y_ref[:, p * LANES:(p + 1) * LANES] = outs[p].astype(y_ref.dtype)
            h_scr[p * LANES:(p + 1) * LANES, :] = hs[p]

    vec = pl.BlockSpec((1, LANES), lambda c: (0, 0))
    return pl.pallas_call(
        body, name=name, grid=(nchunk,),
        in_specs=[pl.BlockSpec((CHUNK, cc), lambda c: (c, 0)), pl.BlockSpec((CHUNK, di), lambda c: (c, 0)),
                  pl.BlockSpec((CHUNK, LANES), lambda c: (c, dt_cb)), vec, vec, vec,
                  pl.BlockSpec((1, di), lambda c: (0, 0))],
        out_specs=[pl.BlockSpec((CHUNK, di), lambda c: (c, 0)), pl.BlockSpec((None, di, NSTATE), lambda c: (c, 0, 0))],
        out_shape=[jax.ShapeDtypeStruct((S, di), BF16), jax.ShapeDtypeStruct((nchunk, di, NSTATE), F32)],
        scratch_shapes=[pltpu.VMEM((di, NSTATE), F32)],
        compiler_params=_cparams(("arbitrary",)),
    )(xbc, proj, proj, dtb, alog, dsk, nw)


def _ssd_bwd(xbc, proj, dt_cb, dtb, alog, dsk, nw, hsave, dy, name):
    S, cc = xbc.shape
    di = NH_SSD * HD
    n_pairs = di // LANES
    nchunk = S // CHUNK

    def body(xbc_ref, z_ref, dtr_ref, dtb_ref, alog_ref, dsk_ref, nw_ref, hs_ref, dy_ref,
             dxbc_ref, dz_ref, ddtr_ref, ddtb_ref, dalog_ref, ddsk_ref, dnw_ref, dh_scr):
        @pl.when(pl.program_id(0) == 0)
        def _():
            dh_scr[...] = jnp.zeros_like(dh_scr)
            ddtb_ref[...] = jnp.zeros_like(ddtb_ref)
            dalog_ref[...] = jnp.zeros_like(dalog_ref)
            ddsk_ref[...] = jnp.zeros_like(ddsk_ref)
            dnw_ref[...] = jnp.zeros_like(dnw_ref)

        xs_p, bm_g, cm_g, z_p, nw_p = _ssd_slices(xbc_ref, z_ref, nw_ref, di)
        h_p = [hs_ref[p * LANES:(p + 1) * LANES, :] for p in range(n_pairs)]
        dy_p = [dy_ref[:, p * LANES:(p + 1) * LANES].astype(F32) for p in range(n_pairs)]
        dh_p = [dh_scr[p * LANES:(p + 1) * LANES, :] for p in range(n_pairs)]
        _, vjp = jax.vjp(_ssd_chunk, xs_p, bm_g, cm_g, dtr_ref[...], z_p, dtb_ref[...], alog_ref[...], dsk_ref[...],
                         nw_p, h_p)
        dxs, dbm, dcm, ddtr, dz, ddtb, dalog, ddsk, dnw, dh = vjp((dy_p, dh_p))
        for p in range(n_pairs):
            sl = slice(p * LANES, (p + 1) * LANES)
            dxbc_ref[:, sl] = dxs[p]
            dz_ref[:, sl] = dz[p]
            dnw_ref[:, sl] += dnw[p]
            dh_scr[sl, :] = dh[p]
        for g in range(NG):
            dxbc_ref[:, di + g * NSTATE:di + (g + 1) * NSTATE] = dbm[g]
            dxbc_ref[:, di + (NG + g) * NSTATE:di + (NG + g + 1) * NSTATE] = dcm[g]
        ddtr_ref[...] = ddtr
        ddtb_ref[...] += ddtb
        dalog_ref[...] += dalog
        ddsk_ref[...] += ddsk

    last = nchunk - 1
    vec = pl.BlockSpec((1, LANES), lambda c: (0, 0))
    return pl.pallas_call(
        body, name=name, grid=(nchunk,),
        in_specs=[pl.BlockSpec((CHUNK, cc), lambda c: (last - c, 0)), pl.BlockSpec((CHUNK, di), lambda c: (last - c, 0)),
                  pl.BlockSpec((CHUNK, LANES), lambda c: (last - c, dt_cb)), vec, vec, vec,
                  pl.BlockSpec((1, di), lambda c: (0, 0)),
                  pl.BlockSpec((None, di, NSTATE), lambda c: (last - c, 0, 0)),
                  pl.BlockSpec((CHUNK, di), lambda c: (last - c, 0))],
        out_specs=[pl.BlockSpec((CHUNK, cc), lambda c: (last - c, 0)), pl.BlockSpec((CHUNK, di), lambda c: (last - c, 0)),
                   pl.BlockSpec((CHUNK, LANES), lambda c: (last - c, 0)), vec, vec, vec,
                   pl.BlockSpec((1, di), lambda c: (0, 0))],
        out_shape=[jax.ShapeDtypeStruct((S, cc), F32), jax.ShapeDtypeStruct((S, di), F32),
                   jax.ShapeDtypeStruct((S, LANES), F32), jax.ShapeDtypeStruct((1, LANES), F32),
                   jax.ShapeDtypeStruct((1, LANES), F32), jax.ShapeDtypeStruct((1, LANES), F32),
                   jax.ShapeDtypeStruct((1, di), F32)],
        scratch_shapes=[pltpu.VMEM((di, NSTATE), F32)],
        compiler_params=_cparams(("arbitrary",)),
    )(xbc, proj, proj, dtb, alog, dsk, nw, hsave, dy)


def _band_masks(rows_q, rows_k):
    qi = lax.broadcasted_iota(jnp.int32, (rows_q, rows_k), 0)
    ki = lax.broadcasted_iota(jnp.int32, (rows_q, rows_k), 1)
    return qi, ki


def _class_chunks(n_rows, d):
    per_class = n_rows // d
    ch = min(per_class, 256)
    out = []
    for r in range(d):
        for c0 in range(0, per_class, ch):
            tok = pl.ds(c0, ch) if d == 1 else pl.ds(r + d * c0, ch, stride=d)
            out.append((tok, pl.ds(r * per_class + c0, ch)))
    return out


def _to_class_order(src_ref, dst_ref, n_rows, d):
    for tok, cls in _class_chunks(n_rows, d):
        dst_ref[cls, :] = src_ref[tok, :].astype(dst_ref.dtype)


def _blk_rows(t):
    return pl.ds(pl.multiple_of(t * ABLK, ABLK), ABLK)


def _head_lanes(msk, t, t_rolled):
    return jnp.where(msk, t, t_rolled)


def _zero_unless(msk, t):
    return jnp.where(msk, t, jnp.zeros_like(t))


def _attn_fwd(qn, kn, proj, v_cb, name):
    S, ad = qn.shape
    nb = S // ABLK
    nbr = len(PATTERNS)

    def body(q_ref, k_ref, v_ref, o_ref, lse_ref, qc, kc, vc, ob, mb, lb, m_s, l_s):
        lo = _lane_mask()
        qi, ki = _band_masks(ABLK, 2 * ABLK)
        band, in_cur, prev_ok = ki <= qi + ABLK, ki >= ABLK, ki >= qi
        for bi, (_, d) in enumerate(PATTERNS):
            nbc = S // d // ABLK
            first, last = bi == 0, bi == nbr - 1
            qs, ks, vs = q_ref, k_ref, v_ref
            if d > 1:
                qs, ks, vs = qc, kc, vc
                for src, dst in ((q_ref, qc), (k_ref, kc), (v_ref, vc)):
                    _to_class_order(src, dst, S, d)
            o_dst, m_dst, l_dst = (o_ref, m_s, l_s) if first else (ob, mb, lb)

            def blk(t, carry, nbc=nbc, qs=qs, ks=ks, vs=vs, o_dst=o_dst, m_dst=m_dst, l_dst=l_dst):
                rows, prow = _blk_rows(t), _blk_rows(jnp.maximum(t - 1, 0))
                has_prev = (t % nbc) != 0
                kk = jnp.concatenate([ks[prow, :], ks[rows, :]], axis=0)
                vv = jnp.concatenate([vs[prow, :], vs[rows, :]], axis=0)
                for u in range(ABLK // QTILE):
                    sub = pl.ds(pl.multiple_of(t * ABLK + u * QTILE, QTILE), QTILE)
                    sl = slice(u * QTILE, (u + 1) * QTILE)
                    valid = band[sl] & (in_cur[sl] | (prev_ok[sl] & has_prev))
                    qv = qs[sub, :]
                    os_, ms_, ls_ = [], [], []
                    for msk in (lo, jnp.logical_not(lo)):
                        s = jnp.where(valid, _dot(_zero_unless(msk, qv), kk, "nt"), NEG)
                        m = jnp.max(s, axis=1, keepdims=True)
                        p = jnp.exp(s - m)
                        os_.append(_dot(p, vv, "nn"))
                        ms_.append(m)
                        ls_.append(jnp.sum(p, axis=1, keepdims=True))
                    o_dst[sub, :] = jnp.where(lo, os_[0], os_[1])
                    m_dst[sub, :] = jnp.where(lo, ms_[0], ms_[1])
                    l_dst[sub, :] = jnp.where(lo, ls_[0], ls_[1])
                return carry

            lax.fori_loop(0, nb, blk, 0, unroll=8)
            if first:
                continue
            for tok, cls in _class_chunks(S, d):
                m_old, m_b = m_s[tok, :], mb[cls, :]
                m_new = jnp.maximum(m_old, m_b)
                a, b = jnp.exp(m_old - m_new), jnp.exp(m_b - m_new)
                l_new = a * l_s[tok, :] + b * lb[cls, :]
                o_new = a * o_ref[tok, :] + b * ob[cls, :]
                if last:
                    o_ref[tok, :] = o_new / l_new
                    lse_ref[tok, :] = m_new + jnp.log(l_new)
                else:
                    o_ref[tok, :] = o_new
                    m_s[tok, :] = m_new
                    l_s[tok, :] = l_new

    col = pl.BlockSpec((S, LANES), lambda h: (0, h))
    return pl.pallas_call(
        body, name=name, grid=(ad // LANES,),
        in_specs=[col, col, pl.BlockSpec((S, LANES), lambda h: (0, h + v_cb))], out_specs=[col, col],
        out_shape=[jax.ShapeDtypeStruct((S, ad), F32), jax.ShapeDtypeStruct((S, ad), F32)],
        scratch_shapes=[pltpu.VMEM((S, LANES), BF16)] * 3 + [pltpu.VMEM((S, LANES), F32)] * 5,
        compiler_params=_cparams(("parallel",)),
    )(qn, kn, proj)


def _attn_bwd(qn, kn, proj, v_cb, do, lse, dd, name):
    S, ad = qn.shape
    nb = S // ABLK

    def body(q_ref, k_ref, v_ref, do_ref, lse_ref, dd_ref, dq_ref, dk_ref, dv_ref,
             qc, kc, vc, doc, lsec, ddc, dqc, dkc, dvc):
        lo = _lane_mask()
        qi, ki = _band_masks(ABLK, ABLK)
        cur_ok, prev_ok = ki <= qi, ki >= qi
        for bi, (_, d) in enumerate(PATTERNS):
            nbc = S // d // ABLK
            first = bi == 0
            token_order = (q_ref, k_ref, v_ref, do_ref, lse_ref, dd_ref)
            class_order = (qc, kc, vc, doc, lsec, ddc)
            if d > 1:
                for src, dst in zip(token_order, class_order):
                    _to_class_order(src, dst, S, d)
            qs, ks, vs, dos, lses, dds = class_order if d > 1 else token_order
            dq_dst, dk_dst, dv_dst = (dq_ref, dk_ref, dv_ref) if first else (dqc, dkc, dvc)
            dk_dst[...] = jnp.zeros_like(dk_dst)
            dv_dst[...] = jnp.zeros_like(dv_dst)

            def blk(t, carry, nbc=nbc, qs=qs, ks=ks, vs=vs, dos=dos, lses=lses, dds=dds,
                    dq_dst=dq_dst, dk_dst=dk_dst, dv_dst=dv_dst):
                rows, prow = _blk_rows(t), _blk_rows(jnp.maximum(t - 1, 0))
                has_prev = (t % nbc) != 0
                qv, dov, lse_b, dd_b = qs[rows, :], dos[rows, :], lses[rows, :], dds[rows, :]
                lse_r, dd_r = pltpu.roll(lse_b, HD, 1), pltpu.roll(dd_b, HD, 1)
                heads = []
                for msk in (lo, jnp.logical_not(lo)):
                    heads.append((_zero_unless(msk, qv), _zero_unless(msk, dov), _head_lanes(msk, lse_b, lse_r),
                                  _head_lanes(msk, dd_b, dd_r)))
                dqs = [None, None]
                for krows, vmask in ((rows, cur_ok), (prow, prev_ok & has_prev)):
                    kv, vv = ks[krows, :], vs[krows, :]
                    dk = jnp.zeros((ABLK, LANES), F32)
                    dv = jnp.zeros((ABLK, LANES), F32)
                    for hi, (qh, doh, lse_h, dd_h) in enumerate(heads):
                        s = jnp.where(vmask, _dot(qh, kv, "nt"), NEG)
                        p = jnp.exp(s - lse_h)
                        ds = p * (_dot(doh, vv, "nt") - dd_h)
                        dqh = _dot(ds, kv, "nn")
                        dqs[hi] = dqh if dqs[hi] is None else dqs[hi] + dqh
                        dv = dv + _dot(p, doh, "tn")
                        dk = dk + _dot(ds, qh, "tn")
                    dk_dst[krows, :] += dk
                    dv_dst[krows, :] += dv
                dq_dst[rows, :] = jnp.where(lo, dqs[0], dqs[1])
                return carry

            lax.fori_loop(0, nb, blk, 0, unroll=4)
            if not first:
                for tok, cls in _class_chunks(S, d):
                    dq_ref[tok, :] = dq_ref[tok, :] + dqc[cls, :]
                    dk_ref[tok, :] = dk_ref[tok, :] + dkc[cls, :]
                    dv_ref[tok, :] = dv_ref[tok, :] + dvc[cls, :]

    col = pl.BlockSpec((S, LANES), lambda h: (0, h))
    col1 = pl.BlockSpec((S, LANES), lambda h: (0, h), pipeline_mode=pl.Buffered(1))
    vcol1 = pl.BlockSpec((S, LANES), lambda h: (0, h + v_cb), pipeline_mode=pl.Buffered(1))
    return pl.pallas_call(
        body, name=name, grid=(ad // LANES,),
        in_specs=[col, col, vcol1, col1, col1, col1], out_specs=[col, col, col],
        out_shape=[jax.ShapeDtypeStruct((S, ad), F32)] * 3,
        scratch_shapes=[pltpu.VMEM((S, LANES), BF16)] * 4 + [pltpu.VMEM((S, LANES), F32)] * 5,
        compiler_params=_cparams(("parallel",)),
    )(qn, kn, proj, do, lse, dd)


def _coords():
    return lax.axis_index("x"), lax.axis_index("y"), lax.axis_index("c")


def _exchange8(xs, per_dest, name):
    n = len(xs)
    blk = [x.shape[1:] if per_dest else x.shape for x in xs]

    def body(*refs):
        ins, outs = refs[:n], refs[n:2 * n]
        send_sems, recv_sems, local_sems = refs[2 * n:]
        x, y, c = _coords()
        sibling = (x, y, 1 - c)
        chips = [(1 - x, y), (x, 1 - y), (1 - x, 1 - y)]
        first, passed, mine = [], [], []
        for a in range(n):
            def src_for(cx, cy, a=a):
                return ins[a].at[2 * cx + cy] if per_dest else ins[a]

            def slot(px, py, pc, a=a):
                return outs[a].at[4 * px + 2 * py + pc]

            def copy(k, src, dst, to, a=a):
                return pltpu.make_async_remote_copy(src_ref=src, dst_ref=dst, send_sem=send_sems.at[7 * a + k],
                                                    recv_sem=recv_sems.at[7 * a + k], device_id=to, device_id_type=MESH)

            m = pltpu.make_async_copy(src_for(x, y), slot(x, y, c), local_sems.at[a])
            m.start()
            mine.append(m)
            cps = [copy(0, src_for(x, y), slot(x, y, c), sibling)]
            cps += [copy(1 + j, src_for(*chip), slot(x, y, c), (*chip, c)) for j, chip in enumerate(chips)]
            for cp in cps:
                cp.start()
            first += cps
        for a in range(n):
            def slot(px, py, pc, a=a):
                return outs[a].at[4 * px + 2 * py + pc]

            def copy(k, src, dst, to, a=a):
                return pltpu.make_async_remote_copy(src_ref=src, dst_ref=dst, send_sem=send_sems.at[7 * a + k],
                                                    recv_sem=recv_sems.at[7 * a + k], device_id=to, device_id_type=MESH)

            for j, chip in enumerate(chips):
                copy(1 + j, slot(*chip, c), slot(*chip, c), (*chip, c)).wait_recv()
                fw = copy(4 + j, slot(*chip, c), slot(*chip, c), sibling)
                fw.start()
                passed.append(fw)
        for a in range(n):
            def slot(px, py, pc, a=a):
                return outs[a].at[4 * px + 2 * py + pc]

            def copy(k, src, dst, to, a=a):
                return pltpu.make_async_remote_copy(src_ref=src, dst_ref=dst, send_sem=send_sems.at[7 * a + k],
                                                    recv_sem=recv_sems.at[7 * a + k], device_id=to, device_id_type=MESH)

            copy(0, slot(x, y, 1 - c), slot(x, y, 1 - c), sibling).wait_recv()
            for j, chip in enumerate(chips):
                copy(4 + j, slot(*chip, 1 - c), slot(*chip, 1 - c), sibling).wait_recv()
        for cp in first + passed:
            cp.wait_send()
        for m in mine:
            m.wait()

    anyspec = pl.BlockSpec(memory_space=pl.ANY)
    res = pl.pallas_call(
        body, name=name, in_specs=[anyspec] * n, out_specs=[anyspec] * n,
        out_shape=[jax.ShapeDtypeStruct((8,) + tuple(b), x.dtype) for b, x in zip(blk, xs)],
        scratch_shapes=[pltpu.SemaphoreType.DMA((7 * n,)), pltpu.SemaphoreType.DMA((7 * n,)),
                        pltpu.SemaphoreType.DMA((n,))],
    )(*xs)
    return list(res)


def _pair_swap(xs, name):
    n = len(xs)

    def body(*refs):
        ins, outs = refs[:n], refs[n:2 * n]
        send_sems, recv_sems = refs[2 * n:]
        x, y, c = _coords()
        cps = [pltpu.make_async_remote_copy(src_ref=ins[a].at[1 - c], dst_ref=outs[a], send_sem=send_sems.at[a],
                                            recv_sem=recv_sems.at[a], device_id=(x, y, 1 - c), device_id_type=MESH)
               for a in range(n)]
        for cp in cps:
            cp.start()
        for cp in cps:
            cp.wait()

    anyspec = pl.BlockSpec(memory_space=pl.ANY)
    res = pl.pallas_call(
        body, name=name, in_specs=[anyspec] * n, out_specs=[anyspec] * n,
        out_shape=[jax.ShapeDtypeStruct(x.shape[1:], x.dtype) for x in xs],
        scratch_shapes=[pltpu.SemaphoreType.DMA((n,)), pltpu.SemaphoreType.DMA((n,))],
    )(*xs)
    return list(res)


_HBM = pl.BlockSpec(memory_space=pltpu.HBM)
_SEM = pl.BlockSpec(memory_space=pltpu.SEMAPHORE)
_EFFECT = pltpu.SideEffectType.DATAFLOW_SIDE_EFFECTING


N_PEER = 3


def _peer(x, y, j):
    dx, dy = (j + 1) >> 1 & 1, (j + 1) & 1
    return (1 - x if dx else x, 1 - y if dy else y)


def _spread_copies(s_refs, l_refs, send_sems, recv_sems, per_dest):
    x, y, c = _coords()
    me = 4 * x + 2 * y + c
    cps = []
    for a in range(len(s_refs)):
        for j in range(N_PEER):
            tx, ty = _peer(x, y, j)
            src = s_refs[a].at[2 * tx + ty] if per_dest else s_refs[a]
            cps.append(pltpu.make_async_remote_copy(src_ref=src, dst_ref=l_refs[a].at[me],
                                                    send_sem=send_sems.at[N_PEER * a + j],
                                                    recv_sem=recv_sems.at[N_PEER * a + j], device_id=(tx, ty, c),
                                                    device_id_type=MESH))
    return cps


def _sibling_fill(lands, name):
    n = len(lands)

    def body(*refs):
        outs, send_sems, recv_sems = refs[n:2 * n], refs[2 * n], refs[2 * n + 1]
        x, y, c = _coords()
        cps = [pltpu.make_async_remote_copy(src_ref=outs[a].at[2 * k + c], dst_ref=outs[a].at[2 * k + c],
                                            send_sem=send_sems.at[4 * a + k], recv_sem=recv_sems.at[4 * a + k],
                                            device_id=(x, y, 1 - c), device_id_type=MESH)
               for a in range(n) for k in range(4)]
        for cp in cps:
            cp.start()
        for cp in cps:
            cp.wait()

    anyspec = pl.BlockSpec(memory_space=pl.ANY)
    res = pl.pallas_call(
        body, name=name, in_specs=[anyspec] * n, out_specs=[anyspec] * n,
        out_shape=[jax.ShapeDtypeStruct(t.shape, t.dtype) for t in lands], input_output_aliases={i: i for i in range(n)},
        scratch_shapes=[pltpu.SemaphoreType.DMA((4 * n,)), pltpu.SemaphoreType.DMA((4 * n,))],
    )(*lands)
    return list(res)


def _spread_start(srcs, per_dest, dev, chip, name):
    n = len(srcs)
    lands = []
    for s in srcs:
        own = lax.dynamic_index_in_dim(s, chip, 0, keepdims=False) if per_dest else s
        lands.append(lax.dynamic_update_index_in_dim(lax.empty((8,) + own.shape, own.dtype), own, dev, 0))

    def body(*refs):
        s_refs, l_refs, send_sems, recv_sems, token = refs[:n], refs[n:2 * n], refs[2 * n], refs[2 * n + 1], refs[-1]
        for cp in _spread_copies(s_refs, l_refs, send_sems, recv_sems, per_dest):
            cp.start()
        token[...] = jnp.zeros_like(token)

    hbm_in = [pltpu.with_memory_space_constraint(t, pltpu.HBM) for t in list(srcs) + lands]
    outs = pl.pallas_call(
        body, name=name,
        out_shape=(pltpu.SemaphoreType.DMA((N_PEER * n,)), pltpu.SemaphoreType.DMA((N_PEER * n,)),
                   *[pltpu.HBM(t.shape, t.dtype) for t in hbm_in], jax.ShapeDtypeStruct((8, LANES), F32)),
        in_specs=[_HBM] * (2 * n), out_specs=(_SEM, _SEM, *[_HBM] * (2 * n), pl.BlockSpec(memory_space=pltpu.VMEM)),
        input_output_aliases={i: 2 + i for i in range(2 * n)},
        compiler_params=pltpu.CompilerParams(has_side_effects=_EFFECT),
    )(*hbm_in)
    return (outs[0], outs[1], list(outs[2:2 + n]), list(outs[2 + n:2 + 2 * n])), outs[-1]


def _spread_wait(handle, per_dest, after, name):
    send_sems, recv_sems, srcs, lands = handle
    n = len(srcs)

    def body(*refs):
        s_refs, l_refs, send_ref, recv_ref = refs[:n], refs[n:2 * n], refs[2 * n], refs[2 * n + 1]
        for cp in _spread_copies(s_refs, l_refs, send_ref, recv_ref, per_dest):
            cp.wait_send()
            cp.wait_recv()

    outs = pl.pallas_call(
        body, name=name, out_shape=tuple(pltpu.HBM(t.shape, t.dtype) for t in srcs + lands),
        in_specs=[_HBM] * (2 * n) + [_SEM, _SEM, pl.BlockSpec(memory_space=pl.ANY)], out_specs=tuple([_HBM] * (2 * n)),
        input_output_aliases={i: i for i in range(2 * n)},
        compiler_params=pltpu.CompilerParams(has_side_effects=_EFFECT),
    )(*srcs, *lands, send_sems, recv_sems, after)
    return list(outs[n:])


def _row_tile(n, cap, mult):
    best = n
    for t in range(mult, min(n, cap) + 1, mult):
        if n % t == 0:
            best = t
    return best


def _pair_add(g2, theirs, half, name):
    _, n, cdim = g2.shape
    tm = _row_tile(n, 512, 16)

    def body(h_ref, a_ref, b_ref, o_ref):
        o_ref[...] = (a_ref[...] + b_ref[...]).astype(o_ref.dtype)

    grid_spec = pltpu.PrefetchScalarGridSpec(
        num_scalar_prefetch=1, grid=(n // tm,),
        in_specs=[pl.BlockSpec((None, tm, cdim), lambda i, h: (h[0], i, 0)), pl.BlockSpec((tm, cdim), lambda i, h: (i, 0))],
        out_specs=pl.BlockSpec((tm, cdim), lambda i, h: (i, 0)))
    return pl.pallas_call(body, name=name, grid_spec=grid_spec, out_shape=jax.ShapeDtypeStruct((n, cdim), BF16),
                          compiler_params=_cparams(("parallel",)))(half.reshape(1).astype(jnp.int32), g2, theirs)


def _adamw_math(w, g, m, v):
    m = ADAM_B1 * m + (1.0 - ADAM_B1) * g
    v = ADAM_B2 * v + (1.0 - ADAM_B2) * (g * g)
    m_hat = m / (1.0 - ADAM_B1 ** ADAM_STEP)
    v_hat = v / (1.0 - ADAM_B2 ** ADAM_STEP)
    delta = -ADAM_LR * (m_hat / (jnp.sqrt(v_hat) + ADAM_EPS) + ADAM_WD * w)
    return delta, m, v


def _adamw(parts, w, m, v, name, tm=128):
    npart, R, C = parts.shape
    tm = min(tm, R)

    def body(p_ref, w_ref, m_ref, v_ref, g_out, d_out, m_out, v_out):
        g = p_ref[0].astype(F32)
        for i in range(1, npart):
            g = g + p_ref[i].astype(F32)
        d, mm, vv = _adamw_math(w_ref[...], g, m_ref[...], v_ref[...])
        g_out[...] = g
        d_out[...] = d
        m_out[...] = mm
        v_out[...] = vv

    spec = pl.BlockSpec((tm, C), lambda i: (i, 0))
    return pl.pallas_call(
        body, name=name, grid=(R // tm,),
        in_specs=[pl.BlockSpec((npart, tm, C), lambda i: (0, i, 0)), spec, spec, spec], out_specs=[spec] * 4,
        out_shape=[jax.ShapeDtypeStruct((R, C), F32)] * 4,
        compiler_params=_cparams(("parallel",)),
    )(parts, w, m, v)


def _sum_parts(parts, name):
    npart, R, C = parts.shape

    def body(p_ref, o_ref):
        g = p_ref[0]
        for i in range(1, npart):
            g = g + p_ref[i]
        o_ref[...] = g

    return pl.pallas_call(body, name=name, out_shape=jax.ShapeDtypeStruct((R, C), F32))(parts)


def _mod_fwd(c_all, w_ada, b_sh, name):
    def body(c_ref, w_ref, b_ref, o_ref):
        o_ref[...] = _dot(_silu(c_ref[...]), w_ref[...], "nn") + b_ref[...]

    return pl.pallas_call(body, name=name, out_shape=jax.ShapeDtypeStruct((c_all.shape[0], w_ada.shape[1]), F32),
                          compiler_params=pltpu.CompilerParams(vmem_limit_bytes=VMEM_LIMIT))(c_all, w_ada, b_sh)


def _mod_wgrad(c_all, dmod_sh, name):
    def body(c_ref, d_ref, o_ref):
        o_ref[...] = _dot(_silu(c_ref[...]), d_ref[...], "tn")

    return pl.pallas_call(body, name=name, out_shape=jax.ShapeDtypeStruct((c_all.shape[1], dmod_sh.shape[1]), F32),
                          compiler_params=pltpu.CompilerParams(vmem_limit_bytes=VMEM_LIMIT))(c_all, dmod_sh)


def _pad_lanes(v):
    return jnp.pad(v, ((0, 0), (0, (-v.shape[1]) % LANES)))


def kernel(x, c, norm1_w, norm2_w, w_ada, b_ada, w_in, conv_w, conv_b, dt_bias, a_log, d_skip, ssd_norm_w, q_norm_w, k_norm_w, attn_norm_w, w_out, w_ff1, w_ff2, loss_target, m_norm1_w, m_norm2_w, m_w_ada, m_b_ada, m_w_in, m_conv_w, m_conv_b, m_dt_bias, m_a_log, m_d_skip, m_ssd_norm_w, m_q_norm_w, m_k_norm_w, m_attn_norm_w, m_w_out, m_w_ff1, m_w_ff2, v_norm1_w, v_norm2_w, v_w_ada, v_b_ada, v_w_in, v_conv_w, v_conv_b, v_dt_bias, v_a_log, v_d_skip, v_ssd_norm_w, v_q_norm_w, v_k_norm_w, v_attn_norm_w, v_w_out, v_w_ff1, v_w_ff2):
    xi, yi, ci = _coords()
    chip = 2 * xi + yi
    dev = 2 * chip + ci
    xs, tgt = x[0], loss_target[0]
    S, D = xs.shape
    DI, AD = NH_SSD * HD, NH_ATT * HD
    CC = DI + 2 * NG * NSTATE
    PW = DI + CC + 3 * AD + LANES
    DFF = w_ff1.shape[2] * 4
    MIX = DI + AD
    o_xbc, o_q, o_k, o_v, o_dt = DI, DI + CC, DI + CC + AD, DI + CC + 2 * AD, DI + CC + 3 * AD

    def half_rows(w):
        r = w.shape[0] // 2
        return lax.dynamic_slice_in_dim(w, ci * r, r, 0).astype(BF16)

    c_all, conv_w_all = _exchange8([c, conv_w[0]], False, "gather_c_conv_w")
    c_all = c_all.reshape(8, D)
    c_all = jnp.pad(c_all, ((0, 8), (0, 0)))
    nmod = w_ada.shape[2]
    b_sh = lax.dynamic_slice_in_dim(b_ada, chip * nmod, nmod, 1)
    mod_sh = _mod_fwd(c_all, w_ada[0], b_sh, "mod_fwd")
    mod_all = _exchange8([mod_sh[:8]], False, "gather_mod")[0]
    mod_me = lax.dynamic_index_in_dim(mod_all[0::2], dev, 1, keepdims=False).reshape(1, 4 * nmod)
    shift1, scale1, gate1, shift2, scale2, gate2 = [mod_me[:, i * D:(i + 1) * D] for i in range(6)]

    w_in_t, m_in_t, v_in_t = w_in[0].T, m_w_in[0].T, v_w_in[0].T
    g_in = _exchange8([lax.dynamic_slice_in_dim(w_in_t, ci * (D // 2), D // 2, 1).astype(BF16)], False, "gather_w_in")[0]
    rest_handle, rest_token = _spread_start([half_rows(w_out[0]), half_rows(w_ff1[0]), half_rows(w_ff2[0])], False, dev,
                                            chip, "gather_rest_start")
    shift1 = shift1 + rest_token[0, 0]
    wsh = w_in.shape[2]
    w_in_ft = g_in.reshape(4, 2, wsh, D // 2).transpose(0, 2, 1, 3).reshape(4 * wsh, D)
    n_zx = DI + CC
    w_proj_t = jnp.concatenate([w_in_ft[:n_zx], w_in_ft[n_zx + NH_SSD:], w_in_ft[n_zx:n_zx + NH_SSD],
                                jnp.zeros((LANES - NH_SSD, D), BF16)], axis=0)

    dtb, alog, dsk = _pad_lanes(dt_bias), _pad_lanes(a_log), _pad_lanes(d_skip)
    qw2 = jnp.concatenate([q_norm_w, q_norm_w], axis=1)
    kw2 = jnp.concatenate([k_norm_w, k_norm_w], axis=1)
    conv_w_f = conv_w_all[0::2].transpose(1, 0, 2).reshape(KCONV, CC)

    h1 = _rows("norm1", lambda r, k: ([_normmod(r[0], *k)], []), [(xs, 0, D)], [norm1_w, scale1, shift1],
               [(D, BF16)], [], S)[0]
    proj = _matmul(h1, w_proj_t, "nt", F32, "in_proj", tn=896)
    xbc = _conv_fwd(proj, o_xbc, CC, conv_w_f, conv_b, "conv_fwd")
    y_ssd, hsave = _ssd_fwd(xbc, proj, o_dt // LANES, dtb, alog, dsk, ssd_norm_w, "ssd_fwd")

    def qk_call(name, col0, w2, scale):
        def body(t_ref, w_ref, o_ref):
            o_ref[...] = _headnorm(t_ref[...], w_ref[...], scale)
        return pl.pallas_call(
            body, name=name, grid=(AD // LANES,),
            in_specs=[pl.BlockSpec((S, LANES), lambda j: (0, j + col0 // LANES)),
                      pl.BlockSpec((1, LANES), lambda j: (0, 0))],
            out_specs=pl.BlockSpec((S, LANES), lambda j: (0, j)),
            out_shape=jax.ShapeDtypeStruct((S, AD), F32), compiler_params=_cparams(("parallel",)),
        )(proj, w2)

    qn = qk_call("q_norm", o_q, qw2, HD ** -0.5)
    kn = qk_call("k_norm", o_k, kw2, 1.0)
    o_att, lse = _attn_fwd(qn, kn, proj, o_v // LANES, "attn_fwd")
    y_att = _rows("attn_out_norm", lambda r, k: ([_rmsw(r[0], k[0])], []), [(o_att, 0, AD)], [attn_norm_w],
                  [(AD, BF16)], [], S)[0]
    g_out, g_ff1, g_ff2 = _sibling_fill(_spread_wait(rest_handle, False, o_att, "gather_rest_wait"), "gather_rest_fill")
    w_out_f = g_out.reshape(MIX, D)
    w_out_a, w_out_b = w_out_f[:DI], w_out_f[DI:]
    w_ff1_f = g_ff1.reshape(4, D, DFF // 4).transpose(1, 0, 2).reshape(D, DFF)
    w_ff2_f = g_ff2.reshape(DFF, D)
    mix_a = _matmul(y_ssd, w_out_a, "nn", F32, "out_proj_a")
    mix = _matmul(y_att, w_out_b, "nn", F32, "out_proj_b", epilogue=lambda r, e: r + e, extras=(mix_a,))
    x2, h2 = _rows("resid_norm2", lambda r, k: (list(_resid_normmod(r[0], r[1], *k)), []), [(xs, 0, D), (mix, 0, D)],
                   [gate1, norm2_w, scale2, shift2], [(D, F32), (D, BF16)], [], S)
    u = _matmul(h2, w_ff1_f, "nn", F32, "ff1")
    relu2 = lambda t: jnp.square(jnp.maximum(t, 0.0))
    ff = _matmul(u, w_ff2_f, "nn", F32, "ff2", a_fn=relu2)

    def loss_fn(r, k):
        x2_, ff_, t_ = r
        err = x2_ + k[0] * ff_ - t_
        dy_ = err * (1.0 / D)
        ls = jnp.sum(jnp.sum(0.5 * err * err, axis=1, keepdims=True), axis=0, keepdims=True) * (1.0 / D)
        return [dy_, dy_ * k[0]], [ls, jnp.sum(dy_ * ff_, axis=0, keepdims=True)]

    dy, dff, loss_p, dgate2 = _rows("loss", loss_fn, [(x2, 0, D), (ff, 0, D), (tgt, 0, D)], [gate2],
                                    [(D, F32), (D, BF16)], [(1, 1), (1, D)], S)
    du = _matmul(dff, w_ff2_f, "nt", BF16, "ff2_dx", epilogue=lambda r, e: r * (2.0 * jnp.maximum(e, 0.0)), extras=(u,))
    gw_ff2 = _matmul(u, dff, "tn", F32, "ff2_dw", a_fn=relu2)
    gw_ff1 = _matmul(h2, du, "tn", F32, "ff1_dw")

    def by_half_cols(g):
        r, c4 = g.shape
        return g.reshape(2, r // 2, 4, c4 // 4).transpose(0, 2, 1, 3)

    def by_half_rows(g):
        r4, cdim = g.shape
        return g.reshape(4, 2, r4 // 8, cdim).transpose(1, 0, 2, 3)

    def scatter_start(layouts, tag):
        theirs = _pair_swap(layouts, "pair_swap_" + tag)
        sums = []
        for i, (g2, t) in enumerate(zip(layouts, theirs)):
            _, r2, cdim = t.shape
            sm = _pair_add(g2.reshape(2, 4 * r2, cdim), t.reshape(4 * r2, cdim), ci, "pair_add_%s_%d" % (tag, i))
            sums.append(sm.reshape(4, r2, cdim))
        return _spread_start(sums, True, dev, chip, "scatter_%s_start" % tag)

    def scatter_wait(handle, after, tag):
        lands = _sibling_fill(_spread_wait(handle, True, after, "scatter_%s_wait" % tag), "scatter_%s_fill" % tag)
        return [s.reshape(4, 2 * s.shape[1], s.shape[2]) for s in lands]

    ff_handle, ff_token = scatter_start([by_half_cols(gw_ff1), by_half_rows(gw_ff2)], "ff")
    dh2 = _matmul(du, w_ff1_f, "nt", F32, "ff1_dx")

    def resid_bwd(r, k):
        x_, mix_, dx2a, dh2_ = r
        _, vjp = jax.vjp(_resid_normmod, x_, mix_, *k)
        dx, dmix_, dg, dnw, dsc, dsh = vjp((dx2a, dh2_))
        return [dx, dmix_], [dg, dnw, dsc, dsh]

    dx2, dmix, dgate1, g_norm2, dscale2, dshift2 = _rows(
        "resid_norm2_bwd", resid_bwd, [(xs, 0, D), (mix, 0, D), (dy, 0, D), (dh2, 0, D)],
        [gate1 + ff_token[0, 0], norm2_w, scale2, shift2], [(D, F32), (D, BF16)], [(1, D)] * 4, S)
    gw_out = jnp.concatenate([_matmul(y_ssd, dmix, "tn", F32, "out_proj_dw_a"),
                              _matmul(y_att, dmix, "tn", F32, "out_proj_dw_b")], axis=0)
    out_handle, out_token = scatter_start([by_half_rows(gw_out)], "out")
    dy_ssd = _matmul(dmix, w_out_a, "nt", F32, "out_proj_dx_a")
    dy_att = _matmul(dmix, w_out_b, "nt", F32, "out_proj_dx_b")

    def attn_norm_bwd(r, k):
        o_, dyo = r
        _, vjp = jax.vjp(_rmsw, o_, k[0])
        do_, dw_ = vjp(dyo)
        lo = _lane_mask()
        dd_blocks = []
        for b in range(AD // LANES):
            t = (do_ * o_)[:, b * LANES:(b + 1) * LANES]
            s0 = jnp.sum(jnp.where(lo, t, 0.0), axis=1, keepdims=True)
            s1 = jnp.sum(jnp.where(lo, 0.0, t), axis=1, keepdims=True)
            dd_blocks.append(jnp.where(lo, s0, s1))
        return [do_, jnp.concatenate(dd_blocks, axis=1)], [dw_]

    do_att, dd_att, g_attn_norm = _rows("attn_norm_bwd", attn_norm_bwd, [(o_att, 0, AD), (dy_att, 0, AD)],
                                        [attn_norm_w + out_token[0, 0]], [(AD, F32), (AD, F32)], [(1, AD)], S)
    dq_n, dk_n, dv = _attn_bwd(qn, kn, proj, o_v // LANES, do_att, lse, dd_att, "attn_bwd")

    def qk_bwd_call(name, col0, w2, scale, g):
        def body(t_ref, w_ref, g_ref, o_ref, dw_ref):
            @pl.when(pl.program_id(0) == 0)
            def _():
                dw_ref[...] = jnp.zeros_like(dw_ref)
            _, vjp = jax.vjp(lambda t, w: _headnorm(t, w, scale), t_ref[...], w_ref[...])
            dt_, dw_ = vjp(g_ref[...])
            o_ref[...] = dt_.astype(BF16)
            dw_ref[...] += dw_
        blk = pl.BlockSpec((S, LANES), lambda j: (0, j))
        return pl.pallas_call(
            body, name=name, grid=(AD // LANES,),
            in_specs=[pl.BlockSpec((S, LANES), lambda j: (0, j + col0 // LANES)),
                      pl.BlockSpec((1, LANES), lambda j: (0, 0)), blk],
            out_specs=[blk, pl.BlockSpec((1, LANES), lambda j: (0, 0))],
            out_shape=[jax.ShapeDtypeStruct((S, AD), BF16), jax.ShapeDtypeStruct((1, LANES), F32)],
            compiler_params=_cparams(("arbitrary",)),
        )(proj, w2, g)

    dq, g_qw2 = qk_bwd_call("q_norm_bwd", o_q, qw2, HD ** -0.5, dq_n)
    dk, g_kw2 = qk_bwd_call("k_norm_bwd", o_k, kw2, 1.0, dk_n)
    g_q_norm = g_qw2[:, :HD] + g_qw2[:, HD:]
    g_k_norm = g_kw2[:, :HD] + g_kw2[:, HD:]

    dxbc, dz, ddtr, g_dtb, g_alog, g_dsk, g_ssd_norm = _ssd_bwd(
        xbc, proj, o_dt // LANES, dtb, alog, dsk, ssd_norm_w, hsave, dy_ssd, "ssd_bwd")
    dxbc_pre, g_conv_w, g_conv_b = _conv_bwd(proj, o_xbc, CC, conv_w_f, conv_b, dxbc, "conv_bwd")
    dproj = jnp.concatenate([dz.astype(BF16), dxbc_pre.astype(BF16), dq, dk, dv.astype(BF16), ddtr.astype(BF16)], axis=1)
    gw_proj_t = _matmul(dproj, h1, "tn", F32, "in_proj_dw", tm=896)
    gw_in_t = jnp.concatenate([gw_proj_t[:n_zx], gw_proj_t[o_dt:o_dt + NH_SSD], gw_proj_t[n_zx:o_dt]], axis=0)
    in_handle, in_token = scatter_start([gw_in_t.reshape(4, wsh, 2, D // 2).transpose(2, 0, 1, 3)], "in")
    dh1 = _matmul(dproj, w_proj_t, "nn", F32, "in_proj_dx", tk=896)

    def norm1_bwd(r, k):
        x_, dh_, dres = r
        _, vjp = jax.vjp(_normmod, x_, *k)
        dx, dnw, dsc, dsh = vjp(dh_)
        return [dx + dres], [dnw, dsc, dsh]

    grad_x, g_norm1, dscale1, dshift1 = _rows("norm1_bwd", norm1_bwd, [(xs, 0, D), (dh1, 0, D), (dx2, 0, D)],
                                              [norm1_w + in_token[0, 0], scale1, shift1], [(D, F32)], [(1, D)] * 3, S)
    dmod =jnp.concatenate([dshift1, dscale1, dgate1, dshift2, dscale2, dgate2], axis=1)

    small = [g_norm1, g_norm2, dmod, g_conv_b, g_dtb, g_alog, g_dsk, g_ssd_norm, _pad_lanes(g_q_norm),
             _pad_lanes(g_k_norm), g_attn_norm, g_conv_w.reshape(1, KCONV * CC)]
    sizes = [t.shape[1] for t in small]
    packed = jnp.concatenate(small, axis=1)
    nrow = -(-packed.shape[1] // LANES // 8) * 8
    packed = jnp.pad(packed, ((0, 0), (0, nrow * LANES - packed.shape[1]))).reshape(nrow, LANES)
    packed_all = _exchange8([packed], False, "gather_small_grads")[0]
    tot = _sum_parts(packed_all, "sum_small_grads").reshape(1, nrow * LANES)
    offs = [sum(sizes[:i]) for i in range(len(sizes))]
    (g_norm1, g_norm2, g_b_ada, g_conv_b, g_dtb, g_alog, g_dsk, g_ssd_norm, g_q_norm, g_k_norm, g_attn_norm,
     g_conv_w) = [tot[:, o:o + n] for o, n in zip(offs, sizes)]
    g_dtb, g_alog, g_dsk = g_dtb[:, :NH_SSD], g_alog[:, :NH_SSD], g_dsk[:, :NH_SSD]
    g_q_norm, g_k_norm = g_q_norm[:, :HD], g_k_norm[:, :HD]
    ccs = CC // 4
    g_conv_w = lax.dynamic_slice_in_dim(g_conv_w.reshape(KCONV, CC), chip * ccs, ccs, 1)

    dmod_all = packed_all.reshape(8, nrow * LANES)[:, offs[2]:offs[2] + 6 * D]
    dmod_sh = jnp.pad(lax.dynamic_slice_in_dim(dmod_all, chip * nmod, nmod, 1), ((0, 8), (0, 0)))
    gw_ada = _mod_wgrad(c_all, dmod_sh, "mod_wgrad")

    parts_ff1, parts_ff2 = scatter_wait(ff_handle, in_token, "ff")
    res_ff1 = _adamw(parts_ff1, w_ff1[0], m_w_ff1[0], v_w_ff1[0], "adamw_w_ff1")
    res_ff2 = _adamw(parts_ff2, w_ff2[0], m_w_ff2[0], v_w_ff2[0], "adamw_w_ff2")
    res_out = _adamw(scatter_wait(out_handle, in_token, "out")[0], w_out[0], m_w_out[0], v_w_out[0], "adamw_w_out")
    res_ada = _adamw(gw_ada[None], w_ada[0], m_w_ada[0], v_w_ada[0], "adamw_w_ada")
    nlt = D // LANES
    parts_in = scatter_wait(in_handle, res_ada[0], "in")[0].reshape(4, 2, wsh, nlt // 2, LANES)
    parts_in = parts_in.transpose(0, 2, 1, 3, 4).reshape(4, wsh * nlt, LANES)
    res_in = _adamw(parts_in, *[t.reshape(wsh * nlt, LANES) for t in (w_in_t, m_in_t, v_in_t)], "adamw_w_in",
                    tm=_row_tile(wsh * nlt, 2048, 16))
    res_in = [t.reshape(wsh, D).T for t in res_in]

    small_names = ["norm1_w", "norm2_w", "b_ada", "conv_w", "conv_b", "dt_bias", "a_log", "d_skip", "ssd_norm_w",
                   "q_norm_w", "k_norm_w", "attn_norm_w"]
    small_g = dict(norm1_w=g_norm1, norm2_w=g_norm2, b_ada=g_b_ada, conv_w=g_conv_w.reshape(1, KCONV * ccs),
                   conv_b=g_conv_b, dt_bias=g_dtb, a_log=g_alog, d_skip=g_dsk, ssd_norm_w=g_ssd_norm, q_norm_w=g_q_norm,
                   k_norm_w=g_k_norm, attn_norm_w=g_attn_norm)
    small_w = dict(norm1_w=(norm1_w, m_norm1_w, v_norm1_w), norm2_w=(norm2_w, m_norm2_w, v_norm2_w),
                   b_ada=(b_ada, m_b_ada, v_b_ada),
                   conv_w=tuple(t.reshape(1, KCONV * ccs) for t in (conv_w, m_conv_w, v_conv_w)),
                   conv_b=(conv_b, m_conv_b, v_conv_b), dt_bias=(dt_bias, m_dt_bias, v_dt_bias),
                   a_log=(a_log, m_a_log, v_a_log), d_skip=(d_skip, m_d_skip, v_d_skip),
                   ssd_norm_w=(ssd_norm_w, m_ssd_norm_w, v_ssd_norm_w), q_norm_w=(q_norm_w, m_q_norm_w, v_q_norm_w),
                   k_norm_w=(k_norm_w, m_k_norm_w, v_k_norm_w), attn_norm_w=(attn_norm_w, m_attn_norm_w, v_attn_norm_w))
    ssz = [_pad_lanes(small_g[n]).shape[1] for n in small_names]
    soff = [sum(ssz[:i]) for i in range(len(ssz))]
    srow = -(-sum(ssz) // LANES // 8) * 8

    def pack(ts, fill):
        t = jnp.concatenate([jnp.pad(t, ((0, 0), (0, (-t.shape[1]) % LANES)), constant_values=fill) for t in ts], axis=1)
        return jnp.pad(t, ((0, 0), (0, srow * LANES - t.shape[1])), constant_values=fill).reshape(srow, LANES)

    sg = pack([small_g[n] for n in small_names], 0.0)
    sw = pack([small_w[n][0] for n in small_names], 0.0)
    sm_ = pack([small_w[n][1] for n in small_names], 0.0)
    sv = pack([small_w[n][2] for n in small_names], 1.0)
    _, s_delta, s_m, s_v = _adamw(sg[None], sw, sm_, sv, "adamw_small", tm=srow)

    def unpack(t, n):
        i = small_names.index(n)
        return t.reshape(1, srow * LANES)[:, soff[i]:soff[i] + small_g[n].shape[1]].reshape(small_w[n][0].shape)

    loss = lax.psum(loss_p[0, 0], ("x", "y", "c"))
    big_res = dict(w_ada=res_ada, w_in=res_in, w_out=res_out, w_ff1=res_ff1, w_ff2=res_ff2)
    order = ["norm1_w", "norm2_w", "w_ada", "b_ada", "w_in", "conv_w", "conv_b", "dt_bias", "a_log", "d_skip",
             "ssd_norm_w", "q_norm_w", "k_norm_w", "attn_norm_w", "w_out", "w_ff1", "w_ff2"]
    grads, deltas, new_m, new_v = [], [], [], []
    for n in order:
        if n in big_res:
            g_, d_, m_, v_ = [t[None] for t in big_res[n]]
        else:
            g_ = small_g[n].reshape(small_w[n][0].shape)
            d_, m_, v_ = unpack(s_delta, n), unpack(s_m, n), unpack(s_v, n)
            if n == "conv_w":
                g_, d_, m_, v_ = [t.reshape(conv_w.shape) for t in (g_, d_, m_, v_)]
        grads.append(g_)
        deltas.append(d_)
        new_m.append(m_)
        new_v.append(v_)
    return (loss, grad_x[None], *grads, *deltas, *new_m, *new_v)
```

```python
import functools

import jax
import jax.numpy as jnp
from jax import lax
from jax.experimental import pallas as pl
from jax.experimental.pallas import tpu as pltpu

F32, BF16 = jnp.float32, jnp.bfloat16
EPS = 1e-6
HD = 64
NH_SSD = 16
NG = 4
NSTATE = 128
KCONV = 4
CHUNK = 128
NH_ATT = 16
PATTERNS = ((128, 1), (512, 4), (2048, 16))
ABLK = 128
QTILE = 128
LANES = 128
ADAM_LR, ADAM_B1, ADAM_B2, ADAM_EPS, ADAM_WD, ADAM_STEP = 0.001, 0.9, 0.999, 1e-08, 0.01, 10
VMEM_LIMIT = 56 * 1024 * 1024
MESH = pl.DeviceIdType.MESH
NEG = -1e30

_DN = {"nn": (((1,), (0,)), ((), ())), "nt": (((1,), (1,)), ((), ())), "tn": (((0,), (0,)), ((), ()))}


def _cparams(sem):
    return pltpu.CompilerParams(dimension_semantics=sem, vmem_limit_bytes=VMEM_LIMIT)


def _tile(n, cap):
    if n % LANES or n <= LANES:
        return n
    best = LANES
    for t in range(LANES, min(n, cap) + 1, LANES):
        if n % t == 0:
            best = t
    return best


def _silu(x):
    return x / (1.0 + jnp.exp(-x))


def _softplus(x):
    return jnp.maximum(x, 0.0) + jnp.log(1.0 + jnp.exp(-jnp.abs(x)))


def _dot(a, b, dims):
    return lax.dot_general(a.astype(BF16), b.astype(BF16), _DN[dims], preferred_element_type=F32)


def _matmul(a, b, dims, out_dtype, name, a_fn=None, epilogue=None, extras=(), tm=1024, tn=1024, tk=1024,
            out_shape4=None, out_map=None):
    if dims == "nn":
        (M, K), (_, N) = a.shape, b.shape
    elif dims == "nt":
        (M, K), (N, _) = a.shape, b.shape
    else:
        (K, M), (_, N) = a.shape, b.shape
    tm, tn, tk = _tile(M, tm), _tile(N, tn), _tile(K, tk)
    nk = K // tk
    ne = len(extras)

    def body(a_ref, b_ref, *rest):
        e_refs, o_ref = rest[:ne], rest[ne]
        av = a_ref[...]
        if a_fn is not None:
            av = a_fn(av)
        part = _dot(av, b_ref[...], dims)

        def finish(r):
            if epilogue is not None:
                r = epilogue(r, *[e[...] for e in e_refs])
            o_ref[...] = r.astype(out_dtype)

        if nk == 1:
            finish(part)
            return
        acc = rest[ne + 1]
        k = pl.program_id(2)

        @pl.when(k == 0)
        def _():
            acc[...] = part

        @pl.when(k > 0)
        def _():
            acc[...] += part

        @pl.when(k == nk - 1)
        def _():
            finish(acc[...])

    a_spec = pl.BlockSpec((tk, tm), lambda i, j, k: (k, i)) if dims == "tn" else pl.BlockSpec((tm, tk), lambda i, j, k: (i, k))
    b_spec = pl.BlockSpec((tn, tk), lambda i, j, k: (j, k)) if dims == "nt" else pl.BlockSpec((tk, tn), lambda i, j, k: (k, j))
    o_spec = pl.BlockSpec((tm, tn), lambda i, j, k: (i, j))
    out_spec, out_dims = o_spec, (M, N)
    if out_shape4 is not None:
        assert out_shape4[2:] == (tm, tn)
        out_spec = pl.BlockSpec((None, None, tm, tn), lambda i, j, k: (*out_map(i, j), 0, 0))
        out_dims = out_shape4
    return pl.pallas_call(
        body, name=name, grid=(M // tm, N // tn, nk),
        in_specs=[a_spec, b_spec] + [o_spec] * ne, out_specs=out_spec,
        out_shape=jax.ShapeDtypeStruct(out_dims, out_dtype),
        scratch_shapes=[pltpu.VMEM((tm, tn), F32)] if nk > 1 else [],
        compiler_params=_cparams(("parallel", "parallel", "arbitrary")),
    )(a, b, *extras)


def _rows(name, fn, rows, consts, outs, accs, n_rows, tm=256):
    tm = min(tm, n_rows)
    nr, nc, no, na = len(rows), len(consts), len(outs), len(accs)

    def body(*refs):
        r_refs, c_refs = refs[:nr], refs[nr:nr + nc]
        o_refs, a_refs = refs[nr + nc:nr + nc + no], refs[nr + nc + no:]
        o_vals, a_vals = fn([r[...] for r in r_refs], [c[...] for c in c_refs])
        for ref, val in zip(o_refs, o_vals):
            ref[...] = val.astype(ref.dtype)
        if na:
            @pl.when(pl.program_id(0) == 0)
            def _():
                for ref in a_refs:
                    ref[...] = jnp.zeros_like(ref)
            for ref, val in zip(a_refs, a_vals):
                ref[...] += val

    in_specs = [pl.BlockSpec((tm, w), lambda i, cb=cb: (i, cb)) for (_, cb, w) in rows]
    in_specs += [pl.BlockSpec(cst.shape, lambda i, nd=cst.ndim: (0,) * nd) for cst in consts]
    out_specs = [pl.BlockSpec((tm, w), lambda i: (i, 0)) for (w, _) in outs]
    out_specs += [pl.BlockSpec(s, lambda i: (0, 0)) for s in accs]
    out_shape = [jax.ShapeDtypeStruct((n_rows, w), dt) for (w, dt) in outs]
    out_shape += [jax.ShapeDtypeStruct(s, F32) for s in accs]
    res = pl.pallas_call(
        body, name=name, grid=(n_rows // tm,), in_specs=in_specs, out_specs=out_specs, out_shape=out_shape,
        compiler_params=_cparams(("arbitrary",)),
    )(*[r[0] for r in rows], *consts)
    return res


def _normmod(x, nw, sc, sh):
    r = lax.rsqrt(jnp.mean(x * x, axis=-1, keepdims=True) + EPS)
    return (x * r) * nw * (1.0 + sc) + sh


def _resid_normmod(x, mix, g, nw, sc, sh):
    x2 = x + g * mix
    return x2, _normmod(x2, nw, sc, sh)


def _rmsw(o, w):
    return o * lax.rsqrt(jnp.mean(o * o, axis=-1, keepdims=True) + EPS) * w


def _lane_mask():
    return lax.broadcasted_iota(jnp.int32, (1, LANES), 1) < HD


def _headnorm(t, w, scale):
    lo = _lane_mask()
    t2 = t * t
    s0 = jnp.sum(jnp.where(lo, t2, 0.0), axis=1, keepdims=True)
    s1 = jnp.sum(jnp.where(lo, 0.0, t2), axis=1, keepdims=True)
    ms = jnp.where(lo, s0, s1) * (1.0 / HD)
    return t * lax.rsqrt(ms + EPS) * w * scale


CONV_ROWS = 128
CONV_HALO = 8


def _conv_cols(n_ch):
    return _tile(n_ch, LANES)


def _conv_fwd(proj, col0, n_ch, conv_w, conv_b, name):
    S = proj.shape[0]
    tc = _conv_cols(n_ch)

    R, H = CONV_ROWS, CONV_HALO

    def body(u_ref, w_ref, b_ref, o_ref):
        w = [w_ref[i:i + 1, :] for i in range(KCONV)]
        b = b_ref[...]

        def chunk(ext):
            acc = b + w[KCONV - 1] * ext[H:]
            for i in range(KCONV - 1):
                acc = acc + w[i] * pltpu.roll(ext, KCONV - 1 - i, 0)[H:]
            return _silu(acc)

        o_ref[0:R, :] = chunk(jnp.concatenate([jnp.zeros((H, tc), F32), u_ref[0:R, :]], axis=0))

        def step(c, carry):
            r0 = pl.multiple_of(c * R, R)
            o_ref[pl.ds(r0, R), :] = chunk(u_ref[pl.ds(pl.multiple_of(r0 - H, H), R + H), :])
            return carry

        lax.fori_loop(1, S // R, step, 0)

    return pl.pallas_call(
        body, name=name, grid=(n_ch // tc,),
        in_specs=[pl.BlockSpec((S, tc), lambda j: (0, j + col0 // tc)),
                  pl.BlockSpec((KCONV, tc), lambda j: (0, j)), pl.BlockSpec((1, tc), lambda j: (0, j))],
        out_specs=pl.BlockSpec((S, tc), lambda j: (0, j)),
        out_shape=jax.ShapeDtypeStruct((S, n_ch), F32),
        compiler_params=_cparams(("parallel",)),
    )(proj, conv_w, conv_b)


def _conv_bwd(proj, col0, n_ch, conv_w, conv_b, dxbc, name):
    S = proj.shape[0]
    tc = _conv_cols(n_ch)

    R, H = CONV_ROWS, CONV_HALO

    def body(u_ref, w_ref, b_ref, g_ref, du_ref, dw_ref, db_ref):
        w = [w_ref[i:i + 1, :] for i in range(KCONV)]
        b = b_ref[...]
        pad = jnp.zeros((H, tc), F32)

        def chunk(u_ext, g_ext):
            taps = [pltpu.roll(u_ext, KCONV - 1 - i, 0)[H:] for i in range(KCONV - 1)] + [u_ext[H:]]
            acc = b
            for i in range(KCONV):
                acc = acc + w[i] * taps[i]
            sig = 1.0 / (1.0 + jnp.exp(-acc))
            dacc = g_ext * (sig * (1.0 + acc * (1.0 - sig)))
            du = w[KCONV - 1] * dacc[:R]
            for i in range(KCONV - 1):
                du = du + w[i] * pltpu.roll(dacc, R + H - (KCONV - 1 - i), 0)[:R]
            d = dacc[:R]
            return du, [jnp.sum(d * t[:R], axis=0, keepdims=True) for t in taps], jnp.sum(d, axis=0, keepdims=True)

        du, dws, db = chunk(jnp.concatenate([pad, u_ref[0:R + H, :]], axis=0), g_ref[0:R + H, :])
        du_ref[0:R, :] = du

        def step(c, carry):
            r0 = pl.multiple_of(c * R, R)
            du_c, dws_c, db_c = chunk(u_ref[pl.ds(pl.multiple_of(r0 - H, H), R + 2 * H), :], g_ref[pl.ds(r0, R + H), :])
            du_ref[pl.ds(r0, R), :] = du_c
            return [a + b_ for a, b_ in zip(carry[0], dws_c)], carry[1] + db_c

        dws, db = lax.fori_loop(1, S // R - 1, step, (dws, db))
        du, dws_l, db_l = chunk(jnp.concatenate([u_ref[S - R - H:S, :], pad], axis=0),
                                jnp.concatenate([g_ref[S - R:S, :], pad], axis=0))
        du_ref[S - R:S, :] = du
        for i in range(KCONV):
            dw_ref[i:i + 1, :] = dws[i] + dws_l[i]
        db_ref[...] = db + db_l

    return pl.pallas_call(
        body, name=name, grid=(n_ch // tc,),
        in_specs=[pl.BlockSpec((S, tc), lambda j: (0, j + col0 // tc)),
                  pl.BlockSpec((KCONV, tc), lambda j: (0, j)), pl.BlockSpec((1, tc), lambda j: (0, j)),
                  pl.BlockSpec((S, tc), lambda j: (0, j))],
        out_specs=[pl.BlockSpec((S, tc), lambda j: (0, j)), pl.BlockSpec((KCONV, tc), lambda j: (0, j)),
                   pl.BlockSpec((1, tc), lambda j: (0, j))],
        out_shape=[jax.ShapeDtypeStruct((S, n_ch), F32), jax.ShapeDtypeStruct((KCONV, n_ch), F32),
                   jax.ShapeDtypeStruct((1, n_ch), F32)],
        compiler_params=_cparams(("parallel",)),
    )(proj, conv_w, conv_b, dxbc)


@functools.partial(jax.custom_vjp, nondiff_argnums=(2,))
def _mm(a, b, dims):
    return _dot(a, b, dims)


def _mm_fwd(a, b, dims):
    return _dot(a, b, dims), (a, b)


def _mm_bwd(dims, res, g):
    a, b = res
    if dims == "nn":
        return _dot(g, b, "nt"), _dot(a, g, "tn")
    if dims == "nt":
        return _dot(g, b, "nn"), _dot(g, a, "tn")
    return _dot(b, g, "nt"), _dot(a, g, "nn")


_mm.defvjp(_mm_fwd, _mm_bwd)


def _tri_dot(x, upper):
    n = x.shape[0]
    r = lax.broadcasted_iota(jnp.int32, (n, n), 0)
    c = lax.broadcasted_iota(jnp.int32, (n, n), 1)
    t = jnp.where((r <= c) if upper else (r >= c), 1.0, 0.0)
    return lax.dot_general(t, x, _DN["nn"], precision=lax.Precision.HIGHEST, preferred_element_type=F32)


@jax.custom_vjp
def _cumsum_rows(x):
    return _tri_dot(x, False)


_cumsum_rows.defvjp(lambda x: (_tri_dot(x, False), None), lambda _, g: (_tri_dot(g, True),))


def _ssd_chunk(xs_p, bm_g, cm_g, dtr, z_p, dtb, alog, dsk, nw_p, h_p):
    L = dtr.shape[0]
    n_pairs = len(xs_p)
    ppg = n_pairs // len(bm_g)
    lane = lax.broadcasted_iota(jnp.int32, (1, LANES), 1)
    sub = lax.broadcasted_iota(jnp.int32, (LANES, 1), 0)
    lo = lane < HD
    row_l = lax.broadcasted_iota(jnp.int32, (L, 1), 0)
    tri = lax.broadcasted_iota(jnp.int32, (L, L), 0) >= lax.broadcasted_iota(jnp.int32, (L, L), 1)

    dt = _softplus(dtr + dtb)
    acs = _cumsum_rows(dt * (-jnp.exp(alog)))
    acs_t = acs.T
    a_last = jnp.sum(jnp.where(row_l == L - 1, acs, 0.0), axis=0, keepdims=True)
    e_acs = jnp.exp(acs)
    dec = jnp.exp(a_last - acs)
    cdec = jnp.exp(a_last)

    def colv(m, h):
        return jnp.sum(jnp.where(lane == h, m, 0.0), axis=1, keepdims=True)

    def rowv(mt, h):
        return jnp.sum(jnp.where(sub == h, mt, 0.0), axis=0, keepdims=True)

    def pair(m, h0):
        return jnp.where(lo, colv(m, h0), colv(m, h0 + 1))

    ys, hs = [], []
    cb = None
    for p in range(n_pairs):
        g, h0 = p // ppg, 2 * p
        bmat, cmat = bm_g[g], cm_g[g]
        if p % ppg == 0:
            cb = _mm(cmat, bmat, "nt")
        x = xs_p[p]
        xdt = x * pair(dt, h0)
        yd = []
        for h in (h0, h0 + 1):
            seg = colv(acs, h) - rowv(acs_t, h)
            lm = jnp.where(tri, jnp.exp(jnp.where(tri, seg, 0.0)), 0.0)
            yd.append(_mm(cb * lm, xdt, "nn"))
        y = jnp.where(lo, yd[0], yd[1])
        y = y + _mm(cmat, h_p[p], "nt") * pair(e_acs, h0)
        st = _mm(xdt * pair(dec, h0), bmat, "tn")
        cd_col = jnp.where(sub < HD, colv(cdec, h0), colv(cdec, h0 + 1))
        hs.append(h_p[p] * cd_col + st)
        ys.append(y + pair(dsk, h0) * x)

    y2 = [ys[p] * _silu(z_p[p]) for p in range(n_pairs)]
    outs = []
    for g in range(len(bm_g)):
        ps = range(g * ppg, (g + 1) * ppg)
        ss = sum(jnp.sum(y2[p] * y2[p], axis=1, keepdims=True) for p in ps)
        rs = lax.rsqrt(ss * (1.0 / (ppg * LANES)) + EPS)
        outs += [y2[p] * rs * nw_p[p] for p in ps]
    return outs, hs


def _ssd_slices(xbc_ref, z_ref, nw_ref, di):
    n_pairs = di // LANES
    xs_p = [xbc_ref[:, p * LANES:(p + 1) * LANES] for p in range(n_pairs)]
    bm_g = [xbc_ref[:, di + g * NSTATE:di + (g + 1) * NSTATE] for g in range(NG)]
    cm_g = [xbc_ref[:, di + (NG + g) * NSTATE:di + (NG + g + 1) * NSTATE] for g in range(NG)]
    z_p = [z_ref[:, p * LANES:(p + 1) * LANES] for p in range(n_pairs)]
    nw_p = [nw_ref[:, p * LANES:(p + 1) * LANES] for p in range(n_pairs)]
    return xs_p, bm_g, cm_g, z_p, nw_p


def _ssd_fwd(xbc, proj, dt_cb, dtb, alog, dsk, nw, name):
    S, cc = xbc.shape
    di = NH_SSD * HD
    n_pairs = di // LANES
    nchunk = S // CHUNK

    def body(xbc_ref, z_ref, dtr_ref, dtb_ref, alog_ref, dsk_ref, nw_ref, y_ref, hs_ref, h_scr):
        @pl.when(pl.program_id(0) == 0)
        def _():
            h_scr[...] = jnp.zeros_like(h_scr)

        xs_p, bm_g, cm_g, z_p, nw_p = _ssd_slices(xbc_ref, z_ref, nw_ref, di)
        h_p = [h_scr[p * LANES:(p + 1) * LANES, :] for p in range(n_pairs)]
        hs_ref[...] = h_scr[...]
        outs, hs = _ssd_chunk(xs_p, bm_g, cm_g, dtr_ref[...], z_p, dtb_ref[...], alog_ref[...], dsk_ref[...], nw_p, h_p)
        for p in range(n_pairs):
            y_ref[:, p * LANES:(p + 1) * LANES] = outs[p].astype(y_ref.dtype)
            h_scr[p * LANES:(p + 1) * LANES, :] = hs[p]

    vec = pl.BlockSpec((1, LANES), lambda c: (0, 0))
    return pl.pallas_call(
        body, name=name, grid=(nchunk,),
        in_specs=[pl.BlockSpec((CHUNK, cc), lambda c: (c, 0)), pl.BlockSpec((CHUNK, di), lambda c: (c, 0)),
                  pl.BlockSpec((CHUNK, LANES), lambda c: (c, dt_cb)), vec, vec, vec,
                  pl.BlockSpec((1, di), lambda c: (0, 0))],
        out_specs=[pl.BlockSpec((CHUNK, di), lambda c: (c, 0)), pl.BlockSpec((None, di, NSTATE), lambda c: (c, 0, 0))],
        out_shape=[jax.ShapeDtypeStruct((S, di), BF16), jax.ShapeDtypeStruct((nchunk, di, NSTATE), F32)],
        scratch_shapes=[pltpu.VMEM((di, NSTATE), F32)],
        compiler_params=_cparams(("arbitrary",)),
    )(xbc, proj, proj, dtb, alog, dsk, nw)


def _ssd_bwd(xbc, proj, dt_cb, dtb, alog, dsk, nw, hsave, dy, name):
    S, cc = xbc.shape
    di = NH_SSD * HD
    n_pairs = di // LANES
    nchunk = S // CHUNK

    def body(xbc_ref, z_ref, dtr_ref, dtb_ref, alog_ref, dsk_ref, nw_ref, hs_ref, dy_ref,
             dxbc_ref, dz_ref, ddtr_ref, ddtb_ref, dalog_ref, ddsk_ref, dnw_ref, dh_scr):
        @pl.when(pl.program_id(0) == 0)
        def _():
            dh_scr[...] = jnp.zeros_like(dh_scr)
            ddtb_ref[...] = jnp.zeros_like(ddtb_ref)
            dalog_ref[...] = jnp.zeros_like(dalog_ref)
            ddsk_ref[...] = jnp.zeros_like(ddsk_ref)
            dnw_ref[...] = jnp.zeros_like(dnw_ref)

        xs_p, bm_g, cm_g, z_p, nw_p = _ssd_slices(xbc_ref, z_ref, nw_ref, di)
        h_p = [hs_ref[p * LANES:(p + 1) * LANES, :] for p in range(n_pairs)]
        dy_p = [dy_ref[:, p * LANES:(p + 1) * LANES].astype(F32) for p in range(n_pairs)]
        dh_p = [dh_scr[p * LANES:(p + 1) * LANES, :] for p in range(n_pairs)]
        _, vjp = jax.vjp(_ssd_chunk, xs_p, bm_g, cm_g, dtr_ref[...], z_p, dtb_ref[...], alog_ref[...], dsk_ref[...],
                         nw_p, h_p)
        dxs, dbm, dcm, ddtr, dz, ddtb, dalog, ddsk, dnw, dh = vjp((dy_p, dh_p))
        for p in range(n_pairs):
            sl = slice(p * LANES, (p + 1) * LANES)
            dxbc_ref[:, sl] = dxs[p]
            dz_ref[:, sl] = dz[p]
            dnw_ref[:, sl] += dnw[p]
            dh_scr[sl, :] = dh[p]
        for g in range(NG):
            dxbc_ref[:, di + g * NSTATE:di + (g + 1) * NSTATE] = dbm[g]
            dxbc_ref[:, di + (NG + g) * NSTATE:di + (NG + g + 1) * NSTATE] = dcm[g]
        ddtr_ref[...] = ddtr
        ddtb_ref[...] += ddtb
        dalog_ref[...] += dalog
        ddsk_ref[...] += ddsk

    last = nchunk - 1
    vec = pl.BlockSpec((1, LANES), lambda c: (0, 0))
    return pl.pallas_call(
        body, name=name, grid=(nchunk,),
        in_specs=[pl.BlockSpec((CHUNK, cc), lambda c: (last - c, 0)), pl.BlockSpec((CHUNK, di), lambda c: (last - c, 0)),
                  pl.BlockSpec((CHUNK, LANES), lambda c: (last - c, dt_cb)), vec, vec, vec,
                  pl.BlockSpec((1, di), lambda c: (0, 0)),
                  pl.BlockSpec((None, di, NSTATE), lambda c: (last - c, 0, 0)),
                  pl.BlockSpec((CHUNK, di), lambda c: (last - c, 0))],
        out_specs=[pl.BlockSpec((CHUNK, cc), lambda c: (last - c, 0)), pl.BlockSpec((CHUNK, di), lambda c: (last - c, 0)),
                   pl.BlockSpec((CHUNK, LANES), lambda c: (last - c, 0)), vec, vec, vec,
                   pl.BlockSpec((1, di), lambda c: (0, 0))],
        out_shape=[jax.ShapeDtypeStruct((S, cc), F32), jax.ShapeDtypeStruct((S, di), F32),
                   jax.ShapeDtypeStruct((S, LANES), F32), jax.ShapeDtypeStruct((1, LANES), F32),
                   jax.ShapeDtypeStruct((1, LANES), F32), jax.ShapeDtypeStruct((1, LANES), F32),
                   jax.ShapeDtypeStruct((1, di), F32)],
        scratch_shapes=[pltpu.VMEM((di, NSTATE), F32)],
        compiler_params=_cparams(("arbitrary",)),
    )(xbc, proj, proj, dtb, alog, dsk, nw, hsave, dy)


def _band_masks(rows_q, rows_k):
    qi = lax.broadcasted_iota(jnp.int32, (rows_q, rows_k), 0)
    ki = lax.broadcasted_iota(jnp.int32, (rows_q, rows_k), 1)
    return qi, ki


def _class_chunks(n_rows, d):
    per_class = n_rows // d
    ch = min(per_class, 256)
    out = []
    for r in range(d):
        for c0 in range(0, per_class, ch):
            tok = pl.ds(c0, ch) if d == 1 else pl.ds(r + d * c0, ch, stride=d)
            out.append((tok, pl.ds(r * per_class + c0, ch)))
    return out


def _to_class_order(src_ref, dst_ref, n_rows, d):
    for tok, cls in _class_chunks(n_rows, d):
        dst_ref[cls, :] = src_ref[tok, :].astype(dst_ref.dtype)


def _blk_rows(t):
    return pl.ds(pl.multiple_of(t * ABLK, ABLK), ABLK)


def _head_lanes(msk, t, t_rolled):
    return jnp.where(msk, t, t_rolled)


def _zero_unless(msk, t):
    return jnp.where(msk, t, jnp.zeros_like(t))


def _attn_fwd(qn, kn, proj, v_cb, name):
    S, ad = qn.shape
    nb = S // ABLK
    nbr = len(PATTERNS)

    def body(q_ref, k_ref, v_ref, o_ref, lse_ref, qc, kc, vc, ob, mb, lb, m_s, l_s):
        lo = _lane_mask()
        qi, ki = _band_masks(ABLK, 2 * ABLK)
        band, in_cur, prev_ok = ki <= qi + ABLK, ki >= ABLK, ki >= qi
        for bi, (_, d) in enumerate(PATTERNS):
            nbc = S // d // ABLK
            first, last = bi == 0, bi == nbr - 1
            qs, ks, vs = q_ref, k_ref, v_ref
            if d > 1:
                qs, ks, vs = qc, kc, vc
                for src, dst in ((q_ref, qc), (k_ref, kc), (v_ref, vc)):
                    _to_class_order(src, dst, S, d)
            o_dst, m_dst, l_dst = (o_ref, m_s, l_s) if first else (ob, mb, lb)

            def blk(t, carry, nbc=nbc, qs=qs, ks=ks, vs=vs, o_dst=o_dst, m_dst=m_dst, l_dst=l_dst):
                rows, prow = _blk_rows(t), _blk_rows(jnp.maximum(t - 1, 0))
                has_prev = (t % nbc) != 0
                kk = jnp.concatenate([ks[prow, :], ks[rows, :]], axis=0)
                vv = jnp.concatenate([vs[prow, :], vs[rows, :]], axis=0)
                for u in range(ABLK // QTILE):
                    sub = pl.ds(pl.multiple_of(t * ABLK + u * QTILE, QTILE), QTILE)
                    sl = slice(u * QTILE, (u + 1) * QTILE)
                    valid = band[sl] & (in_cur[sl] | (prev_ok[sl] & has_prev))
                    qv = qs[sub, :]
                    os_, ms_, ls_ = [], [], []
                    for msk in (lo, jnp.logical_not(lo)):
                        s = jnp.where(valid, _dot(_zero_unless(msk, qv), kk, "nt"), NEG)
                        m = jnp.max(s, axis=1, keepdims=True)
                        p = jnp.exp(s - m)
                        os_.append(_dot(p, vv, "nn"))
                        ms_.append(m)
                        ls_.append(jnp.sum(p, axis=1, keepdims=True))
                    o_dst[sub, :] = jnp.where(lo, os_[0], os_[1])
                    m_dst[sub, :] = jnp.where(lo, ms_[0], ms_[1])
                    l_dst[sub, :] = jnp.where(lo, ls_[0], ls_[1])
                return carry

            lax.fori_loop(0, nb, blk, 0, unroll=8)
            if first:
                continue
            for tok, cls in _class_chunks(S, d):
                m_old, m_b = m_s[tok, :], mb[cls, :]
                m_new = jnp.maximum(m_old, m_b)
                a, b = jnp.exp(m_old - m_new), jnp.exp(m_b - m_new)
                l_new = a * l_s[tok, :] + b * lb[cls, :]
                o_new = a * o_ref[tok, :] + b * ob[cls, :]
                if last:
                    o_ref[tok, :] = o_new / l_new
                    lse_ref[tok, :] = m_new + jnp.log(l_new)
                else:
                    o_ref[tok, :] = o_new
                    m_s[tok, :] = m_new
                    l_s[tok, :] = l_new

    col = pl.BlockSpec((S, LANES), lambda h: (0, h))
    return pl.pallas_call(
        body, name=name, grid=(ad // LANES,),
        in_specs=[col, col, pl.BlockSpec((S, LANES), lambda h: (0, h + v_cb))], out_specs=[col, col],
        out_shape=[jax.ShapeDtypeStruct((S, ad), F32), jax.ShapeDtypeStruct((S, ad), F32)],
        scratch_shapes=[pltpu.VMEM((S, LANES), BF16)] * 3 + [pltpu.VMEM((S, LANES), F32)] * 5,
        compiler_params=_cparams(("parallel",)),
    )(qn, kn, proj)


def _attn_bwd(qn, kn, proj, v_cb, do, lse, dd, name):
    S, ad = qn.shape
    nb = S // ABLK

    def body(q_ref, k_ref, v_ref, do_ref, lse_ref, dd_ref, dq_ref, dk_ref, dv_ref,
             qc, kc, vc, doc, lsec, ddc, dqc, dkc, dvc):
        lo = _lane_mask()
        qi, ki = _band_masks(ABLK, ABLK)
        cur_ok, prev_ok = ki <= qi, ki >= qi
        for bi, (_, d) in enumerate(PATTERNS):
            nbc = S // d // ABLK
            first = bi == 0
            token_order = (q_ref, k_ref, v_ref, do_ref, lse_ref, dd_ref)
            class_order = (qc, kc, vc, doc, lsec, ddc)
            if d > 1:
                for src, dst in zip(token_order, class_order):
                    _to_class_order(src, dst, S, d)
            qs, ks, vs, dos, lses, dds = class_order if d > 1 else token_order
            dq_dst, dk_dst, dv_dst = (dq_ref, dk_ref, dv_ref) if first else (dqc, dkc, dvc)
            dk_dst[...] = jnp.zeros_like(dk_dst)
            dv_dst[...] = jnp.zeros_like(dv_dst)

            def blk(t, carry, nbc=nbc, qs=qs, ks=ks, vs=vs, dos=dos, lses=lses, dds=dds,
                    dq_dst=dq_dst, dk_dst=dk_dst, dv_dst=dv_dst):
                rows, prow = _blk_rows(t), _blk_rows(jnp.maximum(t - 1, 0))
                has_prev = (t % nbc) != 0
                qv, dov, lse_b, dd_b = qs[rows, :], dos[rows, :], lses[rows, :], dds[rows, :]
                lse_r, dd_r = pltpu.roll(lse_b, HD, 1), pltpu.roll(dd_b, HD, 1)
                heads = []
                for msk in (lo, jnp.logical_not(lo)):
                    heads.append((_zero_unless(msk, qv), _zero_unless(msk, dov), _head_lanes(msk, lse_b, lse_r),
                                  _head_lanes(msk, dd_b, dd_r)))
                dqs = [None, None]
                for krows, vmask in ((rows, cur_ok), (prow, prev_ok & has_prev)):
                    kv, vv = ks[krows, :], vs[krows, :]
                    dk = jnp.zeros((ABLK, LANES), F32)
                    dv = jnp.zeros((ABLK, LANES), F32)
                    for hi, (qh, doh, lse_h, dd_h) in enumerate(heads):
                        s = jnp.where(vmask, _dot(qh, kv, "nt"), NEG)
                        p = jnp.exp(s - lse_h)
                        ds = p * (_dot(doh, vv, "nt") - dd_h)
                        dqh = _dot(ds, kv, "nn")
                        dqs[hi] = dqh if dqs[hi] is None else dqs[hi] + dqh
                        dv = dv + _dot(p, doh, "tn")
                        dk = dk + _dot(ds, qh, "tn")
                    dk_dst[krows, :] += dk
                    dv_dst[krows, :] += dv
                dq_dst[rows, :] = jnp.where(lo, dqs[0], dqs[1])
                return carry

            lax.fori_loop(0, nb, blk, 0, unroll=4)
            if not first:
                for tok, cls in _class_chunks(S, d):
                    dq_ref[tok, :] = dq_ref[tok, :] + dqc[cls, :]
                    dk_ref[tok, :] = dk_ref[tok, :] + dkc[cls, :]
                    dv_ref[tok, :] = dv_ref[tok, :] + dvc[cls, :]

    col = pl.BlockSpec((S, LANES), lambda h: (0, h))
    col1 = pl.BlockSpec((S, LANES), lambda h: (0, h), pipeline_mode=pl.Buffered(1))
    vcol1 = pl.BlockSpec((S, LANES), lambda h: (0, h + v_cb), pipeline_mode=pl.Buffered(1))
    return pl.pallas_call(
        body, name=name, grid=(ad // LANES,),
        in_specs=[col, col, vcol1, col1, col1, col1], out_specs=[col, col, col],
        out_shape=[jax.ShapeDtypeStruct((S, ad), F32)] * 3,
        scratch_shapes=[pltpu.VMEM((S, LANES), BF16)] * 4 + [pltpu.VMEM((S, LANES), F32)] * 5,
        compiler_params=_cparams(("parallel",)),
    )(qn, kn, proj, do, lse, dd)


def _coords():
    return lax.axis_index("x"), lax.axis_index("y"), lax.axis_index("c")


def _exchange8(xs, per_dest, name):
    n = len(xs)
    blk = [x.shape[1:] if per_dest else x.shape for x in xs]

    def body(*refs):
        ins, outs = refs[:n], refs[n:2 * n]
        send_sems, recv_sems, local_sems = refs[2 * n:]
        x, y, c = _coords()
        sibling = (x, y, 1 - c)
        chips = [(1 - x, y), (x, 1 - y), (1 - x, 1 - y)]
        first, passed, mine = [], [], []
        for a in range(n):
            def src_for(cx, cy, a=a):
                return ins[a].at[2 * cx + cy] if per_dest else ins[a]

            def slot(px, py, pc, a=a):
                return outs[a].at[4 * px + 2 * py + pc]

            def copy(k, src, dst, to, a=a):
                return pltpu.make_async_remote_copy(src_ref=src, dst_ref=dst, send_sem=send_sems.at[7 * a + k],
                                                    recv_sem=recv_sems.at[7 * a + k], device_id=to, device_id_type=MESH)

            m = pltpu.make_async_copy(src_for(x, y), slot(x, y, c), local_sems.at[a])
            m.start()
            mine.append(m)
            cps = [copy(0, src_for(x, y), slot(x, y, c), sibling)]
            cps += [copy(1 + j, src_for(*chip), slot(x, y, c), (*chip, c)) for j, chip in enumerate(chips)]
            for cp in cps:
                cp.start()
            first += cps
        for a in range(n):
            def slot(px, py, pc, a=a):
                return outs[a].at[4 * px + 2 * py + pc]

            def copy(k, src, dst, to, a=a):
                return pltpu.make_async_remote_copy(src_ref=src, dst_ref=dst, send_sem=send_sems.at[7 * a + k],
                                                    recv_sem=recv_sems.at[7 * a + k], device_id=to, device_id_type=MESH)

            for j, chip in enumerate(chips):
                copy(1 + j, slot(*chip, c), slot(*chip, c), (*chip, c)).wait_recv()
                fw = copy(4 + j, slot(*chip, c), slot(*chip, c), sibling)
                fw.start()
                passed.append(fw)
        for a in range(n):
            def slot(px, py, pc, a=a):
                return outs[a].at[4 * px + 2 * py + pc]

            def copy(k, src, dst, to, a=a):
                return pltpu.make_async_remote_copy(src_ref=src, dst_ref=dst, send_sem=send_sems.at[7 * a + k],
                                                    recv_sem=recv_sems.at[7 * a + k], device_id=to, device_id_type=MESH)

            copy(0, slot(x, y, 1 - c), slot(x, y, 1 - c), sibling).wait_recv()
            for j, chip in enumerate(chips):
                copy(4 + j, slot(*chip, 1 - c), slot(*chip, 1 - c), sibling).wait_recv()
        for cp in first + passed:
            cp.wait_send()
        for m in mine:
            m.wait()

    anyspec = pl.BlockSpec(memory_space=pl.ANY)
    res = pl.pallas_call(
        body, name=name, in_specs=[anyspec] * n, out_specs=[anyspec] * n,
        out_shape=[jax.ShapeDtypeStruct((8,) + tuple(b), x.dtype) for b, x in zip(blk, xs)],
        scratch_shapes=[pltpu.SemaphoreType.DMA((7 * n,)), pltpu.SemaphoreType.DMA((7 * n,)),
                        pltpu.SemaphoreType.DMA((n,))],
    )(*xs)
    return list(res)


def _pair_swap(xs, name):
    n = len(xs)

    def body(*refs):
        ins, outs = refs[:n], refs[n:2 * n]
        send_sems, recv_sems = refs[2 * n:]
        x, y, c = _coords()
        cps = [pltpu.make_async_remote_copy(src_ref=ins[a].at[1 - c], dst_ref=outs[a], send_sem=send_sems.at[a],
                                            recv_sem=recv_sems.at[a], device_id=(x, y, 1 - c), device_id_type=MESH)
               for a in range(n)]
        for cp in cps:
            cp.start()
        for cp in cps:
            cp.wait()

    anyspec = pl.BlockSpec(memory_space=pl.ANY)
    res = pl.pallas_call(
        body, name=name, in_specs=[anyspec] * n, out_specs=[anyspec] * n,
        out_shape=[jax.ShapeDtypeStruct(x.shape[1:], x.dtype) for x in xs],
        scratch_shapes=[pltpu.SemaphoreType.DMA((n,)), pltpu.SemaphoreType.DMA((n,))],
    )(*xs)
    return list(res)


_HBM = pl.BlockSpec(memory_space=pltpu.HBM)
_SEM = pl.BlockSpec(memory_space=pltpu.SEMAPHORE)
_EFFECT = pltpu.SideEffectType.DATAFLOW_SIDE_EFFECTING


def _n_peers(both):
    return 7 if both else 3


def _peer(x, y, c, j, both):
    bits = j + 1 if both else 2 * (j + 1)
    dx, dy, dc = bits >> 2 & 1, bits >> 1 & 1, bits & 1
    return (1 - x if dx else x, 1 - y if dy else y, 1 - c if dc else c)


def _spread_copies(s_refs, l_refs, send_sems, recv_sems, per_dest, both):
    x, y, c = _coords()
    me = 4 * x + 2 * y + c
    npeer = _n_peers(both)
    cps = []
    for a in range(len(s_refs)):
        for j in range(npeer):
            tx, ty, tc = _peer(x, y, c, j, both)
            src = s_refs[a].at[2 * tx + ty] if per_dest else s_refs[a]
            cps.append(pltpu.make_async_remote_copy(src_ref=src, dst_ref=l_refs[a].at[me],
                                                    send_sem=send_sems.at[npeer * a + j],
                                                    recv_sem=recv_sems.at[npeer * a + j], device_id=(tx, ty, tc),
                                                    device_id_type=MESH))
    return cps


def _sibling_fill(lands, name):
    n = len(lands)

    def body(*refs):
        outs, send_sems, recv_sems = refs[n:2 * n], refs[2 * n], refs[2 * n + 1]
        x, y, c = _coords()
        cps = [pltpu.make_async_remote_copy(src_ref=outs[a].at[2 * k + c], dst_ref=outs[a].at[2 * k + c],
                                            send_sem=send_sems.at[4 * a + k], recv_sem=recv_sems.at[4 * a + k],
                                            device_id=(x, y, 1 - c), device_id_type=MESH)
               for a in range(n) for k in range(4)]
        for cp in cps:
            cp.start()
        for cp in cps:
            cp.wait()

    anyspec = pl.BlockSpec(memory_space=pl.ANY)
    res = pl.pallas_call(
        body, name=name, in_specs=[anyspec] * n, out_specs=[anyspec] * n,
        out_shape=[jax.ShapeDtypeStruct(t.shape, t.dtype) for t in lands], input_output_aliases={i: i for i in range(n)},
        scratch_shapes=[pltpu.SemaphoreType.DMA((4 * n,)), pltpu.SemaphoreType.DMA((4 * n,))],
    )(*lands)
    return list(res)


def _spread_start(srcs, per_dest, both, dev, chip, name):
    n = len(srcs)
    npeer = _n_peers(both)
    lands = []
    for s in srcs:
        own = lax.dynamic_index_in_dim(s, chip, 0, keepdims=False) if per_dest else s
        lands.append(lax.dynamic_update_index_in_dim(lax.empty((8,) + own.shape, own.dtype), own, dev, 0))

    def body(*refs):
        s_refs, l_refs, send_sems, recv_sems, token = refs[:n], refs[n:2 * n], refs[2 * n], refs[2 * n + 1], refs[-1]
        for cp in _spread_copies(s_refs, l_refs, send_sems, recv_sems, per_dest, both):
            cp.start()
        token[...] = jnp.zeros_like(token)

    hbm_in = [pltpu.with_memory_space_constraint(t, pltpu.HBM) for t in list(srcs) + lands]
    outs = pl.pallas_call(
        body, name=name,
        out_shape=(pltpu.SemaphoreType.DMA((npeer * n,)), pltpu.SemaphoreType.DMA((npeer * n,)),
                   *[pltpu.HBM(t.shape, t.dtype) for t in hbm_in], jax.ShapeDtypeStruct((8, LANES), F32)),
        in_specs=[_HBM] * (2 * n), out_specs=(_SEM, _SEM, *[_HBM] * (2 * n), pl.BlockSpec(memory_space=pltpu.VMEM)),
        input_output_aliases={i: 2 + i for i in range(2 * n)},
        compiler_params=pltpu.CompilerParams(has_side_effects=_EFFECT),
    )(*hbm_in)
    return (outs[0], outs[1], list(outs[2:2 + n]), list(outs[2 + n:2 + 2 * n])), outs[-1]


def _spread_wait(handle, per_dest, both, after, name):
    send_sems, recv_sems, srcs, lands = handle
    n = len(srcs)

    def body(*refs):
        s_refs, l_refs, send_ref, recv_ref = refs[:n], refs[n:2 * n], refs[2 * n], refs[2 * n + 1]
        for cp in _spread_copies(s_refs, l_refs, send_ref, recv_ref, per_dest, both):
            cp.wait_send()
            cp.wait_recv()

    outs = pl.pallas_call(
        body, name=name, out_shape=tuple(pltpu.HBM(t.shape, t.dtype) for t in srcs + lands),
        in_specs=[_HBM] * (2 * n) + [_SEM, _SEM, pl.BlockSpec(memory_space=pl.ANY)], out_specs=tuple([_HBM] * (2 * n)),
        input_output_aliases={i: i for i in range(2 * n)},
        compiler_params=pltpu.CompilerParams(has_side_effects=_EFFECT),
    )(*srcs, *lands, send_sems, recv_sems, after)
    return list(outs[n:])


def _row_tile(n, cap, mult):
    best = n
    for t in range(mult, min(n, cap) + 1, mult):
        if n % t == 0:
            best = t
    return best


def _pair_add(g2, theirs, half, name):
    _, n, cdim = g2.shape
    tm = _row_tile(n, 512, 16)

    def body(h_ref, a_ref, b_ref, o_ref):
        o_ref[...] = (a_ref[...] + b_ref[...]).astype(o_ref.dtype)

    grid_spec = pltpu.PrefetchScalarGridSpec(
        num_scalar_prefetch=1, grid=(n // tm,),
        in_specs=[pl.BlockSpec((None, tm, cdim), lambda i, h: (h[0], i, 0)), pl.BlockSpec((tm, cdim), lambda i, h: (i, 0))],
        out_specs=pl.BlockSpec((tm, cdim), lambda i, h: (i, 0)))
    return pl.pallas_call(body, name=name, grid_spec=grid_spec, out_shape=jax.ShapeDtypeStruct((n, cdim), BF16),
                          compiler_params=_cparams(("parallel",)))(half.reshape(1).astype(jnp.int32), g2, theirs)


def _adamw_math(w, g, m, v):
    m = ADAM_B1 * m + (1.0 - ADAM_B1) * g
    v = ADAM_B2 * v + (1.0 - ADAM_B2) * (g * g)
    m_hat = m / (1.0 - ADAM_B1 ** ADAM_STEP)
    v_hat = v / (1.0 - ADAM_B2 ** ADAM_STEP)
    delta = -ADAM_LR * (m_hat / (jnp.sqrt(v_hat) + ADAM_EPS) + ADAM_WD * w)
    return delta, m, v


def _adamw(parts, w, m, v, name, tm=128):
    npart, R, C = parts.shape
    tm = min(tm, R)

    def body(p_ref, w_ref, m_ref, v_ref, g_out, d_out, m_out, v_out):
        g = p_ref[0].astype(F32)
        for i in range(1, npart):
            g = g + p_ref[i].astype(F32)
        d, mm, vv = _adamw_math(w_ref[...], g, m_ref[...], v_ref[...])
        g_out[...] = g
        d_out[...] = d
        m_out[...] = mm
        v_out[...] = vv

    spec = pl.BlockSpec((tm, C), lambda i: (i, 0))
    return pl.pallas_call(
        body, name=name, grid=(R // tm,),
        in_specs=[pl.BlockSpec((npart, tm, C), lambda i: (0, i, 0)), spec, spec, spec], out_specs=[spec] * 4,
        out_shape=[jax.ShapeDtypeStruct((R, C), F32)] * 4,
        compiler_params=_cparams(("parallel",)),
    )(parts, w, m, v)


def _sum_parts(parts, name):
    npart, R, C = parts.shape

    def body(p_ref, o_ref):
        g = p_ref[0]
        for i in range(1, npart):
            g = g + p_ref[i]
        o_ref[...] = g

    return pl.pallas_call(body, name=name, out_shape=jax.ShapeDtypeStruct((R, C), F32))(parts)


def _mod_fwd(c_all, w_ada, b_sh, name):
    def body(c_ref, w_ref, b_ref, o_ref):
        o_ref[...] = _dot(_silu(c_ref[...]), w_ref[...], "nn") + b_ref[...]

    return pl.pallas_call(body, name=name, out_shape=jax.ShapeDtypeStruct((c_all.shape[0], w_ada.shape[1]), F32),
                          compiler_params=pltpu.CompilerParams(vmem_limit_bytes=VMEM_LIMIT))(c_all, w_ada, b_sh)


def _mod_wgrad(c_all, dmod_sh, name):
    def body(c_ref, d_ref, o_ref):
        o_ref[...] = _dot(_silu(c_ref[...]), d_ref[...], "tn")

    return pl.pallas_call(body, name=name, out_shape=jax.ShapeDtypeStruct((c_all.shape[1], dmod_sh.shape[1]), F32),
                          compiler_params=pltpu.CompilerParams(vmem_limit_bytes=VMEM_LIMIT))(c_all, dmod_sh)


def _pad_lanes(v):
    return jnp.pad(v, ((0, 0), (0, (-v.shape[1]) % LANES)))


def kernel(x, c, norm1_w, norm2_w, w_ada, b_ada, w_in, conv_w, conv_b, dt_bias, a_log, d_skip, ssd_norm_w, q_norm_w, k_norm_w, attn_norm_w, w_out, w_ff1, w_ff2, loss_target, m_norm1_w, m_norm2_w, m_w_ada, m_b_ada, m_w_in, m_conv_w, m_conv_b, m_dt_bias, m_a_log, m_d_skip, m_ssd_norm_w, m_q_norm_w, m_k_norm_w, m_attn_norm_w, m_w_out, m_w_ff1, m_w_ff2, v_norm1_w, v_norm2_w, v_w_ada, v_b_ada, v_w_in, v_conv_w, v_conv_b, v_dt_bias, v_a_log, v_d_skip, v_ssd_norm_w, v_q_norm_w, v_k_norm_w, v_attn_norm_w, v_w_out, v_w_ff1, v_w_ff2):
    xi, yi, ci = _coords()
    chip = 2 * xi + yi
    dev = 2 * chip + ci
    xs, tgt = x[0], loss_target[0]
    S, D = xs.shape
    DI, AD = NH_SSD * HD, NH_ATT * HD
    CC = DI + 2 * NG * NSTATE
    PW = DI + CC + 3 * AD + LANES
    DFF = w_ff1.shape[2] * 4
    MIX = DI + AD
    o_xbc, o_q, o_k, o_v, o_dt = DI, DI + CC, DI + CC + AD, DI + CC + 2 * AD, DI + CC + 3 * AD

    def half_rows(w):
        r = w.shape[0] // 2
        return lax.dynamic_slice_in_dim(w, ci * r, r, 0).astype(BF16)

    c_all, conv_w_all = _exchange8([c, conv_w[0]], False, "gather_c_conv_w")
    c_all = c_all.reshape(8, D)
    c_all = jnp.pad(c_all, ((0, 8), (0, 0)))
    nmod = w_ada.shape[2]
    b_sh = lax.dynamic_slice_in_dim(b_ada, chip * nmod, nmod, 1)
    mod_sh = _mod_fwd(c_all, w_ada[0], b_sh, "mod_fwd")
    mod_all = _exchange8([mod_sh[:8]], False, "gather_mod")[0]
    mod_me = lax.dynamic_index_in_dim(mod_all[0::2], dev, 1, keepdims=False).reshape(1, 4 * nmod)
    shift1, scale1, gate1, shift2, scale2, gate2 = [mod_me[:, i * D:(i + 1) * D] for i in range(6)]

    g_in = _exchange8([half_rows(w_in[0])], False, "gather_w_in")[0]
    rest_handle, rest_token = _spread_start([half_rows(w_out[0]), half_rows(w_ff1[0]), half_rows(w_ff2[0])], False, True,
                                            dev, chip, "gather_rest_start")
    shift1 = shift1 + rest_token[0, 0]
    wsh = w_in.shape[2]
    w_in_f = g_in.reshape(4, D, wsh).transpose(1, 0, 2).reshape(D, 4 * wsh)
    n_zx = DI + CC
    w_proj = jnp.concatenate([w_in_f[:, :n_zx], w_in_f[:, n_zx + NH_SSD:], w_in_f[:, n_zx:n_zx + NH_SSD],
                              jnp.zeros((D, LANES - NH_SSD), BF16)], axis=1)

    dtb, alog, dsk = _pad_lanes(dt_bias), _pad_lanes(a_log), _pad_lanes(d_skip)
    qw2 = jnp.concatenate([q_norm_w, q_norm_w], axis=1)
    kw2 = jnp.concatenate([k_norm_w, k_norm_w], axis=1)
    conv_w_f = conv_w_all[0::2].transpose(1, 0, 2).reshape(KCONV, CC)

    h1 = _rows("norm1", lambda r, k: ([_normmod(r[0], *k)], []), [(xs, 0, D)], [norm1_w, scale1, shift1],
               [(D, BF16)], [], S)[0]
    proj = _matmul(h1, w_proj, "nn", F32, "in_proj", tn=896)
    xbc = _conv_fwd(proj, o_xbc, CC, conv_w_f, conv_b, "conv_fwd")
    y_ssd, hsave = _ssd_fwd(xbc, proj, o_dt // LANES, dtb, alog, dsk, ssd_norm_w, "ssd_fwd")

    def qk_call(name, col0, w2, scale):
        def body(t_ref, w_ref, o_ref):
            o_ref[...] = _headnorm(t_ref[...], w_ref[...], scale)
        return pl.pallas_call(
            body, name=name, grid=(AD // LANES,),
            in_specs=[pl.BlockSpec((S, LANES), lambda j: (0, j + col0 // LANES)),
                      pl.BlockSpec((1, LANES), lambda j: (0, 0))],
            out_specs=pl.BlockSpec((S, LANES), lambda j: (0, j)),
            out_shape=jax.ShapeDtypeStruct((S, AD), F32), compiler_params=_cparams(("parallel",)),
        )(proj, w2)

    qn = qk_call("q_norm", o_q, qw2, HD ** -0.5)
    kn = qk_call("k_norm", o_k, kw2, 1.0)
    o_att, lse = _attn_fwd(qn, kn, proj, o_v // LANES, "attn_fwd")
    y_att = _rows("attn_out_norm", lambda r, k: ([_rmsw(r[0], k[0])], []), [(o_att, 0, AD)], [attn_norm_w],
                  [(AD, BF16)], [], S)[0]
    g_out, g_ff1, g_ff2 = _spread_wait(rest_handle, False, True, o_att, "gather_rest_wait")
    w_out_f = g_out.reshape(MIX, D)
    w_out_a, w_out_b = w_out_f[:DI], w_out_f[DI:]
    w_ff1_f = g_ff1.reshape(4, D, DFF // 4).transpose(1, 0, 2).reshape(D, DFF)
    w_ff2_f = g_ff2.reshape(DFF, D)
    mix_a = _matmul(y_ssd, w_out_a, "nn", F32, "out_proj_a")
    mix = _matmul(y_att, w_out_b, "nn", F32, "out_proj_b", epilogue=lambda r, e: r + e, extras=(mix_a,))
    x2, h2 = _rows("resid_norm2", lambda r, k: (list(_resid_normmod(r[0], r[1], *k)), []), [(xs, 0, D), (mix, 0, D)],
                   [gate1, norm2_w, scale2, shift2], [(D, F32), (D, BF16)], [], S)
    u = _matmul(h2, w_ff1_f, "nn", F32, "ff1")
    relu2 = lambda t: jnp.square(jnp.maximum(t, 0.0))
    ff = _matmul(u, w_ff2_f, "nn", F32, "ff2", a_fn=relu2)

    def loss_fn(r, k):
        x2_, ff_, t_ = r
        err = x2_ + k[0] * ff_ - t_
        dy_ = err * (1.0 / D)
        ls = jnp.sum(jnp.sum(0.5 * err * err, axis=1, keepdims=True), axis=0, keepdims=True) * (1.0 / D)
        return [dy_, dy_ * k[0]], [ls, jnp.sum(dy_ * ff_, axis=0, keepdims=True)]

    dy, dff, loss_p, dgate2 = _rows("loss", loss_fn, [(x2, 0, D), (ff, 0, D), (tgt, 0, D)], [gate2],
                                    [(D, F32), (D, BF16)], [(1, 1), (1, D)], S)
    du = _matmul(dff, w_ff2_f, "nt", BF16, "ff2_dx", epilogue=lambda r, e: r * (2.0 * jnp.maximum(e, 0.0)), extras=(u,))
    gw_ff2 = _matmul(u, dff, "tn", F32, "ff2_dw", a_fn=relu2, tm=DFF // 8, tn=D,
                     out_shape4=(2, 4, DFF // 8, D), out_map=lambda i, j: (i % 2, i // 2))
    gw_ff1 = _matmul(h2, du, "tn", F32, "ff1_dw", tm=D // 2, tn=DFF // 4,
                     out_shape4=(2, 4, D // 2, DFF // 4), out_map=lambda i, j: (i, j))

    def by_half_cols(g):
        r, c4 = g.shape
        return g.reshape(2, r // 2, 4, c4 // 4).transpose(0, 2, 1, 3)

    def by_half_rows(g):
        r4, cdim = g.shape
        return g.reshape(4, 2, r4 // 8, cdim).transpose(1, 0, 2, 3)

    def scatter_start(layouts, both, tag):
        theirs = _pair_swap(layouts, "pair_swap_" + tag)
        sums = []
        for i, (g2, t) in enumerate(zip(layouts, theirs)):
            _, r2, cdim = t.shape
            sm = _pair_add(g2.reshape(2, 4 * r2, cdim), t.reshape(4 * r2, cdim), ci, "pair_add_%s_%d" % (tag, i))
            sums.append(sm.reshape(4, r2, cdim))
        return _spread_start(sums, True, both, dev, chip, "scatter_%s_start" % tag)

    def scatter_wait(handle, both, after, tag):
        lands = _spread_wait(handle, True, both, after, "scatter_%s_wait" % tag)
        if not both:
            lands = _sibling_fill(lands, "scatter_%s_fill" % tag)
        return [s.reshape(4, 2 * s.shape[1], s.shape[2]) for s in lands]

    ff_handle, ff_token = scatter_start([gw_ff1, gw_ff2], True, "ff")
    dh2 = _matmul(du, w_ff1_f, "nt", F32, "ff1_dx")

    def resid_bwd(r, k):
        x_, mix_, dx2a, dh2_ = r
        _, vjp = jax.vjp(_resid_normmod, x_, mix_, *k)
        dx, dmix_, dg, dnw, dsc, dsh = vjp((dx2a, dh2_))
        return [dx, dmix_], [dg, dnw, dsc, dsh]

    dx2, dmix, dgate1, g_norm2, dscale2, dshift2 = _rows(
        "resid_norm2_bwd", resid_bwd, [(xs, 0, D), (mix, 0, D), (dy, 0, D), (dh2, 0, D)],
        [gate1 + ff_token[0, 0], norm2_w, scale2, shift2], [(D, F32), (D, BF16)], [(1, D)] * 4, S)
    gw_out = jnp.concatenate([_matmul(y_ssd, dmix, "tn", F32, "out_proj_dw_a"),
                              _matmul(y_att, dmix, "tn", F32, "out_proj_dw_b")], axis=0)
    out_handle, out_token = scatter_start([by_half_rows(gw_out)], True, "out")
    dy_ssd = _matmul(dmix, w_out_a, "nt", F32, "out_proj_dx_a")
    dy_att = _matmul(dmix, w_out_b, "nt", F32, "out_proj_dx_b")

    def attn_norm_bwd(r, k):
        o_, dyo = r
        _, vjp = jax.vjp(_rmsw, o_, k[0])
        do_, dw_ = vjp(dyo)
        lo = _lane_mask()
        dd_blocks = []
        for b in range(AD // LANES):
            t = (do_ * o_)[:, b * LANES:(b + 1) * LANES]
            s0 = jnp.sum(jnp.where(lo, t, 0.0), axis=1, keepdims=True)
            s1 = jnp.sum(jnp.where(lo, 0.0, t), axis=1, keepdims=True)
            dd_blocks.append(jnp.where(lo, s0, s1))
        return [do_, jnp.concatenate(dd_blocks, axis=1)], [dw_]

    do_att, dd_att, g_attn_norm = _rows("attn_norm_bwd", attn_norm_bwd, [(o_att, 0, AD), (dy_att, 0, AD)],
                                        [attn_norm_w + out_token[0, 0]], [(AD, F32), (AD, F32)], [(1, AD)], S)
    dq_n, dk_n, dv = _attn_bwd(qn, kn, proj, o_v // LANES, do_att, lse, dd_att, "attn_bwd")

    def qk_bwd_call(name, col0, w2, scale, g):
        def body(t_ref, w_ref, g_ref, o_ref, dw_ref):
            @pl.when(pl.program_id(0) == 0)
            def _():
                dw_ref[...] = jnp.zeros_like(dw_ref)
            _, vjp = jax.vjp(lambda t, w: _headnorm(t, w, scale), t_ref[...], w_ref[...])
            dt_, dw_ = vjp(g_ref[...])
            o_ref[...] = dt_.astype(BF16)
            dw_ref[...] += dw_
        blk = pl.BlockSpec((S, LANES), lambda j: (0, j))
        return pl.pallas_call(
            body, name=name, grid=(AD // LANES,),
            in_specs=[pl.BlockSpec((S, LANES), lambda j: (0, j + col0 // LANES)),
                      pl.BlockSpec((1, LANES), lambda j: (0, 0)), blk],
            out_specs=[blk, pl.BlockSpec((1, LANES), lambda j: (0, 0))],
            out_shape=[jax.ShapeDtypeStruct((S, AD), BF16), jax.ShapeDtypeStruct((1, LANES), F32)],
            compiler_params=_cparams(("arbitrary",)),
        )(proj, w2, g)

    dq, g_qw2 = qk_bwd_call("q_norm_bwd", o_q, qw2, HD ** -0.5, dq_n)
    dk, g_kw2 = qk_bwd_call("k_norm_bwd", o_k, kw2, 1.0, dk_n)
    g_q_norm = g_qw2[:, :HD] + g_qw2[:, HD:]
    g_k_norm = g_kw2[:, :HD] + g_kw2[:, HD:]

    dxbc, dz, ddtr, g_dtb, g_alog, g_dsk, g_ssd_norm = _ssd_bwd(
        xbc, proj, o_dt // LANES, dtb, alog, dsk, ssd_norm_w, hsave, dy_ssd, "ssd_bwd")
    dxbc_pre, g_conv_w, g_conv_b = _conv_bwd(proj, o_xbc, CC, conv_w_f, conv_b, dxbc, "conv_bwd")
    dproj = jnp.concatenate([dz.astype(BF16), dxbc_pre.astype(BF16), dq, dk, dv.astype(BF16), ddtr.astype(BF16)], axis=1)
    gw_proj = _matmul(h1, dproj, "tn", F32, "in_proj_dw", tn=896)
    gw_in = jnp.concatenate([gw_proj[:, :n_zx], gw_proj[:, o_dt:o_dt + NH_SSD], gw_proj[:, n_zx:o_dt]], axis=1)
    in_handle, in_token = scatter_start([by_half_cols(gw_in)], False, "in")
    dh1 = _matmul(dproj, w_proj, "nt", F32, "in_proj_dx", tk=896)

    def norm1_bwd(r, k):
        x_, dh_, dres = r
        _, vjp = jax.vjp(_normmod, x_, *k)
        dx, dnw, dsc, dsh = vjp(dh_)
        return [dx + dres], [dnw, dsc, dsh]

    grad_x, g_norm1, dscale1, dshift1 = _rows("norm1_bwd", norm1_bwd, [(xs, 0, D), (dh1, 0, D), (dx2, 0, D)],
                                              [norm1_w + in_token[0, 0], scale1, shift1], [(D, F32)], [(1, D)] * 3, S)
    dmod =jnp.concatenate([dshift1, dscale1, dgate1, dshift2, dscale2, dgate2], axis=1)

    small = [g_norm1, g_norm2, dmod, g_conv_b, g_dtb, g_alog, g_dsk, g_ssd_norm, _pad_lanes(g_q_norm),
             _pad_lanes(g_k_norm), g_attn_norm, g_conv_w.reshape(1, KCONV * CC)]
    sizes = [t.shape[1] for t in small]
    packed = jnp.concatenate(small, axis=1)
    nrow = -(-packed.shape[1] // LANES // 8) * 8
    packed = jnp.pad(packed, ((0, 0), (0, nrow * LANES - packed.shape[1]))).reshape(nrow, LANES)
    packed_all = _exchange8([packed], False, "gather_small_grads")[0]
    tot = _sum_parts(packed_all, "sum_small_grads").reshape(1, nrow * LANES)
    offs = [sum(sizes[:i]) for i in range(len(sizes))]
    (g_norm1, g_norm2, g_b_ada, g_conv_b, g_dtb, g_alog, g_dsk, g_ssd_norm, g_q_norm, g_k_norm, g_attn_norm,
     g_conv_w) = [tot[:, o:o + n] for o, n in zip(offs, sizes)]
    g_dtb, g_alog, g_dsk = g_dtb[:, :NH_SSD], g_alog[:, :NH_SSD], g_dsk[:, :NH_SSD]
    g_q_norm, g_k_norm = g_q_norm[:, :HD], g_k_norm[:, :HD]
    ccs = CC // 4
    g_conv_w = lax.dynamic_slice_in_dim(g_conv_w.reshape(KCONV, CC), chip * ccs, ccs, 1)

    dmod_all = packed_all.reshape(8, nrow * LANES)[:, offs[2]:offs[2] + 6 * D]
    dmod_sh = jnp.pad(lax.dynamic_slice_in_dim(dmod_all, chip * nmod, nmod, 1), ((0, 8), (0, 0)))
    gw_ada = _mod_wgrad(c_all, dmod_sh, "mod_wgrad")

    parts_ff1, parts_ff2 = scatter_wait(ff_handle, True, in_token, "ff")
    res_ff1 = _adamw(parts_ff1, w_ff1[0], m_w_ff1[0], v_w_ff1[0], "adamw_w_ff1")
    res_ff2 = _adamw(parts_ff2, w_ff2[0], m_w_ff2[0], v_w_ff2[0], "adamw_w_ff2")
    res_out = _adamw(scatter_wait(out_handle, True, in_token, "out")[0], w_out[0], m_w_out[0], v_w_out[0], "adamw_w_out")
    res_ada = _adamw(gw_ada[None], w_ada[0], m_w_ada[0], v_w_ada[0], "adamw_w_ada")
    res_in = _adamw(scatter_wait(in_handle, False, res_ada[0], "in")[0], w_in[0], m_w_in[0], v_w_in[0], "adamw_w_in")

    small_names = ["norm1_w", "norm2_w", "b_ada", "conv_w", "conv_b", "dt_bias", "a_log", "d_skip", "ssd_norm_w",
                   "q_norm_w", "k_norm_w", "attn_norm_w"]
    small_g = dict(norm1_w=g_norm1, norm2_w=g_norm2, b_ada=g_b_ada, conv_w=g_conv_w.reshape(1, KCONV * ccs),
                   conv_b=g_conv_b, dt_bias=g_dtb, a_log=g_alog, d_skip=g_dsk, ssd_norm_w=g_ssd_norm, q_norm_w=g_q_norm,
                   k_norm_w=g_k_norm, attn_norm_w=g_attn_norm)
    small_w = dict(norm1_w=(norm1_w, m_norm1_w, v_norm1_w), norm2_w=(norm2_w, m_norm2_w, v_norm2_w),
                   b_ada=(b_ada, m_b_ada, v_b_ada),
                   conv_w=tuple(t.reshape(1, KCONV * ccs) for t in (conv_w, m_conv_w, v_conv_w)),
                   conv_b=(conv_b, m_conv_b, v_conv_b), dt_bias=(dt_bias, m_dt_bias, v_dt_bias),
                   a_log=(a_log, m_a_log, v_a_log), d_skip=(d_skip, m_d_skip, v_d_skip),
                   ssd_norm_w=(ssd_norm_w, m_ssd_norm_w, v_ssd_norm_w), q_norm_w=(q_norm_w, m_q_norm_w, v_q_norm_w),
                   k_norm_w=(k_norm_w, m_k_norm_w, v_k_norm_w), attn_norm_w=(attn_norm_w, m_attn_norm_w, v_attn_norm_w))
    ssz = [_pad_lanes(small_g[n]).shape[1] for n in small_names]
    soff = [sum(ssz[:i]) for i in range(len(ssz))]
    srow = -(-sum(ssz) // LANES // 8) * 8

    def pack(ts, fill):
        t = jnp.concatenate([jnp.pad(t, ((0, 0), (0, (-t.shape[1]) % LANES)), constant_values=fill) for t in ts], axis=1)
        return jnp.pad(t, ((0, 0), (0, srow * LANES - t.shape[1])), constant_values=fill).reshape(srow, LANES)

    sg = pack([small_g[n] for n in small_names], 0.0)
    sw = pack([small_w[n][0] for n in small_names], 0.0)
    sm_ = pack([small_w[n][1] for n in small_names], 0.0)
    sv = pack([small_w[n][2] for n in small_names], 1.0)
    _, s_delta, s_m, s_v = _adamw(sg[None], sw, sm_, sv, "adamw_small", tm=srow)

    def unpack(t, n):
        i = small_names.index(n)
        return t.reshape(1, srow * LANES)[:, soff[i]:soff[i] + small_g[n].shape[1]].reshape(small_w[n][0].shape)

    loss = lax.psum(loss_p[0, 0], ("x", "y", "c"))
    big_res = dict(w_ada=res_ada, w_in=res_in, w_out=res_out, w_ff1=res_ff1, w_ff2=res_ff2)
    order = ["norm1_w", "norm2_w", "w_ada", "b_ada", "w_in", "conv_w", "conv_b", "dt_bias", "a_log", "d_skip",
             "ssd_norm_w", "q_norm_w", "k_norm_w", "attn_norm_w", "w_out", "w_ff1", "w_ff2"]
    grads, deltas, new_m, new_v = [], [], [], []
    for n in order:
        if n in big_res:
            g_, d_, m_, v_ = [t[None] for t in big_res[n]]
        else:
            g_ = small_g[n].reshape(small_w[n][0].shape)
            d_, m_, v_ = unpack(s_delta, n), unpack(s_m, n), unpack(s_v, n)
            if n == "conv_w":
                g_, d_, m_, v_ = [t.reshape(conv_w.shape) for t in (g_, d_, m_, v_)]
        grads.append(g_)
        deltas.append(d_)
        new_m.append(m_)
        new_v.append(v_)
    return (loss, grad_x[None], *grads, *deltas, *new_m, *new_v)
```

```python
import functools

import jax
import jax.numpy as jnp
from jax import lax
from jax.experimental import pallas as pl
from jax.experimental.pallas import tpu as pltpu

F32, BF16 = jnp.float32, jnp.bfloat16
EPS = 1e-6
HD = 64
NH_SSD = 16
NG = 4
NSTATE = 128
KCONV = 4
CHUNK = 128
NH_ATT = 16
PATTERNS = ((128, 1), (512, 4), (2048, 16))
ABLK = 128
QTILE = 128
LANES = 128
ADAM_LR, ADAM_B1, ADAM_B2, ADAM_EPS, ADAM_WD, ADAM_STEP = 0.001, 0.9, 0.999, 1e-08, 0.01, 10
VMEM_LIMIT = 56 * 1024 * 1024
MESH = pl.DeviceIdType.MESH
NEG = -1e30

_DN = {"nn": (((1,), (0,)), ((), ())), "nt": (((1,), (1,)), ((), ())), "tn": (((0,), (0,)), ((), ()))}


def _cparams(sem):
    return pltpu.CompilerParams(dimension_semantics=sem, vmem_limit_bytes=VMEM_LIMIT)


def _tile(n, cap):
    if n % LANES or n <= LANES:
        return n
    best = LANES
    for t in range(LANES, min(n, cap) + 1, LANES):
        if n % t == 0:
            best = t
    return best


def _silu(x):
    return x / (1.0 + jnp.exp(-x))


def _softplus(x):
    return jnp.maximum(x, 0.0) + jnp.log(1.0 + jnp.exp(-jnp.abs(x)))


def _dot(a, b, dims):
    return lax.dot_general(a.astype(BF16), b.astype(BF16), _DN[dims], preferred_element_type=F32)


def _matmul(a, b, dims, out_dtype, name, a_fn=None, epilogue=None, extras=(), tm=1024, tn=1024, tk=1024,
            out_shape4=None, out_map=None):
    if dims == "nn":
        (M, K), (_, N) = a.shape, b.shape
    elif dims == "nt":
        (M, K), (N, _) = a.shape, b.shape
    else:
        (K, M), (_, N) = a.shape, b.shape
    tm, tn, tk = _tile(M, tm), _tile(N, tn), _tile(K, tk)
    nk = K // tk
    ne = len(extras)

    def body(a_ref, b_ref, *rest):
        e_refs, o_ref = rest[:ne], rest[ne]
        av = a_ref[...]
        if a_fn is not None:
            av = a_fn(av)
        part = _dot(av, b_ref[...], dims)

        def finish(r):
            if epilogue is not None:
                r = epilogue(r, *[e[...] for e in e_refs])
            o_ref[...] = r.astype(out_dtype)

        if nk == 1:
            finish(part)
            return
        acc = rest[ne + 1]
        k = pl.program_id(2)

        @pl.when(k == 0)
        def _():
            acc[...] = part

        @pl.when(k > 0)
        def _():
            acc[...] += part

        @pl.when(k == nk - 1)
        def _():
            finish(acc[...])

    a_spec = pl.BlockSpec((tk, tm), lambda i, j, k: (k, i)) if dims == "tn" else pl.BlockSpec((tm, tk), lambda i, j, k: (i, k))
    b_spec = pl.BlockSpec((tn, tk), lambda i, j, k: (j, k)) if dims == "nt" else pl.BlockSpec((tk, tn), lambda i, j, k: (k, j))
    o_spec = pl.BlockSpec((tm, tn), lambda i, j, k: (i, j))
    out_spec, out_dims = o_spec, (M, N)
    if out_shape4 is not None:
        assert out_shape4[2:] == (tm, tn)
        out_spec = pl.BlockSpec((None, None, tm, tn), lambda i, j, k: (*out_map(i, j), 0, 0))
        out_dims = out_shape4
    return pl.pallas_call(
        body, name=name, grid=(M // tm, N // tn, nk),
        in_specs=[a_spec, b_spec] + [o_spec] * ne, out_specs=out_spec,
        out_shape=jax.ShapeDtypeStruct(out_dims, out_dtype),
        scratch_shapes=[pltpu.VMEM((tm, tn), F32)] if nk > 1 else [],
        compiler_params=_cparams(("parallel", "parallel", "arbitrary")),
    )(a, b, *extras)


def _rows(name, fn, rows, consts, outs, accs, n_rows, tm=256):
    tm = min(tm, n_rows)
    nr, nc, no, na = len(rows), len(consts), len(outs), len(accs)

    def body(*refs):
        r_refs, c_refs = refs[:nr], refs[nr:nr + nc]
        o_refs, a_refs = refs[nr + nc:nr + nc + no], refs[nr + nc + no:]
        o_vals, a_vals = fn([r[...] for r in r_refs], [c[...] for c in c_refs])
        for ref, val in zip(o_refs, o_vals):
            ref[...] = val.astype(ref.dtype)
        if na:
            @pl.when(pl.program_id(0) == 0)
            def _():
                for ref in a_refs:
                    ref[...] = jnp.zeros_like(ref)
            for ref, val in zip(a_refs, a_vals):
                ref[...] += val

    in_specs = [pl.BlockSpec((tm, w), lambda i, cb=cb: (i, cb)) for (_, cb, w) in rows]
    in_specs += [pl.BlockSpec(cst.shape, lambda i, nd=cst.ndim: (0,) * nd) for cst in consts]
    out_specs = [pl.BlockSpec((tm, w), lambda i: (i, 0)) for (w, _) in outs]
    out_specs += [pl.BlockSpec(s, lambda i: (0, 0)) for s in accs]
    out_shape = [jax.ShapeDtypeStruct((n_rows, w), dt) for (w, dt) in outs]
    out_shape += [jax.ShapeDtypeStruct(s, F32) for s in accs]
    res = pl.pallas_call(
        body, name=name, grid=(n_rows // tm,), in_specs=in_specs, out_specs=out_specs, out_shape=out_shape,
        compiler_params=_cparams(("arbitrary",)),
    )(*[r[0] for r in rows], *consts)
    return res


def _normmod(x, nw, sc, sh):
    r = lax.rsqrt(jnp.mean(x * x, axis=-1, keepdims=True) + EPS)
    return (x * r) * nw * (1.0 + sc) + sh


def _resid_normmod(x, mix, g, nw, sc, sh):
    x2 = x + g * mix
    return x2, _normmod(x2, nw, sc, sh)


def _rmsw(o, w):
    return o * lax.rsqrt(jnp.mean(o * o, axis=-1, keepdims=True) + EPS) * w


def _lane_mask():
    return lax.broadcasted_iota(jnp.int32, (1, LANES), 1) < HD


def _headnorm(t, w, scale):
    lo = _lane_mask()
    t2 = t * t
    s0 = jnp.sum(jnp.where(lo, t2, 0.0), axis=1, keepdims=True)
    s1 = jnp.sum(jnp.where(lo, 0.0, t2), axis=1, keepdims=True)
    ms = jnp.where(lo, s0, s1) * (1.0 / HD)
    return t * lax.rsqrt(ms + EPS) * w * scale


CONV_ROWS = 128
CONV_HALO = 8


def _conv_cols(n_ch):
    return _tile(n_ch, LANES)


def _conv_fwd(proj, col0, n_ch, conv_w, conv_b, name):
    S = proj.shape[0]
    tc = _conv_cols(n_ch)

    R, H = CONV_ROWS, CONV_HALO

    def body(u_ref, w_ref, b_ref, o_ref):
        w = [w_ref[i:i + 1, :] for i in range(KCONV)]
        b = b_ref[...]

        def chunk(ext):
            acc = b + w[KCONV - 1] * ext[H:]
            for i in range(KCONV - 1):
                acc = acc + w[i] * pltpu.roll(ext, KCONV - 1 - i, 0)[H:]
            return _silu(acc)

        o_ref[0:R, :] = chunk(jnp.concatenate([jnp.zeros((H, tc), F32), u_ref[0:R, :]], axis=0))

        def step(c, carry):
            r0 = pl.multiple_of(c * R, R)
            o_ref[pl.ds(r0, R), :] = chunk(u_ref[pl.ds(pl.multiple_of(r0 - H, H), R + H), :])
            return carry

        lax.fori_loop(1, S // R, step, 0)

    return pl.pallas_call(
        body, name=name, grid=(n_ch // tc,),
        in_specs=[pl.BlockSpec((S, tc), lambda j: (0, j + col0 // tc)),
                  pl.BlockSpec((KCONV, tc), lambda j: (0, j)), pl.BlockSpec((1, tc), lambda j: (0, j))],
        out_specs=pl.BlockSpec((S, tc), lambda j: (0, j)),
        out_shape=jax.ShapeDtypeStruct((S, n_ch), F32),
        compiler_params=_cparams(("parallel",)),
    )(proj, conv_w, conv_b)


def _conv_bwd(proj, col0, n_ch, conv_w, conv_b, dxbc, name):
    S = proj.shape[0]
    tc = _conv_cols(n_ch)

    R, H = CONV_ROWS, CONV_HALO

    def body(u_ref, w_ref, b_ref, g_ref, du_ref, dw_ref, db_ref):
        w = [w_ref[i:i + 1, :] for i in range(KCONV)]
        b = b_ref[...]
        pad = jnp.zeros((H, tc), F32)

        def chunk(u_ext, g_ext):
            taps = [pltpu.roll(u_ext, KCONV - 1 - i, 0)[H:] for i in range(KCONV - 1)] + [u_ext[H:]]
            acc = b
            for i in range(KCONV):
                acc = acc + w[i] * taps[i]
            sig = 1.0 / (1.0 + jnp.exp(-acc))
            dacc = g_ext * (sig * (1.0 + acc * (1.0 - sig)))
            du = w[KCONV - 1] * dacc[:R]
            for i in range(KCONV - 1):
                du = du + w[i] * pltpu.roll(dacc, R + H - (KCONV - 1 - i), 0)[:R]
            d = dacc[:R]
            return du, [jnp.sum(d * t[:R], axis=0, keepdims=True) for t in taps], jnp.sum(d, axis=0, keepdims=True)

        du, dws, db = chunk(jnp.concatenate([pad, u_ref[0:R + H, :]], axis=0), g_ref[0:R + H, :])
        du_ref[0:R, :] = du

        def step(c, carry):
            r0 = pl.multiple_of(c * R, R)
            du_c, dws_c, db_c = chunk(u_ref[pl.ds(pl.multiple_of(r0 - H, H), R + 2 * H), :], g_ref[pl.ds(r0, R + H), :])
            du_ref[pl.ds(r0, R), :] = du_c
            return [a + b_ for a, b_ in zip(carry[0], dws_c)], carry[1] + db_c

        dws, db = lax.fori_loop(1, S // R - 1, step, (dws, db))
        du, dws_l, db_l = chunk(jnp.concatenate([u_ref[S - R - H:S, :], pad], axis=0),
                                jnp.concatenate([g_ref[S - R:S, :], pad], axis=0))
        du_ref[S - R:S, :] = du
        for i in range(KCONV):
            dw_ref[i:i + 1, :] = dws[i] + dws_l[i]
        db_ref[...] = db + db_l

    return pl.pallas_call(
        body, name=name, grid=(n_ch // tc,),
        in_specs=[pl.BlockSpec((S, tc), lambda j: (0, j + col0 // tc)),
                  pl.BlockSpec((KCONV, tc), lambda j: (0, j)), pl.BlockSpec((1, tc), lambda j: (0, j)),
                  pl.BlockSpec((S, tc), lambda j: (0, j))],
        out_specs=[pl.BlockSpec((S, tc), lambda j: (0, j)), pl.BlockSpec((KCONV, tc), lambda j: (0, j)),
                   pl.BlockSpec((1, tc), lambda j: (0, j))],
        out_shape=[jax.ShapeDtypeStruct((S, n_ch), F32), jax.ShapeDtypeStruct((KCONV, n_ch), F32),
                   jax.ShapeDtypeStruct((1, n_ch), F32)],
        compiler_params=_cparams(("parallel",)),
    )(proj, conv_w, conv_b, dxbc)


@functools.partial(jax.custom_vjp, nondiff_argnums=(2,))
def _mm(a, b, dims):
    return _dot(a, b, dims)


def _mm_fwd(a, b, dims):
    return _dot(a, b, dims), (a, b)


def _mm_bwd(dims, res, g):
    a, b = res
    if dims == "nn":
        return _dot(g, b, "nt"), _dot(a, g, "tn")
    if dims == "nt":
        return _dot(g, b, "nn"), _dot(g, a, "tn")
    return _dot(b, g, "nt"), _dot(a, g, "nn")


_mm.defvjp(_mm_fwd, _mm_bwd)


def _tri_dot(x, upper):
    n = x.shape[0]
    r = lax.broadcasted_iota(jnp.int32, (n, n), 0)
    c = lax.broadcasted_iota(jnp.int32, (n, n), 1)
    t = jnp.where((r <= c) if upper else (r >= c), 1.0, 0.0)
    return lax.dot_general(t, x, _DN["nn"], precision=lax.Precision.HIGHEST, preferred_element_type=F32)


@jax.custom_vjp
def _cumsum_rows(x):
    return _tri_dot(x, False)


_cumsum_rows.defvjp(lambda x: (_tri_dot(x, False), None), lambda _, g: (_tri_dot(g, True),))


def _ssd_chunk(xs_p, bm_g, cm_g, dtr, z_p, dtb, alog, dsk, nw_p, h_p):
    L = dtr.shape[0]
    n_pairs = len(xs_p)
    ppg = n_pairs // len(bm_g)
    lane = lax.broadcasted_iota(jnp.int32, (1, LANES), 1)
    sub = lax.broadcasted_iota(jnp.int32, (LANES, 1), 0)
    lo = lane < HD
    row_l = lax.broadcasted_iota(jnp.int32, (L, 1), 0)
    tri = lax.broadcasted_iota(jnp.int32, (L, L), 0) >= lax.broadcasted_iota(jnp.int32, (L, L), 1)

    dt = _softplus(dtr + dtb)
    acs = _cumsum_rows(dt * (-jnp.exp(alog)))
    acs_t = acs.T
    a_last = jnp.sum(jnp.where(row_l == L - 1, acs, 0.0), axis=0, keepdims=True)
    e_acs = jnp.exp(acs)
    dec = jnp.exp(a_last - acs)
    cdec = jnp.exp(a_last)

    def colv(m, h):
        return jnp.sum(jnp.where(lane == h, m, 0.0), axis=1, keepdims=True)

    def rowv(mt, h):
        return jnp.sum(jnp.where(sub == h, mt, 0.0), axis=0, keepdims=True)

    def pair(m, h0):
        return jnp.where(lo, colv(m, h0), colv(m, h0 + 1))

    ys, hs = [], []
    cb = None
    for p in range(n_pairs):
        g, h0 = p // ppg, 2 * p
        bmat, cmat = bm_g[g], cm_g[g]
        if p % ppg == 0:
            cb = _mm(cmat, bmat, "nt")
        x = xs_p[p]
        xdt = x * pair(dt, h0)
        yd = []
        for h in (h0, h0 + 1):
            seg = colv(acs, h) - rowv(acs_t, h)
            lm = jnp.where(tri, jnp.exp(jnp.where(tri, seg, 0.0)), 0.0)
            yd.append(_mm(cb * lm, xdt, "nn"))
        y = jnp.where(lo, yd[0], yd[1])
        y = y + _mm(cmat, h_p[p], "nt") * pair(e_acs, h0)
        st = _mm(xdt * pair(dec, h0), bmat, "tn")
        cd_col = jnp.where(sub < HD, colv(cdec, h0), colv(cdec, h0 + 1))
        hs.append(h_p[p] * cd_col + st)
        ys.append(y + pair(dsk, h0) * x)

    y2 = [ys[p] * _silu(z_p[p]) for p in range(n_pairs)]
    outs = []
    for g in range(len(bm_g)):
        ps = range(g * ppg, (g + 1) * ppg)
        ss = sum(jnp.sum(y2[p] * y2[p], axis=1, keepdims=True) for p in ps)
        rs = lax.rsqrt(ss * (1.0 / (ppg * LANES)) + EPS)
        outs += [y2[p] * rs * nw_p[p] for p in ps]
    return outs, hs


def _ssd_slices(xbc_ref, z_ref, nw_ref, di):
    n_pairs = di // LANES
    xs_p = [xbc_ref[:, p * LANES:(p + 1) * LANES] for p in range(n_pairs)]
    bm_g = [xbc_ref[:, di + g * NSTATE:di + (g + 1) * NSTATE] for g in range(NG)]
    cm_g = [xbc_ref[:, di + (NG + g) * NSTATE:di + (NG + g + 1) * NSTATE] for g in range(NG)]
    z_p = [z_ref[:, p * LANES:(p + 1) * LANES] for p in range(n_pairs)]
    nw_p = [nw_ref[:, p * LANES:(p + 1) * LANES] for p in range(n_pairs)]
    return xs_p, bm_g, cm_g, z_p, nw_p


def _ssd_fwd(xbc, proj, dt_cb, dtb, alog, dsk, nw, name):
    S, cc = xbc.shape
    di = NH_SSD * HD
    n_pairs = di // LANES
    nchunk = S // CHUNK

    def body(xbc_ref, z_ref, dtr_ref, dtb_ref, alog_ref, dsk_ref, nw_ref, y_ref, hs_ref, h_scr):
        @pl.when(pl.program_id(0) == 0)
        def _():
            h_scr[...] = jnp.zeros_like(h_scr)

        xs_p, bm_g, cm_g, z_p, nw_p = _ssd_slices(xbc_ref, z_ref, nw_ref, di)
        h_p = [h_scr[p * LANES:(p + 1) * LANES, :] for p in range(n_pairs)]
        hs_ref[...] = h_scr[...]
        outs, hs = _ssd_chunk(xs_p, bm_g, cm_g, dtr_ref[...], z_p, dtb_ref[...], alog_ref[...], dsk_ref[...], nw_p, h_p)
        for p in range(n_pairs):
            y_ref[:, p * LANES:(p + 1) * LANES] = outs[p].astype(y_ref.dtype)
            h_scr[p * LANES:(p + 1) * LANES, :] = hs[p]

    vec = pl.BlockSpec((1, LANES), lambda c: (0, 0))
    return pl.pallas_call(
        body, name=name, grid=(nchunk,),
        in_specs=[pl.BlockSpec((CHUNK, cc), lambda c: (c, 0)), pl.BlockSpec((CHUNK, di), lambda c: (c, 0)),
                  pl.BlockSpec((CHUNK, LANES), lambda c: (c, dt_cb)), vec, vec, vec,
                  pl.BlockSpec((1, di), lambda c: (0, 0))],
        out_specs=[pl.BlockSpec((CHUNK, di), lambda c: (c, 0)), pl.BlockSpec((None, di, NSTATE), lambda c: (c, 0, 0))],
        out_shape=[jax.ShapeDtypeStruct((S, di), BF16), jax.ShapeDtypeStruct((nchunk, di, NSTATE), F32)],
        scratch_shapes=[pltpu.VMEM((di, NSTATE), F32)],
        compiler_params=_cparams(("arbitrary",)),
    )(xbc, proj, proj, dtb, alog, dsk, nw)


def _ssd_bwd(xbc, proj, dt_cb, dtb, alog, dsk, nw, hsave, dy, name):
    S, cc = xbc.shape
    di = NH_SSD * HD
    n_pairs = di // LANES
    nchunk = S // CHUNK

    def body(xbc_ref, z_ref, dtr_ref, dtb_ref, alog_ref, dsk_ref, nw_ref, hs_ref, dy_ref,
             dxbc_ref, dz_ref, ddtr_ref, ddtb_ref, dalog_ref, ddsk_ref, dnw_ref, dh_scr):
        @pl.when(pl.program_id(0) == 0)
        def _():
            dh_scr[...] = jnp.zeros_like(dh_scr)
            ddtb_ref[...] = jnp.zeros_like(ddtb_ref)
            dalog_ref[...] = jnp.zeros_like(dalog_ref)
            ddsk_ref[...] = jnp.zeros_like(ddsk_ref)
            dnw_ref[...] = jnp.zeros_like(dnw_ref)

        xs_p, bm_g, cm_g, z_p, nw_p = _ssd_slices(xbc_ref, z_ref, nw_ref, di)
        h_p = [hs_ref[p * LANES:(p + 1) * LANES, :] for p in range(n_pairs)]
        dy_p = [dy_ref[:, p * LANES:(p + 1) * LANES].astype(F32) for p in range(n_pairs)]
        dh_p = [dh_scr[p * LANES:(p + 1) * LANES, :] for p in range(n_pairs)]
        _, vjp = jax.vjp(_ssd_chunk, xs_p, bm_g, cm_g, dtr_ref[...], z_p, dtb_ref[...], alog_ref[...], dsk_ref[...],
                         nw_p, h_p)
        dxs, dbm, dcm, ddtr, dz, ddtb, dalog, ddsk, dnw, dh = vjp((dy_p, dh_p))
        for p in range(n_pairs):
            sl = slice(p * LANES, (p + 1) * LANES)
            dxbc_ref[:, sl] = dxs[p]
            dz_ref[:, sl] = dz[p]
            dnw_ref[:, sl] += dnw[p]
            dh_scr[sl, :] = dh[p]
        for g in range(NG):
            dxbc_ref[:, di + g * NSTATE:di + (g + 1) * NSTATE] = dbm[g]
            dxbc_ref[:, di + (NG + g) * NSTATE:di + (NG + g + 1) * NSTATE] = dcm[g]
        ddtr_ref[...] = ddtr
        ddtb_ref[...] += ddtb
        dalog_ref[...] += dalog
        ddsk_ref[...] += ddsk

    last = nchunk - 1
    vec = pl.BlockSpec((1, LANES), lambda c: (0, 0))
    return pl.pallas_call(
        body, name=name, grid=(nchunk,),
        in_specs=[pl.BlockSpec((CHUNK, cc), lambda c: (last - c, 0)), pl.BlockSpec((CHUNK, di), lambda c: (last - c, 0)),
                  pl.BlockSpec((CHUNK, LANES), lambda c: (last - c, dt_cb)), vec, vec, vec,
                  pl.BlockSpec((1, di), lambda c: (0, 0)),
                  pl.BlockSpec((None, di, NSTATE), lambda c: (last - c, 0, 0)),
                  pl.BlockSpec((CHUNK, di), lambda c: (last - c, 0))],
        out_specs=[pl.BlockSpec((CHUNK, cc), lambda c: (last - c, 0)), pl.BlockSpec((CHUNK, di), lambda c: (last - c, 0)),
                   pl.BlockSpec((CHUNK, LANES), lambda c: (last - c, 0)), vec, vec, vec,
                   pl.BlockSpec((1, di), lambda c: (0, 0))],
        out_shape=[jax.ShapeDtypeStruct((S, cc), F32), jax.ShapeDtypeStruct((S, di), F32),
                   jax.ShapeDtypeStruct((S, LANES), F32), jax.ShapeDtypeStruct((1, LANES), F32),
                   jax.ShapeDtypeStruct((1, LANES), F32), jax.ShapeDtypeStruct((1, LANES), F32),
                   jax.ShapeDtypeStruct((1, di), F32)],
        scratch_shapes=[pltpu.VMEM((di, NSTATE), F32)],
        compiler_params=_cparams(("arbitrary",)),
    )(xbc, proj, proj, dtb, alog, dsk, nw, hsave, dy)


def _band_masks(rows_q, rows_k):
    qi = lax.broadcasted_iota(jnp.int32, (rows_q, rows_k), 0)
    ki = lax.broadcasted_iota(jnp.int32, (rows_q, rows_k), 1)
    return qi, ki


def _class_chunks(n_rows, d):
    per_class = n_rows // d
    ch = min(per_class, 256)
    out = []
    for r in range(d):
        for c0 in range(0, per_class, ch):
            tok = pl.ds(c0, ch) if d == 1 else pl.ds(r + d * c0, ch, stride=d)
            out.append((tok, pl.ds(r * per_class + c0, ch)))
    return out


def _to_class_order(src_ref, dst_ref, n_rows, d):
    for tok, cls in _class_chunks(n_rows, d):
        dst_ref[cls, :] = src_ref[tok, :].astype(dst_ref.dtype)


def _blk_rows(t):
    return pl.ds(pl.multiple_of(t * ABLK, ABLK), ABLK)


def _head_lanes(msk, t, t_rolled):
    return jnp.where(msk, t, t_rolled)


def _zero_unless(msk, t):
    return jnp.where(msk, t, jnp.zeros_like(t))


def _attn_fwd(qn, kn, proj, v_cb, name):
    S, ad = qn.shape
    nb = S // ABLK
    nbr = len(PATTERNS)

    def body(q_ref, k_ref, v_ref, o_ref, lse_ref, qc, kc, vc, ob, mb, lb, m_s, l_s):
        lo = _lane_mask()
        qi, ki = _band_masks(ABLK, 2 * ABLK)
        band, in_cur, prev_ok = ki <= qi + ABLK, ki >= ABLK, ki >= qi
        for bi, (_, d) in enumerate(PATTERNS):
            nbc = S // d // ABLK
            first, last = bi == 0, bi == nbr - 1
            qs, ks, vs = q_ref, k_ref, v_ref
            if d > 1:
                qs, ks, vs = qc, kc, vc
                for src, dst in ((q_ref, qc), (k_ref, kc), (v_ref, vc)):
                    _to_class_order(src, dst, S, d)
            o_dst, m_dst, l_dst = (o_ref, m_s, l_s) if first else (ob, mb, lb)

            def blk(t, carry, nbc=nbc, qs=qs, ks=ks, vs=vs, o_dst=o_dst, m_dst=m_dst, l_dst=l_dst):
                rows, prow = _blk_rows(t), _blk_rows(jnp.maximum(t - 1, 0))
                has_prev = (t % nbc) != 0
                kk = jnp.concatenate([ks[prow, :], ks[rows, :]], axis=0)
                vv = jnp.concatenate([vs[prow, :], vs[rows, :]], axis=0)
                for u in range(ABLK // QTILE):
                    sub = pl.ds(pl.multiple_of(t * ABLK + u * QTILE, QTILE), QTILE)
                    sl = slice(u * QTILE, (u + 1) * QTILE)
                    valid = band[sl] & (in_cur[sl] | (prev_ok[sl] & has_prev))
                    qv = qs[sub, :]
                    os_, ms_, ls_ = [], [], []
                    for msk in (lo, jnp.logical_not(lo)):
                        s = jnp.where(valid, _dot(_zero_unless(msk, qv), kk, "nt"), NEG)
                        m = jnp.max(s, axis=1, keepdims=True)
                        p = jnp.exp(s - m)
                        os_.append(_dot(p, vv, "nn"))
                        ms_.append(m)
                        ls_.append(jnp.sum(p, axis=1, keepdims=True))
                    o_dst[sub, :] = jnp.where(lo, os_[0], os_[1])
                    m_dst[sub, :] = jnp.where(lo, ms_[0], ms_[1])
                    l_dst[sub, :] = jnp.where(lo, ls_[0], ls_[1])
                return carry

            lax.fori_loop(0, nb, blk, 0, unroll=8)
            if first:
                continue
            for tok, cls in _class_chunks(S, d):
                m_old, m_b = m_s[tok, :], mb[cls, :]
                m_new = jnp.maximum(m_old, m_b)
                a, b = jnp.exp(m_old - m_new), jnp.exp(m_b - m_new)
                l_new = a * l_s[tok, :] + b * lb[cls, :]
                o_new = a * o_ref[tok, :] + b * ob[cls, :]
                if last:
                    o_ref[tok, :] = o_new / l_new
                    lse_ref[tok, :] = m_new + jnp.log(l_new)
                else:
                    o_ref[tok, :] = o_new
                    m_s[tok, :] = m_new
                    l_s[tok, :] = l_new

    col = pl.BlockSpec((S, LANES), lambda h: (0, h))
    return pl.pallas_call(
        body, name=name, grid=(ad // LANES,),
        in_specs=[col, col, pl.BlockSpec((S, LANES), lambda h: (0, h + v_cb))], out_specs=[col, col],
        out_shape=[jax.ShapeDtypeStruct((S, ad), F32), jax.ShapeDtypeStruct((S, ad), F32)],
        scratch_shapes=[pltpu.VMEM((S, LANES), BF16)] * 3 + [pltpu.VMEM((S, LANES), F32)] * 5,
        compiler_params=_cparams(("parallel",)),
    )(qn, kn, proj)


def _attn_bwd(qn, kn, proj, v_cb, do, lse, dd, name):
    S, ad = qn.shape
    nb = S // ABLK

    def body(q_ref, k_ref, v_ref, do_ref, lse_ref, dd_ref, dq_ref, dk_ref, dv_ref,
             qc, kc, vc, doc, lsec, ddc, dqc, dkc, dvc):
        lo = _lane_mask()
        qi, ki = _band_masks(ABLK, ABLK)
        cur_ok, prev_ok = ki <= qi, ki >= qi
        for bi, (_, d) in enumerate(PATTERNS):
            nbc = S // d // ABLK
            first = bi == 0
            token_order = (q_ref, k_ref, v_ref, do_ref, lse_ref, dd_ref)
            class_order = (qc, kc, vc, doc, lsec, ddc)
            if d > 1:
                for src, dst in zip(token_order, class_order):
                    _to_class_order(src, dst, S, d)
            qs, ks, vs, dos, lses, dds = class_order if d > 1 else token_order
            dq_dst, dk_dst, dv_dst = (dq_ref, dk_ref, dv_ref) if first else (dqc, dkc, dvc)
            dk_dst[...] = jnp.zeros_like(dk_dst)
            dv_dst[...] = jnp.zeros_like(dv_dst)

            def blk(t, carry, nbc=nbc, qs=qs, ks=ks, vs=vs, dos=dos, lses=lses, dds=dds,
                    dq_dst=dq_dst, dk_dst=dk_dst, dv_dst=dv_dst):
                rows, prow = _blk_rows(t), _blk_rows(jnp.maximum(t - 1, 0))
                has_prev = (t % nbc) != 0
                qv, dov, lse_b, dd_b = qs[rows, :], dos[rows, :], lses[rows, :], dds[rows, :]
                lse_r, dd_r = pltpu.roll(lse_b, HD, 1), pltpu.roll(dd_b, HD, 1)
                nlo = jnp.logical_not(lo)
                q2 = jnp.concatenate([_zero_unless(lo, qv), _zero_unless(nlo, qv)], axis=0).astype(BF16)
                do2 = jnp.concatenate([_zero_unless(lo, dov), _zero_unless(nlo, dov)], axis=0).astype(BF16)
                lse2 = jnp.concatenate([_head_lanes(lo, lse_b, lse_r), _head_lanes(nlo, lse_b, lse_r)], axis=0)
                dd2 = jnp.concatenate([_head_lanes(lo, dd_b, dd_r), _head_lanes(nlo, dd_b, dd_r)], axis=0)
                dq2 = None
                for krows, vmask in ((rows, cur_ok), (prow, prev_ok & has_prev)):
                    kv, vv = ks[krows, :], vs[krows, :]
                    vmask2 = jnp.concatenate([vmask, vmask], axis=0)
                    s = jnp.where(vmask2, _dot(q2, kv, "nt"), NEG)
                    p = jnp.exp(s - lse2)
                    ds = p * (_dot(do2, vv, "nt") - dd2)
                    dqk = _dot(ds, kv, "nn")
                    dq2 = dqk if dq2 is None else dq2 + dqk
                    dv_dst[krows, :] += _dot(p, do2, "tn")
                    dk_dst[krows, :] += _dot(ds, q2, "tn")
                dq_dst[rows, :] = jnp.where(lo, dq2[:ABLK], dq2[ABLK:])
                return carry

            lax.fori_loop(0, nb, blk, 0, unroll=4)
            if not first:
                for tok, cls in _class_chunks(S, d):
                    dq_ref[tok, :] = dq_ref[tok, :] + dqc[cls, :]
                    dk_ref[tok, :] = dk_ref[tok, :] + dkc[cls, :]
                    dv_ref[tok, :] = dv_ref[tok, :] + dvc[cls, :]

    col = pl.BlockSpec((S, LANES), lambda h: (0, h))
    col1 = pl.BlockSpec((S, LANES), lambda h: (0, h), pipeline_mode=pl.Buffered(1))
    vcol1 = pl.BlockSpec((S, LANES), lambda h: (0, h + v_cb), pipeline_mode=pl.Buffered(1))
    return pl.pallas_call(
        body, name=name, grid=(ad // LANES,),
        in_specs=[col, col, vcol1, col1, col1, col1], out_specs=[col, col, col],
        out_shape=[jax.ShapeDtypeStruct((S, ad), F32)] * 3,
        scratch_shapes=[pltpu.VMEM((S, LANES), BF16)] * 4 + [pltpu.VMEM((S, LANES), F32)] * 5,
        compiler_params=_cparams(("parallel",)),
    )(qn, kn, proj, do, lse, dd)


def _coords():
    return lax.axis_index("x"), lax.axis_index("y"), lax.axis_index("c")


def _exchange8(xs, per_dest, name):
    n = len(xs)
    blk = [x.shape[1:] if per_dest else x.shape for x in xs]

    def body(*refs):
        ins, outs = refs[:n], refs[n:2 * n]
        send_sems, recv_sems, local_sems = refs[2 * n:]
        x, y, c = _coords()
        sibling = (x, y, 1 - c)
        chips = [(1 - x, y), (x, 1 - y), (1 - x, 1 - y)]
        first, passed, mine = [], [], []
        for a in range(n):
            def src_for(cx, cy, a=a):
                return ins[a].at[2 * cx + cy] if per_dest else ins[a]

            def slot(px, py, pc, a=a):
                return outs[a].at[4 * px + 2 * py + pc]

            def copy(k, src, dst, to, a=a):
                return pltpu.make_async_remote_copy(src_ref=src, dst_ref=dst, send_sem=send_sems.at[7 * a + k],
                                                    recv_sem=recv_sems.at[7 * a + k], device_id=to, device_id_type=MESH)

            m = pltpu.make_async_copy(src_for(x, y), slot(x, y, c), local_sems.at[a])
            m.start()
            mine.append(m)
            cps = [copy(0, src_for(x, y), slot(x, y, c), sibling)]
            cps += [copy(1 + j, src_for(*chip), slot(x, y, c), (*chip, c)) for j, chip in enumerate(chips)]
            for cp in cps:
                cp.start()
            first += cps
        for a in range(n):
            def slot(px, py, pc, a=a):
                return outs[a].at[4 * px + 2 * py + pc]

            def copy(k, src, dst, to, a=a):
                return pltpu.make_async_remote_copy(src_ref=src, dst_ref=dst, send_sem=send_sems.at[7 * a + k],
                                                    recv_sem=recv_sems.at[7 * a + k], device_id=to, device_id_type=MESH)

            for j, chip in enumerate(chips):
                copy(1 + j, slot(*chip, c), slot(*chip, c), (*chip, c)).wait_recv()
                fw = copy(4 + j, slot(*chip, c), slot(*chip, c), sibling)
                fw.start()
                passed.append(fw)
        for a in range(n):
            def slot(px, py, pc, a=a):
                return outs[a].at[4 * px + 2 * py + pc]

            def copy(k, src, dst, to, a=a):
                return pltpu.make_async_remote_copy(src_ref=src, dst_ref=dst, send_sem=send_sems.at[7 * a + k],
                                                    recv_sem=recv_sems.at[7 * a + k], device_id=to, device_id_type=MESH)

            copy(0, slot(x, y, 1 - c), slot(x, y, 1 - c), sibling).wait_recv()
            for j, chip in enumerate(chips):
                copy(4 + j, slot(*chip, 1 - c), slot(*chip, 1 - c), sibling).wait_recv()
        for cp in first + passed:
            cp.wait_send()
        for m in mine:
            m.wait()

    anyspec = pl.BlockSpec(memory_space=pl.ANY)
    res = pl.pallas_call(
        body, name=name, in_specs=[anyspec] * n, out_specs=[anyspec] * n,
        out_shape=[jax.ShapeDtypeStruct((8,) + tuple(b), x.dtype) for b, x in zip(blk, xs)],
        scratch_shapes=[pltpu.SemaphoreType.DMA((7 * n,)), pltpu.SemaphoreType.DMA((7 * n,)),
                        pltpu.SemaphoreType.DMA((n,))],
    )(*xs)
    return list(res)


def _pair_swap(xs, name):
    n = len(xs)

    def body(*refs):
        ins, outs = refs[:n], refs[n:2 * n]
        send_sems, recv_sems = refs[2 * n:]
        x, y, c = _coords()
        cps = [pltpu.make_async_remote_copy(src_ref=ins[a].at[1 - c], dst_ref=outs[a], send_sem=send_sems.at[a],
                                            recv_sem=recv_sems.at[a], device_id=(x, y, 1 - c), device_id_type=MESH)
               for a in range(n)]
        for cp in cps:
            cp.start()
        for cp in cps:
            cp.wait()

    anyspec = pl.BlockSpec(memory_space=pl.ANY)
    res = pl.pallas_call(
        body, name=name, in_specs=[anyspec] * n, out_specs=[anyspec] * n,
        out_shape=[jax.ShapeDtypeStruct(x.shape[1:], x.dtype) for x in xs],
        scratch_shapes=[pltpu.SemaphoreType.DMA((n,)), pltpu.SemaphoreType.DMA((n,))],
    )(*xs)
    return list(res)


_HBM = pl.BlockSpec(memory_space=pltpu.HBM)
_SEM = pl.BlockSpec(memory_space=pltpu.SEMAPHORE)
_EFFECT = pltpu.SideEffectType.DATAFLOW_SIDE_EFFECTING


def _n_peers(both):
    return 7 if both else 3


def _peer(x, y, c, j, both):
    bits = j + 1 if both else 2 * (j + 1)
    dx, dy, dc = bits >> 2 & 1, bits >> 1 & 1, bits & 1
    return (1 - x if dx else x, 1 - y if dy else y, 1 - c if dc else c)


def _spread_copies(s_refs, l_refs, send_sems, recv_sems, per_dest, both):
    x, y, c = _coords()
    me = 4 * x + 2 * y + c
    npeer = _n_peers(both)
    cps = []
    for a in range(len(s_refs)):
        for j in range(npeer):
            tx, ty, tc = _peer(x, y, c, j, both)
            src = s_refs[a].at[2 * tx + ty] if per_dest else s_refs[a]
            cps.append(pltpu.make_async_remote_copy(src_ref=src, dst_ref=l_refs[a].at[me],
                                                    send_sem=send_sems.at[npeer * a + j],
                                                    recv_sem=recv_sems.at[npeer * a + j], device_id=(tx, ty, tc),
                                                    device_id_type=MESH))
    return cps


def _sibling_fill(lands, name):
    n = len(lands)

    def body(*refs):
        outs, send_sems, recv_sems = refs[n:2 * n], refs[2 * n], refs[2 * n + 1]
        x, y, c = _coords()
        cps = [pltpu.make_async_remote_copy(src_ref=outs[a].at[2 * k + c], dst_ref=outs[a].at[2 * k + c],
                                            send_sem=send_sems.at[4 * a + k], recv_sem=recv_sems.at[4 * a + k],
                                            device_id=(x, y, 1 - c), device_id_type=MESH)
               for a in range(n) for k in range(4)]
        for cp in cps:
            cp.start()
        for cp in cps:
            cp.wait()

    anyspec = pl.BlockSpec(memory_space=pl.ANY)
    res = pl.pallas_call(
        body, name=name, in_specs=[anyspec] * n, out_specs=[anyspec] * n,
        out_shape=[jax.ShapeDtypeStruct(t.shape, t.dtype) for t in lands], input_output_aliases={i: i for i in range(n)},
        scratch_shapes=[pltpu.SemaphoreType.DMA((4 * n,)), pltpu.SemaphoreType.DMA((4 * n,))],
    )(*lands)
    return list(res)


def _spread_start(srcs, per_dest, both, dev, chip, name):
    n = len(srcs)
    npeer = _n_peers(both)
    lands = []
    for s in srcs:
        own = lax.dynamic_index_in_dim(s, chip, 0, keepdims=False) if per_dest else s
        lands.append(lax.dynamic_update_index_in_dim(lax.empty((8,) + own.shape, own.dtype), own, dev, 0))

    def body(*refs):
        s_refs, l_refs, send_sems, recv_sems, token = refs[:n], refs[n:2 * n], refs[2 * n], refs[2 * n + 1], refs[-1]
        for cp in _spread_copies(s_refs, l_refs, send_sems, recv_sems, per_dest, both):
            cp.start()
        token[...] = jnp.zeros_like(token)

    hbm_in = [pltpu.with_memory_space_constraint(t, pltpu.HBM) for t in list(srcs) + lands]
    outs = pl.pallas_call(
        body, name=name,
        out_shape=(pltpu.SemaphoreType.DMA((npeer * n,)), pltpu.SemaphoreType.DMA((npeer * n,)),
                   *[pltpu.HBM(t.shape, t.dtype) for t in hbm_in], jax.ShapeDtypeStruct((8, LANES), F32)),
        in_specs=[_HBM] * (2 * n), out_specs=(_SEM, _SEM, *[_HBM] * (2 * n), pl.BlockSpec(memory_space=pltpu.VMEM)),
        input_output_aliases={i: 2 + i for i in range(2 * n)},
        compiler_params=pltpu.CompilerParams(has_side_effects=_EFFECT),
    )(*hbm_in)
    return (outs[0], outs[1], list(outs[2:2 + n]), list(outs[2 + n:2 + 2 * n])), outs[-1]


def _spread_wait(handle, per_dest, both, after, name):
    send_sems, recv_sems, srcs, lands = handle
    n = len(srcs)

    def body(*refs):
        s_refs, l_refs, send_ref, recv_ref = refs[:n], refs[n:2 * n], refs[2 * n], refs[2 * n + 1]
        for cp in _spread_copies(s_refs, l_refs, send_ref, recv_ref, per_dest, both):
            cp.wait_send()
            cp.wait_recv()

    outs = pl.pallas_call(
        body, name=name, out_shape=tuple(pltpu.HBM(t.shape, t.dtype) for t in srcs + lands),
        in_specs=[_HBM] * (2 * n) + [_SEM, _SEM, pl.BlockSpec(memory_space=pl.ANY)], out_specs=tuple([_HBM] * (2 * n)),
        input_output_aliases={i: i for i in range(2 * n)},
        compiler_params=pltpu.CompilerParams(has_side_effects=_EFFECT),
    )(*srcs, *lands, send_sems, recv_sems, after)
    return list(outs[n:])


def _row_tile(n, cap, mult):
    best = n
    for t in range(mult, min(n, cap) + 1, mult):
        if n % t == 0:
            best = t
    return best


def _pair_add(g2, theirs, half, name):
    _, n, cdim = g2.shape
    tm = _row_tile(n, 512, 16)

    def body(h_ref, a_ref, b_ref, o_ref):
        o_ref[...] = (a_ref[...] + b_ref[...]).astype(o_ref.dtype)

    grid_spec = pltpu.PrefetchScalarGridSpec(
        num_scalar_prefetch=1, grid=(n // tm,),
        in_specs=[pl.BlockSpec((None, tm, cdim), lambda i, h: (h[0], i, 0)), pl.BlockSpec((tm, cdim), lambda i, h: (i, 0))],
        out_specs=pl.BlockSpec((tm, cdim), lambda i, h: (i, 0)))
    return pl.pallas_call(body, name=name, grid_spec=grid_spec, out_shape=jax.ShapeDtypeStruct((n, cdim), BF16),
                          compiler_params=_cparams(("parallel",)))(half.reshape(1).astype(jnp.int32), g2, theirs)


def _adamw_math(w, g, m, v):
    m = ADAM_B1 * m + (1.0 - ADAM_B1) * g
    v = ADAM_B2 * v + (1.0 - ADAM_B2) * (g * g)
    m_hat = m / (1.0 - ADAM_B1 ** ADAM_STEP)
    v_hat = v / (1.0 - ADAM_B2 ** ADAM_STEP)
    delta = -ADAM_LR * (m_hat / (jnp.sqrt(v_hat) + ADAM_EPS) + ADAM_WD * w)
    return delta, m, v


def _adamw(parts, w, m, v, name, tm=128):
    npart, R, C = parts.shape
    tm = min(tm, R)

    def body(p_ref, w_ref, m_ref, v_ref, g_out, d_out, m_out, v_out):
        g = p_ref[0].astype(F32)
        for i in range(1, npart):
            g = g + p_ref[i].astype(F32)
        d, mm, vv = _adamw_math(w_ref[...], g, m_ref[...], v_ref[...])
        g_out[...] = g
        d_out[...] = d
        m_out[...] = mm
        v_out[...] = vv

    spec = pl.BlockSpec((tm, C), lambda i: (i, 0))
    return pl.pallas_call(
        body, name=name, grid=(R // tm,),
        in_specs=[pl.BlockSpec((npart, tm, C), lambda i: (0, i, 0)), spec, spec, spec], out_specs=[spec] * 4,
        out_shape=[jax.ShapeDtypeStruct((R, C), F32)] * 4,
        compiler_params=_cparams(("parallel",)),
    )(parts, w, m, v)


def _sum_parts(parts, name):
    npart, R, C = parts.shape

    def body(p_ref, o_ref):
        g = p_ref[0]
        for i in range(1, npart):
            g = g + p_ref[i]
        o_ref[...] = g

    return pl.pallas_call(body, name=name, out_shape=jax.ShapeDtypeStruct((R, C), F32))(parts)


def _mod_fwd(c_all, w_ada, b_sh, name):
    def body(c_ref, w_ref, b_ref, o_ref):
        o_ref[...] = _dot(_silu(c_ref[...]), w_ref[...], "nn") + b_ref[...]

    return pl.pallas_call(body, name=name, out_shape=jax.ShapeDtypeStruct((c_all.shape[0], w_ada.shape[1]), F32),
                          compiler_params=pltpu.CompilerParams(vmem_limit_bytes=VMEM_LIMIT))(c_all, w_ada, b_sh)


def _mod_wgrad(c_all, dmod_sh, name):
    def body(c_ref, d_ref, o_ref):
        o_ref[...] = _dot(_silu(c_ref[...]), d_ref[...], "tn")

    return pl.pallas_call(body, name=name, out_shape=jax.ShapeDtypeStruct((c_all.shape[1], dmod_sh.shape[1]), F32),
                          compiler_params=pltpu.CompilerParams(vmem_limit_bytes=VMEM_LIMIT))(c_all, dmod_sh)


def _pad_lanes(v):
    return jnp.pad(v, ((0, 0), (0, (-v.shape[1]) % LANES)))


def kernel(x, c, norm1_w, norm2_w, w_ada, b_ada, w_in, conv_w, conv_b, dt_bias, a_log, d_skip, ssd_norm_w, q_norm_w, k_norm_w, attn_norm_w, w_out, w_ff1, w_ff2, loss_target, m_norm1_w, m_norm2_w, m_w_ada, m_b_ada, m_w_in, m_conv_w, m_conv_b, m_dt_bias, m_a_log, m_d_skip, m_ssd_norm_w, m_q_norm_w, m_k_norm_w, m_attn_norm_w, m_w_out, m_w_ff1, m_w_ff2, v_norm1_w, v_norm2_w, v_w_ada, v_b_ada, v_w_in, v_conv_w, v_conv_b, v_dt_bias, v_a_log, v_d_skip, v_ssd_norm_w, v_q_norm_w, v_k_norm_w, v_attn_norm_w, v_w_out, v_w_ff1, v_w_ff2):
    xi, yi, ci = _coords()
    chip = 2 * xi + yi
    dev = 2 * chip + ci
    xs, tgt = x[0], loss_target[0]
    S, D = xs.shape
    DI, AD = NH_SSD * HD, NH_ATT * HD
    CC = DI + 2 * NG * NSTATE
    PW = DI + CC + 3 * AD + LANES
    DFF = w_ff1.shape[2] * 4
    MIX = DI + AD
    o_xbc, o_q, o_k, o_v, o_dt = DI, DI + CC, DI + CC + AD, DI + CC + 2 * AD, DI + CC + 3 * AD

    def half_rows(w):
        r = w.shape[0] // 2
        return lax.dynamic_slice_in_dim(w, ci * r, r, 0).astype(BF16)

    c_all, conv_w_all = _exchange8([c, conv_w[0]], False, "gather_c_conv_w")
    c_all = c_all.reshape(8, D)
    c_all = jnp.pad(c_all, ((0, 8), (0, 0)))
    nmod = w_ada.shape[2]
    b_sh = lax.dynamic_slice_in_dim(b_ada, chip * nmod, nmod, 1)
    mod_sh = _mod_fwd(c_all, w_ada[0], b_sh, "mod_fwd")
    mod_all = _exchange8([mod_sh[:8]], False, "gather_mod")[0]
    mod_me = lax.dynamic_index_in_dim(mod_all[0::2], dev, 1, keepdims=False).reshape(1, 4 * nmod)
    shift1, scale1, gate1, shift2, scale2, gate2 = [mod_me[:, i * D:(i + 1) * D] for i in range(6)]

    g_in = _exchange8([half_rows(w_in[0])], False, "gather_w_in")[0]
    rest_handle, rest_token = _spread_start([half_rows(w_out[0]), half_rows(w_ff1[0]), half_rows(w_ff2[0])], False, True,
                                            dev, chip, "gather_rest_start")
    shift1 = shift1 + rest_token[0, 0]
    wsh = w_in.shape[2]
    w_in_f = g_in.reshape(4, D, wsh).transpose(1, 0, 2).reshape(D, 4 * wsh)
    n_zx = DI + CC
    w_proj = jnp.concatenate([w_in_f[:, :n_zx], w_in_f[:, n_zx + NH_SSD:], w_in_f[:, n_zx:n_zx + NH_SSD],
                              jnp.zeros((D, LANES - NH_SSD), BF16)], axis=1)

    dtb, alog, dsk = _pad_lanes(dt_bias), _pad_lanes(a_log), _pad_lanes(d_skip)
    qw2 = jnp.concatenate([q_norm_w, q_norm_w], axis=1)
    kw2 = jnp.concatenate([k_norm_w, k_norm_w], axis=1)
    conv_w_f = conv_w_all[0::2].transpose(1, 0, 2).reshape(KCONV, CC)

    h1 = _rows("norm1", lambda r, k: ([_normmod(r[0], *k)], []), [(xs, 0, D)], [norm1_w, scale1, shift1],
               [(D, BF16)], [], S)[0]
    proj = _matmul(h1, w_proj, "nn", F32, "in_proj", tn=896)
    xbc = _conv_fwd(proj, o_xbc, CC, conv_w_f, conv_b, "conv_fwd")
    y_ssd, hsave = _ssd_fwd(xbc, proj, o_dt // LANES, dtb, alog, dsk, ssd_norm_w, "ssd_fwd")

    def qk_call(name, col0, w2, scale):
        def body(t_ref, w_ref, o_ref):
            o_ref[...] = _headnorm(t_ref[...], w_ref[...], scale)
        return pl.pallas_call(
            body, name=name, grid=(AD // LANES,),
            in_specs=[pl.BlockSpec((S, LANES), lambda j: (0, j + col0 // LANES)),
                      pl.BlockSpec((1, LANES), lambda j: (0, 0))],
            out_specs=pl.BlockSpec((S, LANES), lambda j: (0, j)),
            out_shape=jax.ShapeDtypeStruct((S, AD), F32), compiler_params=_cparams(("parallel",)),
        )(proj, w2)

    qn = qk_call("q_norm", o_q, qw2, HD ** -0.5)
    kn = qk_call("k_norm", o_k, kw2, 1.0)
    o_att, lse = _attn_fwd(qn, kn, proj, o_v // LANES, "attn_fwd")
    y_att = _rows("attn_out_norm", lambda r, k: ([_rmsw(r[0], k[0])], []), [(o_att, 0, AD)], [attn_norm_w],
                  [(AD, BF16)], [], S)[0]
    g_out, g_ff1, g_ff2 = _spread_wait(rest_handle, False, True, o_att, "gather_rest_wait")
    w_out_f = g_out.reshape(MIX, D)
    w_out_a, w_out_b = w_out_f[:DI], w_out_f[DI:]
    w_ff1_f = g_ff1.reshape(4, D, DFF // 4).transpose(1, 0, 2).reshape(D, DFF)
    w_ff2_f = g_ff2.reshape(DFF, D)
    mix_a = _matmul(y_ssd, w_out_a, "nn", F32, "out_proj_a")
    mix = _matmul(y_att, w_out_b, "nn", F32, "out_proj_b", epilogue=lambda r, e: r + e, extras=(mix_a,))
    x2, h2 = _rows("resid_norm2", lambda r, k: (list(_resid_normmod(r[0], r[1], *k)), []), [(xs, 0, D), (mix, 0, D)],
                   [gate1, norm2_w, scale2, shift2], [(D, F32), (D, BF16)], [], S)
    u = _matmul(h2, w_ff1_f, "nn", F32, "ff1")
    relu2 = lambda t: jnp.square(jnp.maximum(t, 0.0))
    ff = _matmul(u, w_ff2_f, "nn", F32, "ff2", a_fn=relu2)

    def loss_fn(r, k):
        x2_, ff_, t_ = r
        err = x2_ + k[0] * ff_ - t_
        dy_ = err * (1.0 / D)
        ls = jnp.sum(jnp.sum(0.5 * err * err, axis=1, keepdims=True), axis=0, keepdims=True) * (1.0 / D)
        return [dy_, dy_ * k[0]], [ls, jnp.sum(dy_ * ff_, axis=0, keepdims=True)]

    dy, dff, loss_p, dgate2 = _rows("loss", loss_fn, [(x2, 0, D), (ff, 0, D), (tgt, 0, D)], [gate2],
                                    [(D, F32), (D, BF16)], [(1, 1), (1, D)], S)
    du = _matmul(dff, w_ff2_f, "nt", BF16, "ff2_dx", epilogue=lambda r, e: r * (2.0 * jnp.maximum(e, 0.0)), extras=(u,))
    gw_ff2 = _matmul(u, dff, "tn", F32, "ff2_dw", a_fn=relu2, tm=DFF // 8, tn=D,
                     out_shape4=(2, 4, DFF // 8, D), out_map=lambda i, j: (i % 2, i // 2))
    gw_ff1 = _matmul(h2, du, "tn", F32, "ff1_dw", tm=D // 2, tn=DFF // 4,
                     out_shape4=(2, 4, D // 2, DFF // 4), out_map=lambda i, j: (i, j))

    def by_half_cols(g):
        r, c4 = g.shape
        return g.reshape(2, r // 2, 4, c4 // 4).transpose(0, 2, 1, 3)

    def by_half_rows(g):
        r4, cdim = g.shape
        return g.reshape(4, 2, r4 // 8, cdim).transpose(1, 0, 2, 3)

    def scatter_start(layouts, both, tag):
        theirs = _pair_swap(layouts, "pair_swap_" + tag)
        sums = []
        for i, (g2, t) in enumerate(zip(layouts, theirs)):
            _, r2, cdim = t.shape
            sm = _pair_add(g2.reshape(2, 4 * r2, cdim), t.reshape(4 * r2, cdim), ci, "pair_add_%s_%d" % (tag, i))
            sums.append(sm.reshape(4, r2, cdim))
        return _spread_start(sums, True, both, dev, chip, "scatter_%s_start" % tag)

    def scatter_wait(handle, both, after, tag):
        lands = _spread_wait(handle, True, both, after, "scatter_%s_wait" % tag)
        if not both:
            lands = _sibling_fill(lands, "scatter_%s_fill" % tag)
        return [s.reshape(4, 2 * s.shape[1], s.shape[2]) for s in lands]

    ff_handle, ff_token = scatter_start([gw_ff1, gw_ff2], True, "ff")
    dh2 = _matmul(du, w_ff1_f, "nt", F32, "ff1_dx")

    def resid_bwd(r, k):
        x_, mix_, dx2a, dh2_ = r
        _, vjp = jax.vjp(_resid_normmod, x_, mix_, *k)
        dx, dmix_, dg, dnw, dsc, dsh = vjp((dx2a, dh2_))
        return [dx, dmix_], [dg, dnw, dsc, dsh]

    dx2, dmix, dgate1, g_norm2, dscale2, dshift2 = _rows(
        "resid_norm2_bwd", resid_bwd, [(xs, 0, D), (mix, 0, D), (dy, 0, D), (dh2, 0, D)],
        [gate1 + ff_token[0, 0], norm2_w, scale2, shift2], [(D, F32), (D, BF16)], [(1, D)] * 4, S)
    gw_out = jnp.concatenate([_matmul(y_ssd, dmix, "tn", F32, "out_proj_dw_a"),
                              _matmul(y_att, dmix, "tn", F32, "out_proj_dw_b")], axis=0)
    out_handle, out_token = scatter_start([by_half_rows(gw_out)], True, "out")
    dy_ssd = _matmul(dmix, w_out_a, "nt", F32, "out_proj_dx_a")
    dy_att = _matmul(dmix, w_out_b, "nt", F32, "out_proj_dx_b")

    def attn_norm_bwd(r, k):
        o_, dyo = r
        _, vjp = jax.vjp(_rmsw, o_, k[0])
        do_, dw_ = vjp(dyo)
        lo = _lane_mask()
        dd_blocks = []
        for b in range(AD // LANES):
            t = (do_ * o_)[:, b * LANES:(b + 1) * LANES]
            s0 = jnp.sum(jnp.where(lo, t, 0.0), axis=1, keepdims=True)
            s1 = jnp.sum(jnp.where(lo, 0.0, t), axis=1, keepdims=True)
            dd_blocks.append(jnp.where(lo, s0, s1))
        return [do_, jnp.concatenate(dd_blocks, axis=1)], [dw_]

    do_att, dd_att, g_attn_norm = _rows("attn_norm_bwd", attn_norm_bwd, [(o_att, 0, AD), (dy_att, 0, AD)],
                                        [attn_norm_w + out_token[0, 0]], [(AD, F32), (AD, F32)], [(1, AD)], S)
    dq_n, dk_n, dv = _attn_bwd(qn, kn, proj, o_v // LANES, do_att, lse, dd_att, "attn_bwd")

    def qk_bwd_call(name, col0, w2, scale, g):
        def body(t_ref, w_ref, g_ref, o_ref, dw_ref):
            @pl.when(pl.program_id(0) == 0)
            def _():
                dw_ref[...] = jnp.zeros_like(dw_ref)
            _, vjp = jax.vjp(lambda t, w: _headnorm(t, w, scale), t_ref[...], w_ref[...])
            dt_, dw_ = vjp(g_ref[...])
            o_ref[...] = dt_.astype(BF16)
            dw_ref[...] += dw_
        blk = pl.BlockSpec((S, LANES), lambda j: (0, j))
        return pl.pallas_call(
            body, name=name, grid=(AD // LANES,),
            in_specs=[pl.BlockSpec((S, LANES), lambda j: (0, j + col0 // LANES)),
                      pl.BlockSpec((1, LANES), lambda j: (0, 0)), blk],
            out_specs=[blk, pl.BlockSpec((1, LANES), lambda j: (0, 0))],
            out_shape=[jax.ShapeDtypeStruct((S, AD), BF16), jax.ShapeDtypeStruct((1, LANES), F32)],
            compiler_params=_cparams(("arbitrary",)),
        )(proj, w2, g)

    dq, g_qw2 = qk_bwd_call("q_norm_bwd", o_q, qw2, HD ** -0.5, dq_n)
    dk, g_kw2 = qk_bwd_call("k_norm_bwd", o_k, kw2, 1.0, dk_n)
    g_q_norm = g_qw2[:, :HD] + g_qw2[:, HD:]
    g_k_norm = g_kw2[:, :HD] + g_kw2[:, HD:]

    dxbc, dz, ddtr, g_dtb, g_alog, g_dsk, g_ssd_norm = _ssd_bwd(
        xbc, proj, o_dt // LANES, dtb, alog, dsk, ssd_norm_w, hsave, dy_ssd, "ssd_bwd")
    dxbc_pre, g_conv_w, g_conv_b = _conv_bwd(proj, o_xbc, CC, conv_w_f, conv_b, dxbc, "conv_bwd")
    dproj = jnp.concatenate([dz.astype(BF16), dxbc_pre.astype(BF16), dq, dk, dv.astype(BF16), ddtr.astype(BF16)], axis=1)
    gw_proj = _matmul(h1, dproj, "tn", F32, "in_proj_dw", tn=896)
    gw_in = jnp.concatenate([gw_proj[:, :n_zx], gw_proj[:, o_dt:o_dt + NH_SSD], gw_proj[:, n_zx:o_dt]], axis=1)
    in_handle, in_token = scatter_start([by_half_cols(gw_in)], False, "in")
    dh1 = _matmul(dproj, w_proj, "nt", F32, "in_proj_dx", tk=896)

    def norm1_bwd(r, k):
        x_, dh_, dres = r
        _, vjp = jax.vjp(_normmod, x_, *k)
        dx, dnw, dsc, dsh = vjp(dh_)
        return [dx + dres], [dnw, dsc, dsh]

    grad_x, g_norm1, dscale1, dshift1 = _rows("norm1_bwd", norm1_bwd, [(xs, 0, D), (dh1, 0, D), (dx2, 0, D)],
                                              [norm1_w + in_token[0, 0], scale1, shift1], [(D, F32)], [(1, D)] * 3, S)
    dmod =jnp.concatenate([dshift1, dscale1, dgate1, dshift2, dscale2, dgate2], axis=1)

    small = [g_norm1, g_norm2, dmod, g_conv_b, g_dtb, g_alog, g_dsk, g_ssd_norm, _pad_lanes(g_q_norm),
             _pad_lanes(g_k_norm), g_attn_norm, g_conv_w.reshape(1, KCONV * CC)]
    sizes = [t.shape[1] for t in small]
    packed = jnp.concatenate(small, axis=1)
    nrow = -(-packed.shape[1] // LANES // 8) * 8
    packed = jnp.pad(packed, ((0, 0), (0, nrow * LANES - packed.shape[1]))).reshape(nrow, LANES)
    packed_all = _exchange8([packed], False, "gather_small_grads")[0]
    tot = _sum_parts(packed_all, "sum_small_grads").reshape(1, nrow * LANES)
    offs = [sum(sizes[:i]) for i in range(len(sizes))]
    (g_norm1, g_norm2, g_b_ada, g_conv_b, g_dtb, g_alog, g_dsk, g_ssd_norm, g_q_norm, g_k_norm, g_attn_norm,
     g_conv_w) = [tot[:, o:o + n] for o, n in zip(offs, sizes)]
    g_dtb, g_alog, g_dsk = g_dtb[:, :NH_SSD], g_alog[:, :NH_SSD], g_dsk[:, :NH_SSD]
    g_q_norm, g_k_norm = g_q_norm[:, :HD], g_k_norm[:, :HD]
    ccs = CC // 4
    g_conv_w = lax.dynamic_slice_in_dim(g_conv_w.reshape(KCONV, CC), chip * ccs, ccs, 1)

    dmod_all = packed_all.reshape(8, nrow * LANES)[:, offs[2]:offs[2] + 6 * D]
    dmod_sh = jnp.pad(lax.dynamic_slice_in_dim(dmod_all, chip * nmod, nmod, 1), ((0, 8), (0, 0)))
    gw_ada = _mod_wgrad(c_all, dmod_sh, "mod_wgrad")

    parts_ff1, parts_ff2 = scatter_wait(ff_handle, True, in_token, "ff")
    res_ff1 = _adamw(parts_ff1, w_ff1[0], m_w_ff1[0], v_w_ff1[0], "adamw_w_ff1")
    res_ff2 = _adamw(parts_ff2, w_ff2[0], m_w_ff2[0], v_w_ff2[0], "adamw_w_ff2")
    res_out = _adamw(scatter_wait(out_handle, True, in_token, "out")[0], w_out[0], m_w_out[0], v_w_out[0], "adamw_w_out")
    res_ada = _adamw(gw_ada[None], w_ada[0], m_w_ada[0], v_w_ada[0], "adamw_w_ada")
    res_in = _adamw(scatter_wait(in_handle, False, res_ada[0], "in")[0], w_in[0], m_w_in[0], v_w_in[0], "adamw_w_in")

    small_names = ["norm1_w", "norm2_w", "b_ada", "conv_w", "conv_b", "dt_bias", "a_log", "d_skip", "ssd_norm_w",
                   "q_norm_w", "k_norm_w", "attn_norm_w"]
    small_g = dict(norm1_w=g_norm1, norm2_w=g_norm2, b_ada=g_b_ada, conv_w=g_conv_w.reshape(1, KCONV * ccs),
                   conv_b=g_conv_b, dt_bias=g_dtb, a_log=g_alog, d_skip=g_dsk, ssd_norm_w=g_ssd_norm, q_norm_w=g_q_norm,
                   k_norm_w=g_k_norm, attn_norm_w=g_attn_norm)
    small_w = dict(norm1_w=(norm1_w, m_norm1_w, v_norm1_w), norm2_w=(norm2_w, m_norm2_w, v_norm2_w),
                   b_ada=(b_ada, m_b_ada, v_b_ada),
                   conv_w=tuple(t.reshape(1, KCONV * ccs) for t in (conv_w, m_conv_w, v_conv_w)),
                   conv_b=(conv_b, m_conv_b, v_conv_b), dt_bias=(dt_bias, m_dt_bias, v_dt_bias),
                   a_log=(a_log, m_a_log, v_a_log), d_skip=(d_skip, m_d_skip, v_d_skip),
                   ssd_norm_w=(ssd_norm_w, m_ssd_norm_w, v_ssd_norm_w), q_norm_w=(q_norm_w, m_q_norm_w, v_q_norm_w),
                   k_norm_w=(k_norm_w, m_k_norm_w, v_k_norm_w), attn_norm_w=(attn_norm_w, m_attn_norm_w, v_attn_norm_w))
    ssz = [_pad_lanes(small_g[n]).shape[1] for n in small_names]
    soff = [sum(ssz[:i]) for i in range(len(ssz))]
    srow = -(-sum(ssz) // LANES // 8) * 8

    def pack(ts, fill):
        t = jnp.concatenate([jnp.pad(t, ((0, 0), (0, (-t.shape[1]) % LANES)), constant_values=fill) for t in ts], axis=1)
        return jnp.pad(t, ((0, 0), (0, srow * LANES - t.shape[1])), constant_values=fill).reshape(srow, LANES)

    sg = pack([small_g[n] for n in small_names], 0.0)
    sw = pack([small_w[n][0] for n in small_names], 0.0)
    sm_ = pack([small_w[n][1] for n in small_names], 0.0)
    sv = pack([small_w[n][2] for n in small_names], 1.0)
    _, s_delta, s_m, s_v = _adamw(sg[None], sw, sm_, sv, "adamw_small", tm=srow)

    def unpack(t, n):
        i = small_names.index(n)
        return t.reshape(1, srow * LANES)[:, soff[i]:soff[i] + small_g[n].shape[1]].reshape(small_w[n][0].shape)

    loss = lax.psum(loss_p[0, 0], ("x", "y", "c"))
    big_res = dict(w_ada=res_ada, w_in=res_in, w_out=res_out, w_ff1=res_ff1, w_ff2=res_ff2)
    order = ["norm1_w", "norm2_w", "w_ada", "b_ada", "w_in", "conv_w", "conv_b", "dt_bias", "a_log", "d_skip",
             "ssd_norm_w", "q_norm_w", "k_norm_w", "attn_norm_w", "w_out", "w_ff1", "w_ff2"]
    grads, deltas, new_m, new_v = [], [], [], []
    for n in order:
        if n in big_res:
            g_, d_, m_, v_ = [t[None] for t in big_res[n]]
        else:
            g_ = small_g[n].reshape(small_w[n][0].shape)
            d_, m_, v_ = unpack(s_delta, n), unpack(s_m, n), unpack(s_v, n)
            if n == "conv_w":
                g_, d_, m_, v_ = [t.reshape(conv_w.shape) for t in (g_, d_, m_, v_)]
        grads.append(g_)
        deltas.append(d_)
        new_m.append(m_)
        new_v.append(v_)
    return (loss, grad_x[None], *grads, *deltas, *new_m, *new_v)
```

```python
import functools

import jax
import jax.numpy as jnp
from jax import lax
from jax.experimental import pallas as pl
from jax.experimental.pallas import tpu as pltpu

F32, BF16 = jnp.float32, jnp.bfloat16
EPS = 1e-6
HD = 64
NH_SSD = 16
NG = 4
NSTATE = 128
KCONV = 4
CHUNK = 128
NH_ATT = 16
PATTERNS = ((128, 1), (512, 4), (2048, 16))
ABLK = 128
LANES = 128
ADAM_LR, ADAM_B1, ADAM_B2, ADAM_EPS, ADAM_WD, ADAM_STEP = 0.001, 0.9, 0.999, 1e-08, 0.01, 10
VMEM_LIMIT = 56 * 1024 * 1024
MESH = pl.DeviceIdType.MESH
NEG = -1e30

_DN = {"nn": (((1,), (0,)), ((), ())), "nt": (((1,), (1,)), ((), ())), "tn": (((0,), (0,)), ((), ()))}


def _cparams(sem):
    return pltpu.CompilerParams(dimension_semantics=sem, vmem_limit_bytes=VMEM_LIMIT)


def _tile(n, cap):
    if n % LANES or n <= LANES:
        return n
    best = LANES
    for t in range(LANES, min(n, cap) + 1, LANES):
        if n % t == 0:
            best = t
    return best


def _silu(x):
    return x / (1.0 + jnp.exp(-x))


def _softplus(x):
    return jnp.maximum(x, 0.0) + jnp.log(1.0 + jnp.exp(-jnp.abs(x)))


def _dot(a, b, dims):
    return lax.dot_general(a.astype(BF16), b.astype(BF16), _DN[dims], preferred_element_type=F32)


def _matmul(a, b, dims, out_dtype, name, a_fn=None, epilogue=None, extras=(), tm=1024, tn=1024, tk=1024,
            out_shape4=None, out_map=None):
    if dims == "nn":
        (M, K), (_, N) = a.shape, b.shape
    elif dims == "nt":
        (M, K), (N, _) = a.shape, b.shape
    else:
        (K, M), (_, N) = a.shape, b.shape
    tm, tn, tk = _tile(M, tm), _tile(N, tn), _tile(K, tk)
    nk = K // tk
    ne = len(extras)

    def body(a_ref, b_ref, *rest):
        e_refs, o_ref = rest[:ne], rest[ne]
        av = a_ref[...]
        if a_fn is not None:
            av = a_fn(av)
        part = _dot(av, b_ref[...], dims)

        def finish(r):
            if epilogue is not None:
                r = epilogue(r, *[e[...] for e in e_refs])
            o_ref[...] = r.astype(out_dtype).reshape(o_ref.shape)

        if nk == 1:
            finish(part)
            return
        acc = rest[ne + 1]
        k = pl.program_id(2)

        @pl.when(k == 0)
        def _():
            acc[...] = part

        @pl.when(k > 0)
        def _():
            acc[...] += part

        @pl.when(k == nk - 1)
        def _():
            finish(acc[...])

    a_spec = pl.BlockSpec((tk, tm), lambda i, j, k: (k, i)) if dims == "tn" else pl.BlockSpec((tm, tk), lambda i, j, k: (i, k))
    b_spec = pl.BlockSpec((tn, tk), lambda i, j, k: (j, k)) if dims == "nt" else pl.BlockSpec((tk, tn), lambda i, j, k: (k, j))
    o_spec = pl.BlockSpec((tm, tn), lambda i, j, k: (i, j))
    out_spec, out_dims = o_spec, (M, N)
    if out_shape4 is not None:
        assert out_shape4[2:] == (tm // 2, tn) and out_shape4[0] == 2
        out_spec = pl.BlockSpec((2, None, tm // 2, tn), lambda i, j, k: (0, out_map(i, j), 0, 0))
        out_dims = out_shape4
    return pl.pallas_call(
        body, name=name, grid=(M // tm, N // tn, nk),
        in_specs=[a_spec, b_spec] + [o_spec] * ne, out_specs=out_spec,
        out_shape=jax.ShapeDtypeStruct(out_dims, out_dtype),
        scratch_shapes=[pltpu.VMEM((tm, tn), F32)] if nk > 1 else [],
        compiler_params=_cparams(("parallel", "parallel", "arbitrary")),
    )(a, b, *extras)


def _rows(name, fn, rows, consts, outs, accs, n_rows, tm=256):
    tm = min(tm, n_rows)
    nr, nc, no, na = len(rows), len(consts), len(outs), len(accs)

    def body(*refs):
        r_refs, c_refs = refs[:nr], refs[nr:nr + nc]
        o_refs, a_refs = refs[nr + nc:nr + nc + no], refs[nr + nc + no:]
        o_vals, a_vals = fn([r[...] for r in r_refs], [c[...] for c in c_refs])
        for ref, val in zip(o_refs, o_vals):
            ref[...] = val.astype(ref.dtype)
        if na:
            @pl.when(pl.program_id(0) == 0)
            def _():
                for ref in a_refs:
                    ref[...] = jnp.zeros_like(ref)
            for ref, val in zip(a_refs, a_vals):
                ref[...] += val

    in_specs = [pl.BlockSpec((tm, w), lambda i, cb=cb: (i, cb)) for (_, cb, w) in rows]
    in_specs += [pl.BlockSpec(cst.shape, lambda i, nd=cst.ndim: (0,) * nd) for cst in consts]
    out_specs = [pl.BlockSpec((tm, w), lambda i: (i, 0)) for (w, _) in outs]
    out_specs += [pl.BlockSpec(s, lambda i: (0, 0)) for s in accs]
    out_shape = [jax.ShapeDtypeStruct((n_rows, w), dt) for (w, dt) in outs]
    out_shape += [jax.ShapeDtypeStruct(s, F32) for s in accs]
    res = pl.pallas_call(
        body, name=name, grid=(n_rows // tm,), in_specs=in_specs, out_specs=out_specs, out_shape=out_shape,
        compiler_params=_cparams(("arbitrary",)),
    )(*[r[0] for r in rows], *consts)
    return res


def _normmod(x, nw, sc, sh):
    r = lax.rsqrt(jnp.mean(x * x, axis=-1, keepdims=True) + EPS)
    return (x * r) * nw * (1.0 + sc) + sh


def _resid_normmod(x, mix, g, nw, sc, sh):
    x2 = x + g * mix
    return x2, _normmod(x2, nw, sc, sh)


def _rmsw(o, w):
    return o * lax.rsqrt(jnp.mean(o * o, axis=-1, keepdims=True) + EPS) * w


def _lane_mask():
    return lax.broadcasted_iota(jnp.int32, (1, LANES), 1) < HD


def _headnorm(t, w, scale):
    lo = _lane_mask()
    t2 = t * t
    s0 = jnp.sum(jnp.where(lo, t2, 0.0), axis=1, keepdims=True)
    s1 = jnp.sum(jnp.where(lo, 0.0, t2), axis=1, keepdims=True)
    ms = jnp.where(lo, s0, s1) * (1.0 / HD)
    return t * lax.rsqrt(ms + EPS) * w * scale


CONV_ROWS = 128
CONV_HALO = 8


def _conv_cols(n_ch):
    return _tile(n_ch, LANES)


def _conv_fwd(proj, col0, n_ch, conv_w, conv_b, name):
    S = proj.shape[0]
    tc = _conv_cols(n_ch)

    R, H = CONV_ROWS, CONV_HALO

    def body(u_ref, w_ref, b_ref, o_ref):
        w = [w_ref[i:i + 1, :] for i in range(KCONV)]
        b = b_ref[...]

        def chunk(ext):
            acc = b + w[KCONV - 1] * ext[H:]
            for i in range(KCONV - 1):
                acc = acc + w[i] * pltpu.roll(ext, KCONV - 1 - i, 0)[H:]
            return _silu(acc)

        o_ref[0:R, :] = chunk(jnp.concatenate([jnp.zeros((H, tc), F32), u_ref[0:R, :]], axis=0))

        def step(c, carry):
            r0 = pl.multiple_of(c * R, R)
            o_ref[pl.ds(r0, R), :] = chunk(u_ref[pl.ds(pl.multiple_of(r0 - H, H), R + H), :])
            return carry

        lax.fori_loop(1, S // R, step, 0)

    return pl.pallas_call(
        body, name=name, grid=(n_ch // tc,),
        in_specs=[pl.BlockSpec((S, tc), lambda j: (0, j + col0 // tc)),
                  pl.BlockSpec((KCONV, tc), lambda j: (0, j)), pl.BlockSpec((1, tc), lambda j: (0, j))],
        out_specs=pl.BlockSpec((S, tc), lambda j: (0, j)),
        out_shape=jax.ShapeDtypeStruct((S, n_ch), F32),
        compiler_params=_cparams(("parallel",)),
    )(proj, conv_w, conv_b)


def _conv_bwd(proj, col0, n_ch, conv_w, conv_b, dxbc, name):
    S = proj.shape[0]
    tc = _conv_cols(n_ch)

    R, H = CONV_ROWS, CONV_HALO

    def body(u_ref, w_ref, b_ref, g_ref, du_ref, dw_ref, db_ref):
        w = [w_ref[i:i + 1, :] for i in range(KCONV)]
        b = b_ref[...]
        pad = jnp.zeros((H, tc), F32)

        def chunk(u_ext, g_ext):
            taps = [pltpu.roll(u_ext, KCONV - 1 - i, 0)[H:] for i in range(KCONV - 1)] + [u_ext[H:]]
            acc = b
            for i in range(KCONV):
                acc = acc + w[i] * taps[i]
            sig = 1.0 / (1.0 + jnp.exp(-acc))
            dacc = g_ext * (sig * (1.0 + acc * (1.0 - sig)))
            du = w[KCONV - 1] * dacc[:R]
            for i in range(KCONV - 1):
                du = du + w[i] * pltpu.roll(dacc, R + H - (KCONV - 1 - i), 0)[:R]
            d = dacc[:R]
            return du, [jnp.sum(d * t[:R], axis=0, keepdims=True) for t in taps], jnp.sum(d, axis=0, keepdims=True)

        du, dws, db = chunk(jnp.concatenate([pad, u_ref[0:R + H, :]], axis=0), g_ref[0:R + H, :])
        du_ref[0:R, :] = du

        def step(c, carry):
            r0 = pl.multiple_of(c * R, R)
            du_c, dws_c, db_c = chunk(u_ref[pl.ds(pl.multiple_of(r0 - H, H), R + 2 * H), :], g_ref[pl.ds(r0, R + H), :])
            du_ref[pl.ds(r0, R), :] = du_c
            return [a + b_ for a, b_ in zip(carry[0], dws_c)], carry[1] + db_c

        dws, db = lax.fori_loop(1, S // R - 1, step, (dws, db))
        du, dws_l, db_l = chunk(jnp.concatenate([u_ref[S - R - H:S, :], pad], axis=0),
                                jnp.concatenate([g_ref[S - R:S, :], pad], axis=0))
        du_ref[S - R:S, :] = du
        for i in range(KCONV):
            dw_ref[i:i + 1, :] = dws[i] + dws_l[i]
        db_ref[...] = db + db_l

    return pl.pallas_call(
        body, name=name, grid=(n_ch // tc,),
        in_specs=[pl.BlockSpec((S, tc), lambda j: (0, j + col0 // tc)),
                  pl.BlockSpec((KCONV, tc), lambda j: (0, j)), pl.BlockSpec((1, tc), lambda j: (0, j)),
                  pl.BlockSpec((S, tc), lambda j: (0, j))],
        out_specs=[pl.BlockSpec((S, tc), lambda j: (0, j)), pl.BlockSpec((KCONV, tc), lambda j: (0, j)),
                   pl.BlockSpec((1, tc), lambda j: (0, j))],
        out_shape=[jax.ShapeDtypeStruct((S, n_ch), F32), jax.ShapeDtypeStruct((KCONV, n_ch), F32),
                   jax.ShapeDtypeStruct((1, n_ch), F32)],
        compiler_params=_cparams(("parallel",)),
    )(proj, conv_w, conv_b, dxbc)


@functools.partial(jax.custom_vjp, nondiff_argnums=(2,))
def _mm(a, b, dims):
    return _dot(a, b, dims)


def _mm_fwd(a, b, dims):
    return _dot(a, b, dims), (a, b)


def _mm_bwd(dims, res, g):
    a, b = res
    if dims == "nn":
        return _dot(g, b, "nt"), _dot(a, g, "tn")
    if dims == "nt":
        return _dot(g, b, "nn"), _dot(g, a, "tn")
    return _dot(b, g, "nt"), _dot(a, g, "nn")


_mm.defvjp(_mm_fwd, _mm_bwd)


def _tri_dot(x, upper):
    n = x.shape[0]
    r = lax.broadcasted_iota(jnp.int32, (n, n), 0)
    c = lax.broadcasted_iota(jnp.int32, (n, n), 1)
    t = jnp.where((r <= c) if upper else (r >= c), 1.0, 0.0)
    return lax.dot_general(t, x, _DN["nn"], precision=lax.Precision.HIGHEST, preferred_element_type=F32)


@jax.custom_vjp
def _cumsum_rows(x):
    return _tri_dot(x, False)


_cumsum_rows.defvjp(lambda x: (_tri_dot(x, False), None), lambda _, g: (_tri_dot(g, True),))


def _ssd_chunk(xs_p, bm_g, cm_g, dtr, z_p, dtb, alog, dsk, nw_p, h_p):
    L = dtr.shape[0]
    n_pairs = len(xs_p)
    ppg = n_pairs // len(bm_g)
    lane = lax.broadcasted_iota(jnp.int32, (1, LANES), 1)
    sub = lax.broadcasted_iota(jnp.int32, (LANES, 1), 0)
    lo = lane < HD
    row_l = lax.broadcasted_iota(jnp.int32, (L, 1), 0)
    tri = lax.broadcasted_iota(jnp.int32, (L, L), 0) >= lax.broadcasted_iota(jnp.int32, (L, L), 1)

    dt = _softplus(dtr + dtb)
    acs = _cumsum_rows(dt * (-jnp.exp(alog)))
    acs_t = acs.T
    a_last = jnp.sum(jnp.where(row_l == L - 1, acs, 0.0), axis=0, keepdims=True)
    e_acs = jnp.exp(acs)
    dec = jnp.exp(a_last - acs)
    cdec = jnp.exp(a_last)

    def colv(m, h):
        return jnp.sum(jnp.where(lane == h, m, 0.0), axis=1, keepdims=True)

    def rowv(mt, h):
        return jnp.sum(jnp.where(sub == h, mt, 0.0), axis=0, keepdims=True)

    def pair(m, h0):
        return jnp.where(lo, colv(m, h0), colv(m, h0 + 1))

    ys, hs = [], []
    cb = None
    for p in range(n_pairs):
        g, h0 = p // ppg, 2 * p
        bmat, cmat = bm_g[g], cm_g[g]
        if p % ppg == 0:
            cb = _mm(cmat, bmat, "nt")
        x = xs_p[p]
        xdt = x * pair(dt, h0)
        yd = []
        for h in (h0, h0 + 1):
            seg = colv(acs, h) - rowv(acs_t, h)
            lm = jnp.where(tri, jnp.exp(jnp.where(tri, seg, 0.0)), 0.0)
            yd.append(_mm(cb * lm, xdt, "nn"))
        y = jnp.where(lo, yd[0], yd[1])
        y = y + _mm(cmat, h_p[p], "nt") * pair(e_acs, h0)
        st = _mm(xdt * pair(dec, h0), bmat, "tn")
        cd_col = jnp.where(sub < HD, colv(cdec, h0), colv(cdec, h0 + 1))
        hs.append(h_p[p] * cd_col + st)
        ys.append(y + pair(dsk, h0) * x)

    y2 = [ys[p] * _silu(z_p[p]) for p in range(n_pairs)]
    outs = []
    for g in range(len(bm_g)):
        ps = range(g * ppg, (g + 1) * ppg)
        ss = sum(jnp.sum(y2[p] * y2[p], axis=1, keepdims=True) for p in ps)
        rs = lax.rsqrt(ss * (1.0 / (ppg * LANES)) + EPS)
        outs += [y2[p] * rs * nw_p[p] for p in ps]
    return outs, hs


def _ssd_slices(xbc_ref, z_ref, nw_ref, di):
    n_pairs = di // LANES
    xs_p = [xbc_ref[:, p * LANES:(p + 1) * LANES] for p in range(n_pairs)]
    bm_g = [xbc_ref[:, di + g * NSTATE:di + (g + 1) * NSTATE] for g in range(NG)]
    cm_g = [xbc_ref[:, di + (NG + g) * NSTATE:di + (NG + g + 1) * NSTATE] for g in range(NG)]
    z_p = [z_ref[:, p * LANES:(p + 1) * LANES] for p in range(n_pairs)]
    nw_p = [nw_ref[:, p * LANES:(p + 1) * LANES] for p in range(n_pairs)]
    return xs_p, bm_g, cm_g, z_p, nw_p


def _ssd_fwd(xbc, proj, dt_cb, dtb, alog, dsk, nw, name):
    S, cc = xbc.shape
    di = NH_SSD * HD
    n_pairs = di // LANES
    nchunk = S // CHUNK

    def body(xbc_ref, z_ref, dtr_ref, dtb_ref, alog_ref, dsk_ref, nw_ref, y_ref, hs_ref, h_scr):
        @pl.when(pl.program_id(0) == 0)
        def _():
            h_scr[...] = jnp.zeros_like(h_scr)

        xs_p, bm_g, cm_g, z_p, nw_p = _ssd_slices(xbc_ref, z_ref, nw_ref, di)
        h_p = [h_scr[p * LANES:(p + 1) * LANES, :] for p in range(n_pairs)]
        hs_ref[...] = h_scr[...]
        outs, hs = _ssd_chunk(xs_p, bm_g, cm_g, dtr_ref[...], z_p, dtb_ref[...], alog_ref[...], dsk_ref[...], nw_p, h_p)
        for p in range(n_pairs):
            y_ref[:, p * LANES:(p + 1) * LANES] = outs[p].astype(y_ref.dtype)
            h_scr[p * LANES:(p + 1) * LANES, :] = hs[p]

    vec = pl.BlockSpec((1, LANES), lambda c: (0, 0))
    return pl.pallas_call(
        body, name=name, grid=(nchunk,),
        in_specs=[pl.BlockSpec((CHUNK, cc), lambda c: (c, 0)), pl.BlockSpec((CHUNK, di), lambda c: (c, 0)),
                  pl.BlockSpec((CHUNK, LANES), lambda c: (c, dt_cb)), vec, vec, vec,
                  pl.BlockSpec((1, di), lambda c: (0, 0))],
        out_specs=[pl.BlockSpec((CHUNK, di), lambda c: (c, 0)), pl.BlockSpec((None, di, NSTATE), lambda c: (c, 0, 0))],
        out_shape=[jax.ShapeDtypeStruct((S, di), BF16), jax.ShapeDtypeStruct((nchunk, di, NSTATE), F32)],
        scratch_shapes=[pltpu.VMEM((di, NSTATE), F32)],
        compiler_params=_cparams(("arbitrary",)),
    )(xbc, proj, proj, dtb, alog, dsk, nw)


def _ssd_bwd(xbc, proj, dt_cb, dtb, alog, dsk, nw, hsave, dy, name):
    S, cc = xbc.shape
    di = NH_SSD * HD
    n_pairs = di // LANES
    nchunk = S // CHUNK

    def body(xbc_ref, z_ref, dtr_ref, dtb_ref, alog_ref, dsk_ref, nw_ref, hs_ref, dy_ref,
             dxbc_ref, dz_ref, ddtr_ref, ddtb_ref, dalog_ref, ddsk_ref, dnw_ref, dh_scr):
        @pl.when(pl.program_id(0) == 0)
        def _():
            dh_scr[...] = jnp.zeros_like(dh_scr)
            ddtb_ref[...] = jnp.zeros_like(ddtb_ref)
            dalog_ref[...] = jnp.zeros_like(dalog_ref)
            ddsk_ref[...] = jnp.zeros_like(ddsk_ref)
            dnw_ref[...] = jnp.zeros_like(dnw_ref)

        xs_p, bm_g, cm_g, z_p, nw_p = _ssd_slices(xbc_ref, z_ref, nw_ref, di)
        h_p = [hs_ref[p * LANES:(p + 1) * LANES, :] for p in range(n_pairs)]
        dy_p = [dy_ref[:, p * LANES:(p + 1) * LANES].astype(F32) for p in range(n_pairs)]
        dh_p = [dh_scr[p * LANES:(p + 1) * LANES, :] for p in range(n_pairs)]
        _, vjp = jax.vjp(_ssd_chunk, xs_p, bm_g, cm_g, dtr_ref[...], z_p, dtb_ref[...], alog_ref[...], dsk_ref[...],
                         nw_p, h_p)
        dxs, dbm, dcm, ddtr, dz, ddtb, dalog, ddsk, dnw, dh = vjp((dy_p, dh_p))
        for p in range(n_pairs):
            sl = slice(p * LANES, (p + 1) * LANES)
            dxbc_ref[:, sl] = dxs[p]
            dz_ref[:, sl] = dz[p]
            dnw_ref[:, sl] += dnw[p]
            dh_scr[sl, :] = dh[p]
        for g in range(NG):
            dxbc_ref[:, di + g * NSTATE:di + (g + 1) * NSTATE] = dbm[g]
            dxbc_ref[:, di + (NG + g) * NSTATE:di + (NG + g + 1) * NSTATE] = dcm[g]
        ddtr_ref[...] = ddtr
        ddtb_ref[...] += ddtb
        dalog_ref[...] += dalog
        ddsk_ref[...] += ddsk

    last = nchunk - 1
    vec = pl.BlockSpec((1, LANES), lambda c: (0, 0))
    return pl.pallas_call(
        body, name=name, grid=(nchunk,),
        in_specs=[pl.BlockSpec((CHUNK, cc), lambda c: (last - c, 0)), pl.BlockSpec((CHUNK, di), lambda c: (last - c, 0)),
                  pl.BlockSpec((CHUNK, LANES), lambda c: (last - c, dt_cb)), vec, vec, vec,
                  pl.BlockSpec((1, di), lambda c: (0, 0)),
                  pl.BlockSpec((None, di, NSTATE), lambda c: (last - c, 0, 0)),
                  pl.BlockSpec((CHUNK, di), lambda c: (last - c, 0))],
        out_specs=[pl.BlockSpec((CHUNK, cc), lambda c: (last - c, 0)), pl.BlockSpec((CHUNK, di), lambda c: (last - c, 0)),
                   pl.BlockSpec((CHUNK, LANES), lambda c: (last - c, 0)), vec, vec, vec,
                   pl.BlockSpec((1, di), lambda c: (0, 0))],
        out_shape=[jax.ShapeDtypeStruct((S, cc), F32), jax.ShapeDtypeStruct((S, di), F32),
                   jax.ShapeDtypeStruct((S, LANES), F32), jax.ShapeDtypeStruct((1, LANES), F32),
                   jax.ShapeDtypeStruct((1, LANES), F32), jax.ShapeDtypeStruct((1, LANES), F32),
                   jax.ShapeDtypeStruct((1, di), F32)],
        scratch_shapes=[pltpu.VMEM((di, NSTATE), F32)],
        compiler_params=_cparams(("arbitrary",)),
    )(xbc, proj, proj, dtb, alog, dsk, nw, hsave, dy)


def _band_masks(rows_q, rows_k):
    qi = lax.broadcasted_iota(jnp.int32, (rows_q, rows_k), 0)
    ki = lax.broadcasted_iota(jnp.int32, (rows_q, rows_k), 1)
    return qi, ki


def _class_chunks(n_rows, d):
    per_class = n_rows // d
    ch = min(per_class, 256)
    out = []
    for r in range(d):
        for c0 in range(0, per_class, ch):
            tok = pl.ds(c0, ch) if d == 1 else pl.ds(r + d * c0, ch, stride=d)
            out.append((tok, pl.ds(r * per_class + c0, ch)))
    return out


def _to_class_order(src_ref, dst_ref, n_rows, d):
    for tok, cls in _class_chunks(n_rows, d):
        dst_ref[cls, :] = src_ref[tok, :].astype(dst_ref.dtype)


def _blk_rows(t):
    return pl.ds(pl.multiple_of(t * ABLK, ABLK), ABLK)


def _head_lanes(msk, t, t_rolled):
    return jnp.where(msk, t, t_rolled)


def _zero_unless(msk, t):
    return jnp.where(msk, t, jnp.zeros_like(t))


def _attn_fwd(qn, kn, proj, v_cb, name):
    S, ad = qn.shape
    nb = S // ABLK
    nbr = len(PATTERNS)

    def body(q_ref, k_ref, v_ref, o_ref, lse_ref, qc, kc, vc, ob, mb, lb, m_s, l_s):
        lo = _lane_mask()
        qi, ki = _band_masks(ABLK, ABLK)
        cur_ok, prev_ok = ki <= qi, ki >= qi
        for bi, (_, d) in enumerate(PATTERNS):
            nbc = S // d // ABLK
            first, last = bi == 0, bi == nbr - 1
            qs, ks, vs = q_ref, k_ref, v_ref
            if d > 1:
                qs, ks, vs = qc, kc, vc
                for src, dst in ((q_ref, qc), (k_ref, kc), (v_ref, vc)):
                    _to_class_order(src, dst, S, d)
            o_dst, m_dst, l_dst = (o_ref, m_s, l_s) if first else (ob, mb, lb)

            def blk(t, carry, nbc=nbc, qs=qs, ks=ks, vs=vs, o_dst=o_dst, m_dst=m_dst, l_dst=l_dst):
                rows, prow = _blk_rows(t), _blk_rows(jnp.maximum(t - 1, 0))
                has_prev = (t % nbc) != 0
                qv = qs[rows, :]
                q2 = jnp.concatenate([_zero_unless(lo, qv), _zero_unless(jnp.logical_not(lo), qv)], axis=0).astype(BF16)
                ok_c = jnp.concatenate([cur_ok, cur_ok], axis=0)
                ok_p = jnp.concatenate([prev_ok, prev_ok], axis=0) & has_prev
                s_c = jnp.where(ok_c, _dot(q2, ks[rows, :], "nt"), NEG)
                s_p = jnp.where(ok_p, _dot(q2, ks[prow, :], "nt"), NEG)
                m = jnp.max(jnp.maximum(s_c, s_p), axis=1, keepdims=True)
                p_c, p_p = jnp.exp(s_c - m), jnp.exp(s_p - m)
                l = jnp.sum(p_c + p_p, axis=1, keepdims=True)
                o2 = _dot(p_c, vs[rows, :], "nn") + _dot(p_p, vs[prow, :], "nn")
                o_dst[rows, :] = jnp.where(lo, o2[:ABLK], o2[ABLK:])
                m_dst[rows, :] = jnp.where(lo, m[:ABLK], m[ABLK:])
                l_dst[rows, :] = jnp.where(lo, l[:ABLK], l[ABLK:])
                return carry

            lax.fori_loop(0, nb, blk, 0, unroll=8)
            if first:
                continue
            for tok, cls in _class_chunks(S, d):
                m_old, m_b = m_s[tok, :], mb[cls, :]
                m_new = jnp.maximum(m_old, m_b)
                a, b = jnp.exp(m_old - m_new), jnp.exp(m_b - m_new)
                l_new = a * l_s[tok, :] + b * lb[cls, :]
                o_new = a * o_ref[tok, :] + b * ob[cls, :]
                if last:
                    o_ref[tok, :] = o_new / l_new
                    lse_ref[tok, :] = m_new + jnp.log(l_new)
                else:
                    o_ref[tok, :] = o_new
                    m_s[tok, :] = m_new
                    l_s[tok, :] = l_new

    col = pl.BlockSpec((S, LANES), lambda h: (0, h))
    return pl.pallas_call(
        body, name=name, grid=(ad // LANES,),
        in_specs=[col, col, pl.BlockSpec((S, LANES), lambda h: (0, h + v_cb))], out_specs=[col, col],
        out_shape=[jax.ShapeDtypeStruct((S, ad), F32), jax.ShapeDtypeStruct((S, ad), F32)],
        scratch_shapes=[pltpu.VMEM((S, LANES), BF16)] * 3 + [pltpu.VMEM((S, LANES), F32)] * 5,
        compiler_params=_cparams(("parallel",)),
    )(qn, kn, proj)


def _attn_bwd(qn, kn, proj, v_cb, do, lse, dd, name):
    S, ad = qn.shape
    nb = S // ABLK

    def body(q_ref, k_ref, v_ref, do_ref, lse_ref, dd_ref, dq_ref, dk_ref, dv_ref,
             qc, kc, vc, doc, lsec, ddc, dqc, dkc, dvc):
        lo = _lane_mask()
        qi, ki = _band_masks(ABLK, ABLK)
        cur_ok, prev_ok = ki <= qi, ki >= qi
        for bi, (_, d) in enumerate(PATTERNS):
            nbc = S // d // ABLK
            first = bi == 0
            token_order = (q_ref, k_ref, v_ref, do_ref, lse_ref, dd_ref)
            class_order = (qc, kc, vc, doc, lsec, ddc)
            if d > 1:
                for src, dst in zip(token_order, class_order):
                    _to_class_order(src, dst, S, d)
            qs, ks, vs, dos, lses, dds = class_order if d > 1 else token_order
            dq_dst, dk_dst, dv_dst = (dq_ref, dk_ref, dv_ref) if first else (dqc, dkc, dvc)
            dk_dst[...] = jnp.zeros_like(dk_dst)
            dv_dst[...] = jnp.zeros_like(dv_dst)

            def blk(t, carry, nbc=nbc, qs=qs, ks=ks, vs=vs, dos=dos, lses=lses, dds=dds,
                    dq_dst=dq_dst, dk_dst=dk_dst, dv_dst=dv_dst):
                rows, prow = _blk_rows(t), _blk_rows(jnp.maximum(t - 1, 0))
                has_prev = (t % nbc) != 0
                qv, dov, lse_b, dd_b = qs[rows, :], dos[rows, :], lses[rows, :], dds[rows, :]
                lse_r, dd_r = pltpu.roll(lse_b, HD, 1), pltpu.roll(dd_b, HD, 1)
                nlo = jnp.logical_not(lo)
                q2 = jnp.concatenate([_zero_unless(lo, qv), _zero_unless(nlo, qv)], axis=0).astype(BF16)
                do2 = jnp.concatenate([_zero_unless(lo, dov), _zero_unless(nlo, dov)], axis=0).astype(BF16)
                lse2 = jnp.concatenate([_head_lanes(lo, lse_b, lse_r), _head_lanes(nlo, lse_b, lse_r)], axis=0)
                dd2 = jnp.concatenate([_head_lanes(lo, dd_b, dd_r), _head_lanes(nlo, dd_b, dd_r)], axis=0)
                dq2 = None
                for krows, vmask in ((rows, cur_ok), (prow, prev_ok & has_prev)):
                    kv, vv = ks[krows, :], vs[krows, :]
                    vmask2 = jnp.concatenate([vmask, vmask], axis=0)
                    s = jnp.where(vmask2, _dot(q2, kv, "nt"), NEG)
                    p = jnp.exp(s - lse2)
                    ds = p * (_dot(do2, vv, "nt") - dd2)
                    dqk = _dot(ds, kv, "nn")
                    dq2 = dqk if dq2 is None else dq2 + dqk
                    dv_dst[krows, :] += _dot(p, do2, "tn")
                    dk_dst[krows, :] += _dot(ds, q2, "tn")
                dq_dst[rows, :] = jnp.where(lo, dq2[:ABLK], dq2[ABLK:])
                return carry

            lax.fori_loop(0, nb, blk, 0, unroll=4)
            if not first:
                for tok, cls in _class_chunks(S, d):
                    dq_ref[tok, :] = dq_ref[tok, :] + dqc[cls, :]
                    dk_ref[tok, :] = dk_ref[tok, :] + dkc[cls, :]
                    dv_ref[tok, :] = dv_ref[tok, :] + dvc[cls, :]

    col = pl.BlockSpec((S, LANES), lambda h: (0, h))
    col1 = pl.BlockSpec((S, LANES), lambda h: (0, h), pipeline_mode=pl.Buffered(1))
    vcol1 = pl.BlockSpec((S, LANES), lambda h: (0, h + v_cb), pipeline_mode=pl.Buffered(1))
    return pl.pallas_call(
        body, name=name, grid=(ad // LANES,),
        in_specs=[col, col, vcol1, col1, col1, col1], out_specs=[col, col, col],
        out_shape=[jax.ShapeDtypeStruct((S, ad), F32)] * 3,
        scratch_shapes=[pltpu.VMEM((S, LANES), BF16)] * 4 + [pltpu.VMEM((S, LANES), F32)] * 5,
        compiler_params=_cparams(("parallel",)),
    )(qn, kn, proj, do, lse, dd)


def _coords():
    return lax.axis_index("x"), lax.axis_index("y"), lax.axis_index("c")


def _exchange8(xs, per_dest, name):
    n = len(xs)
    blk = [x.shape[1:] if per_dest else x.shape for x in xs]

    def body(*refs):
        ins, outs = refs[:n], refs[n:2 * n]
        send_sems, recv_sems, local_sems = refs[2 * n:]
        x, y, c = _coords()
        sibling = (x, y, 1 - c)
        chips = [(1 - x, y), (x, 1 - y), (1 - x, 1 - y)]
        first, passed, mine = [], [], []
        for a in range(n):
            def src_for(cx, cy, a=a):
                return ins[a].at[2 * cx + cy] if per_dest else ins[a]

            def slot(px, py, pc, a=a):
                return outs[a].at[4 * px + 2 * py + pc]

            def copy(k, src, dst, to, a=a):
                return pltpu.make_async_remote_copy(src_ref=src, dst_ref=dst, send_sem=send_sems.at[7 * a + k],
                                                    recv_sem=recv_sems.at[7 * a + k], device_id=to, device_id_type=MESH)

            m = pltpu.make_async_copy(src_for(x, y), slot(x, y, c), local_sems.at[a])
            m.start()
            mine.append(m)
            cps = [copy(0, src_for(x, y), slot(x, y, c), sibling)]
            cps += [copy(1 + j, src_for(*chip), slot(x, y, c), (*chip, c)) for j, chip in enumerate(chips)]
            for cp in cps:
                cp.start()
            first += cps
        for a in range(n):
            def slot(px, py, pc, a=a):
                return outs[a].at[4 * px + 2 * py + pc]

            def copy(k, src, dst, to, a=a):
                return pltpu.make_async_remote_copy(src_ref=src, dst_ref=dst, send_sem=send_sems.at[7 * a + k],
                                                    recv_sem=recv_sems.at[7 * a + k], device_id=to, device_id_type=MESH)

            for j, chip in enumerate(chips):
                copy(1 + j, slot(*chip, c), slot(*chip, c), (*chip, c)).wait_recv()
                fw = copy(4 + j, slot(*chip, c), slot(*chip, c), sibling)
                fw.start()
                passed.append(fw)
        for a in range(n):
            def slot(px, py, pc, a=a):
                return outs[a].at[4 * px + 2 * py + pc]

            def copy(k, src, dst, to, a=a):
                return pltpu.make_async_remote_copy(src_ref=src, dst_ref=dst, send_sem=send_sems.at[7 * a + k],
                                                    recv_sem=recv_sems.at[7 * a + k], device_id=to, device_id_type=MESH)

            copy(0, slot(x, y, 1 - c), slot(x, y, 1 - c), sibling).wait_recv()
            for j, chip in enumerate(chips):
                copy(4 + j, slot(*chip, 1 - c), slot(*chip, 1 - c), sibling).wait_recv()
        for cp in first + passed:
            cp.wait_send()
        for m in mine:
            m.wait()

    anyspec = pl.BlockSpec(memory_space=pl.ANY)
    res = pl.pallas_call(
        body, name=name, in_specs=[anyspec] * n, out_specs=[anyspec] * n,
        out_shape=[jax.ShapeDtypeStruct((8,) + tuple(b), x.dtype) for b, x in zip(blk, xs)],
        scratch_shapes=[pltpu.SemaphoreType.DMA((7 * n,)), pltpu.SemaphoreType.DMA((7 * n,)),
                        pltpu.SemaphoreType.DMA((n,))],
    )(*xs)
    return list(res)


def _pair_swap(xs, name):
    n = len(xs)

    def body(*refs):
        ins, outs = refs[:n], refs[n:2 * n]
        send_sems, recv_sems = refs[2 * n:]
        x, y, c = _coords()
        cps = [pltpu.make_async_remote_copy(src_ref=ins[a].at[1 - c], dst_ref=outs[a], send_sem=send_sems.at[a],
                                            recv_sem=recv_sems.at[a], device_id=(x, y, 1 - c), device_id_type=MESH)
               for a in range(n)]
        for cp in cps:
            cp.start()
        for cp in cps:
            cp.wait()

    anyspec = pl.BlockSpec(memory_space=pl.ANY)
    res = pl.pallas_call(
        body, name=name, in_specs=[anyspec] * n, out_specs=[anyspec] * n,
        out_shape=[jax.ShapeDtypeStruct(x.shape[1:], x.dtype) for x in xs],
        scratch_shapes=[pltpu.SemaphoreType.DMA((n,)), pltpu.SemaphoreType.DMA((n,))],
    )(*xs)
    return list(res)


_HBM = pl.BlockSpec(memory_space=pltpu.HBM)
_SEM = pl.BlockSpec(memory_space=pltpu.SEMAPHORE)
_EFFECT = pltpu.SideEffectType.DATAFLOW_SIDE_EFFECTING


def _n_peers(both):
    return 7 if both else 3


def _peer(x, y, c, j, both):
    bits = j + 1 if both else 2 * (j + 1)
    dx, dy, dc = bits >> 2 & 1, bits >> 1 & 1, bits & 1
    return (1 - x if dx else x, 1 - y if dy else y, 1 - c if dc else c)


def _spread_copies(s_refs, l_refs, send_sems, recv_sems, per_dest, both):
    x, y, c = _coords()
    me = 4 * x + 2 * y + c
    npeer = _n_peers(both)
    cps = []
    for a in range(len(s_refs)):
        for j in range(npeer):
            tx, ty, tc = _peer(x, y, c, j, both)
            src = s_refs[a].at[2 * tx + ty] if per_dest else s_refs[a]
            cps.append(pltpu.make_async_remote_copy(src_ref=src, dst_ref=l_refs[a].at[me],
                                                    send_sem=send_sems.at[npeer * a + j],
                                                    recv_sem=recv_sems.at[npeer * a + j], device_id=(tx, ty, tc),
                                                    device_id_type=MESH))
    return cps


def _sibling_fill(lands, name):
    n = len(lands)

    def body(*refs):
        outs, send_sems, recv_sems = refs[n:2 * n], refs[2 * n], refs[2 * n + 1]
        x, y, c = _coords()
        cps = [pltpu.make_async_remote_copy(src_ref=outs[a].at[2 * k + c], dst_ref=outs[a].at[2 * k + c],
                                            send_sem=send_sems.at[4 * a + k], recv_sem=recv_sems.at[4 * a + k],
                                            device_id=(x, y, 1 - c), device_id_type=MESH)
               for a in range(n) for k in range(4)]
        for cp in cps:
            cp.start()
        for cp in cps:
            cp.wait()

    anyspec = pl.BlockSpec(memory_space=pl.ANY)
    res = pl.pallas_call(
        body, name=name, in_specs=[anyspec] * n, out_specs=[anyspec] * n,
        out_shape=[jax.ShapeDtypeStruct(t.shape, t.dtype) for t in lands], input_output_aliases={i: i for i in range(n)},
        scratch_shapes=[pltpu.SemaphoreType.DMA((4 * n,)), pltpu.SemaphoreType.DMA((4 * n,))],
    )(*lands)
    return list(res)


def _spread_start(srcs, per_dest, both, dev, chip, name):
    n = len(srcs)
    npeer = _n_peers(both)
    lands = []
    for s in srcs:
        own = lax.dynamic_index_in_dim(s, chip, 0, keepdims=False) if per_dest else s
        lands.append(lax.dynamic_update_index_in_dim(lax.empty((8,) + own.shape, own.dtype), own, dev, 0))

    def body(*refs):
        s_refs, l_refs, send_sems, recv_sems, token = refs[:n], refs[n:2 * n], refs[2 * n], refs[2 * n + 1], refs[-1]
        for cp in _spread_copies(s_refs, l_refs, send_sems, recv_sems, per_dest, both):
            cp.start()
        token[...] = jnp.zeros_like(token)

    hbm_in = [pltpu.with_memory_space_constraint(t, pltpu.HBM) for t in list(srcs) + lands]
    outs = pl.pallas_call(
        body, name=name,
        out_shape=(pltpu.SemaphoreType.DMA((npeer * n,)), pltpu.SemaphoreType.DMA((npeer * n,)),
                   *[pltpu.HBM(t.shape, t.dtype) for t in hbm_in], jax.ShapeDtypeStruct((8, LANES), F32)),
        in_specs=[_HBM] * (2 * n), out_specs=(_SEM, _SEM, *[_HBM] * (2 * n), pl.BlockSpec(memory_space=pltpu.VMEM)),
        input_output_aliases={i: 2 + i for i in range(2 * n)},
        compiler_params=pltpu.CompilerParams(has_side_effects=_EFFECT),
    )(*hbm_in)
    return (outs[0], outs[1], list(outs[2:2 + n]), list(outs[2 + n:2 + 2 * n])), outs[-1]


def _spread_wait(handle, per_dest, both, after, name):
    send_sems, recv_sems, srcs, lands = handle
    n = len(srcs)

    def body(*refs):
        s_refs, l_refs, send_ref, recv_ref = refs[:n], refs[n:2 * n], refs[2 * n], refs[2 * n + 1]
        for cp in _spread_copies(s_refs, l_refs, send_ref, recv_ref, per_dest, both):
            cp.wait_send()
            cp.wait_recv()

    outs = pl.pallas_call(
        body, name=name, out_shape=tuple(pltpu.HBM(t.shape, t.dtype) for t in srcs + lands),
        in_specs=[_HBM] * (2 * n) + [_SEM, _SEM, pl.BlockSpec(memory_space=pl.ANY)], out_specs=tuple([_HBM] * (2 * n)),
        input_output_aliases={i: i for i in range(2 * n)},
        compiler_params=pltpu.CompilerParams(has_side_effects=_EFFECT),
    )(*srcs, *lands, send_sems, recv_sems, after)
    return list(outs[n:])


def _row_tile(n, cap, mult):
    best = n
    for t in range(mult, min(n, cap) + 1, mult):
        if n % t == 0:
            best = t
    return best


def _pair_add(g2, theirs, half, name):
    _, n, cdim = g2.shape
    tm = _row_tile(n, 512, 16)

    def body(h_ref, a_ref, b_ref, o_ref):
        o_ref[...] = (a_ref[...] + b_ref[...]).astype(o_ref.dtype)

    grid_spec = pltpu.PrefetchScalarGridSpec(
        num_scalar_prefetch=1, grid=(n // tm,),
        in_specs=[pl.BlockSpec((None, tm, cdim), lambda i, h: (h[0], i, 0)), pl.BlockSpec((tm, cdim), lambda i, h: (i, 0))],
        out_specs=pl.BlockSpec((tm, cdim), lambda i, h: (i, 0)))
    return pl.pallas_call(body, name=name, grid_spec=grid_spec, out_shape=jax.ShapeDtypeStruct((n, cdim), BF16),
                          compiler_params=_cparams(("parallel",)))(half.reshape(1).astype(jnp.int32), g2, theirs)


def _adamw_math(w, g, m, v):
    m = ADAM_B1 * m + (1.0 - ADAM_B1) * g
    v = ADAM_B2 * v + (1.0 - ADAM_B2) * (g * g)
    m_hat = m / (1.0 - ADAM_B1 ** ADAM_STEP)
    v_hat = v / (1.0 - ADAM_B2 ** ADAM_STEP)
    delta = -ADAM_LR * (m_hat / (jnp.sqrt(v_hat) + ADAM_EPS) + ADAM_WD * w)
    return delta, m, v


def _adamw(parts, w, m, v, name, tm=128):
    npart, R, C = parts.shape
    tm = min(tm, R)

    def body(p_ref, w_ref, m_ref, v_ref, g_out, d_out, m_out, v_out):
        g = p_ref[0].astype(F32)
        for i in range(1, npart):
            g = g + p_ref[i].astype(F32)
        d, mm, vv = _adamw_math(w_ref[...], g, m_ref[...], v_ref[...])
        g_out[...] = g
        d_out[...] = d
        m_out[...] = mm
        v_out[...] = vv

    spec = pl.BlockSpec((tm, C), lambda i: (i, 0))
    return pl.pallas_call(
        body, name=name, grid=(R // tm,),
        in_specs=[pl.BlockSpec((npart, tm, C), lambda i: (0, i, 0)), spec, spec, spec], out_specs=[spec] * 4,
        out_shape=[jax.ShapeDtypeStruct((R, C), F32)] * 4,
        compiler_params=_cparams(("parallel",)),
    )(parts, w, m, v)


def _sum_parts(parts, name):
    npart, R, C = parts.shape

    def body(p_ref, o_ref):
        g = p_ref[0]
        for i in range(1, npart):
            g = g + p_ref[i]
        o_ref[...] = g

    return pl.pallas_call(body, name=name, out_shape=jax.ShapeDtypeStruct((R, C), F32))(parts)


def _mod_fwd(c_all, w_ada, b_sh, name):
    def body(c_ref, w_ref, b_ref, o_ref):
        o_ref[...] = _dot(_silu(c_ref[...]), w_ref[...], "nn") + b_ref[...]

    return pl.pallas_call(body, name=name, out_shape=jax.ShapeDtypeStruct((c_all.shape[0], w_ada.shape[1]), F32),
                          compiler_params=pltpu.CompilerParams(vmem_limit_bytes=VMEM_LIMIT))(c_all, w_ada, b_sh)


def _mod_wgrad(c_all, dmod_sh, name):
    def body(c_ref, d_ref, o_ref):
        o_ref[...] = _dot(_silu(c_ref[...]), d_ref[...], "tn")

    return pl.pallas_call(body, name=name, out_shape=jax.ShapeDtypeStruct((c_all.shape[1], dmod_sh.shape[1]), F32),
                          compiler_params=pltpu.CompilerParams(vmem_limit_bytes=VMEM_LIMIT))(c_all, dmod_sh)


def _pad_lanes(v):
    return jnp.pad(v, ((0, 0), (0, (-v.shape[1]) % LANES)))


def kernel(x, c, norm1_w, norm2_w, w_ada, b_ada, w_in, conv_w, conv_b, dt_bias, a_log, d_skip, ssd_norm_w, q_norm_w, k_norm_w, attn_norm_w, w_out, w_ff1, w_ff2, loss_target, m_norm1_w, m_norm2_w, m_w_ada, m_b_ada, m_w_in, m_conv_w, m_conv_b, m_dt_bias, m_a_log, m_d_skip, m_ssd_norm_w, m_q_norm_w, m_k_norm_w, m_attn_norm_w, m_w_out, m_w_ff1, m_w_ff2, v_norm1_w, v_norm2_w, v_w_ada, v_b_ada, v_w_in, v_conv_w, v_conv_b, v_dt_bias, v_a_log, v_d_skip, v_ssd_norm_w, v_q_norm_w, v_k_norm_w, v_attn_norm_w, v_w_out, v_w_ff1, v_w_ff2):
    xi, yi, ci = _coords()
    chip = 2 * xi + yi
    dev = 2 * chip + ci
    xs, tgt = x[0], loss_target[0]
    S, D = xs.shape
    DI, AD = NH_SSD * HD, NH_ATT * HD
    CC = DI + 2 * NG * NSTATE
    PW = DI + CC + 3 * AD + LANES
    DFF = w_ff1.shape[2] * 4
    MIX = DI + AD
    o_xbc, o_q, o_k, o_v, o_dt = DI, DI + CC, DI + CC + AD, DI + CC + 2 * AD, DI + CC + 3 * AD

    def half_rows(w):
        r = w.shape[0] // 2
        return lax.dynamic_slice_in_dim(w, ci * r, r, 0).astype(BF16)

    c_all, conv_w_all = _exchange8([c, conv_w[0]], False, "gather_c_conv_w")
    c_all = c_all.reshape(8, D)
    c_all = jnp.pad(c_all, ((0, 8), (0, 0)))
    nmod = w_ada.shape[2]
    b_sh = lax.dynamic_slice_in_dim(b_ada, chip * nmod, nmod, 1)
    mod_sh = _mod_fwd(c_all, w_ada[0], b_sh, "mod_fwd")
    mod_all = _exchange8([mod_sh[:8]], False, "gather_mod")[0]
    mod_me = lax.dynamic_index_in_dim(mod_all[0::2], dev, 1, keepdims=False).reshape(1, 4 * nmod)
    shift1, scale1, gate1, shift2, scale2, gate2 = [mod_me[:, i * D:(i + 1) * D] for i in range(6)]

    g_in = _exchange8([half_rows(w_in[0])], False, "gather_w_in")[0]
    rest_handle, rest_token = _spread_start([half_rows(w_out[0]), half_rows(w_ff1[0]), half_rows(w_ff2[0])], False, True,
                                            dev, chip, "gather_rest_start")
    shift1 = shift1 + rest_token[0, 0]
    wsh = w_in.shape[2]
    w_in_f = g_in.reshape(4, D, wsh).transpose(1, 0, 2).reshape(D, 4 * wsh)
    n_zx = DI + CC
    w_proj = jnp.concatenate([w_in_f[:, :n_zx], w_in_f[:, n_zx + NH_SSD:], w_in_f[:, n_zx:n_zx + NH_SSD],
                              jnp.zeros((D, LANES - NH_SSD), BF16)], axis=1)

    dtb, alog, dsk = _pad_lanes(dt_bias), _pad_lanes(a_log), _pad_lanes(d_skip)
    qw2 = jnp.concatenate([q_norm_w, q_norm_w], axis=1)
    kw2 = jnp.concatenate([k_norm_w, k_norm_w], axis=1)
    conv_w_f = conv_w_all[0::2].transpose(1, 0, 2).reshape(KCONV, CC)

    h1 = _rows("norm1", lambda r, k: ([_normmod(r[0], *k)], []), [(xs, 0, D)], [norm1_w, scale1, shift1],
               [(D, BF16)], [], S)[0]
    proj = _matmul(h1, w_proj, "nn", F32, "in_proj", tn=896)
    xbc = _conv_fwd(proj, o_xbc, CC, conv_w_f, conv_b, "conv_fwd")
    y_ssd, hsave = _ssd_fwd(xbc, proj, o_dt // LANES, dtb, alog, dsk, ssd_norm_w, "ssd_fwd")

    def qk_call(name, col0, w2, scale):
        def body(t_ref, w_ref, o_ref):
            o_ref[...] = _headnorm(t_ref[...], w_ref[...], scale)
        return pl.pallas_call(
            body, name=name, grid=(AD // LANES,),
            in_specs=[pl.BlockSpec((S, LANES), lambda j: (0, j + col0 // LANES)),
                      pl.BlockSpec((1, LANES), lambda j: (0, 0))],
            out_specs=pl.BlockSpec((S, LANES), lambda j: (0, j)),
            out_shape=jax.ShapeDtypeStruct((S, AD), F32), compiler_params=_cparams(("parallel",)),
        )(proj, w2)

    qn = qk_call("q_norm", o_q, qw2, HD ** -0.5)
    kn = qk_call("k_norm", o_k, kw2, 1.0)
    o_att, lse = _attn_fwd(qn, kn, proj, o_v // LANES, "attn_fwd")
    y_att = _rows("attn_out_norm", lambda r, k: ([_rmsw(r[0], k[0])], []), [(o_att, 0, AD)], [attn_norm_w],
                  [(AD, BF16)], [], S)[0]
    g_out, g_ff1, g_ff2 = _spread_wait(rest_handle, False, True, o_att, "gather_rest_wait")
    w_out_f = g_out.reshape(MIX, D)
    w_out_a, w_out_b = w_out_f[:DI], w_out_f[DI:]
    w_ff1_f = g_ff1.reshape(4, D, DFF // 4).transpose(1, 0, 2).reshape(D, DFF)
    w_ff2_f = g_ff2.reshape(DFF, D)
    mix_a = _matmul(y_ssd, w_out_a, "nn", F32, "out_proj_a")
    mix = _matmul(y_att, w_out_b, "nn", F32, "out_proj_b", epilogue=lambda r, e: r + e, extras=(mix_a,))
    x2, h2 = _rows("resid_norm2", lambda r, k: (list(_resid_normmod(r[0], r[1], *k)), []), [(xs, 0, D), (mix, 0, D)],
                   [gate1, norm2_w, scale2, shift2], [(D, F32), (D, BF16)], [], S)
    u = _matmul(h2, w_ff1_f, "nn", F32, "ff1")
    relu2 = lambda t: jnp.square(jnp.maximum(t, 0.0))
    ff = _matmul(u, w_ff2_f, "nn", F32, "ff2", a_fn=relu2)

    def loss_fn(r, k):
        x2_, ff_, t_ = r
        err = x2_ + k[0] * ff_ - t_
        dy_ = err * (1.0 / D)
        ls = jnp.sum(jnp.sum(0.5 * err * err, axis=1, keepdims=True), axis=0, keepdims=True) * (1.0 / D)
        return [dy_, dy_ * k[0]], [ls, jnp.sum(dy_ * ff_, axis=0, keepdims=True)]

    dy, dff, loss_p, dgate2 = _rows("loss", loss_fn, [(x2, 0, D), (ff, 0, D), (tgt, 0, D)], [gate2],
                                    [(D, F32), (D, BF16)], [(1, 1), (1, D)], S)
    du = _matmul(dff, w_ff2_f, "nt", BF16, "ff2_dx", epilogue=lambda r, e: r * (2.0 * jnp.maximum(e, 0.0)), extras=(u,))
    gw_ff2 = _matmul(u, dff, "tn", F32, "ff2_dw", a_fn=relu2, tm=DFF // 4, tn=D,
                     out_shape4=(2, 4, DFF // 8, D), out_map=lambda i, j: i)
    gw_ff1 = _matmul(h2, du, "tn", F32, "ff1_dw", tm=D, tn=DFF // 4,
                     out_shape4=(2, 4, D // 2, DFF // 4), out_map=lambda i, j: j)

    def by_half_cols(g):
        r, c4 = g.shape
        return g.reshape(2, r // 2, 4, c4 // 4).transpose(0, 2, 1, 3)

    def by_half_rows(g):
        r4, cdim = g.shape
        return g.reshape(4, 2, r4 // 8, cdim).transpose(1, 0, 2, 3)

    def scatter_start(layouts, both, tag):
        theirs = _pair_swap(layouts, "pair_swap_" + tag)
        sums = []
        for i, (g2, t) in enumerate(zip(layouts, theirs)):
            _, r2, cdim = t.shape
            sm = _pair_add(g2.reshape(2, 4 * r2, cdim), t.reshape(4 * r2, cdim), ci, "pair_add_%s_%d" % (tag, i))
            sums.append(sm.reshape(4, r2, cdim))
        return _spread_start(sums, True, both, dev, chip, "scatter_%s_start" % tag)

    def scatter_wait(handle, both, after, tag):
        lands = _spread_wait(handle, True, both, after, "scatter_%s_wait" % tag)
        if not both:
            lands = _sibling_fill(lands, "scatter_%s_fill" % tag)
        return [s.reshape(4, 2 * s.shape[1], s.shape[2]) for s in lands]

    ff_handle, ff_token = scatter_start([gw_ff1, gw_ff2], True, "ff")
    dh2 = _matmul(du, w_ff1_f, "nt", F32, "ff1_dx")

    def resid_bwd(r, k):
        x_, mix_, dx2a, dh2_ = r
        _, vjp = jax.vjp(_resid_normmod, x_, mix_, *k)
        dx, dmix_, dg, dnw, dsc, dsh = vjp((dx2a, dh2_))
        return [dx, dmix_], [dg, dnw, dsc, dsh]

    dx2, dmix, dgate1, g_norm2, dscale2, dshift2 = _rows(
        "resid_norm2_bwd", resid_bwd, [(xs, 0, D), (mix, 0, D), (dy, 0, D), (dh2, 0, D)],
        [gate1 + ff_token[0, 0], norm2_w, scale2, shift2], [(D, F32), (D, BF16)], [(1, D)] * 4, S)
    gw_out = jnp.concatenate([_matmul(y_ssd, dmix, "tn", F32, "out_proj_dw_a"),
                              _matmul(y_att, dmix, "tn", F32, "out_proj_dw_b")], axis=0)
    out_handle, out_token = scatter_start([by_half_rows(gw_out)], True, "out")
    dy_ssd = _matmul(dmix, w_out_a, "nt", F32, "out_proj_dx_a")
    dy_att = _matmul(dmix, w_out_b, "nt", F32, "out_proj_dx_b")

    def attn_norm_bwd(r, k):
        o_, dyo = r
        _, vjp = jax.vjp(_rmsw, o_, k[0])
        do_, dw_ = vjp(dyo)
        lo = _lane_mask()
        dd_blocks = []
        for b in range(AD // LANES):
            t = (do_ * o_)[:, b * LANES:(b + 1) * LANES]
            s0 = jnp.sum(jnp.where(lo, t, 0.0), axis=1, keepdims=True)
            s1 = jnp.sum(jnp.where(lo, 0.0, t), axis=1, keepdims=True)
            dd_blocks.append(jnp.where(lo, s0, s1))
        return [do_, jnp.concatenate(dd_blocks, axis=1)], [dw_]

    do_att, dd_att, g_attn_norm = _rows("attn_norm_bwd", attn_norm_bwd, [(o_att, 0, AD), (dy_att, 0, AD)],
                                        [attn_norm_w + out_token[0, 0]], [(AD, F32), (AD, F32)], [(1, AD)], S)
    dq_n, dk_n, dv = _attn_bwd(qn, kn, proj, o_v // LANES, do_att, lse, dd_att, "attn_bwd")

    def qk_bwd_call(name, col0, w2, scale, g):
        def body(t_ref, w_ref, g_ref, o_ref, dw_ref):
            @pl.when(pl.program_id(0) == 0)
            def _():
                dw_ref[...] = jnp.zeros_like(dw_ref)
            _, vjp = jax.vjp(lambda t, w: _headnorm(t, w, scale), t_ref[...], w_ref[...])
            dt_, dw_ = vjp(g_ref[...])
            o_ref[...] = dt_.astype(BF16)
            dw_ref[...] += dw_
        blk = pl.BlockSpec((S, LANES), lambda j: (0, j))
        return pl.pallas_call(
            body, name=name, grid=(AD // LANES,),
            in_specs=[pl.BlockSpec((S, LANES), lambda j: (0, j + col0 // LANES)),
                      pl.BlockSpec((1, LANES), lambda j: (0, 0)), blk],
            out_specs=[blk, pl.BlockSpec((1, LANES), lambda j: (0, 0))],
            out_shape=[jax.ShapeDtypeStruct((S, AD), BF16), jax.ShapeDtypeStruct((1, LANES), F32)],
            compiler_params=_cparams(("arbitrary",)),
        )(proj, w2, g)

    dq, g_qw2 = qk_bwd_call("q_norm_bwd", o_q, qw2, HD ** -0.5, dq_n)
    dk, g_kw2 = qk_bwd_call("k_norm_bwd", o_k, kw2, 1.0, dk_n)
    g_q_norm = g_qw2[:, :HD] + g_qw2[:, HD:]
    g_k_norm = g_kw2[:, :HD] + g_kw2[:, HD:]

    dxbc, dz, ddtr, g_dtb, g_alog, g_dsk, g_ssd_norm = _ssd_bwd(
        xbc, proj, o_dt // LANES, dtb, alog, dsk, ssd_norm_w, hsave, dy_ssd, "ssd_bwd")
    dxbc_pre, g_conv_w, g_conv_b = _conv_bwd(proj, o_xbc, CC, conv_w_f, conv_b, dxbc, "conv_bwd")
    dproj = jnp.concatenate([dz.astype(BF16), dxbc_pre.astype(BF16), dq, dk, dv.astype(BF16), ddtr.astype(BF16)], axis=1)
    gw_proj = _matmul(h1, dproj, "tn", F32, "in_proj_dw", tn=896)
    gw_in = jnp.concatenate([gw_proj[:, :n_zx], gw_proj[:, o_dt:o_dt + NH_SSD], gw_proj[:, n_zx:o_dt]], axis=1)
    in_handle, in_token = scatter_start([by_half_cols(gw_in)], False, "in")
    dh1 = _matmul(dproj, w_proj, "nt", F32, "in_proj_dx", tk=896)

    def norm1_bwd(r, k):
        x_, dh_, dres = r
        _, vjp = jax.vjp(_normmod, x_, *k)
        dx, dnw, dsc, dsh = vjp(dh_)
        return [dx + dres], [dnw, dsc, dsh]

    grad_x, g_norm1, dscale1, dshift1 = _rows("norm1_bwd", norm1_bwd, [(xs, 0, D), (dh1, 0, D), (dx2, 0, D)],
                                              [norm1_w + in_token[0, 0], scale1, shift1], [(D, F32)], [(1, D)] * 3, S)
    dmod =jnp.concatenate([dshift1, dscale1, dgate1, dshift2, dscale2, dgate2], axis=1)

    small = [g_norm1, g_norm2, dmod, g_conv_b, g_dtb, g_alog, g_dsk, g_ssd_norm, _pad_lanes(g_q_norm),
             _pad_lanes(g_k_norm), g_attn_norm, g_conv_w.reshape(1, KCONV * CC)]
    sizes = [t.shape[1] for t in small]
    packed = jnp.concatenate(small, axis=1)
    nrow = -(-packed.shape[1] // LANES // 8) * 8
    packed = jnp.pad(packed, ((0, 0), (0, nrow * LANES - packed.shape[1]))).reshape(nrow, LANES)
    packed_all = _exchange8([packed], False, "gather_small_grads")[0]
    tot = _sum_parts(packed_all, "sum_small_grads").reshape(1, nrow * LANES)
    offs = [sum(sizes[:i]) for i in range(len(sizes))]
    (g_norm1, g_norm2, g_b_ada, g_conv_b, g_dtb, g_alog, g_dsk, g_ssd_norm, g_q_norm, g_k_norm, g_attn_norm,
     g_conv_w) = [tot[:, o:o + n] for o, n in zip(offs, sizes)]
    g_dtb, g_alog, g_dsk = g_dtb[:, :NH_SSD], g_alog[:, :NH_SSD], g_dsk[:, :NH_SSD]
    g_q_norm, g_k_norm = g_q_norm[:, :HD], g_k_norm[:, :HD]
    ccs = CC // 4
    g_conv_w = lax.dynamic_slice_in_dim(g_conv_w.reshape(KCONV, CC), chip * ccs, ccs, 1)

    dmod_all = packed_all.reshape(8, nrow * LANES)[:, offs[2]:offs[2] + 6 * D]
    dmod_sh = jnp.pad(lax.dynamic_slice_in_dim(dmod_all, chip * nmod, nmod, 1), ((0, 8), (0, 0)))
    gw_ada = _mod_wgrad(c_all, dmod_sh, "mod_wgrad")

    parts_ff1, parts_ff2 = scatter_wait(ff_handle, True, in_token, "ff")
    res_ff1 = _adamw(parts_ff1, w_ff1[0], m_w_ff1[0], v_w_ff1[0], "adamw_w_ff1")
    res_ff2 = _adamw(parts_ff2, w_ff2[0], m_w_ff2[0], v_w_ff2[0], "adamw_w_ff2")
    res_out = _adamw(scatter_wait(out_handle, True, in_token, "out")[0], w_out[0], m_w_out[0], v_w_out[0], "adamw_w_out")
    res_ada = _adamw(gw_ada[None], w_ada[0], m_w_ada[0], v_w_ada[0], "adamw_w_ada")
    res_in = _adamw(scatter_wait(in_handle, False, res_ada[0], "in")[0], w_in[0], m_w_in[0], v_w_in[0], "adamw_w_in")

    small_names = ["norm1_w", "norm2_w", "b_ada", "conv_w", "conv_b", "dt_bias", "a_log", "d_skip", "ssd_norm_w",
                   "q_norm_w", "k_norm_w", "attn_norm_w"]
    small_g = dict(norm1_w=g_norm1, norm2_w=g_norm2, b_ada=g_b_ada, conv_w=g_conv_w.reshape(1, KCONV * ccs),
                   conv_b=g_conv_b, dt_bias=g_dtb, a_log=g_alog, d_skip=g_dsk, ssd_norm_w=g_ssd_norm, q_norm_w=g_q_norm,
                   k_norm_w=g_k_norm, attn_norm_w=g_attn_norm)
    small_w = dict(norm1_w=(norm1_w, m_norm1_w, v_norm1_w), norm2_w=(norm2_w, m_norm2_w, v_norm2_w),
                   b_ada=(b_ada, m_b_ada, v_b_ada),
                   conv_w=tuple(t.reshape(1, KCONV * ccs) for t in (conv_w, m_conv_w, v_conv_w)),
                   conv_b=(conv_b, m_conv_b, v_conv_b), dt_bias=(dt_bias, m_dt_bias, v_dt_bias),
                   a_log=(a_log, m_a_log, v_a_log), d_skip=(d_skip, m_d_skip, v_d_skip),
                   ssd_norm_w=(ssd_norm_w, m_ssd_norm_w, v_ssd_norm_w), q_norm_w=(q_norm_w, m_q_norm_w, v_q_norm_w),
                   k_norm_w=(k_norm_w, m_k_norm_w, v_k_norm_w), attn_norm_w=(attn_norm_w, m_attn_norm_w, v_attn_norm_w))
    ssz = [_pad_lanes(small_g[n]).shape[1] for n in small_names]
    soff = [sum(ssz[:i]) for i in range(len(ssz))]
    srow = -(-sum(ssz) // LANES // 8) * 8

    def pack(ts, fill):
        t = jnp.concatenate([jnp.pad(t, ((0, 0), (0, (-t.shape[1]) % LANES)), constant_values=fill) for t in ts], axis=1)
        return jnp.pad(t, ((0, 0), (0, srow * LANES - t.shape[1])), constant_values=fill).reshape(srow, LANES)

    sg = pack([small_g[n] for n in small_names], 0.0)
    sw = pack([small_w[n][0] for n in small_names], 0.0)
    sm_ = pack([small_w[n][1] for n in small_names], 0.0)
    sv = pack([small_w[n][2] for n in small_names], 1.0)
    _, s_delta, s_m, s_v = _adamw(sg[None], sw, sm_, sv, "adamw_small", tm=srow)

    def unpack(t, n):
        i = small_names.index(n)
        return t.reshape(1, srow * LANES)[:, soff[i]:soff[i] + small_g[n].shape[1]].reshape(small_w[n][0].shape)

    loss = lax.psum(loss_p[0, 0], ("x", "y", "c"))
    big_res = dict(w_ada=res_ada, w_in=res_in, w_out=res_out, w_ff1=res_ff1, w_ff2=res_ff2)
    order = ["norm1_w", "norm2_w", "w_ada", "b_ada", "w_in", "conv_w", "conv_b", "dt_bias", "a_log", "d_skip",
             "ssd_norm_w", "q_norm_w", "k_norm_w", "attn_norm_w", "w_out", "w_ff1", "w_ff2"]
    grads, deltas, new_m, new_v = [], [], [], []
    for n in order:
        if n in big_res:
            g_, d_, m_, v_ = [t[None] for t in big_res[n]]
        else:
            g_ = small_g[n].reshape(small_w[n][0].shape)
            d_, m_, v_ = unpack(s_delta, n), unpack(s_m, n), unpack(s_v, n)
            if n == "conv_w":
                g_, d_, m_, v_ = [t.reshape(conv_w.shape) for t in (g_, d_, m_, v_)]
        grads.append(g_)
        deltas.append(d_)
        new_m.append(m_)
        new_v.append(v_)
    return (loss, grad_x[None], *grads, *deltas, *new_m, *new_v)
```

```python
import functools

import jax
import jax.numpy as jnp
from jax import lax
from jax.experimental import pallas as pl
from jax.experimental.pallas import tpu as pltpu

F32, BF16 = jnp.float32, jnp.bfloat16
EPS = 1e-6
HD = 64
NH_SSD = 16
NG = 4
NSTATE = 128
KCONV = 4
CHUNK = 128
NH_ATT = 16
PATTERNS = ((128, 1), (512, 4), (2048, 16))
ABLK = 128
LANES = 128
ADAM_LR, ADAM_B1, ADAM_B2, ADAM_EPS, ADAM_WD, ADAM_STEP = 0.001, 0.9, 0.999, 1e-08, 0.01, 10
VMEM_LIMIT = 56 * 1024 * 1024
MESH = pl.DeviceIdType.MESH
NEG = -1e30

_DN = {"nn": (((1,), (0,)), ((), ())), "nt": (((1,), (1,)), ((), ())), "tn": (((0,), (0,)), ((), ()))}


def _cparams(sem):
    return pltpu.CompilerParams(dimension_semantics=sem, vmem_limit_bytes=VMEM_LIMIT)


def _tile(n, cap):
    if n % LANES or n <= LANES:
        return n
    best = LANES
    for t in range(LANES, min(n, cap) + 1, LANES):
        if n % t == 0:
            best = t
    return best


def _silu(x):
    return x / (1.0 + jnp.exp(-x))


def _softplus(x):
    return jnp.maximum(x, 0.0) + jnp.log(1.0 + jnp.exp(-jnp.abs(x)))


def _dot(a, b, dims):
    return lax.dot_general(a.astype(BF16), b.astype(BF16), _DN[dims], preferred_element_type=F32)


def _matmul(a, b, dims, out_dtype, name, a_fn=None, epilogue=None, extras=(), tm=1024, tn=1024, tk=1024,
            chip_of_tile=None):
    if dims == "nn":
        (M, K), (_, N) = a.shape, b.shape
    elif dims == "nt":
        (M, K), (N, _) = a.shape, b.shape
    else:
        (K, M), (_, N) = a.shape, b.shape
    tm, tn, tk = _tile(M, tm), _tile(N, tn), _tile(K, tk)
    nk = K // tk
    ne = len(extras)

    def body(a_ref, b_ref, *rest):
        e_refs, o_ref = rest[:ne], rest[ne]
        av = a_ref[...]
        if a_fn is not None:
            av = a_fn(av)
        part = _dot(av, b_ref[...], dims)

        def finish(r):
            if epilogue is not None:
                r = epilogue(r, *[e[...] for e in e_refs])
            o_ref[...] = r.astype(out_dtype).reshape(o_ref.shape)

        if nk == 1:
            finish(part)
            return
        acc = rest[ne + 1]
        k = pl.program_id(2)

        @pl.when(k == 0)
        def _():
            acc[...] = part

        @pl.when(k > 0)
        def _():
            acc[...] += part

        @pl.when(k == nk - 1)
        def _():
            finish(acc[...])

    a_spec = pl.BlockSpec((tk, tm), lambda i, j, k: (k, i)) if dims == "tn" else pl.BlockSpec((tm, tk), lambda i, j, k: (i, k))
    b_spec = pl.BlockSpec((tn, tk), lambda i, j, k: (j, k)) if dims == "nt" else pl.BlockSpec((tk, tn), lambda i, j, k: (k, j))
    o_spec = pl.BlockSpec((tm, tn), lambda i, j, k: (i, j))
    out_spec, out_dims = o_spec, (M, N)
    if chip_of_tile is not None:
        assert (M // tm) * (N // tn) == 4
        out_spec = pl.BlockSpec((None, tm, tn), lambda i, j, k: (chip_of_tile(i, j), 0, 0))
        out_dims = (4, tm, tn)
    return pl.pallas_call(
        body, name=name, grid=(M // tm, N // tn, nk),
        in_specs=[a_spec, b_spec] + [o_spec] * ne, out_specs=out_spec,
        out_shape=jax.ShapeDtypeStruct(out_dims, out_dtype),
        scratch_shapes=[pltpu.VMEM((tm, tn), F32)] if nk > 1 else [],
        compiler_params=_cparams(("parallel", "parallel", "arbitrary")),
    )(a, b, *extras)


def _rows(name, fn, rows, consts, outs, accs, n_rows, tm=256):
    tm = min(tm, n_rows)
    nr, nc, no, na = len(rows), len(consts), len(outs), len(accs)

    def body(*refs):
        r_refs, c_refs = refs[:nr], refs[nr:nr + nc]
        o_refs, a_refs = refs[nr + nc:nr + nc + no], refs[nr + nc + no:]
        o_vals, a_vals = fn([r[...] for r in r_refs], [c[...] for c in c_refs])
        for ref, val in zip(o_refs, o_vals):
            ref[...] = val.astype(ref.dtype)
        if na:
            @pl.when(pl.program_id(0) == 0)
            def _():
                for ref in a_refs:
                    ref[...] = jnp.zeros_like(ref)
            for ref, val in zip(a_refs, a_vals):
                ref[...] += val

    in_specs = [pl.BlockSpec((tm, w), lambda i, cb=cb: (i, cb)) for (_, cb, w) in rows]
    in_specs += [pl.BlockSpec(cst.shape, lambda i, nd=cst.ndim: (0,) * nd) for cst in consts]
    out_specs = [pl.BlockSpec((tm, w), lambda i: (i, 0)) for (w, _) in outs]
    out_specs += [pl.BlockSpec(s, lambda i: (0, 0)) for s in accs]
    out_shape = [jax.ShapeDtypeStruct((n_rows, w), dt) for (w, dt) in outs]
    out_shape += [jax.ShapeDtypeStruct(s, F32) for s in accs]
    res = pl.pallas_call(
        body, name=name, grid=(n_rows // tm,), in_specs=in_specs, out_specs=out_specs, out_shape=out_shape,
        compiler_params=_cparams(("arbitrary",)),
    )(*[r[0] for r in rows], *consts)
    return res


def _normmod(x, nw, sc, sh):
    r = lax.rsqrt(jnp.mean(x * x, axis=-1, keepdims=True) + EPS)
    return (x * r) * nw * (1.0 + sc) + sh


def _resid_normmod(x, mix, g, nw, sc, sh):
    x2 = x + g * mix
    return x2, _normmod(x2, nw, sc, sh)


def _rmsw(o, w):
    return o * lax.rsqrt(jnp.mean(o * o, axis=-1, keepdims=True) + EPS) * w


def _lane_mask():
    return lax.broadcasted_iota(jnp.int32, (1, LANES), 1) < HD


def _headnorm(t, w, scale):
    lo = _lane_mask()
    t2 = t * t
    s0 = jnp.sum(jnp.where(lo, t2, 0.0), axis=1, keepdims=True)
    s1 = jnp.sum(jnp.where(lo, 0.0, t2), axis=1, keepdims=True)
    ms = jnp.where(lo, s0, s1) * (1.0 / HD)
    return t * lax.rsqrt(ms + EPS) * w * scale


CONV_ROWS = 128
CONV_HALO = 8


def _conv_cols(n_ch):
    return _tile(n_ch, LANES)


def _conv_fwd(proj, col0, n_ch, conv_w, conv_b, name):
    S = proj.shape[0]
    tc = _conv_cols(n_ch)

    R, H = CONV_ROWS, CONV_HALO

    def body(u_ref, w_ref, b_ref, o_ref):
        w = [w_ref[i:i + 1, :] for i in range(KCONV)]
        b = b_ref[...]

        def chunk(ext):
            acc = b + w[KCONV - 1] * ext[H:]
            for i in range(KCONV - 1):
                acc = acc + w[i] * pltpu.roll(ext, KCONV - 1 - i, 0)[H:]
            return _silu(acc)

        o_ref[0:R, :] = chunk(jnp.concatenate([jnp.zeros((H, tc), F32), u_ref[0:R, :]], axis=0))

        def step(c, carry):
            r0 = pl.multiple_of(c * R, R)
            o_ref[pl.ds(r0, R), :] = chunk(u_ref[pl.ds(pl.multiple_of(r0 - H, H), R + H), :])
            return carry

        lax.fori_loop(1, S // R, step, 0)

    return pl.pallas_call(
        body, name=name, grid=(n_ch // tc,),
        in_specs=[pl.BlockSpec((S, tc), lambda j: (0, j + col0 // tc)),
                  pl.BlockSpec((KCONV, tc), lambda j: (0, j)), pl.BlockSpec((1, tc), lambda j: (0, j))],
        out_specs=pl.BlockSpec((S, tc), lambda j: (0, j)),
        out_shape=jax.ShapeDtypeStruct((S, n_ch), F32),
        compiler_params=_cparams(("parallel",)),
    )(proj, conv_w, conv_b)


def _conv_bwd(proj, col0, n_ch, conv_w, conv_b, dxbc, name):
    S = proj.shape[0]
    tc = _conv_cols(n_ch)

    R, H = CONV_ROWS, CONV_HALO

    def body(u_ref, w_ref, b_ref, g_ref, du_ref, dw_ref, db_ref):
        w = [w_ref[i:i + 1, :] for i in range(KCONV)]
        b = b_ref[...]
        pad = jnp.zeros((H, tc), F32)

        def chunk(u_ext, g_ext):
            taps = [pltpu.roll(u_ext, KCONV - 1 - i, 0)[H:] for i in range(KCONV - 1)] + [u_ext[H:]]
            acc = b
            for i in range(KCONV):
                acc = acc + w[i] * taps[i]
            sig = 1.0 / (1.0 + jnp.exp(-acc))
            dacc = g_ext * (sig * (1.0 + acc * (1.0 - sig)))
            du = w[KCONV - 1] * dacc[:R]
            for i in range(KCONV - 1):
                du = du + w[i] * pltpu.roll(dacc, R + H - (KCONV - 1 - i), 0)[:R]
            d = dacc[:R]
            return du, [jnp.sum(d * t[:R], axis=0, keepdims=True) for t in taps], jnp.sum(d, axis=0, keepdims=True)

        du, dws, db = chunk(jnp.concatenate([pad, u_ref[0:R + H, :]], axis=0), g_ref[0:R + H, :])
        du_ref[0:R, :] = du

        def step(c, carry):
            r0 = pl.multiple_of(c * R, R)
            du_c, dws_c, db_c = chunk(u_ref[pl.ds(pl.multiple_of(r0 - H, H), R + 2 * H), :], g_ref[pl.ds(r0, R + H), :])
            du_ref[pl.ds(r0, R), :] = du_c
            return [a + b_ for a, b_ in zip(carry[0], dws_c)], carry[1] + db_c

        dws, db = lax.fori_loop(1, S // R - 1, step, (dws, db))
        du, dws_l, db_l = chunk(jnp.concatenate([u_ref[S - R - H:S, :], pad], axis=0),
                                jnp.concatenate([g_ref[S - R:S, :], pad], axis=0))
        du_ref[S - R:S, :] = du
        for i in range(KCONV):
            dw_ref[i:i + 1, :] = dws[i] + dws_l[i]
        db_ref[...] = db + db_l

    return pl.pallas_call(
        body, name=name, grid=(n_ch // tc,),
        in_specs=[pl.BlockSpec((S, tc), lambda j: (0, j + col0 // tc)),
                  pl.BlockSpec((KCONV, tc), lambda j: (0, j)), pl.BlockSpec((1, tc), lambda j: (0, j)),
                  pl.BlockSpec((S, tc), lambda j: (0, j))],
        out_specs=[pl.BlockSpec((S, tc), lambda j: (0, j)), pl.BlockSpec((KCONV, tc), lambda j: (0, j)),
                   pl.BlockSpec((1, tc), lambda j: (0, j))],
        out_shape=[jax.ShapeDtypeStruct((S, n_ch), F32), jax.ShapeDtypeStruct((KCONV, n_ch), F32),
                   jax.ShapeDtypeStruct((1, n_ch), F32)],
        compiler_params=_cparams(("parallel",)),
    )(proj, conv_w, conv_b, dxbc)


@functools.partial(jax.custom_vjp, nondiff_argnums=(2,))
def _mm(a, b, dims):
    return _dot(a, b, dims)


def _mm_fwd(a, b, dims):
    return _dot(a, b, dims), (a, b)


def _mm_bwd(dims, res, g):
    a, b = res
    if dims == "nn":
        return _dot(g, b, "nt"), _dot(a, g, "tn")
    if dims == "nt":
        return _dot(g, b, "nn"), _dot(g, a, "tn")
    return _dot(b, g, "nt"), _dot(a, g, "nn")


_mm.defvjp(_mm_fwd, _mm_bwd)


def _tri_dot(x, upper):
    n = x.shape[0]
    r = lax.broadcasted_iota(jnp.int32, (n, n), 0)
    c = lax.broadcasted_iota(jnp.int32, (n, n), 1)
    t = jnp.where((r <= c) if upper else (r >= c), 1.0, 0.0)
    return lax.dot_general(t, x, _DN["nn"], precision=lax.Precision.HIGHEST, preferred_element_type=F32)


@jax.custom_vjp
def _cumsum_rows(x):
    return _tri_dot(x, False)


_cumsum_rows.defvjp(lambda x: (_tri_dot(x, False), None), lambda _, g: (_tri_dot(g, True),))


def _ssd_chunk(xs_p, bm_g, cm_g, dtr, z_p, dtb, alog, dsk, nw_p, h_p):
    L = dtr.shape[0]
    n_pairs = len(xs_p)
    ppg = n_pairs // len(bm_g)
    lane = lax.broadcasted_iota(jnp.int32, (1, LANES), 1)
    sub = lax.broadcasted_iota(jnp.int32, (LANES, 1), 0)
    lo = lane < HD
    row_l = lax.broadcasted_iota(jnp.int32, (L, 1), 0)
    tri = lax.broadcasted_iota(jnp.int32, (L, L), 0) >= lax.broadcasted_iota(jnp.int32, (L, L), 1)

    dt = _softplus(dtr + dtb)
    acs = _cumsum_rows(dt * (-jnp.exp(alog)))
    acs_t = acs.T
    a_last = jnp.sum(jnp.where(row_l == L - 1, acs, 0.0), axis=0, keepdims=True)
    e_acs = jnp.exp(acs)
    dec = jnp.exp(a_last - acs)
    cdec = jnp.exp(a_last)

    def colv(m, h):
        return jnp.sum(jnp.where(lane == h, m, 0.0), axis=1, keepdims=True)

    def rowv(mt, h):
        return jnp.sum(jnp.where(sub == h, mt, 0.0), axis=0, keepdims=True)

    def pair(m, h0):
        return jnp.where(lo, colv(m, h0), colv(m, h0 + 1))

    ys, hs = [], []
    cb = None
    for p in range(n_pairs):
        g, h0 = p // ppg, 2 * p
        bmat, cmat = bm_g[g], cm_g[g]
        if p % ppg == 0:
            cb = _mm(cmat, bmat, "nt")
        x = xs_p[p]
        xdt = x * pair(dt, h0)
        yd = []
        for h in (h0, h0 + 1):
            seg = colv(acs, h) - rowv(acs_t, h)
            lm = jnp.where(tri, jnp.exp(jnp.where(tri, seg, 0.0)), 0.0)
            yd.append(_mm(cb * lm, xdt, "nn"))
        y = jnp.where(lo, yd[0], yd[1])
        y = y + _mm(cmat, h_p[p], "nt") * pair(e_acs, h0)
        st = _mm(xdt * pair(dec, h0), bmat, "tn")
        cd_col = jnp.where(sub < HD, colv(cdec, h0), colv(cdec, h0 + 1))
        hs.append(h_p[p] * cd_col + st)
        ys.append(y + pair(dsk, h0) * x)

    y2 = [ys[p] * _silu(z_p[p]) for p in range(n_pairs)]
    outs = []
    for g in range(len(bm_g)):
        ps = range(g * ppg, (g + 1) * ppg)
        ss = sum(jnp.sum(y2[p] * y2[p], axis=1, keepdims=True) for p in ps)
        rs = lax.rsqrt(ss * (1.0 / (ppg * LANES)) + EPS)
        outs += [y2[p] * rs * nw_p[p] for p in ps]
    return outs, hs


def _ssd_slices(xbc_ref, z_ref, nw_ref, di):
    n_pairs = di // LANES
    xs_p = [xbc_ref[:, p * LANES:(p + 1) * LANES] for p in range(n_pairs)]
    bm_g = [xbc_ref[:, di + g * NSTATE:di + (g + 1) * NSTATE] for g in range(NG)]
    cm_g = [xbc_ref[:, di + (NG + g) * NSTATE:di + (NG + g + 1) * NSTATE] for g in range(NG)]
    z_p = [z_ref[:, p * LANES:(p + 1) * LANES] for p in range(n_pairs)]
    nw_p = [nw_ref[:, p * LANES:(p + 1) * LANES] for p in range(n_pairs)]
    return xs_p, bm_g, cm_g, z_p, nw_p


def _ssd_fwd(xbc, proj, dt_cb, dtb, alog, dsk, nw, name):
    S, cc = xbc.shape
    di = NH_SSD * HD
    n_pairs = di // LANES
    nchunk = S // CHUNK

    def body(xbc_ref, z_ref, dtr_ref, dtb_ref, alog_ref, dsk_ref, nw_ref, y_ref, hs_ref, h_scr):
        @pl.when(pl.program_id(0) == 0)
        def _():
            h_scr[...] = jnp.zeros_like(h_scr)

        xs_p, bm_g, cm_g, z_p, nw_p = _ssd_slices(xbc_ref, z_ref, nw_ref, di)
        h_p = [h_scr[p * LANES:(p + 1) * LANES, :] for p in range(n_pairs)]
        hs_ref[...] = h_scr[...]
        outs, hs = _ssd_chunk(xs_p, bm_g, cm_g, dtr_ref[...], z_p, dtb_ref[...], alog_ref[...], dsk_ref[...], nw_p, h_p)
        for p in range(n_pairs):
            y_ref[:, p * LANES:(p + 1) * LANES] = outs[p].astype(y_ref.dtype)
            h_scr[p * LANES:(p + 1) * LANES, :] = hs[p]

    vec = pl.BlockSpec((1, LANES), lambda c: (0, 0))
    return pl.pallas_call(
        body, name=name, grid=(nchunk,),
        in_specs=[pl.BlockSpec((CHUNK, cc), lambda c: (c, 0)), pl.BlockSpec((CHUNK, di), lambda c: (c, 0)),
                  pl.BlockSpec((CHUNK, LANES), lambda c: (c, dt_cb)), vec, vec, vec,
                  pl.BlockSpec((1, di), lambda c: (0, 0))],
        out_specs=[pl.BlockSpec((CHUNK, di), lambda c: (c, 0)), pl.BlockSpec((None, di, NSTATE), lambda c: (c, 0, 0))],
        out_shape=[jax.ShapeDtypeStruct((S, di), BF16), jax.ShapeDtypeStruct((nchunk, di, NSTATE), F32)],
        scratch_shapes=[pltpu.VMEM((di, NSTATE), F32)],
        compiler_params=_cparams(("arbitrary",)),
    )(xbc, proj, proj, dtb, alog, dsk, nw)


def _ssd_bwd(xbc, proj, dt_cb, dtb, alog, dsk, nw, hsave, dy, name):
    S, cc = xbc.shape
    di = NH_SSD * HD
    n_pairs = di // LANES
    nchunk = S // CHUNK

    def body(xbc_ref, z_ref, dtr_ref, dtb_ref, alog_ref, dsk_ref, nw_ref, hs_ref, dy_ref,
             dxbc_ref, dz_ref, ddtr_ref, ddtb_ref, dalog_ref, ddsk_ref, dnw_ref, dh_scr):
        @pl.when(pl.program_id(0) == 0)
        def _():
            dh_scr[...] = jnp.zeros_like(dh_scr)
            ddtb_ref[...] = jnp.zeros_like(ddtb_ref)
            dalog_ref[...] = jnp.zeros_like(dalog_ref)
            ddsk_ref[...] = jnp.zeros_like(ddsk_ref)
            dnw_ref[...] = jnp.zeros_like(dnw_ref)

        xs_p, bm_g, cm_g, z_p, nw_p = _ssd_slices(xbc_ref, z_ref, nw_ref, di)
        h_p = [hs_ref[p * LANES:(p + 1) * LANES, :] for p in range(n_pairs)]
        dy_p = [dy_ref[:, p * LANES:(p + 1) * LANES].astype(F32) for p in range(n_pairs)]
        dh_p = [dh_scr[p * LANES:(p + 1) * LANES, :] for p in range(n_pairs)]
        _, vjp = jax.vjp(_ssd_chunk, xs_p, bm_g, cm_g, dtr_ref[...], z_p, dtb_ref[...], alog_ref[...], dsk_ref[...],
                         nw_p, h_p)
        dxs, dbm, dcm, ddtr, dz, ddtb, dalog, ddsk, dnw, dh = vjp((dy_p, dh_p))
        for p in range(n_pairs):
            sl = slice(p * LANES, (p + 1) * LANES)
            dxbc_ref[:, sl] = dxs[p]
            dz_ref[:, sl] = dz[p]
            dnw_ref[:, sl] += dnw[p]
            dh_scr[sl, :] = dh[p]
        for g in range(NG):
            dxbc_ref[:, di + g * NSTATE:di + (g + 1) * NSTATE] = dbm[g]
            dxbc_ref[:, di + (NG + g) * NSTATE:di + (NG + g + 1) * NSTATE] = dcm[g]
        ddtr_ref[...] = ddtr
        ddtb_ref[...] += ddtb
        dalog_ref[...] += dalog
        ddsk_ref[...] += ddsk

    last = nchunk - 1
    vec = pl.BlockSpec((1, LANES), lambda c: (0, 0))
    return pl.pallas_call(
        body, name=name, grid=(nchunk,),
        in_specs=[pl.BlockSpec((CHUNK, cc), lambda c: (last - c, 0)), pl.BlockSpec((CHUNK, di), lambda c: (last - c, 0)),
                  pl.BlockSpec((CHUNK, LANES), lambda c: (last - c, dt_cb)), vec, vec, vec,
                  pl.BlockSpec((1, di), lambda c: (0, 0)),
                  pl.BlockSpec((None, di, NSTATE), lambda c: (last - c, 0, 0)),
                  pl.BlockSpec((CHUNK, di), lambda c: (last - c, 0))],
        out_specs=[pl.BlockSpec((CHUNK, cc), lambda c: (last - c, 0)), pl.BlockSpec((CHUNK, di), lambda c: (last - c, 0)),
                   pl.BlockSpec((CHUNK, LANES), lambda c: (last - c, 0)), vec, vec, vec,
                   pl.BlockSpec((1, di), lambda c: (0, 0))],
        out_shape=[jax.ShapeDtypeStruct((S, cc), F32), jax.ShapeDtypeStruct((S, di), F32),
                   jax.ShapeDtypeStruct((S, LANES), F32), jax.ShapeDtypeStruct((1, LANES), F32),
                   jax.ShapeDtypeStruct((1, LANES), F32), jax.ShapeDtypeStruct((1, LANES), F32),
                   jax.ShapeDtypeStruct((1, di), F32)],
        scratch_shapes=[pltpu.VMEM((di, NSTATE), F32)],
        compiler_params=_cparams(("arbitrary",)),
    )(xbc, proj, proj, dtb, alog, dsk, nw, hsave, dy)


def _band_masks(rows_q, rows_k):
    qi = lax.broadcasted_iota(jnp.int32, (rows_q, rows_k), 0)
    ki = lax.broadcasted_iota(jnp.int32, (rows_q, rows_k), 1)
    return qi, ki


def _class_chunks(n_rows, d):
    per_class = n_rows // d
    ch = min(per_class, 256)
    out = []
    for r in range(d):
        for c0 in range(0, per_class, ch):
            tok = pl.ds(c0, ch) if d == 1 else pl.ds(r + d * c0, ch, stride=d)
            out.append((tok, pl.ds(r * per_class + c0, ch)))
    return out


def _to_class_order(src_ref, dst_ref, n_rows, d):
    for tok, cls in _class_chunks(n_rows, d):
        dst_ref[cls, :] = src_ref[tok, :].astype(dst_ref.dtype)


def _blk_rows(t):
    return pl.ds(pl.multiple_of(t * ABLK, ABLK), ABLK)


def _head_lanes(msk, t, t_rolled):
    return jnp.where(msk, t, t_rolled)


def _zero_unless(msk, t):
    return jnp.where(msk, t, jnp.zeros_like(t))


def _attn_fwd(qn, kn, proj, v_cb, name):
    S, ad = qn.shape
    nb = S // ABLK
    nbr = len(PATTERNS)

    def body(q_ref, k_ref, v_ref, o_ref, lse_ref, qc, kc, vc, ob, mb, lb, m_s, l_s):
        lo = _lane_mask()
        qi, ki = _band_masks(ABLK, ABLK)
        cur_ok, prev_ok = ki <= qi, ki >= qi
        for bi, (_, d) in enumerate(PATTERNS):
            nbc = S // d // ABLK
            first, last = bi == 0, bi == nbr - 1
            qs, ks, vs = q_ref, k_ref, v_ref
            if d > 1:
                qs, ks, vs = qc, kc, vc
                for src, dst in ((q_ref, qc), (k_ref, kc), (v_ref, vc)):
                    _to_class_order(src, dst, S, d)
            o_dst, m_dst, l_dst = (o_ref, m_s, l_s) if first else (ob, mb, lb)

            def blk(t, carry, nbc=nbc, qs=qs, ks=ks, vs=vs, o_dst=o_dst, m_dst=m_dst, l_dst=l_dst):
                rows, prow = _blk_rows(t), _blk_rows(jnp.maximum(t - 1, 0))
                has_prev = (t % nbc) != 0
                qv = qs[rows, :]
                q2 = jnp.concatenate([_zero_unless(lo, qv), _zero_unless(jnp.logical_not(lo), qv)], axis=0).astype(BF16)
                ok_c = jnp.concatenate([cur_ok, cur_ok], axis=0)
                ok_p = jnp.concatenate([prev_ok, prev_ok], axis=0) & has_prev
                s_c = jnp.where(ok_c, _dot(q2, ks[rows, :], "nt"), NEG)
                s_p = jnp.where(ok_p, _dot(q2, ks[prow, :], "nt"), NEG)
                m = jnp.max(jnp.maximum(s_c, s_p), axis=1, keepdims=True)
                p_c, p_p = jnp.exp(s_c - m), jnp.exp(s_p - m)
                l = jnp.sum(p_c + p_p, axis=1, keepdims=True)
                o2 = _dot(p_c, vs[rows, :], "nn") + _dot(p_p, vs[prow, :], "nn")
                o_dst[rows, :] = jnp.where(lo, o2[:ABLK], o2[ABLK:])
                m_dst[rows, :] = jnp.where(lo, m[:ABLK], m[ABLK:])
                l_dst[rows, :] = jnp.where(lo, l[:ABLK], l[ABLK:])
                return carry

            lax.fori_loop(0, nb, blk, 0, unroll=8)
            if first:
                continue
            for tok, cls in _class_chunks(S, d):
                m_old, m_b = m_s[tok, :], mb[cls, :]
                m_new = jnp.maximum(m_old, m_b)
                a, b = jnp.exp(m_old - m_new), jnp.exp(m_b - m_new)
                l_new = a * l_s[tok, :] + b * lb[cls, :]
                o_new = a * o_ref[tok, :] + b * ob[cls, :]
                if last:
                    o_ref[tok, :] = o_new / l_new
                    lse_ref[tok, :] = m_new + jnp.log(l_new)
                else:
                    o_ref[tok, :] = o_new
                    m_s[tok, :] = m_new
                    l_s[tok, :] = l_new

    col = pl.BlockSpec((S, LANES), lambda h: (0, h))
    return pl.pallas_call(
        body, name=name, grid=(ad // LANES,),
        in_specs=[col, col, pl.BlockSpec((S, LANES), lambda h: (0, h + v_cb))], out_specs=[col, col],
        out_shape=[jax.ShapeDtypeStruct((S, ad), F32), jax.ShapeDtypeStruct((S, ad), F32)],
        scratch_shapes=[pltpu.VMEM((S, LANES), BF16)] * 3 + [pltpu.VMEM((S, LANES), F32)] * 5,
        compiler_params=_cparams(("parallel",)),
    )(qn, kn, proj)


def _attn_bwd(qn, kn, proj, v_cb, do, lse, dd, name):
    S, ad = qn.shape
    nb = S // ABLK

    def body(q_ref, k_ref, v_ref, do_ref, lse_ref, dd_ref, dq_ref, dk_ref, dv_ref,
             qc, kc, vc, doc, lsec, ddc, dqc, dkc, dvc):
        lo = _lane_mask()
        qi, ki = _band_masks(ABLK, ABLK)
        cur_ok, prev_ok = ki <= qi, ki >= qi
        for bi, (_, d) in enumerate(PATTERNS):
            nbc = S // d // ABLK
            first = bi == 0
            token_order = (q_ref, k_ref, v_ref, do_ref, lse_ref, dd_ref)
            class_order = (qc, kc, vc, doc, lsec, ddc)
            if d > 1:
                for src, dst in zip(token_order, class_order):
                    _to_class_order(src, dst, S, d)
            qs, ks, vs, dos, lses, dds = class_order if d > 1 else token_order
            dq_dst, dk_dst, dv_dst = (dq_ref, dk_ref, dv_ref) if first else (dqc, dkc, dvc)
            dk_dst[...] = jnp.zeros_like(dk_dst)
            dv_dst[...] = jnp.zeros_like(dv_dst)

            def blk(t, carry, nbc=nbc, qs=qs, ks=ks, vs=vs, dos=dos, lses=lses, dds=dds,
                    dq_dst=dq_dst, dk_dst=dk_dst, dv_dst=dv_dst):
                rows, prow = _blk_rows(t), _blk_rows(jnp.maximum(t - 1, 0))
                has_prev = (t % nbc) != 0
                qv, dov, lse_b, dd_b = qs[rows, :], dos[rows, :], lses[rows, :], dds[rows, :]
                lse_r, dd_r = pltpu.roll(lse_b, HD, 1), pltpu.roll(dd_b, HD, 1)
                nlo = jnp.logical_not(lo)
                q2 = jnp.concatenate([_zero_unless(lo, qv), _zero_unless(nlo, qv)], axis=0).astype(BF16)
                do2 = jnp.concatenate([_zero_unless(lo, dov), _zero_unless(nlo, dov)], axis=0).astype(BF16)
                lse2 = jnp.concatenate([_head_lanes(lo, lse_b, lse_r), _head_lanes(nlo, lse_b, lse_r)], axis=0)
                dd2 = jnp.concatenate([_head_lanes(lo, dd_b, dd_r), _head_lanes(nlo, dd_b, dd_r)], axis=0)
                dq2 = None
                for krows, vmask in ((rows, cur_ok), (prow, prev_ok & has_prev)):
                    kv, vv = ks[krows, :], vs[krows, :]
                    vmask2 = jnp.concatenate([vmask, vmask], axis=0)
                    s = jnp.where(vmask2, _dot(q2, kv, "nt"), NEG)
                    p = jnp.exp(s - lse2)
                    ds = p * (_dot(do2, vv, "nt") - dd2)
                    dqk = _dot(ds, kv, "nn")
                    dq2 = dqk if dq2 is None else dq2 + dqk
                    dv_dst[krows, :] += _dot(p, do2, "tn")
                    dk_dst[krows, :] += _dot(ds, q2, "tn")
                dq_dst[rows, :] = jnp.where(lo, dq2[:ABLK], dq2[ABLK:])
                return carry

            lax.fori_loop(0, nb, blk, 0, unroll=4)
            if not first:
                for tok, cls in _class_chunks(S, d):
                    dq_ref[tok, :] = dq_ref[tok, :] + dqc[cls, :]
                    dk_ref[tok, :] = dk_ref[tok, :] + dkc[cls, :]
                    dv_ref[tok, :] = dv_ref[tok, :] + dvc[cls, :]

    col = pl.BlockSpec((S, LANES), lambda h: (0, h))
    col1 = pl.BlockSpec((S, LANES), lambda h: (0, h), pipeline_mode=pl.Buffered(1))
    vcol1 = pl.BlockSpec((S, LANES), lambda h: (0, h + v_cb), pipeline_mode=pl.Buffered(1))
    return pl.pallas_call(
        body, name=name, grid=(ad // LANES,),
        in_specs=[col, col, vcol1, col1, col1, col1], out_specs=[col, col, col],
        out_shape=[jax.ShapeDtypeStruct((S, ad), F32)] * 3,
        scratch_shapes=[pltpu.VMEM((S, LANES), BF16)] * 4 + [pltpu.VMEM((S, LANES), F32)] * 5,
        compiler_params=_cparams(("parallel",)),
    )(qn, kn, proj, do, lse, dd)


def _coords():
    return lax.axis_index("x"), lax.axis_index("y"), lax.axis_index("c")


def _exchange8(xs, per_dest, name):
    n = len(xs)
    blk = [x.shape[1:] if per_dest else x.shape for x in xs]

    def body(*refs):
        ins, outs = refs[:n], refs[n:2 * n]
        send_sems, recv_sems, local_sems = refs[2 * n:]
        x, y, c = _coords()
        sibling = (x, y, 1 - c)
        chips = [(1 - x, y), (x, 1 - y), (1 - x, 1 - y)]
        first, passed, mine = [], [], []
        for a in range(n):
            def src_for(cx, cy, a=a):
                return ins[a].at[2 * cx + cy] if per_dest else ins[a]

            def slot(px, py, pc, a=a):
                return outs[a].at[4 * px + 2 * py + pc]

            def copy(k, src, dst, to, a=a):
                return pltpu.make_async_remote_copy(src_ref=src, dst_ref=dst, send_sem=send_sems.at[7 * a + k],
                                                    recv_sem=recv_sems.at[7 * a + k], device_id=to, device_id_type=MESH)

            m = pltpu.make_async_copy(src_for(x, y), slot(x, y, c), local_sems.at[a])
            m.start()
            mine.append(m)
            cps = [copy(0, src_for(x, y), slot(x, y, c), sibling)]
            cps += [copy(1 + j, src_for(*chip), slot(x, y, c), (*chip, c)) for j, chip in enumerate(chips)]
            for cp in cps:
                cp.start()
            first += cps
        for a in range(n):
            def slot(px, py, pc, a=a):
                return outs[a].at[4 * px + 2 * py + pc]

            def copy(k, src, dst, to, a=a):
                return pltpu.make_async_remote_copy(src_ref=src, dst_ref=dst, send_sem=send_sems.at[7 * a + k],
                                                    recv_sem=recv_sems.at[7 * a + k], device_id=to, device_id_type=MESH)

            for j, chip in enumerate(chips):
                copy(1 + j, slot(*chip, c), slot(*chip, c), (*chip, c)).wait_recv()
                fw = copy(4 + j, slot(*chip, c), slot(*chip, c), sibling)
                fw.start()
                passed.append(fw)
        for a in range(n):
            def slot(px, py, pc, a=a):
                return outs[a].at[4 * px + 2 * py + pc]

            def copy(k, src, dst, to, a=a):
                return pltpu.make_async_remote_copy(src_ref=src, dst_ref=dst, send_sem=send_sems.at[7 * a + k],
                                                    recv_sem=recv_sems.at[7 * a + k], device_id=to, device_id_type=MESH)

            copy(0, slot(x, y, 1 - c), slot(x, y, 1 - c), sibling).wait_recv()
            for j, chip in enumerate(chips):
                copy(4 + j, slot(*chip, 1 - c), slot(*chip, 1 - c), sibling).wait_recv()
        for cp in first + passed:
            cp.wait_send()
        for m in mine:
            m.wait()

    anyspec = pl.BlockSpec(memory_space=pl.ANY)
    res = pl.pallas_call(
        body, name=name, in_specs=[anyspec] * n, out_specs=[anyspec] * n,
        out_shape=[jax.ShapeDtypeStruct((8,) + tuple(b), x.dtype) for b, x in zip(blk, xs)],
        scratch_shapes=[pltpu.SemaphoreType.DMA((7 * n,)), pltpu.SemaphoreType.DMA((7 * n,)),
                        pltpu.SemaphoreType.DMA((n,))],
    )(*xs)
    return list(res)


def _pair_swap(xs, name):
    n = len(xs)

    def body(*refs):
        ins, outs = refs[:n], refs[n:2 * n]
        send_sems, recv_sems = refs[2 * n:]
        x, y, c = _coords()
        cps = [pltpu.make_async_remote_copy(src_ref=ins[a].at[1 - c], dst_ref=outs[a], send_sem=send_sems.at[a],
                                            recv_sem=recv_sems.at[a], device_id=(x, y, 1 - c), device_id_type=MESH)
               for a in range(n)]
        for cp in cps:
            cp.start()
        for cp in cps:
            cp.wait()

    anyspec = pl.BlockSpec(memory_space=pl.ANY)
    res = pl.pallas_call(
        body, name=name, in_specs=[anyspec] * n, out_specs=[anyspec] * n,
        out_shape=[jax.ShapeDtypeStruct(x.shape[1:], x.dtype) for x in xs],
        scratch_shapes=[pltpu.SemaphoreType.DMA((n,)), pltpu.SemaphoreType.DMA((n,))],
    )(*xs)
    return list(res)


_HBM = pl.BlockSpec(memory_space=pltpu.HBM)
_SEM = pl.BlockSpec(memory_space=pltpu.SEMAPHORE)
_EFFECT = pltpu.SideEffectType.DATAFLOW_SIDE_EFFECTING


def _n_peers(both):
    return 7 if both else 3


def _peer(x, y, c, j, both):
    bits = j + 1 if both else 2 * (j + 1)
    dx, dy, dc = bits >> 2 & 1, bits >> 1 & 1, bits & 1
    return (1 - x if dx else x, 1 - y if dy else y, 1 - c if dc else c)


def _spread_copies(s_refs, l_refs, send_sems, recv_sems, per_dest, both):
    x, y, c = _coords()
    me = 4 * x + 2 * y + c
    npeer = _n_peers(both)
    cps = []
    for a in range(len(s_refs)):
        for j in range(npeer):
            tx, ty, tc = _peer(x, y, c, j, both)
            src = s_refs[a].at[2 * tx + ty] if per_dest else s_refs[a]
            cps.append(pltpu.make_async_remote_copy(src_ref=src, dst_ref=l_refs[a].at[me],
                                                    send_sem=send_sems.at[npeer * a + j],
                                                    recv_sem=recv_sems.at[npeer * a + j], device_id=(tx, ty, tc),
                                                    device_id_type=MESH))
    return cps


def _sibling_fill(lands, name):
    n = len(lands)

    def body(*refs):
        outs, send_sems, recv_sems = refs[n:2 * n], refs[2 * n], refs[2 * n + 1]
        x, y, c = _coords()
        cps = [pltpu.make_async_remote_copy(src_ref=outs[a].at[2 * k + c], dst_ref=outs[a].at[2 * k + c],
                                            send_sem=send_sems.at[4 * a + k], recv_sem=recv_sems.at[4 * a + k],
                                            device_id=(x, y, 1 - c), device_id_type=MESH)
               for a in range(n) for k in range(4)]
        for cp in cps:
            cp.start()
        for cp in cps:
            cp.wait()

    anyspec = pl.BlockSpec(memory_space=pl.ANY)
    res = pl.pallas_call(
        body, name=name, in_specs=[anyspec] * n, out_specs=[anyspec] * n,
        out_shape=[jax.ShapeDtypeStruct(t.shape, t.dtype) for t in lands], input_output_aliases={i: i for i in range(n)},
        scratch_shapes=[pltpu.SemaphoreType.DMA((4 * n,)), pltpu.SemaphoreType.DMA((4 * n,))],
    )(*lands)
    return list(res)


def _spread_start(srcs, per_dest, both, dev, chip, name):
    n = len(srcs)
    npeer = _n_peers(both)
    lands = []
    for s in srcs:
        own = lax.dynamic_index_in_dim(s, chip, 0, keepdims=False) if per_dest else s
        lands.append(lax.dynamic_update_index_in_dim(lax.empty((8,) + own.shape, own.dtype), own, dev, 0))

    def body(*refs):
        s_refs, l_refs, send_sems, recv_sems, token = refs[:n], refs[n:2 * n], refs[2 * n], refs[2 * n + 1], refs[-1]
        for cp in _spread_copies(s_refs, l_refs, send_sems, recv_sems, per_dest, both):
            cp.start()
        token[...] = jnp.zeros_like(token)

    hbm_in = [pltpu.with_memory_space_constraint(t, pltpu.HBM) for t in list(srcs) + lands]
    outs = pl.pallas_call(
        body, name=name,
        out_shape=(pltpu.SemaphoreType.DMA((npeer * n,)), pltpu.SemaphoreType.DMA((npeer * n,)),
                   *[pltpu.HBM(t.shape, t.dtype) for t in hbm_in], jax.ShapeDtypeStruct((8, LANES), F32)),
        in_specs=[_HBM] * (2 * n), out_specs=(_SEM, _SEM, *[_HBM] * (2 * n), pl.BlockSpec(memory_space=pltpu.VMEM)),
        input_output_aliases={i: 2 + i for i in range(2 * n)},
        compiler_params=pltpu.CompilerParams(has_side_effects=_EFFECT),
    )(*hbm_in)
    return (outs[0], outs[1], list(outs[2:2 + n]), list(outs[2 + n:2 + 2 * n])), outs[-1]


def _spread_wait(handle, per_dest, both, after, name):
    send_sems, recv_sems, srcs, lands = handle
    n = len(srcs)

    def body(*refs):
        s_refs, l_refs, send_ref, recv_ref = refs[:n], refs[n:2 * n], refs[2 * n], refs[2 * n + 1]
        for cp in _spread_copies(s_refs, l_refs, send_ref, recv_ref, per_dest, both):
            cp.wait_send()
            cp.wait_recv()

    outs = pl.pallas_call(
        body, name=name, out_shape=tuple(pltpu.HBM(t.shape, t.dtype) for t in srcs + lands),
        in_specs=[_HBM] * (2 * n) + [_SEM, _SEM, pl.BlockSpec(memory_space=pl.ANY)], out_specs=tuple([_HBM] * (2 * n)),
        input_output_aliases={i: i for i in range(2 * n)},
        compiler_params=pltpu.CompilerParams(has_side_effects=_EFFECT),
    )(*srcs, *lands, send_sems, recv_sems, after)
    return list(outs[n:])


def _row_tile(n, cap, mult):
    best = n
    for t in range(mult, min(n, cap) + 1, mult):
        if n % t == 0:
            best = t
    return best


def _pair_add(g2, theirs, half, name):
    _, n, cdim = g2.shape
    tm = _row_tile(n, 512, 16)

    def body(h_ref, a_ref, b_ref, o_ref):
        o_ref[...] = (a_ref[...] + b_ref[...]).astype(o_ref.dtype)

    grid_spec = pltpu.PrefetchScalarGridSpec(
        num_scalar_prefetch=1, grid=(n // tm,),
        in_specs=[pl.BlockSpec((None, tm, cdim), lambda i, h: (h[0], i, 0)), pl.BlockSpec((tm, cdim), lambda i, h: (i, 0))],
        out_specs=pl.BlockSpec((tm, cdim), lambda i, h: (i, 0)))
    return pl.pallas_call(body, name=name, grid_spec=grid_spec, out_shape=jax.ShapeDtypeStruct((n, cdim), BF16),
                          compiler_params=_cparams(("parallel",)))(half.reshape(1).astype(jnp.int32), g2, theirs)


def _adamw_math(w, g, m, v):
    m = ADAM_B1 * m + (1.0 - ADAM_B1) * g
    v = ADAM_B2 * v + (1.0 - ADAM_B2) * (g * g)
    m_hat = m / (1.0 - ADAM_B1 ** ADAM_STEP)
    v_hat = v / (1.0 - ADAM_B2 ** ADAM_STEP)
    delta = -ADAM_LR * (m_hat / (jnp.sqrt(v_hat) + ADAM_EPS) + ADAM_WD * w)
    return delta, m, v


def _adamw(parts, w, m, v, name, tm=128):
    npart, R, C = parts.shape
    tm = min(tm, R)

    def body(p_ref, w_ref, m_ref, v_ref, g_out, d_out, m_out, v_out):
        g = p_ref[0].astype(F32)
        for i in range(1, npart):
            g = g + p_ref[i].astype(F32)
        d, mm, vv = _adamw_math(w_ref[...], g, m_ref[...], v_ref[...])
        g_out[...] = g
        d_out[...] = d
        m_out[...] = mm
        v_out[...] = vv

    spec = pl.BlockSpec((tm, C), lambda i: (i, 0))
    return pl.pallas_call(
        body, name=name, grid=(R // tm,),
        in_specs=[pl.BlockSpec((npart, tm, C), lambda i: (0, i, 0)), spec, spec, spec], out_specs=[spec] * 4,
        out_shape=[jax.ShapeDtypeStruct((R, C), F32)] * 4,
        compiler_params=_cparams(("parallel",)),
    )(parts, w, m, v)


def _sum_parts(parts, name):
    npart, R, C = parts.shape

    def body(p_ref, o_ref):
        g = p_ref[0]
        for i in range(1, npart):
            g = g + p_ref[i]
        o_ref[...] = g

    return pl.pallas_call(body, name=name, out_shape=jax.ShapeDtypeStruct((R, C), F32))(parts)


def _mod_fwd(c_all, w_ada, b_sh, name):
    def body(c_ref, w_ref, b_ref, o_ref):
        o_ref[...] = _dot(_silu(c_ref[...]), w_ref[...], "nn") + b_ref[...]

    return pl.pallas_call(body, name=name, out_shape=jax.ShapeDtypeStruct((c_all.shape[0], w_ada.shape[1]), F32),
                          compiler_params=pltpu.CompilerParams(vmem_limit_bytes=VMEM_LIMIT))(c_all, w_ada, b_sh)


def _mod_wgrad(c_all, dmod_sh, name):
    def body(c_ref, d_ref, o_ref):
        o_ref[...] = _dot(_silu(c_ref[...]), d_ref[...], "tn")

    return pl.pallas_call(body, name=name, out_shape=jax.ShapeDtypeStruct((c_all.shape[1], dmod_sh.shape[1]), F32),
                          compiler_params=pltpu.CompilerParams(vmem_limit_bytes=VMEM_LIMIT))(c_all, dmod_sh)


def _pad_lanes(v):
    return jnp.pad(v, ((0, 0), (0, (-v.shape[1]) % LANES)))


def kernel(x, c, norm1_w, norm2_w, w_ada, b_ada, w_in, conv_w, conv_b, dt_bias, a_log, d_skip, ssd_norm_w, q_norm_w, k_norm_w, attn_norm_w, w_out, w_ff1, w_ff2, loss_target, m_norm1_w, m_norm2_w, m_w_ada, m_b_ada, m_w_in, m_conv_w, m_conv_b, m_dt_bias, m_a_log, m_d_skip, m_ssd_norm_w, m_q_norm_w, m_k_norm_w, m_attn_norm_w, m_w_out, m_w_ff1, m_w_ff2, v_norm1_w, v_norm2_w, v_w_ada, v_b_ada, v_w_in, v_conv_w, v_conv_b, v_dt_bias, v_a_log, v_d_skip, v_ssd_norm_w, v_q_norm_w, v_k_norm_w, v_attn_norm_w, v_w_out, v_w_ff1, v_w_ff2):
    xi, yi, ci = _coords()
    chip = 2 * xi + yi
    dev = 2 * chip + ci
    xs, tgt = x[0], loss_target[0]
    S, D = xs.shape
    DI, AD = NH_SSD * HD, NH_ATT * HD
    CC = DI + 2 * NG * NSTATE
    PW = DI + CC + 3 * AD + LANES
    DFF = w_ff1.shape[2] * 4
    MIX = DI + AD
    o_xbc, o_q, o_k, o_v, o_dt = DI, DI + CC, DI + CC + AD, DI + CC + 2 * AD, DI + CC + 3 * AD

    def half_rows(w):
        r = w.shape[0] // 2
        return lax.dynamic_slice_in_dim(w, ci * r, r, 0).astype(BF16)

    c_all, conv_w_all = _exchange8([c, conv_w[0]], False, "gather_c_conv_w")
    c_all = c_all.reshape(8, D)
    c_all = jnp.pad(c_all, ((0, 8), (0, 0)))
    nmod = w_ada.shape[2]
    b_sh = lax.dynamic_slice_in_dim(b_ada, chip * nmod, nmod, 1)
    mod_sh = _mod_fwd(c_all, w_ada[0], b_sh, "mod_fwd")
    mod_all = _exchange8([mod_sh[:8]], False, "gather_mod")[0]
    mod_me = lax.dynamic_index_in_dim(mod_all[0::2], dev, 1, keepdims=False).reshape(1, 4 * nmod)
    shift1, scale1, gate1, shift2, scale2, gate2 = [mod_me[:, i * D:(i + 1) * D] for i in range(6)]

    g_in = _exchange8([half_rows(w_in[0])], False, "gather_w_in")[0]
    rest_handle, rest_token = _spread_start([half_rows(w_out[0]), half_rows(w_ff1[0]), half_rows(w_ff2[0])], False, True,
                                            dev, chip, "gather_rest_start")
    shift1 = shift1 + rest_token[0, 0]
    wsh = w_in.shape[2]
    w_in_f = g_in.reshape(4, D, wsh).transpose(1, 0, 2).reshape(D, 4 * wsh)
    n_zx = DI + CC
    w_proj = jnp.concatenate([w_in_f[:, :n_zx], w_in_f[:, n_zx + NH_SSD:], w_in_f[:, n_zx:n_zx + NH_SSD],
                              jnp.zeros((D, LANES - NH_SSD), BF16)], axis=1)

    dtb, alog, dsk = _pad_lanes(dt_bias), _pad_lanes(a_log), _pad_lanes(d_skip)
    qw2 = jnp.concatenate([q_norm_w, q_norm_w], axis=1)
    kw2 = jnp.concatenate([k_norm_w, k_norm_w], axis=1)
    conv_w_f = conv_w_all[0::2].transpose(1, 0, 2).reshape(KCONV, CC)

    h1 = _rows("norm1", lambda r, k: ([_normmod(r[0], *k)], []), [(xs, 0, D)], [norm1_w, scale1, shift1],
               [(D, BF16)], [], S)[0]
    proj = _matmul(h1, w_proj, "nn", F32, "in_proj", tn=896)
    xbc = _conv_fwd(proj, o_xbc, CC, conv_w_f, conv_b, "conv_fwd")
    y_ssd, hsave = _ssd_fwd(xbc, proj, o_dt // LANES, dtb, alog, dsk, ssd_norm_w, "ssd_fwd")

    def qk_call(name, col0, w2, scale):
        def body(t_ref, w_ref, o_ref):
            o_ref[...] = _headnorm(t_ref[...], w_ref[...], scale)
        return pl.pallas_call(
            body, name=name, grid=(AD // LANES,),
            in_specs=[pl.BlockSpec((S, LANES), lambda j: (0, j + col0 // LANES)),
                      pl.BlockSpec((1, LANES), lambda j: (0, 0))],
            out_specs=pl.BlockSpec((S, LANES), lambda j: (0, j)),
            out_shape=jax.ShapeDtypeStruct((S, AD), F32), compiler_params=_cparams(("parallel",)),
        )(proj, w2)

    qn = qk_call("q_norm", o_q, qw2, HD ** -0.5)
    kn = qk_call("k_norm", o_k, kw2, 1.0)
    o_att, lse = _attn_fwd(qn, kn, proj, o_v // LANES, "attn_fwd")
    y_att = _rows("attn_out_norm", lambda r, k: ([_rmsw(r[0], k[0])], []), [(o_att, 0, AD)], [attn_norm_w],
                  [(AD, BF16)], [], S)[0]
    g_out, g_ff1, g_ff2 = _spread_wait(rest_handle, False, True, o_att, "gather_rest_wait")
    w_out_f = g_out.reshape(MIX, D)
    w_out_a, w_out_b = w_out_f[:DI], w_out_f[DI:]
    w_ff1_f = g_ff1.reshape(4, D, DFF // 4).transpose(1, 0, 2).reshape(D, DFF)
    w_ff2_f = g_ff2.reshape(DFF, D)
    mix_a = _matmul(y_ssd, w_out_a, "nn", F32, "out_proj_a")
    mix = _matmul(y_att, w_out_b, "nn", F32, "out_proj_b", epilogue=lambda r, e: r + e, extras=(mix_a,))
    x2, h2 = _rows("resid_norm2", lambda r, k: (list(_resid_normmod(r[0], r[1], *k)), []), [(xs, 0, D), (mix, 0, D)],
                   [gate1, norm2_w, scale2, shift2], [(D, F32), (D, BF16)], [], S)
    u = _matmul(h2, w_ff1_f, "nn", F32, "ff1")
    relu2 = lambda t: jnp.square(jnp.maximum(t, 0.0))
    ff = _matmul(u, w_ff2_f, "nn", F32, "ff2", a_fn=relu2)

    def loss_fn(r, k):
        x2_, ff_, t_ = r
        err = x2_ + k[0] * ff_ - t_
        dy_ = err * (1.0 / D)
        ls = jnp.sum(jnp.sum(0.5 * err * err, axis=1, keepdims=True), axis=0, keepdims=True) * (1.0 / D)
        return [dy_, dy_ * k[0]], [ls, jnp.sum(dy_ * ff_, axis=0, keepdims=True)]

    dy, dff, loss_p, dgate2 = _rows("loss", loss_fn, [(x2, 0, D), (ff, 0, D), (tgt, 0, D)], [gate2],
                                    [(D, F32), (D, BF16)], [(1, 1), (1, D)], S)
    du = _matmul(dff, w_ff2_f, "nt", BF16, "ff2_dx", epilogue=lambda r, e: r * (2.0 * jnp.maximum(e, 0.0)), extras=(u,))
    gw_ff2 = _matmul(u, dff, "tn", BF16, "ff2_dw", a_fn=relu2, tm=DFF // 4, tn=D, chip_of_tile=lambda i, j: i)
    gw_ff1 = _matmul(h2, du, "tn", BF16, "ff1_dw", tm=D, tn=DFF // 4, chip_of_tile=lambda i, j: j)
    ff_handle, ff_token = _spread_start([gw_ff1, gw_ff2], True, True, dev, chip, "scatter_ff_start")
    dh2 = _matmul(du, w_ff1_f, "nt", F32, "ff1_dx")

    def resid_bwd(r, k):
        x_, mix_, dx2a, dh2_ = r
        _, vjp = jax.vjp(_resid_normmod, x_, mix_, *k)
        dx, dmix_, dg, dnw, dsc, dsh = vjp((dx2a, dh2_))
        return [dx, dmix_], [dg, dnw, dsc, dsh]

    dx2, dmix, dgate1, g_norm2, dscale2, dshift2 = _rows(
        "resid_norm2_bwd", resid_bwd, [(xs, 0, D), (mix, 0, D), (dy, 0, D), (dh2, 0, D)],
        [gate1 + ff_token[0, 0], norm2_w, scale2, shift2], [(D, F32), (D, BF16)], [(1, D)] * 4, S)
    gw_out = jnp.concatenate([_matmul(y_ssd, dmix, "tn", BF16, "out_proj_dw_a"),
                              _matmul(y_att, dmix, "tn", BF16, "out_proj_dw_b")], axis=0)
    out_handle, out_token = _spread_start([gw_out.reshape(4, MIX // 4, D)], True, True, dev, chip, "scatter_out_start")
    dy_ssd = _matmul(dmix, w_out_a, "nt", F32, "out_proj_dx_a")
    dy_att = _matmul(dmix, w_out_b, "nt", F32, "out_proj_dx_b")

    def attn_norm_bwd(r, k):
        o_, dyo = r
        _, vjp = jax.vjp(_rmsw, o_, k[0])
        do_, dw_ = vjp(dyo)
        lo = _lane_mask()
        dd_blocks = []
        for b in range(AD // LANES):
            t = (do_ * o_)[:, b * LANES:(b + 1) * LANES]
            s0 = jnp.sum(jnp.where(lo, t, 0.0), axis=1, keepdims=True)
            s1 = jnp.sum(jnp.where(lo, 0.0, t), axis=1, keepdims=True)
            dd_blocks.append(jnp.where(lo, s0, s1))
        return [do_, jnp.concatenate(dd_blocks, axis=1)], [dw_]

    do_att, dd_att, g_attn_norm = _rows("attn_norm_bwd", attn_norm_bwd, [(o_att, 0, AD), (dy_att, 0, AD)],
                                        [attn_norm_w + out_token[0, 0]], [(AD, F32), (AD, F32)], [(1, AD)], S)
    dq_n, dk_n, dv = _attn_bwd(qn, kn, proj, o_v // LANES, do_att, lse, dd_att, "attn_bwd")

    def qk_bwd_call(name, col0, w2, scale, g):
        def body(t_ref, w_ref, g_ref, o_ref, dw_ref):
            @pl.when(pl.program_id(0) == 0)
            def _():
                dw_ref[...] = jnp.zeros_like(dw_ref)
            _, vjp = jax.vjp(lambda t, w: _headnorm(t, w, scale), t_ref[...], w_ref[...])
            dt_, dw_ = vjp(g_ref[...])
            o_ref[...] = dt_.astype(BF16)
            dw_ref[...] += dw_
        blk = pl.BlockSpec((S, LANES), lambda j: (0, j))
        return pl.pallas_call(
            body, name=name, grid=(AD // LANES,),
            in_specs=[pl.BlockSpec((S, LANES), lambda j: (0, j + col0 // LANES)),
                      pl.BlockSpec((1, LANES), lambda j: (0, 0)), blk],
            out_specs=[blk, pl.BlockSpec((1, LANES), lambda j: (0, 0))],
            out_shape=[jax.ShapeDtypeStruct((S, AD), BF16), jax.ShapeDtypeStruct((1, LANES), F32)],
            compiler_params=_cparams(("arbitrary",)),
        )(proj, w2, g)

    dq, g_qw2 = qk_bwd_call("q_norm_bwd", o_q, qw2, HD ** -0.5, dq_n)
    dk, g_kw2 = qk_bwd_call("k_norm_bwd", o_k, kw2, 1.0, dk_n)
    g_q_norm = g_qw2[:, :HD] + g_qw2[:, HD:]
    g_k_norm = g_kw2[:, :HD] + g_kw2[:, HD:]

    dxbc, dz, ddtr, g_dtb, g_alog, g_dsk, g_ssd_norm = _ssd_bwd(
        xbc, proj, o_dt // LANES, dtb, alog, dsk, ssd_norm_w, hsave, dy_ssd, "ssd_bwd")
    dxbc_pre, g_conv_w, g_conv_b = _conv_bwd(proj, o_xbc, CC, conv_w_f, conv_b, dxbc, "conv_bwd")
    dproj = jnp.concatenate([dz.astype(BF16), dxbc_pre.astype(BF16), dq, dk, dv.astype(BF16), ddtr.astype(BF16)], axis=1)
    gw_proj = _matmul(h1, dproj, "tn", F32, "in_proj_dw", tn=896)
    gw_in = jnp.concatenate([gw_proj[:, :n_zx], gw_proj[:, o_dt:o_dt + NH_SSD], gw_proj[:, n_zx:o_dt]], axis=1)
    gw_in2 = gw_in.reshape(2, D // 2, 4, wsh).transpose(0, 2, 1, 3)
    theirs_in = _pair_swap([gw_in2], "pair_swap_in")[0]
    sum_in = _pair_add(gw_in2.reshape(2, 2 * D, wsh), theirs_in.reshape(2 * D, wsh), ci, "pair_add_in")
    in_handle, in_token = _spread_start([sum_in.reshape(4, D // 2, wsh)], True, False, dev, chip, "scatter_in_start")
    dh1 = _matmul(dproj, w_proj, "nt", F32, "in_proj_dx", tk=896)

    def norm1_bwd(r, k):
        x_, dh_, dres = r
        _, vjp = jax.vjp(_normmod, x_, *k)
        dx, dnw, dsc, dsh = vjp(dh_)
        return [dx + dres], [dnw, dsc, dsh]

    grad_x, g_norm1, dscale1, dshift1 = _rows("norm1_bwd", norm1_bwd, [(xs, 0, D), (dh1, 0, D), (dx2, 0, D)],
                                              [norm1_w + in_token[0, 0], scale1, shift1], [(D, F32)], [(1, D)] * 3, S)
    dmod =jnp.concatenate([dshift1, dscale1, dgate1, dshift2, dscale2, dgate2], axis=1)

    small = [g_norm1, g_norm2, dmod, g_conv_b, g_dtb, g_alog, g_dsk, g_ssd_norm, _pad_lanes(g_q_norm),
             _pad_lanes(g_k_norm), g_attn_norm, g_conv_w.reshape(1, KCONV * CC)]
    sizes = [t.shape[1] for t in small]
    packed = jnp.concatenate(small, axis=1)
    nrow = -(-packed.shape[1] // LANES // 8) * 8
    packed = jnp.pad(packed, ((0, 0), (0, nrow * LANES - packed.shape[1]))).reshape(nrow, LANES)
    packed_all = _exchange8([packed], False, "gather_small_grads")[0]
    tot = _sum_parts(packed_all, "sum_small_grads").reshape(1, nrow * LANES)
    offs = [sum(sizes[:i]) for i in range(len(sizes))]
    (g_norm1, g_norm2, g_b_ada, g_conv_b, g_dtb, g_alog, g_dsk, g_ssd_norm, g_q_norm, g_k_norm, g_attn_norm,
     g_conv_w) = [tot[:, o:o + n] for o, n in zip(offs, sizes)]
    g_dtb, g_alog, g_dsk = g_dtb[:, :NH_SSD], g_alog[:, :NH_SSD], g_dsk[:, :NH_SSD]
    g_q_norm, g_k_norm = g_q_norm[:, :HD], g_k_norm[:, :HD]
    ccs = CC // 4
    g_conv_w = lax.dynamic_slice_in_dim(g_conv_w.reshape(KCONV, CC), chip * ccs, ccs, 1)

    dmod_all = packed_all.reshape(8, nrow * LANES)[:, offs[2]:offs[2] + 6 * D]
    dmod_sh = jnp.pad(lax.dynamic_slice_in_dim(dmod_all, chip * nmod, nmod, 1), ((0, 8), (0, 0)))
    gw_ada = _mod_wgrad(c_all, dmod_sh, "mod_wgrad")

    parts_ff1, parts_ff2 = _spread_wait(ff_handle, True, True, in_token, "scatter_ff_wait")
    res_ff1 = _adamw(parts_ff1, w_ff1[0], m_w_ff1[0], v_w_ff1[0], "adamw_w_ff1")
    res_ff2 = _adamw(parts_ff2, w_ff2[0], m_w_ff2[0], v_w_ff2[0], "adamw_w_ff2")
    parts_out = _spread_wait(out_handle, True, True, in_token, "scatter_out_wait")[0]
    res_out = _adamw(parts_out, w_out[0], m_w_out[0], v_w_out[0], "adamw_w_out")
    res_ada = _adamw(gw_ada[None], w_ada[0], m_w_ada[0], v_w_ada[0], "adamw_w_ada")
    lands_in = _sibling_fill(_spread_wait(in_handle, True, False, res_ada[0], "scatter_in_wait"), "scatter_in_fill")[0]
    res_in = _adamw(lands_in.reshape(4, D, wsh), w_in[0], m_w_in[0], v_w_in[0], "adamw_w_in")

    small_names = ["norm1_w", "norm2_w", "b_ada", "conv_w", "conv_b", "dt_bias", "a_log", "d_skip", "ssd_norm_w",
                   "q_norm_w", "k_norm_w", "attn_norm_w"]
    small_g = dict(norm1_w=g_norm1, norm2_w=g_norm2, b_ada=g_b_ada, conv_w=g_conv_w.reshape(1, KCONV * ccs),
                   conv_b=g_conv_b, dt_bias=g_dtb, a_log=g_alog, d_skip=g_dsk, ssd_norm_w=g_ssd_norm, q_norm_w=g_q_norm,
                   k_norm_w=g_k_norm, attn_norm_w=g_attn_norm)
    small_w = dict(norm1_w=(norm1_w, m_norm1_w, v_norm1_w), norm2_w=(norm2_w, m_norm2_w, v_norm2_w),
                   b_ada=(b_ada, m_b_ada, v_b_ada),
                   conv_w=tuple(t.reshape(1, KCONV * ccs) for t in (conv_w, m_conv_w, v_conv_w)),
                   conv_b=(conv_b, m_conv_b, v_conv_b), dt_bias=(dt_bias, m_dt_bias, v_dt_bias),
                   a_log=(a_log, m_a_log, v_a_log), d_skip=(d_skip, m_d_skip, v_d_skip),
                   ssd_norm_w=(ssd_norm_w, m_ssd_norm_w, v_ssd_norm_w), q_norm_w=(q_norm_w, m_q_norm_w, v_q_norm_w),
                   k_norm_w=(k_norm_w, m_k_norm_w, v_k_norm_w), attn_norm_w=(attn_norm_w, m_attn_norm_w, v_attn_norm_w))
    ssz = [_pad_lanes(small_g[n]).shape[1] for n in small_names]
    soff = [sum(ssz[:i]) for i in range(len(ssz))]
    srow = -(-sum(ssz) // LANES // 8) * 8

    def pack(ts, fill):
        t = jnp.concatenate([jnp.pad(t, ((0, 0), (0, (-t.shape[1]) % LANES)), constant_values=fill) for t in ts], axis=1)
        return jnp.pad(t, ((0, 0), (0, srow * LANES - t.shape[1])), constant_values=fill).reshape(srow, LANES)

    sg = pack([small_g[n] for n in small_names], 0.0)
    sw = pack([small_w[n][0] for n in small_names], 0.0)
    sm_ = pack([small_w[n][1] for n in small_names], 0.0)
    sv = pack([small_w[n][2] for n in small_names], 1.0)
    _, s_delta, s_m, s_v = _adamw(sg[None], sw, sm_, sv, "adamw_small", tm=srow)

    def unpack(t, n):
        i = small_names.index(n)
        return t.reshape(1, srow * LANES)[:, soff[i]:soff[i] + small_g[n].shape[1]].reshape(small_w[n][0].shape)

    loss = lax.psum(loss_p[0, 0], ("x", "y", "c"))
    big_res = dict(w_ada=res_ada, w_in=res_in, w_out=res_out, w_ff1=res_ff1, w_ff2=res_ff2)
    order = ["norm1_w", "norm2_w", "w_ada", "b_ada", "w_in", "conv_w", "conv_b", "dt_bias", "a_log", "d_skip",
             "ssd_norm_w", "q_norm_w", "k_norm_w", "attn_norm_w", "w_out", "w_ff1", "w_ff2"]
    grads, deltas, new_m, new_v = [], [], [], []
    for n in order:
        if n in big_res:
            g_, d_, m_, v_ = [t[None] for t in big_res[n]]
        else:
            g_ = small_g[n].reshape(small_w[n][0].shape)
            d_, m_, v_ = unpack(s_delta, n), unpack(s_m, n), unpack(s_v, n)
            if n == "conv_w":
                g_, d_, m_, v_ = [t.reshape(conv_w.shape) for t in (g_, d_, m_, v_)]
        grads.append(g_)
        deltas.append(d_)
        new_m.append(m_)
        new_v.append(v_)
    return (loss, grad_x[None], *grads, *deltas, *new_m, *new_v)
```

```python
import functools

import jax
import jax.numpy as jnp
from jax import lax
from jax.experimental import pallas as pl
from jax.experimental.pallas import tpu as pltpu

F32, BF16 = jnp.float32, jnp.bfloat16
EPS = 1e-6
HD = 64
NH_SSD = 16
NG = 4
NSTATE = 128
KCONV = 4
CHUNK = 128
NH_ATT = 16
PATTERNS = ((128, 1), (512, 4), (2048, 16))
ABLK = 128
LANES = 128
ADAM_LR, ADAM_B1, ADAM_B2, ADAM_EPS, ADAM_WD, ADAM_STEP = 0.001, 0.9, 0.999, 1e-08, 0.01, 10
VMEM_LIMIT = 56 * 1024 * 1024
MESH = pl.DeviceIdType.MESH
NEG = -1e30

_DN = {"nn": (((1,), (0,)), ((), ())), "nt": (((1,), (1,)), ((), ())), "tn": (((0,), (0,)), ((), ()))}


def _cparams(sem):
    return pltpu.CompilerParams(dimension_semantics=sem, vmem_limit_bytes=VMEM_LIMIT)


def _tile(n, cap):
    if n % LANES or n <= LANES:
        return n
    best = LANES
    for t in range(LANES, min(n, cap) + 1, LANES):
        if n % t == 0:
            best = t
    return best


def _silu(x):
    return x / (1.0 + jnp.exp(-x))


def _softplus(x):
    return jnp.maximum(x, 0.0) + jnp.log(1.0 + jnp.exp(-jnp.abs(x)))


def _dot(a, b, dims):
    return lax.dot_general(a.astype(BF16), b.astype(BF16), _DN[dims], preferred_element_type=F32)


def _matmul(a, b, dims, out_dtype, name, a_fn=None, epilogue=None, extras=(), tm=1024, tn=1024, tk=1024,
            chip_of_tile=None):
    if dims == "nn":
        (M, K), (_, N) = a.shape, b.shape
    elif dims == "nt":
        (M, K), (N, _) = a.shape, b.shape
    else:
        (K, M), (_, N) = a.shape, b.shape
    tm, tn, tk = _tile(M, tm), _tile(N, tn), _tile(K, tk)
    nk = K // tk
    ne = len(extras)

    def body(a_ref, b_ref, *rest):
        e_refs, o_ref = rest[:ne], rest[ne]
        av = a_ref[...]
        if a_fn is not None:
            av = a_fn(av)
        part = _dot(av, b_ref[...], dims)

        def finish(r):
            if epilogue is not None:
                r = epilogue(r, *[e[...] for e in e_refs])
            o_ref[...] = r.astype(out_dtype).reshape(o_ref.shape)

        if nk == 1:
            finish(part)
            return
        acc = rest[ne + 1]
        k = pl.program_id(2)

        @pl.when(k == 0)
        def _():
            acc[...] = part

        @pl.when(k > 0)
        def _():
            acc[...] += part

        @pl.when(k == nk - 1)
        def _():
            finish(acc[...])

    a_spec = pl.BlockSpec((tk, tm), lambda i, j, k: (k, i)) if dims == "tn" else pl.BlockSpec((tm, tk), lambda i, j, k: (i, k))
    b_spec = pl.BlockSpec((tn, tk), lambda i, j, k: (j, k)) if dims == "nt" else pl.BlockSpec((tk, tn), lambda i, j, k: (k, j))
    o_spec = pl.BlockSpec((tm, tn), lambda i, j, k: (i, j))
    out_spec, out_dims = o_spec, (M, N)
    if chip_of_tile is not None:
        assert (M // tm) * (N // tn) == 4
        out_spec = pl.BlockSpec((None, tm, tn), lambda i, j, k: (chip_of_tile(i, j), 0, 0))
        out_dims = (4, tm, tn)
    return pl.pallas_call(
        body, name=name, grid=(M // tm, N // tn, nk),
        in_specs=[a_spec, b_spec] + [o_spec] * ne, out_specs=out_spec,
        out_shape=jax.ShapeDtypeStruct(out_dims, out_dtype),
        scratch_shapes=[pltpu.VMEM((tm, tn), F32)] if nk > 1 else [],
        compiler_params=_cparams(("parallel", "parallel", "arbitrary")),
    )(a, b, *extras)


def _rows(name, fn, rows, consts, outs, accs, n_rows, tm=256):
    tm = min(tm, n_rows)
    nr, nc, no, na = len(rows), len(consts), len(outs), len(accs)

    def body(*refs):
        r_refs, c_refs = refs[:nr], refs[nr:nr + nc]
        o_refs, a_refs = refs[nr + nc:nr + nc + no], refs[nr + nc + no:]
        o_vals, a_vals = fn([r[...] for r in r_refs], [c[...] for c in c_refs])
        for ref, val in zip(o_refs, o_vals):
            ref[...] = val.astype(ref.dtype)
        if na:
            @pl.when(pl.program_id(0) == 0)
            def _():
                for ref in a_refs:
                    ref[...] = jnp.zeros_like(ref)
            for ref, val in zip(a_refs, a_vals):
                ref[...] += val

    in_specs = [pl.BlockSpec((tm, w), lambda i, cb=cb: (i, cb)) for (_, cb, w) in rows]
    in_specs += [pl.BlockSpec(cst.shape, lambda i, nd=cst.ndim: (0,) * nd) for cst in consts]
    out_specs = [pl.BlockSpec((tm, w), lambda i: (i, 0)) for (w, _) in outs]
    out_specs += [pl.BlockSpec(s, lambda i: (0, 0)) for s in accs]
    out_shape = [jax.ShapeDtypeStruct((n_rows, w), dt) for (w, dt) in outs]
    out_shape += [jax.ShapeDtypeStruct(s, F32) for s in accs]
    res = pl.pallas_call(
        body, name=name, grid=(n_rows // tm,), in_specs=in_specs, out_specs=out_specs, out_shape=out_shape,
        compiler_params=_cparams(("arbitrary",)),
    )(*[r[0] for r in rows], *consts)
    return res


def _normmod(x, nw, sc, sh):
    r = lax.rsqrt(jnp.mean(x * x, axis=-1, keepdims=True) + EPS)
    return (x * r) * nw * (1.0 + sc) + sh


def _resid_normmod(x, mix, g, nw, sc, sh):
    x2 = x + g * mix
    return x2, _normmod(x2, nw, sc, sh)


def _rmsw(o, w):
    return o * lax.rsqrt(jnp.mean(o * o, axis=-1, keepdims=True) + EPS) * w


def _lane_mask():
    return lax.broadcasted_iota(jnp.int32, (1, LANES), 1) < HD


def _headnorm(t, w, scale):
    lo = _lane_mask()
    t2 = t * t
    s0 = jnp.sum(jnp.where(lo, t2, 0.0), axis=1, keepdims=True)
    s1 = jnp.sum(jnp.where(lo, 0.0, t2), axis=1, keepdims=True)
    ms = jnp.where(lo, s0, s1) * (1.0 / HD)
    return t * lax.rsqrt(ms + EPS) * w * scale


CONV_ROWS = 128
CONV_HALO = 8


def _conv_cols(n_ch):
    return _tile(n_ch, LANES)


def _conv_fwd(proj, col0, n_ch, conv_w, conv_b, name):
    S = proj.shape[0]
    tc = _conv_cols(n_ch)

    R, H = CONV_ROWS, CONV_HALO

    def body(u_ref, w_ref, b_ref, o_ref):
        w = [w_ref[i:i + 1, :] for i in range(KCONV)]
        b = b_ref[...]

        def chunk(ext):
            acc = b + w[KCONV - 1] * ext[H:]
            for i in range(KCONV - 1):
                acc = acc + w[i] * pltpu.roll(ext, KCONV - 1 - i, 0)[H:]
            return _silu(acc)

        o_ref[0:R, :] = chunk(jnp.concatenate([jnp.zeros((H, tc), F32), u_ref[0:R, :]], axis=0))

        def step(c, carry):
            r0 = pl.multiple_of(c * R, R)
            o_ref[pl.ds(r0, R), :] = chunk(u_ref[pl.ds(pl.multiple_of(r0 - H, H), R + H), :])
            return carry

        lax.fori_loop(1, S // R, step, 0)

    return pl.pallas_call(
        body, name=name, grid=(n_ch // tc,),
        in_specs=[pl.BlockSpec((S, tc), lambda j: (0, j + col0 // tc)),
                  pl.BlockSpec((KCONV, tc), lambda j: (0, j)), pl.BlockSpec((1, tc), lambda j: (0, j))],
        out_specs=pl.BlockSpec((S, tc), lambda j: (0, j)),
        out_shape=jax.ShapeDtypeStruct((S, n_ch), F32),
        compiler_params=_cparams(("parallel",)),
    )(proj, conv_w, conv_b)


def _conv_bwd(proj, col0, n_ch, conv_w, conv_b, dxbc, name):
    S = proj.shape[0]
    tc = _conv_cols(n_ch)

    R, H = CONV_ROWS, CONV_HALO

    def body(u_ref, w_ref, b_ref, g_ref, du_ref, dw_ref, db_ref):
        w = [w_ref[i:i + 1, :] for i in range(KCONV)]
        b = b_ref[...]
        pad = jnp.zeros((H, tc), F32)

        def chunk(u_ext, g_ext):
            taps = [pltpu.roll(u_ext, KCONV - 1 - i, 0)[H:] for i in range(KCONV - 1)] + [u_ext[H:]]
            acc = b
            for i in range(KCONV):
                acc = acc + w[i] * taps[i]
            sig = 1.0 / (1.0 + jnp.exp(-acc))
            dacc = g_ext * (sig * (1.0 + acc * (1.0 - sig)))
            du = w[KCONV - 1] * dacc[:R]
            for i in range(KCONV - 1):
                du = du + w[i] * pltpu.roll(dacc, R + H - (KCONV - 1 - i), 0)[:R]
            d = dacc[:R]
            return du, [jnp.sum(d * t[:R], axis=0, keepdims=True) for t in taps], jnp.sum(d, axis=0, keepdims=True)

        du, dws, db = chunk(jnp.concatenate([pad, u_ref[0:R + H, :]], axis=0), g_ref[0:R + H, :])
        du_ref[0:R, :] = du

        def step(c, carry):
            r0 = pl.multiple_of(c * R, R)
            du_c, dws_c, db_c = chunk(u_ref[pl.ds(pl.multiple_of(r0 - H, H), R + 2 * H), :], g_ref[pl.ds(r0, R + H), :])
            du_ref[pl.ds(r0, R), :] = du_c
            return [a + b_ for a, b_ in zip(carry[0], dws_c)], carry[1] + db_c

        dws, db = lax.fori_loop(1, S // R - 1, step, (dws, db))
        du, dws_l, db_l = chunk(jnp.concatenate([u_ref[S - R - H:S, :], pad], axis=0),
                                jnp.concatenate([g_ref[S - R:S, :], pad], axis=0))
        du_ref[S - R:S, :] = du
        for i in range(KCONV):
            dw_ref[i:i + 1, :] = dws[i] + dws_l[i]
        db_ref[...] = db + db_l

    return pl.pallas_call(
        body, name=name, grid=(n_ch // tc,),
        in_specs=[pl.BlockSpec((S, tc), lambda j: (0, j + col0 // tc)),
                  pl.BlockSpec((KCONV, tc), lambda j: (0, j)), pl.BlockSpec((1, tc), lambda j: (0, j)),
                  pl.BlockSpec((S, tc), lambda j: (0, j))],
        out_specs=[pl.BlockSpec((S, tc), lambda j: (0, j)), pl.BlockSpec((KCONV, tc), lambda j: (0, j)),
                   pl.BlockSpec((1, tc), lambda j: (0, j))],
        out_shape=[jax.ShapeDtypeStruct((S, n_ch), F32), jax.ShapeDtypeStruct((KCONV, n_ch), F32),
                   jax.ShapeDtypeStruct((1, n_ch), F32)],
        compiler_params=_cparams(("parallel",)),
    )(proj, conv_w, conv_b, dxbc)


@functools.partial(jax.custom_vjp, nondiff_argnums=(2,))
def _mm(a, b, dims):
    return _dot(a, b, dims)


def _mm_fwd(a, b, dims):
    return _dot(a, b, dims), (a, b)


def _mm_bwd(dims, res, g):
    a, b = res
    if dims == "nn":
        return _dot(g, b, "nt"), _dot(a, g, "tn")
    if dims == "nt":
        return _dot(g, b, "nn"), _dot(g, a, "tn")
    return _dot(b, g, "nt"), _dot(a, g, "nn")


_mm.defvjp(_mm_fwd, _mm_bwd)


def _tri_dot(x, upper):
    n = x.shape[0]
    r = lax.broadcasted_iota(jnp.int32, (n, n), 0)
    c = lax.broadcasted_iota(jnp.int32, (n, n), 1)
    t = jnp.where((r <= c) if upper else (r >= c), 1.0, 0.0)
    return lax.dot_general(t, x, _DN["nn"], precision=lax.Precision.HIGHEST, preferred_element_type=F32)


@jax.custom_vjp
def _cumsum_rows(x):
    return _tri_dot(x, False)


_cumsum_rows.defvjp(lambda x: (_tri_dot(x, False), None), lambda _, g: (_tri_dot(g, True),))


def _ssd_chunk(xs_p, bm_g, cm_g, dtr, z_p, dtb, alog, dsk, nw_p, h_p):
    L = dtr.shape[0]
    n_pairs = len(xs_p)
    ppg = n_pairs // len(bm_g)
    lane = lax.broadcasted_iota(jnp.int32, (1, LANES), 1)
    sub = lax.broadcasted_iota(jnp.int32, (LANES, 1), 0)
    lo = lane < HD
    row_l = lax.broadcasted_iota(jnp.int32, (L, 1), 0)
    tri = lax.broadcasted_iota(jnp.int32, (L, L), 0) >= lax.broadcasted_iota(jnp.int32, (L, L), 1)

    dt = _softplus(dtr + dtb)
    acs = _cumsum_rows(dt * (-jnp.exp(alog)))
    acs_t = acs.T
    a_last = jnp.sum(jnp.where(row_l == L - 1, acs, 0.0), axis=0, keepdims=True)
    e_acs = jnp.exp(acs)
    dec = jnp.exp(a_last - acs)
    cdec = jnp.exp(a_last)

    def colv(m, h):
        return jnp.sum(jnp.where(lane == h, m, 0.0), axis=1, keepdims=True)

    def rowv(mt, h):
        return jnp.sum(jnp.where(sub == h, mt, 0.0), axis=0, keepdims=True)

    def pair(m, h0):
        return jnp.where(lo, colv(m, h0), colv(m, h0 + 1))

    ys, hs = [], []
    cb = None
    for p in range(n_pairs):
        g, h0 = p // ppg, 2 * p
        bmat, cmat = bm_g[g], cm_g[g]
        if p % ppg == 0:
            cb = _mm(cmat, bmat, "nt")
        x = xs_p[p]
        xdt = x * pair(dt, h0)
        yd = []
        for h in (h0, h0 + 1):
            seg = colv(acs, h) - rowv(acs_t, h)
            lm = jnp.where(tri, jnp.exp(jnp.where(tri, seg, 0.0)), 0.0)
            yd.append(_mm(cb * lm, xdt, "nn"))
        y = jnp.where(lo, yd[0], yd[1])
        y = y + _mm(cmat, h_p[p], "nt") * pair(e_acs, h0)
        st = _mm(xdt * pair(dec, h0), bmat, "tn")
        cd_col = jnp.where(sub < HD, colv(cdec, h0), colv(cdec, h0 + 1))
        hs.append(h_p[p] * cd_col + st)
        ys.append(y + pair(dsk, h0) * x)

    y2 = [ys[p] * _silu(z_p[p]) for p in range(n_pairs)]
    outs = []
    for g in range(len(bm_g)):
        ps = range(g * ppg, (g + 1) * ppg)
        ss = sum(jnp.sum(y2[p] * y2[p], axis=1, keepdims=True) for p in ps)
        rs = lax.rsqrt(ss * (1.0 / (ppg * LANES)) + EPS)
        outs += [y2[p] * rs * nw_p[p] for p in ps]
    return outs, hs


def _ssd_slices(xbc_ref, z_ref, nw_ref, di):
    n_pairs = di // LANES
    xs_p = [xbc_ref[:, p * LANES:(p + 1) * LANES] for p in range(n_pairs)]
    bm_g = [xbc_ref[:, di + g * NSTATE:di + (g + 1) * NSTATE] for g in range(NG)]
    cm_g = [xbc_ref[:, di + (NG + g) * NSTATE:di + (NG + g + 1) * NSTATE] for g in range(NG)]
    z_p = [z_ref[:, p * LANES:(p + 1) * LANES] for p in range(n_pairs)]
    nw_p = [nw_ref[:, p * LANES:(p + 1) * LANES] for p in range(n_pairs)]
    return xs_p, bm_g, cm_g, z_p, nw_p


def _ssd_fwd(xbc, proj, dt_cb, dtb, alog, dsk, nw, name):
    S, cc = xbc.shape
    di = NH_SSD * HD
    n_pairs = di // LANES
    nchunk = S // CHUNK

    def body(xbc_ref, z_ref, dtr_ref, dtb_ref, alog_ref, dsk_ref, nw_ref, y_ref, hs_ref, h_scr):
        @pl.when(pl.program_id(0) == 0)
        def _():
            h_scr[...] = jnp.zeros_like(h_scr)

        xs_p, bm_g, cm_g, z_p, nw_p = _ssd_slices(xbc_ref, z_ref, nw_ref, di)
        h_p = [h_scr[p * LANES:(p + 1) * LANES, :] for p in range(n_pairs)]
        hs_ref[...] = h_scr[...]
        outs, hs = _ssd_chunk(xs_p, bm_g, cm_g, dtr_ref[...], z_p, dtb_ref[...], alog_ref[...], dsk_ref[...], nw_p, h_p)
        for p in range(n_pairs):
            y_ref[:, p * LANES:(p + 1) * LANES] = outs[p].astype(y_ref.dtype)
            h_scr[p * LANES:(p + 1) * LANES, :] = hs[p]

    vec = pl.BlockSpec((1, LANES), lambda c: (0, 0))
    return pl.pallas_call(
        body, name=name, grid=(nchunk,),
        in_specs=[pl.BlockSpec((CHUNK, cc), lambda c: (c, 0)), pl.BlockSpec((CHUNK, di), lambda c: (c, 0)),
                  pl.BlockSpec((CHUNK, LANES), lambda c: (c, dt_cb)), vec, vec, vec,
                  pl.BlockSpec((1, di), lambda c: (0, 0))],
        out_specs=[pl.BlockSpec((CHUNK, di), lambda c: (c, 0)), pl.BlockSpec((None, di, NSTATE), lambda c: (c, 0, 0))],
        out_shape=[jax.ShapeDtypeStruct((S, di), BF16), jax.ShapeDtypeStruct((nchunk, di, NSTATE), F32)],
        scratch_shapes=[pltpu.VMEM((di, NSTATE), F32)],
        compiler_params=_cparams(("arbitrary",)),
    )(xbc, proj, proj, dtb, alog, dsk, nw)


def _ssd_bwd(xbc, proj, dt_cb, dtb, alog, dsk, nw, hsave, dy, name):
    S, cc = xbc.shape
    di = NH_SSD * HD
    n_pairs = di // LANES
    nchunk = S // CHUNK

    def body(xbc_ref, z_ref, dtr_ref, dtb_ref, alog_ref, dsk_ref, nw_ref, hs_ref, dy_ref,
             dxbc_ref, dz_ref, ddtr_ref, ddtb_ref, dalog_ref, ddsk_ref, dnw_ref, dh_scr):
        @pl.when(pl.program_id(0) == 0)
        def _():
            dh_scr[...] = jnp.zeros_like(dh_scr)
            ddtb_ref[...] = jnp.zeros_like(ddtb_ref)
            dalog_ref[...] = jnp.zeros_like(dalog_ref)
            ddsk_ref[...] = jnp.zeros_like(ddsk_ref)
            dnw_ref[...] = jnp.zeros_like(dnw_ref)

        xs_p, bm_g, cm_g, z_p, nw_p = _ssd_slices(xbc_ref, z_ref, nw_ref, di)
        h_p = [hs_ref[p * LANES:(p + 1) * LANES, :] for p in range(n_pairs)]
        dy_p = [dy_ref[:, p * LANES:(p + 1) * LANES].astype(F32) for p in range(n_pairs)]
        dh_p = [dh_scr[p * LANES:(p + 1) * LANES, :] for p in range(n_pairs)]
        _, vjp = jax.vjp(_ssd_chunk, xs_p, bm_g, cm_g, dtr_ref[...], z_p, dtb_ref[...], alog_ref[...], dsk_ref[...],
                         nw_p, h_p)
        dxs, dbm, dcm, ddtr, dz, ddtb, dalog, ddsk, dnw, dh = vjp((dy_p, dh_p))
        for p in range(n_pairs):
            sl = slice(p * LANES, (p + 1) * LANES)
            dxbc_ref[:, sl] = dxs[p]
            dz_ref[:, sl] = dz[p]
            dnw_ref[:, sl] += dnw[p]
            dh_scr[sl, :] = dh[p]
        for g in range(NG):
            dxbc_ref[:, di + g * NSTATE:di + (g + 1) * NSTATE] = dbm[g]
            dxbc_ref[:, di + (NG + g) * NSTATE:di + (NG + g + 1) * NSTATE] = dcm[g]
        ddtr_ref[...] = ddtr
        ddtb_ref[...] += ddtb
        dalog_ref[...] += dalog
        ddsk_ref[...] += ddsk

    last = nchunk - 1
    vec = pl.BlockSpec((1, LANES), lambda c: (0, 0))
    return pl.pallas_call(
        body, name=name, grid=(nchunk,),
        in_specs=[pl.BlockSpec((CHUNK, cc), lambda c: (last - c, 0)), pl.BlockSpec((CHUNK, di), lambda c: (last - c, 0)),
                  pl.BlockSpec((CHUNK, LANES), lambda c: (last - c, dt_cb)), vec, vec, vec,
                  pl.BlockSpec((1, di), lambda c: (0, 0)),
                  pl.BlockSpec((None, di, NSTATE), lambda c: (last - c, 0, 0)),
                  pl.BlockSpec((CHUNK, di), lambda c: (last - c, 0))],
        out_specs=[pl.BlockSpec((CHUNK, cc), lambda c: (last - c, 0)), pl.BlockSpec((CHUNK, di), lambda c: (last - c, 0)),
                   pl.BlockSpec((CHUNK, LANES), lambda c: (last - c, 0)), vec, vec, vec,
                   pl.BlockSpec((1, di), lambda c: (0, 0))],
        out_shape=[jax.ShapeDtypeStruct((S, cc), F32), jax.ShapeDtypeStruct((S, di), F32),
                   jax.ShapeDtypeStruct((S, LANES), F32), jax.ShapeDtypeStruct((1, LANES), F32),
                   jax.ShapeDtypeStruct((1, LANES), F32), jax.ShapeDtypeStruct((1, LANES), F32),
                   jax.ShapeDtypeStruct((1, di), F32)],
        scratch_shapes=[pltpu.VMEM((di, NSTATE), F32)],
        compiler_params=_cparams(("arbitrary",)),
    )(xbc, proj, proj, dtb, alog, dsk, nw, hsave, dy)


def _band_masks(rows_q, rows_k):
    qi = lax.broadcasted_iota(jnp.int32, (rows_q, rows_k), 0)
    ki = lax.broadcasted_iota(jnp.int32, (rows_q, rows_k), 1)
    return qi, ki


def _class_chunks(n_rows, d):
    per_class = n_rows // d
    ch = min(per_class, 256)
    out = []
    for r in range(d):
        for c0 in range(0, per_class, ch):
            tok = pl.ds(c0, ch) if d == 1 else pl.ds(r + d * c0, ch, stride=d)
            out.append((tok, pl.ds(r * per_class + c0, ch)))
    return out


def _to_class_order(src_ref, dst_ref, n_rows, d):
    for tok, cls in _class_chunks(n_rows, d):
        dst_ref[cls, :] = src_ref[tok, :].astype(dst_ref.dtype)


def _blk_rows(t):
    return pl.ds(pl.multiple_of(t * ABLK, ABLK), ABLK)


def _head_lanes(msk, t, t_rolled):
    return jnp.where(msk, t, t_rolled)


def _zero_unless(msk, t):
    return jnp.where(msk, t, jnp.zeros_like(t))


def _attn_fwd(qn, kn, proj, v_cb, name):
    S, ad = qn.shape
    nb = S // ABLK
    nbr = len(PATTERNS)

    def body(q_ref, k_ref, v_ref, o_ref, lse_ref, qc, kc, vc, ob, mb, lb, m_s, l_s):
        lo = _lane_mask()
        qi, ki = _band_masks(ABLK, ABLK)
        cur_ok, prev_ok = ki <= qi, ki >= qi
        for bi, (_, d) in enumerate(PATTERNS):
            nbc = S // d // ABLK
            first, last = bi == 0, bi == nbr - 1
            qs, ks, vs = q_ref, k_ref, v_ref
            if d > 1:
                qs, ks, vs = qc, kc, vc
                for src, dst in ((q_ref, qc), (k_ref, kc), (v_ref, vc)):
                    _to_class_order(src, dst, S, d)
            o_dst, m_dst, l_dst = (o_ref, m_s, l_s) if first else (ob, mb, lb)

            def blk(t, carry, nbc=nbc, qs=qs, ks=ks, vs=vs, o_dst=o_dst, m_dst=m_dst, l_dst=l_dst):
                rows, prow = _blk_rows(t), _blk_rows(jnp.maximum(t - 1, 0))
                has_prev = (t % nbc) != 0
                qv = qs[rows, :]
                q2 = jnp.concatenate([_zero_unless(lo, qv), _zero_unless(jnp.logical_not(lo), qv)], axis=0).astype(BF16)
                ok_c = jnp.concatenate([cur_ok, cur_ok], axis=0)
                ok_p = jnp.concatenate([prev_ok, prev_ok], axis=0) & has_prev
                s_c = jnp.where(ok_c, _dot(q2, ks[rows, :], "nt"), NEG)
                s_p = jnp.where(ok_p, _dot(q2, ks[prow, :], "nt"), NEG)
                m = jnp.max(jnp.maximum(s_c, s_p), axis=1, keepdims=True)
                p_c, p_p = jnp.exp(s_c - m), jnp.exp(s_p - m)
                l = jnp.sum(p_c + p_p, axis=1, keepdims=True)
                o2 = _dot(p_c, vs[rows, :], "nn") + _dot(p_p, vs[prow, :], "nn")
                o_dst[rows, :] = jnp.where(lo, o2[:ABLK], o2[ABLK:])
                m_dst[rows, :] = jnp.where(lo, m[:ABLK], m[ABLK:])
                l_dst[rows, :] = jnp.where(lo, l[:ABLK], l[ABLK:])
                return carry

            lax.fori_loop(0, nb, blk, 0, unroll=8)
            if first:
                continue
            for tok, cls in _class_chunks(S, d):
                m_old, m_b = m_s[tok, :], mb[cls, :]
                m_new = jnp.maximum(m_old, m_b)
                a, b = jnp.exp(m_old - m_new), jnp.exp(m_b - m_new)
                l_new = a * l_s[tok, :] + b * lb[cls, :]
                o_new = a * o_ref[tok, :] + b * ob[cls, :]
                if last:
                    o_ref[tok, :] = o_new / l_new
                    lse_ref[tok, :] = m_new + jnp.log(l_new)
                else:
                    o_ref[tok, :] = o_new
                    m_s[tok, :] = m_new
                    l_s[tok, :] = l_new

    col = pl.BlockSpec((S, LANES), lambda h: (0, h))
    return pl.pallas_call(
        body, name=name, grid=(ad // LANES,),
        in_specs=[col, col, pl.BlockSpec((S, LANES), lambda h: (0, h + v_cb))], out_specs=[col, col],
        out_shape=[jax.ShapeDtypeStruct((S, ad), F32), jax.ShapeDtypeStruct((S, ad), F32)],
        scratch_shapes=[pltpu.VMEM((S, LANES), BF16)] * 3 + [pltpu.VMEM((S, LANES), F32)] * 5,
        compiler_params=_cparams(("parallel",)),
    )(qn, kn, proj)


def _attn_bwd(qn, kn, proj, v_cb, do, lse, dd, name):
    S, ad = qn.shape
    nb = S // ABLK

    def body(q_ref, k_ref, v_ref, do_ref, lse_ref, dd_ref, dq_ref, dk_ref, dv_ref,
             qc, kc, vc, doc, lsec, ddc, dqc, dkc, dvc):
        lo = _lane_mask()
        qi, ki = _band_masks(ABLK, ABLK)
        cur_ok, prev_ok = ki <= qi, ki >= qi
        for bi, (_, d) in enumerate(PATTERNS):
            nbc = S // d // ABLK
            first = bi == 0
            token_order = (q_ref, k_ref, v_ref, do_ref, lse_ref, dd_ref)
            class_order = (qc, kc, vc, doc, lsec, ddc)
            if d > 1:
                for src, dst in zip(token_order, class_order):
                    _to_class_order(src, dst, S, d)
            qs, ks, vs, dos, lses, dds = class_order if d > 1 else token_order
            dq_dst, dk_dst, dv_dst = (dq_ref, dk_ref, dv_ref) if first else (dqc, dkc, dvc)
            dk_dst[...] = jnp.zeros_like(dk_dst)
            dv_dst[...] = jnp.zeros_like(dv_dst)

            def blk(t, carry, nbc=nbc, qs=qs, ks=ks, vs=vs, dos=dos, lses=lses, dds=dds,
                    dq_dst=dq_dst, dk_dst=dk_dst, dv_dst=dv_dst):
                rows, prow = _blk_rows(t), _blk_rows(jnp.maximum(t - 1, 0))
                has_prev = (t % nbc) != 0
                qv, dov, lse_b, dd_b = qs[rows, :], dos[rows, :], lses[rows, :], dds[rows, :]
                lse_r, dd_r = pltpu.roll(lse_b, HD, 1), pltpu.roll(dd_b, HD, 1)
                nlo = jnp.logical_not(lo)
                q2 = jnp.concatenate([_zero_unless(lo, qv), _zero_unless(nlo, qv)], axis=0).astype(BF16)
                do2 = jnp.concatenate([_zero_unless(lo, dov), _zero_unless(nlo, dov)], axis=0).astype(BF16)
                lse2 = jnp.concatenate([_head_lanes(lo, lse_b, lse_r), _head_lanes(nlo, lse_b, lse_r)], axis=0)
                dd2 = jnp.concatenate([_head_lanes(lo, dd_b, dd_r), _head_lanes(nlo, dd_b, dd_r)], axis=0)
                dq2 = None
                for krows, vmask in ((rows, cur_ok), (prow, prev_ok & has_prev)):
                    kv, vv = ks[krows, :], vs[krows, :]
                    vmask2 = jnp.concatenate([vmask, vmask], axis=0)
                    s = jnp.where(vmask2, _dot(q2, kv, "nt"), NEG)
                    p = jnp.exp(s - lse2)
                    ds = p * (_dot(do2, vv, "nt") - dd2)
                    dqk = _dot(ds, kv, "nn")
                    dq2 = dqk if dq2 is None else dq2 + dqk
                    dv_dst[krows, :] += _dot(p, do2, "tn")
                    dk_dst[krows, :] += _dot(ds, q2, "tn")
                dq_dst[rows, :] = jnp.where(lo, dq2[:ABLK], dq2[ABLK:])
                return carry

            lax.fori_loop(0, nb, blk, 0, unroll=4)
            if not first:
                for tok, cls in _class_chunks(S, d):
                    dq_ref[tok, :] = dq_ref[tok, :] + dqc[cls, :]
                    dk_ref[tok, :] = dk_ref[tok, :] + dkc[cls, :]
                    dv_ref[tok, :] = dv_ref[tok, :] + dvc[cls, :]

    col = pl.BlockSpec((S, LANES), lambda h: (0, h))
    col1 = pl.BlockSpec((S, LANES), lambda h: (0, h), pipeline_mode=pl.Buffered(1))
    vcol1 = pl.BlockSpec((S, LANES), lambda h: (0, h + v_cb), pipeline_mode=pl.Buffered(1))
    return pl.pallas_call(
        body, name=name, grid=(ad // LANES,),
        in_specs=[col, col, vcol1, col1, col1, col1], out_specs=[col, col, col],
        out_shape=[jax.ShapeDtypeStruct((S, ad), F32)] * 3,
        scratch_shapes=[pltpu.VMEM((S, LANES), BF16)] * 4 + [pltpu.VMEM((S, LANES), F32)] * 5,
        compiler_params=_cparams(("parallel",)),
    )(qn, kn, proj, do, lse, dd)


def _coords():
    return lax.axis_index("x"), lax.axis_index("y"), lax.axis_index("c")


def _exchange8(xs, per_dest, name):
    n = len(xs)
    blk = [x.shape[1:] if per_dest else x.shape for x in xs]

    def body(*refs):
        ins, outs = refs[:n], refs[n:2 * n]
        send_sems, recv_sems, local_sems = refs[2 * n:]
        x, y, c = _coords()
        sibling = (x, y, 1 - c)
        chips = [(1 - x, y), (x, 1 - y), (1 - x, 1 - y)]
        first, passed, mine = [], [], []
        for a in range(n):
            def src_for(cx, cy, a=a):
                return ins[a].at[2 * cx + cy] if per_dest else ins[a]

            def slot(px, py, pc, a=a):
                return outs[a].at[4 * px + 2 * py + pc]

            def copy(k, src, dst, to, a=a):
                return pltpu.make_async_remote_copy(src_ref=src, dst_ref=dst, send_sem=send_sems.at[7 * a + k],
                                                    recv_sem=recv_sems.at[7 * a + k], device_id=to, device_id_type=MESH)

            m = pltpu.make_async_copy(src_for(x, y), slot(x, y, c), local_sems.at[a])
            m.start()
            mine.append(m)
            cps = [copy(0, src_for(x, y), slot(x, y, c), sibling)]
            cps += [copy(1 + j, src_for(*chip), slot(x, y, c), (*chip, c)) for j, chip in enumerate(chips)]
            for cp in cps:
                cp.start()
            first += cps
        for a in range(n):
            def slot(px, py, pc, a=a):
                return outs[a].at[4 * px + 2 * py + pc]

            def copy(k, src, dst, to, a=a):
                return pltpu.make_async_remote_copy(src_ref=src, dst_ref=dst, send_sem=send_sems.at[7 * a + k],
                                                    recv_sem=recv_sems.at[7 * a + k], device_id=to, device_id_type=MESH)

            for j, chip in enumerate(chips):
                copy(1 + j, slot(*chip, c), slot(*chip, c), (*chip, c)).wait_recv()
                fw = copy(4 + j, slot(*chip, c), slot(*chip, c), sibling)
                fw.start()
                passed.append(fw)
        for a in range(n):
            def slot(px, py, pc, a=a):
                return outs[a].at[4 * px + 2 * py + pc]

            def copy(k, src, dst, to, a=a):
                return pltpu.make_async_remote_copy(src_ref=src, dst_ref=dst, send_sem=send_sems.at[7 * a + k],
                                                    recv_sem=recv_sems.at[7 * a + k], device_id=to, device_id_type=MESH)

            copy(0, slot(x, y, 1 - c), slot(x, y, 1 - c), sibling).wait_recv()
            for j, chip in enumerate(chips):
                copy(4 + j, slot(*chip, 1 - c), slot(*chip, 1 - c), sibling).wait_recv()
        for cp in first + passed:
            cp.wait_send()
        for m in mine:
            m.wait()

    anyspec = pl.BlockSpec(memory_space=pl.ANY)
    res = pl.pallas_call(
        body, name=name, in_specs=[anyspec] * n, out_specs=[anyspec] * n,
        out_shape=[jax.ShapeDtypeStruct((8,) + tuple(b), x.dtype) for b, x in zip(blk, xs)],
        scratch_shapes=[pltpu.SemaphoreType.DMA((7 * n,)), pltpu.SemaphoreType.DMA((7 * n,)),
                        pltpu.SemaphoreType.DMA((n,))],
    )(*xs)
    return list(res)


def _pair_swap(xs, name):
    n = len(xs)

    def body(*refs):
        ins, outs = refs[:n], refs[n:2 * n]
        send_sems, recv_sems = refs[2 * n:]
        x, y, c = _coords()
        cps = [pltpu.make_async_remote_copy(src_ref=ins[a].at[1 - c], dst_ref=outs[a], send_sem=send_sems.at[a],
                                            recv_sem=recv_sems.at[a], device_id=(x, y, 1 - c), device_id_type=MESH)
               for a in range(n)]
        for cp in cps:
            cp.start()
        for cp in cps:
            cp.wait()

    anyspec = pl.BlockSpec(memory_space=pl.ANY)
    res = pl.pallas_call(
        body, name=name, in_specs=[anyspec] * n, out_specs=[anyspec] * n,
        out_shape=[jax.ShapeDtypeStruct(x.shape[1:], x.dtype) for x in xs],
        scratch_shapes=[pltpu.SemaphoreType.DMA((n,)), pltpu.SemaphoreType.DMA((n,))],
    )(*xs)
    return list(res)


_HBM = pl.BlockSpec(memory_space=pltpu.HBM)
_SEM = pl.BlockSpec(memory_space=pltpu.SEMAPHORE)
_EFFECT = pltpu.SideEffectType.DATAFLOW_SIDE_EFFECTING


def _n_peers(both):
    return 7 if both else 3


def _peer(x, y, c, j, both):
    bits = j + 1 if both else 2 * (j + 1)
    dx, dy, dc = bits >> 2 & 1, bits >> 1 & 1, bits & 1
    return (1 - x if dx else x, 1 - y if dy else y, 1 - c if dc else c)


def _spread_copies(s_refs, l_refs, send_sems, recv_sems, per_dest, both):
    x, y, c = _coords()
    me = 4 * x + 2 * y + c
    npeer = _n_peers(both)
    cps = []
    for a in range(len(s_refs)):
        for j in range(npeer):
            tx, ty, tc = _peer(x, y, c, j, both)
            src = s_refs[a].at[2 * tx + ty] if per_dest else s_refs[a]
            cps.append(pltpu.make_async_remote_copy(src_ref=src, dst_ref=l_refs[a].at[me],
                                                    send_sem=send_sems.at[npeer * a + j],
                                                    recv_sem=recv_sems.at[npeer * a + j], device_id=(tx, ty, tc),
                                                    device_id_type=MESH))
    return cps


def _sibling_fill(lands, name):
    n = len(lands)

    def body(*refs):
        outs, send_sems, recv_sems = refs[n:2 * n], refs[2 * n], refs[2 * n + 1]
        x, y, c = _coords()
        cps = [pltpu.make_async_remote_copy(src_ref=outs[a].at[2 * k + c], dst_ref=outs[a].at[2 * k + c],
                                            send_sem=send_sems.at[4 * a + k], recv_sem=recv_sems.at[4 * a + k],
                                            device_id=(x, y, 1 - c), device_id_type=MESH)
               for a in range(n) for k in range(4)]
        for cp in cps:
            cp.start()
        for cp in cps:
            cp.wait()

    anyspec = pl.BlockSpec(memory_space=pl.ANY)
    res = pl.pallas_call(
        body, name=name, in_specs=[anyspec] * n, out_specs=[anyspec] * n,
        out_shape=[jax.ShapeDtypeStruct(t.shape, t.dtype) for t in lands], input_output_aliases={i: i for i in range(n)},
        scratch_shapes=[pltpu.SemaphoreType.DMA((4 * n,)), pltpu.SemaphoreType.DMA((4 * n,))],
    )(*lands)
    return list(res)


def _spread_start(srcs, per_dest, both, dev, chip, name):
    n = len(srcs)
    npeer = _n_peers(both)
    lands = []
    for s in srcs:
        own = lax.dynamic_index_in_dim(s, chip, 0, keepdims=False) if per_dest else s
        lands.append(lax.dynamic_update_index_in_dim(lax.empty((8,) + own.shape, own.dtype), own, dev, 0))

    def body(*refs):
        s_refs, l_refs, send_sems, recv_sems, token = refs[:n], refs[n:2 * n], refs[2 * n], refs[2 * n + 1], refs[-1]
        for cp in _spread_copies(s_refs, l_refs, send_sems, recv_sems, per_dest, both):
            cp.start()
        token[...] = jnp.zeros_like(token)

    hbm_in = [pltpu.with_memory_space_constraint(t, pltpu.HBM) for t in list(srcs) + lands]
    outs = pl.pallas_call(
        body, name=name,
        out_shape=(pltpu.SemaphoreType.DMA((npeer * n,)), pltpu.SemaphoreType.DMA((npeer * n,)),
                   *[pltpu.HBM(t.shape, t.dtype) for t in hbm_in], jax.ShapeDtypeStruct((8, LANES), F32)),
        in_specs=[_HBM] * (2 * n), out_specs=(_SEM, _SEM, *[_HBM] * (2 * n), pl.BlockSpec(memory_space=pltpu.VMEM)),
        input_output_aliases={i: 2 + i for i in range(2 * n)},
        compiler_params=pltpu.CompilerParams(has_side_effects=_EFFECT),
    )(*hbm_in)
    return (outs[0], outs[1], list(outs[2:2 + n]), list(outs[2 + n:2 + 2 * n])), outs[-1]


def _spread_wait(handle, per_dest, both, after, name):
    send_sems, recv_sems, srcs, lands = handle
    n = len(srcs)

    def body(*refs):
        s_refs, l_refs, send_ref, recv_ref = refs[:n], refs[n:2 * n], refs[2 * n], refs[2 * n + 1]
        for cp in _spread_copies(s_refs, l_refs, send_ref, recv_ref, per_dest, both):
            cp.wait_send()
            cp.wait_recv()

    outs = pl.pallas_call(
        body, name=name, out_shape=tuple(pltpu.HBM(t.shape, t.dtype) for t in srcs + lands),
        in_specs=[_HBM] * (2 * n) + [_SEM, _SEM, pl.BlockSpec(memory_space=pl.ANY)], out_specs=tuple([_HBM] * (2 * n)),
        input_output_aliases={i: i for i in range(2 * n)},
        compiler_params=pltpu.CompilerParams(has_side_effects=_EFFECT),
    )(*srcs, *lands, send_sems, recv_sems, after)
    return list(outs[n:])


def _row_tile(n, cap, mult):
    best = n
    for t in range(mult, min(n, cap) + 1, mult):
        if n % t == 0:
            best = t
    return best


PAIR_ADD_BLOCK_BYTES = 2 << 20


def _pair_add(g2, theirs, half, name):
    _, n, cdim = g2.shape
    tm = _row_tile(n, max(16, PAIR_ADD_BLOCK_BYTES // (4 * cdim)), 16)

    def body(h_ref, a_ref, b_ref, o_ref):
        o_ref[...] = (a_ref[...] + b_ref[...]).astype(o_ref.dtype)

    grid_spec = pltpu.PrefetchScalarGridSpec(
        num_scalar_prefetch=1, grid=(n // tm,),
        in_specs=[pl.BlockSpec((None, tm, cdim), lambda i, h: (h[0], i, 0)), pl.BlockSpec((tm, cdim), lambda i, h: (i, 0))],
        out_specs=pl.BlockSpec((tm, cdim), lambda i, h: (i, 0)))
    return pl.pallas_call(body, name=name, grid_spec=grid_spec, out_shape=jax.ShapeDtypeStruct((n, cdim), BF16),
                          compiler_params=_cparams(("parallel",)))(half.reshape(1).astype(jnp.int32), g2, theirs)


def _adamw_math(w, g, m, v):
    m = ADAM_B1 * m + (1.0 - ADAM_B1) * g
    v = ADAM_B2 * v + (1.0 - ADAM_B2) * (g * g)
    m_hat = m / (1.0 - ADAM_B1 ** ADAM_STEP)
    v_hat = v / (1.0 - ADAM_B2 ** ADAM_STEP)
    delta = -ADAM_LR * (m_hat / (jnp.sqrt(v_hat) + ADAM_EPS) + ADAM_WD * w)
    return delta, m, v


def _adamw(parts, w, m, v, name, tm=128):
    npart, R, C = parts.shape
    tm = min(tm, R)

    def body(p_ref, w_ref, m_ref, v_ref, g_out, d_out, m_out, v_out):
        g = p_ref[0].astype(F32)
        for i in range(1, npart):
            g = g + p_ref[i].astype(F32)
        d, mm, vv = _adamw_math(w_ref[...], g, m_ref[...], v_ref[...])
        g_out[...] = g
        d_out[...] = d
        m_out[...] = mm
        v_out[...] = vv

    spec = pl.BlockSpec((tm, C), lambda i: (i, 0))
    return pl.pallas_call(
        body, name=name, grid=(R // tm,),
        in_specs=[pl.BlockSpec((npart, tm, C), lambda i: (0, i, 0)), spec, spec, spec], out_specs=[spec] * 4,
        out_shape=[jax.ShapeDtypeStruct((R, C), F32)] * 4,
        compiler_params=_cparams(("parallel",)),
    )(parts, w, m, v)


def _sum_parts(parts, name):
    npart, R, C = parts.shape

    def body(p_ref, o_ref):
        g = p_ref[0]
        for i in range(1, npart):
            g = g + p_ref[i]
        o_ref[...] = g

    return pl.pallas_call(body, name=name, out_shape=jax.ShapeDtypeStruct((R, C), F32))(parts)


def _mod_fwd(c_all, w_ada, b_sh, name):
    def body(c_ref, w_ref, b_ref, o_ref):
        o_ref[...] = _dot(_silu(c_ref[...]), w_ref[...], "nn") + b_ref[...]

    return pl.pallas_call(body, name=name, out_shape=jax.ShapeDtypeStruct((c_all.shape[0], w_ada.shape[1]), F32),
                          compiler_params=pltpu.CompilerParams(vmem_limit_bytes=VMEM_LIMIT))(c_all, w_ada, b_sh)


def _mod_wgrad(c_all, dmod_sh, name):
    def body(c_ref, d_ref, o_ref):
        o_ref[...] = _dot(_silu(c_ref[...]), d_ref[...], "tn")

    return pl.pallas_call(body, name=name, out_shape=jax.ShapeDtypeStruct((c_all.shape[1], dmod_sh.shape[1]), F32),
                          compiler_params=pltpu.CompilerParams(vmem_limit_bytes=VMEM_LIMIT))(c_all, dmod_sh)


def _pad_lanes(v):
    return jnp.pad(v, ((0, 0), (0, (-v.shape[1]) % LANES)))


def kernel(x, c, norm1_w, norm2_w, w_ada, b_ada, w_in, conv_w, conv_b, dt_bias, a_log, d_skip, ssd_norm_w, q_norm_w, k_norm_w, attn_norm_w, w_out, w_ff1, w_ff2, loss_target, m_norm1_w, m_norm2_w, m_w_ada, m_b_ada, m_w_in, m_conv_w, m_conv_b, m_dt_bias, m_a_log, m_d_skip, m_ssd_norm_w, m_q_norm_w, m_k_norm_w, m_attn_norm_w, m_w_out, m_w_ff1, m_w_ff2, v_norm1_w, v_norm2_w, v_w_ada, v_b_ada, v_w_in, v_conv_w, v_conv_b, v_dt_bias, v_a_log, v_d_skip, v_ssd_norm_w, v_q_norm_w, v_k_norm_w, v_attn_norm_w, v_w_out, v_w_ff1, v_w_ff2):
    xi, yi, ci = _coords()
    chip = 2 * xi + yi
    dev = 2 * chip + ci
    xs, tgt = x[0], loss_target[0]
    S, D = xs.shape
    DI, AD = NH_SSD * HD, NH_ATT * HD
    CC = DI + 2 * NG * NSTATE
    PW = DI + CC + 3 * AD + LANES
    DFF = w_ff1.shape[2] * 4
    MIX = DI + AD
    o_xbc, o_q, o_k, o_v, o_dt = DI, DI + CC, DI + CC + AD, DI + CC + 2 * AD, DI + CC + 3 * AD

    def half_rows(w):
        r = w.shape[0] // 2
        return lax.dynamic_slice_in_dim(w, ci * r, r, 0).astype(BF16)

    c_all, conv_w_all = _exchange8([c, conv_w[0]], False, "gather_c_conv_w")
    c_all = c_all.reshape(8, D)
    c_all = jnp.pad(c_all, ((0, 8), (0, 0)))
    nmod = w_ada.shape[2]
    b_sh = lax.dynamic_slice_in_dim(b_ada, chip * nmod, nmod, 1)
    mod_sh = _mod_fwd(c_all, w_ada[0], b_sh, "mod_fwd")
    mod_all = _exchange8([mod_sh[:8]], False, "gather_mod")[0]
    mod_me = lax.dynamic_index_in_dim(mod_all[0::2], dev, 1, keepdims=False).reshape(1, 4 * nmod)
    shift1, scale1, gate1, shift2, scale2, gate2 = [mod_me[:, i * D:(i + 1) * D] for i in range(6)]

    g_in = _exchange8([half_rows(w_in[0])], False, "gather_w_in")[0]
    rest_handle, rest_token = _spread_start([half_rows(w_out[0]), half_rows(w_ff1[0]), half_rows(w_ff2[0])], False, True,
                                            dev, chip, "gather_rest_start")
    shift1 = shift1 + rest_token[0, 0]
    wsh = w_in.shape[2]
    w_in_f = g_in.reshape(4, D, wsh).transpose(1, 0, 2).reshape(D, 4 * wsh)
    n_zx = DI + CC
    w_proj = jnp.concatenate([w_in_f[:, :n_zx], w_in_f[:, n_zx + NH_SSD:], w_in_f[:, n_zx:n_zx + NH_SSD],
                              jnp.zeros((D, LANES - NH_SSD), BF16)], axis=1)

    dtb, alog, dsk = _pad_lanes(dt_bias), _pad_lanes(a_log), _pad_lanes(d_skip)
    qw2 = jnp.concatenate([q_norm_w, q_norm_w], axis=1)
    kw2 = jnp.concatenate([k_norm_w, k_norm_w], axis=1)
    conv_w_f = conv_w_all[0::2].transpose(1, 0, 2).reshape(KCONV, CC)

    h1 = _rows("norm1", lambda r, k: ([_normmod(r[0], *k)], []), [(xs, 0, D)], [norm1_w, scale1, shift1],
               [(D, BF16)], [], S)[0]
    proj = _matmul(h1, w_proj, "nn", F32, "in_proj", tn=896)
    xbc = _conv_fwd(proj, o_xbc, CC, conv_w_f, conv_b, "conv_fwd")
    y_ssd, hsave = _ssd_fwd(xbc, proj, o_dt // LANES, dtb, alog, dsk, ssd_norm_w, "ssd_fwd")

    def qk_call(name, col0, w2, scale):
        def body(t_ref, w_ref, o_ref):
            o_ref[...] = _headnorm(t_ref[...], w_ref[...], scale)
        return pl.pallas_call(
            body, name=name, grid=(AD // LANES,),
            in_specs=[pl.BlockSpec((S, LANES), lambda j: (0, j + col0 // LANES)),
                      pl.BlockSpec((1, LANES), lambda j: (0, 0))],
            out_specs=pl.BlockSpec((S, LANES), lambda j: (0, j)),
            out_shape=jax.ShapeDtypeStruct((S, AD), F32), compiler_params=_cparams(("parallel",)),
        )(proj, w2)

    qn = qk_call("q_norm", o_q, qw2, HD ** -0.5)
    kn = qk_call("k_norm", o_k, kw2, 1.0)
    o_att, lse = _attn_fwd(qn, kn, proj, o_v // LANES, "attn_fwd")
    y_att = _rows("attn_out_norm", lambda r, k: ([_rmsw(r[0], k[0])], []), [(o_att, 0, AD)], [attn_norm_w],
                  [(AD, BF16)], [], S)[0]
    g_out, g_ff1, g_ff2 = _spread_wait(rest_handle, False, True, o_att, "gather_rest_wait")
    w_out_f = g_out.reshape(MIX, D)
    w_out_a, w_out_b = w_out_f[:DI], w_out_f[DI:]
    w_ff1_f = g_ff1.reshape(4, D, DFF // 4).transpose(1, 0, 2).reshape(D, DFF)
    w_ff2_f = g_ff2.reshape(DFF, D)
    mix_a = _matmul(y_ssd, w_out_a, "nn", F32, "out_proj_a")
    mix = _matmul(y_att, w_out_b, "nn", F32, "out_proj_b", epilogue=lambda r, e: r + e, extras=(mix_a,))
    x2, h2 = _rows("resid_norm2", lambda r, k: (list(_resid_normmod(r[0], r[1], *k)), []), [(xs, 0, D), (mix, 0, D)],
                   [gate1, norm2_w, scale2, shift2], [(D, F32), (D, BF16)], [], S)
    u = _matmul(h2, w_ff1_f, "nn", F32, "ff1")
    relu2 = lambda t: jnp.square(jnp.maximum(t, 0.0))
    ff = _matmul(u, w_ff2_f, "nn", F32, "ff2", a_fn=relu2)

    def loss_fn(r, k):
        x2_, ff_, t_ = r
        err = x2_ + k[0] * ff_ - t_
        dy_ = err * (1.0 / D)
        ls = jnp.sum(jnp.sum(0.5 * err * err, axis=1, keepdims=True), axis=0, keepdims=True) * (1.0 / D)
        return [dy_, dy_ * k[0]], [ls, jnp.sum(dy_ * ff_, axis=0, keepdims=True)]

    dy, dff, loss_p, dgate2 = _rows("loss", loss_fn, [(x2, 0, D), (ff, 0, D), (tgt, 0, D)], [gate2],
                                    [(D, F32), (D, BF16)], [(1, 1), (1, D)], S)
    du = _matmul(dff, w_ff2_f, "nt", BF16, "ff2_dx", epilogue=lambda r, e: r * (2.0 * jnp.maximum(e, 0.0)), extras=(u,))
    gw_ff2 = _matmul(u, dff, "tn", BF16, "ff2_dw", a_fn=relu2, tm=DFF // 4, tn=D, chip_of_tile=lambda i, j: i)
    gw_ff1 = _matmul(h2, du, "tn", BF16, "ff1_dw", tm=D, tn=DFF // 4, chip_of_tile=lambda i, j: j)
    ff_handle, ff_token = _spread_start([gw_ff1, gw_ff2], True, True, dev, chip, "scatter_ff_start")
    dh2 = _matmul(du, w_ff1_f, "nt", F32, "ff1_dx")

    def resid_bwd(r, k):
        x_, mix_, dx2a, dh2_ = r
        _, vjp = jax.vjp(_resid_normmod, x_, mix_, *k)
        dx, dmix_, dg, dnw, dsc, dsh = vjp((dx2a, dh2_))
        return [dx, dmix_], [dg, dnw, dsc, dsh]

    dx2, dmix, dgate1, g_norm2, dscale2, dshift2 = _rows(
        "resid_norm2_bwd", resid_bwd, [(xs, 0, D), (mix, 0, D), (dy, 0, D), (dh2, 0, D)],
        [gate1 + ff_token[0, 0], norm2_w, scale2, shift2], [(D, F32), (D, BF16)], [(1, D)] * 4, S)
    gw_out = jnp.concatenate([_matmul(y_ssd, dmix, "tn", BF16, "out_proj_dw_a"),
                              _matmul(y_att, dmix, "tn", BF16, "out_proj_dw_b")], axis=0)
    out_handle, out_token = _spread_start([gw_out.reshape(4, MIX // 4, D)], True, True, dev, chip, "scatter_out_start")
    dy_ssd = _matmul(dmix, w_out_a, "nt", F32, "out_proj_dx_a")
    dy_att = _matmul(dmix, w_out_b, "nt", F32, "out_proj_dx_b")

    def attn_norm_bwd(r, k):
        o_, dyo = r
        _, vjp = jax.vjp(_rmsw, o_, k[0])
        do_, dw_ = vjp(dyo)
        lo = _lane_mask()
        dd_blocks = []
        for b in range(AD // LANES):
            t = (do_ * o_)[:, b * LANES:(b + 1) * LANES]
            s0 = jnp.sum(jnp.where(lo, t, 0.0), axis=1, keepdims=True)
            s1 = jnp.sum(jnp.where(lo, 0.0, t), axis=1, keepdims=True)
            dd_blocks.append(jnp.where(lo, s0, s1))
        return [do_, jnp.concatenate(dd_blocks, axis=1)], [dw_]

    do_att, dd_att, g_attn_norm = _rows("attn_norm_bwd", attn_norm_bwd, [(o_att, 0, AD), (dy_att, 0, AD)],
                                        [attn_norm_w + out_token[0, 0]], [(AD, F32), (AD, F32)], [(1, AD)], S)
    dq_n, dk_n, dv = _attn_bwd(qn, kn, proj, o_v // LANES, do_att, lse, dd_att, "attn_bwd")

    def qk_bwd_call(name, col0, w2, scale, g):
        def body(t_ref, w_ref, g_ref, o_ref, dw_ref):
            @pl.when(pl.program_id(0) == 0)
            def _():
                dw_ref[...] = jnp.zeros_like(dw_ref)
            _, vjp = jax.vjp(lambda t, w: _headnorm(t, w, scale), t_ref[...], w_ref[...])
            dt_, dw_ = vjp(g_ref[...])
            o_ref[...] = dt_.astype(BF16)
            dw_ref[...] += dw_
        blk = pl.BlockSpec((S, LANES), lambda j: (0, j))
        return pl.pallas_call(
            body, name=name, grid=(AD // LANES,),
            in_specs=[pl.BlockSpec((S, LANES), lambda j: (0, j + col0 // LANES)),
                      pl.BlockSpec((1, LANES), lambda j: (0, 0)), blk],
            out_specs=[blk, pl.BlockSpec((1, LANES), lambda j: (0, 0))],
            out_shape=[jax.ShapeDtypeStruct((S, AD), BF16), jax.ShapeDtypeStruct((1, LANES), F32)],
            compiler_params=_cparams(("arbitrary",)),
        )(proj, w2, g)

    dq, g_qw2 = qk_bwd_call("q_norm_bwd", o_q, qw2, HD ** -0.5, dq_n)
    dk, g_kw2 = qk_bwd_call("k_norm_bwd", o_k, kw2, 1.0, dk_n)
    g_q_norm = g_qw2[:, :HD] + g_qw2[:, HD:]
    g_k_norm = g_kw2[:, :HD] + g_kw2[:, HD:]

    dxbc, dz, ddtr, g_dtb, g_alog, g_dsk, g_ssd_norm = _ssd_bwd(
        xbc, proj, o_dt // LANES, dtb, alog, dsk, ssd_norm_w, hsave, dy_ssd, "ssd_bwd")
    dxbc_pre, g_conv_w, g_conv_b = _conv_bwd(proj, o_xbc, CC, conv_w_f, conv_b, dxbc, "conv_bwd")
    dproj = jnp.concatenate([dz.astype(BF16), dxbc_pre.astype(BF16), dq, dk, dv.astype(BF16), ddtr.astype(BF16)], axis=1)
    gw_proj = _matmul(h1, dproj, "tn", F32, "in_proj_dw", tn=896)
    gw_halves = gw_proj.reshape(2, D // 2, PW)
    sum_p = _pair_add(gw_halves, _pair_swap([gw_halves], "pair_swap_in")[0], ci, "pair_add_in")
    sum_in = jnp.concatenate([sum_p[:, :n_zx], sum_p[:, o_dt:o_dt + NH_SSD], sum_p[:, n_zx:o_dt]], axis=1)
    in_handle, in_token = _spread_start([sum_in.reshape(D // 2, 4, wsh).transpose(1, 0, 2)], True, False, dev, chip,
                                        "scatter_in_start")
    dh1 = _matmul(dproj, w_proj, "nt", F32, "in_proj_dx", tk=896)

    def norm1_bwd(r, k):
        x_, dh_, dres = r
        _, vjp = jax.vjp(_normmod, x_, *k)
        dx, dnw, dsc, dsh = vjp(dh_)
        return [dx + dres], [dnw, dsc, dsh]

    grad_x, g_norm1, dscale1, dshift1 = _rows("norm1_bwd", norm1_bwd, [(xs, 0, D), (dh1, 0, D), (dx2, 0, D)],
                                              [norm1_w + in_token[0, 0], scale1, shift1], [(D, F32)], [(1, D)] * 3, S)
    dmod =jnp.concatenate([dshift1, dscale1, dgate1, dshift2, dscale2, dgate2], axis=1)

    small = [g_norm1, g_norm2, dmod, g_conv_b, g_dtb, g_alog, g_dsk, g_ssd_norm, _pad_lanes(g_q_norm),
             _pad_lanes(g_k_norm), g_attn_norm, g_conv_w.reshape(1, KCONV * CC)]
    sizes = [t.shape[1] for t in small]
    packed = jnp.concatenate(small, axis=1)
    nrow = -(-packed.shape[1] // LANES // 8) * 8
    packed = jnp.pad(packed, ((0, 0), (0, nrow * LANES - packed.shape[1]))).reshape(nrow, LANES)
    packed_all = _exchange8([packed], False, "gather_small_grads")[0]
    tot = _sum_parts(packed_all, "sum_small_grads").reshape(1, nrow * LANES)
    offs = [sum(sizes[:i]) for i in range(len(sizes))]
    (g_norm1, g_norm2, g_b_ada, g_conv_b, g_dtb, g_alog, g_dsk, g_ssd_norm, g_q_norm, g_k_norm, g_attn_norm,
     g_conv_w) = [tot[:, o:o + n] for o, n in zip(offs, sizes)]
    g_dtb, g_alog, g_dsk = g_dtb[:, :NH_SSD], g_alog[:, :NH_SSD], g_dsk[:, :NH_SSD]
    g_q_norm, g_k_norm = g_q_norm[:, :HD], g_k_norm[:, :HD]
    ccs = CC // 4
    g_conv_w = lax.dynamic_slice_in_dim(g_conv_w.reshape(KCONV, CC), chip * ccs, ccs, 1)

    dmod_all = packed_all.reshape(8, nrow * LANES)[:, offs[2]:offs[2] + 6 * D]
    dmod_sh = jnp.pad(lax.dynamic_slice_in_dim(dmod_all, chip * nmod, nmod, 1), ((0, 8), (0, 0)))
    gw_ada = _mod_wgrad(c_all, dmod_sh, "mod_wgrad")

    parts_ff1, parts_ff2 = _spread_wait(ff_handle, True, True, in_token, "scatter_ff_wait")
    res_ff1 = _adamw(parts_ff1, w_ff1[0], m_w_ff1[0], v_w_ff1[0], "adamw_w_ff1")
    res_ff2 = _adamw(parts_ff2, w_ff2[0], m_w_ff2[0], v_w_ff2[0], "adamw_w_ff2")
    parts_out = _spread_wait(out_handle, True, True, in_token, "scatter_out_wait")[0]
    res_out = _adamw(parts_out, w_out[0], m_w_out[0], v_w_out[0], "adamw_w_out")
    res_ada = _adamw(gw_ada[None], w_ada[0], m_w_ada[0], v_w_ada[0], "adamw_w_ada")
    lands_in = _sibling_fill(_spread_wait(in_handle, True, False, res_ada[0], "scatter_in_wait"), "scatter_in_fill")[0]
    res_in = _adamw(lands_in.reshape(4, D, wsh), w_in[0], m_w_in[0], v_w_in[0], "adamw_w_in")

    small_names = ["norm1_w", "norm2_w", "b_ada", "conv_w", "conv_b", "dt_bias", "a_log", "d_skip", "ssd_norm_w",
                   "q_norm_w", "k_norm_w", "attn_norm_w"]
    small_g = dict(norm1_w=g_norm1, norm2_w=g_norm2, b_ada=g_b_ada, conv_w=g_conv_w.reshape(1, KCONV * ccs),
                   conv_b=g_conv_b, dt_bias=g_dtb, a_log=g_alog, d_skip=g_dsk, ssd_norm_w=g_ssd_norm, q_norm_w=g_q_norm,
                   k_norm_w=g_k_norm, attn_norm_w=g_attn_norm)
    small_w = dict(norm1_w=(norm1_w, m_norm1_w, v_norm1_w), norm2_w=(norm2_w, m_norm2_w, v_norm2_w),
                   b_ada=(b_ada, m_b_ada, v_b_ada),
                   conv_w=tuple(t.reshape(1, KCONV * ccs) for t in (conv_w, m_conv_w, v_conv_w)),
                   conv_b=(conv_b, m_conv_b, v_conv_b), dt_bias=(dt_bias, m_dt_bias, v_dt_bias),
                   a_log=(a_log, m_a_log, v_a_log), d_skip=(d_skip, m_d_skip, v_d_skip),
                   ssd_norm_w=(ssd_norm_w, m_ssd_norm_w, v_ssd_norm_w), q_norm_w=(q_norm_w, m_q_norm_w, v_q_norm_w),
                   k_norm_w=(k_norm_w, m_k_norm_w, v_k_norm_w), attn_norm_w=(attn_norm_w, m_attn_norm_w, v_attn_norm_w))
    ssz = [_pad_lanes(small_g[n]).shape[1] for n in small_names]
    soff = [sum(ssz[:i]) for i in range(len(ssz))]
    srow = -(-sum(ssz) // LANES // 8) * 8

    def pack(ts, fill):
        t = jnp.concatenate([jnp.pad(t, ((0, 0), (0, (-t.shape[1]) % LANES)), constant_values=fill) for t in ts], axis=1)
        return jnp.pad(t, ((0, 0), (0, srow * LANES - t.shape[1])), constant_values=fill).reshape(srow, LANES)

    sg = pack([small_g[n] for n in small_names], 0.0)
    sw = pack([small_w[n][0] for n in small_names], 0.0)
    sm_ = pack([small_w[n][1] for n in small_names], 0.0)
    sv = pack([small_w[n][2] for n in small_names], 1.0)
    _, s_delta, s_m, s_v = _adamw(sg[None], sw, sm_, sv, "adamw_small", tm=srow)

    def unpack(t, n):
        i = small_names.index(n)
        return t.reshape(1, srow * LANES)[:, soff[i]:soff[i] + small_g[n].shape[1]].reshape(small_w[n][0].shape)

    loss = lax.psum(loss_p[0, 0], ("x", "y", "c"))
    big_res = dict(w_ada=res_ada, w_in=res_in, w_out=res_out, w_ff1=res_ff1, w_ff2=res_ff2)
    order = ["norm1_w", "norm2_w", "w_ada", "b_ada", "w_in", "conv_w", "conv_b", "dt_bias", "a_log", "d_skip",
             "ssd_norm_w", "q_norm_w", "k_norm_w", "attn_norm_w", "w_out", "w_ff1", "w_ff2"]
    grads, deltas, new_m, new_v = [], [], [], []
    for n in order:
        if n in big_res:
            g_, d_, m_, v_ = [t[None] for t in big_res[n]]
        else:
            g_ = small_g[n].reshape(small_w[n][0].shape)
            d_, m_, v_ = unpack(s_delta, n), unpack(s_m, n), unpack(s_v, n)
            if n == "conv_w":
                g_, d_, m_, v_ = [t.reshape(conv_w.shape) for t in (g_, d_, m_, v_)]
        grads.append(g_)
        deltas.append(d_)
        new_m.append(m_)
        new_v.append(v_)
    return (loss, grad_x[None], *grads, *deltas, *new_m, *new_v)
```

```python
import functools

import jax
import jax.numpy as jnp
from jax import lax
from jax.experimental import pallas as pl
from jax.experimental.pallas import tpu as pltpu

F32, BF16 = jnp.float32, jnp.bfloat16
EPS = 1e-6
HD = 64
NH_SSD = 16
NG = 4
NSTATE = 128
KCONV = 4
CHUNK = 128
NH_ATT = 16
PATTERNS = ((128, 1), (512, 4), (2048, 16))
ABLK = 128
LANES = 128
ADAM_LR, ADAM_B1, ADAM_B2, ADAM_EPS, ADAM_WD, ADAM_STEP = 0.001, 0.9, 0.999, 1e-08, 0.01, 10
VMEM_LIMIT = 56 * 1024 * 1024
MESH = pl.DeviceIdType.MESH
NEG = -1e30

_DN = {"nn": (((1,), (0,)), ((), ())), "nt": (((1,), (1,)), ((), ())), "tn": (((0,), (0,)), ((), ()))}


def _cparams(sem):
    return pltpu.CompilerParams(dimension_semantics=sem, vmem_limit_bytes=VMEM_LIMIT)


def _tile(n, cap):
    if n % LANES or n <= LANES:
        return n
    best = LANES
    for t in range(LANES, min(n, cap) + 1, LANES):
        if n % t == 0:
            best = t
    return best


def _silu(x):
    return x / (1.0 + jnp.exp(-x))


def _softplus(x):
    return jnp.maximum(x, 0.0) + jnp.log(1.0 + jnp.exp(-jnp.abs(x)))


def _dot(a, b, dims):
    return lax.dot_general(a.astype(BF16), b.astype(BF16), _DN[dims], preferred_element_type=F32)


def _matmul(a, b, dims, out_dtype, name, a_fn=None, epilogue=None, extras=(), tm=1024, tn=1024, tk=2048,
            chip_of_tile=None):
    if dims == "nn":
        (M, K), (_, N) = a.shape, b.shape
    elif dims == "nt":
        (M, K), (N, _) = a.shape, b.shape
    else:
        (K, M), (_, N) = a.shape, b.shape
    tm, tn, tk = _tile(M, tm), _tile(N, tn), _tile(K, tk)
    nk = K // tk
    ne = len(extras)

    def body(a_ref, b_ref, *rest):
        e_refs, o_ref = rest[:ne], rest[ne]
        av = a_ref[...]
        if a_fn is not None:
            av = a_fn(av)
        part = _dot(av, b_ref[...], dims)

        def finish(r):
            if epilogue is not None:
                r = epilogue(r, *[e[...] for e in e_refs])
            o_ref[...] = r.astype(out_dtype).reshape(o_ref.shape)

        if nk == 1:
            finish(part)
            return
        acc = rest[ne + 1]
        k = pl.program_id(2)

        @pl.when(k == 0)
        def _():
            acc[...] = part

        @pl.when(k > 0)
        def _():
            acc[...] += part

        @pl.when(k == nk - 1)
        def _():
            finish(acc[...])

    a_spec = pl.BlockSpec((tk, tm), lambda i, j, k: (k, i)) if dims == "tn" else pl.BlockSpec((tm, tk), lambda i, j, k: (i, k))
    b_spec = pl.BlockSpec((tn, tk), lambda i, j, k: (j, k)) if dims == "nt" else pl.BlockSpec((tk, tn), lambda i, j, k: (k, j))
    o_spec = pl.BlockSpec((tm, tn), lambda i, j, k: (i, j))
    out_spec, out_dims = o_spec, (M, N)
    if chip_of_tile is not None:
        assert (M // tm) * (N // tn) == 4
        out_spec = pl.BlockSpec((None, tm, tn), lambda i, j, k: (chip_of_tile(i, j), 0, 0))
        out_dims = (4, tm, tn)
    return pl.pallas_call(
        body, name=name, grid=(M // tm, N // tn, nk),
        in_specs=[a_spec, b_spec] + [o_spec] * ne, out_specs=out_spec,
        out_shape=jax.ShapeDtypeStruct(out_dims, out_dtype),
        scratch_shapes=[pltpu.VMEM((tm, tn), F32)] if nk > 1 else [],
        compiler_params=_cparams(("parallel", "parallel", "arbitrary")),
    )(a, b, *extras)


def _rows(name, fn, rows, consts, outs, accs, n_rows, tm=256):
    tm = min(tm, n_rows)
    nr, nc, no, na = len(rows), len(consts), len(outs), len(accs)

    def body(*refs):
        r_refs, c_refs = refs[:nr], refs[nr:nr + nc]
        o_refs, a_refs = refs[nr + nc:nr + nc + no], refs[nr + nc + no:]
        o_vals, a_vals = fn([r[...] for r in r_refs], [c[...] for c in c_refs])
        for ref, val in zip(o_refs, o_vals):
            ref[...] = val.astype(ref.dtype)
        if na:
            @pl.when(pl.program_id(0) == 0)
            def _():
                for ref in a_refs:
                    ref[...] = jnp.zeros_like(ref)
            for ref, val in zip(a_refs, a_vals):
                ref[...] += val

    in_specs = [pl.BlockSpec((tm, w), lambda i, cb=cb: (i, cb)) for (_, cb, w) in rows]
    in_specs += [pl.BlockSpec(cst.shape, lambda i, nd=cst.ndim: (0,) * nd) for cst in consts]
    out_specs = [pl.BlockSpec((tm, w), lambda i: (i, 0)) for (w, _) in outs]
    out_specs += [pl.BlockSpec(s, lambda i: (0, 0)) for s in accs]
    out_shape = [jax.ShapeDtypeStruct((n_rows, w), dt) for (w, dt) in outs]
    out_shape += [jax.ShapeDtypeStruct(s, F32) for s in accs]
    res = pl.pallas_call(
        body, name=name, grid=(n_rows // tm,), in_specs=in_specs, out_specs=out_specs, out_shape=out_shape,
        compiler_params=_cparams(("arbitrary",)),
    )(*[r[0] for r in rows], *consts)
    return res


def _normmod(x, nw, sc, sh):
    r = lax.rsqrt(jnp.mean(x * x, axis=-1, keepdims=True) + EPS)
    return (x * r) * nw * (1.0 + sc) + sh


def _resid_normmod(x, mix, g, nw, sc, sh):
    x2 = x + g * mix
    return x2, _normmod(x2, nw, sc, sh)


def _rmsw(o, w):
    return o * lax.rsqrt(jnp.mean(o * o, axis=-1, keepdims=True) + EPS) * w


def _lane_mask():
    return lax.broadcasted_iota(jnp.int32, (1, LANES), 1) < HD


def _headnorm(t, w, scale):
    lo = _lane_mask()
    t2 = t * t
    s0 = jnp.sum(jnp.where(lo, t2, 0.0), axis=1, keepdims=True)
    s1 = jnp.sum(jnp.where(lo, 0.0, t2), axis=1, keepdims=True)
    ms = jnp.where(lo, s0, s1) * (1.0 / HD)
    return t * lax.rsqrt(ms + EPS) * w * scale


CONV_ROWS = 128
CONV_HALO = 8


def _conv_cols(n_ch):
    return _tile(n_ch, LANES)


def _conv_fwd(proj, col0, n_ch, conv_w, conv_b, name):
    S = proj.shape[0]
    tc = _conv_cols(n_ch)

    R, H = CONV_ROWS, CONV_HALO

    def body(u_ref, w_ref, b_ref, o_ref):
        w = [w_ref[i:i + 1, :] for i in range(KCONV)]
        b = b_ref[...]

        def chunk(ext):
            acc = b + w[KCONV - 1] * ext[H:]
            for i in range(KCONV - 1):
                acc = acc + w[i] * pltpu.roll(ext, KCONV - 1 - i, 0)[H:]
            return _silu(acc)

        o_ref[0:R, :] = chunk(jnp.concatenate([jnp.zeros((H, tc), F32), u_ref[0:R, :]], axis=0))

        def step(c, carry):
            r0 = pl.multiple_of(c * R, R)
            o_ref[pl.ds(r0, R), :] = chunk(u_ref[pl.ds(pl.multiple_of(r0 - H, H), R + H), :])
            return carry

        lax.fori_loop(1, S // R, step, 0)

    return pl.pallas_call(
        body, name=name, grid=(n_ch // tc,),
        in_specs=[pl.BlockSpec((S, tc), lambda j: (0, j + col0 // tc)),
                  pl.BlockSpec((KCONV, tc), lambda j: (0, j)), pl.BlockSpec((1, tc), lambda j: (0, j))],
        out_specs=pl.BlockSpec((S, tc), lambda j: (0, j)),
        out_shape=jax.ShapeDtypeStruct((S, n_ch), F32),
        compiler_params=_cparams(("parallel",)),
    )(proj, conv_w, conv_b)


def _conv_bwd(proj, col0, n_ch, conv_w, conv_b, dxbc, name):
    S = proj.shape[0]
    tc = _conv_cols(n_ch)

    R, H = CONV_ROWS, CONV_HALO

    def body(u_ref, w_ref, b_ref, g_ref, du_ref, dw_ref, db_ref):
        w = [w_ref[i:i + 1, :] for i in range(KCONV)]
        b = b_ref[...]
        pad = jnp.zeros((H, tc), F32)

        def chunk(u_ext, g_ext):
            taps = [pltpu.roll(u_ext, KCONV - 1 - i, 0)[H:] for i in range(KCONV - 1)] + [u_ext[H:]]
            acc = b
            for i in range(KCONV):
                acc = acc + w[i] * taps[i]
            sig = 1.0 / (1.0 + jnp.exp(-acc))
            dacc = g_ext * (sig * (1.0 + acc * (1.0 - sig)))
            du = w[KCONV - 1] * dacc[:R]
            for i in range(KCONV - 1):
                du = du + w[i] * pltpu.roll(dacc, R + H - (KCONV - 1 - i), 0)[:R]
            d = dacc[:R]
            return du, [jnp.sum(d * t[:R], axis=0, keepdims=True) for t in taps], jnp.sum(d, axis=0, keepdims=True)

        du, dws, db = chunk(jnp.concatenate([pad, u_ref[0:R + H, :]], axis=0), g_ref[0:R + H, :])
        du_ref[0:R, :] = du

        def step(c, carry):
            r0 = pl.multiple_of(c * R, R)
            du_c, dws_c, db_c = chunk(u_ref[pl.ds(pl.multiple_of(r0 - H, H), R + 2 * H), :], g_ref[pl.ds(r0, R + H), :])
            du_ref[pl.ds(r0, R), :] = du_c
            return [a + b_ for a, b_ in zip(carry[0], dws_c)], carry[1] + db_c

        dws, db = lax.fori_loop(1, S // R - 1, step, (dws, db))
        du, dws_l, db_l = chunk(jnp.concatenate([u_ref[S - R - H:S, :], pad], axis=0),
                                jnp.concatenate([g_ref[S - R:S, :], pad], axis=0))
        du_ref[S - R:S, :] = du
        for i in range(KCONV):
            dw_ref[i:i + 1, :] = dws[i] + dws_l[i]
        db_ref[...] = db + db_l

    return pl.pallas_call(
        body, name=name, grid=(n_ch // tc,),
        in_specs=[pl.BlockSpec((S, tc), lambda j: (0, j + col0 // tc)),
                  pl.BlockSpec((KCONV, tc), lambda j: (0, j)), pl.BlockSpec((1, tc), lambda j: (0, j)),
                  pl.BlockSpec((S, tc), lambda j: (0, j))],
        out_specs=[pl.BlockSpec((S, tc), lambda j: (0, j)), pl.BlockSpec((KCONV, tc), lambda j: (0, j)),
                   pl.BlockSpec((1, tc), lambda j: (0, j))],
        out_shape=[jax.ShapeDtypeStruct((S, n_ch), F32), jax.ShapeDtypeStruct((KCONV, n_ch), F32),
                   jax.ShapeDtypeStruct((1, n_ch), F32)],
        compiler_params=_cparams(("parallel",)),
    )(proj, conv_w, conv_b, dxbc)


@functools.partial(jax.custom_vjp, nondiff_argnums=(2,))
def _mm(a, b, dims):
    return _dot(a, b, dims)


def _mm_fwd(a, b, dims):
    return _dot(a, b, dims), (a, b)


def _mm_bwd(dims, res, g):
    a, b = res
    if dims == "nn":
        return _dot(g, b, "nt"), _dot(a, g, "tn")
    if dims == "nt":
        return _dot(g, b, "nn"), _dot(g, a, "tn")
    return _dot(b, g, "nt"), _dot(a, g, "nn")


_mm.defvjp(_mm_fwd, _mm_bwd)


def _tri_dot(x, upper):
    n = x.shape[0]
    r = lax.broadcasted_iota(jnp.int32, (n, n), 0)
    c = lax.broadcasted_iota(jnp.int32, (n, n), 1)
    t = jnp.where((r <= c) if upper else (r >= c), 1.0, 0.0)
    return lax.dot_general(t, x, _DN["nn"], precision=lax.Precision.HIGHEST, preferred_element_type=F32)


@jax.custom_vjp
def _cumsum_rows(x):
    return _tri_dot(x, False)


_cumsum_rows.defvjp(lambda x: (_tri_dot(x, False), None), lambda _, g: (_tri_dot(g, True),))


def _ssd_chunk(xs_p, bm_g, cm_g, dtr, z_p, dtb, alog, dsk, nw_p, h_p):
    L = dtr.shape[0]
    n_pairs = len(xs_p)
    ppg = n_pairs // len(bm_g)
    lane = lax.broadcasted_iota(jnp.int32, (1, LANES), 1)
    sub = lax.broadcasted_iota(jnp.int32, (LANES, 1), 0)
    lo = lane < HD
    row_l = lax.broadcasted_iota(jnp.int32, (L, 1), 0)
    tri = lax.broadcasted_iota(jnp.int32, (L, L), 0) >= lax.broadcasted_iota(jnp.int32, (L, L), 1)

    dt = _softplus(dtr + dtb)
    acs = _cumsum_rows(dt * (-jnp.exp(alog)))
    acs_t = acs.T
    a_last = jnp.sum(jnp.where(row_l == L - 1, acs, 0.0), axis=0, keepdims=True)
    e_acs = jnp.exp(acs)
    dec = jnp.exp(a_last - acs)
    cdec = jnp.exp(a_last)

    def colv(m, h):
        return jnp.sum(jnp.where(lane == h, m, 0.0), axis=1, keepdims=True)

    def rowv(mt, h):
        return jnp.sum(jnp.where(sub == h, mt, 0.0), axis=0, keepdims=True)

    def pair(m, h0):
        return jnp.where(lo, colv(m, h0), colv(m, h0 + 1))

    ys, hs = [], []
    cb = None
    for p in range(n_pairs):
        g, h0 = p // ppg, 2 * p
        bmat, cmat = bm_g[g], cm_g[g]
        if p % ppg == 0:
            cb = _mm(cmat, bmat, "nt")
        x = xs_p[p]
        xdt = x * pair(dt, h0)
        yd = []
        for h in (h0, h0 + 1):
            seg = colv(acs, h) - rowv(acs_t, h)
            lm = jnp.where(tri, jnp.exp(jnp.where(tri, seg, 0.0)), 0.0)
            yd.append(_mm(cb * lm, xdt, "nn"))
        y = jnp.where(lo, yd[0], yd[1])
        y = y + _mm(cmat, h_p[p], "nt") * pair(e_acs, h0)
        st = _mm(xdt * pair(dec, h0), bmat, "tn")
        cd_col = jnp.where(sub < HD, colv(cdec, h0), colv(cdec, h0 + 1))
        hs.append(h_p[p] * cd_col + st)
        ys.append(y + pair(dsk, h0) * x)

    y2 = [ys[p] * _silu(z_p[p]) for p in range(n_pairs)]
    outs = []
    for g in range(len(bm_g)):
        ps = range(g * ppg, (g + 1) * ppg)
        ss = sum(jnp.sum(y2[p] * y2[p], axis=1, keepdims=True) for p in ps)
        rs = lax.rsqrt(ss * (1.0 / (ppg * LANES)) + EPS)
        outs += [y2[p] * rs * nw_p[p] for p in ps]
    return outs, hs


def _ssd_slices(xbc_ref, z_ref, nw_ref, di):
    n_pairs = di // LANES
    xs_p = [xbc_ref[:, p * LANES:(p + 1) * LANES] for p in range(n_pairs)]
    bm_g = [xbc_ref[:, di + g * NSTATE:di + (g + 1) * NSTATE] for g in range(NG)]
    cm_g = [xbc_ref[:, di + (NG + g) * NSTATE:di + (NG + g + 1) * NSTATE] for g in range(NG)]
    z_p = [z_ref[:, p * LANES:(p + 1) * LANES] for p in range(n_pairs)]
    nw_p = [nw_ref[:, p * LANES:(p + 1) * LANES] for p in range(n_pairs)]
    return xs_p, bm_g, cm_g, z_p, nw_p


def _ssd_fwd(xbc, proj, dt_cb, dtb, alog, dsk, nw, name):
    S, cc = xbc.shape
    di = NH_SSD * HD
    n_pairs = di // LANES
    nchunk = S // CHUNK

    def body(xbc_ref, z_ref, dtr_ref, dtb_ref, alog_ref, dsk_ref, nw_ref, y_ref, hs_ref, h_scr):
        @pl.when(pl.program_id(0) == 0)
        def _():
            h_scr[...] = jnp.zeros_like(h_scr)

        xs_p, bm_g, cm_g, z_p, nw_p = _ssd_slices(xbc_ref, z_ref, nw_ref, di)
        h_p = [h_scr[p * LANES:(p + 1) * LANES, :] for p in range(n_pairs)]
        hs_ref[...] = h_scr[...]
        outs, hs = _ssd_chunk(xs_p, bm_g, cm_g, dtr_ref[...], z_p, dtb_ref[...], alog_ref[...], dsk_ref[...], nw_p, h_p)
        for p in range(n_pairs):
            y_ref[:, p * LANES:(p + 1) * LANES] = outs[p].astype(y_ref.dtype)
            h_scr[p * LANES:(p + 1) * LANES, :] = hs[p]

    vec = pl.BlockSpec((1, LANES), lambda c: (0, 0))
    return pl.pallas_call(
        body, name=name, grid=(nchunk,),
        in_specs=[pl.BlockSpec((CHUNK, cc), lambda c: (c, 0)), pl.BlockSpec((CHUNK, di), lambda c: (c, 0)),
                  pl.BlockSpec((CHUNK, LANES), lambda c: (c, dt_cb)), vec, vec, vec,
                  pl.BlockSpec((1, di), lambda c: (0, 0))],
        out_specs=[pl.BlockSpec((CHUNK, di), lambda c: (c, 0)), pl.BlockSpec((None, di, NSTATE), lambda c: (c, 0, 0))],
        out_shape=[jax.ShapeDtypeStruct((S, di), BF16), jax.ShapeDtypeStruct((nchunk, di, NSTATE), F32)],
        scratch_shapes=[pltpu.VMEM((di, NSTATE), F32)],
        compiler_params=_cparams(("arbitrary",)),
    )(xbc, proj, proj, dtb, alog, dsk, nw)


def _ssd_bwd(xbc, proj, dt_cb, dtb, alog, dsk, nw, hsave, dy, name):
    S, cc = xbc.shape
    di = NH_SSD * HD
    n_pairs = di // LANES
    nchunk = S // CHUNK

    def body(xbc_ref, z_ref, dtr_ref, dtb_ref, alog_ref, dsk_ref, nw_ref, hs_ref, dy_ref,
             dxbc_ref, dz_ref, ddtr_ref, ddtb_ref, dalog_ref, ddsk_ref, dnw_ref, dh_scr):
        @pl.when(pl.program_id(0) == 0)
        def _():
            dh_scr[...] = jnp.zeros_like(dh_scr)
            ddtb_ref[...] = jnp.zeros_like(ddtb_ref)
            dalog_ref[...] = jnp.zeros_like(dalog_ref)
            ddsk_ref[...] = jnp.zeros_like(ddsk_ref)
            dnw_ref[...] = jnp.zeros_like(dnw_ref)

        xs_p, bm_g, cm_g, z_p, nw_p = _ssd_slices(xbc_ref, z_ref, nw_ref, di)
        h_p = [hs_ref[p * LANES:(p + 1) * LANES, :] for p in range(n_pairs)]
        dy_p = [dy_ref[:, p * LANES:(p + 1) * LANES].astype(F32) for p in range(n_pairs)]
        dh_p = [dh_scr[p * LANES:(p + 1) * LANES, :] for p in range(n_pairs)]
        _, vjp = jax.vjp(_ssd_chunk, xs_p, bm_g, cm_g, dtr_ref[...], z_p, dtb_ref[...], alog_ref[...], dsk_ref[...],
                         nw_p, h_p)
        dxs, dbm, dcm, ddtr, dz, ddtb, dalog, ddsk, dnw, dh = vjp((dy_p, dh_p))
        for p in range(n_pairs):
            sl = slice(p * LANES, (p + 1) * LANES)
            dxbc_ref[:, sl] = dxs[p]
            dz_ref[:, sl] = dz[p]
            dnw_ref[:, sl] += dnw[p]
            dh_scr[sl, :] = dh[p]
        for g in range(NG):
            dxbc_ref[:, di + g * NSTATE:di + (g + 1) * NSTATE] = dbm[g]
            dxbc_ref[:, di + (NG + g) * NSTATE:di + (NG + g + 1) * NSTATE] = dcm[g]
        ddtr_ref[...] = ddtr
        ddtb_ref[...] += ddtb
        dalog_ref[...] += dalog
        ddsk_ref[...] += ddsk

    last = nchunk - 1
    vec = pl.BlockSpec((1, LANES), lambda c: (0, 0))
    return pl.pallas_call(
        body, name=name, grid=(nchunk,),
        in_specs=[pl.BlockSpec((CHUNK, cc), lambda c: (last - c, 0)), pl.BlockSpec((CHUNK, di), lambda c: (last - c, 0)),
                  pl.BlockSpec((CHUNK, LANES), lambda c: (last - c, dt_cb)), vec, vec, vec,
                  pl.BlockSpec((1, di), lambda c: (0, 0)),
                  pl.BlockSpec((None, di, NSTATE), lambda c: (last - c, 0, 0)),
                  pl.BlockSpec((CHUNK, di), lambda c: (last - c, 0))],
        out_specs=[pl.BlockSpec((CHUNK, cc), lambda c: (last - c, 0)), pl.BlockSpec((CHUNK, di), lambda c: (last - c, 0)),
                   pl.BlockSpec((CHUNK, LANES), lambda c: (last - c, 0)), vec, vec, vec,
                   pl.BlockSpec((1, di), lambda c: (0, 0))],
        out_shape=[jax.ShapeDtypeStruct((S, cc), F32), jax.ShapeDtypeStruct((S, di), F32),
                   jax.ShapeDtypeStruct((S, LANES), F32), jax.ShapeDtypeStruct((1, LANES), F32),
                   jax.ShapeDtypeStruct((1, LANES), F32), jax.ShapeDtypeStruct((1, LANES), F32),
                   jax.ShapeDtypeStruct((1, di), F32)],
        scratch_shapes=[pltpu.VMEM((di, NSTATE), F32)],
        compiler_params=_cparams(("arbitrary",)),
    )(xbc, proj, proj, dtb, alog, dsk, nw, hsave, dy)


def _band_masks(rows_q, rows_k):
    qi = lax.broadcasted_iota(jnp.int32, (rows_q, rows_k), 0)
    ki = lax.broadcasted_iota(jnp.int32, (rows_q, rows_k), 1)
    return qi, ki


def _class_chunks(n_rows, d):
    per_class = n_rows // d
    ch = min(per_class, 256)
    out = []
    for r in range(d):
        for c0 in range(0, per_class, ch):
            tok = pl.ds(c0, ch) if d == 1 else pl.ds(r + d * c0, ch, stride=d)
            out.append((tok, pl.ds(r * per_class + c0, ch)))
    return out


def _to_class_order(src_ref, dst_ref, n_rows, d):
    for tok, cls in _class_chunks(n_rows, d):
        dst_ref[cls, :] = src_ref[tok, :].astype(dst_ref.dtype)


def _blk_rows(t):
    return pl.ds(pl.multiple_of(t * ABLK, ABLK), ABLK)


def _head_lanes(msk, t, t_rolled):
    return jnp.where(msk, t, t_rolled)


def _zero_unless(msk, t):
    return jnp.where(msk, t, jnp.zeros_like(t))


def _attn_fwd(qn, kn, proj, v_cb, name):
    S, ad = qn.shape
    nb = S // ABLK
    nbr = len(PATTERNS)

    def body(q_ref, k_ref, v_ref, o_ref, lse_ref, qc, kc, vc, ob, mb, lb, m_s, l_s):
        lo = _lane_mask()
        qi, ki = _band_masks(ABLK, ABLK)
        cur_ok, prev_ok = ki <= qi, ki >= qi
        for bi, (_, d) in enumerate(PATTERNS):
            nbc = S // d // ABLK
            first, last = bi == 0, bi == nbr - 1
            qs, ks, vs = q_ref, k_ref, v_ref
            if d > 1:
                qs, ks, vs = qc, kc, vc
                for src, dst in ((q_ref, qc), (k_ref, kc), (v_ref, vc)):
                    _to_class_order(src, dst, S, d)
            o_dst, m_dst, l_dst = (o_ref, m_s, l_s) if first else (ob, mb, lb)

            def blk(t, carry, nbc=nbc, qs=qs, ks=ks, vs=vs, o_dst=o_dst, m_dst=m_dst, l_dst=l_dst):
                rows, prow = _blk_rows(t), _blk_rows(jnp.maximum(t - 1, 0))
                has_prev = (t % nbc) != 0
                qv = qs[rows, :]
                q2 = jnp.concatenate([_zero_unless(lo, qv), _zero_unless(jnp.logical_not(lo), qv)], axis=0).astype(BF16)
                ok_c = jnp.concatenate([cur_ok, cur_ok], axis=0)
                ok_p = jnp.concatenate([prev_ok, prev_ok], axis=0) & has_prev
                s_c = jnp.where(ok_c, _dot(q2, ks[rows, :], "nt"), NEG)
                s_p = jnp.where(ok_p, _dot(q2, ks[prow, :], "nt"), NEG)
                m = jnp.max(jnp.maximum(s_c, s_p), axis=1, keepdims=True)
                p_c, p_p = jnp.exp(s_c - m), jnp.exp(s_p - m)
                l = jnp.sum(p_c + p_p, axis=1, keepdims=True)
                o2 = _dot(p_c, vs[rows, :], "nn") + _dot(p_p, vs[prow, :], "nn")
                o_dst[rows, :] = jnp.where(lo, o2[:ABLK], o2[ABLK:])
                m_dst[rows, :] = jnp.where(lo, m[:ABLK], m[ABLK:])
                l_dst[rows, :] = jnp.where(lo, l[:ABLK], l[ABLK:])
                return carry

            lax.fori_loop(0, nb, blk, 0, unroll=8)
            if first:
                continue
            for tok, cls in _class_chunks(S, d):
                m_old, m_b = m_s[tok, :], mb[cls, :]
                m_new = jnp.maximum(m_old, m_b)
                a, b = jnp.exp(m_old - m_new), jnp.exp(m_b - m_new)
                l_new = a * l_s[tok, :] + b * lb[cls, :]
                o_new = a * o_ref[tok, :] + b * ob[cls, :]
                if last:
                    o_ref[tok, :] = o_new / l_new
                    lse_ref[tok, :] = m_new + jnp.log(l_new)
                else:
                    o_ref[tok, :] = o_new
                    m_s[tok, :] = m_new
                    l_s[tok, :] = l_new

    col = pl.BlockSpec((S, LANES), lambda h: (0, h))
    return pl.pallas_call(
        body, name=name, grid=(ad // LANES,),
        in_specs=[col, col, pl.BlockSpec((S, LANES), lambda h: (0, h + v_cb))], out_specs=[col, col],
        out_shape=[jax.ShapeDtypeStruct((S, ad), F32), jax.ShapeDtypeStruct((S, ad), F32)],
        scratch_shapes=[pltpu.VMEM((S, LANES), BF16)] * 3 + [pltpu.VMEM((S, LANES), F32)] * 5,
        compiler_params=_cparams(("parallel",)),
    )(qn, kn, proj)


def _attn_bwd(qn, kn, proj, v_cb, do, lse, dd, name):
    S, ad = qn.shape
    nb = S // ABLK

    def body(q_ref, k_ref, v_ref, do_ref, lse_ref, dd_ref, dq_ref, dk_ref, dv_ref,
             qc, kc, vc, doc, lsec, ddc, dqc, dkc, dvc):
        lo = _lane_mask()
        qi, ki = _band_masks(ABLK, ABLK)
        cur_ok, prev_ok = ki <= qi, ki >= qi
        for bi, (_, d) in enumerate(PATTERNS):
            nbc = S // d // ABLK
            first = bi == 0
            token_order = (q_ref, k_ref, v_ref, do_ref, lse_ref, dd_ref)
            class_order = (qc, kc, vc, doc, lsec, ddc)
            if d > 1:
                for src, dst in zip(token_order, class_order):
                    _to_class_order(src, dst, S, d)
            qs, ks, vs, dos, lses, dds = class_order if d > 1 else token_order
            dq_dst, dk_dst, dv_dst = (dq_ref, dk_ref, dv_ref) if first else (dqc, dkc, dvc)
            dk_dst[...] = jnp.zeros_like(dk_dst)
            dv_dst[...] = jnp.zeros_like(dv_dst)

            def blk(t, carry, nbc=nbc, qs=qs, ks=ks, vs=vs, dos=dos, lses=lses, dds=dds,
                    dq_dst=dq_dst, dk_dst=dk_dst, dv_dst=dv_dst):
                rows, prow = _blk_rows(t), _blk_rows(jnp.maximum(t - 1, 0))
                has_prev = (t % nbc) != 0
                qv, dov, lse_b, dd_b = qs[rows, :], dos[rows, :], lses[rows, :], dds[rows, :]
                lse_r, dd_r = pltpu.roll(lse_b, HD, 1), pltpu.roll(dd_b, HD, 1)
                nlo = jnp.logical_not(lo)
                q2 = jnp.concatenate([_zero_unless(lo, qv), _zero_unless(nlo, qv)], axis=0).astype(BF16)
                do2 = jnp.concatenate([_zero_unless(lo, dov), _zero_unless(nlo, dov)], axis=0).astype(BF16)
                lse2 = jnp.concatenate([_head_lanes(lo, lse_b, lse_r), _head_lanes(nlo, lse_b, lse_r)], axis=0)
                dd2 = jnp.concatenate([_head_lanes(lo, dd_b, dd_r), _head_lanes(nlo, dd_b, dd_r)], axis=0)
                dq2 = None
                for krows, vmask in ((rows, cur_ok), (prow, prev_ok & has_prev)):
                    kv, vv = ks[krows, :], vs[krows, :]
                    vmask2 = jnp.concatenate([vmask, vmask], axis=0)
                    s = jnp.where(vmask2, _dot(q2, kv, "nt"), NEG)
                    p = jnp.exp(s - lse2)
                    ds = p * (_dot(do2, vv, "nt") - dd2)
                    dqk = _dot(ds, kv, "nn")
                    dq2 = dqk if dq2 is None else dq2 + dqk
                    dv_dst[krows, :] += _dot(p, do2, "tn")
                    dk_dst[krows, :] += _dot(ds, q2, "tn")
                dq_dst[rows, :] = jnp.where(lo, dq2[:ABLK], dq2[ABLK:])
                return carry

            lax.fori_loop(0, nb, blk, 0, unroll=4)
            if not first:
                for tok, cls in _class_chunks(S, d):
                    dq_ref[tok, :] = dq_ref[tok, :] + dqc[cls, :]
                    dk_ref[tok, :] = dk_ref[tok, :] + dkc[cls, :]
                    dv_ref[tok, :] = dv_ref[tok, :] + dvc[cls, :]

    col = pl.BlockSpec((S, LANES), lambda h: (0, h))
    col1 = pl.BlockSpec((S, LANES), lambda h: (0, h), pipeline_mode=pl.Buffered(1))
    vcol1 = pl.BlockSpec((S, LANES), lambda h: (0, h + v_cb), pipeline_mode=pl.Buffered(1))
    return pl.pallas_call(
        body, name=name, grid=(ad // LANES,),
        in_specs=[col, col, vcol1, col1, col1, col1], out_specs=[col, col, col],
        out_shape=[jax.ShapeDtypeStruct((S, ad), F32)] * 3,
        scratch_shapes=[pltpu.VMEM((S, LANES), BF16)] * 4 + [pltpu.VMEM((S, LANES), F32)] * 5,
        compiler_params=_cparams(("parallel",)),
    )(qn, kn, proj, do, lse, dd)


def _coords():
    return lax.axis_index("x"), lax.axis_index("y"), lax.axis_index("c")


def _exchange8(xs, per_dest, name):
    n = len(xs)
    blk = [x.shape[1:] if per_dest else x.shape for x in xs]

    def body(*refs):
        ins, outs = refs[:n], refs[n:2 * n]
        send_sems, recv_sems, local_sems = refs[2 * n:]
        x, y, c = _coords()
        sibling = (x, y, 1 - c)
        chips = [(1 - x, y), (x, 1 - y), (1 - x, 1 - y)]
        first, passed, mine = [], [], []
        for a in range(n):
            def src_for(cx, cy, a=a):
                return ins[a].at[2 * cx + cy] if per_dest else ins[a]

            def slot(px, py, pc, a=a):
                return outs[a].at[4 * px + 2 * py + pc]

            def copy(k, src, dst, to, a=a):
                return pltpu.make_async_remote_copy(src_ref=src, dst_ref=dst, send_sem=send_sems.at[7 * a + k],
                                                    recv_sem=recv_sems.at[7 * a + k], device_id=to, device_id_type=MESH)

            m = pltpu.make_async_copy(src_for(x, y), slot(x, y, c), local_sems.at[a])
            m.start()
            mine.append(m)
            cps = [copy(0, src_for(x, y), slot(x, y, c), sibling)]
            cps += [copy(1 + j, src_for(*chip), slot(x, y, c), (*chip, c)) for j, chip in enumerate(chips)]
            for cp in cps:
                cp.start()
            first += cps
        for a in range(n):
            def slot(px, py, pc, a=a):
                return outs[a].at[4 * px + 2 * py + pc]

            def copy(k, src, dst, to, a=a):
                return pltpu.make_async_remote_copy(src_ref=src, dst_ref=dst, send_sem=send_sems.at[7 * a + k],
                                                    recv_sem=recv_sems.at[7 * a + k], device_id=to, device_id_type=MESH)

            for j, chip in enumerate(chips):
                copy(1 + j, slot(*chip, c), slot(*chip, c), (*chip, c)).wait_recv()
                fw = copy(4 + j, slot(*chip, c), slot(*chip, c), sibling)
                fw.start()
                passed.append(fw)
        for a in range(n):
            def slot(px, py, pc, a=a):
                return outs[a].at[4 * px + 2 * py + pc]

            def copy(k, src, dst, to, a=a):
                return pltpu.make_async_remote_copy(src_ref=src, dst_ref=dst, send_sem=send_sems.at[7 * a + k],
                                                    recv_sem=recv_sems.at[7 * a + k], device_id=to, device_id_type=MESH)

            copy(0, slot(x, y, 1 - c), slot(x, y, 1 - c), sibling).wait_recv()
            for j, chip in enumerate(chips):
                copy(4 + j, slot(*chip, 1 - c), slot(*chip, 1 - c), sibling).wait_recv()
        for cp in first + passed:
            cp.wait_send()
        for m in mine:
            m.wait()

    anyspec = pl.BlockSpec(memory_space=pl.ANY)
    res = pl.pallas_call(
        body, name=name, in_specs=[anyspec] * n, out_specs=[anyspec] * n,
        out_shape=[jax.ShapeDtypeStruct((8,) + tuple(b), x.dtype) for b, x in zip(blk, xs)],
        scratch_shapes=[pltpu.SemaphoreType.DMA((7 * n,)), pltpu.SemaphoreType.DMA((7 * n,)),
                        pltpu.SemaphoreType.DMA((n,))],
    )(*xs)
    return list(res)


def _pair_swap(xs, name):
    n = len(xs)

    def body(*refs):
        ins, outs = refs[:n], refs[n:2 * n]
        send_sems, recv_sems = refs[2 * n:]
        x, y, c = _coords()
        cps = [pltpu.make_async_remote_copy(src_ref=ins[a].at[1 - c], dst_ref=outs[a], send_sem=send_sems.at[a],
                                            recv_sem=recv_sems.at[a], device_id=(x, y, 1 - c), device_id_type=MESH)
               for a in range(n)]
        for cp in cps:
            cp.start()
        for cp in cps:
            cp.wait()

    anyspec = pl.BlockSpec(memory_space=pl.ANY)
    res = pl.pallas_call(
        body, name=name, in_specs=[anyspec] * n, out_specs=[anyspec] * n,
        out_shape=[jax.ShapeDtypeStruct(x.shape[1:], x.dtype) for x in xs],
        scratch_shapes=[pltpu.SemaphoreType.DMA((n,)), pltpu.SemaphoreType.DMA((n,))],
    )(*xs)
    return list(res)


_HBM = pl.BlockSpec(memory_space=pltpu.HBM)
_SEM = pl.BlockSpec(memory_space=pltpu.SEMAPHORE)
_EFFECT = pltpu.SideEffectType.DATAFLOW_SIDE_EFFECTING


def _n_peers(both):
    return 7 if both else 3


def _peer(x, y, c, j, both):
    bits = j + 1 if both else 2 * (j + 1)
    dx, dy, dc = bits >> 2 & 1, bits >> 1 & 1, bits & 1
    return (1 - x if dx else x, 1 - y if dy else y, 1 - c if dc else c)


def _spread_copies(s_refs, l_refs, send_sems, recv_sems, per_dest, both):
    x, y, c = _coords()
    me = 4 * x + 2 * y + c
    npeer = _n_peers(both)
    cps = []
    for a in range(len(s_refs)):
        for j in range(npeer):
            tx, ty, tc = _peer(x, y, c, j, both)
            src = s_refs[a].at[2 * tx + ty] if per_dest else s_refs[a]
            cps.append(pltpu.make_async_remote_copy(src_ref=src, dst_ref=l_refs[a].at[me],
                                                    send_sem=send_sems.at[npeer * a + j],
                                                    recv_sem=recv_sems.at[npeer * a + j], device_id=(tx, ty, tc),
                                                    device_id_type=MESH))
    return cps


def _sibling_fill(lands, name):
    n = len(lands)

    def body(*refs):
        outs, send_sems, recv_sems = refs[n:2 * n], refs[2 * n], refs[2 * n + 1]
        x, y, c = _coords()
        cps = [pltpu.make_async_remote_copy(src_ref=outs[a].at[2 * k + c], dst_ref=outs[a].at[2 * k + c],
                                            send_sem=send_sems.at[4 * a + k], recv_sem=recv_sems.at[4 * a + k],
                                            device_id=(x, y, 1 - c), device_id_type=MESH)
               for a in range(n) for k in range(4)]
        for cp in cps:
            cp.start()
        for cp in cps:
            cp.wait()

    anyspec = pl.BlockSpec(memory_space=pl.ANY)
    res = pl.pallas_call(
        body, name=name, in_specs=[anyspec] * n, out_specs=[anyspec] * n,
        out_shape=[jax.ShapeDtypeStruct(t.shape, t.dtype) for t in lands], input_output_aliases={i: i for i in range(n)},
        scratch_shapes=[pltpu.SemaphoreType.DMA((4 * n,)), pltpu.SemaphoreType.DMA((4 * n,))],
    )(*lands)
    return list(res)


def _spread_start(srcs, per_dest, both, dev, chip, name):
    n = len(srcs)
    npeer = _n_peers(both)
    lands = []
    for s in srcs:
        own = lax.dynamic_index_in_dim(s, chip, 0, keepdims=False) if per_dest else s
        lands.append(lax.dynamic_update_index_in_dim(lax.empty((8,) + own.shape, own.dtype), own, dev, 0))

    def body(*refs):
        s_refs, l_refs, send_sems, recv_sems, token = refs[:n], refs[n:2 * n], refs[2 * n], refs[2 * n + 1], refs[-1]
        for cp in _spread_copies(s_refs, l_refs, send_sems, recv_sems, per_dest, both):
            cp.start()
        token[...] = jnp.zeros_like(token)

    hbm_in = [pltpu.with_memory_space_constraint(t, pltpu.HBM) for t in list(srcs) + lands]
    outs = pl.pallas_call(
        body, name=name,
        out_shape=(pltpu.SemaphoreType.DMA((npeer * n,)), pltpu.SemaphoreType.DMA((npeer * n,)),
                   *[pltpu.HBM(t.shape, t.dtype) for t in hbm_in], jax.ShapeDtypeStruct((8, LANES), F32)),
        in_specs=[_HBM] * (2 * n), out_specs=(_SEM, _SEM, *[_HBM] * (2 * n), pl.BlockSpec(memory_space=pltpu.VMEM)),
        input_output_aliases={i: 2 + i for i in range(2 * n)},
        compiler_params=pltpu.CompilerParams(has_side_effects=_EFFECT),
    )(*hbm_in)
    return (outs[0], outs[1], list(outs[2:2 + n]), list(outs[2 + n:2 + 2 * n])), outs[-1]


def _spread_wait(handle, per_dest, both, after, name):
    send_sems, recv_sems, srcs, lands = handle
    n = len(srcs)

    def body(*refs):
        s_refs, l_refs, send_ref, recv_ref = refs[:n], refs[n:2 * n], refs[2 * n], refs[2 * n + 1]
        for cp in _spread_copies(s_refs, l_refs, send_ref, recv_ref, per_dest, both):
            cp.wait_send()
            cp.wait_recv()

    outs = pl.pallas_call(
        body, name=name, out_shape=tuple(pltpu.HBM(t.shape, t.dtype) for t in srcs + lands),
        in_specs=[_HBM] * (2 * n) + [_SEM, _SEM, pl.BlockSpec(memory_space=pl.ANY)], out_specs=tuple([_HBM] * (2 * n)),
        input_output_aliases={i: i for i in range(2 * n)},
        compiler_params=pltpu.CompilerParams(has_side_effects=_EFFECT),
    )(*srcs, *lands, send_sems, recv_sems, after)
    return list(outs[n:])


def _row_tile(n, cap, mult):
    best = n
    for t in range(mult, min(n, cap) + 1, mult):
        if n % t == 0:
            best = t
    return best


PAIR_ADD_BLOCK_BYTES = 2 << 20


def _pair_add(g2, theirs, half, name):
    _, n, cdim = g2.shape
    tm = _row_tile(n, max(16, PAIR_ADD_BLOCK_BYTES // (4 * cdim)), 16)

    def body(h_ref, a_ref, b_ref, o_ref):
        o_ref[...] = (a_ref[...] + b_ref[...]).astype(o_ref.dtype)

    grid_spec = pltpu.PrefetchScalarGridSpec(
        num_scalar_prefetch=1, grid=(n // tm,),
        in_specs=[pl.BlockSpec((None, tm, cdim), lambda i, h: (h[0], i, 0)), pl.BlockSpec((tm, cdim), lambda i, h: (i, 0))],
        out_specs=pl.BlockSpec((tm, cdim), lambda i, h: (i, 0)))
    return pl.pallas_call(body, name=name, grid_spec=grid_spec, out_shape=jax.ShapeDtypeStruct((n, cdim), BF16),
                          compiler_params=_cparams(("parallel",)))(half.reshape(1).astype(jnp.int32), g2, theirs)


def _adamw_math(w, g, m, v):
    m = ADAM_B1 * m + (1.0 - ADAM_B1) * g
    v = ADAM_B2 * v + (1.0 - ADAM_B2) * (g * g)
    m_hat = m / (1.0 - ADAM_B1 ** ADAM_STEP)
    v_hat = v / (1.0 - ADAM_B2 ** ADAM_STEP)
    delta = -ADAM_LR * (m_hat / (jnp.sqrt(v_hat) + ADAM_EPS) + ADAM_WD * w)
    return delta, m, v


def _adamw(parts, w, m, v, name, tm=128):
    npart, R, C = parts.shape
    tm = min(tm, R)

    def body(p_ref, w_ref, m_ref, v_ref, g_out, d_out, m_out, v_out):
        g = p_ref[0].astype(F32)
        for i in range(1, npart):
            g = g + p_ref[i].astype(F32)
        d, mm, vv = _adamw_math(w_ref[...], g, m_ref[...], v_ref[...])
        g_out[...] = g
        d_out[...] = d
        m_out[...] = mm
        v_out[...] = vv

    spec = pl.BlockSpec((tm, C), lambda i: (i, 0))
    return pl.pallas_call(
        body, name=name, grid=(R // tm,),
        in_specs=[pl.BlockSpec((npart, tm, C), lambda i: (0, i, 0)), spec, spec, spec], out_specs=[spec] * 4,
        out_shape=[jax.ShapeDtypeStruct((R, C), F32)] * 4,
        compiler_params=_cparams(("parallel",)),
    )(parts, w, m, v)


def _sum_parts(parts, name):
    npart, R, C = parts.shape

    def body(p_ref, o_ref):
        g = p_ref[0]
        for i in range(1, npart):
            g = g + p_ref[i]
        o_ref[...] = g

    return pl.pallas_call(body, name=name, out_shape=jax.ShapeDtypeStruct((R, C), F32))(parts)


def _mod_fwd(c_all, w_ada, b_sh, name):
    def body(c_ref, w_ref, b_ref, o_ref):
        o_ref[...] = _dot(_silu(c_ref[...]), w_ref[...], "nn") + b_ref[...]

    return pl.pallas_call(body, name=name, out_shape=jax.ShapeDtypeStruct((c_all.shape[0], w_ada.shape[1]), F32),
                          compiler_params=pltpu.CompilerParams(vmem_limit_bytes=VMEM_LIMIT))(c_all, w_ada, b_sh)


def _mod_wgrad(c_all, dmod_sh, name):
    def body(c_ref, d_ref, o_ref):
        o_ref[...] = _dot(_silu(c_ref[...]), d_ref[...], "tn")

    return pl.pallas_call(body, name=name, out_shape=jax.ShapeDtypeStruct((c_all.shape[1], dmod_sh.shape[1]), F32),
                          compiler_params=pltpu.CompilerParams(vmem_limit_bytes=VMEM_LIMIT))(c_all, dmod_sh)


def _pad_lanes(v):
    return jnp.pad(v, ((0, 0), (0, (-v.shape[1]) % LANES)))


def kernel(x, c, norm1_w, norm2_w, w_ada, b_ada, w_in, conv_w, conv_b, dt_bias, a_log, d_skip, ssd_norm_w, q_norm_w, k_norm_w, attn_norm_w, w_out, w_ff1, w_ff2, loss_target, m_norm1_w, m_norm2_w, m_w_ada, m_b_ada, m_w_in, m_conv_w, m_conv_b, m_dt_bias, m_a_log, m_d_skip, m_ssd_norm_w, m_q_norm_w, m_k_norm_w, m_attn_norm_w, m_w_out, m_w_ff1, m_w_ff2, v_norm1_w, v_norm2_w, v_w_ada, v_b_ada, v_w_in, v_conv_w, v_conv_b, v_dt_bias, v_a_log, v_d_skip, v_ssd_norm_w, v_q_norm_w, v_k_norm_w, v_attn_norm_w, v_w_out, v_w_ff1, v_w_ff2):
    xi, yi, ci = _coords()
    chip = 2 * xi + yi
    dev = 2 * chip + ci
    xs, tgt = x[0], loss_target[0]
    S, D = xs.shape
    DI, AD = NH_SSD * HD, NH_ATT * HD
    CC = DI + 2 * NG * NSTATE
    PW = DI + CC + 3 * AD + LANES
    DFF = w_ff1.shape[2] * 4
    MIX = DI + AD
    o_xbc, o_q, o_k, o_v, o_dt = DI, DI + CC, DI + CC + AD, DI + CC + 2 * AD, DI + CC + 3 * AD

    def half_rows(w):
        r = w.shape[0] // 2
        return lax.dynamic_slice_in_dim(w, ci * r, r, 0).astype(BF16)

    c_all, conv_w_all = _exchange8([c, conv_w[0]], False, "gather_c_conv_w")
    c_all = c_all.reshape(8, D)
    c_all = jnp.pad(c_all, ((0, 8), (0, 0)))
    nmod = w_ada.shape[2]
    b_sh = lax.dynamic_slice_in_dim(b_ada, chip * nmod, nmod, 1)
    mod_sh = _mod_fwd(c_all, w_ada[0], b_sh, "mod_fwd")
    mod_all = _exchange8([mod_sh[:8]], False, "gather_mod")[0]
    mod_me = lax.dynamic_index_in_dim(mod_all[0::2], dev, 1, keepdims=False).reshape(1, 4 * nmod)
    shift1, scale1, gate1, shift2, scale2, gate2 = [mod_me[:, i * D:(i + 1) * D] for i in range(6)]

    g_in = _exchange8([half_rows(w_in[0])], False, "gather_w_in")[0]
    rest_handle, rest_token = _spread_start([half_rows(w_out[0]), half_rows(w_ff1[0]), half_rows(w_ff2[0])], False, True,
                                            dev, chip, "gather_rest_start")
    shift1 = shift1 + rest_token[0, 0]
    wsh = w_in.shape[2]
    w_in_f = g_in.reshape(4, D, wsh).transpose(1, 0, 2).reshape(D, 4 * wsh)
    n_zx = DI + CC
    w_proj = jnp.concatenate([w_in_f[:, :n_zx], w_in_f[:, n_zx + NH_SSD:], w_in_f[:, n_zx:n_zx + NH_SSD],
                              jnp.zeros((D, LANES - NH_SSD), BF16)], axis=1)

    dtb, alog, dsk = _pad_lanes(dt_bias), _pad_lanes(a_log), _pad_lanes(d_skip)
    qw2 = jnp.concatenate([q_norm_w, q_norm_w], axis=1)
    kw2 = jnp.concatenate([k_norm_w, k_norm_w], axis=1)
    conv_w_f = conv_w_all[0::2].transpose(1, 0, 2).reshape(KCONV, CC)

    h1 = _rows("norm1", lambda r, k: ([_normmod(r[0], *k)], []), [(xs, 0, D)], [norm1_w, scale1, shift1],
               [(D, BF16)], [], S)[0]
    proj = _matmul(h1, w_proj, "nn", F32, "in_proj", tn=896)
    xbc = _conv_fwd(proj, o_xbc, CC, conv_w_f, conv_b, "conv_fwd")
    y_ssd, hsave = _ssd_fwd(xbc, proj, o_dt // LANES, dtb, alog, dsk, ssd_norm_w, "ssd_fwd")

    def qk_call(name, col0, w2, scale):
        def body(t_ref, w_ref, o_ref):
            o_ref[...] = _headnorm(t_ref[...], w_ref[...], scale)
        return pl.pallas_call(
            body, name=name, grid=(AD // LANES,),
            in_specs=[pl.BlockSpec((S, LANES), lambda j: (0, j + col0 // LANES)),
                      pl.BlockSpec((1, LANES), lambda j: (0, 0))],
            out_specs=pl.BlockSpec((S, LANES), lambda j: (0, j)),
            out_shape=jax.ShapeDtypeStruct((S, AD), F32), compiler_params=_cparams(("parallel",)),
        )(proj, w2)

    qn = qk_call("q_norm", o_q, qw2, HD ** -0.5)
    kn = qk_call("k_norm", o_k, kw2, 1.0)
    o_att, lse = _attn_fwd(qn, kn, proj, o_v // LANES, "attn_fwd")
    y_att = _rows("attn_out_norm", lambda r, k: ([_rmsw(r[0], k[0])], []), [(o_att, 0, AD)], [attn_norm_w],
                  [(AD, BF16)], [], S)[0]
    g_out, g_ff1, g_ff2 = _spread_wait(rest_handle, False, True, o_att, "gather_rest_wait")
    w_out_f = g_out.reshape(MIX, D)
    w_out_a, w_out_b = w_out_f[:DI], w_out_f[DI:]
    w_ff1_f = g_ff1.reshape(4, D, DFF // 4).transpose(1, 0, 2).reshape(D, DFF)
    w_ff2_f = g_ff2.reshape(DFF, D)
    mix_a = _matmul(y_ssd, w_out_a, "nn", F32, "out_proj_a")
    mix = _matmul(y_att, w_out_b, "nn", F32, "out_proj_b", epilogue=lambda r, e: r + e, extras=(mix_a,))
    x2, h2 = _rows("resid_norm2", lambda r, k: (list(_resid_normmod(r[0], r[1], *k)), []), [(xs, 0, D), (mix, 0, D)],
                   [gate1, norm2_w, scale2, shift2], [(D, F32), (D, BF16)], [], S)
    u = _matmul(h2, w_ff1_f, "nn", F32, "ff1")
    relu2 = lambda t: jnp.square(jnp.maximum(t, 0.0))
    ff = _matmul(u, w_ff2_f, "nn", F32, "ff2", a_fn=relu2)

    def loss_fn(r, k):
        x2_, ff_, t_ = r
        err = x2_ + k[0] * ff_ - t_
        dy_ = err * (1.0 / D)
        ls = jnp.sum(jnp.sum(0.5 * err * err, axis=1, keepdims=True), axis=0, keepdims=True) * (1.0 / D)
        return [dy_, dy_ * k[0]], [ls, jnp.sum(dy_ * ff_, axis=0, keepdims=True)]

    dy, dff, loss_p, dgate2 = _rows("loss", loss_fn, [(x2, 0, D), (ff, 0, D), (tgt, 0, D)], [gate2],
                                    [(D, F32), (D, BF16)], [(1, 1), (1, D)], S)
    du = _matmul(dff, w_ff2_f, "nt", BF16, "ff2_dx", epilogue=lambda r, e: r * (2.0 * jnp.maximum(e, 0.0)), extras=(u,))
    gw_ff2 = _matmul(u, dff, "tn", BF16, "ff2_dw", a_fn=relu2, tm=DFF // 4, tn=D, chip_of_tile=lambda i, j: i)
    gw_ff1 = _matmul(h2, du, "tn", BF16, "ff1_dw", tm=D, tn=DFF // 4, chip_of_tile=lambda i, j: j)
    ff_handle, ff_token = _spread_start([gw_ff1, gw_ff2], True, True, dev, chip, "scatter_ff_start")
    dh2 = _matmul(du, w_ff1_f, "nt", F32, "ff1_dx")

    def resid_bwd(r, k):
        x_, mix_, dx2a, dh2_ = r
        _, vjp = jax.vjp(_resid_normmod, x_, mix_, *k)
        dx, dmix_, dg, dnw, dsc, dsh = vjp((dx2a, dh2_))
        return [dx, dmix_], [dg, dnw, dsc, dsh]

    dx2, dmix, dgate1, g_norm2, dscale2, dshift2 = _rows(
        "resid_norm2_bwd", resid_bwd, [(xs, 0, D), (mix, 0, D), (dy, 0, D), (dh2, 0, D)],
        [gate1 + ff_token[0, 0], norm2_w, scale2, shift2], [(D, F32), (D, BF16)], [(1, D)] * 4, S)
    gw_out = jnp.concatenate([_matmul(y_ssd, dmix, "tn", BF16, "out_proj_dw_a"),
                              _matmul(y_att, dmix, "tn", BF16, "out_proj_dw_b")], axis=0)
    out_handle, out_token = _spread_start([gw_out.reshape(4, MIX // 4, D)], True, True, dev, chip, "scatter_out_start")
    dy_ssd = _matmul(dmix, w_out_a, "nt", F32, "out_proj_dx_a")
    dy_att = _matmul(dmix, w_out_b, "nt", F32, "out_proj_dx_b")

    def attn_norm_bwd(r, k):
        o_, dyo = r
        _, vjp = jax.vjp(_rmsw, o_, k[0])
        do_, dw_ = vjp(dyo)
        lo = _lane_mask()
        dd_blocks = []
        for b in range(AD // LANES):
            t = (do_ * o_)[:, b * LANES:(b + 1) * LANES]
            s0 = jnp.sum(jnp.where(lo, t, 0.0), axis=1, keepdims=True)
            s1 = jnp.sum(jnp.where(lo, 0.0, t), axis=1, keepdims=True)
            dd_blocks.append(jnp.where(lo, s0, s1))
        return [do_, jnp.concatenate(dd_blocks, axis=1)], [dw_]

    do_att, dd_att, g_attn_norm = _rows("attn_norm_bwd", attn_norm_bwd, [(o_att, 0, AD), (dy_att, 0, AD)],
                                        [attn_norm_w + out_token[0, 0]], [(AD, F32), (AD, F32)], [(1, AD)], S)
    dq_n, dk_n, dv = _attn_bwd(qn, kn, proj, o_v // LANES, do_att, lse, dd_att, "attn_bwd")

    def qk_bwd_call(name, col0, w2, scale, g):
        def body(t_ref, w_ref, g_ref, o_ref, dw_ref):
            @pl.when(pl.program_id(0) == 0)
            def _():
                dw_ref[...] = jnp.zeros_like(dw_ref)
            _, vjp = jax.vjp(lambda t, w: _headnorm(t, w, scale), t_ref[...], w_ref[...])
            dt_, dw_ = vjp(g_ref[...])
            o_ref[...] = dt_.astype(BF16)
            dw_ref[...] += dw_
        blk = pl.BlockSpec((S, LANES), lambda j: (0, j))
        return pl.pallas_call(
            body, name=name, grid=(AD // LANES,),
            in_specs=[pl.BlockSpec((S, LANES), lambda j: (0, j + col0 // LANES)),
                      pl.BlockSpec((1, LANES), lambda j: (0, 0)), blk],
            out_specs=[blk, pl.BlockSpec((1, LANES), lambda j: (0, 0))],
            out_shape=[jax.ShapeDtypeStruct((S, AD), BF16), jax.ShapeDtypeStruct((1, LANES), F32)],
            compiler_params=_cparams(("arbitrary",)),
        )(proj, w2, g)

    dq, g_qw2 = qk_bwd_call("q_norm_bwd", o_q, qw2, HD ** -0.5, dq_n)
    dk, g_kw2 = qk_bwd_call("k_norm_bwd", o_k, kw2, 1.0, dk_n)
    g_q_norm = g_qw2[:, :HD] + g_qw2[:, HD:]
    g_k_norm = g_kw2[:, :HD] + g_kw2[:, HD:]

    dxbc, dz, ddtr, g_dtb, g_alog, g_dsk, g_ssd_norm = _ssd_bwd(
        xbc, proj, o_dt // LANES, dtb, alog, dsk, ssd_norm_w, hsave, dy_ssd, "ssd_bwd")
    dxbc_pre, g_conv_w, g_conv_b = _conv_bwd(proj, o_xbc, CC, conv_w_f, conv_b, dxbc, "conv_bwd")
    dproj = jnp.concatenate([dz.astype(BF16), dxbc_pre.astype(BF16), dq, dk, dv.astype(BF16), ddtr.astype(BF16)], axis=1)
    gw_proj = _matmul(h1, dproj, "tn", F32, "in_proj_dw", tn=896)
    gw_halves = gw_proj.reshape(2, D // 2, PW)
    sum_p = _pair_add(gw_halves, _pair_swap([gw_halves], "pair_swap_in")[0], ci, "pair_add_in")
    sum_in = jnp.concatenate([sum_p[:, :n_zx], sum_p[:, o_dt:o_dt + NH_SSD], sum_p[:, n_zx:o_dt]], axis=1)
    in_handle, in_token = _spread_start([sum_in.reshape(D // 2, 4, wsh).transpose(1, 0, 2)], True, False, dev, chip,
                                        "scatter_in_start")
    dh1 = _matmul(dproj, w_proj, "nt", F32, "in_proj_dx", tm=512, tk=PW)

    def norm1_bwd(r, k):
        x_, dh_, dres = r
        _, vjp = jax.vjp(_normmod, x_, *k)
        dx, dnw, dsc, dsh = vjp(dh_)
        return [dx + dres], [dnw, dsc, dsh]

    grad_x, g_norm1, dscale1, dshift1 = _rows("norm1_bwd", norm1_bwd, [(xs, 0, D), (dh1, 0, D), (dx2, 0, D)],
                                              [norm1_w + in_token[0, 0], scale1, shift1], [(D, F32)], [(1, D)] * 3, S)
    dmod =jnp.concatenate([dshift1, dscale1, dgate1, dshift2, dscale2, dgate2], axis=1)

    small = [g_norm1, g_norm2, dmod, g_conv_b, g_dtb, g_alog, g_dsk, g_ssd_norm, _pad_lanes(g_q_norm),
             _pad_lanes(g_k_norm), g_attn_norm, g_conv_w.reshape(1, KCONV * CC)]
    sizes = [t.shape[1] for t in small]
    packed = jnp.concatenate(small, axis=1)
    nrow = -(-packed.shape[1] // LANES // 8) * 8
    packed = jnp.pad(packed, ((0, 0), (0, nrow * LANES - packed.shape[1]))).reshape(nrow, LANES)
    packed_all = _exchange8([packed], False, "gather_small_grads")[0]
    tot = _sum_parts(packed_all, "sum_small_grads").reshape(1, nrow * LANES)
    offs = [sum(sizes[:i]) for i in range(len(sizes))]
    (g_norm1, g_norm2, g_b_ada, g_conv_b, g_dtb, g_alog, g_dsk, g_ssd_norm, g_q_norm, g_k_norm, g_attn_norm,
     g_conv_w) = [tot[:, o:o + n] for o, n in zip(offs, sizes)]
    g_dtb, g_alog, g_dsk = g_dtb[:, :NH_SSD], g_alog[:, :NH_SSD], g_dsk[:, :NH_SSD]
    g_q_norm, g_k_norm = g_q_norm[:, :HD], g_k_norm[:, :HD]
    ccs = CC // 4
    g_conv_w = lax.dynamic_slice_in_dim(g_conv_w.reshape(KCONV, CC), chip * ccs, ccs, 1)

    dmod_all = packed_all.reshape(8, nrow * LANES)[:, offs[2]:offs[2] + 6 * D]
    dmod_sh = jnp.pad(lax.dynamic_slice_in_dim(dmod_all, chip * nmod, nmod, 1), ((0, 8), (0, 0)))
    gw_ada = _mod_wgrad(c_all, dmod_sh, "mod_wgrad")

    parts_ff1, parts_ff2 = _spread_wait(ff_handle, True, True, in_token, "scatter_ff_wait")
    res_ff1 = _adamw(parts_ff1, w_ff1[0], m_w_ff1[0], v_w_ff1[0], "adamw_w_ff1")
    res_ff2 = _adamw(parts_ff2, w_ff2[0], m_w_ff2[0], v_w_ff2[0], "adamw_w_ff2")
    parts_out = _spread_wait(out_handle, True, True, in_token, "scatter_out_wait")[0]
    res_out = _adamw(parts_out, w_out[0], m_w_out[0], v_w_out[0], "adamw_w_out")
    res_ada = _adamw(gw_ada[None], w_ada[0], m_w_ada[0], v_w_ada[0], "adamw_w_ada")
    lands_in = _sibling_fill(_spread_wait(in_handle, True, False, res_ada[0], "scatter_in_wait"), "scatter_in_fill")[0]
    res_in = _adamw(lands_in.reshape(4, D, wsh), w_in[0], m_w_in[0], v_w_in[0], "adamw_w_in")

    small_names = ["norm1_w", "norm2_w", "b_ada", "conv_w", "conv_b", "dt_bias", "a_log", "d_skip", "ssd_norm_w",
                   "q_norm_w", "k_norm_w", "attn_norm_w"]
    small_g = dict(norm1_w=g_norm1, norm2_w=g_norm2, b_ada=g_b_ada, conv_w=g_conv_w.reshape(1, KCONV * ccs),
                   conv_b=g_conv_b, dt_bias=g_dtb, a_log=g_alog, d_skip=g_dsk, ssd_norm_w=g_ssd_norm, q_norm_w=g_q_norm,
                   k_norm_w=g_k_norm, attn_norm_w=g_attn_norm)
    small_w = dict(norm1_w=(norm1_w, m_norm1_w, v_norm1_w), norm2_w=(norm2_w, m_norm2_w, v_norm2_w),
                   b_ada=(b_ada, m_b_ada, v_b_ada),
                   conv_w=tuple(t.reshape(1, KCONV * ccs) for t in (conv_w, m_conv_w, v_conv_w)),
                   conv_b=(conv_b, m_conv_b, v_conv_b), dt_bias=(dt_bias, m_dt_bias, v_dt_bias),
                   a_log=(a_log, m_a_log, v_a_log), d_skip=(d_skip, m_d_skip, v_d_skip),
                   ssd_norm_w=(ssd_norm_w, m_ssd_norm_w, v_ssd_norm_w), q_norm_w=(q_norm_w, m_q_norm_w, v_q_norm_w),
                   k_norm_w=(k_norm_w, m_k_norm_w, v_k_norm_w), attn_norm_w=(attn_norm_w, m_attn_norm_w, v_attn_norm_w))
    ssz = [_pad_lanes(small_g[n]).shape[1] for n in small_names]
    soff = [sum(ssz[:i]) for i in range(len(ssz))]
    srow = -(-sum(ssz) // LANES // 8) * 8

    def pack(ts, fill):
        t = jnp.concatenate([jnp.pad(t, ((0, 0), (0, (-t.shape[1]) % LANES)), constant_values=fill) for t in ts], axis=1)
        return jnp.pad(t, ((0, 0), (0, srow * LANES - t.shape[1])), constant_values=fill).reshape(srow, LANES)

    sg = pack([small_g[n] for n in small_names], 0.0)
    sw = pack([small_w[n][0] for n in small_names], 0.0)
    sm_ = pack([small_w[n][1] for n in small_names], 0.0)
    sv = pack([small_w[n][2] for n in small_names], 1.0)
    _, s_delta, s_m, s_v = _adamw(sg[None], sw, sm_, sv, "adamw_small", tm=srow)

    def unpack(t, n):
        i = small_names.index(n)
        return t.reshape(1, srow * LANES)[:, soff[i]:soff[i] + small_g[n].shape[1]].reshape(small_w[n][0].shape)

    loss = lax.psum(loss_p[0, 0], ("x", "y", "c"))
    big_res = dict(w_ada=res_ada, w_in=res_in, w_out=res_out, w_ff1=res_ff1, w_ff2=res_ff2)
    order = ["norm1_w", "norm2_w", "w_ada", "b_ada", "w_in", "conv_w", "conv_b", "dt_bias", "a_log", "d_skip",
             "ssd_norm_w", "q_norm_w", "k_norm_w", "attn_norm_w", "w_out", "w_ff1", "w_ff2"]
    grads, deltas, new_m, new_v = [], [], [], []
    for n in order:
        if n in big_res:
            g_, d_, m_, v_ = [t[None] for t in big_res[n]]
        else:
            g_ = small_g[n].reshape(small_w[n][0].shape)
            d_, m_, v_ = unpack(s_delta, n), unpack(s_m, n), unpack(s_v, n)
            if n == "conv_w":
                g_, d_, m_, v_ = [t.reshape(conv_w.shape) for t in (g_, d_, m_, v_)]
        grads.append(g_)
        deltas.append(d_)
        new_m.append(m_)
        new_v.append(v_)
    return (loss, grad_x[None], *grads, *deltas, *new_m, *new_v)
```

```python
import functools

import jax
import jax.numpy as jnp
from jax import lax
from jax.experimental import pallas as pl
from jax.experimental.pallas import tpu as pltpu

F32, BF16 = jnp.float32, jnp.bfloat16
EPS = 1e-6
HD = 64
NH_SSD = 16
NG = 4
NSTATE = 128
KCONV = 4
CHUNK = 128
NH_ATT = 16
PATTERNS = ((128, 1), (512, 4), (2048, 16))
ABLK = 128
LANES = 128
ADAM_LR, ADAM_B1, ADAM_B2, ADAM_EPS, ADAM_WD, ADAM_STEP = 0.001, 0.9, 0.999, 1e-08, 0.01, 10
VMEM_LIMIT = 56 * 1024 * 1024
MESH = pl.DeviceIdType.MESH
NEG = -1e30

_DN = {"nn": (((1,), (0,)), ((), ())), "nt": (((1,), (1,)), ((), ())), "tn": (((0,), (0,)), ((), ()))}


def _cparams(sem):
    return pltpu.CompilerParams(dimension_semantics=sem, vmem_limit_bytes=VMEM_LIMIT)


def _tile(n, cap):
    if n % LANES or n <= LANES:
        return n
    best = LANES
    for t in range(LANES, min(n, cap) + 1, LANES):
        if n % t == 0:
            best = t
    return best


def _silu(x):
    return x / (1.0 + jnp.exp(-x))


def _softplus(x):
    return jnp.maximum(x, 0.0) + jnp.log(1.0 + jnp.exp(-jnp.abs(x)))


def _dot(a, b, dims):
    return lax.dot_general(a.astype(BF16), b.astype(BF16), _DN[dims], preferred_element_type=F32)


def _matmul(a, b, dims, out_dtype, name, a_fn=None, epilogue=None, extras=(), tm=1024, tn=1024, tk=2048,
            chip_of_tile=None):
    if dims == "nn":
        (M, K), (_, N) = a.shape, b.shape
    elif dims == "nt":
        (M, K), (N, _) = a.shape, b.shape
    else:
        (K, M), (_, N) = a.shape, b.shape
    tm, tn, tk = _tile(M, tm), _tile(N, tn), _tile(K, tk)
    nk = K // tk
    ne = len(extras)

    def body(a_ref, b_ref, *rest):
        e_refs, o_ref = rest[:ne], rest[ne]
        av = a_ref[...]
        if a_fn is not None:
            av = a_fn(av)
        part = _dot(av, b_ref[...], dims)

        def finish(r):
            if epilogue is not None:
                r = epilogue(r, *[e[...] for e in e_refs])
            o_ref[...] = r.astype(out_dtype).reshape(o_ref.shape)

        if nk == 1:
            finish(part)
            return
        acc = rest[ne + 1]
        k = pl.program_id(2)

        @pl.when(k == 0)
        def _():
            acc[...] = part

        @pl.when(k > 0)
        def _():
            acc[...] += part

        @pl.when(k == nk - 1)
        def _():
            finish(acc[...])

    a_spec = pl.BlockSpec((tk, tm), lambda i, j, k: (k, i)) if dims == "tn" else pl.BlockSpec((tm, tk), lambda i, j, k: (i, k))
    b_spec = pl.BlockSpec((tn, tk), lambda i, j, k: (j, k)) if dims == "nt" else pl.BlockSpec((tk, tn), lambda i, j, k: (k, j))
    o_spec = pl.BlockSpec((tm, tn), lambda i, j, k: (i, j))
    out_spec, out_dims = o_spec, (M, N)
    if chip_of_tile is not None:
        assert (M // tm) * (N // tn) == 4
        out_spec = pl.BlockSpec((None, tm, tn), lambda i, j, k: (chip_of_tile(i, j), 0, 0))
        out_dims = (4, tm, tn)
    return pl.pallas_call(
        body, name=name, grid=(M // tm, N // tn, nk),
        in_specs=[a_spec, b_spec] + [o_spec] * ne, out_specs=out_spec,
        out_shape=jax.ShapeDtypeStruct(out_dims, out_dtype),
        scratch_shapes=[pltpu.VMEM((tm, tn), F32)] if nk > 1 else [],
        compiler_params=_cparams(("parallel", "parallel", "arbitrary")),
    )(a, b, *extras)


def _rows(name, fn, rows, consts, outs, accs, n_rows, tm=256):
    tm = min(tm, n_rows)
    nr, nc, no, na = len(rows), len(consts), len(outs), len(accs)

    def body(*refs):
        r_refs, c_refs = refs[:nr], refs[nr:nr + nc]
        o_refs, a_refs = refs[nr + nc:nr + nc + no], refs[nr + nc + no:]
        o_vals, a_vals = fn([r[...] for r in r_refs], [c[...] for c in c_refs])
        for ref, val in zip(o_refs, o_vals):
            ref[...] = val.astype(ref.dtype)
        if na:
            @pl.when(pl.program_id(0) == 0)
            def _():
                for ref in a_refs:
                    ref[...] = jnp.zeros_like(ref)
            for ref, val in zip(a_refs, a_vals):
                ref[...] += val

    in_specs = [pl.BlockSpec((tm, w), lambda i, cb=cb: (i, cb)) for (_, cb, w) in rows]
    in_specs += [pl.BlockSpec(cst.shape, lambda i, nd=cst.ndim: (0,) * nd) for cst in consts]
    out_specs = [pl.BlockSpec((tm, w), lambda i: (i, 0)) for (w, _) in outs]
    out_specs += [pl.BlockSpec(s, lambda i: (0, 0)) for s in accs]
    out_shape = [jax.ShapeDtypeStruct((n_rows, w), dt) for (w, dt) in outs]
    out_shape += [jax.ShapeDtypeStruct(s, F32) for s in accs]
    res = pl.pallas_call(
        body, name=name, grid=(n_rows // tm,), in_specs=in_specs, out_specs=out_specs, out_shape=out_shape,
        compiler_params=_cparams(("arbitrary",)),
    )(*[r[0] for r in rows], *consts)
    return res


def _normmod(x, nw, sc, sh):
    r = lax.rsqrt(jnp.mean(x * x, axis=-1, keepdims=True) + EPS)
    return (x * r) * nw * (1.0 + sc) + sh


def _resid_normmod(x, mix, g, nw, sc, sh):
    x2 = x + g * mix
    return x2, _normmod(x2, nw, sc, sh)


def _rmsw(o, w):
    return o * lax.rsqrt(jnp.mean(o * o, axis=-1, keepdims=True) + EPS) * w


def _lane_mask():
    return lax.broadcasted_iota(jnp.int32, (1, LANES), 1) < HD


def _headnorm(t, w, scale):
    lo = _lane_mask()
    t2 = t * t
    s0 = jnp.sum(jnp.where(lo, t2, 0.0), axis=1, keepdims=True)
    s1 = jnp.sum(jnp.where(lo, 0.0, t2), axis=1, keepdims=True)
    ms = jnp.where(lo, s0, s1) * (1.0 / HD)
    return t * lax.rsqrt(ms + EPS) * w * scale


CONV_ROWS = 128
CONV_HALO = 8


def _conv_cols(n_ch):
    return _tile(n_ch, LANES)


def _conv_fwd(proj, col0, n_ch, conv_w, conv_b, name):
    S = proj.shape[0]
    tc = _conv_cols(n_ch)

    R, H = CONV_ROWS, CONV_HALO

    def body(u_ref, w_ref, b_ref, o_ref):
        w = [w_ref[i:i + 1, :] for i in range(KCONV)]
        b = b_ref[...]

        def chunk(ext):
            acc = b + w[KCONV - 1] * ext[H:]
            for i in range(KCONV - 1):
                acc = acc + w[i] * pltpu.roll(ext, KCONV - 1 - i, 0)[H:]
            return _silu(acc)

        o_ref[0:R, :] = chunk(jnp.concatenate([jnp.zeros((H, tc), F32), u_ref[0:R, :]], axis=0))

        def step(c, carry):
            r0 = pl.multiple_of(c * R, R)
            o_ref[pl.ds(r0, R), :] = chunk(u_ref[pl.ds(pl.multiple_of(r0 - H, H), R + H), :])
            return carry

        lax.fori_loop(1, S // R, step, 0)

    return pl.pallas_call(
        body, name=name, grid=(n_ch // tc,),
        in_specs=[pl.BlockSpec((S, tc), lambda j: (0, j + col0 // tc)),
                  pl.BlockSpec((KCONV, tc), lambda j: (0, j)), pl.BlockSpec((1, tc), lambda j: (0, j))],
        out_specs=pl.BlockSpec((S, tc), lambda j: (0, j)),
        out_shape=jax.ShapeDtypeStruct((S, n_ch), F32),
        compiler_params=_cparams(("parallel",)),
    )(proj, conv_w, conv_b)


def _conv_bwd(proj, col0, n_ch, conv_w, conv_b, dxbc, name):
    S = proj.shape[0]
    tc = _conv_cols(n_ch)

    R, H = CONV_ROWS, CONV_HALO

    def body(u_ref, w_ref, b_ref, g_ref, du_ref, dw_ref, db_ref):
        w = [w_ref[i:i + 1, :] for i in range(KCONV)]
        b = b_ref[...]
        pad = jnp.zeros((H, tc), F32)

        def chunk(u_ext, g_ext):
            taps = [pltpu.roll(u_ext, KCONV - 1 - i, 0)[H:] for i in range(KCONV - 1)] + [u_ext[H:]]
            acc = b
            for i in range(KCONV):
                acc = acc + w[i] * taps[i]
            sig = 1.0 / (1.0 + jnp.exp(-acc))
            dacc = g_ext * (sig * (1.0 + acc * (1.0 - sig)))
            du = w[KCONV - 1] * dacc[:R]
            for i in range(KCONV - 1):
                du = du + w[i] * pltpu.roll(dacc, R + H - (KCONV - 1 - i), 0)[:R]
            d = dacc[:R]
            return du, [jnp.sum(d * t[:R], axis=0, keepdims=True) for t in taps], jnp.sum(d, axis=0, keepdims=True)

        du, dws, db = chunk(jnp.concatenate([pad, u_ref[0:R + H, :]], axis=0), g_ref[0:R + H, :])
        du_ref[0:R, :] = du

        def step(c, carry):
            r0 = pl.multiple_of(c * R, R)
            du_c, dws_c, db_c = chunk(u_ref[pl.ds(pl.multiple_of(r0 - H, H), R + 2 * H), :], g_ref[pl.ds(r0, R + H), :])
            du_ref[pl.ds(r0, R), :] = du_c
            return [a + b_ for a, b_ in zip(carry[0], dws_c)], carry[1] + db_c

        dws, db = lax.fori_loop(1, S // R - 1, step, (dws, db))
        du, dws_l, db_l = chunk(jnp.concatenate([u_ref[S - R - H:S, :], pad], axis=0),
                                jnp.concatenate([g_ref[S - R:S, :], pad], axis=0))
        du_ref[S - R:S, :] = du
        for i in range(KCONV):
            dw_ref[i:i + 1, :] = dws[i] + dws_l[i]
        db_ref[...] = db + db_l

    return pl.pallas_call(
        body, name=name, grid=(n_ch // tc,),
        in_specs=[pl.BlockSpec((S, tc), lambda j: (0, j + col0 // tc)),
                  pl.BlockSpec((KCONV, tc), lambda j: (0, j)), pl.BlockSpec((1, tc), lambda j: (0, j)),
                  pl.BlockSpec((S, tc), lambda j: (0, j))],
        out_specs=[pl.BlockSpec((S, tc), lambda j: (0, j)), pl.BlockSpec((KCONV, tc), lambda j: (0, j)),
                   pl.BlockSpec((1, tc), lambda j: (0, j))],
        out_shape=[jax.ShapeDtypeStruct((S, n_ch), F32), jax.ShapeDtypeStruct((KCONV, n_ch), F32),
                   jax.ShapeDtypeStruct((1, n_ch), F32)],
        compiler_params=_cparams(("parallel",)),
    )(proj, conv_w, conv_b, dxbc)


@functools.partial(jax.custom_vjp, nondiff_argnums=(2,))
def _mm(a, b, dims):
    return _dot(a, b, dims)


def _mm_fwd(a, b, dims):
    return _dot(a, b, dims), (a, b)


def _mm_bwd(dims, res, g):
    a, b = res
    if dims == "nn":
        return _dot(g, b, "nt"), _dot(a, g, "tn")
    if dims == "nt":
        return _dot(g, b, "nn"), _dot(g, a, "tn")
    return _dot(b, g, "nt"), _dot(a, g, "nn")


_mm.defvjp(_mm_fwd, _mm_bwd)


def _tri_dot(x, upper):
    n = x.shape[0]
    r = lax.broadcasted_iota(jnp.int32, (n, n), 0)
    c = lax.broadcasted_iota(jnp.int32, (n, n), 1)
    t = jnp.where((r <= c) if upper else (r >= c), 1.0, 0.0)
    return lax.dot_general(t, x, _DN["nn"], precision=lax.Precision.HIGHEST, preferred_element_type=F32)


@jax.custom_vjp
def _cumsum_rows(x):
    return _tri_dot(x, False)


_cumsum_rows.defvjp(lambda x: (_tri_dot(x, False), None), lambda _, g: (_tri_dot(g, True),))


def _ssd_chunk(xs_p, bm_g, cm_g, dtr, z_p, dtb, alog, dsk, nw_p, h_p):
    L = dtr.shape[0]
    n_pairs = len(xs_p)
    ppg = n_pairs // len(bm_g)
    lane = lax.broadcasted_iota(jnp.int32, (1, LANES), 1)
    sub = lax.broadcasted_iota(jnp.int32, (LANES, 1), 0)
    lo = lane < HD
    row_l = lax.broadcasted_iota(jnp.int32, (L, 1), 0)
    tri = lax.broadcasted_iota(jnp.int32, (L, L), 0) >= lax.broadcasted_iota(jnp.int32, (L, L), 1)

    dt = _softplus(dtr + dtb)
    acs = _cumsum_rows(dt * (-jnp.exp(alog)))
    acs_t = acs.T
    a_last = jnp.sum(jnp.where(row_l == L - 1, acs, 0.0), axis=0, keepdims=True)
    e_acs = jnp.exp(acs)
    dec = jnp.exp(a_last - acs)
    cdec = jnp.exp(a_last)

    def colv(m, h):
        return jnp.sum(jnp.where(lane == h, m, 0.0), axis=1, keepdims=True)

    def rowv(mt, h):
        return jnp.sum(jnp.where(sub == h, mt, 0.0), axis=0, keepdims=True)

    def pair(m, h0):
        return jnp.where(lo, colv(m, h0), colv(m, h0 + 1))

    ys, hs = [], []
    cb = None
    for p in range(n_pairs):
        g, h0 = p // ppg, 2 * p
        bmat, cmat = bm_g[g], cm_g[g]
        if p % ppg == 0:
            cb = _mm(cmat, bmat, "nt")
        x = xs_p[p]
        xdt = x * pair(dt, h0)
        yd = []
        for h in (h0, h0 + 1):
            seg = colv(acs, h) - rowv(acs_t, h)
            lm = jnp.where(tri, jnp.exp(jnp.where(tri, seg, 0.0)), 0.0)
            yd.append(_mm(cb * lm, xdt, "nn"))
        y = jnp.where(lo, yd[0], yd[1])
        y = y + _mm(cmat, h_p[p], "nt") * pair(e_acs, h0)
        st = _mm(xdt * pair(dec, h0), bmat, "tn")
        cd_col = jnp.where(sub < HD, colv(cdec, h0), colv(cdec, h0 + 1))
        hs.append(h_p[p] * cd_col + st)
        ys.append(y + pair(dsk, h0) * x)

    y2 = [ys[p] * _silu(z_p[p]) for p in range(n_pairs)]
    outs = []
    for g in range(len(bm_g)):
        ps = range(g * ppg, (g + 1) * ppg)
        ss = sum(jnp.sum(y2[p] * y2[p], axis=1, keepdims=True) for p in ps)
        rs = lax.rsqrt(ss * (1.0 / (ppg * LANES)) + EPS)
        outs += [y2[p] * rs * nw_p[p] for p in ps]
    return outs, hs


def _ssd_slices(xbc_ref, z_ref, nw_ref, di):
    n_pairs = di // LANES
    xs_p = [xbc_ref[:, p * LANES:(p + 1) * LANES] for p in range(n_pairs)]
    bm_g = [xbc_ref[:, di + g * NSTATE:di + (g + 1) * NSTATE] for g in range(NG)]
    cm_g = [xbc_ref[:, di + (NG + g) * NSTATE:di + (NG + g + 1) * NSTATE] for g in range(NG)]
    z_p = [z_ref[:, p * LANES:(p + 1) * LANES] for p in range(n_pairs)]
    nw_p = [nw_ref[:, p * LANES:(p + 1) * LANES] for p in range(n_pairs)]
    return xs_p, bm_g, cm_g, z_p, nw_p


def _ssd_fwd(xbc, proj, dt_cb, dtb, alog, dsk, nw, name):
    S, cc = xbc.shape
    di = NH_SSD * HD
    n_pairs = di // LANES
    nchunk = S // CHUNK

    def body(xbc_ref, z_ref, dtr_ref, dtb_ref, alog_ref, dsk_ref, nw_ref, y_ref, hs_ref, h_scr):
        @pl.when(pl.program_id(0) == 0)
        def _():
            h_scr[...] = jnp.zeros_like(h_scr)

        xs_p, bm_g, cm_g, z_p, nw_p = _ssd_slices(xbc_ref, z_ref, nw_ref, di)
        h_p = [h_scr[p * LANES:(p + 1) * LANES, :] for p in range(n_pairs)]
        hs_ref[...] = h_scr[...]
        outs, hs = _ssd_chunk(xs_p, bm_g, cm_g, dtr_ref[...], z_p, dtb_ref[...], alog_ref[...], dsk_ref[...], nw_p, h_p)
        for p in range(n_pairs):
            y_ref[:, p * LANES:(p + 1) * LANES] = outs[p].astype(y_ref.dtype)
            h_scr[p * LANES:(p + 1) * LANES, :] = hs[p]

    vec = pl.BlockSpec((1, LANES), lambda c: (0, 0))
    return pl.pallas_call(
        body, name=name, grid=(nchunk,),
        in_specs=[pl.BlockSpec((CHUNK, cc), lambda c: (c, 0)), pl.BlockSpec((CHUNK, di), lambda c: (c, 0)),
                  pl.BlockSpec((CHUNK, LANES), lambda c: (c, dt_cb)), vec, vec, vec,
                  pl.BlockSpec((1, di), lambda c: (0, 0))],
        out_specs=[pl.BlockSpec((CHUNK, di), lambda c: (c, 0)), pl.BlockSpec((None, di, NSTATE), lambda c: (c, 0, 0))],
        out_shape=[jax.ShapeDtypeStruct((S, di), BF16), jax.ShapeDtypeStruct((nchunk, di, NSTATE), F32)],
        scratch_shapes=[pltpu.VMEM((di, NSTATE), F32)],
        compiler_params=_cparams(("arbitrary",)),
    )(xbc, proj, proj, dtb, alog, dsk, nw)


def _ssd_bwd(xbc, proj, dt_cb, dtb, alog, dsk, nw, hsave, dy, name):
    S, cc = xbc.shape
    di = NH_SSD * HD
    n_pairs = di // LANES
    nchunk = S // CHUNK

    def body(xbc_ref, z_ref, dtr_ref, dtb_ref, alog_ref, dsk_ref, nw_ref, hs_ref, dy_ref,
             dxbc_ref, dz_ref, ddtr_ref, ddtb_ref, dalog_ref, ddsk_ref, dnw_ref, dh_scr):
        @pl.when(pl.program_id(0) == 0)
        def _():
            dh_scr[...] = jnp.zeros_like(dh_scr)
            ddtb_ref[...] = jnp.zeros_like(ddtb_ref)
            dalog_ref[...] = jnp.zeros_like(dalog_ref)
            ddsk_ref[...] = jnp.zeros_like(ddsk_ref)
            dnw_ref[...] = jnp.zeros_like(dnw_ref)

        xs_p, bm_g, cm_g, z_p, nw_p = _ssd_slices(xbc_ref, z_ref, nw_ref, di)
        h_p = [hs_ref[p * LANES:(p + 1) * LANES, :] for p in range(n_pairs)]
        dy_p = [dy_ref[:, p * LANES:(p + 1) * LANES].astype(F32) for p in range(n_pairs)]
        dh_p = [dh_scr[p * LANES:(p + 1) * LANES, :] for p in range(n_pairs)]
        _, vjp = jax.vjp(_ssd_chunk, xs_p, bm_g, cm_g, dtr_ref[...], z_p, dtb_ref[...], alog_ref[...], dsk_ref[...],
                         nw_p, h_p)
        dxs, dbm, dcm, ddtr, dz, ddtb, dalog, ddsk, dnw, dh = vjp((dy_p, dh_p))
        for p in range(n_pairs):
            sl = slice(p * LANES, (p + 1) * LANES)
            dxbc_ref[:, sl] = dxs[p]
            dz_ref[:, sl] = dz[p]
            dnw_ref[:, sl] += dnw[p]
            dh_scr[sl, :] = dh[p]
        for g in range(NG):
            dxbc_ref[:, di + g * NSTATE:di + (g + 1) * NSTATE] = dbm[g]
            dxbc_ref[:, di + (NG + g) * NSTATE:di + (NG + g + 1) * NSTATE] = dcm[g]
        ddtr_ref[...] = ddtr
        ddtb_ref[...] += ddtb
        dalog_ref[...] += dalog
        ddsk_ref[...] += ddsk

    last = nchunk - 1
    vec = pl.BlockSpec((1, LANES), lambda c: (0, 0))
    return pl.pallas_call(
        body, name=name, grid=(nchunk,),
        in_specs=[pl.BlockSpec((CHUNK, cc), lambda c: (last - c, 0)), pl.BlockSpec((CHUNK, di), lambda c: (last - c, 0)),
                  pl.BlockSpec((CHUNK, LANES), lambda c: (last - c, dt_cb)), vec, vec, vec,
                  pl.BlockSpec((1, di), lambda c: (0, 0)),
                  pl.BlockSpec((None, di, NSTATE), lambda c: (last - c, 0, 0)),
                  pl.BlockSpec((CHUNK, di), lambda c: (last - c, 0))],
        out_specs=[pl.BlockSpec((CHUNK, cc), lambda c: (last - c, 0)), pl.BlockSpec((CHUNK, di), lambda c: (last - c, 0)),
                   pl.BlockSpec((CHUNK, LANES), lambda c: (last - c, 0)), vec, vec, vec,
                   pl.BlockSpec((1, di), lambda c: (0, 0))],
        out_shape=[jax.ShapeDtypeStruct((S, cc), F32), jax.ShapeDtypeStruct((S, di), F32),
                   jax.ShapeDtypeStruct((S, LANES), F32), jax.ShapeDtypeStruct((1, LANES), F32),
                   jax.ShapeDtypeStruct((1, LANES), F32), jax.ShapeDtypeStruct((1, LANES), F32),
                   jax.ShapeDtypeStruct((1, di), F32)],
        scratch_shapes=[pltpu.VMEM((di, NSTATE), F32)],
        compiler_params=_cparams(("arbitrary",)),
    )(xbc, proj, proj, dtb, alog, dsk, nw, hsave, dy)


def _band_masks(rows_q, rows_k):
    qi = lax.broadcasted_iota(jnp.int32, (rows_q, rows_k), 0)
    ki = lax.broadcasted_iota(jnp.int32, (rows_q, rows_k), 1)
    return qi, ki


def _class_chunks(n_rows, d):
    per_class = n_rows // d
    ch = min(per_class, 256)
    out = []
    for r in range(d):
        for c0 in range(0, per_class, ch):
            tok = pl.ds(c0, ch) if d == 1 else pl.ds(r + d * c0, ch, stride=d)
            out.append((tok, pl.ds(r * per_class + c0, ch)))
    return out


def _to_class_order(src_ref, dst_ref, n_rows, d):
    for tok, cls in _class_chunks(n_rows, d):
        dst_ref[cls, :] = src_ref[tok, :].astype(dst_ref.dtype)


def _blk_rows(t):
    return pl.ds(pl.multiple_of(t * ABLK, ABLK), ABLK)


def _head_lanes(msk, t, t_rolled):
    return jnp.where(msk, t, t_rolled)


def _zero_unless(msk, t):
    return jnp.where(msk, t, jnp.zeros_like(t))


def _attn_fwd(qn, kn, proj, v_cb, name):
    S, ad = qn.shape
    nb = S // ABLK
    nbr = len(PATTERNS)

    def body(q_ref, k_ref, v_ref, o_ref, lse_ref, qc, kc, vc, ob, mb, lb, m_s, l_s):
        lo = _lane_mask()
        qi, ki = _band_masks(ABLK, ABLK)
        cur_ok, prev_ok = ki <= qi, ki >= qi
        for bi, (_, d) in enumerate(PATTERNS):
            nbc = S // d // ABLK
            first, last = bi == 0, bi == nbr - 1
            qs, ks, vs = q_ref, k_ref, v_ref
            if d > 1:
                qs, ks, vs = qc, kc, vc
                for src, dst in ((q_ref, qc), (k_ref, kc), (v_ref, vc)):
                    _to_class_order(src, dst, S, d)
            o_dst, m_dst, l_dst = (o_ref, m_s, l_s) if first else (ob, mb, lb)

            def blk(t, carry, nbc=nbc, qs=qs, ks=ks, vs=vs, o_dst=o_dst, m_dst=m_dst, l_dst=l_dst):
                rows, prow = _blk_rows(t), _blk_rows(jnp.maximum(t - 1, 0))
                has_prev = (t % nbc) != 0
                qv = qs[rows, :]
                q2 = jnp.concatenate([_zero_unless(lo, qv), _zero_unless(jnp.logical_not(lo), qv)], axis=0).astype(BF16)
                ok_c = jnp.concatenate([cur_ok, cur_ok], axis=0)
                ok_p = jnp.concatenate([prev_ok, prev_ok], axis=0) & has_prev
                s_c = jnp.where(ok_c, _dot(q2, ks[rows, :], "nt"), NEG)
                s_p = jnp.where(ok_p, _dot(q2, ks[prow, :], "nt"), NEG)
                m = jnp.max(jnp.maximum(s_c, s_p), axis=1, keepdims=True)
                p_c, p_p = jnp.exp(s_c - m), jnp.exp(s_p - m)
                l = jnp.sum(p_c + p_p, axis=1, keepdims=True)
                o2 = _dot(p_c, vs[rows, :], "nn") + _dot(p_p, vs[prow, :], "nn")
                o_dst[rows, :] = jnp.where(lo, o2[:ABLK], o2[ABLK:])
                m_dst[rows, :] = jnp.where(lo, m[:ABLK], m[ABLK:])
                l_dst[rows, :] = jnp.where(lo, l[:ABLK], l[ABLK:])
                return carry

            lax.fori_loop(0, nb, blk, 0, unroll=8)
            if first:
                continue
            for tok, cls in _class_chunks(S, d):
                m_old, m_b = m_s[tok, :], mb[cls, :]
                m_new = jnp.maximum(m_old, m_b)
                a, b = jnp.exp(m_old - m_new), jnp.exp(m_b - m_new)
                l_new = a * l_s[tok, :] + b * lb[cls, :]
                o_new = a * o_ref[tok, :] + b * ob[cls, :]
                if last:
                    o_ref[tok, :] = o_new / l_new
                    lse_ref[tok, :] = m_new + jnp.log(l_new)
                else:
                    o_ref[tok, :] = o_new
                    m_s[tok, :] = m_new
                    l_s[tok, :] = l_new

    col = pl.BlockSpec((S, LANES), lambda h: (0, h))
    return pl.pallas_call(
        body, name=name, grid=(ad // LANES,),
        in_specs=[col, col, pl.BlockSpec((S, LANES), lambda h: (0, h + v_cb))], out_specs=[col, col],
        out_shape=[jax.ShapeDtypeStruct((S, ad), F32), jax.ShapeDtypeStruct((S, ad), F32)],
        scratch_shapes=[pltpu.VMEM((S, LANES), BF16)] * 3 + [pltpu.VMEM((S, LANES), F32)] * 5,
        compiler_params=_cparams(("parallel",)),
    )(qn, kn, proj)


def _attn_bwd(qn, kn, proj, v_cb, do, lse, dd, name):
    S, ad = qn.shape
    nb = S // ABLK

    def body(q_ref, k_ref, v_ref, do_ref, lse_ref, dd_ref, dq_ref, dk_ref, dv_ref,
             qc, kc, vc, doc, lsec, ddc, dqc, dkc, dvc):
        lo = _lane_mask()
        qi, ki = _band_masks(ABLK, ABLK)
        cur_ok, prev_ok = ki <= qi, ki >= qi
        for bi, (_, d) in enumerate(PATTERNS):
            nbc = S // d // ABLK
            first = bi == 0
            token_order = (q_ref, k_ref, v_ref, do_ref, lse_ref, dd_ref)
            class_order = (qc, kc, vc, doc, lsec, ddc)
            if d > 1:
                for src, dst in zip(token_order, class_order):
                    _to_class_order(src, dst, S, d)
            qs, ks, vs, dos, lses, dds = class_order if d > 1 else token_order
            dq_dst, dk_dst, dv_dst = (dq_ref, dk_ref, dv_ref) if first else (dqc, dkc, dvc)
            dk_dst[...] = jnp.zeros_like(dk_dst)
            dv_dst[...] = jnp.zeros_like(dv_dst)

            def blk(t, carry, nbc=nbc, qs=qs, ks=ks, vs=vs, dos=dos, lses=lses, dds=dds,
                    dq_dst=dq_dst, dk_dst=dk_dst, dv_dst=dv_dst):
                rows, prow = _blk_rows(t), _blk_rows(jnp.maximum(t - 1, 0))
                has_prev = (t % nbc) != 0
                qv, dov, lse_b, dd_b = qs[rows, :], dos[rows, :], lses[rows, :], dds[rows, :]
                lse_r, dd_r = pltpu.roll(lse_b, HD, 1), pltpu.roll(dd_b, HD, 1)
                nlo = jnp.logical_not(lo)
                q2 = jnp.concatenate([_zero_unless(lo, qv), _zero_unless(nlo, qv)], axis=0).astype(BF16)
                do2 = jnp.concatenate([_zero_unless(lo, dov), _zero_unless(nlo, dov)], axis=0).astype(BF16)
                lse2 = jnp.concatenate([_head_lanes(lo, lse_b, lse_r), _head_lanes(nlo, lse_b, lse_r)], axis=0)
                dd2 = jnp.concatenate([_head_lanes(lo, dd_b, dd_r), _head_lanes(nlo, dd_b, dd_r)], axis=0)
                dq2 = None
                for krows, vmask in ((rows, cur_ok), (prow, prev_ok & has_prev)):
                    kv, vv = ks[krows, :], vs[krows, :]
                    vmask2 = jnp.concatenate([vmask, vmask], axis=0)
                    s = jnp.where(vmask2, _dot(q2, kv, "nt"), NEG)
                    p = jnp.exp(s - lse2)
                    ds = p * (_dot(do2, vv, "nt") - dd2)
                    dqk = _dot(ds, kv, "nn")
                    dq2 = dqk if dq2 is None else dq2 + dqk
                    dv_dst[krows, :] += _dot(p, do2, "tn")
                    dk_dst[krows, :] += _dot(ds, q2, "tn")
                dq_dst[rows, :] = jnp.where(lo, dq2[:ABLK], dq2[ABLK:])
                return carry

            lax.fori_loop(0, nb, blk, 0, unroll=4)
            if not first:
                for tok, cls in _class_chunks(S, d):
                    dq_ref[tok, :] = dq_ref[tok, :] + dqc[cls, :]
                    dk_ref[tok, :] = dk_ref[tok, :] + dkc[cls, :]
                    dv_ref[tok, :] = dv_ref[tok, :] + dvc[cls, :]

    col = pl.BlockSpec((S, LANES), lambda h: (0, h))
    col1 = pl.BlockSpec((S, LANES), lambda h: (0, h), pipeline_mode=pl.Buffered(1))
    vcol1 = pl.BlockSpec((S, LANES), lambda h: (0, h + v_cb), pipeline_mode=pl.Buffered(1))
    return pl.pallas_call(
        body, name=name, grid=(ad // LANES,),
        in_specs=[col, col, vcol1, col1, col1, col1], out_specs=[col, col, col],
        out_shape=[jax.ShapeDtypeStruct((S, ad), F32)] * 3,
        scratch_shapes=[pltpu.VMEM((S, LANES), BF16)] * 4 + [pltpu.VMEM((S, LANES), F32)] * 5,
        compiler_params=_cparams(("parallel",)),
    )(qn, kn, proj, do, lse, dd)


def _coords():
    return lax.axis_index("x"), lax.axis_index("y"), lax.axis_index("c")


def _exchange8(xs, per_dest, name):
    n = len(xs)
    blk = [x.shape[1:] if per_dest else x.shape for x in xs]

    def body(*refs):
        ins, outs = refs[:n], refs[n:2 * n]
        send_sems, recv_sems, local_sems = refs[2 * n:]
        x, y, c = _coords()
        sibling = (x, y, 1 - c)
        chips = [(1 - x, y), (x, 1 - y), (1 - x, 1 - y)]
        first, passed, mine = [], [], []
        for a in range(n):
            def src_for(cx, cy, a=a):
                return ins[a].at[2 * cx + cy] if per_dest else ins[a]

            def slot(px, py, pc, a=a):
                return outs[a].at[4 * px + 2 * py + pc]

            def copy(k, src, dst, to, a=a):
                return pltpu.make_async_remote_copy(src_ref=src, dst_ref=dst, send_sem=send_sems.at[7 * a + k],
                                                    recv_sem=recv_sems.at[7 * a + k], device_id=to, device_id_type=MESH)

            m = pltpu.make_async_copy(src_for(x, y), slot(x, y, c), local_sems.at[a])
            m.start()
            mine.append(m)
            cps = [copy(0, src_for(x, y), slot(x, y, c), sibling)]
            cps += [copy(1 + j, src_for(*chip), slot(x, y, c), (*chip, c)) for j, chip in enumerate(chips)]
            for cp in cps:
                cp.start()
            first += cps
        for a in range(n):
            def slot(px, py, pc, a=a):
                return outs[a].at[4 * px + 2 * py + pc]

            def copy(k, src, dst, to, a=a):
                return pltpu.make_async_remote_copy(src_ref=src, dst_ref=dst, send_sem=send_sems.at[7 * a + k],
                                                    recv_sem=recv_sems.at[7 * a + k], device_id=to, device_id_type=MESH)

            for j, chip in enumerate(chips):
                copy(1 + j, slot(*chip, c), slot(*chip, c), (*chip, c)).wait_recv()
                fw = copy(4 + j, slot(*chip, c), slot(*chip, c), sibling)
                fw.start()
                passed.append(fw)
        for a in range(n):
            def slot(px, py, pc, a=a):
                return outs[a].at[4 * px + 2 * py + pc]

            def copy(k, src, dst, to, a=a):
                return pltpu.make_async_remote_copy(src_ref=src, dst_ref=dst, send_sem=send_sems.at[7 * a + k],
                                                    recv_sem=recv_sems.at[7 * a + k], device_id=to, device_id_type=MESH)

            copy(0, slot(x, y, 1 - c), slot(x, y, 1 - c), sibling).wait_recv()
            for j, chip in enumerate(chips):
                copy(4 + j, slot(*chip, 1 - c), slot(*chip, 1 - c), sibling).wait_recv()
        for cp in first + passed:
            cp.wait_send()
        for m in mine:
            m.wait()

    anyspec = pl.BlockSpec(memory_space=pl.ANY)
    res = pl.pallas_call(
        body, name=name, in_specs=[anyspec] * n, out_specs=[anyspec] * n,
        out_shape=[jax.ShapeDtypeStruct((8,) + tuple(b), x.dtype) for b, x in zip(blk, xs)],
        scratch_shapes=[pltpu.SemaphoreType.DMA((7 * n,)), pltpu.SemaphoreType.DMA((7 * n,)),
                        pltpu.SemaphoreType.DMA((n,))],
    )(*xs)
    return list(res)


def _pair_swap(xs, name):
    n = len(xs)

    def body(*refs):
        ins, outs = refs[:n], refs[n:2 * n]
        send_sems, recv_sems = refs[2 * n:]
        x, y, c = _coords()
        cps = [pltpu.make_async_remote_copy(src_ref=ins[a].at[1 - c], dst_ref=outs[a], send_sem=send_sems.at[a],
                                            recv_sem=recv_sems.at[a], device_id=(x, y, 1 - c), device_id_type=MESH)
               for a in range(n)]
        for cp in cps:
            cp.start()
        for cp in cps:
            cp.wait()

    anyspec = pl.BlockSpec(memory_space=pl.ANY)
    res = pl.pallas_call(
        body, name=name, in_specs=[anyspec] * n, out_specs=[anyspec] * n,
        out_shape=[jax.ShapeDtypeStruct(x.shape[1:], x.dtype) for x in xs],
        scratch_shapes=[pltpu.SemaphoreType.DMA((n,)), pltpu.SemaphoreType.DMA((n,))],
    )(*xs)
    return list(res)


_HBM = pl.BlockSpec(memory_space=pltpu.HBM)
_SEM = pl.BlockSpec(memory_space=pltpu.SEMAPHORE)
_EFFECT = pltpu.SideEffectType.DATAFLOW_SIDE_EFFECTING


def _n_peers(both):
    return 7 if both else 3


def _peer(x, y, c, j, both):
    bits = j + 1 if both else 2 * (j + 1)
    dx, dy, dc = bits >> 2 & 1, bits >> 1 & 1, bits & 1
    return (1 - x if dx else x, 1 - y if dy else y, 1 - c if dc else c)


def _spread_copies(s_refs, l_refs, send_sems, recv_sems, per_dest, both):
    x, y, c = _coords()
    me = 4 * x + 2 * y + c
    npeer = _n_peers(both)
    cps = []
    for a in range(len(s_refs)):
        for j in range(npeer):
            tx, ty, tc = _peer(x, y, c, j, both)
            src = s_refs[a].at[2 * tx + ty] if per_dest else s_refs[a]
            cps.append(pltpu.make_async_remote_copy(src_ref=src, dst_ref=l_refs[a].at[me],
                                                    send_sem=send_sems.at[npeer * a + j],
                                                    recv_sem=recv_sems.at[npeer * a + j], device_id=(tx, ty, tc),
                                                    device_id_type=MESH))
    return cps


def _sibling_fill(lands, name):
    n = len(lands)

    def body(*refs):
        outs, send_sems, recv_sems = refs[n:2 * n], refs[2 * n], refs[2 * n + 1]
        x, y, c = _coords()
        cps = [pltpu.make_async_remote_copy(src_ref=outs[a].at[2 * k + c], dst_ref=outs[a].at[2 * k + c],
                                            send_sem=send_sems.at[4 * a + k], recv_sem=recv_sems.at[4 * a + k],
                                            device_id=(x, y, 1 - c), device_id_type=MESH)
               for a in range(n) for k in range(4)]
        for cp in cps:
            cp.start()
        for cp in cps:
            cp.wait()

    anyspec = pl.BlockSpec(memory_space=pl.ANY)
    res = pl.pallas_call(
        body, name=name, in_specs=[anyspec] * n, out_specs=[anyspec] * n,
        out_shape=[jax.ShapeDtypeStruct(t.shape, t.dtype) for t in lands], input_output_aliases={i: i for i in range(n)},
        scratch_shapes=[pltpu.SemaphoreType.DMA((4 * n,)), pltpu.SemaphoreType.DMA((4 * n,))],
    )(*lands)
    return list(res)


def _spread_start(srcs, per_dest, both, dev, chip, name):
    n = len(srcs)
    npeer = _n_peers(both)
    lands = []
    for s in srcs:
        own = lax.dynamic_index_in_dim(s, chip, 0, keepdims=False) if per_dest else s
        lands.append(lax.dynamic_update_index_in_dim(lax.empty((8,) + own.shape, own.dtype), own, dev, 0))

    def body(*refs):
        s_refs, l_refs, send_sems, recv_sems, token = refs[:n], refs[n:2 * n], refs[2 * n], refs[2 * n + 1], refs[-1]
        for cp in _spread_copies(s_refs, l_refs, send_sems, recv_sems, per_dest, both):
            cp.start()
        token[...] = jnp.zeros_like(token)

    hbm_in = [pltpu.with_memory_space_constraint(t, pltpu.HBM) for t in list(srcs) + lands]
    outs = pl.pallas_call(
        body, name=name,
        out_shape=(pltpu.SemaphoreType.DMA((npeer * n,)), pltpu.SemaphoreType.DMA((npeer * n,)),
                   *[pltpu.HBM(t.shape, t.dtype) for t in hbm_in], jax.ShapeDtypeStruct((8, LANES), F32)),
        in_specs=[_HBM] * (2 * n), out_specs=(_SEM, _SEM, *[_HBM] * (2 * n), pl.BlockSpec(memory_space=pltpu.VMEM)),
        input_output_aliases={i: 2 + i for i in range(2 * n)},
        compiler_params=pltpu.CompilerParams(has_side_effects=_EFFECT),
    )(*hbm_in)
    return (outs[0], outs[1], list(outs[2:2 + n]), list(outs[2 + n:2 + 2 * n])), outs[-1]


def _spread_wait(handle, per_dest, both, after, name):
    send_sems, recv_sems, srcs, lands = handle
    n = len(srcs)

    def body(*refs):
        s_refs, l_refs, send_ref, recv_ref = refs[:n], refs[n:2 * n], refs[2 * n], refs[2 * n + 1]
        for cp in _spread_copies(s_refs, l_refs, send_ref, recv_ref, per_dest, both):
            cp.wait_send()
            cp.wait_recv()

    outs = pl.pallas_call(
        body, name=name, out_shape=tuple(pltpu.HBM(t.shape, t.dtype) for t in srcs + lands),
        in_specs=[_HBM] * (2 * n) + [_SEM, _SEM, pl.BlockSpec(memory_space=pl.ANY)], out_specs=tuple([_HBM] * (2 * n)),
        input_output_aliases={i: i for i in range(2 * n)},
        compiler_params=pltpu.CompilerParams(has_side_effects=_EFFECT),
    )(*srcs, *lands, send_sems, recv_sems, after)
    return list(outs[n:])


def _row_tile(n, cap, mult):
    best = n
    for t in range(mult, min(n, cap) + 1, mult):
        if n % t == 0:
            best = t
    return best


PAIR_ADD_BLOCK_BYTES = 2 << 20


def _pair_add(g2, theirs, half, name):
    _, n, cdim = g2.shape
    tm = _row_tile(n, max(16, PAIR_ADD_BLOCK_BYTES // (4 * cdim)), 16)

    def body(h_ref, a_ref, b_ref, o_ref):
        o_ref[...] = (a_ref[...] + b_ref[...]).astype(o_ref.dtype)

    grid_spec = pltpu.PrefetchScalarGridSpec(
        num_scalar_prefetch=1, grid=(n // tm,),
        in_specs=[pl.BlockSpec((None, tm, cdim), lambda i, h: (h[0], i, 0)), pl.BlockSpec((tm, cdim), lambda i, h: (i, 0))],
        out_specs=pl.BlockSpec((tm, cdim), lambda i, h: (i, 0)))
    return pl.pallas_call(body, name=name, grid_spec=grid_spec, out_shape=jax.ShapeDtypeStruct((n, cdim), BF16),
                          compiler_params=_cparams(("parallel",)))(half.reshape(1).astype(jnp.int32), g2, theirs)


def _adamw_math(w, g, m, v):
    m = ADAM_B1 * m + (1.0 - ADAM_B1) * g
    v = ADAM_B2 * v + (1.0 - ADAM_B2) * (g * g)
    m_hat = m / (1.0 - ADAM_B1 ** ADAM_STEP)
    v_hat = v / (1.0 - ADAM_B2 ** ADAM_STEP)
    delta = -ADAM_LR * (m_hat / (jnp.sqrt(v_hat) + ADAM_EPS) + ADAM_WD * w)
    return delta, m, v


def _adamw(parts, w, m, v, name, tm=128):
    npart, R, C = parts.shape
    tm = min(tm, R)

    def body(p_ref, w_ref, m_ref, v_ref, g_out, d_out, m_out, v_out):
        g = p_ref[0].astype(F32)
        for i in range(1, npart):
            g = g + p_ref[i].astype(F32)
        d, mm, vv = _adamw_math(w_ref[...], g, m_ref[...], v_ref[...])
        g_out[...] = g
        d_out[...] = d
        m_out[...] = mm
        v_out[...] = vv

    spec = pl.BlockSpec((tm, C), lambda i: (i, 0))
    return pl.pallas_call(
        body, name=name, grid=(R // tm,),
        in_specs=[pl.BlockSpec((npart, tm, C), lambda i: (0, i, 0)), spec, spec, spec], out_specs=[spec] * 4,
        out_shape=[jax.ShapeDtypeStruct((R, C), F32)] * 4,
        compiler_params=_cparams(("parallel",)),
    )(parts, w, m, v)


def _sum_parts(parts, name):
    npart, R, C = parts.shape

    def body(p_ref, o_ref):
        g = p_ref[0]
        for i in range(1, npart):
            g = g + p_ref[i]
        o_ref[...] = g

    return pl.pallas_call(body, name=name, out_shape=jax.ShapeDtypeStruct((R, C), F32))(parts)


def _mod_fwd(c_all, w_ada, b_sh, name):
    def body(c_ref, w_ref, b_ref, o_ref):
        o_ref[...] = _dot(_silu(c_ref[...]), w_ref[...], "nn") + b_ref[...]

    return pl.pallas_call(body, name=name, out_shape=jax.ShapeDtypeStruct((c_all.shape[0], w_ada.shape[1]), F32),
                          compiler_params=pltpu.CompilerParams(vmem_limit_bytes=VMEM_LIMIT))(c_all, w_ada, b_sh)


def _mod_wgrad(c_all, dmod_sh, name):
    def body(c_ref, d_ref, o_ref):
        o_ref[...] = _dot(_silu(c_ref[...]), d_ref[...], "tn")

    return pl.pallas_call(body, name=name, out_shape=jax.ShapeDtypeStruct((c_all.shape[1], dmod_sh.shape[1]), F32),
                          compiler_params=pltpu.CompilerParams(vmem_limit_bytes=VMEM_LIMIT))(c_all, dmod_sh)


def _pad_lanes(v):
    return jnp.pad(v, ((0, 0), (0, (-v.shape[1]) % LANES)))


def kernel(x, c, norm1_w, norm2_w, w_ada, b_ada, w_in, conv_w, conv_b, dt_bias, a_log, d_skip, ssd_norm_w, q_norm_w, k_norm_w, attn_norm_w, w_out, w_ff1, w_ff2, loss_target, m_norm1_w, m_norm2_w, m_w_ada, m_b_ada, m_w_in, m_conv_w, m_conv_b, m_dt_bias, m_a_log, m_d_skip, m_ssd_norm_w, m_q_norm_w, m_k_norm_w, m_attn_norm_w, m_w_out, m_w_ff1, m_w_ff2, v_norm1_w, v_norm2_w, v_w_ada, v_b_ada, v_w_in, v_conv_w, v_conv_b, v_dt_bias, v_a_log, v_d_skip, v_ssd_norm_w, v_q_norm_w, v_k_norm_w, v_attn_norm_w, v_w_out, v_w_ff1, v_w_ff2):
    xi, yi, ci = _coords()
    chip = 2 * xi + yi
    dev = 2 * chip + ci
    xs, tgt = x[0], loss_target[0]
    S, D = xs.shape
    DI, AD = NH_SSD * HD, NH_ATT * HD
    CC = DI + 2 * NG * NSTATE
    PW = DI + CC + 3 * AD + LANES
    DFF = w_ff1.shape[2] * 4
    MIX = DI + AD
    o_xbc, o_q, o_k, o_v, o_dt = DI, DI + CC, DI + CC + AD, DI + CC + 2 * AD, DI + CC + 3 * AD

    def half_rows(w):
        r = w.shape[0] // 2
        return lax.dynamic_slice_in_dim(w, ci * r, r, 0).astype(BF16)

    c_all, conv_w_all = _exchange8([c, conv_w[0]], False, "gather_c_conv_w")
    c_all = c_all.reshape(8, D)
    c_all = jnp.pad(c_all, ((0, 8), (0, 0)))
    nmod = w_ada.shape[2]
    b_sh = lax.dynamic_slice_in_dim(b_ada, chip * nmod, nmod, 1)
    mod_sh = _mod_fwd(c_all, w_ada[0], b_sh, "mod_fwd")
    mod_all = _exchange8([mod_sh[:8]], False, "gather_mod")[0]
    mod_me = lax.dynamic_index_in_dim(mod_all[0::2], dev, 1, keepdims=False).reshape(1, 4 * nmod)
    shift1, scale1, gate1, shift2, scale2, gate2 = [mod_me[:, i * D:(i + 1) * D] for i in range(6)]

    g_in = _exchange8([half_rows(w_in[0])], False, "gather_w_in")[0]
    rest_handle, rest_token = _spread_start([half_rows(w_out[0]), half_rows(w_ff1[0]), half_rows(w_ff2[0])], False, True,
                                            dev, chip, "gather_rest_start")
    shift1 = shift1 + rest_token[0, 0]
    wsh = w_in.shape[2]
    w_in_f = g_in.reshape(4, D, wsh).transpose(1, 0, 2).reshape(D, 4 * wsh)
    n_zx = DI + CC
    w_proj = jnp.concatenate([w_in_f[:, :n_zx], w_in_f[:, n_zx + NH_SSD:], w_in_f[:, n_zx:n_zx + NH_SSD],
                              jnp.zeros((D, LANES - NH_SSD), BF16)], axis=1)

    dtb, alog, dsk = _pad_lanes(dt_bias), _pad_lanes(a_log), _pad_lanes(d_skip)
    qw2 = jnp.concatenate([q_norm_w, q_norm_w], axis=1)
    kw2 = jnp.concatenate([k_norm_w, k_norm_w], axis=1)
    conv_w_f = conv_w_all[0::2].transpose(1, 0, 2).reshape(KCONV, CC)

    h1 = _rows("norm1", lambda r, k: ([_normmod(r[0], *k)], []), [(xs, 0, D)], [norm1_w, scale1, shift1],
               [(D, BF16)], [], S)[0]
    proj = _matmul(h1, w_proj, "nn", F32, "in_proj", tm=256, tn=PW)
    xbc = _conv_fwd(proj, o_xbc, CC, conv_w_f, conv_b, "conv_fwd")
    y_ssd, hsave = _ssd_fwd(xbc, proj, o_dt // LANES, dtb, alog, dsk, ssd_norm_w, "ssd_fwd")

    def qk_call(name, col0, w2, scale):
        def body(t_ref, w_ref, o_ref):
            o_ref[...] = _headnorm(t_ref[...], w_ref[...], scale)
        return pl.pallas_call(
            body, name=name, grid=(AD // LANES,),
            in_specs=[pl.BlockSpec((S, LANES), lambda j: (0, j + col0 // LANES)),
                      pl.BlockSpec((1, LANES), lambda j: (0, 0))],
            out_specs=pl.BlockSpec((S, LANES), lambda j: (0, j)),
            out_shape=jax.ShapeDtypeStruct((S, AD), F32), compiler_params=_cparams(("parallel",)),
        )(proj, w2)

    qn = qk_call("q_norm", o_q, qw2, HD ** -0.5)
    kn = qk_call("k_norm", o_k, kw2, 1.0)
    o_att, lse = _attn_fwd(qn, kn, proj, o_v // LANES, "attn_fwd")
    y_att = _rows("attn_out_norm", lambda r, k: ([_rmsw(r[0], k[0])], []), [(o_att, 0, AD)], [attn_norm_w],
                  [(AD, BF16)], [], S)[0]
    g_out, g_ff1, g_ff2 = _spread_wait(rest_handle, False, True, o_att, "gather_rest_wait")
    w_out_f = g_out.reshape(MIX, D)
    w_out_a, w_out_b = w_out_f[:DI], w_out_f[DI:]
    w_ff1_f = g_ff1.reshape(4, D, DFF // 4).transpose(1, 0, 2).reshape(D, DFF)
    w_ff2_f = g_ff2.reshape(DFF, D)
    mix_a = _matmul(y_ssd, w_out_a, "nn", F32, "out_proj_a")
    mix = _matmul(y_att, w_out_b, "nn", F32, "out_proj_b", epilogue=lambda r, e: r + e, extras=(mix_a,))
    x2, h2 = _rows("resid_norm2", lambda r, k: (list(_resid_normmod(r[0], r[1], *k)), []), [(xs, 0, D), (mix, 0, D)],
                   [gate1, norm2_w, scale2, shift2], [(D, F32), (D, BF16)], [], S)
    u = _matmul(h2, w_ff1_f, "nn", F32, "ff1")
    relu2 = lambda t: jnp.square(jnp.maximum(t, 0.0))
    ff = _matmul(u, w_ff2_f, "nn", F32, "ff2", a_fn=relu2)

    def loss_fn(r, k):
        x2_, ff_, t_ = r
        err = x2_ + k[0] * ff_ - t_
        dy_ = err * (1.0 / D)
        ls = jnp.sum(jnp.sum(0.5 * err * err, axis=1, keepdims=True), axis=0, keepdims=True) * (1.0 / D)
        return [dy_, dy_ * k[0]], [ls, jnp.sum(dy_ * ff_, axis=0, keepdims=True)]

    dy, dff, loss_p, dgate2 = _rows("loss", loss_fn, [(x2, 0, D), (ff, 0, D), (tgt, 0, D)], [gate2],
                                    [(D, F32), (D, BF16)], [(1, 1), (1, D)], S)
    du = _matmul(dff, w_ff2_f, "nt", BF16, "ff2_dx", epilogue=lambda r, e: r * (2.0 * jnp.maximum(e, 0.0)), extras=(u,))
    gw_ff2 = _matmul(u, dff, "tn", BF16, "ff2_dw", a_fn=relu2, tm=DFF // 4, tn=D, chip_of_tile=lambda i, j: i)
    gw_ff1 = _matmul(h2, du, "tn", BF16, "ff1_dw", tm=D, tn=DFF // 4, chip_of_tile=lambda i, j: j)
    ff_handle, ff_token = _spread_start([gw_ff1, gw_ff2], True, True, dev, chip, "scatter_ff_start")
    dh2 = _matmul(du, w_ff1_f, "nt", F32, "ff1_dx")

    def resid_bwd(r, k):
        x_, mix_, dx2a, dh2_ = r
        _, vjp = jax.vjp(_resid_normmod, x_, mix_, *k)
        dx, dmix_, dg, dnw, dsc, dsh = vjp((dx2a, dh2_))
        return [dx, dmix_], [dg, dnw, dsc, dsh]

    dx2, dmix, dgate1, g_norm2, dscale2, dshift2 = _rows(
        "resid_norm2_bwd", resid_bwd, [(xs, 0, D), (mix, 0, D), (dy, 0, D), (dh2, 0, D)],
        [gate1 + ff_token[0, 0], norm2_w, scale2, shift2], [(D, F32), (D, BF16)], [(1, D)] * 4, S)
    gw_out = jnp.concatenate([_matmul(y_ssd, dmix, "tn", BF16, "out_proj_dw_a"),
                              _matmul(y_att, dmix, "tn", BF16, "out_proj_dw_b")], axis=0)
    out_handle, out_token = _spread_start([gw_out.reshape(4, MIX // 4, D)], True, True, dev, chip, "scatter_out_start")
    dy_ssd = _matmul(dmix, w_out_a, "nt", F32, "out_proj_dx_a")
    dy_att = _matmul(dmix, w_out_b, "nt", F32, "out_proj_dx_b")

    def attn_norm_bwd(r, k):
        o_, dyo = r
        _, vjp = jax.vjp(_rmsw, o_, k[0])
        do_, dw_ = vjp(dyo)
        lo = _lane_mask()
        dd_blocks = []
        for b in range(AD // LANES):
            t = (do_ * o_)[:, b * LANES:(b + 1) * LANES]
            s0 = jnp.sum(jnp.where(lo, t, 0.0), axis=1, keepdims=True)
            s1 = jnp.sum(jnp.where(lo, 0.0, t), axis=1, keepdims=True)
            dd_blocks.append(jnp.where(lo, s0, s1))
        return [do_, jnp.concatenate(dd_blocks, axis=1)], [dw_]

    do_att, dd_att, g_attn_norm = _rows("attn_norm_bwd", attn_norm_bwd, [(o_att, 0, AD), (dy_att, 0, AD)],
                                        [attn_norm_w + out_token[0, 0]], [(AD, F32), (AD, F32)], [(1, AD)], S)
    dq_n, dk_n, dv = _attn_bwd(qn, kn, proj, o_v // LANES, do_att, lse, dd_att, "attn_bwd")

    def qk_bwd_call(name, col0, w2, scale, g):
        def body(t_ref, w_ref, g_ref, o_ref, dw_ref):
            @pl.when(pl.program_id(0) == 0)
            def _():
                dw_ref[...] = jnp.zeros_like(dw_ref)
            _, vjp = jax.vjp(lambda t, w: _headnorm(t, w, scale), t_ref[...], w_ref[...])
            dt_, dw_ = vjp(g_ref[...])
            o_ref[...] = dt_.astype(BF16)
            dw_ref[...] += dw_
        blk = pl.BlockSpec((S, LANES), lambda j: (0, j))
        return pl.pallas_call(
            body, name=name, grid=(AD // LANES,),
            in_specs=[pl.BlockSpec((S, LANES), lambda j: (0, j + col0 // LANES)),
                      pl.BlockSpec((1, LANES), lambda j: (0, 0)), blk],
            out_specs=[blk, pl.BlockSpec((1, LANES), lambda j: (0, 0))],
            out_shape=[jax.ShapeDtypeStruct((S, AD), BF16), jax.ShapeDtypeStruct((1, LANES), F32)],
            compiler_params=_cparams(("arbitrary",)),
        )(proj, w2, g)

    dq, g_qw2 = qk_bwd_call("q_norm_bwd", o_q, qw2, HD ** -0.5, dq_n)
    dk, g_kw2 = qk_bwd_call("k_norm_bwd", o_k, kw2, 1.0, dk_n)
    g_q_norm = g_qw2[:, :HD] + g_qw2[:, HD:]
    g_k_norm = g_kw2[:, :HD] + g_kw2[:, HD:]

    dxbc, dz, ddtr, g_dtb, g_alog, g_dsk, g_ssd_norm = _ssd_bwd(
        xbc, proj, o_dt // LANES, dtb, alog, dsk, ssd_norm_w, hsave, dy_ssd, "ssd_bwd")
    dxbc_pre, g_conv_w, g_conv_b = _conv_bwd(proj, o_xbc, CC, conv_w_f, conv_b, dxbc, "conv_bwd")
    dproj = jnp.concatenate([dz.astype(BF16), dxbc_pre.astype(BF16), dq, dk, dv.astype(BF16), ddtr.astype(BF16)], axis=1)
    dh1 = _matmul(dproj, w_proj, "nt", F32, "in_proj_dx", tm=512, tk=PW)

    def norm1_bwd(r, k):
        x_, dh_, dres = r
        _, vjp = jax.vjp(_normmod, x_, *k)
        dx, dnw, dsc, dsh = vjp(dh_)
        return [dx + dres], [dnw, dsc, dsh]

    grad_x, g_norm1, dscale1, dshift1 = _rows("norm1_bwd", norm1_bwd, [(xs, 0, D), (dh1, 0, D), (dx2, 0, D)],
                                              [norm1_w, scale1, shift1], [(D, F32)], [(1, D)] * 3, S)
    dmod =jnp.concatenate([dshift1, dscale1, dgate1, dshift2, dscale2, dgate2], axis=1)

    small = [g_norm1, g_norm2, dmod, g_conv_b, g_dtb, g_alog, g_dsk, g_ssd_norm, _pad_lanes(g_q_norm),
             _pad_lanes(g_k_norm), g_attn_norm, g_conv_w.reshape(1, KCONV * CC)]
    sizes = [t.shape[1] for t in small]
    packed = jnp.concatenate(small, axis=1)
    nrow = -(-packed.shape[1] // LANES // 8) * 8
    packed = jnp.pad(packed, ((0, 0), (0, nrow * LANES - packed.shape[1]))).reshape(nrow, LANES)
    packed_all = _exchange8([packed], False, "gather_small_grads")[0]
    tot = _sum_parts(packed_all, "sum_small_grads").reshape(1, nrow * LANES)
    offs = [sum(sizes[:i]) for i in range(len(sizes))]
    (g_norm1, g_norm2, g_b_ada, g_conv_b, g_dtb, g_alog, g_dsk, g_ssd_norm, g_q_norm, g_k_norm, g_attn_norm,
     g_conv_w) = [tot[:, o:o + n] for o, n in zip(offs, sizes)]
    g_dtb, g_alog, g_dsk = g_dtb[:, :NH_SSD], g_alog[:, :NH_SSD], g_dsk[:, :NH_SSD]
    g_q_norm, g_k_norm = g_q_norm[:, :HD], g_k_norm[:, :HD]
    ccs = CC // 4
    g_conv_w = lax.dynamic_slice_in_dim(g_conv_w.reshape(KCONV, CC), chip * ccs, ccs, 1)

    dmod_all = packed_all.reshape(8, nrow * LANES)[:, offs[2]:offs[2] + 6 * D]
    dmod_sh = jnp.pad(lax.dynamic_slice_in_dim(dmod_all, chip * nmod, nmod, 1), ((0, 8), (0, 0)))
    gw_ada = _mod_wgrad(c_all, dmod_sh, "mod_wgrad")

    gw_proj = _matmul(h1, dproj, "tn", F32, "in_proj_dw", tn=896)
    gw_halves = gw_proj.reshape(2, D // 2, PW)
    sum_p = _pair_add(gw_halves, _pair_swap([gw_halves], "pair_swap_in")[0], ci, "pair_add_in")
    sum_in = jnp.concatenate([sum_p[:, :n_zx], sum_p[:, o_dt:o_dt + NH_SSD], sum_p[:, n_zx:o_dt]], axis=1)
    in_handle, in_token = _spread_start([sum_in.reshape(D // 2, 4, wsh).transpose(1, 0, 2)], True, False, dev, chip,
                                        "scatter_in_start")

    parts_ff1, parts_ff2 = _spread_wait(ff_handle, True, True, in_token, "scatter_ff_wait")
    res_ff1 = _adamw(parts_ff1, w_ff1[0], m_w_ff1[0], v_w_ff1[0], "adamw_w_ff1")
    res_ff2 = _adamw(parts_ff2, w_ff2[0], m_w_ff2[0], v_w_ff2[0], "adamw_w_ff2")
    parts_out = _spread_wait(out_handle, True, True, in_token, "scatter_out_wait")[0]
    res_out = _adamw(parts_out, w_out[0], m_w_out[0], v_w_out[0], "adamw_w_out")
    res_ada = _adamw(gw_ada[None], w_ada[0], m_w_ada[0], v_w_ada[0], "adamw_w_ada")
    lands_in = _sibling_fill(_spread_wait(in_handle, True, False, res_ada[0], "scatter_in_wait"), "scatter_in_fill")[0]
    res_in = _adamw(lands_in.reshape(4, D, wsh), w_in[0], m_w_in[0], v_w_in[0], "adamw_w_in")

    small_names = ["norm1_w", "norm2_w", "b_ada", "conv_w", "conv_b", "dt_bias", "a_log", "d_skip", "ssd_norm_w",
                   "q_norm_w", "k_norm_w", "attn_norm_w"]
    small_g = dict(norm1_w=g_norm1, norm2_w=g_norm2, b_ada=g_b_ada, conv_w=g_conv_w.reshape(1, KCONV * ccs),
                   conv_b=g_conv_b, dt_bias=g_dtb, a_log=g_alog, d_skip=g_dsk, ssd_norm_w=g_ssd_norm, q_norm_w=g_q_norm,
                   k_norm_w=g_k_norm, attn_norm_w=g_attn_norm)
    small_w = dict(norm1_w=(norm1_w, m_norm1_w, v_norm1_w), norm2_w=(norm2_w, m_norm2_w, v_norm2_w),
                   b_ada=(b_ada, m_b_ada, v_b_ada),
                   conv_w=tuple(t.reshape(1, KCONV * ccs) for t in (conv_w, m_conv_w, v_conv_w)),
                   conv_b=(conv_b, m_conv_b, v_conv_b), dt_bias=(dt_bias, m_dt_bias, v_dt_bias),
                   a_log=(a_log, m_a_log, v_a_log), d_skip=(d_skip, m_d_skip, v_d_skip),
                   ssd_norm_w=(ssd_norm_w, m_ssd_norm_w, v_ssd_norm_w), q_norm_w=(q_norm_w, m_q_norm_w, v_q_norm_w),
                   k_norm_w=(k_norm_w, m_k_norm_w, v_k_norm_w), attn_norm_w=(attn_norm_w, m_attn_norm_w, v_attn_norm_w))
    ssz = [_pad_lanes(small_g[n]).shape[1] for n in small_names]
    soff = [sum(ssz[:i]) for i in range(len(ssz))]
    srow = -(-sum(ssz) // LANES // 8) * 8

    def pack(ts, fill):
        t = jnp.concatenate([jnp.pad(t, ((0, 0), (0, (-t.shape[1]) % LANES)), constant_values=fill) for t in ts], axis=1)
        return jnp.pad(t, ((0, 0), (0, srow * LANES - t.shape[1])), constant_values=fill).reshape(srow, LANES)

    sg = pack([small_g[n] for n in small_names], 0.0)
    sw = pack([small_w[n][0] for n in small_names], 0.0)
    sm_ = pack([small_w[n][1] for n in small_names], 0.0)
    sv = pack([small_w[n][2] for n in small_names], 1.0)
    _, s_delta, s_m, s_v = _adamw(sg[None], sw, sm_, sv, "adamw_small", tm=srow)

    def unpack(t, n):
        i = small_names.index(n)
        return t.reshape(1, srow * LANES)[:, soff[i]:soff[i] + small_g[n].shape[1]].reshape(small_w[n][0].shape)

    loss = lax.psum(loss_p[0, 0], ("x", "y", "c"))
    big_res = dict(w_ada=res_ada, w_in=res_in, w_out=res_out, w_ff1=res_ff1, w_ff2=res_ff2)
    order = ["norm1_w", "norm2_w", "w_ada", "b_ada", "w_in", "conv_w", "conv_b", "dt_bias", "a_log", "d_skip",
             "ssd_norm_w", "q_norm_w", "k_norm_w", "attn_norm_w", "w_out", "w_ff1", "w_ff2"]
    grads, deltas, new_m, new_v = [], [], [], []
    for n in order:
        if n in big_res:
            g_, d_, m_, v_ = [t[None] for t in big_res[n]]
        else:
            g_ = small_g[n].reshape(small_w[n][0].shape)
            d_, m_, v_ = unpack(s_delta, n), unpack(s_m, n), unpack(s_v, n)
            if n == "conv_w":
                g_, d_, m_, v_ = [t.reshape(conv_w.shape) for t in (g_, d_, m_, v_)]
        grads.append(g_)
        deltas.append(d_)
        new_m.append(m_)
        new_v.append(v_)
    return (loss, grad_x[None], *grads, *deltas, *new_m, *new_v)
```

```python
import functools

import jax
import jax.numpy as jnp
from jax import lax
from jax.experimental import pallas as pl
from jax.experimental.pallas import tpu as pltpu

F32, BF16 = jnp.float32, jnp.bfloat16
EPS = 1e-6
HD = 64
NH_SSD = 16
NG = 4
NSTATE = 128
KCONV = 4
CHUNK = 128
NH_ATT = 16
PATTERNS = ((128, 1), (512, 4), (2048, 16))
ABLK = 128
LANES = 128
ADAM_LR, ADAM_B1, ADAM_B2, ADAM_EPS, ADAM_WD, ADAM_STEP = 0.001, 0.9, 0.999, 1e-08, 0.01, 10
VMEM_LIMIT = 56 * 1024 * 1024
MESH = pl.DeviceIdType.MESH
NEG = -1e30

_DN = {"nn": (((1,), (0,)), ((), ())), "nt": (((1,), (1,)), ((), ())), "tn": (((0,), (0,)), ((), ()))}


def _cparams(sem):
    return pltpu.CompilerParams(dimension_semantics=sem, vmem_limit_bytes=VMEM_LIMIT)


def _tile(n, cap):
    if n % LANES or n <= LANES:
        return n
    best = LANES
    for t in range(LANES, min(n, cap) + 1, LANES):
        if n % t == 0:
            best = t
    return best


def _silu(x):
    return x / (1.0 + jnp.exp(-x))


def _softplus(x):
    return jnp.maximum(x, 0.0) + jnp.log(1.0 + jnp.exp(-jnp.abs(x)))


def _dot(a, b, dims):
    return lax.dot_general(a.astype(BF16), b.astype(BF16), _DN[dims], preferred_element_type=F32)


def _matmul(a, b, dims, out_dtype, name, a_fn=None, epilogue=None, extras=(), tm=1024, tn=1024, tk=2048,
            chip_of_tile=None):
    if dims == "nn":
        (M, K), (_, N) = a.shape, b.shape
    elif dims == "nt":
        (M, K), (N, _) = a.shape, b.shape
    else:
        (K, M), (_, N) = a.shape, b.shape
    tm, tn, tk = _tile(M, tm), _tile(N, tn), _tile(K, tk)
    nk = K // tk
    ne = len(extras)

    def body(a_ref, b_ref, *rest):
        e_refs, o_ref = rest[:ne], rest[ne]
        av = a_ref[...]
        if a_fn is not None:
            av = a_fn(av)
        part = _dot(av, b_ref[...], dims)

        def finish(r):
            if epilogue is not None:
                r = epilogue(r, *[e[...] for e in e_refs])
            o_ref[...] = r.astype(out_dtype).reshape(o_ref.shape)

        if nk == 1:
            finish(part)
            return
        acc = rest[ne + 1]
        k = pl.program_id(2)

        @pl.when(k == 0)
        def _():
            acc[...] = part

        @pl.when(k > 0)
        def _():
            acc[...] += part

        @pl.when(k == nk - 1)
        def _():
            finish(acc[...])

    a_spec = pl.BlockSpec((tk, tm), lambda i, j, k: (k, i)) if dims == "tn" else pl.BlockSpec((tm, tk), lambda i, j, k: (i, k))
    b_spec = pl.BlockSpec((tn, tk), lambda i, j, k: (j, k)) if dims == "nt" else pl.BlockSpec((tk, tn), lambda i, j, k: (k, j))
    o_spec = pl.BlockSpec((tm, tn), lambda i, j, k: (i, j))
    out_spec, out_dims = o_spec, (M, N)
    if chip_of_tile is not None:
        assert (M // tm) * (N // tn) == 4
        out_spec = pl.BlockSpec((None, tm, tn), lambda i, j, k: (chip_of_tile(i, j), 0, 0))
        out_dims = (4, tm, tn)
    return pl.pallas_call(
        body, name=name, grid=(M // tm, N // tn, nk),
        in_specs=[a_spec, b_spec] + [o_spec] * ne, out_specs=out_spec,
        out_shape=jax.ShapeDtypeStruct(out_dims, out_dtype),
        scratch_shapes=[pltpu.VMEM((tm, tn), F32)] if nk > 1 else [],
        compiler_params=_cparams(("parallel", "parallel", "arbitrary")),
    )(a, b, *extras)


def _rows(name, fn, rows, consts, outs, accs, n_rows, tm=256):
    tm = min(tm, n_rows)
    nr, nc, no, na = len(rows), len(consts), len(outs), len(accs)

    def body(*refs):
        r_refs, c_refs = refs[:nr], refs[nr:nr + nc]
        o_refs, a_refs = refs[nr + nc:nr + nc + no], refs[nr + nc + no:]
        o_vals, a_vals = fn([r[...] for r in r_refs], [c[...] for c in c_refs])
        for ref, val in zip(o_refs, o_vals):
            ref[...] = val.astype(ref.dtype)
        if na:
            @pl.when(pl.program_id(0) == 0)
            def _():
                for ref in a_refs:
                    ref[...] = jnp.zeros_like(ref)
            for ref, val in zip(a_refs, a_vals):
                ref[...] += val

    in_specs = [pl.BlockSpec((tm, w), lambda i, cb=cb: (i, cb)) for (_, cb, w) in rows]
    in_specs += [pl.BlockSpec(cst.shape, lambda i, nd=cst.ndim: (0,) * nd) for cst in consts]
    out_specs = [pl.BlockSpec((tm, w), lambda i: (i, 0)) for (w, _) in outs]
    out_specs += [pl.BlockSpec(s, lambda i: (0, 0)) for s in accs]
    out_shape = [jax.ShapeDtypeStruct((n_rows, w), dt) for (w, dt) in outs]
    out_shape += [jax.ShapeDtypeStruct(s, F32) for s in accs]
    res = pl.pallas_call(
        body, name=name, grid=(n_rows // tm,), in_specs=in_specs, out_specs=out_specs, out_shape=out_shape,
        compiler_params=_cparams(("arbitrary",)),
    )(*[r[0] for r in rows], *consts)
    return res


def _normmod(x, nw, sc, sh):
    r = lax.rsqrt(jnp.mean(x * x, axis=-1, keepdims=True) + EPS)
    return (x * r) * nw * (1.0 + sc) + sh


def _resid_normmod(x, mix, g, nw, sc, sh):
    x2 = x + g * mix
    return x2, _normmod(x2, nw, sc, sh)


def _rmsw(o, w):
    return o * lax.rsqrt(jnp.mean(o * o, axis=-1, keepdims=True) + EPS) * w


def _lane_mask():
    return lax.broadcasted_iota(jnp.int32, (1, LANES), 1) < HD


def _headnorm(t, w, scale):
    lo = _lane_mask()
    t2 = t * t
    s0 = jnp.sum(jnp.where(lo, t2, 0.0), axis=1, keepdims=True)
    s1 = jnp.sum(jnp.where(lo, 0.0, t2), axis=1, keepdims=True)
    ms = jnp.where(lo, s0, s1) * (1.0 / HD)
    return t * lax.rsqrt(ms + EPS) * w * scale


CONV_ROWS = 128
CONV_HALO = 8


def _conv_cols(n_ch):
    return _tile(n_ch, LANES)


def _conv_fwd(proj, col0, n_ch, conv_w, conv_b, name):
    S = proj.shape[0]
    tc = _conv_cols(n_ch)

    R, H = CONV_ROWS, CONV_HALO

    def body(u_ref, w_ref, b_ref, o_ref):
        w = [w_ref[i:i + 1, :] for i in range(KCONV)]
        b = b_ref[...]

        def chunk(ext):
            acc = b + w[KCONV - 1] * ext[H:]
            for i in range(KCONV - 1):
                acc = acc + w[i] * pltpu.roll(ext, KCONV - 1 - i, 0)[H:]
            return _silu(acc)

        o_ref[0:R, :] = chunk(jnp.concatenate([jnp.zeros((H, tc), F32), u_ref[0:R, :]], axis=0))

        def step(c, carry):
            r0 = pl.multiple_of(c * R, R)
            o_ref[pl.ds(r0, R), :] = chunk(u_ref[pl.ds(pl.multiple_of(r0 - H, H), R + H), :])
            return carry

        lax.fori_loop(1, S // R, step, 0)

    return pl.pallas_call(
        body, name=name, grid=(n_ch // tc,),
        in_specs=[pl.BlockSpec((S, tc), lambda j: (0, j + col0 // tc)),
                  pl.BlockSpec((KCONV, tc), lambda j: (0, j)), pl.BlockSpec((1, tc), lambda j: (0, j))],
        out_specs=pl.BlockSpec((S, tc), lambda j: (0, j)),
        out_shape=jax.ShapeDtypeStruct((S, n_ch), F32),
        compiler_params=_cparams(("parallel",)),
    )(proj, conv_w, conv_b)


def _conv_bwd(proj, col0, n_ch, conv_w, conv_b, dxbc, name):
    S = proj.shape[0]
    tc = _conv_cols(n_ch)

    R, H = CONV_ROWS, CONV_HALO

    def body(u_ref, w_ref, b_ref, g_ref, du_ref, dw_ref, db_ref):
        w = [w_ref[i:i + 1, :] for i in range(KCONV)]
        b = b_ref[...]
        pad = jnp.zeros((H, tc), F32)

        def chunk(u_ext, g_ext):
            taps = [pltpu.roll(u_ext, KCONV - 1 - i, 0)[H:] for i in range(KCONV - 1)] + [u_ext[H:]]
            acc = b
            for i in range(KCONV):
                acc = acc + w[i] * taps[i]
            sig = 1.0 / (1.0 + jnp.exp(-acc))
            dacc = g_ext * (sig * (1.0 + acc * (1.0 - sig)))
            du = w[KCONV - 1] * dacc[:R]
            for i in range(KCONV - 1):
                du = du + w[i] * pltpu.roll(dacc, R + H - (KCONV - 1 - i), 0)[:R]
            d = dacc[:R]
            return du, [jnp.sum(d * t[:R], axis=0, keepdims=True) for t in taps], jnp.sum(d, axis=0, keepdims=True)

        du, dws, db = chunk(jnp.concatenate([pad, u_ref[0:R + H, :]], axis=0), g_ref[0:R + H, :])
        du_ref[0:R, :] = du

        def step(c, carry):
            r0 = pl.multiple_of(c * R, R)
            du_c, dws_c, db_c = chunk(u_ref[pl.ds(pl.multiple_of(r0 - H, H), R + 2 * H), :], g_ref[pl.ds(r0, R + H), :])
            du_ref[pl.ds(r0, R), :] = du_c
            return [a + b_ for a, b_ in zip(carry[0], dws_c)], carry[1] + db_c

        dws, db = lax.fori_loop(1, S // R - 1, step, (dws, db))
        du, dws_l, db_l = chunk(jnp.concatenate([u_ref[S - R - H:S, :], pad], axis=0),
                                jnp.concatenate([g_ref[S - R:S, :], pad], axis=0))
        du_ref[S - R:S, :] = du
        for i in range(KCONV):
            dw_ref[i:i + 1, :] = dws[i] + dws_l[i]
        db_ref[...] = db + db_l

    return pl.pallas_call(
        body, name=name, grid=(n_ch // tc,),
        in_specs=[pl.BlockSpec((S, tc), lambda j: (0, j + col0 // tc)),
                  pl.BlockSpec((KCONV, tc), lambda j: (0, j)), pl.BlockSpec((1, tc), lambda j: (0, j)),
                  pl.BlockSpec((S, tc), lambda j: (0, j))],
        out_specs=[pl.BlockSpec((S, tc), lambda j: (0, j)), pl.BlockSpec((KCONV, tc), lambda j: (0, j)),
                   pl.BlockSpec((1, tc), lambda j: (0, j))],
        out_shape=[jax.ShapeDtypeStruct((S, n_ch), F32), jax.ShapeDtypeStruct((KCONV, n_ch), F32),
                   jax.ShapeDtypeStruct((1, n_ch), F32)],
        compiler_params=_cparams(("parallel",)),
    )(proj, conv_w, conv_b, dxbc)


@functools.partial(jax.custom_vjp, nondiff_argnums=(2,))
def _mm(a, b, dims):
    return _dot(a, b, dims)


def _mm_fwd(a, b, dims):
    return _dot(a, b, dims), (a, b)


def _mm_bwd(dims, res, g):
    a, b = res
    if dims == "nn":
        return _dot(g, b, "nt"), _dot(a, g, "tn")
    if dims == "nt":
        return _dot(g, b, "nn"), _dot(g, a, "tn")
    return _dot(b, g, "nt"), _dot(a, g, "nn")


_mm.defvjp(_mm_fwd, _mm_bwd)


def _tri_dot(x, upper):
    n = x.shape[0]
    r = lax.broadcasted_iota(jnp.int32, (n, n), 0)
    c = lax.broadcasted_iota(jnp.int32, (n, n), 1)
    t = jnp.where((r <= c) if upper else (r >= c), 1.0, 0.0)
    return lax.dot_general(t, x, _DN["nn"], precision=lax.Precision.HIGHEST, preferred_element_type=F32)


@jax.custom_vjp
def _cumsum_rows(x):
    return _tri_dot(x, False)


_cumsum_rows.defvjp(lambda x: (_tri_dot(x, False), None), lambda _, g: (_tri_dot(g, True),))


def _ssd_chunk(xs_p, bm_g, cm_g, dtr, z_p, dtb, alog, dsk, nw_p, h_p):
    L = dtr.shape[0]
    n_pairs = len(xs_p)
    ppg = n_pairs // len(bm_g)
    lane = lax.broadcasted_iota(jnp.int32, (1, LANES), 1)
    sub = lax.broadcasted_iota(jnp.int32, (LANES, 1), 0)
    lo = lane < HD
    row_l = lax.broadcasted_iota(jnp.int32, (L, 1), 0)
    tri = lax.broadcasted_iota(jnp.int32, (L, L), 0) >= lax.broadcasted_iota(jnp.int32, (L, L), 1)

    dt = _softplus(dtr + dtb)
    acs = _cumsum_rows(dt * (-jnp.exp(alog)))
    acs_t = acs.T
    a_last = jnp.sum(jnp.where(row_l == L - 1, acs, 0.0), axis=0, keepdims=True)
    e_acs = jnp.exp(acs)
    dec = jnp.exp(a_last - acs)
    cdec = jnp.exp(a_last)

    def colv(m, h):
        return jnp.sum(jnp.where(lane == h, m, 0.0), axis=1, keepdims=True)

    def rowv(mt, h):
        return jnp.sum(jnp.where(sub == h, mt, 0.0), axis=0, keepdims=True)

    def pair(m, h0):
        return jnp.where(lo, colv(m, h0), colv(m, h0 + 1))

    ys, hs = [], []
    cb = None
    for p in range(n_pairs):
        g, h0 = p // ppg, 2 * p
        bmat, cmat = bm_g[g], cm_g[g]
        if p % ppg == 0:
            cb = _mm(cmat, bmat, "nt")
        x = xs_p[p]
        xdt = x * pair(dt, h0)
        yd = []
        for h in (h0, h0 + 1):
            seg = colv(acs, h) - rowv(acs_t, h)
            lm = jnp.where(tri, jnp.exp(jnp.where(tri, seg, 0.0)), 0.0)
            yd.append(_mm(cb * lm, xdt, "nn"))
        y = jnp.where(lo, yd[0], yd[1])
        y = y + _mm(cmat, h_p[p], "nt") * pair(e_acs, h0)
        st = _mm(xdt * pair(dec, h0), bmat, "tn")
        cd_col = jnp.where(sub < HD, colv(cdec, h0), colv(cdec, h0 + 1))
        hs.append(h_p[p] * cd_col + st)
        ys.append(y + pair(dsk, h0) * x)

    y2 = [ys[p] * _silu(z_p[p]) for p in range(n_pairs)]
    outs = []
    for g in range(len(bm_g)):
        ps = range(g * ppg, (g + 1) * ppg)
        ss = sum(jnp.sum(y2[p] * y2[p], axis=1, keepdims=True) for p in ps)
        rs = lax.rsqrt(ss * (1.0 / (ppg * LANES)) + EPS)
        outs += [y2[p] * rs * nw_p[p] for p in ps]
    return outs, hs


def _ssd_slices(xbc_ref, z_ref, nw_ref, di):
    n_pairs = di // LANES
    xs_p = [xbc_ref[:, p * LANES:(p + 1) * LANES] for p in range(n_pairs)]
    bm_g = [xbc_ref[:, di + g * NSTATE:di + (g + 1) * NSTATE] for g in range(NG)]
    cm_g = [xbc_ref[:, di + (NG + g) * NSTATE:di + (NG + g + 1) * NSTATE] for g in range(NG)]
    z_p = [z_ref[:, p * LANES:(p + 1) * LANES] for p in range(n_pairs)]
    nw_p = [nw_ref[:, p * LANES:(p + 1) * LANES] for p in range(n_pairs)]
    return xs_p, bm_g, cm_g, z_p, nw_p


def _ssd_fwd(xbc, proj, dt_cb, dtb, alog, dsk, nw, name):
    S, cc = xbc.shape
    di = NH_SSD * HD
    n_pairs = di // LANES
    nchunk = S // CHUNK

    def body(xbc_ref, z_ref, dtr_ref, dtb_ref, alog_ref, dsk_ref, nw_ref, y_ref, hs_ref, h_scr):
        @pl.when(pl.program_id(0) == 0)
        def _():
            h_scr[...] = jnp.zeros_like(h_scr)

        xs_p, bm_g, cm_g, z_p, nw_p = _ssd_slices(xbc_ref, z_ref, nw_ref, di)
        h_p = [h_scr[p * LANES:(p + 1) * LANES, :] for p in range(n_pairs)]
        hs_ref[...] = h_scr[...]
        outs, hs = _ssd_chunk(xs_p, bm_g, cm_g, dtr_ref[...], z_p, dtb_ref[...], alog_ref[...], dsk_ref[...], nw_p, h_p)
        for p in range(n_pairs):
            y_ref[:, p * LANES:(p + 1) * LANES] = outs[p].astype(y_ref.dtype)
            h_scr[p * LANES:(p + 1) * LANES, :] = hs[p]

    vec = pl.BlockSpec((1, LANES), lambda c: (0, 0))
    return pl.pallas_call(
        body, name=name, grid=(nchunk,),
        in_specs=[pl.BlockSpec((CHUNK, cc), lambda c: (c, 0)), pl.BlockSpec((CHUNK, di), lambda c: (c, 0)),
                  pl.BlockSpec((CHUNK, LANES), lambda c: (c, dt_cb)), vec, vec, vec,
                  pl.BlockSpec((1, di), lambda c: (0, 0))],
        out_specs=[pl.BlockSpec((CHUNK, di), lambda c: (c, 0)), pl.BlockSpec((None, di, NSTATE), lambda c: (c, 0, 0))],
        out_shape=[jax.ShapeDtypeStruct((S, di), BF16), jax.ShapeDtypeStruct((nchunk, di, NSTATE), F32)],
        scratch_shapes=[pltpu.VMEM((di, NSTATE), F32)],
        compiler_params=_cparams(("arbitrary",)),
    )(xbc, proj, proj, dtb, alog, dsk, nw)


def _ssd_bwd(xbc, proj, dt_cb, dtb, alog, dsk, nw, hsave, dy, name):
    S, cc = xbc.shape
    di = NH_SSD * HD
    n_pairs = di // LANES
    nchunk = S // CHUNK

    def body(xbc_ref, z_ref, dtr_ref, dtb_ref, alog_ref, dsk_ref, nw_ref, hs_ref, dy_ref,
             dxbc_ref, dz_ref, ddtr_ref, ddtb_ref, dalog_ref, ddsk_ref, dnw_ref, dh_scr):
        @pl.when(pl.program_id(0) == 0)
        def _():
            dh_scr[...] = jnp.zeros_like(dh_scr)
            ddtb_ref[...] = jnp.zeros_like(ddtb_ref)
            dalog_ref[...] = jnp.zeros_like(dalog_ref)
            ddsk_ref[...] = jnp.zeros_like(ddsk_ref)
            dnw_ref[...] = jnp.zeros_like(dnw_ref)

        xs_p, bm_g, cm_g, z_p, nw_p = _ssd_slices(xbc_ref, z_ref, nw_ref, di)
        h_p = [hs_ref[p * LANES:(p + 1) * LANES, :] for p in range(n_pairs)]
        dy_p = [dy_ref[:, p * LANES:(p + 1) * LANES].astype(F32) for p in range(n_pairs)]
        dh_p = [dh_scr[p * LANES:(p + 1) * LANES, :] for p in range(n_pairs)]
        _, vjp = jax.vjp(_ssd_chunk, xs_p, bm_g, cm_g, dtr_ref[...], z_p, dtb_ref[...], alog_ref[...], dsk_ref[...],
                         nw_p, h_p)
        dxs, dbm, dcm, ddtr, dz, ddtb, dalog, ddsk, dnw, dh = vjp((dy_p, dh_p))
        for p in range(n_pairs):
            sl = slice(p * LANES, (p + 1) * LANES)
            dxbc_ref[:, sl] = dxs[p]
            dz_ref[:, sl] = dz[p]
            dnw_ref[:, sl] += dnw[p]
            dh_scr[sl, :] = dh[p]
        for g in range(NG):
            dxbc_ref[:, di + g * NSTATE:di + (g + 1) * NSTATE] = dbm[g]
            dxbc_ref[:, di + (NG + g) * NSTATE:di + (NG + g + 1) * NSTATE] = dcm[g]
        ddtr_ref[...] = ddtr
        ddtb_ref[...] += ddtb
        dalog_ref[...] += dalog
        ddsk_ref[...] += ddsk

    last = nchunk - 1
    vec = pl.BlockSpec((1, LANES), lambda c: (0, 0))
    return pl.pallas_call(
        body, name=name, grid=(nchunk,),
        in_specs=[pl.BlockSpec((CHUNK, cc), lambda c: (last - c, 0)), pl.BlockSpec((CHUNK, di), lambda c: (last - c, 0)),
                  pl.BlockSpec((CHUNK, LANES), lambda c: (last - c, dt_cb)), vec, vec, vec,
                  pl.BlockSpec((1, di), lambda c: (0, 0)),
                  pl.BlockSpec((None, di, NSTATE), lambda c: (last - c, 0, 0)),
                  pl.BlockSpec((CHUNK, di), lambda c: (last - c, 0))],
        out_specs=[pl.BlockSpec((CHUNK, cc), lambda c: (last - c, 0)), pl.BlockSpec((CHUNK, di), lambda c: (last - c, 0)),
                   pl.BlockSpec((CHUNK, LANES), lambda c: (last - c, 0)), vec, vec, vec,
                   pl.BlockSpec((1, di), lambda c: (0, 0))],
        out_shape=[jax.ShapeDtypeStruct((S, cc), F32), jax.ShapeDtypeStruct((S, di), F32),
                   jax.ShapeDtypeStruct((S, LANES), F32), jax.ShapeDtypeStruct((1, LANES), F32),
                   jax.ShapeDtypeStruct((1, LANES), F32), jax.ShapeDtypeStruct((1, LANES), F32),
                   jax.ShapeDtypeStruct((1, di), F32)],
        scratch_shapes=[pltpu.VMEM((di, NSTATE), F32)],
        compiler_params=_cparams(("arbitrary",)),
    )(xbc, proj, proj, dtb, alog, dsk, nw, hsave, dy)


def _band_masks(rows_q, rows_k):
    qi = lax.broadcasted_iota(jnp.int32, (rows_q, rows_k), 0)
    ki = lax.broadcasted_iota(jnp.int32, (rows_q, rows_k), 1)
    return qi, ki


def _class_chunks(n_rows, d):
    per_class = n_rows // d
    ch = min(per_class, 256)
    out = []
    for r in range(d):
        for c0 in range(0, per_class, ch):
            tok = pl.ds(c0, ch) if d == 1 else pl.ds(r + d * c0, ch, stride=d)
            out.append((tok, pl.ds(r * per_class + c0, ch)))
    return out


def _to_class_order(src_ref, dst_ref, n_rows, d):
    for tok, cls in _class_chunks(n_rows, d):
        dst_ref[cls, :] = src_ref[tok, :].astype(dst_ref.dtype)


def _blk_rows(t):
    return pl.ds(pl.multiple_of(t * ABLK, ABLK), ABLK)


def _head_lanes(msk, t, t_rolled):
    return jnp.where(msk, t, t_rolled)


def _zero_unless(msk, t):
    return jnp.where(msk, t, jnp.zeros_like(t))


def _attn_fwd(qn, kn, proj, v_cb, name):
    S, ad = qn.shape
    nb = S // ABLK
    nbr = len(PATTERNS)

    def body(q_ref, k_ref, v_ref, o_ref, lse_ref, qc, kc, vc, ob, mb, lb, m_s, l_s):
        lo = _lane_mask()
        qi, ki = _band_masks(ABLK, ABLK)
        cur_ok, prev_ok = ki <= qi, ki >= qi
        for bi, (_, d) in enumerate(PATTERNS):
            nbc = S // d // ABLK
            first, last = bi == 0, bi == nbr - 1
            qs, ks, vs = q_ref, k_ref, v_ref
            if d > 1:
                qs, ks, vs = qc, kc, vc
                for src, dst in ((q_ref, qc), (k_ref, kc), (v_ref, vc)):
                    _to_class_order(src, dst, S, d)
            o_dst, m_dst, l_dst = (o_ref, m_s, l_s) if first else (ob, mb, lb)

            def blk(t, carry, nbc=nbc, qs=qs, ks=ks, vs=vs, o_dst=o_dst, m_dst=m_dst, l_dst=l_dst):
                rows, prow = _blk_rows(t), _blk_rows(jnp.maximum(t - 1, 0))
                has_prev = (t % nbc) != 0
                qv = qs[rows, :]
                q2 = jnp.concatenate([_zero_unless(lo, qv), _zero_unless(jnp.logical_not(lo), qv)], axis=0).astype(BF16)
                ok_c = jnp.concatenate([cur_ok, cur_ok], axis=0)
                ok_p = jnp.concatenate([prev_ok, prev_ok], axis=0) & has_prev
                s_c = jnp.where(ok_c, _dot(q2, ks[rows, :], "nt"), NEG)
                s_p = jnp.where(ok_p, _dot(q2, ks[prow, :], "nt"), NEG)
                m = jnp.max(jnp.maximum(s_c, s_p), axis=1, keepdims=True)
                p_c, p_p = jnp.exp(s_c - m), jnp.exp(s_p - m)
                l = jnp.sum(p_c + p_p, axis=1, keepdims=True)
                o2 = _dot(p_c, vs[rows, :], "nn") + _dot(p_p, vs[prow, :], "nn")
                o_dst[rows, :] = jnp.where(lo, o2[:ABLK], o2[ABLK:])
                m_dst[rows, :] = jnp.where(lo, m[:ABLK], m[ABLK:])
                l_dst[rows, :] = jnp.where(lo, l[:ABLK], l[ABLK:])
                return carry

            lax.fori_loop(0, nb, blk, 0, unroll=8)
            if first:
                continue
            for tok, cls in _class_chunks(S, d):
                m_old, m_b = m_s[tok, :], mb[cls, :]
                m_new = jnp.maximum(m_old, m_b)
                a, b = jnp.exp(m_old - m_new), jnp.exp(m_b - m_new)
                l_new = a * l_s[tok, :] + b * lb[cls, :]
                o_new = a * o_ref[tok, :] + b * ob[cls, :]
                if last:
                    o_ref[tok, :] = o_new / l_new
                    lse_ref[tok, :] = m_new + jnp.log(l_new)
                else:
                    o_ref[tok, :] = o_new
                    m_s[tok, :] = m_new
                    l_s[tok, :] = l_new

    col = pl.BlockSpec((S, LANES), lambda h: (0, h))
    return pl.pallas_call(
        body, name=name, grid=(ad // LANES,),
        in_specs=[col, col, pl.BlockSpec((S, LANES), lambda h: (0, h + v_cb))], out_specs=[col, col],
        out_shape=[jax.ShapeDtypeStruct((S, ad), F32), jax.ShapeDtypeStruct((S, ad), F32)],
        scratch_shapes=[pltpu.VMEM((S, LANES), BF16)] * 3 + [pltpu.VMEM((S, LANES), F32)] * 5,
        compiler_params=_cparams(("parallel",)),
    )(qn, kn, proj)


def _attn_bwd(qn, kn, proj, v_cb, do, lse, dd, name):
    S, ad = qn.shape
    nb = S // ABLK

    def body(q_ref, k_ref, v_ref, do_ref, lse_ref, dd_ref, dq_ref, dk_ref, dv_ref,
             qc, kc, vc, doc, lsec, ddc, dqc, dkc, dvc):
        lo = _lane_mask()
        qi, ki = _band_masks(ABLK, ABLK)
        cur_ok, prev_ok = ki <= qi, ki >= qi
        for bi, (_, d) in enumerate(PATTERNS):
            nbc = S // d // ABLK
            first = bi == 0
            token_order = (q_ref, k_ref, v_ref, do_ref, lse_ref, dd_ref)
            class_order = (qc, kc, vc, doc, lsec, ddc)
            if d > 1:
                for src, dst in zip(token_order, class_order):
                    _to_class_order(src, dst, S, d)
            qs, ks, vs, dos, lses, dds = class_order if d > 1 else token_order
            dq_dst, dk_dst, dv_dst = (dq_ref, dk_ref, dv_ref) if first else (dqc, dkc, dvc)
            dk_dst[...] = jnp.zeros_like(dk_dst)
            dv_dst[...] = jnp.zeros_like(dv_dst)

            def blk(t, carry, nbc=nbc, qs=qs, ks=ks, vs=vs, dos=dos, lses=lses, dds=dds,
                    dq_dst=dq_dst, dk_dst=dk_dst, dv_dst=dv_dst):
                rows, prow = _blk_rows(t), _blk_rows(jnp.maximum(t - 1, 0))
                has_prev = (t % nbc) != 0
                qv, dov, lse_b, dd_b = qs[rows, :], dos[rows, :], lses[rows, :], dds[rows, :]
                lse_r, dd_r = pltpu.roll(lse_b, HD, 1), pltpu.roll(dd_b, HD, 1)
                nlo = jnp.logical_not(lo)
                q2 = jnp.concatenate([_zero_unless(lo, qv), _zero_unless(nlo, qv)], axis=0).astype(BF16)
                do2 = jnp.concatenate([_zero_unless(lo, dov), _zero_unless(nlo, dov)], axis=0).astype(BF16)
                lse2 = jnp.concatenate([_head_lanes(lo, lse_b, lse_r), _head_lanes(nlo, lse_b, lse_r)], axis=0)
                dd2 = jnp.concatenate([_head_lanes(lo, dd_b, dd_r), _head_lanes(nlo, dd_b, dd_r)], axis=0)
                dq2 = None
                for krows, vmask in ((rows, cur_ok), (prow, prev_ok & has_prev)):
                    kv, vv = ks[krows, :], vs[krows, :]
                    vmask2 = jnp.concatenate([vmask, vmask], axis=0)
                    s = jnp.where(vmask2, _dot(q2, kv, "nt"), NEG)
                    p = jnp.exp(s - lse2)
                    ds = p * (_dot(do2, vv, "nt") - dd2)
                    dqk = _dot(ds, kv, "nn")
                    dq2 = dqk if dq2 is None else dq2 + dqk
                    dv_dst[krows, :] += _dot(p, do2, "tn")
                    dk_dst[krows, :] += _dot(ds, q2, "tn")
                dq_dst[rows, :] = jnp.where(lo, dq2[:ABLK], dq2[ABLK:])
                return carry

            lax.fori_loop(0, nb, blk, 0, unroll=4)
            if not first:
                for tok, cls in _class_chunks(S, d):
                    dq_ref[tok, :] = dq_ref[tok, :] + dqc[cls, :]
                    dk_ref[tok, :] = dk_ref[tok, :] + dkc[cls, :]
                    dv_ref[tok, :] = dv_ref[tok, :] + dvc[cls, :]

    col = pl.BlockSpec((S, LANES), lambda h: (0, h))
    col1 = pl.BlockSpec((S, LANES), lambda h: (0, h), pipeline_mode=pl.Buffered(1))
    vcol1 = pl.BlockSpec((S, LANES), lambda h: (0, h + v_cb), pipeline_mode=pl.Buffered(1))
    return pl.pallas_call(
        body, name=name, grid=(ad // LANES,),
        in_specs=[col, col, vcol1, col1, col1, col1], out_specs=[col, col, col],
        out_shape=[jax.ShapeDtypeStruct((S, ad), F32)] * 3,
        scratch_shapes=[pltpu.VMEM((S, LANES), BF16)] * 4 + [pltpu.VMEM((S, LANES), F32)] * 5,
        compiler_params=_cparams(("parallel",)),
    )(qn, kn, proj, do, lse, dd)


def _coords():
    return lax.axis_index("x"), lax.axis_index("y"), lax.axis_index("c")


def _exchange8(xs, per_dest, name):
    n = len(xs)
    blk = [x.shape[1:] if per_dest else x.shape for x in xs]

    def body(*refs):
        ins, outs = refs[:n], refs[n:2 * n]
        send_sems, recv_sems, local_sems = refs[2 * n:]
        x, y, c = _coords()
        sibling = (x, y, 1 - c)
        chips = [(1 - x, y), (x, 1 - y), (1 - x, 1 - y)]
        first, passed, mine = [], [], []
        for a in range(n):
            def src_for(cx, cy, a=a):
                return ins[a].at[2 * cx + cy] if per_dest else ins[a]

            def slot(px, py, pc, a=a):
                return outs[a].at[4 * px + 2 * py + pc]

            def copy(k, src, dst, to, a=a):
                return pltpu.make_async_remote_copy(src_ref=src, dst_ref=dst, send_sem=send_sems.at[7 * a + k],
                                                    recv_sem=recv_sems.at[7 * a + k], device_id=to, device_id_type=MESH)

            m = pltpu.make_async_copy(src_for(x, y), slot(x, y, c), local_sems.at[a])
            m.start()
            mine.append(m)
            cps = [copy(0, src_for(x, y), slot(x, y, c), sibling)]
            cps += [copy(1 + j, src_for(*chip), slot(x, y, c), (*chip, c)) for j, chip in enumerate(chips)]
            for cp in cps:
                cp.start()
            first += cps
        for a in range(n):
            def slot(px, py, pc, a=a):
                return outs[a].at[4 * px + 2 * py + pc]

            def copy(k, src, dst, to, a=a):
                return pltpu.make_async_remote_copy(src_ref=src, dst_ref=dst, send_sem=send_sems.at[7 * a + k],
                                                    recv_sem=recv_sems.at[7 * a + k], device_id=to, device_id_type=MESH)

            for j, chip in enumerate(chips):
                copy(1 + j, slot(*chip, c), slot(*chip, c), (*chip, c)).wait_recv()
                fw = copy(4 + j, slot(*chip, c), slot(*chip, c), sibling)
                fw.start()
                passed.append(fw)
        for a in range(n):
            def slot(px, py, pc, a=a):
                return outs[a].at[4 * px + 2 * py + pc]

            def copy(k, src, dst, to, a=a):
                return pltpu.make_async_remote_copy(src_ref=src, dst_ref=dst, send_sem=send_sems.at[7 * a + k],
                                                    recv_sem=recv_sems.at[7 * a + k], device_id=to, device_id_type=MESH)

            copy(0, slot(x, y, 1 - c), slot(x, y, 1 - c), sibling).wait_recv()
            for j, chip in enumerate(chips):
                copy(4 + j, slot(*chip, 1 - c), slot(*chip, 1 - c), sibling).wait_recv()
        for cp in first + passed:
            cp.wait_send()
        for m in mine:
            m.wait()

    anyspec = pl.BlockSpec(memory_space=pl.ANY)
    res = pl.pallas_call(
        body, name=name, in_specs=[anyspec] * n, out_specs=[anyspec] * n,
        out_shape=[jax.ShapeDtypeStruct((8,) + tuple(b), x.dtype) for b, x in zip(blk, xs)],
        scratch_shapes=[pltpu.SemaphoreType.DMA((7 * n,)), pltpu.SemaphoreType.DMA((7 * n,)),
                        pltpu.SemaphoreType.DMA((n,))],
    )(*xs)
    return list(res)


def _pair_swap(xs, name):
    n = len(xs)

    def body(*refs):
        ins, outs = refs[:n], refs[n:2 * n]
        send_sems, recv_sems = refs[2 * n:]
        x, y, c = _coords()
        cps = [pltpu.make_async_remote_copy(src_ref=ins[a].at[1 - c], dst_ref=outs[a], send_sem=send_sems.at[a],
                                            recv_sem=recv_sems.at[a], device_id=(x, y, 1 - c), device_id_type=MESH)
               for a in range(n)]
        for cp in cps:
            cp.start()
        for cp in cps:
            cp.wait()

    anyspec = pl.BlockSpec(memory_space=pl.ANY)
    res = pl.pallas_call(
        body, name=name, in_specs=[anyspec] * n, out_specs=[anyspec] * n,
        out_shape=[jax.ShapeDtypeStruct(x.shape[1:], x.dtype) for x in xs],
        scratch_shapes=[pltpu.SemaphoreType.DMA((n,)), pltpu.SemaphoreType.DMA((n,))],
    )(*xs)
    return list(res)


_HBM = pl.BlockSpec(memory_space=pltpu.HBM)
_SEM = pl.BlockSpec(memory_space=pltpu.SEMAPHORE)
_EFFECT = pltpu.SideEffectType.DATAFLOW_SIDE_EFFECTING


def _n_peers(both):
    return 7 if both else 3


def _peer(x, y, c, j, both):
    bits = j + 1 if both else 2 * (j + 1)
    dx, dy, dc = bits >> 2 & 1, bits >> 1 & 1, bits & 1
    return (1 - x if dx else x, 1 - y if dy else y, 1 - c if dc else c)


def _spread_copies(s_refs, l_refs, send_sems, recv_sems, per_dest, both):
    x, y, c = _coords()
    me = 4 * x + 2 * y + c
    npeer = _n_peers(both)
    cps = []
    for a in range(len(s_refs)):
        for j in range(npeer):
            tx, ty, tc = _peer(x, y, c, j, both)
            src = s_refs[a].at[2 * tx + ty] if per_dest else s_refs[a]
            cps.append(pltpu.make_async_remote_copy(src_ref=src, dst_ref=l_refs[a].at[me],
                                                    send_sem=send_sems.at[npeer * a + j],
                                                    recv_sem=recv_sems.at[npeer * a + j], device_id=(tx, ty, tc),
                                                    device_id_type=MESH))
    return cps


def _sibling_fill(lands, name):
    n = len(lands)

    def body(*refs):
        outs, send_sems, recv_sems = refs[n:2 * n], refs[2 * n], refs[2 * n + 1]
        x, y, c = _coords()
        cps = [pltpu.make_async_remote_copy(src_ref=outs[a].at[2 * k + c], dst_ref=outs[a].at[2 * k + c],
                                            send_sem=send_sems.at[4 * a + k], recv_sem=recv_sems.at[4 * a + k],
                                            device_id=(x, y, 1 - c), device_id_type=MESH)
               for a in range(n) for k in range(4)]
        for cp in cps:
            cp.start()
        for cp in cps:
            cp.wait()

    anyspec = pl.BlockSpec(memory_space=pl.ANY)
    res = pl.pallas_call(
        body, name=name, in_specs=[anyspec] * n, out_specs=[anyspec] * n,
        out_shape=[jax.ShapeDtypeStruct(t.shape, t.dtype) for t in lands], input_output_aliases={i: i for i in range(n)},
        scratch_shapes=[pltpu.SemaphoreType.DMA((4 * n,)), pltpu.SemaphoreType.DMA((4 * n,))],
    )(*lands)
    return list(res)


def _spread_start(srcs, per_dest, both, dev, chip, name):
    n = len(srcs)
    npeer = _n_peers(both)
    lands = []
    for s in srcs:
        own = lax.dynamic_index_in_dim(s, chip, 0, keepdims=False) if per_dest else s
        lands.append(lax.dynamic_update_index_in_dim(lax.empty((8,) + own.shape, own.dtype), own, dev, 0))

    def body(*refs):
        s_refs, l_refs, send_sems, recv_sems, token = refs[:n], refs[n:2 * n], refs[2 * n], refs[2 * n + 1], refs[-1]
        for cp in _spread_copies(s_refs, l_refs, send_sems, recv_sems, per_dest, both):
            cp.start()
        token[...] = jnp.zeros_like(token)

    hbm_in = [pltpu.with_memory_space_constraint(t, pltpu.HBM) for t in list(srcs) + lands]
    outs = pl.pallas_call(
        body, name=name,
        out_shape=(pltpu.SemaphoreType.DMA((npeer * n,)), pltpu.SemaphoreType.DMA((npeer * n,)),
                   *[pltpu.HBM(t.shape, t.dtype) for t in hbm_in], jax.ShapeDtypeStruct((8, LANES), F32)),
        in_specs=[_HBM] * (2 * n), out_specs=(_SEM, _SEM, *[_HBM] * (2 * n), pl.BlockSpec(memory_space=pltpu.VMEM)),
        input_output_aliases={i: 2 + i for i in range(2 * n)},
        compiler_params=pltpu.CompilerParams(has_side_effects=_EFFECT),
    )(*hbm_in)
    return (outs[0], outs[1], list(outs[2:2 + n]), list(outs[2 + n:2 + 2 * n])), outs[-1]


def _spread_wait(handle, per_dest, both, after, name):
    send_sems, recv_sems, srcs, lands = handle
    n = len(srcs)

    def body(*refs):
        s_refs, l_refs, send_ref, recv_ref = refs[:n], refs[n:2 * n], refs[2 * n], refs[2 * n + 1]
        for cp in _spread_copies(s_refs, l_refs, send_ref, recv_ref, per_dest, both):
            cp.wait_send()
            cp.wait_recv()

    outs = pl.pallas_call(
        body, name=name, out_shape=tuple(pltpu.HBM(t.shape, t.dtype) for t in srcs + lands),
        in_specs=[_HBM] * (2 * n) + [_SEM, _SEM, pl.BlockSpec(memory_space=pl.ANY)], out_specs=tuple([_HBM] * (2 * n)),
        input_output_aliases={i: i for i in range(2 * n)},
        compiler_params=pltpu.CompilerParams(has_side_effects=_EFFECT),
    )(*srcs, *lands, send_sems, recv_sems, after)
    return list(outs[n:])


def _row_tile(n, cap, mult):
    best = n
    for t in range(mult, min(n, cap) + 1, mult):
        if n % t == 0:
            best = t
    return best


PAIR_ADD_BLOCK_BYTES = 2 << 20


def _pair_add(g2, theirs, half, name):
    _, n, cdim = g2.shape
    tm = _row_tile(n, max(16, PAIR_ADD_BLOCK_BYTES // (4 * cdim)), 16)

    def body(h_ref, a_ref, b_ref, o_ref):
        o_ref[...] = (a_ref[...] + b_ref[...]).astype(o_ref.dtype)

    grid_spec = pltpu.PrefetchScalarGridSpec(
        num_scalar_prefetch=1, grid=(n // tm,),
        in_specs=[pl.BlockSpec((None, tm, cdim), lambda i, h: (h[0], i, 0)), pl.BlockSpec((tm, cdim), lambda i, h: (i, 0))],
        out_specs=pl.BlockSpec((tm, cdim), lambda i, h: (i, 0)))
    return pl.pallas_call(body, name=name, grid_spec=grid_spec, out_shape=jax.ShapeDtypeStruct((n, cdim), BF16),
                          compiler_params=_cparams(("parallel",)))(half.reshape(1).astype(jnp.int32), g2, theirs)


def _adamw_math(w, g, m, v):
    m = ADAM_B1 * m + (1.0 - ADAM_B1) * g
    v = ADAM_B2 * v + (1.0 - ADAM_B2) * (g * g)
    m_hat = m / (1.0 - ADAM_B1 ** ADAM_STEP)
    v_hat = v / (1.0 - ADAM_B2 ** ADAM_STEP)
    delta = -ADAM_LR * (m_hat / (jnp.sqrt(v_hat) + ADAM_EPS) + ADAM_WD * w)
    return delta, m, v


def _adamw(parts, w, m, v, name, tm=128):
    npart, R, C = parts.shape
    tm = min(tm, R)

    def body(p_ref, w_ref, m_ref, v_ref, g_out, d_out, m_out, v_out):
        g = p_ref[0].astype(F32)
        for i in range(1, npart):
            g = g + p_ref[i].astype(F32)
        d, mm, vv = _adamw_math(w_ref[...], g, m_ref[...], v_ref[...])
        g_out[...] = g
        d_out[...] = d
        m_out[...] = mm
        v_out[...] = vv

    spec = pl.BlockSpec((tm, C), lambda i: (i, 0))
    return pl.pallas_call(
        body, name=name, grid=(R // tm,),
        in_specs=[pl.BlockSpec((npart, tm, C), lambda i: (0, i, 0)), spec, spec, spec], out_specs=[spec] * 4,
        out_shape=[jax.ShapeDtypeStruct((R, C), F32)] * 4,
        compiler_params=_cparams(("parallel",)),
    )(parts, w, m, v)


def _sum_parts(parts, name):
    npart, R, C = parts.shape

    def body(p_ref, o_ref):
        g = p_ref[0]
        for i in range(1, npart):
            g = g + p_ref[i]
        o_ref[...] = g

    return pl.pallas_call(body, name=name, out_shape=jax.ShapeDtypeStruct((R, C), F32))(parts)


def _mod_fwd(c_all, w_ada, b_sh, name):
    def body(c_ref, w_ref, b_ref, o_ref):
        o_ref[...] = _dot(_silu(c_ref[...]), w_ref[...], "nn") + b_ref[...]

    return pl.pallas_call(body, name=name, out_shape=jax.ShapeDtypeStruct((c_all.shape[0], w_ada.shape[1]), F32),
                          compiler_params=pltpu.CompilerParams(vmem_limit_bytes=VMEM_LIMIT))(c_all, w_ada, b_sh)


def _mod_wgrad(c_all, dmod_sh, name):
    def body(c_ref, d_ref, o_ref):
        o_ref[...] = _dot(_silu(c_ref[...]), d_ref[...], "tn")

    return pl.pallas_call(body, name=name, out_shape=jax.ShapeDtypeStruct((c_all.shape[1], dmod_sh.shape[1]), F32),
                          compiler_params=pltpu.CompilerParams(vmem_limit_bytes=VMEM_LIMIT))(c_all, dmod_sh)


def _pad_lanes(v):
    return jnp.pad(v, ((0, 0), (0, (-v.shape[1]) % LANES)))


def kernel(x, c, norm1_w, norm2_w, w_ada, b_ada, w_in, conv_w, conv_b, dt_bias, a_log, d_skip, ssd_norm_w, q_norm_w, k_norm_w, attn_norm_w, w_out, w_ff1, w_ff2, loss_target, m_norm1_w, m_norm2_w, m_w_ada, m_b_ada, m_w_in, m_conv_w, m_conv_b, m_dt_bias, m_a_log, m_d_skip, m_ssd_norm_w, m_q_norm_w, m_k_norm_w, m_attn_norm_w, m_w_out, m_w_ff1, m_w_ff2, v_norm1_w, v_norm2_w, v_w_ada, v_b_ada, v_w_in, v_conv_w, v_conv_b, v_dt_bias, v_a_log, v_d_skip, v_ssd_norm_w, v_q_norm_w, v_k_norm_w, v_attn_norm_w, v_w_out, v_w_ff1, v_w_ff2):
    xi, yi, ci = _coords()
    chip = 2 * xi + yi
    dev = 2 * chip + ci
    xs, tgt = x[0], loss_target[0]
    S, D = xs.shape
    DI, AD = NH_SSD * HD, NH_ATT * HD
    CC = DI + 2 * NG * NSTATE
    PW = DI + CC + 3 * AD + LANES
    DFF = w_ff1.shape[2] * 4
    MIX = DI + AD
    o_xbc, o_q, o_k, o_v, o_dt = DI, DI + CC, DI + CC + AD, DI + CC + 2 * AD, DI + CC + 3 * AD

    def half_rows(w):
        r = w.shape[0] // 2
        return lax.dynamic_slice_in_dim(w, ci * r, r, 0).astype(BF16)

    c_all, conv_w_all = _exchange8([c, conv_w[0]], False, "gather_c_conv_w")
    c_all = c_all.reshape(8, D)
    c_all = jnp.pad(c_all, ((0, 8), (0, 0)))
    nmod = w_ada.shape[2]
    b_sh = lax.dynamic_slice_in_dim(b_ada, chip * nmod, nmod, 1)
    mod_sh = _mod_fwd(c_all, w_ada[0], b_sh, "mod_fwd")
    mod_all = _exchange8([mod_sh[:8]], False, "gather_mod")[0]
    mod_me = lax.dynamic_index_in_dim(mod_all[0::2], dev, 1, keepdims=False).reshape(1, 4 * nmod)
    shift1, scale1, gate1, shift2, scale2, gate2 = [mod_me[:, i * D:(i + 1) * D] for i in range(6)]

    g_in = _exchange8([half_rows(w_in[0])], False, "gather_w_in")[0]
    rest_handle, rest_token = _spread_start([half_rows(w_out[0]), half_rows(w_ff1[0]), half_rows(w_ff2[0])], False, True,
                                            dev, chip, "gather_rest_start")
    shift1 = shift1 + rest_token[0, 0]
    wsh = w_in.shape[2]
    w_in_f = g_in.reshape(4, D, wsh).transpose(1, 0, 2).reshape(D, 4 * wsh)
    n_zx = DI + CC
    w_proj = jnp.concatenate([w_in_f[:, :n_zx], w_in_f[:, n_zx + NH_SSD:], w_in_f[:, n_zx:n_zx + NH_SSD],
                              jnp.zeros((D, LANES - NH_SSD), BF16)], axis=1)

    dtb, alog, dsk = _pad_lanes(dt_bias), _pad_lanes(a_log), _pad_lanes(d_skip)
    qw2 = jnp.concatenate([q_norm_w, q_norm_w], axis=1)
    kw2 = jnp.concatenate([k_norm_w, k_norm_w], axis=1)
    conv_w_f = conv_w_all[0::2].transpose(1, 0, 2).reshape(KCONV, CC)

    h1 = _rows("norm1", lambda r, k: ([_normmod(r[0], *k)], []), [(xs, 0, D)], [norm1_w, scale1, shift1],
               [(D, BF16)], [], S)[0]
    proj = _matmul(h1, w_proj, "nn", F32, "in_proj", tm=256, tn=PW)
    xbc = _conv_fwd(proj, o_xbc, CC, conv_w_f, conv_b, "conv_fwd")
    y_ssd, hsave = _ssd_fwd(xbc, proj, o_dt // LANES, dtb, alog, dsk, ssd_norm_w, "ssd_fwd")

    def qk_call(name, col0, w2, scale):
        def body(t_ref, w_ref, o_ref):
            o_ref[...] = _headnorm(t_ref[...], w_ref[...], scale)
        return pl.pallas_call(
            body, name=name, grid=(AD // LANES,),
            in_specs=[pl.BlockSpec((S, LANES), lambda j: (0, j + col0 // LANES)),
                      pl.BlockSpec((1, LANES), lambda j: (0, 0))],
            out_specs=pl.BlockSpec((S, LANES), lambda j: (0, j)),
            out_shape=jax.ShapeDtypeStruct((S, AD), F32), compiler_params=_cparams(("parallel",)),
        )(proj, w2)

    qn = qk_call("q_norm", o_q, qw2, HD ** -0.5)
    kn = qk_call("k_norm", o_k, kw2, 1.0)
    o_att, lse = _attn_fwd(qn, kn, proj, o_v // LANES, "attn_fwd")
    y_att = _rows("attn_out_norm", lambda r, k: ([_rmsw(r[0], k[0])], []), [(o_att, 0, AD)], [attn_norm_w],
                  [(AD, BF16)], [], S)[0]
    g_out, g_ff1, g_ff2 = _spread_wait(rest_handle, False, True, o_att, "gather_rest_wait")
    w_out_f = g_out.reshape(MIX, D)
    w_out_a, w_out_b = w_out_f[:DI], w_out_f[DI:]
    w_ff1_f = g_ff1.reshape(4, D, DFF // 4).transpose(1, 0, 2).reshape(D, DFF)
    w_ff2_f = g_ff2.reshape(DFF, D)
    mix_a = _matmul(y_ssd, w_out_a, "nn", F32, "out_proj_a")
    mix = _matmul(y_att, w_out_b, "nn", F32, "out_proj_b", epilogue=lambda r, e: r + e, extras=(mix_a,))
    x2, h2 = _rows("resid_norm2", lambda r, k: (list(_resid_normmod(r[0], r[1], *k)), []), [(xs, 0, D), (mix, 0, D)],
                   [gate1, norm2_w, scale2, shift2], [(D, F32), (D, BF16)], [], S)
    u = _matmul(h2, w_ff1_f, "nn", F32, "ff1")
    relu2 = lambda t: jnp.square(jnp.maximum(t, 0.0))
    ff = _matmul(u, w_ff2_f, "nn", F32, "ff2", a_fn=relu2)

    def loss_fn(r, k):
        x2_, ff_, t_ = r
        err = x2_ + k[0] * ff_ - t_
        dy_ = err * (1.0 / D)
        ls = jnp.sum(jnp.sum(0.5 * err * err, axis=1, keepdims=True), axis=0, keepdims=True) * (1.0 / D)
        return [dy_, dy_ * k[0]], [ls, jnp.sum(dy_ * ff_, axis=0, keepdims=True)]

    dy, dff, loss_p, dgate2 = _rows("loss", loss_fn, [(x2, 0, D), (ff, 0, D), (tgt, 0, D)], [gate2],
                                    [(D, F32), (D, BF16)], [(1, 1), (1, D)], S)
    du = _matmul(dff, w_ff2_f, "nt", BF16, "ff2_dx", epilogue=lambda r, e: r * (2.0 * jnp.maximum(e, 0.0)), extras=(u,))
    gw_ff2 = _matmul(u, dff, "tn", BF16, "ff2_dw", a_fn=relu2, tm=DFF // 4, tn=D, chip_of_tile=lambda i, j: i)
    gw_ff1 = _matmul(h2, du, "tn", BF16, "ff1_dw", tm=D, tn=DFF // 4, chip_of_tile=lambda i, j: j)
    ff_handle, ff_token = _spread_start([gw_ff1, gw_ff2], True, True, dev, chip, "scatter_ff_start")
    dh2 = _matmul(du, w_ff1_f, "nt", F32, "ff1_dx")

    def resid_bwd(r, k):
        x_, mix_, dx2a, dh2_ = r
        _, vjp = jax.vjp(_resid_normmod, x_, mix_, *k)
        dx, dmix_, dg, dnw, dsc, dsh = vjp((dx2a, dh2_))
        return [dx, dmix_], [dg, dnw, dsc, dsh]

    dx2, dmix, dgate1, g_norm2, dscale2, dshift2 = _rows(
        "resid_norm2_bwd", resid_bwd, [(xs, 0, D), (mix, 0, D), (dy, 0, D), (dh2, 0, D)],
        [gate1 + ff_token[0, 0], norm2_w, scale2, shift2], [(D, F32), (D, BF16)], [(1, D)] * 4, S)
    gw_out = jnp.concatenate([_matmul(y_ssd, dmix, "tn", BF16, "out_proj_dw_a"),
                              _matmul(y_att, dmix, "tn", BF16, "out_proj_dw_b")], axis=0)
    out_handle, out_token = _spread_start([gw_out.reshape(4, MIX // 4, D)], True, True, dev, chip, "scatter_out_start")
    dy_ssd = _matmul(dmix, w_out_a, "nt", F32, "out_proj_dx_a")
    dy_att = _matmul(dmix, w_out_b, "nt", F32, "out_proj_dx_b")

    def attn_norm_bwd(r, k):
        o_, dyo = r
        _, vjp = jax.vjp(_rmsw, o_, k[0])
        do_, dw_ = vjp(dyo)
        lo = _lane_mask()
        dd_blocks = []
        for b in range(AD // LANES):
            t = (do_ * o_)[:, b * LANES:(b + 1) * LANES]
            s0 = jnp.sum(jnp.where(lo, t, 0.0), axis=1, keepdims=True)
            s1 = jnp.sum(jnp.where(lo, 0.0, t), axis=1, keepdims=True)
            dd_blocks.append(jnp.where(lo, s0, s1))
        return [do_, jnp.concatenate(dd_blocks, axis=1)], [dw_]

    do_att, dd_att, g_attn_norm = _rows("attn_norm_bwd", attn_norm_bwd, [(o_att, 0, AD), (dy_att, 0, AD)],
                                        [attn_norm_w + out_token[0, 0]], [(AD, F32), (AD, F32)], [(1, AD)], S)
    dq_n, dk_n, dv = _attn_bwd(qn, kn, proj, o_v // LANES, do_att, lse, dd_att, "attn_bwd")

    def qk_bwd_call(name, col0, w2, scale, g):
        def body(t_ref, w_ref, g_ref, o_ref, dw_ref):
            @pl.when(pl.program_id(0) == 0)
            def _():
                dw_ref[...] = jnp.zeros_like(dw_ref)
            _, vjp = jax.vjp(lambda t, w: _headnorm(t, w, scale), t_ref[...], w_ref[...])
            dt_, dw_ = vjp(g_ref[...])
            o_ref[...] = dt_.astype(BF16)
            dw_ref[...] += dw_
        blk = pl.BlockSpec((S, LANES), lambda j: (0, j))
        return pl.pallas_call(
            body, name=name, grid=(AD // LANES,),
            in_specs=[pl.BlockSpec((S, LANES), lambda j: (0, j + col0 // LANES)),
                      pl.BlockSpec((1, LANES), lambda j: (0, 0)), blk],
            out_specs=[blk, pl.BlockSpec((1, LANES), lambda j: (0, 0))],
            out_shape=[jax.ShapeDtypeStruct((S, AD), BF16), jax.ShapeDtypeStruct((1, LANES), F32)],
            compiler_params=_cparams(("arbitrary",)),
        )(proj, w2, g)

    dq, g_qw2 = qk_bwd_call("q_norm_bwd", o_q, qw2, HD ** -0.5, dq_n)
    dk, g_kw2 = qk_bwd_call("k_norm_bwd", o_k, kw2, 1.0, dk_n)
    g_q_norm = g_qw2[:, :HD] + g_qw2[:, HD:]
    g_k_norm = g_kw2[:, :HD] + g_kw2[:, HD:]

    dxbc, dz, ddtr, g_dtb, g_alog, g_dsk, g_ssd_norm = _ssd_bwd(
        xbc, proj, o_dt // LANES, dtb, alog, dsk, ssd_norm_w, hsave, dy_ssd, "ssd_bwd")
    dxbc_pre, g_conv_w, g_conv_b = _conv_bwd(proj, o_xbc, CC, conv_w_f, conv_b, dxbc, "conv_bwd")
    dproj = jnp.concatenate([dz.astype(BF16), dxbc_pre.astype(BF16), dq, dk, dv.astype(BF16), ddtr.astype(BF16)], axis=1)
    gw_proj = _matmul(h1, dproj, "tn", F32, "in_proj_dw", tn=896)
    gw_halves = gw_proj.reshape(2, D // 2, PW)
    sum_p = _pair_add(gw_halves, _pair_swap([gw_halves], "pair_swap_in")[0], ci, "pair_add_in")
    sum_in = jnp.concatenate([sum_p[:, :n_zx], sum_p[:, o_dt:o_dt + NH_SSD], sum_p[:, n_zx:o_dt]], axis=1)
    in_handle, in_token = _spread_start([sum_in.reshape(D // 2, 4, wsh).transpose(1, 0, 2)], True, False, dev, chip,
                                        "scatter_in_start")
    dh1 = _matmul(dproj, w_proj, "nt", F32, "in_proj_dx", tm=512, tk=PW)

    def norm1_bwd(r, k):
        x_, dh_, dres = r
        _, vjp = jax.vjp(_normmod, x_, *k)
        dx, dnw, dsc, dsh = vjp(dh_)
        return [dx + dres], [dnw, dsc, dsh]

    grad_x, g_norm1, dscale1, dshift1 = _rows("norm1_bwd", norm1_bwd, [(xs, 0, D), (dh1, 0, D), (dx2, 0, D)],
                                              [norm1_w + in_token[0, 0], scale1, shift1], [(D, F32)], [(1, D)] * 3, S)
    dmod =jnp.concatenate([dshift1, dscale1, dgate1, dshift2, dscale2, dgate2], axis=1)

    small = [g_norm1, g_norm2, dmod, g_conv_b, g_dtb, g_alog, g_dsk, g_ssd_norm, _pad_lanes(g_q_norm),
             _pad_lanes(g_k_norm), g_attn_norm, g_conv_w.reshape(1, KCONV * CC)]
    sizes = [t.shape[1] for t in small]
    packed = jnp.concatenate(small, axis=1)
    nrow = -(-packed.shape[1] // LANES // 8) * 8
    packed = jnp.pad(packed, ((0, 0), (0, nrow * LANES - packed.shape[1]))).reshape(nrow, LANES)
    packed_all = _exchange8([packed], False, "gather_small_grads")[0]
    tot = _sum_parts(packed_all, "sum_small_grads").reshape(1, nrow * LANES)
    offs = [sum(sizes[:i]) for i in range(len(sizes))]
    (g_norm1, g_norm2, g_b_ada, g_conv_b, g_dtb, g_alog, g_dsk, g_ssd_norm, g_q_norm, g_k_norm, g_attn_norm,
     g_conv_w) = [tot[:, o:o + n] for o, n in zip(offs, sizes)]
    g_dtb, g_alog, g_dsk = g_dtb[:, :NH_SSD], g_alog[:, :NH_SSD], g_dsk[:, :NH_SSD]
    g_q_norm, g_k_norm = g_q_norm[:, :HD], g_k_norm[:, :HD]
    ccs = CC // 4
    g_conv_w = lax.dynamic_slice_in_dim(g_conv_w.reshape(KCONV, CC), chip * ccs, ccs, 1)

    dmod_all = packed_all.reshape(8, nrow * LANES)[:, offs[2]:offs[2] + 6 * D]
    dmod_sh = jnp.pad(lax.dynamic_slice_in_dim(dmod_all, chip * nmod, nmod, 1), ((0, 8), (0, 0)))
    gw_ada = _mod_wgrad(c_all, dmod_sh, "mod_wgrad")

    parts_ff1, parts_ff2 = _spread_wait(ff_handle, True, True, in_token, "scatter_ff_wait")
    res_ff1 = _adamw(parts_ff1, w_ff1[0], m_w_ff1[0], v_w_ff1[0], "adamw_w_ff1")
    res_ff2 = _adamw(parts_ff2, w_ff2[0], m_w_ff2[0], v_w_ff2[0], "adamw_w_ff2")
    parts_out = _spread_wait(out_handle, True, True, in_token, "scatter_out_wait")[0]
    res_out = _adamw(parts_out, w_out[0], m_w_out[0], v_w_out[0], "adamw_w_out")
    res_ada = _adamw(gw_ada[None], w_ada[0], m_w_ada[0], v_w_ada[0], "adamw_w_ada")
    lands_in = _sibling_fill(_spread_wait(in_handle, True, False, res_ada[0], "scatter_in_wait"), "scatter_in_fill")[0]
    res_in = _adamw(lands_in.reshape(4, D, wsh), w_in[0], m_w_in[0], v_w_in[0], "adamw_w_in")

    small_names = ["norm1_w", "norm2_w", "b_ada", "conv_w", "conv_b", "dt_bias", "a_log", "d_skip", "ssd_norm_w",
                   "q_norm_w", "k_norm_w", "attn_norm_w"]
    small_g = dict(norm1_w=g_norm1, norm2_w=g_norm2, b_ada=g_b_ada, conv_w=g_conv_w.reshape(1, KCONV * ccs),
                   conv_b=g_conv_b, dt_bias=g_dtb, a_log=g_alog, d_skip=g_dsk, ssd_norm_w=g_ssd_norm, q_norm_w=g_q_norm,
                   k_norm_w=g_k_norm, attn_norm_w=g_attn_norm)
    small_w = dict(norm1_w=(norm1_w, m_norm1_w, v_norm1_w), norm2_w=(norm2_w, m_norm2_w, v_norm2_w),
                   b_ada=(b_ada, m_b_ada, v_b_ada),
                   conv_w=tuple(t.reshape(1, KCONV * ccs) for t in (conv_w, m_conv_w, v_conv_w)),
                   conv_b=(conv_b, m_conv_b, v_conv_b), dt_bias=(dt_bias, m_dt_bias, v_dt_bias),
                   a_log=(a_log, m_a_log, v_a_log), d_skip=(d_skip, m_d_skip, v_d_skip),
                   ssd_norm_w=(ssd_norm_w, m_ssd_norm_w, v_ssd_norm_w), q_norm_w=(q_norm_w, m_q_norm_w, v_q_norm_w),
                   k_norm_w=(k_norm_w, m_k_norm_w, v_k_norm_w), attn_norm_w=(attn_norm_w, m_attn_norm_w, v_attn_norm_w))
    ssz = [_pad_lanes(small_g[n]).shape[1] for n in small_names]
    soff = [sum(ssz[:i]) for i in range(len(ssz))]
    srow = -(-sum(ssz) // LANES // 8) * 8

    def pack(ts, fill):
        t = jnp.concatenate([jnp.pad(t, ((0, 0), (0, (-t.shape[1]) % LANES)), constant_values=fill) for t in ts], axis=1)
        return jnp.pad(t, ((0, 0), (0, srow * LANES - t.shape[1])), constant_values=fill).reshape(srow, LANES)

    sg = pack([small_g[n] for n in small_names], 0.0)
    sw = pack([small_w[n][0] for n in small_names], 0.0)
    sm_ = pack([small_w[n][1] for n in small_names], 0.0)
    sv = pack([small_w[n][2] for n in small_names], 1.0)
    _, s_delta, s_m, s_v = _adamw(sg[None], sw, sm_, sv, "adamw_small", tm=srow)

    def unpack(t, n):
        i = small_names.index(n)
        return t.reshape(1, srow * LANES)[:, soff[i]:soff[i] + small_g[n].shape[1]].reshape(small_w[n][0].shape)

    loss = lax.psum(loss_p[0, 0], ("x", "y", "c"))
    big_res = dict(w_ada=res_ada, w_in=res_in, w_out=res_out, w_ff1=res_ff1, w_ff2=res_ff2)
    order = ["norm1_w", "norm2_w", "w_ada", "b_ada", "w_in", "conv_w", "conv_b", "dt_bias", "a_log", "d_skip",
             "ssd_norm_w", "q_norm_w", "k_norm_w", "attn_norm_w", "w_out", "w_ff1", "w_ff2"]
    grads, deltas, new_m, new_v = [], [], [], []
    for n in order:
        if n in big_res:
            g_, d_, m_, v_ = [t[None] for t in big_res[n]]
        else:
            g_ = small_g[n].reshape(small_w[n][0].shape)
            d_, m_, v_ = unpack(s_delta, n), unpack(s_m, n), unpack(s_v, n)
            if n == "conv_w":
                g_, d_, m_, v_ = [t.reshape(conv_w.shape) for t in (g_, d_, m_, v_)]
        grads.append(g_)
        deltas.append(d_)
        new_m.append(m_)
        new_v.append(v_)
    return (loss, grad_x[None], *grads, *deltas, *new_m, *new_v)
```

```python
import functools

import jax
import jax.numpy as jnp
from jax import lax
from jax.experimental import pallas as pl
from jax.experimental.pallas import tpu as pltpu

F32, BF16 = jnp.float32, jnp.bfloat16
EPS = 1e-6
HD = 64
NH_SSD = 16
NG = 4
NSTATE = 128
KCONV = 4
CHUNK = 128
NH_ATT = 16
PATTERNS = ((128, 1), (512, 4), (2048, 16))
ABLK = 128
LANES = 128
ADAM_LR, ADAM_B1, ADAM_B2, ADAM_EPS, ADAM_WD, ADAM_STEP = 0.001, 0.9, 0.999, 1e-08, 0.01, 10
VMEM_LIMIT = 56 * 1024 * 1024
MESH = pl.DeviceIdType.MESH
NEG = -1e30

_DN = {"nn": (((1,), (0,)), ((), ())), "nt": (((1,), (1,)), ((), ())), "tn": (((0,), (0,)), ((), ()))}


def _cparams(sem):
    return pltpu.CompilerParams(dimension_semantics=sem, vmem_limit_bytes=VMEM_LIMIT)


def _tile(n, cap):
    if n % LANES or n <= LANES:
        return n
    best = LANES
    for t in range(LANES, min(n, cap) + 1, LANES):
        if n % t == 0:
            best = t
    return best


def _silu(x):
    return x / (1.0 + jnp.exp(-x))


def _softplus(x):
    return jnp.maximum(x, 0.0) + jnp.log(1.0 + jnp.exp(-jnp.abs(x)))


def _dot(a, b, dims):
    return lax.dot_general(a.astype(BF16), b.astype(BF16), _DN[dims], preferred_element_type=F32)


def _matmul(a, b, dims, out_dtype, name, a_fn=None, epilogue=None, extras=(), tm=1024, tn=1024, tk=2048,
            chip_of_tile=None):
    if dims == "nn":
        (M, K), (_, N) = a.shape, b.shape
    elif dims == "nt":
        (M, K), (N, _) = a.shape, b.shape
    else:
        (K, M), (_, N) = a.shape, b.shape
    tm, tn, tk = _tile(M, tm), _tile(N, tn), _tile(K, tk)
    nk = K // tk
    ne = len(extras)

    def body(a_ref, b_ref, *rest):
        e_refs, o_ref = rest[:ne], rest[ne]
        av = a_ref[...]
        if a_fn is not None:
            av = a_fn(av)
        part = _dot(av, b_ref[...], dims)

        def finish(r):
            if epilogue is not None:
                r = epilogue(r, *[e[...] for e in e_refs])
            o_ref[...] = r.astype(out_dtype).reshape(o_ref.shape)

        if nk == 1:
            finish(part)
            return
        acc = rest[ne + 1]
        k = pl.program_id(2)

        @pl.when(k == 0)
        def _():
            acc[...] = part

        @pl.when(k > 0)
        def _():
            acc[...] += part

        @pl.when(k == nk - 1)
        def _():
            finish(acc[...])

    a_spec = pl.BlockSpec((tk, tm), lambda i, j, k: (k, i)) if dims == "tn" else pl.BlockSpec((tm, tk), lambda i, j, k: (i, k))
    b_spec = pl.BlockSpec((tn, tk), lambda i, j, k: (j, k)) if dims == "nt" else pl.BlockSpec((tk, tn), lambda i, j, k: (k, j))
    o_spec = pl.BlockSpec((tm, tn), lambda i, j, k: (i, j))
    out_spec, out_dims = o_spec, (M, N)
    if chip_of_tile is not None:
        assert (M // tm) * (N // tn) == 4
        out_spec = pl.BlockSpec((None, tm, tn), lambda i, j, k: (chip_of_tile(i, j), 0, 0))
        out_dims = (4, tm, tn)
    return pl.pallas_call(
        body, name=name, grid=(M // tm, N // tn, nk),
        in_specs=[a_spec, b_spec] + [o_spec] * ne, out_specs=out_spec,
        out_shape=jax.ShapeDtypeStruct(out_dims, out_dtype),
        scratch_shapes=[pltpu.VMEM((tm, tn), F32)] if nk > 1 else [],
        compiler_params=_cparams(("parallel", "parallel", "arbitrary")),
    )(a, b, *extras)


def _rows(name, fn, rows, consts, outs, accs, n_rows, tm=256):
    tm = min(tm, n_rows)
    nr, nc, no, na = len(rows), len(consts), len(outs), len(accs)

    def body(*refs):
        r_refs, c_refs = refs[:nr], refs[nr:nr + nc]
        o_refs, a_refs = refs[nr + nc:nr + nc + no], refs[nr + nc + no:]
        o_vals, a_vals = fn([r[...] for r in r_refs], [c[...] for c in c_refs])
        for ref, val in zip(o_refs, o_vals):
            ref[...] = val.astype(ref.dtype)
        if na:
            @pl.when(pl.program_id(0) == 0)
            def _():
                for ref in a_refs:
                    ref[...] = jnp.zeros_like(ref)
            for ref, val in zip(a_refs, a_vals):
                ref[...] += val

    in_specs = [pl.BlockSpec((tm, w), lambda i, cb=cb: (i, cb)) for (_, cb, w) in rows]
    in_specs += [pl.BlockSpec(cst.shape, lambda i, nd=cst.ndim: (0,) * nd) for cst in consts]
    out_specs = [pl.BlockSpec((tm, w), lambda i: (i, 0)) for (w, _) in outs]
    out_specs += [pl.BlockSpec(s, lambda i: (0, 0)) for s in accs]
    out_shape = [jax.ShapeDtypeStruct((n_rows, w), dt) for (w, dt) in outs]
    out_shape += [jax.ShapeDtypeStruct(s, F32) for s in accs]
    res = pl.pallas_call(
        body, name=name, grid=(n_rows // tm,), in_specs=in_specs, out_specs=out_specs, out_shape=out_shape,
        compiler_params=_cparams(("arbitrary",)),
    )(*[r[0] for r in rows], *consts)
    return res


def _normmod(x, nw, sc, sh):
    r = lax.rsqrt(jnp.mean(x * x, axis=-1, keepdims=True) + EPS)
    return (x * r) * nw * (1.0 + sc) + sh


def _resid_normmod(x, mix, g, nw, sc, sh):
    x2 = x + g * mix
    return x2, _normmod(x2, nw, sc, sh)


def _rmsw(o, w):
    return o * lax.rsqrt(jnp.mean(o * o, axis=-1, keepdims=True) + EPS) * w


def _lane_mask():
    return lax.broadcasted_iota(jnp.int32, (1, LANES), 1) < HD


def _headnorm(t, w, scale):
    lo = _lane_mask()
    t2 = t * t
    s0 = jnp.sum(jnp.where(lo, t2, 0.0), axis=1, keepdims=True)
    s1 = jnp.sum(jnp.where(lo, 0.0, t2), axis=1, keepdims=True)
    ms = jnp.where(lo, s0, s1) * (1.0 / HD)
    return t * lax.rsqrt(ms + EPS) * w * scale


CONV_ROWS = 128
CONV_HALO = 8


def _conv_cols(n_ch):
    return _tile(n_ch, LANES)


def _conv_fwd(proj, col0, n_ch, conv_w, conv_b, name):
    S = proj.shape[0]
    tc = _conv_cols(n_ch)

    R, H = CONV_ROWS, CONV_HALO

    def body(u_ref, w_ref, b_ref, o_ref):
        w = [w_ref[i:i + 1, :] for i in range(KCONV)]
        b = b_ref[...]

        def chunk(ext):
            acc = b + w[KCONV - 1] * ext[H:]
            for i in range(KCONV - 1):
                acc = acc + w[i] * pltpu.roll(ext, KCONV - 1 - i, 0)[H:]
            return _silu(acc)

        o_ref[0:R, :] = chunk(jnp.concatenate([jnp.zeros((H, tc), F32), u_ref[0:R, :]], axis=0))

        def step(c, carry):
            r0 = pl.multiple_of(c * R, R)
            o_ref[pl.ds(r0, R), :] = chunk(u_ref[pl.ds(pl.multiple_of(r0 - H, H), R + H), :])
            return carry

        lax.fori_loop(1, S // R, step, 0)

    return pl.pallas_call(
        body, name=name, grid=(n_ch // tc,),
        in_specs=[pl.BlockSpec((S, tc), lambda j: (0, j + col0 // tc)),
                  pl.BlockSpec((KCONV, tc), lambda j: (0, j)), pl.BlockSpec((1, tc), lambda j: (0, j))],
        out_specs=pl.BlockSpec((S, tc), lambda j: (0, j)),
        out_shape=jax.ShapeDtypeStruct((S, n_ch), F32),
        compiler_params=_cparams(("parallel",)),
    )(proj, conv_w, conv_b)


def _conv_bwd(proj, col0, n_ch, conv_w, conv_b, dxbc, name):
    S = proj.shape[0]
    tc = _conv_cols(n_ch)

    R, H = CONV_ROWS, CONV_HALO

    def body(u_ref, w_ref, b_ref, g_ref, du_ref, dw_ref, db_ref):
        w = [w_ref[i:i + 1, :] for i in range(KCONV)]
        b = b_ref[...]
        pad = jnp.zeros((H, tc), F32)

        def chunk(u_ext, g_ext):
            taps = [pltpu.roll(u_ext, KCONV - 1 - i, 0)[H:] for i in range(KCONV - 1)] + [u_ext[H:]]
            acc = b
            for i in range(KCONV):
                acc = acc + w[i] * taps[i]
            sig = 1.0 / (1.0 + jnp.exp(-acc))
            dacc = g_ext * (sig * (1.0 + acc * (1.0 - sig)))
            du = w[KCONV - 1] * dacc[:R]
            for i in range(KCONV - 1):
                du = du + w[i] * pltpu.roll(dacc, R + H - (KCONV - 1 - i), 0)[:R]
            d = dacc[:R]
            return du, [jnp.sum(d * t[:R], axis=0, keepdims=True) for t in taps], jnp.sum(d, axis=0, keepdims=True)

        du, dws, db = chunk(jnp.concatenate([pad, u_ref[0:R + H, :]], axis=0), g_ref[0:R + H, :])
        du_ref[0:R, :] = du

        def step(c, carry):
            r0 = pl.multiple_of(c * R, R)
            du_c, dws_c, db_c = chunk(u_ref[pl.ds(pl.multiple_of(r0 - H, H), R + 2 * H), :], g_ref[pl.ds(r0, R + H), :])
            du_ref[pl.ds(r0, R), :] = du_c
            return [a + b_ for a, b_ in zip(carry[0], dws_c)], carry[1] + db_c

        dws, db = lax.fori_loop(1, S // R - 1, step, (dws, db))
        du, dws_l, db_l = chunk(jnp.concatenate([u_ref[S - R - H:S, :], pad], axis=0),
                                jnp.concatenate([g_ref[S - R:S, :], pad], axis=0))
        du_ref[S - R:S, :] = du
        for i in range(KCONV):
            dw_ref[i:i + 1, :] = dws[i] + dws_l[i]
        db_ref[...] = db + db_l

    return pl.pallas_call(
        body, name=name, grid=(n_ch // tc,),
        in_specs=[pl.BlockSpec((S, tc), lambda j: (0, j + col0 // tc)),
                  pl.BlockSpec((KCONV, tc), lambda j: (0, j)), pl.BlockSpec((1, tc), lambda j: (0, j)),
                  pl.BlockSpec((S, tc), lambda j: (0, j))],
        out_specs=[pl.BlockSpec((S, tc), lambda j: (0, j)), pl.BlockSpec((KCONV, tc), lambda j: (0, j)),
                   pl.BlockSpec((1, tc), lambda j: (0, j))],
        out_shape=[jax.ShapeDtypeStruct((S, n_ch), F32), jax.ShapeDtypeStruct((KCONV, n_ch), F32),
                   jax.ShapeDtypeStruct((1, n_ch), F32)],
        compiler_params=_cparams(("parallel",)),
    )(proj, conv_w, conv_b, dxbc)


@functools.partial(jax.custom_vjp, nondiff_argnums=(2,))
def _mm(a, b, dims):
    return _dot(a, b, dims)


def _mm_fwd(a, b, dims):
    return _dot(a, b, dims), (a, b)


def _mm_bwd(dims, res, g):
    a, b = res
    if dims == "nn":
        return _dot(g, b, "nt"), _dot(a, g, "tn")
    if dims == "nt":
        return _dot(g, b, "nn"), _dot(g, a, "tn")
    return _dot(b, g, "nt"), _dot(a, g, "nn")


_mm.defvjp(_mm_fwd, _mm_bwd)


def _tri_dot(x, upper):
    n = x.shape[0]
    r = lax.broadcasted_iota(jnp.int32, (n, n), 0)
    c = lax.broadcasted_iota(jnp.int32, (n, n), 1)
    t = jnp.where((r <= c) if upper else (r >= c), 1.0, 0.0)
    return lax.dot_general(t, x, _DN["nn"], precision=lax.Precision.HIGHEST, preferred_element_type=F32)


@jax.custom_vjp
def _cumsum_rows(x):
    return _tri_dot(x, False)


_cumsum_rows.defvjp(lambda x: (_tri_dot(x, False), None), lambda _, g: (_tri_dot(g, True),))


def _ssd_chunk(xs_p, bm_g, cm_g, dtr, z_p, dtb, alog, dsk, nw_p, h_p):
    L = dtr.shape[0]
    n_pairs = len(xs_p)
    ppg = n_pairs // len(bm_g)
    lane = lax.broadcasted_iota(jnp.int32, (1, LANES), 1)
    sub = lax.broadcasted_iota(jnp.int32, (LANES, 1), 0)
    lo = lane < HD
    row_l = lax.broadcasted_iota(jnp.int32, (L, 1), 0)
    tri = lax.broadcasted_iota(jnp.int32, (L, L), 0) >= lax.broadcasted_iota(jnp.int32, (L, L), 1)

    dt = _softplus(dtr + dtb)
    acs = _cumsum_rows(dt * (-jnp.exp(alog)))
    acs_t = acs.T
    a_last = jnp.sum(jnp.where(row_l == L - 1, acs, 0.0), axis=0, keepdims=True)
    e_acs = jnp.exp(acs)
    dec = jnp.exp(a_last - acs)
    cdec = jnp.exp(a_last)

    def colv(m, h):
        return jnp.sum(jnp.where(lane == h, m, 0.0), axis=1, keepdims=True)

    def rowv(mt, h):
        return jnp.sum(jnp.where(sub == h, mt, 0.0), axis=0, keepdims=True)

    def pair(m, h0):
        return jnp.where(lo, colv(m, h0), colv(m, h0 + 1))

    ys, hs = [], []
    cb = None
    for p in range(n_pairs):
        g, h0 = p // ppg, 2 * p
        bmat, cmat = bm_g[g], cm_g[g]
        if p % ppg == 0:
            cb = _mm(cmat, bmat, "nt")
        x = xs_p[p]
        xdt = x * pair(dt, h0)
        yd = []
        for h in (h0, h0 + 1):
            seg = colv(acs, h) - rowv(acs_t, h)
            lm = jnp.where(tri, jnp.exp(jnp.where(tri, seg, 0.0)), 0.0)
            yd.append(_mm(cb * lm, xdt, "nn"))
        y = jnp.where(lo, yd[0], yd[1])
        y = y + _mm(cmat, h_p[p], "nt") * pair(e_acs, h0)
        st = _mm(xdt * pair(dec, h0), bmat, "tn")
        cd_col = jnp.where(sub < HD, colv(cdec, h0), colv(cdec, h0 + 1))
        hs.append(h_p[p] * cd_col + st)
        ys.append(y + pair(dsk, h0) * x)

    y2 = [ys[p] * _silu(z_p[p]) for p in range(n_pairs)]
    outs = []
    for g in range(len(bm_g)):
        ps = range(g * ppg, (g + 1) * ppg)
        ss = sum(jnp.sum(y2[p] * y2[p], axis=1, keepdims=True) for p in ps)
        rs = lax.rsqrt(ss * (1.0 / (ppg * LANES)) + EPS)
        outs += [y2[p] * rs * nw_p[p] for p in ps]
    return outs, hs


def _ssd_slices(xbc_ref, z_ref, nw_ref, di):
    n_pairs = di // LANES
    xs_p = [xbc_ref[:, p * LANES:(p + 1) * LANES] for p in range(n_pairs)]
    bm_g = [xbc_ref[:, di + g * NSTATE:di + (g + 1) * NSTATE] for g in range(NG)]
    cm_g = [xbc_ref[:, di + (NG + g) * NSTATE:di + (NG + g + 1) * NSTATE] for g in range(NG)]
    z_p = [z_ref[:, p * LANES:(p + 1) * LANES] for p in range(n_pairs)]
    nw_p = [nw_ref[:, p * LANES:(p + 1) * LANES] for p in range(n_pairs)]
    return xs_p, bm_g, cm_g, z_p, nw_p


def _ssd_fwd(xbc, proj, dt_cb, dtb, alog, dsk, nw, name):
    S, cc = xbc.shape
    di = NH_SSD * HD
    n_pairs = di // LANES
    nchunk = S // CHUNK

    def body(xbc_ref, z_ref, dtr_ref, dtb_ref, alog_ref, dsk_ref, nw_ref, y_ref, hs_ref, h_scr):
        @pl.when(pl.program_id(0) == 0)
        def _():
            h_scr[...] = jnp.zeros_like(h_scr)

        xs_p, bm_g, cm_g, z_p, nw_p = _ssd_slices(xbc_ref, z_ref, nw_ref, di)
        h_p = [h_scr[p * LANES:(p + 1) * LANES, :] for p in range(n_pairs)]
        hs_ref[...] = h_scr[...]
        outs, hs = _ssd_chunk(xs_p, bm_g, cm_g, dtr_ref[...], z_p, dtb_ref[...], alog_ref[...], dsk_ref[...], nw_p, h_p)
        for p in range(n_pairs):
            y_ref[:, p * LANES:(p + 1) * LANES] = outs[p].astype(y_ref.dtype)
            h_scr[p * LANES:(p + 1) * LANES, :] = hs[p]

    vec = pl.BlockSpec((1, LANES), lambda c: (0, 0))
    return pl.pallas_call(
        body, name=name, grid=(nchunk,),
        in_specs=[pl.BlockSpec((CHUNK, cc), lambda c: (c, 0)), pl.BlockSpec((CHUNK, di), lambda c: (c, 0)),
                  pl.BlockSpec((CHUNK, LANES), lambda c: (c, dt_cb)), vec, vec, vec,
                  pl.BlockSpec((1, di), lambda c: (0, 0))],
        out_specs=[pl.BlockSpec((CHUNK, di), lambda c: (c, 0)), pl.BlockSpec((None, di, NSTATE), lambda c: (c, 0, 0))],
        out_shape=[jax.ShapeDtypeStruct((S, di), BF16), jax.ShapeDtypeStruct((nchunk, di, NSTATE), F32)],
        scratch_shapes=[pltpu.VMEM((di, NSTATE), F32)],
        compiler_params=_cparams(("arbitrary",)),
    )(xbc, proj, proj, dtb, alog, dsk, nw)


def _ssd_bwd(xbc, proj, dt_cb, dtb, alog, dsk, nw, hsave, dy, name):
    S, cc = xbc.shape
    di = NH_SSD * HD
    n_pairs = di // LANES
    nchunk = S // CHUNK

    def body(xbc_ref, z_ref, dtr_ref, dtb_ref, alog_ref, dsk_ref, nw_ref, hs_ref, dy_ref,
             dxbc_ref, dz_ref, ddtr_ref, ddtb_ref, dalog_ref, ddsk_ref, dnw_ref, dh_scr):
        @pl.when(pl.program_id(0) == 0)
        def _():
            dh_scr[...] = jnp.zeros_like(dh_scr)
            ddtb_ref[...] = jnp.zeros_like(ddtb_ref)
            dalog_ref[...] = jnp.zeros_like(dalog_ref)
            ddsk_ref[...] = jnp.zeros_like(ddsk_ref)
            dnw_ref[...] = jnp.zeros_like(dnw_ref)

        xs_p, bm_g, cm_g, z_p, nw_p = _ssd_slices(xbc_ref, z_ref, nw_ref, di)
        h_p = [hs_ref[p * LANES:(p + 1) * LANES, :] for p in range(n_pairs)]
        dy_p = [dy_ref[:, p * LANES:(p + 1) * LANES].astype(F32) for p in range(n_pairs)]
        dh_p = [dh_scr[p * LANES:(p + 1) * LANES, :] for p in range(n_pairs)]
        _, vjp = jax.vjp(_ssd_chunk, xs_p, bm_g, cm_g, dtr_ref[...], z_p, dtb_ref[...], alog_ref[...], dsk_ref[...],
                         nw_p, h_p)
        dxs, dbm, dcm, ddtr, dz, ddtb, dalog, ddsk, dnw, dh = vjp((dy_p, dh_p))
        for p in range(n_pairs):
            sl = slice(p * LANES, (p + 1) * LANES)
            dxbc_ref[:, sl] = dxs[p]
            dz_ref[:, sl] = dz[p]
            dnw_ref[:, sl] += dnw[p]
            dh_scr[sl, :] = dh[p]
        for g in range(NG):
            dxbc_ref[:, di + g * NSTATE:di + (g + 1) * NSTATE] = dbm[g]
            dxbc_ref[:, di + (NG + g) * NSTATE:di + (NG + g + 1) * NSTATE] = dcm[g]
        ddtr_ref[...] = ddtr
        ddtb_ref[...] += ddtb
        dalog_ref[...] += dalog
        ddsk_ref[...] += ddsk

    last = nchunk - 1
    vec = pl.BlockSpec((1, LANES), lambda c: (0, 0))
    return pl.pallas_call(
        body, name=name, grid=(nchunk,),
        in_specs=[pl.BlockSpec((CHUNK, cc), lambda c: (last - c, 0)), pl.BlockSpec((CHUNK, di), lambda c: (last - c, 0)),
                  pl.BlockSpec((CHUNK, LANES), lambda c: (last - c, dt_cb)), vec, vec, vec,
                  pl.BlockSpec((1, di), lambda c: (0, 0)),
                  pl.BlockSpec((None, di, NSTATE), lambda c: (last - c, 0, 0)),
                  pl.BlockSpec((CHUNK, di), lambda c: (last - c, 0))],
        out_specs=[pl.BlockSpec((CHUNK, cc), lambda c: (last - c, 0)), pl.BlockSpec((CHUNK, di), lambda c: (last - c, 0)),
                   pl.BlockSpec((CHUNK, LANES), lambda c: (last - c, 0)), vec, vec, vec,
                   pl.BlockSpec((1, di), lambda c: (0, 0))],
        out_shape=[jax.ShapeDtypeStruct((S, cc), F32), jax.ShapeDtypeStruct((S, di), F32),
                   jax.ShapeDtypeStruct((S, LANES), F32), jax.ShapeDtypeStruct((1, LANES), F32),
                   jax.ShapeDtypeStruct((1, LANES), F32), jax.ShapeDtypeStruct((1, LANES), F32),
                   jax.ShapeDtypeStruct((1, di), F32)],
        scratch_shapes=[pltpu.VMEM((di, NSTATE), F32)],
        compiler_params=_cparams(("arbitrary",)),
    )(xbc, proj, proj, dtb, alog, dsk, nw, hsave, dy)


def _band_masks(rows_q, rows_k):
    qi = lax.broadcasted_iota(jnp.int32, (rows_q, rows_k), 0)
    ki = lax.broadcasted_iota(jnp.int32, (rows_q, rows_k), 1)
    return qi, ki


def _class_chunks(n_rows, d):
    per_class = n_rows // d
    ch = min(per_class, 256)
    out = []
    for r in range(d):
        for c0 in range(0, per_class, ch):
            tok = pl.ds(c0, ch) if d == 1 else pl.ds(r + d * c0, ch, stride=d)
            out.append((tok, pl.ds(r * per_class + c0, ch)))
    return out


def _to_class_order(src_ref, dst_ref, n_rows, d):
    for tok, cls in _class_chunks(n_rows, d):
        dst_ref[cls, :] = src_ref[tok, :].astype(dst_ref.dtype)


def _blk_rows(t):
    return pl.ds(pl.multiple_of(t * ABLK, ABLK), ABLK)


def _head_lanes(msk, t, t_rolled):
    return jnp.where(msk, t, t_rolled)


def _zero_unless(msk, t):
    return jnp.where(msk, t, jnp.zeros_like(t))


def _attn_fwd(qn, kn, proj, v_cb, name):
    S, ad = qn.shape
    nb = S // ABLK
    nbr = len(PATTERNS)

    def body(q_ref, k_ref, v_ref, o_ref, lse_ref, qc, kc, vc, ob, mb, lb, m_s, l_s):
        lo = _lane_mask()
        qi, ki = _band_masks(ABLK, ABLK)
        cur_ok, prev_ok = ki <= qi, ki >= qi
        for bi, (_, d) in enumerate(PATTERNS):
            nbc = S // d // ABLK
            first, last = bi == 0, bi == nbr - 1
            qs, ks, vs = q_ref, k_ref, v_ref
            if d > 1:
                qs, ks, vs = qc, kc, vc
                for src, dst in ((q_ref, qc), (k_ref, kc), (v_ref, vc)):
                    _to_class_order(src, dst, S, d)
            o_dst, m_dst, l_dst = (o_ref, m_s, l_s) if first else (ob, mb, lb)

            def blk(t, carry, nbc=nbc, qs=qs, ks=ks, vs=vs, o_dst=o_dst, m_dst=m_dst, l_dst=l_dst):
                rows, prow = _blk_rows(t), _blk_rows(jnp.maximum(t - 1, 0))
                has_prev = (t % nbc) != 0
                qv = qs[rows, :]
                q2 = jnp.concatenate([_zero_unless(lo, qv), _zero_unless(jnp.logical_not(lo), qv)], axis=0).astype(BF16)
                ok_c = jnp.concatenate([cur_ok, cur_ok], axis=0)
                ok_p = jnp.concatenate([prev_ok, prev_ok], axis=0) & has_prev
                s_c = jnp.where(ok_c, _dot(q2, ks[rows, :], "nt"), NEG)
                s_p = jnp.where(ok_p, _dot(q2, ks[prow, :], "nt"), NEG)
                m = jnp.max(jnp.maximum(s_c, s_p), axis=1, keepdims=True)
                p_c, p_p = jnp.exp(s_c - m), jnp.exp(s_p - m)
                l = jnp.sum(p_c + p_p, axis=1, keepdims=True)
                o2 = _dot(p_c, vs[rows, :], "nn") + _dot(p_p, vs[prow, :], "nn")
                o_dst[rows, :] = jnp.where(lo, o2[:ABLK], o2[ABLK:])
                m_dst[rows, :] = jnp.where(lo, m[:ABLK], m[ABLK:])
                l_dst[rows, :] = jnp.where(lo, l[:ABLK], l[ABLK:])
                return carry

            lax.fori_loop(0, nb, blk, 0, unroll=8)
            if first:
                continue
            for tok, cls in _class_chunks(S, d):
                m_old, m_b = m_s[tok, :], mb[cls, :]
                m_new = jnp.maximum(m_old, m_b)
                a, b = jnp.exp(m_old - m_new), jnp.exp(m_b - m_new)
                l_new = a * l_s[tok, :] + b * lb[cls, :]
                o_new = a * o_ref[tok, :] + b * ob[cls, :]
                if last:
                    o_ref[tok, :] = o_new / l_new
                    lse_ref[tok, :] = m_new + jnp.log(l_new)
                else:
                    o_ref[tok, :] = o_new
                    m_s[tok, :] = m_new
                    l_s[tok, :] = l_new

    col = pl.BlockSpec((S, LANES), lambda h: (0, h))
    return pl.pallas_call(
        body, name=name, grid=(ad // LANES,),
        in_specs=[col, col, pl.BlockSpec((S, LANES), lambda h: (0, h + v_cb))], out_specs=[col, col],
        out_shape=[jax.ShapeDtypeStruct((S, ad), F32), jax.ShapeDtypeStruct((S, ad), F32)],
        scratch_shapes=[pltpu.VMEM((S, LANES), BF16)] * 3 + [pltpu.VMEM((S, LANES), F32)] * 5,
        compiler_params=_cparams(("parallel",)),
    )(qn, kn, proj)


def _attn_bwd(qn, kn, proj, v_cb, do, lse, dd, name):
    S, ad = qn.shape
    nb = S // ABLK

    def body(q_ref, k_ref, v_ref, do_ref, lse_ref, dd_ref, dq_ref, dk_ref, dv_ref,
             qc, kc, vc, doc, lsec, ddc, dqc, dkc, dvc):
        lo = _lane_mask()
        qi, ki = _band_masks(ABLK, ABLK)
        cur_ok, prev_ok = ki <= qi, ki >= qi
        for bi, (_, d) in enumerate(PATTERNS):
            nbc = S // d // ABLK
            first = bi == 0
            token_order = (q_ref, k_ref, v_ref, do_ref, lse_ref, dd_ref)
            class_order = (qc, kc, vc, doc, lsec, ddc)
            if d > 1:
                for src, dst in zip(token_order, class_order):
                    _to_class_order(src, dst, S, d)
            qs, ks, vs, dos, lses, dds = class_order if d > 1 else token_order
            dq_dst, dk_dst, dv_dst = (dq_ref, dk_ref, dv_ref) if first else (dqc, dkc, dvc)
            dk_dst[...] = jnp.zeros_like(dk_dst)
            dv_dst[...] = jnp.zeros_like(dv_dst)

            def blk(t, carry, nbc=nbc, qs=qs, ks=ks, vs=vs, dos=dos, lses=lses, dds=dds,
                    dq_dst=dq_dst, dk_dst=dk_dst, dv_dst=dv_dst):
                rows, prow = _blk_rows(t), _blk_rows(jnp.maximum(t - 1, 0))
                has_prev = (t % nbc) != 0
                qv, dov, lse_b, dd_b = qs[rows, :], dos[rows, :], lses[rows, :], dds[rows, :]
                lse_r, dd_r = pltpu.roll(lse_b, HD, 1), pltpu.roll(dd_b, HD, 1)
                nlo = jnp.logical_not(lo)
                q2 = jnp.concatenate([_zero_unless(lo, qv), _zero_unless(nlo, qv)], axis=0).astype(BF16)
                do2 = jnp.concatenate([_zero_unless(lo, dov), _zero_unless(nlo, dov)], axis=0).astype(BF16)
                lse2 = jnp.concatenate([_head_lanes(lo, lse_b, lse_r), _head_lanes(nlo, lse_b, lse_r)], axis=0)
                dd2 = jnp.concatenate([_head_lanes(lo, dd_b, dd_r), _head_lanes(nlo, dd_b, dd_r)], axis=0)
                dq2 = None
                for krows, vmask in ((rows, cur_ok), (prow, prev_ok & has_prev)):
                    kv, vv = ks[krows, :], vs[krows, :]
                    vmask2 = jnp.concatenate([vmask, vmask], axis=0)
                    s = jnp.where(vmask2, _dot(q2, kv, "nt"), NEG)
                    p = jnp.exp(s - lse2)
                    ds = p * (_dot(do2, vv, "nt") - dd2)
                    dqk = _dot(ds, kv, "nn")
                    dq2 = dqk if dq2 is None else dq2 + dqk
                    dv_dst[krows, :] += _dot(p, do2, "tn")
                    dk_dst[krows, :] += _dot(ds, q2, "tn")
                dq_dst[rows, :] = jnp.where(lo, dq2[:ABLK], dq2[ABLK:])
                return carry

            lax.fori_loop(0, nb, blk, 0, unroll=4)
            if not first:
                for tok, cls in _class_chunks(S, d):
                    dq_ref[tok, :] = dq_ref[tok, :] + dqc[cls, :]
                    dk_ref[tok, :] = dk_ref[tok, :] + dkc[cls, :]
                    dv_ref[tok, :] = dv_ref[tok, :] + dvc[cls, :]

    col = pl.BlockSpec((S, LANES), lambda h: (0, h))
    col1 = pl.BlockSpec((S, LANES), lambda h: (0, h), pipeline_mode=pl.Buffered(1))
    vcol1 = pl.BlockSpec((S, LANES), lambda h: (0, h + v_cb), pipeline_mode=pl.Buffered(1))
    return pl.pallas_call(
        body, name=name, grid=(ad // LANES,),
        in_specs=[col, col, vcol1, col1, col1, col1], out_specs=[col, col, col],
        out_shape=[jax.ShapeDtypeStruct((S, ad), F32)] * 3,
        scratch_shapes=[pltpu.VMEM((S, LANES), BF16)] * 4 + [pltpu.VMEM((S, LANES), F32)] * 5,
        compiler_params=_cparams(("parallel",)),
    )(qn, kn, proj, do, lse, dd)


def _coords():
    return lax.axis_index("x"), lax.axis_index("y"), lax.axis_index("c")


def _exchange8(xs, per_dest, name):
    n = len(xs)
    blk = [x.shape[1:] if per_dest else x.shape for x in xs]

    def body(*refs):
        ins, outs = refs[:n], refs[n:2 * n]
        send_sems, recv_sems, local_sems = refs[2 * n:]
        x, y, c = _coords()
        sibling = (x, y, 1 - c)
        chips = [(1 - x, y), (x, 1 - y), (1 - x, 1 - y)]
        first, passed, mine = [], [], []
        for a in range(n):
            def src_for(cx, cy, a=a):
                return ins[a].at[2 * cx + cy] if per_dest else ins[a]

            def slot(px, py, pc, a=a):
                return outs[a].at[4 * px + 2 * py + pc]

            def copy(k, src, dst, to, a=a):
                return pltpu.make_async_remote_copy(src_ref=src, dst_ref=dst, send_sem=send_sems.at[7 * a + k],
                                                    recv_sem=recv_sems.at[7 * a + k], device_id=to, device_id_type=MESH)

            m = pltpu.make_async_copy(src_for(x, y), slot(x, y, c), local_sems.at[a])
            m.start()
            mine.append(m)
            cps = [copy(0, src_for(x, y), slot(x, y, c), sibling)]
            cps += [copy(1 + j, src_for(*chip), slot(x, y, c), (*chip, c)) for j, chip in enumerate(chips)]
            for cp in cps:
                cp.start()
            first += cps
        for a in range(n):
            def slot(px, py, pc, a=a):
                return outs[a].at[4 * px + 2 * py + pc]

            def copy(k, src, dst, to, a=a):
                return pltpu.make_async_remote_copy(src_ref=src, dst_ref=dst, send_sem=send_sems.at[7 * a + k],
                                                    recv_sem=recv_sems.at[7 * a + k], device_id=to, device_id_type=MESH)

            for j, chip in enumerate(chips):
                copy(1 + j, slot(*chip, c), slot(*chip, c), (*chip, c)).wait_recv()
                fw = copy(4 + j, slot(*chip, c), slot(*chip, c), sibling)
                fw.start()
                passed.append(fw)
        for a in range(n):
            def slot(px, py, pc, a=a):
                return outs[a].at[4 * px + 2 * py + pc]

            def copy(k, src, dst, to, a=a):
                return pltpu.make_async_remote_copy(src_ref=src, dst_ref=dst, send_sem=send_sems.at[7 * a + k],
                                                    recv_sem=recv_sems.at[7 * a + k], device_id=to, device_id_type=MESH)

            copy(0, slot(x, y, 1 - c), slot(x, y, 1 - c), sibling).wait_recv()
            for j, chip in enumerate(chips):
                copy(4 + j, slot(*chip, 1 - c), slot(*chip, 1 - c), sibling).wait_recv()
        for cp in first + passed:
            cp.wait_send()
        for m in mine:
            m.wait()

    anyspec = pl.BlockSpec(memory_space=pl.ANY)
    res = pl.pallas_call(
        body, name=name, in_specs=[anyspec] * n, out_specs=[anyspec] * n,
        out_shape=[jax.ShapeDtypeStruct((8,) + tuple(b), x.dtype) for b, x in zip(blk, xs)],
        scratch_shapes=[pltpu.SemaphoreType.DMA((7 * n,)), pltpu.SemaphoreType.DMA((7 * n,)),
                        pltpu.SemaphoreType.DMA((n,))],
    )(*xs)
    return list(res)


def _pair_swap(xs, name):
    n = len(xs)

    def body(*refs):
        ins, outs = refs[:n], refs[n:2 * n]
        send_sems, recv_sems = refs[2 * n:]
        x, y, c = _coords()
        cps = [pltpu.make_async_remote_copy(src_ref=ins[a].at[1 - c], dst_ref=outs[a], send_sem=send_sems.at[a],
                                            recv_sem=recv_sems.at[a], device_id=(x, y, 1 - c), device_id_type=MESH)
               for a in range(n)]
        for cp in cps:
            cp.start()
        for cp in cps:
            cp.wait()

    anyspec = pl.BlockSpec(memory_space=pl.ANY)
    res = pl.pallas_call(
        body, name=name, in_specs=[anyspec] * n, out_specs=[anyspec] * n,
        out_shape=[jax.ShapeDtypeStruct(x.shape[1:], x.dtype) for x in xs],
        scratch_shapes=[pltpu.SemaphoreType.DMA((n,)), pltpu.SemaphoreType.DMA((n,))],
    )(*xs)
    return list(res)


_HBM = pl.BlockSpec(memory_space=pltpu.HBM)
_SEM = pl.BlockSpec(memory_space=pltpu.SEMAPHORE)
_EFFECT = pltpu.SideEffectType.DATAFLOW_SIDE_EFFECTING


def _n_peers(both):
    return 7 if both else 3


def _peer(x, y, c, j, both):
    bits = j + 1 if both else 2 * (j + 1)
    dx, dy, dc = bits >> 2 & 1, bits >> 1 & 1, bits & 1
    return (1 - x if dx else x, 1 - y if dy else y, 1 - c if dc else c)


def _spread_copies(s_refs, l_refs, send_sems, recv_sems, per_dest, both):
    x, y, c = _coords()
    me = 4 * x + 2 * y + c
    npeer = _n_peers(both)
    cps = []
    for a in range(len(s_refs)):
        for j in range(npeer):
            tx, ty, tc = _peer(x, y, c, j, both)
            src = s_refs[a].at[2 * tx + ty] if per_dest else s_refs[a]
            cps.append(pltpu.make_async_remote_copy(src_ref=src, dst_ref=l_refs[a].at[me],
                                                    send_sem=send_sems.at[npeer * a + j],
                                                    recv_sem=recv_sems.at[npeer * a + j], device_id=(tx, ty, tc),
                                                    device_id_type=MESH))
    return cps


def _sibling_fill(lands, name):
    n = len(lands)

    def body(*refs):
        outs, send_sems, recv_sems = refs[n:2 * n], refs[2 * n], refs[2 * n + 1]
        x, y, c = _coords()
        cps = [pltpu.make_async_remote_copy(src_ref=outs[a].at[2 * k + c], dst_ref=outs[a].at[2 * k + c],
                                            send_sem=send_sems.at[4 * a + k], recv_sem=recv_sems.at[4 * a + k],
                                            device_id=(x, y, 1 - c), device_id_type=MESH)
               for a in range(n) for k in range(4)]
        for cp in cps:
            cp.start()
        for cp in cps:
            cp.wait()

    anyspec = pl.BlockSpec(memory_space=pl.ANY)
    res = pl.pallas_call(
        body, name=name, in_specs=[anyspec] * n, out_specs=[anyspec] * n,
        out_shape=[jax.ShapeDtypeStruct(t.shape, t.dtype) for t in lands], input_output_aliases={i: i for i in range(n)},
        scratch_shapes=[pltpu.SemaphoreType.DMA((4 * n,)), pltpu.SemaphoreType.DMA((4 * n,))],
    )(*lands)
    return list(res)


def _spread_start(srcs, per_dest, both, dev, chip, name):
    n = len(srcs)
    npeer = _n_peers(both)
    lands = []
    for s in srcs:
        own = lax.dynamic_index_in_dim(s, chip, 0, keepdims=False) if per_dest else s
        lands.append(lax.dynamic_update_index_in_dim(lax.empty((8,) + own.shape, own.dtype), own, dev, 0))

    def body(*refs):
        s_refs, l_refs, send_sems, recv_sems, token = refs[:n], refs[n:2 * n], refs[2 * n], refs[2 * n + 1], refs[-1]
        for cp in _spread_copies(s_refs, l_refs, send_sems, recv_sems, per_dest, both):
            cp.start()
        token[...] = jnp.zeros_like(token)

    hbm_in = [pltpu.with_memory_space_constraint(t, pltpu.HBM) for t in list(srcs) + lands]
    outs = pl.pallas_call(
        body, name=name,
        out_shape=(pltpu.SemaphoreType.DMA((npeer * n,)), pltpu.SemaphoreType.DMA((npeer * n,)),
                   *[pltpu.HBM(t.shape, t.dtype) for t in hbm_in], jax.ShapeDtypeStruct((8, LANES), F32)),
        in_specs=[_HBM] * (2 * n), out_specs=(_SEM, _SEM, *[_HBM] * (2 * n), pl.BlockSpec(memory_space=pltpu.VMEM)),
        input_output_aliases={i: 2 + i for i in range(2 * n)},
        compiler_params=pltpu.CompilerParams(has_side_effects=_EFFECT),
    )(*hbm_in)
    return (outs[0], outs[1], list(outs[2:2 + n]), list(outs[2 + n:2 + 2 * n])), outs[-1]


def _spread_wait(handle, per_dest, both, after, name):
    send_sems, recv_sems, srcs, lands = handle
    n = len(srcs)

    def body(*refs):
        s_refs, l_refs, send_ref, recv_ref = refs[:n], refs[n:2 * n], refs[2 * n], refs[2 * n + 1]
        for cp in _spread_copies(s_refs, l_refs, send_ref, recv_ref, per_dest, both):
            cp.wait_send()
            cp.wait_recv()

    outs = pl.pallas_call(
        body, name=name, out_shape=tuple(pltpu.HBM(t.shape, t.dtype) for t in srcs + lands),
        in_specs=[_HBM] * (2 * n) + [_SEM, _SEM, pl.BlockSpec(memory_space=pl.ANY)], out_specs=tuple([_HBM] * (2 * n)),
        input_output_aliases={i: i for i in range(2 * n)},
        compiler_params=pltpu.CompilerParams(has_side_effects=_EFFECT),
    )(*srcs, *lands, send_sems, recv_sems, after)
    return list(outs[n:])


def _row_tile(n, cap, mult):
    best = n
    for t in range(mult, min(n, cap) + 1, mult):
        if n % t == 0:
            best = t
    return best


PAIR_ADD_BLOCK_BYTES = 2 << 20


def _pair_add(g2, theirs, half, name):
    _, n, cdim = g2.shape
    tm = _row_tile(n, max(16, PAIR_ADD_BLOCK_BYTES // (4 * cdim)), 16)

    def body(h_ref, a_ref, b_ref, o_ref):
        o_ref[...] = (a_ref[...] + b_ref[...]).astype(o_ref.dtype)

    grid_spec = pltpu.PrefetchScalarGridSpec(
        num_scalar_prefetch=1, grid=(n // tm,),
        in_specs=[pl.BlockSpec((None, tm, cdim), lambda i, h: (h[0], i, 0)), pl.BlockSpec((tm, cdim), lambda i, h: (i, 0))],
        out_specs=pl.BlockSpec((tm, cdim), lambda i, h: (i, 0)))
    return pl.pallas_call(body, name=name, grid_spec=grid_spec, out_shape=jax.ShapeDtypeStruct((n, cdim), BF16),
                          compiler_params=_cparams(("parallel",)))(half.reshape(1).astype(jnp.int32), g2, theirs)


def _adamw_math(w, g, m, v):
    m = ADAM_B1 * m + (1.0 - ADAM_B1) * g
    v = ADAM_B2 * v + (1.0 - ADAM_B2) * (g * g)
    m_hat = m / (1.0 - ADAM_B1 ** ADAM_STEP)
    v_hat = v / (1.0 - ADAM_B2 ** ADAM_STEP)
    delta = -ADAM_LR * (m_hat / (jnp.sqrt(v_hat) + ADAM_EPS) + ADAM_WD * w)
    return delta, m, v


def _adamw(parts, w, m, v, name, tm=128):
    npart, R, C = parts.shape
    tm = min(tm, R)

    def body(p_ref, w_ref, m_ref, v_ref, g_out, d_out, m_out, v_out):
        g = p_ref[0].astype(F32)
        for i in range(1, npart):
            g = g + p_ref[i].astype(F32)
        d, mm, vv = _adamw_math(w_ref[...], g, m_ref[...], v_ref[...])
        g_out[...] = g
        d_out[...] = d
        m_out[...] = mm
        v_out[...] = vv

    spec = pl.BlockSpec((tm, C), lambda i: (i, 0))
    return pl.pallas_call(
        body, name=name, grid=(R // tm,),
        in_specs=[pl.BlockSpec((npart, tm, C), lambda i: (0, i, 0)), spec, spec, spec], out_specs=[spec] * 4,
        out_shape=[jax.ShapeDtypeStruct((R, C), F32)] * 4,
        compiler_params=_cparams(("parallel",)),
    )(parts, w, m, v)


def _sum_parts(parts, name):
    npart, R, C = parts.shape

    def body(p_ref, o_ref):
        g = p_ref[0]
        for i in range(1, npart):
            g = g + p_ref[i]
        o_ref[...] = g

    return pl.pallas_call(body, name=name, out_shape=jax.ShapeDtypeStruct((R, C), F32))(parts)


def _mod_fwd(c_all, w_ada, b_sh, name):
    def body(c_ref, w_ref, b_ref, o_ref):
        o_ref[...] = _dot(_silu(c_ref[...]), w_ref[...], "nn") + b_ref[...]

    return pl.pallas_call(body, name=name, out_shape=jax.ShapeDtypeStruct((c_all.shape[0], w_ada.shape[1]), F32),
                          compiler_params=pltpu.CompilerParams(vmem_limit_bytes=VMEM_LIMIT))(c_all, w_ada, b_sh)


def _mod_wgrad(c_all, dmod_sh, name):
    def body(c_ref, d_ref, o_ref):
        o_ref[...] = _dot(_silu(c_ref[...]), d_ref[...], "tn")

    return pl.pallas_call(body, name=name, out_shape=jax.ShapeDtypeStruct((c_all.shape[1], dmod_sh.shape[1]), F32),
                          compiler_params=pltpu.CompilerParams(vmem_limit_bytes=VMEM_LIMIT))(c_all, dmod_sh)


def _pad_lanes(v):
    return jnp.pad(v, ((0, 0), (0, (-v.shape[1]) % LANES)))


def kernel(x, c, norm1_w, norm2_w, w_ada, b_ada, w_in, conv_w, conv_b, dt_bias, a_log, d_skip, ssd_norm_w, q_norm_w, k_norm_w, attn_norm_w, w_out, w_ff1, w_ff2, loss_target, m_norm1_w, m_norm2_w, m_w_ada, m_b_ada, m_w_in, m_conv_w, m_conv_b, m_dt_bias, m_a_log, m_d_skip, m_ssd_norm_w, m_q_norm_w, m_k_norm_w, m_attn_norm_w, m_w_out, m_w_ff1, m_w_ff2, v_norm1_w, v_norm2_w, v_w_ada, v_b_ada, v_w_in, v_conv_w, v_conv_b, v_dt_bias, v_a_log, v_d_skip, v_ssd_norm_w, v_q_norm_w, v_k_norm_w, v_attn_norm_w, v_w_out, v_w_ff1, v_w_ff2):
    xi, yi, ci = _coords()
    chip = 2 * xi + yi
    dev = 2 * chip + ci
    xs, tgt = x[0], loss_target[0]
    S, D = xs.shape
    DI, AD = NH_SSD * HD, NH_ATT * HD
    CC = DI + 2 * NG * NSTATE
    PW = DI + CC + 3 * AD + LANES
    DFF = w_ff1.shape[2] * 4
    MIX = DI + AD
    o_xbc, o_q, o_k, o_v, o_dt = DI, DI + CC, DI + CC + AD, DI + CC + 2 * AD, DI + CC + 3 * AD

    def half_rows(w):
        r = w.shape[0] // 2
        return lax.dynamic_slice_in_dim(w, ci * r, r, 0).astype(BF16)

    c_all, conv_w_all = _exchange8([c, conv_w[0]], False, "gather_c_conv_w")
    c_all = c_all.reshape(8, D)
    c_all = jnp.pad(c_all, ((0, 8), (0, 0)))
    nmod = w_ada.shape[2]
    b_sh = lax.dynamic_slice_in_dim(b_ada, chip * nmod, nmod, 1)
    mod_sh = _mod_fwd(c_all, w_ada[0], b_sh, "mod_fwd")
    mod_all = _exchange8([mod_sh[:8]], False, "gather_mod")[0]
    mod_me = lax.dynamic_index_in_dim(mod_all[0::2], dev, 1, keepdims=False).reshape(1, 4 * nmod)
    shift1, scale1, gate1, shift2, scale2, gate2 = [mod_me[:, i * D:(i + 1) * D] for i in range(6)]

    g_in = _exchange8([half_rows(w_in[0])], False, "gather_w_in")[0]
    rest_handle, rest_token = _spread_start([half_rows(w_out[0]), half_rows(w_ff1[0]), half_rows(w_ff2[0])], False, True,
                                            dev, chip, "gather_rest_start")
    shift1 = shift1 + rest_token[0, 0]
    wsh = w_in.shape[2]
    w_in_f = g_in.reshape(4, D, wsh).transpose(1, 0, 2).reshape(D, 4 * wsh)
    n_zx = DI + CC
    w_proj = jnp.concatenate([w_in_f[:, :n_zx], w_in_f[:, n_zx + NH_SSD:], w_in_f[:, n_zx:n_zx + NH_SSD],
                              jnp.zeros((D, LANES - NH_SSD), BF16)], axis=1)

    dtb, alog, dsk = _pad_lanes(dt_bias), _pad_lanes(a_log), _pad_lanes(d_skip)
    qw2 = jnp.concatenate([q_norm_w, q_norm_w], axis=1)
    kw2 = jnp.concatenate([k_norm_w, k_norm_w], axis=1)
    conv_w_f = conv_w_all[0::2].transpose(1, 0, 2).reshape(KCONV, CC)

    h1 = _rows("norm1", lambda r, k: ([_normmod(r[0], *k)], []), [(xs, 0, D)], [norm1_w, scale1, shift1],
               [(D, BF16)], [], S)[0]
    proj = _matmul(h1, w_proj, "nn", F32, "in_proj", tn=896)
    xbc = _conv_fwd(proj, o_xbc, CC, conv_w_f, conv_b, "conv_fwd")
    y_ssd, hsave = _ssd_fwd(xbc, proj, o_dt // LANES, dtb, alog, dsk, ssd_norm_w, "ssd_fwd")

    def qk_call(name, col0, w2, scale):
        def body(t_ref, w_ref, o_ref):
            o_ref[...] = _headnorm(t_ref[...], w_ref[...], scale)
        return pl.pallas_call(
            body, name=name, grid=(AD // LANES,),
            in_specs=[pl.BlockSpec((S, LANES), lambda j: (0, j + col0 // LANES)),
                      pl.BlockSpec((1, LANES), lambda j: (0, 0))],
            out_specs=pl.BlockSpec((S, LANES), lambda j: (0, j)),
            out_shape=jax.ShapeDtypeStruct((S, AD), F32), compiler_params=_cparams(("parallel",)),
        )(proj, w2)

    qn = qk_call("q_norm", o_q, qw2, HD ** -0.5)
    kn = qk_call("k_norm", o_k, kw2, 1.0)
    o_att, lse = _attn_fwd(qn, kn, proj, o_v // LANES, "attn_fwd")
    y_att = _rows("attn_out_norm", lambda r, k: ([_rmsw(r[0], k[0])], []), [(o_att, 0, AD)], [attn_norm_w],
                  [(AD, BF16)], [], S)[0]
    g_out, g_ff1, g_ff2 = _spread_wait(rest_handle, False, True, o_att, "gather_rest_wait")
    w_out_f = g_out.reshape(MIX, D)
    w_out_a, w_out_b = w_out_f[:DI], w_out_f[DI:]
    w_ff1_f = g_ff1.reshape(4, D, DFF // 4).transpose(1, 0, 2).reshape(D, DFF)
    w_ff2_f = g_ff2.reshape(DFF, D)
    mix_a = _matmul(y_ssd, w_out_a, "nn", F32, "out_proj_a")
    mix = _matmul(y_att, w_out_b, "nn", F32, "out_proj_b", epilogue=lambda r, e: r + e, extras=(mix_a,))
    x2, h2 = _rows("resid_norm2", lambda r, k: (list(_resid_normmod(r[0], r[1], *k)), []), [(xs, 0, D), (mix, 0, D)],
                   [gate1, norm2_w, scale2, shift2], [(D, F32), (D, BF16)], [], S)
    u = _matmul(h2, w_ff1_f, "nn", F32, "ff1")
    relu2 = lambda t: jnp.square(jnp.maximum(t, 0.0))
    ff = _matmul(u, w_ff2_f, "nn", F32, "ff2", a_fn=relu2)

    def loss_fn(r, k):
        x2_, ff_, t_ = r
        err = x2_ + k[0] * ff_ - t_
        dy_ = err * (1.0 / D)
        ls = jnp.sum(jnp.sum(0.5 * err * err, axis=1, keepdims=True), axis=0, keepdims=True) * (1.0 / D)
        return [dy_, dy_ * k[0]], [ls, jnp.sum(dy_ * ff_, axis=0, keepdims=True)]

    dy, dff, loss_p, dgate2 = _rows("loss", loss_fn, [(x2, 0, D), (ff, 0, D), (tgt, 0, D)], [gate2],
                                    [(D, F32), (D, BF16)], [(1, 1), (1, D)], S)
    du = _matmul(dff, w_ff2_f, "nt", BF16, "ff2_dx", epilogue=lambda r, e: r * (2.0 * jnp.maximum(e, 0.0)), extras=(u,))
    gw_ff2 = _matmul(u, dff, "tn", BF16, "ff2_dw", a_fn=relu2, tm=DFF // 4, tn=D, chip_of_tile=lambda i, j: i)
    gw_ff1 = _matmul(h2, du, "tn", BF16, "ff1_dw", tm=D, tn=DFF // 4, tk=S, chip_of_tile=lambda i, j: j)
    ff_handle, ff_token = _spread_start([gw_ff1, gw_ff2], True, True, dev, chip, "scatter_ff_start")
    dh2 = _matmul(du, w_ff1_f, "nt", F32, "ff1_dx")

    def resid_bwd(r, k):
        x_, mix_, dx2a, dh2_ = r
        _, vjp = jax.vjp(_resid_normmod, x_, mix_, *k)
        dx, dmix_, dg, dnw, dsc, dsh = vjp((dx2a, dh2_))
        return [dx, dmix_], [dg, dnw, dsc, dsh]

    dx2, dmix, dgate1, g_norm2, dscale2, dshift2 = _rows(
        "resid_norm2_bwd", resid_bwd, [(xs, 0, D), (mix, 0, D), (dy, 0, D), (dh2, 0, D)],
        [gate1 + ff_token[0, 0], norm2_w, scale2, shift2], [(D, F32), (D, BF16)], [(1, D)] * 4, S)
    gw_out = jnp.concatenate([_matmul(y_ssd, dmix, "tn", BF16, "out_proj_dw_a"),
                              _matmul(y_att, dmix, "tn", BF16, "out_proj_dw_b")], axis=0)
    out_handle, out_token = _spread_start([gw_out.reshape(4, MIX // 4, D)], True, True, dev, chip, "scatter_out_start")
    dy_ssd = _matmul(dmix, w_out_a, "nt", F32, "out_proj_dx_a")
    dy_att = _matmul(dmix, w_out_b, "nt", F32, "out_proj_dx_b")

    def attn_norm_bwd(r, k):
        o_, dyo = r
        _, vjp = jax.vjp(_rmsw, o_, k[0])
        do_, dw_ = vjp(dyo)
        lo = _lane_mask()
        dd_blocks = []
        for b in range(AD // LANES):
            t = (do_ * o_)[:, b * LANES:(b + 1) * LANES]
            s0 = jnp.sum(jnp.where(lo, t, 0.0), axis=1, keepdims=True)
            s1 = jnp.sum(jnp.where(lo, 0.0, t), axis=1, keepdims=True)
            dd_blocks.append(jnp.where(lo, s0, s1))
        return [do_, jnp.concatenate(dd_blocks, axis=1)], [dw_]

    do_att, dd_att, g_attn_norm = _rows("attn_norm_bwd", attn_norm_bwd, [(o_att, 0, AD), (dy_att, 0, AD)],
                                        [attn_norm_w + out_token[0, 0]], [(AD, F32), (AD, F32)], [(1, AD)], S)
    dq_n, dk_n, dv = _attn_bwd(qn, kn, proj, o_v // LANES, do_att, lse, dd_att, "attn_bwd")

    def qk_bwd_call(name, col0, w2, scale, g):
        def body(t_ref, w_ref, g_ref, o_ref, dw_ref):
            @pl.when(pl.program_id(0) == 0)
            def _():
                dw_ref[...] = jnp.zeros_like(dw_ref)
            _, vjp = jax.vjp(lambda t, w: _headnorm(t, w, scale), t_ref[...], w_ref[...])
            dt_, dw_ = vjp(g_ref[...])
            o_ref[...] = dt_.astype(BF16)
            dw_ref[...] += dw_
        blk = pl.BlockSpec((S, LANES), lambda j: (0, j))
        return pl.pallas_call(
            body, name=name, grid=(AD // LANES,),
            in_specs=[pl.BlockSpec((S, LANES), lambda j: (0, j + col0 // LANES)),
                      pl.BlockSpec((1, LANES), lambda j: (0, 0)), blk],
            out_specs=[blk, pl.BlockSpec((1, LANES), lambda j: (0, 0))],
            out_shape=[jax.ShapeDtypeStruct((S, AD), BF16), jax.ShapeDtypeStruct((1, LANES), F32)],
            compiler_params=_cparams(("arbitrary",)),
        )(proj, w2, g)

    dq, g_qw2 = qk_bwd_call("q_norm_bwd", o_q, qw2, HD ** -0.5, dq_n)
    dk, g_kw2 = qk_bwd_call("k_norm_bwd", o_k, kw2, 1.0, dk_n)
    g_q_norm = g_qw2[:, :HD] + g_qw2[:, HD:]
    g_k_norm = g_kw2[:, :HD] + g_kw2[:, HD:]

    dxbc, dz, ddtr, g_dtb, g_alog, g_dsk, g_ssd_norm = _ssd_bwd(
        xbc, proj, o_dt // LANES, dtb, alog, dsk, ssd_norm_w, hsave, dy_ssd, "ssd_bwd")
    dxbc_pre, g_conv_w, g_conv_b = _conv_bwd(proj, o_xbc, CC, conv_w_f, conv_b, dxbc, "conv_bwd")
    dproj = jnp.concatenate([dz.astype(BF16), dxbc_pre.astype(BF16), dq, dk, dv.astype(BF16), ddtr.astype(BF16)], axis=1)
    gw_proj = _matmul(h1, dproj, "tn", F32, "in_proj_dw", tn=896, tk=S)
    gw_halves = gw_proj.reshape(2, D // 2, PW)
    sum_p = _pair_add(gw_halves, _pair_swap([gw_halves], "pair_swap_in")[0], ci, "pair_add_in")
    sum_in = jnp.concatenate([sum_p[:, :n_zx], sum_p[:, o_dt:o_dt + NH_SSD], sum_p[:, n_zx:o_dt]], axis=1)
    in_handle, in_token = _spread_start([sum_in.reshape(D // 2, 4, wsh).transpose(1, 0, 2)], True, False, dev, chip,
                                        "scatter_in_start")
    dh1 = _matmul(dproj, w_proj, "nt", F32, "in_proj_dx", tm=512, tk=PW)

    def norm1_bwd(r, k):
        x_, dh_, dres = r
        _, vjp = jax.vjp(_normmod, x_, *k)
        dx, dnw, dsc, dsh = vjp(dh_)
        return [dx + dres], [dnw, dsc, dsh]

    grad_x, g_norm1, dscale1, dshift1 = _rows("norm1_bwd", norm1_bwd, [(xs, 0, D), (dh1, 0, D), (dx2, 0, D)],
                                              [norm1_w + in_token[0, 0], scale1, shift1], [(D, F32)], [(1, D)] * 3, S)
    dmod =jnp.concatenate([dshift1, dscale1, dgate1, dshift2, dscale2, dgate2], axis=1)

    small = [g_norm1, g_norm2, dmod, g_conv_b, g_dtb, g_alog, g_dsk, g_ssd_norm, _pad_lanes(g_q_norm),
             _pad_lanes(g_k_norm), g_attn_norm, g_conv_w.reshape(1, KCONV * CC)]
    sizes = [t.shape[1] for t in small]
    packed = jnp.concatenate(small, axis=1)
    nrow = -(-packed.shape[1] // LANES // 8) * 8
    packed = jnp.pad(packed, ((0, 0), (0, nrow * LANES - packed.shape[1]))).reshape(nrow, LANES)
    packed_all = _exchange8([packed], False, "gather_small_grads")[0]
    tot = _sum_parts(packed_all, "sum_small_grads").reshape(1, nrow * LANES)
    offs = [sum(sizes[:i]) for i in range(len(sizes))]
    (g_norm1, g_norm2, g_b_ada, g_conv_b, g_dtb, g_alog, g_dsk, g_ssd_norm, g_q_norm, g_k_norm, g_attn_norm,
     g_conv_w) = [tot[:, o:o + n] for o, n in zip(offs, sizes)]
    g_dtb, g_alog, g_dsk = g_dtb[:, :NH_SSD], g_alog[:, :NH_SSD], g_dsk[:, :NH_SSD]
    g_q_norm, g_k_norm = g_q_norm[:, :HD], g_k_norm[:, :HD]
    ccs = CC // 4
    g_conv_w = lax.dynamic_slice_in_dim(g_conv_w.reshape(KCONV, CC), chip * ccs, ccs, 1)

    dmod_all = packed_all.reshape(8, nrow * LANES)[:, offs[2]:offs[2] + 6 * D]
    dmod_sh = jnp.pad(lax.dynamic_slice_in_dim(dmod_all, chip * nmod, nmod, 1), ((0, 8), (0, 0)))
    gw_ada = _mod_wgrad(c_all, dmod_sh, "mod_wgrad")

    parts_ff1, parts_ff2 = _spread_wait(ff_handle, True, True, in_token, "scatter_ff_wait")
    res_ff1 = _adamw(parts_ff1, w_ff1[0], m_w_ff1[0], v_w_ff1[0], "adamw_w_ff1")
    res_ff2 = _adamw(parts_ff2, w_ff2[0], m_w_ff2[0], v_w_ff2[0], "adamw_w_ff2")
    parts_out = _spread_wait(out_handle, True, True, in_token, "scatter_out_wait")[0]
    res_out = _adamw(parts_out, w_out[0], m_w_out[0], v_w_out[0], "adamw_w_out")
    res_ada = _adamw(gw_ada[None], w_ada[0], m_w_ada[0], v_w_ada[0], "adamw_w_ada")
    lands_in = _sibling_fill(_spread_wait(in_handle, True, False, res_ada[0], "scatter_in_wait"), "scatter_in_fill")[0]
    res_in = _adamw(lands_in.reshape(4, D, wsh), w_in[0], m_w_in[0], v_w_in[0], "adamw_w_in")

    small_names = ["norm1_w", "norm2_w", "b_ada", "conv_w", "conv_b", "dt_bias", "a_log", "d_skip", "ssd_norm_w",
                   "q_norm_w", "k_norm_w", "attn_norm_w"]
    small_g = dict(norm1_w=g_norm1, norm2_w=g_norm2, b_ada=g_b_ada, conv_w=g_conv_w.reshape(1, KCONV * ccs),
                   conv_b=g_conv_b, dt_bias=g_dtb, a_log=g_alog, d_skip=g_dsk, ssd_norm_w=g_ssd_norm, q_norm_w=g_q_norm,
                   k_norm_w=g_k_norm, attn_norm_w=g_attn_norm)
    small_w = dict(norm1_w=(norm1_w, m_norm1_w, v_norm1_w), norm2_w=(norm2_w, m_norm2_w, v_norm2_w),
                   b_ada=(b_ada, m_b_ada, v_b_ada),
                   conv_w=tuple(t.reshape(1, KCONV * ccs) for t in (conv_w, m_conv_w, v_conv_w)),
                   conv_b=(conv_b, m_conv_b, v_conv_b), dt_bias=(dt_bias, m_dt_bias, v_dt_bias),
                   a_log=(a_log, m_a_log, v_a_log), d_skip=(d_skip, m_d_skip, v_d_skip),
                   ssd_norm_w=(ssd_norm_w, m_ssd_norm_w, v_ssd_norm_w), q_norm_w=(q_norm_w, m_q_norm_w, v_q_norm_w),
                   k_norm_w=(k_norm_w, m_k_norm_w, v_k_norm_w), attn_norm_w=(attn_norm_w, m_attn_norm_w, v_attn_norm_w))
    ssz = [_pad_lanes(small_g[n]).shape[1] for n in small_names]
    soff = [sum(ssz[:i]) for i in range(len(ssz))]
    srow = -(-sum(ssz) // LANES // 8) * 8

    def pack(ts, fill):
        t = jnp.concatenate([jnp.pad(t, ((0, 0), (0, (-t.shape[1]) % LANES)), constant_values=fill) for t in ts], axis=1)
        return jnp.pad(t, ((0, 0), (0, srow * LANES - t.shape[1])), constant_values=fill).reshape(srow, LANES)

    sg = pack([small_g[n] for n in small_names], 0.0)
    sw = pack([small_w[n][0] for n in small_names], 0.0)
    sm_ = pack([small_w[n][1] for n in small_names], 0.0)
    sv = pack([small_w[n][2] for n in small_names], 1.0)
    _, s_delta, s_m, s_v = _adamw(sg[None], sw, sm_, sv, "adamw_small", tm=srow)

    def unpack(t, n):
        i = small_names.index(n)
        return t.reshape(1, srow * LANES)[:, soff[i]:soff[i] + small_g[n].shape[1]].reshape(small_w[n][0].shape)

    loss = lax.psum(loss_p[0, 0], ("x", "y", "c"))
    big_res = dict(w_ada=res_ada, w_in=res_in, w_out=res_out, w_ff1=res_ff1, w_ff2=res_ff2)
    order = ["norm1_w", "norm2_w", "w_ada", "b_ada", "w_in", "conv_w", "conv_b", "dt_bias", "a_log", "d_skip",
             "ssd_norm_w", "q_norm_w", "k_norm_w", "attn_norm_w", "w_out", "w_ff1", "w_ff2"]
    grads, deltas, new_m, new_v = [], [], [], []
    for n in order:
        if n in big_res:
            g_, d_, m_, v_ = [t[None] for t in big_res[n]]
        else:
            g_ = small_g[n].reshape(small_w[n][0].shape)
            d_, m_, v_ = unpack(s_delta, n), unpack(s_m, n), unpack(s_v, n)
            if n == "conv_w":
                g_, d_, m_, v_ = [t.reshape(conv_w.shape) for t in (g_, d_, m_, v_)]
        grads.append(g_)
        deltas.append(d_)
        new_m.append(m_)
        new_v.append(v_)
    return (loss, grad_x[None], *grads, *deltas, *new_m, *new_v)
```

```python
import functools

import jax
import jax.numpy as jnp
from jax import lax
from jax.experimental import pallas as pl
from jax.experimental.pallas import tpu as pltpu

F32, BF16 = jnp.float32, jnp.bfloat16
EPS = 1e-6
HD = 64
NH_SSD = 16
NG = 4
NSTATE = 128
KCONV = 4
CHUNK = 128
NH_ATT = 16
PATTERNS = ((128, 1), (512, 4), (2048, 16))
ABLK = 128
LANES = 128
ADAM_LR, ADAM_B1, ADAM_B2, ADAM_EPS, ADAM_WD, ADAM_STEP = 0.001, 0.9, 0.999, 1e-08, 0.01, 10
VMEM_LIMIT = 56 * 1024 * 1024
MESH = pl.DeviceIdType.MESH
NEG = -1e30

_DN = {"nn": (((1,), (0,)), ((), ())), "nt": (((1,), (1,)), ((), ())), "tn": (((0,), (0,)), ((), ()))}


def _cparams(sem):
    return pltpu.CompilerParams(dimension_semantics=sem, vmem_limit_bytes=VMEM_LIMIT)


def _tile(n, cap):
    if n % LANES or n <= LANES:
        return n
    best = LANES
    for t in range(LANES, min(n, cap) + 1, LANES):
        if n % t == 0:
            best = t
    return best


def _silu(x):
    return x / (1.0 + jnp.exp(-x))


def _softplus(x):
    return jnp.maximum(x, 0.0) + jnp.log(1.0 + jnp.exp(-jnp.abs(x)))


def _dot(a, b, dims):
    return lax.dot_general(a.astype(BF16), b.astype(BF16), _DN[dims], preferred_element_type=F32)


def _matmul(a, b, dims, out_dtype, name, a_fn=None, epilogue=None, extras=(), tm=1024, tn=1024, tk=2048,
            chip_of_tile=None):
    if dims == "nn":
        (M, K), (_, N) = a.shape, b.shape
    elif dims == "nt":
        (M, K), (N, _) = a.shape, b.shape
    else:
        (K, M), (_, N) = a.shape, b.shape
    tm, tn, tk = _tile(M, tm), _tile(N, tn), _tile(K, tk)
    nk = K // tk
    ne = len(extras)

    def body(a_ref, b_ref, *rest):
        e_refs, o_ref = rest[:ne], rest[ne]
        av = a_ref[...]
        if a_fn is not None:
            av = a_fn(av)
        part = _dot(av, b_ref[...], dims)

        def finish(r):
            if epilogue is not None:
                r = epilogue(r, *[e[...] for e in e_refs])
            o_ref[...] = r.astype(out_dtype).reshape(o_ref.shape)

        if nk == 1:
            finish(part)
            return
        acc = rest[ne + 1]
        k = pl.program_id(2)

        @pl.when(k == 0)
        def _():
            acc[...] = part

        @pl.when(k > 0)
        def _():
            acc[...] += part

        @pl.when(k == nk - 1)
        def _():
            finish(acc[...])

    a_spec = pl.BlockSpec((tk, tm), lambda i, j, k: (k, i)) if dims == "tn" else pl.BlockSpec((tm, tk), lambda i, j, k: (i, k))
    b_spec = pl.BlockSpec((tn, tk), lambda i, j, k: (j, k)) if dims == "nt" else pl.BlockSpec((tk, tn), lambda i, j, k: (k, j))
    o_spec = pl.BlockSpec((tm, tn), lambda i, j, k: (i, j))
    out_spec, out_dims = o_spec, (M, N)
    if chip_of_tile is not None:
        assert (M // tm) * (N // tn) == 4
        out_spec = pl.BlockSpec((None, tm, tn), lambda i, j, k: (chip_of_tile(i, j), 0, 0))
        out_dims = (4, tm, tn)
    return pl.pallas_call(
        body, name=name, grid=(M // tm, N // tn, nk),
        in_specs=[a_spec, b_spec] + [o_spec] * ne, out_specs=out_spec,
        out_shape=jax.ShapeDtypeStruct(out_dims, out_dtype),
        scratch_shapes=[pltpu.VMEM((tm, tn), F32)] if nk > 1 else [],
        compiler_params=_cparams(("parallel", "parallel", "arbitrary")),
    )(a, b, *extras)


def _rows(name, fn, rows, consts, outs, accs, n_rows, tm=256):
    tm = min(tm, n_rows)
    nr, nc, no, na = len(rows), len(consts), len(outs), len(accs)

    def body(*refs):
        r_refs, c_refs = refs[:nr], refs[nr:nr + nc]
        o_refs, a_refs = refs[nr + nc:nr + nc + no], refs[nr + nc + no:]
        o_vals, a_vals = fn([r[...] for r in r_refs], [c[...] for c in c_refs])
        for ref, val in zip(o_refs, o_vals):
            ref[...] = val.astype(ref.dtype)
        if na:
            @pl.when(pl.program_id(0) == 0)
            def _():
                for ref in a_refs:
                    ref[...] = jnp.zeros_like(ref)
            for ref, val in zip(a_refs, a_vals):
                ref[...] += val

    in_specs = [pl.BlockSpec((tm, w), lambda i, cb=cb: (i, cb)) for (_, cb, w) in rows]
    in_specs += [pl.BlockSpec(cst.shape, lambda i, nd=cst.ndim: (0,) * nd) for cst in consts]
    out_specs = [pl.BlockSpec((tm, w), lambda i: (i, 0)) for (w, _) in outs]
    out_specs += [pl.BlockSpec(s, lambda i: (0, 0)) for s in accs]
    out_shape = [jax.ShapeDtypeStruct((n_rows, w), dt) for (w, dt) in outs]
    out_shape += [jax.ShapeDtypeStruct(s, F32) for s in accs]
    res = pl.pallas_call(
        body, name=name, grid=(n_rows // tm,), in_specs=in_specs, out_specs=out_specs, out_shape=out_shape,
        compiler_params=_cparams(("arbitrary",)),
    )(*[r[0] for r in rows], *consts)
    return res


def _normmod(x, nw, sc, sh):
    r = lax.rsqrt(jnp.mean(x * x, axis=-1, keepdims=True) + EPS)
    return (x * r) * nw * (1.0 + sc) + sh


def _resid_normmod(x, mix, g, nw, sc, sh):
    x2 = x + g * mix
    return x2, _normmod(x2, nw, sc, sh)


def _rmsw(o, w):
    return o * lax.rsqrt(jnp.mean(o * o, axis=-1, keepdims=True) + EPS) * w


def _lane_mask():
    return lax.broadcasted_iota(jnp.int32, (1, LANES), 1) < HD


def _headnorm(t, w, scale):
    lo = _lane_mask()
    t2 = t * t
    s0 = jnp.sum(jnp.where(lo, t2, 0.0), axis=1, keepdims=True)
    s1 = jnp.sum(jnp.where(lo, 0.0, t2), axis=1, keepdims=True)
    ms = jnp.where(lo, s0, s1) * (1.0 / HD)
    return t * lax.rsqrt(ms + EPS) * w * scale


NORM_ROWS = 256
CONV_ROWS = 128
CONV_HALO = 8


def _conv_cols(n_ch):
    return _tile(n_ch, LANES)


def _conv_fwd(proj, col0, n_ch, conv_w, conv_b, name):
    S = proj.shape[0]
    tc = _conv_cols(n_ch)

    R, H = CONV_ROWS, CONV_HALO

    def body(u_ref, w_ref, b_ref, o_ref):
        w = [w_ref[i:i + 1, :] for i in range(KCONV)]
        b = b_ref[...]

        def chunk(ext):
            acc = b + w[KCONV - 1] * ext[H:]
            for i in range(KCONV - 1):
                acc = acc + w[i] * pltpu.roll(ext, KCONV - 1 - i, 0)[H:]
            return _silu(acc)

        o_ref[0:R, :] = chunk(jnp.concatenate([jnp.zeros((H, tc), F32), u_ref[0:R, :]], axis=0))

        def step(c, carry):
            r0 = pl.multiple_of(c * R, R)
            o_ref[pl.ds(r0, R), :] = chunk(u_ref[pl.ds(pl.multiple_of(r0 - H, H), R + H), :])
            return carry

        lax.fori_loop(1, S // R, step, 0)

    return pl.pallas_call(
        body, name=name, grid=(n_ch // tc,),
        in_specs=[pl.BlockSpec((S, tc), lambda j: (0, j + col0 // tc)),
                  pl.BlockSpec((KCONV, tc), lambda j: (0, j)), pl.BlockSpec((1, tc), lambda j: (0, j))],
        out_specs=pl.BlockSpec((S, tc), lambda j: (0, j)),
        out_shape=jax.ShapeDtypeStruct((S, n_ch), F32),
        compiler_params=_cparams(("parallel",)),
    )(proj, conv_w, conv_b)


def _conv_bwd(proj, col0, n_ch, conv_w, conv_b, dxbc, name):
    S = proj.shape[0]
    tc = _conv_cols(n_ch)

    R, H = CONV_ROWS, CONV_HALO

    def body(u_ref, w_ref, b_ref, g_ref, du_ref, dw_ref, db_ref):
        w = [w_ref[i:i + 1, :] for i in range(KCONV)]
        b = b_ref[...]
        pad = jnp.zeros((H, tc), F32)

        def chunk(u_ext, g_ext):
            taps = [pltpu.roll(u_ext, KCONV - 1 - i, 0)[H:] for i in range(KCONV - 1)] + [u_ext[H:]]
            acc = b
            for i in range(KCONV):
                acc = acc + w[i] * taps[i]
            sig = 1.0 / (1.0 + jnp.exp(-acc))
            dacc = g_ext * (sig * (1.0 + acc * (1.0 - sig)))
            du = w[KCONV - 1] * dacc[:R]
            for i in range(KCONV - 1):
                du = du + w[i] * pltpu.roll(dacc, R + H - (KCONV - 1 - i), 0)[:R]
            d = dacc[:R]
            return du, [jnp.sum(d * t[:R], axis=0, keepdims=True) for t in taps], jnp.sum(d, axis=0, keepdims=True)

        du, dws, db = chunk(jnp.concatenate([pad, u_ref[0:R + H, :]], axis=0), g_ref[0:R + H, :])
        du_ref[0:R, :] = du

        def step(c, carry):
            r0 = pl.multiple_of(c * R, R)
            du_c, dws_c, db_c = chunk(u_ref[pl.ds(pl.multiple_of(r0 - H, H), R + 2 * H), :], g_ref[pl.ds(r0, R + H), :])
            du_ref[pl.ds(r0, R), :] = du_c
            return [a + b_ for a, b_ in zip(carry[0], dws_c)], carry[1] + db_c

        dws, db = lax.fori_loop(1, S // R - 1, step, (dws, db))
        du, dws_l, db_l = chunk(jnp.concatenate([u_ref[S - R - H:S, :], pad], axis=0),
                                jnp.concatenate([g_ref[S - R:S, :], pad], axis=0))
        du_ref[S - R:S, :] = du
        for i in range(KCONV):
            dw_ref[i:i + 1, :] = dws[i] + dws_l[i]
        db_ref[...] = db + db_l

    return pl.pallas_call(
        body, name=name, grid=(n_ch // tc,),
        in_specs=[pl.BlockSpec((S, tc), lambda j: (0, j + col0 // tc)),
                  pl.BlockSpec((KCONV, tc), lambda j: (0, j)), pl.BlockSpec((1, tc), lambda j: (0, j)),
                  pl.BlockSpec((S, tc), lambda j: (0, j))],
        out_specs=[pl.BlockSpec((S, tc), lambda j: (0, j)), pl.BlockSpec((KCONV, tc), lambda j: (0, j)),
                   pl.BlockSpec((1, tc), lambda j: (0, j))],
        out_shape=[jax.ShapeDtypeStruct((S, n_ch), F32), jax.ShapeDtypeStruct((KCONV, n_ch), F32),
                   jax.ShapeDtypeStruct((1, n_ch), F32)],
        compiler_params=_cparams(("parallel",)),
    )(proj, conv_w, conv_b, dxbc)


@functools.partial(jax.custom_vjp, nondiff_argnums=(2,))
def _mm(a, b, dims):
    return _dot(a, b, dims)


def _mm_fwd(a, b, dims):
    return _dot(a, b, dims), (a, b)


def _mm_bwd(dims, res, g):
    a, b = res
    if dims == "nn":
        return _dot(g, b, "nt"), _dot(a, g, "tn")
    if dims == "nt":
        return _dot(g, b, "nn"), _dot(g, a, "tn")
    return _dot(b, g, "nt"), _dot(a, g, "nn")


_mm.defvjp(_mm_fwd, _mm_bwd)


def _tri_dot(x, upper):
    n = x.shape[0]
    r = lax.broadcasted_iota(jnp.int32, (n, n), 0)
    c = lax.broadcasted_iota(jnp.int32, (n, n), 1)
    t = jnp.where((r <= c) if upper else (r >= c), 1.0, 0.0)
    return lax.dot_general(t, x, _DN["nn"], precision=lax.Precision.HIGHEST, preferred_element_type=F32)


@jax.custom_vjp
def _cumsum_rows(x):
    return _tri_dot(x, False)


_cumsum_rows.defvjp(lambda x: (_tri_dot(x, False), None), lambda _, g: (_tri_dot(g, True),))


def _ssd_chunk(xs_p, bm_g, cm_g, dtr, z_p, dtb, alog, dsk, nw_p, h_p):
    L = dtr.shape[0]
    n_pairs = len(xs_p)
    ppg = n_pairs // len(bm_g)
    lane = lax.broadcasted_iota(jnp.int32, (1, LANES), 1)
    sub = lax.broadcasted_iota(jnp.int32, (LANES, 1), 0)
    lo = lane < HD
    row_l = lax.broadcasted_iota(jnp.int32, (L, 1), 0)
    tri = lax.broadcasted_iota(jnp.int32, (L, L), 0) >= lax.broadcasted_iota(jnp.int32, (L, L), 1)

    dt = _softplus(dtr + dtb)
    acs = _cumsum_rows(dt * (-jnp.exp(alog)))
    acs_t = acs.T
    a_last = jnp.sum(jnp.where(row_l == L - 1, acs, 0.0), axis=0, keepdims=True)
    e_acs = jnp.exp(acs)
    dec = jnp.exp(a_last - acs)
    cdec = jnp.exp(a_last)

    def colv(m, h):
        return jnp.sum(jnp.where(lane == h, m, 0.0), axis=1, keepdims=True)

    def rowv(mt, h):
        return jnp.sum(jnp.where(sub == h, mt, 0.0), axis=0, keepdims=True)

    def pair(m, h0):
        return jnp.where(lo, colv(m, h0), colv(m, h0 + 1))

    ys, hs = [], []
    cb = None
    for p in range(n_pairs):
        g, h0 = p // ppg, 2 * p
        bmat, cmat = bm_g[g], cm_g[g]
        if p % ppg == 0:
            cb = _mm(cmat, bmat, "nt")
        x = xs_p[p]
        xdt = x * pair(dt, h0)
        yd = []
        for h in (h0, h0 + 1):
            seg = colv(acs, h) - rowv(acs_t, h)
            lm = jnp.where(tri, jnp.exp(jnp.where(tri, seg, 0.0)), 0.0)
            yd.append(_mm(cb * lm, xdt, "nn"))
        y = jnp.where(lo, yd[0], yd[1])
        y = y + _mm(cmat, h_p[p], "nt") * pair(e_acs, h0)
        st = _mm(xdt * pair(dec, h0), bmat, "tn")
        cd_col = jnp.where(sub < HD, colv(cdec, h0), colv(cdec, h0 + 1))
        hs.append(h_p[p] * cd_col + st)
        ys.append(y + pair(dsk, h0) * x)

    y2 = [ys[p] * _silu(z_p[p]) for p in range(n_pairs)]
    outs = []
    for g in range(len(bm_g)):
        ps = range(g * ppg, (g + 1) * ppg)
        ss = sum(jnp.sum(y2[p] * y2[p], axis=1, keepdims=True) for p in ps)
        rs = lax.rsqrt(ss * (1.0 / (ppg * LANES)) + EPS)
        outs += [y2[p] * rs * nw_p[p] for p in ps]
    return outs, hs


def _ssd_slices(xbc_ref, z_ref, nw_ref, di):
    n_pairs = di // LANES
    xs_p = [xbc_ref[:, p * LANES:(p + 1) * LANES] for p in range(n_pairs)]
    bm_g = [xbc_ref[:, di + g * NSTATE:di + (g + 1) * NSTATE] for g in range(NG)]
    cm_g = [xbc_ref[:, di + (NG + g) * NSTATE:di + (NG + g + 1) * NSTATE] for g in range(NG)]
    z_p = [z_ref[:, p * LANES:(p + 1) * LANES] for p in range(n_pairs)]
    nw_p = [nw_ref[:, p * LANES:(p + 1) * LANES] for p in range(n_pairs)]
    return xs_p, bm_g, cm_g, z_p, nw_p


def _ssd_fwd(xbc, proj, dt_cb, dtb, alog, dsk, nw, name):
    S, cc = xbc.shape
    di = NH_SSD * HD
    n_pairs = di // LANES
    nchunk = S // CHUNK

    def body(xbc_ref, z_ref, dtr_ref, dtb_ref, alog_ref, dsk_ref, nw_ref, y_ref, hs_ref, h_scr):
        @pl.when(pl.program_id(0) == 0)
        def _():
            h_scr[...] = jnp.zeros_like(h_scr)

        xs_p, bm_g, cm_g, z_p, nw_p = _ssd_slices(xbc_ref, z_ref, nw_ref, di)
        h_p = [h_scr[p * LANES:(p + 1) * LANES, :] for p in range(n_pairs)]
        hs_ref[...] = h_scr[...]
        outs, hs = _ssd_chunk(xs_p, bm_g, cm_g, dtr_ref[...], z_p, dtb_ref[...], alog_ref[...], dsk_ref[...], nw_p, h_p)
        for p in range(n_pairs):
            y_ref[:, p * LANES:(p + 1) * LANES] = outs[p].astype(y_ref.dtype)
            h_scr[p * LANES:(p + 1) * LANES, :] = hs[p]

    vec = pl.BlockSpec((1, LANES), lambda c: (0, 0))
    return pl.pallas_call(
        body, name=name, grid=(nchunk,),
        in_specs=[pl.BlockSpec((CHUNK, cc), lambda c: (c, 0)), pl.BlockSpec((CHUNK, di), lambda c: (c, 0)),
                  pl.BlockSpec((CHUNK, LANES), lambda c: (c, dt_cb)), vec, vec, vec,
                  pl.BlockSpec((1, di), lambda c: (0, 0))],
        out_specs=[pl.BlockSpec((CHUNK, di), lambda c: (c, 0)), pl.BlockSpec((None, di, NSTATE), lambda c: (c, 0, 0))],
        out_shape=[jax.ShapeDtypeStruct((S, di), BF16), jax.ShapeDtypeStruct((nchunk, di, NSTATE), F32)],
        scratch_shapes=[pltpu.VMEM((di, NSTATE), F32)],
        compiler_params=_cparams(("arbitrary",)),
    )(xbc, proj, proj, dtb, alog, dsk, nw)


def _ssd_bwd(xbc, proj, dt_cb, dtb, alog, dsk, nw, hsave, dy, name):
    S, cc = xbc.shape
    di = NH_SSD * HD
    n_pairs = di // LANES
    nchunk = S // CHUNK

    def body(xbc_ref, z_ref, dtr_ref, dtb_ref, alog_ref, dsk_ref, nw_ref, hs_ref, dy_ref,
             dxbc_ref, dz_ref, ddtr_ref, ddtb_ref, dalog_ref, ddsk_ref, dnw_ref, dh_scr):
        @pl.when(pl.program_id(0) == 0)
        def _():
            dh_scr[...] = jnp.zeros_like(dh_scr)
            ddtb_ref[...] = jnp.zeros_like(ddtb_ref)
            dalog_ref[...] = jnp.zeros_like(dalog_ref)
            ddsk_ref[...] = jnp.zeros_like(ddsk_ref)
            dnw_ref[...] = jnp.zeros_like(dnw_ref)

        xs_p, bm_g, cm_g, z_p, nw_p = _ssd_slices(xbc_ref, z_ref, nw_ref, di)
        h_p = [hs_ref[p * LANES:(p + 1) * LANES, :] for p in range(n_pairs)]
        dy_p = [dy_ref[:, p * LANES:(p + 1) * LANES].astype(F32) for p in range(n_pairs)]
        dh_p = [dh_scr[p * LANES:(p + 1) * LANES, :] for p in range(n_pairs)]
        _, vjp = jax.vjp(_ssd_chunk, xs_p, bm_g, cm_g, dtr_ref[...], z_p, dtb_ref[...], alog_ref[...], dsk_ref[...],
                         nw_p, h_p)
        dxs, dbm, dcm, ddtr, dz, ddtb, dalog, ddsk, dnw, dh = vjp((dy_p, dh_p))
        for p in range(n_pairs):
            sl = slice(p * LANES, (p + 1) * LANES)
            dxbc_ref[:, sl] = dxs[p]
            dz_ref[:, sl] = dz[p]
            dnw_ref[:, sl] += dnw[p]
            dh_scr[sl, :] = dh[p]
        for g in range(NG):
            dxbc_ref[:, di + g * NSTATE:di + (g + 1) * NSTATE] = dbm[g]
            dxbc_ref[:, di + (NG + g) * NSTATE:di + (NG + g + 1) * NSTATE] = dcm[g]
        ddtr_ref[...] = ddtr
        ddtb_ref[...] += ddtb
        dalog_ref[...] += dalog
        ddsk_ref[...] += ddsk

    last = nchunk - 1
    vec = pl.BlockSpec((1, LANES), lambda c: (0, 0))
    return pl.pallas_call(
        body, name=name, grid=(nchunk,),
        in_specs=[pl.BlockSpec((CHUNK, cc), lambda c: (last - c, 0)), pl.BlockSpec((CHUNK, di), lambda c: (last - c, 0)),
                  pl.BlockSpec((CHUNK, LANES), lambda c: (last - c, dt_cb)), vec, vec, vec,
                  pl.BlockSpec((1, di), lambda c: (0, 0)),
                  pl.BlockSpec((None, di, NSTATE), lambda c: (last - c, 0, 0)),
                  pl.BlockSpec((CHUNK, di), lambda c: (last - c, 0))],
        out_specs=[pl.BlockSpec((CHUNK, cc), lambda c: (last - c, 0)), pl.BlockSpec((CHUNK, di), lambda c: (last - c, 0)),
                   pl.BlockSpec((CHUNK, LANES), lambda c: (last - c, 0)), vec, vec, vec,
                   pl.BlockSpec((1, di), lambda c: (0, 0))],
        out_shape=[jax.ShapeDtypeStruct((S, cc), F32), jax.ShapeDtypeStruct((S, di), F32),
                   jax.ShapeDtypeStruct((S, LANES), F32), jax.ShapeDtypeStruct((1, LANES), F32),
                   jax.ShapeDtypeStruct((1, LANES), F32), jax.ShapeDtypeStruct((1, LANES), F32),
                   jax.ShapeDtypeStruct((1, di), F32)],
        scratch_shapes=[pltpu.VMEM((di, NSTATE), F32)],
        compiler_params=_cparams(("arbitrary",)),
    )(xbc, proj, proj, dtb, alog, dsk, nw, hsave, dy)


def _band_masks(rows_q, rows_k):
    qi = lax.broadcasted_iota(jnp.int32, (rows_q, rows_k), 0)
    ki = lax.broadcasted_iota(jnp.int32, (rows_q, rows_k), 1)
    return qi, ki


def _class_chunks(n_rows, d):
    per_class = n_rows // d
    ch = min(per_class, 256)
    out = []
    for r in range(d):
        for c0 in range(0, per_class, ch):
            tok = pl.ds(c0, ch) if d == 1 else pl.ds(r + d * c0, ch, stride=d)
            out.append((tok, pl.ds(r * per_class + c0, ch)))
    return out


def _to_class_order(src_ref, dst_ref, n_rows, d):
    for tok, cls in _class_chunks(n_rows, d):
        dst_ref[cls, :] = src_ref[tok, :].astype(dst_ref.dtype)


def _blk_rows(t):
    return pl.ds(pl.multiple_of(t * ABLK, ABLK), ABLK)


def _head_lanes(msk, t, t_rolled):
    return jnp.where(msk, t, t_rolled)


def _zero_unless(msk, t):
    return jnp.where(msk, t, jnp.zeros_like(t))


def _attn_fwd(qn, kn, proj, v_cb, name):
    S, ad = qn.shape
    nb = S // ABLK
    nbr = len(PATTERNS)

    def body(q_ref, k_ref, v_ref, o_ref, lse_ref, qc, kc, vc, ob, mb, lb, m_s, l_s):
        lo = _lane_mask()
        qi, ki = _band_masks(ABLK, ABLK)
        cur_ok, prev_ok = ki <= qi, ki >= qi
        for bi, (_, d) in enumerate(PATTERNS):
            nbc = S // d // ABLK
            first, last = bi == 0, bi == nbr - 1
            qs, ks, vs = q_ref, k_ref, v_ref
            if d > 1:
                qs, ks, vs = qc, kc, vc
                for src, dst in ((q_ref, qc), (k_ref, kc), (v_ref, vc)):
                    _to_class_order(src, dst, S, d)
            o_dst, m_dst, l_dst = (o_ref, m_s, l_s) if first else (ob, mb, lb)

            def blk(t, carry, nbc=nbc, qs=qs, ks=ks, vs=vs, o_dst=o_dst, m_dst=m_dst, l_dst=l_dst):
                rows, prow = _blk_rows(t), _blk_rows(jnp.maximum(t - 1, 0))
                has_prev = (t % nbc) != 0
                qv = qs[rows, :]
                q2 = jnp.concatenate([_zero_unless(lo, qv), _zero_unless(jnp.logical_not(lo), qv)], axis=0).astype(BF16)
                ok_c = jnp.concatenate([cur_ok, cur_ok], axis=0)
                ok_p = jnp.concatenate([prev_ok, prev_ok], axis=0) & has_prev
                s_c = jnp.where(ok_c, _dot(q2, ks[rows, :], "nt"), NEG)
                s_p = jnp.where(ok_p, _dot(q2, ks[prow, :], "nt"), NEG)
                m = jnp.max(jnp.maximum(s_c, s_p), axis=1, keepdims=True)
                p_c, p_p = jnp.exp(s_c - m), jnp.exp(s_p - m)
                l = jnp.sum(p_c + p_p, axis=1, keepdims=True)
                o2 = _dot(p_c, vs[rows, :], "nn") + _dot(p_p, vs[prow, :], "nn")
                o_dst[rows, :] = jnp.where(lo, o2[:ABLK], o2[ABLK:])
                m_dst[rows, :] = jnp.where(lo, m[:ABLK], m[ABLK:])
                l_dst[rows, :] = jnp.where(lo, l[:ABLK], l[ABLK:])
                return carry

            lax.fori_loop(0, nb, blk, 0, unroll=8)
            if first:
                continue
            for tok, cls in _class_chunks(S, d):
                m_old, m_b = m_s[tok, :], mb[cls, :]
                m_new = jnp.maximum(m_old, m_b)
                a, b = jnp.exp(m_old - m_new), jnp.exp(m_b - m_new)
                l_new = a * l_s[tok, :] + b * lb[cls, :]
                o_new = a * o_ref[tok, :] + b * ob[cls, :]
                if last:
                    o_ref[tok, :] = o_new / l_new
                    lse_ref[tok, :] = m_new + jnp.log(l_new)
                else:
                    o_ref[tok, :] = o_new
                    m_s[tok, :] = m_new
                    l_s[tok, :] = l_new

    col = pl.BlockSpec((S, LANES), lambda h: (0, h))
    return pl.pallas_call(
        body, name=name, grid=(ad // LANES,),
        in_specs=[col, col, pl.BlockSpec((S, LANES), lambda h: (0, h + v_cb))], out_specs=[col, col],
        out_shape=[jax.ShapeDtypeStruct((S, ad), F32), jax.ShapeDtypeStruct((S, ad), F32)],
        scratch_shapes=[pltpu.VMEM((S, LANES), BF16)] * 3 + [pltpu.VMEM((S, LANES), F32)] * 5,
        compiler_params=_cparams(("parallel",)),
    )(qn, kn, proj)


def _attn_bwd(qn, kn, proj, v_cb, do, lse, dd, name):
    S, ad = qn.shape
    nb = S // ABLK

    def body(q_ref, k_ref, v_ref, do_ref, lse_ref, dd_ref, dq_ref, dk_ref, dv_ref,
             qc, kc, vc, doc, lsec, ddc, dqc, dkc, dvc):
        lo = _lane_mask()
        qi, ki = _band_masks(ABLK, ABLK)
        cur_ok, prev_ok = ki <= qi, ki >= qi
        for bi, (_, d) in enumerate(PATTERNS):
            nbc = S // d // ABLK
            first = bi == 0
            token_order = (q_ref, k_ref, v_ref, do_ref, lse_ref, dd_ref)
            class_order = (qc, kc, vc, doc, lsec, ddc)
            if d > 1:
                for src, dst in zip(token_order, class_order):
                    _to_class_order(src, dst, S, d)
            qs, ks, vs, dos, lses, dds = class_order if d > 1 else token_order
            dq_dst, dk_dst, dv_dst = (dq_ref, dk_ref, dv_ref) if first else (dqc, dkc, dvc)
            dk_dst[...] = jnp.zeros_like(dk_dst)
            dv_dst[...] = jnp.zeros_like(dv_dst)

            def blk(t, carry, nbc=nbc, qs=qs, ks=ks, vs=vs, dos=dos, lses=lses, dds=dds,
                    dq_dst=dq_dst, dk_dst=dk_dst, dv_dst=dv_dst):
                rows, prow = _blk_rows(t), _blk_rows(jnp.maximum(t - 1, 0))
                has_prev = (t % nbc) != 0
                qv, dov, lse_b, dd_b = qs[rows, :], dos[rows, :], lses[rows, :], dds[rows, :]
                lse_r, dd_r = pltpu.roll(lse_b, HD, 1), pltpu.roll(dd_b, HD, 1)
                nlo = jnp.logical_not(lo)
                q2 = jnp.concatenate([_zero_unless(lo, qv), _zero_unless(nlo, qv)], axis=0).astype(BF16)
                do2 = jnp.concatenate([_zero_unless(lo, dov), _zero_unless(nlo, dov)], axis=0).astype(BF16)
                lse2 = jnp.concatenate([_head_lanes(lo, lse_b, lse_r), _head_lanes(nlo, lse_b, lse_r)], axis=0)
                dd2 = jnp.concatenate([_head_lanes(lo, dd_b, dd_r), _head_lanes(nlo, dd_b, dd_r)], axis=0)
                dq2 = None
                for krows, vmask in ((rows, cur_ok), (prow, prev_ok & has_prev)):
                    kv, vv = ks[krows, :], vs[krows, :]
                    vmask2 = jnp.concatenate([vmask, vmask], axis=0)
                    s = jnp.where(vmask2, _dot(q2, kv, "nt"), NEG)
                    p = jnp.exp(s - lse2)
                    ds = p * (_dot(do2, vv, "nt") - dd2)
                    dqk = _dot(ds, kv, "nn")
                    dq2 = dqk if dq2 is None else dq2 + dqk
                    dv_dst[krows, :] += _dot(p, do2, "tn")
                    dk_dst[krows, :] += _dot(ds, q2, "tn")
                dq_dst[rows, :] = jnp.where(lo, dq2[:ABLK], dq2[ABLK:])
                return carry

            lax.fori_loop(0, nb, blk, 0, unroll=4)
            if not first:
                for tok, cls in _class_chunks(S, d):
                    dq_ref[tok, :] = dq_ref[tok, :] + dqc[cls, :]
                    dk_ref[tok, :] = dk_ref[tok, :] + dkc[cls, :]
                    dv_ref[tok, :] = dv_ref[tok, :] + dvc[cls, :]

    col = pl.BlockSpec((S, LANES), lambda h: (0, h))
    col1 = pl.BlockSpec((S, LANES), lambda h: (0, h), pipeline_mode=pl.Buffered(1))
    vcol1 = pl.BlockSpec((S, LANES), lambda h: (0, h + v_cb), pipeline_mode=pl.Buffered(1))
    return pl.pallas_call(
        body, name=name, grid=(ad // LANES,),
        in_specs=[col, col, vcol1, col1, col1, col1], out_specs=[col, col, col],
        out_shape=[jax.ShapeDtypeStruct((S, ad), F32)] * 3,
        scratch_shapes=[pltpu.VMEM((S, LANES), BF16)] * 4 + [pltpu.VMEM((S, LANES), F32)] * 5,
        compiler_params=_cparams(("parallel",)),
    )(qn, kn, proj, do, lse, dd)


def _coords():
    return lax.axis_index("x"), lax.axis_index("y"), lax.axis_index("c")


def _exchange8(xs, per_dest, name):
    n = len(xs)
    blk = [x.shape[1:] if per_dest else x.shape for x in xs]

    def body(*refs):
        ins, outs = refs[:n], refs[n:2 * n]
        send_sems, recv_sems, local_sems = refs[2 * n:]
        x, y, c = _coords()
        sibling = (x, y, 1 - c)
        chips = [(1 - x, y), (x, 1 - y), (1 - x, 1 - y)]
        first, passed, mine = [], [], []
        for a in range(n):
            def src_for(cx, cy, a=a):
                return ins[a].at[2 * cx + cy] if per_dest else ins[a]

            def slot(px, py, pc, a=a):
                return outs[a].at[4 * px + 2 * py + pc]

            def copy(k, src, dst, to, a=a):
                return pltpu.make_async_remote_copy(src_ref=src, dst_ref=dst, send_sem=send_sems.at[7 * a + k],
                                                    recv_sem=recv_sems.at[7 * a + k], device_id=to, device_id_type=MESH)

            m = pltpu.make_async_copy(src_for(x, y), slot(x, y, c), local_sems.at[a])
            m.start()
            mine.append(m)
            cps = [copy(0, src_for(x, y), slot(x, y, c), sibling)]
            cps += [copy(1 + j, src_for(*chip), slot(x, y, c), (*chip, c)) for j, chip in enumerate(chips)]
            for cp in cps:
                cp.start()
            first += cps
        for a in range(n):
            def slot(px, py, pc, a=a):
                return outs[a].at[4 * px + 2 * py + pc]

            def copy(k, src, dst, to, a=a):
                return pltpu.make_async_remote_copy(src_ref=src, dst_ref=dst, send_sem=send_sems.at[7 * a + k],
                                                    recv_sem=recv_sems.at[7 * a + k], device_id=to, device_id_type=MESH)

            for j, chip in enumerate(chips):
                copy(1 + j, slot(*chip, c), slot(*chip, c), (*chip, c)).wait_recv()
                fw = copy(4 + j, slot(*chip, c), slot(*chip, c), sibling)
                fw.start()
                passed.append(fw)
        for a in range(n):
            def slot(px, py, pc, a=a):
                return outs[a].at[4 * px + 2 * py + pc]

            def copy(k, src, dst, to, a=a):
                return pltpu.make_async_remote_copy(src_ref=src, dst_ref=dst, send_sem=send_sems.at[7 * a + k],
                                                    recv_sem=recv_sems.at[7 * a + k], device_id=to, device_id_type=MESH)

            copy(0, slot(x, y, 1 - c), slot(x, y, 1 - c), sibling).wait_recv()
            for j, chip in enumerate(chips):
                copy(4 + j, slot(*chip, 1 - c), slot(*chip, 1 - c), sibling).wait_recv()
        for cp in first + passed:
            cp.wait_send()
        for m in mine:
            m.wait()

    anyspec = pl.BlockSpec(memory_space=pl.ANY)
    res = pl.pallas_call(
        body, name=name, in_specs=[anyspec] * n, out_specs=[anyspec] * n,
        out_shape=[jax.ShapeDtypeStruct((8,) + tuple(b), x.dtype) for b, x in zip(blk, xs)],
        scratch_shapes=[pltpu.SemaphoreType.DMA((7 * n,)), pltpu.SemaphoreType.DMA((7 * n,)),
                        pltpu.SemaphoreType.DMA((n,))],
    )(*xs)
    return list(res)


def _pair_swap(xs, name):
    n = len(xs)

    def body(*refs):
        ins, outs = refs[:n], refs[n:2 * n]
        send_sems, recv_sems = refs[2 * n:]
        x, y, c = _coords()
        cps = [pltpu.make_async_remote_copy(src_ref=ins[a].at[1 - c], dst_ref=outs[a], send_sem=send_sems.at[a],
                                            recv_sem=recv_sems.at[a], device_id=(x, y, 1 - c), device_id_type=MESH)
               for a in range(n)]
        for cp in cps:
            cp.start()
        for cp in cps:
            cp.wait()

    anyspec = pl.BlockSpec(memory_space=pl.ANY)
    res = pl.pallas_call(
        body, name=name, in_specs=[anyspec] * n, out_specs=[anyspec] * n,
        out_shape=[jax.ShapeDtypeStruct(x.shape[1:], x.dtype) for x in xs],
        scratch_shapes=[pltpu.SemaphoreType.DMA((n,)), pltpu.SemaphoreType.DMA((n,))],
    )(*xs)
    return list(res)


_HBM = pl.BlockSpec(memory_space=pltpu.HBM)
_SEM = pl.BlockSpec(memory_space=pltpu.SEMAPHORE)
_EFFECT = pltpu.SideEffectType.DATAFLOW_SIDE_EFFECTING


def _n_peers(both):
    return 7 if both else 3


def _peer(x, y, c, j, both):
    bits = j + 1 if both else 2 * (j + 1)
    dx, dy, dc = bits >> 2 & 1, bits >> 1 & 1, bits & 1
    return (1 - x if dx else x, 1 - y if dy else y, 1 - c if dc else c)


def _spread_copies(s_refs, l_refs, send_sems, recv_sems, per_dest, both):
    x, y, c = _coords()
    me = 4 * x + 2 * y + c
    npeer = _n_peers(both)
    cps = []
    for a in range(len(s_refs)):
        for j in range(npeer):
            tx, ty, tc = _peer(x, y, c, j, both)
            src = s_refs[a].at[2 * tx + ty] if per_dest else s_refs[a]
            cps.append(pltpu.make_async_remote_copy(src_ref=src, dst_ref=l_refs[a].at[me],
                                                    send_sem=send_sems.at[npeer * a + j],
                                                    recv_sem=recv_sems.at[npeer * a + j], device_id=(tx, ty, tc),
                                                    device_id_type=MESH))
    return cps


def _sibling_fill(lands, name):
    n = len(lands)

    def body(*refs):
        outs, send_sems, recv_sems = refs[n:2 * n], refs[2 * n], refs[2 * n + 1]
        x, y, c = _coords()
        cps = [pltpu.make_async_remote_copy(src_ref=outs[a].at[2 * k + c], dst_ref=outs[a].at[2 * k + c],
                                            send_sem=send_sems.at[4 * a + k], recv_sem=recv_sems.at[4 * a + k],
                                            device_id=(x, y, 1 - c), device_id_type=MESH)
               for a in range(n) for k in range(4)]
        for cp in cps:
            cp.start()
        for cp in cps:
            cp.wait()

    anyspec = pl.BlockSpec(memory_space=pl.ANY)
    res = pl.pallas_call(
        body, name=name, in_specs=[anyspec] * n, out_specs=[anyspec] * n,
        out_shape=[jax.ShapeDtypeStruct(t.shape, t.dtype) for t in lands], input_output_aliases={i: i for i in range(n)},
        scratch_shapes=[pltpu.SemaphoreType.DMA((4 * n,)), pltpu.SemaphoreType.DMA((4 * n,))],
    )(*lands)
    return list(res)


def _spread_start(srcs, per_dest, both, dev, chip, name):
    n = len(srcs)
    npeer = _n_peers(both)
    lands = []
    for s in srcs:
        own = lax.dynamic_index_in_dim(s, chip, 0, keepdims=False) if per_dest else s
        lands.append(lax.dynamic_update_index_in_dim(lax.empty((8,) + own.shape, own.dtype), own, dev, 0))

    def body(*refs):
        s_refs, l_refs, send_sems, recv_sems, token = refs[:n], refs[n:2 * n], refs[2 * n], refs[2 * n + 1], refs[-1]
        for cp in _spread_copies(s_refs, l_refs, send_sems, recv_sems, per_dest, both):
            cp.start()
        token[...] = jnp.zeros_like(token)

    hbm_in = [pltpu.with_memory_space_constraint(t, pltpu.HBM) for t in list(srcs) + lands]
    outs = pl.pallas_call(
        body, name=name,
        out_shape=(pltpu.SemaphoreType.DMA((npeer * n,)), pltpu.SemaphoreType.DMA((npeer * n,)),
                   *[pltpu.HBM(t.shape, t.dtype) for t in hbm_in], jax.ShapeDtypeStruct((8, LANES), F32)),
        in_specs=[_HBM] * (2 * n), out_specs=(_SEM, _SEM, *[_HBM] * (2 * n), pl.BlockSpec(memory_space=pltpu.VMEM)),
        input_output_aliases={i: 2 + i for i in range(2 * n)},
        compiler_params=pltpu.CompilerParams(has_side_effects=_EFFECT),
    )(*hbm_in)
    return (outs[0], outs[1], list(outs[2:2 + n]), list(outs[2 + n:2 + 2 * n])), outs[-1]


def _spread_wait(handle, per_dest, both, after, name):
    send_sems, recv_sems, srcs, lands = handle
    n = len(srcs)

    def body(*refs):
        s_refs, l_refs, send_ref, recv_ref = refs[:n], refs[n:2 * n], refs[2 * n], refs[2 * n + 1]
        for cp in _spread_copies(s_refs, l_refs, send_ref, recv_ref, per_dest, both):
            cp.wait_send()
            cp.wait_recv()

    outs = pl.pallas_call(
        body, name=name, out_shape=tuple(pltpu.HBM(t.shape, t.dtype) for t in srcs + lands),
        in_specs=[_HBM] * (2 * n) + [_SEM, _SEM, pl.BlockSpec(memory_space=pl.ANY)], out_specs=tuple([_HBM] * (2 * n)),
        input_output_aliases={i: i for i in range(2 * n)},
        compiler_params=pltpu.CompilerParams(has_side_effects=_EFFECT),
    )(*srcs, *lands, send_sems, recv_sems, after)
    return list(outs[n:])


def _row_tile(n, cap, mult):
    best = n
    for t in range(mult, min(n, cap) + 1, mult):
        if n % t == 0:
            best = t
    return best


PAIR_ADD_BLOCK_BYTES = 2 << 20


def _pair_add(g2, theirs, half, name):
    _, n, cdim = g2.shape
    tm = _row_tile(n, max(16, PAIR_ADD_BLOCK_BYTES // (4 * cdim)), 16)

    def body(h_ref, a_ref, b_ref, o_ref):
        o_ref[...] = (a_ref[...] + b_ref[...]).astype(o_ref.dtype)

    grid_spec = pltpu.PrefetchScalarGridSpec(
        num_scalar_prefetch=1, grid=(n // tm,),
        in_specs=[pl.BlockSpec((None, tm, cdim), lambda i, h: (h[0], i, 0)), pl.BlockSpec((tm, cdim), lambda i, h: (i, 0))],
        out_specs=pl.BlockSpec((tm, cdim), lambda i, h: (i, 0)))
    return pl.pallas_call(body, name=name, grid_spec=grid_spec, out_shape=jax.ShapeDtypeStruct((n, cdim), BF16),
                          compiler_params=_cparams(("parallel",)))(half.reshape(1).astype(jnp.int32), g2, theirs)


def _adamw_math(w, g, m, v):
    m = ADAM_B1 * m + (1.0 - ADAM_B1) * g
    v = ADAM_B2 * v + (1.0 - ADAM_B2) * (g * g)
    m_hat = m / (1.0 - ADAM_B1 ** ADAM_STEP)
    v_hat = v / (1.0 - ADAM_B2 ** ADAM_STEP)
    delta = -ADAM_LR * (m_hat / (jnp.sqrt(v_hat) + ADAM_EPS) + ADAM_WD * w)
    return delta, m, v


def _adamw(parts, w, m, v, name, tm=128):
    npart, R, C = parts.shape
    tm = min(tm, R)

    def body(p_ref, w_ref, m_ref, v_ref, g_out, d_out, m_out, v_out):
        g = p_ref[0].astype(F32)
        for i in range(1, npart):
            g = g + p_ref[i].astype(F32)
        d, mm, vv = _adamw_math(w_ref[...], g, m_ref[...], v_ref[...])
        g_out[...] = g
        d_out[...] = d
        m_out[...] = mm
        v_out[...] = vv

    spec = pl.BlockSpec((tm, C), lambda i: (i, 0))
    return pl.pallas_call(
        body, name=name, grid=(R // tm,),
        in_specs=[pl.BlockSpec((npart, tm, C), lambda i: (0, i, 0)), spec, spec, spec], out_specs=[spec] * 4,
        out_shape=[jax.ShapeDtypeStruct((R, C), F32)] * 4,
        compiler_params=_cparams(("parallel",)),
    )(parts, w, m, v)


def _sum_parts(parts, name):
    npart, R, C = parts.shape

    def body(p_ref, o_ref):
        g = p_ref[0]
        for i in range(1, npart):
            g = g + p_ref[i]
        o_ref[...] = g

    return pl.pallas_call(body, name=name, out_shape=jax.ShapeDtypeStruct((R, C), F32))(parts)


def _mod_fwd(c_all, w_ada, b_sh, name):
    def body(c_ref, w_ref, b_ref, o_ref):
        o_ref[...] = _dot(_silu(c_ref[...]), w_ref[...], "nn") + b_ref[...]

    return pl.pallas_call(body, name=name, out_shape=jax.ShapeDtypeStruct((c_all.shape[0], w_ada.shape[1]), F32),
                          compiler_params=pltpu.CompilerParams(vmem_limit_bytes=VMEM_LIMIT))(c_all, w_ada, b_sh)


def _mod_wgrad(c_all, dmod_sh, name):
    def body(c_ref, d_ref, o_ref):
        o_ref[...] = _dot(_silu(c_ref[...]), d_ref[...], "tn")

    return pl.pallas_call(body, name=name, out_shape=jax.ShapeDtypeStruct((c_all.shape[1], dmod_sh.shape[1]), F32),
                          compiler_params=pltpu.CompilerParams(vmem_limit_bytes=VMEM_LIMIT))(c_all, dmod_sh)


def _pad_lanes(v):
    return jnp.pad(v, ((0, 0), (0, (-v.shape[1]) % LANES)))


def kernel(x, c, norm1_w, norm2_w, w_ada, b_ada, w_in, conv_w, conv_b, dt_bias, a_log, d_skip, ssd_norm_w, q_norm_w, k_norm_w, attn_norm_w, w_out, w_ff1, w_ff2, loss_target, m_norm1_w, m_norm2_w, m_w_ada, m_b_ada, m_w_in, m_conv_w, m_conv_b, m_dt_bias, m_a_log, m_d_skip, m_ssd_norm_w, m_q_norm_w, m_k_norm_w, m_attn_norm_w, m_w_out, m_w_ff1, m_w_ff2, v_norm1_w, v_norm2_w, v_w_ada, v_b_ada, v_w_in, v_conv_w, v_conv_b, v_dt_bias, v_a_log, v_d_skip, v_ssd_norm_w, v_q_norm_w, v_k_norm_w, v_attn_norm_w, v_w_out, v_w_ff1, v_w_ff2):
    xi, yi, ci = _coords()
    chip = 2 * xi + yi
    dev = 2 * chip + ci
    xs, tgt = x[0], loss_target[0]
    S, D = xs.shape
    DI, AD = NH_SSD * HD, NH_ATT * HD
    CC = DI + 2 * NG * NSTATE
    PW = DI + CC + 3 * AD + LANES
    DFF = w_ff1.shape[2] * 4
    MIX = DI + AD
    o_xbc, o_q, o_k, o_v, o_dt = DI, DI + CC, DI + CC + AD, DI + CC + 2 * AD, DI + CC + 3 * AD

    def half_rows(w):
        r = w.shape[0] // 2
        return lax.dynamic_slice_in_dim(w, ci * r, r, 0).astype(BF16)

    c_all, conv_w_all = _exchange8([c, conv_w[0]], False, "gather_c_conv_w")
    c_all = c_all.reshape(8, D)
    c_all = jnp.pad(c_all, ((0, 8), (0, 0)))
    nmod = w_ada.shape[2]
    b_sh = lax.dynamic_slice_in_dim(b_ada, chip * nmod, nmod, 1)
    mod_sh = _mod_fwd(c_all, w_ada[0], b_sh, "mod_fwd")
    mod_all = _exchange8([mod_sh[:8]], False, "gather_mod")[0]
    mod_me = lax.dynamic_index_in_dim(mod_all[0::2], dev, 1, keepdims=False).reshape(1, 4 * nmod)
    shift1, scale1, gate1, shift2, scale2, gate2 = [mod_me[:, i * D:(i + 1) * D] for i in range(6)]

    g_in = _exchange8([half_rows(w_in[0])], False, "gather_w_in")[0]
    rest_handle, rest_token = _spread_start([half_rows(w_out[0]), half_rows(w_ff1[0]), half_rows(w_ff2[0])], False, True,
                                            dev, chip, "gather_rest_start")
    shift1 = shift1 + rest_token[0, 0]
    wsh = w_in.shape[2]
    w_in_f = g_in.reshape(4, D, wsh).transpose(1, 0, 2).reshape(D, 4 * wsh)
    n_zx = DI + CC
    w_proj = jnp.concatenate([w_in_f[:, :n_zx], w_in_f[:, n_zx + NH_SSD:], w_in_f[:, n_zx:n_zx + NH_SSD],
                              jnp.zeros((D, LANES - NH_SSD), BF16)], axis=1)

    dtb, alog, dsk = _pad_lanes(dt_bias), _pad_lanes(a_log), _pad_lanes(d_skip)
    qw2 = jnp.concatenate([q_norm_w, q_norm_w], axis=1)
    kw2 = jnp.concatenate([k_norm_w, k_norm_w], axis=1)
    conv_w_f = conv_w_all[0::2].transpose(1, 0, 2).reshape(KCONV, CC)

    h1 = _rows("norm1", lambda r, k: ([_normmod(r[0], *k)], []), [(xs, 0, D)], [norm1_w, scale1, shift1],
               [(D, BF16)], [], S)[0]
    proj = _matmul(h1, w_proj, "nn", F32, "in_proj", tn=896)
    xbc = _conv_fwd(proj, o_xbc, CC, conv_w_f, conv_b, "conv_fwd")
    y_ssd, hsave = _ssd_fwd(xbc, proj, o_dt // LANES, dtb, alog, dsk, ssd_norm_w, "ssd_fwd")

    def qk_call(name, col0, w2, scale):
        def body(t_ref, w_ref, o_ref):
            w = w_ref[...]
            for r0 in range(0, S, NORM_ROWS):
                o_ref[r0:r0 + NORM_ROWS, :] = _headnorm(t_ref[r0:r0 + NORM_ROWS, :], w, scale)
        return pl.pallas_call(
            body, name=name, grid=(AD // LANES,),
            in_specs=[pl.BlockSpec((S, LANES), lambda j: (0, j + col0 // LANES)),
                      pl.BlockSpec((1, LANES), lambda j: (0, 0))],
            out_specs=pl.BlockSpec((S, LANES), lambda j: (0, j)),
            out_shape=jax.ShapeDtypeStruct((S, AD), F32), compiler_params=_cparams(("parallel",)),
        )(proj, w2)

    qn = qk_call("q_norm", o_q, qw2, HD ** -0.5)
    kn = qk_call("k_norm", o_k, kw2, 1.0)
    o_att, lse = _attn_fwd(qn, kn, proj, o_v // LANES, "attn_fwd")
    y_att = _rows("attn_out_norm", lambda r, k: ([_rmsw(r[0], k[0])], []), [(o_att, 0, AD)], [attn_norm_w],
                  [(AD, BF16)], [], S)[0]
    g_out, g_ff1, g_ff2 = _spread_wait(rest_handle, False, True, o_att, "gather_rest_wait")
    w_out_f = g_out.reshape(MIX, D)
    w_out_a, w_out_b = w_out_f[:DI], w_out_f[DI:]
    w_ff1_f = g_ff1.reshape(4, D, DFF // 4).transpose(1, 0, 2).reshape(D, DFF)
    w_ff2_f = g_ff2.reshape(DFF, D)
    mix_a = _matmul(y_ssd, w_out_a, "nn", F32, "out_proj_a")
    mix = _matmul(y_att, w_out_b, "nn", F32, "out_proj_b", epilogue=lambda r, e: r + e, extras=(mix_a,))
    x2, h2 = _rows("resid_norm2", lambda r, k: (list(_resid_normmod(r[0], r[1], *k)), []), [(xs, 0, D), (mix, 0, D)],
                   [gate1, norm2_w, scale2, shift2], [(D, F32), (D, BF16)], [], S)
    u = _matmul(h2, w_ff1_f, "nn", F32, "ff1")
    relu2 = lambda t: jnp.square(jnp.maximum(t, 0.0))
    ff = _matmul(u, w_ff2_f, "nn", F32, "ff2", a_fn=relu2)

    def loss_fn(r, k):
        x2_, ff_, t_ = r
        err = x2_ + k[0] * ff_ - t_
        dy_ = err * (1.0 / D)
        ls = jnp.sum(jnp.sum(0.5 * err * err, axis=1, keepdims=True), axis=0, keepdims=True) * (1.0 / D)
        return [dy_, dy_ * k[0]], [ls, jnp.sum(dy_ * ff_, axis=0, keepdims=True)]

    dy, dff, loss_p, dgate2 = _rows("loss", loss_fn, [(x2, 0, D), (ff, 0, D), (tgt, 0, D)], [gate2],
                                    [(D, F32), (D, BF16)], [(1, 1), (1, D)], S)
    du = _matmul(dff, w_ff2_f, "nt", BF16, "ff2_dx", epilogue=lambda r, e: r * (2.0 * jnp.maximum(e, 0.0)), extras=(u,))
    gw_ff2 = _matmul(u, dff, "tn", BF16, "ff2_dw", a_fn=relu2, tm=DFF // 4, tn=D, chip_of_tile=lambda i, j: i)
    gw_ff1 = _matmul(h2, du, "tn", BF16, "ff1_dw", tm=D, tn=DFF // 4, tk=S, chip_of_tile=lambda i, j: j)
    ff_handle, ff_token = _spread_start([gw_ff1, gw_ff2], True, True, dev, chip, "scatter_ff_start")
    dh2 = _matmul(du, w_ff1_f, "nt", F32, "ff1_dx")

    def resid_bwd(r, k):
        x_, mix_, dx2a, dh2_ = r
        _, vjp = jax.vjp(_resid_normmod, x_, mix_, *k)
        dx, dmix_, dg, dnw, dsc, dsh = vjp((dx2a, dh2_))
        return [dx, dmix_], [dg, dnw, dsc, dsh]

    dx2, dmix, dgate1, g_norm2, dscale2, dshift2 = _rows(
        "resid_norm2_bwd", resid_bwd, [(xs, 0, D), (mix, 0, D), (dy, 0, D), (dh2, 0, D)],
        [gate1 + ff_token[0, 0], norm2_w, scale2, shift2], [(D, F32), (D, BF16)], [(1, D)] * 4, S)
    gw_out = jnp.concatenate([_matmul(y_ssd, dmix, "tn", BF16, "out_proj_dw_a"),
                              _matmul(y_att, dmix, "tn", BF16, "out_proj_dw_b")], axis=0)
    out_handle, out_token = _spread_start([gw_out.reshape(4, MIX // 4, D)], True, True, dev, chip, "scatter_out_start")
    dy_ssd = _matmul(dmix, w_out_a, "nt", F32, "out_proj_dx_a")
    dy_att = _matmul(dmix, w_out_b, "nt", F32, "out_proj_dx_b")

    def attn_norm_bwd(r, k):
        o_, dyo = r
        _, vjp = jax.vjp(_rmsw, o_, k[0])
        do_, dw_ = vjp(dyo)
        lo = _lane_mask()
        dd_blocks = []
        for b in range(AD // LANES):
            t = (do_ * o_)[:, b * LANES:(b + 1) * LANES]
            s0 = jnp.sum(jnp.where(lo, t, 0.0), axis=1, keepdims=True)
            s1 = jnp.sum(jnp.where(lo, 0.0, t), axis=1, keepdims=True)
            dd_blocks.append(jnp.where(lo, s0, s1))
        return [do_, jnp.concatenate(dd_blocks, axis=1)], [dw_]

    do_att, dd_att, g_attn_norm = _rows("attn_norm_bwd", attn_norm_bwd, [(o_att, 0, AD), (dy_att, 0, AD)],
                                        [attn_norm_w + out_token[0, 0]], [(AD, F32), (AD, F32)], [(1, AD)], S)
    dq_n, dk_n, dv = _attn_bwd(qn, kn, proj, o_v // LANES, do_att, lse, dd_att, "attn_bwd")

    def qk_bwd_call(name, col0, w2, scale, g):
        def body(t_ref, w_ref, g_ref, o_ref, dw_ref):
            @pl.when(pl.program_id(0) == 0)
            def _():
                dw_ref[...] = jnp.zeros_like(dw_ref)
            w = w_ref[...]
            dw_sum = jnp.zeros_like(w)
            for r0 in range(0, S, NORM_ROWS):
                rows = slice(r0, r0 + NORM_ROWS)
                _, vjp = jax.vjp(lambda t, w_: _headnorm(t, w_, scale), t_ref[rows, :], w)
                dt_, dw_ = vjp(g_ref[rows, :])
                o_ref[rows, :] = dt_.astype(BF16)
                dw_sum = dw_sum + dw_
            dw_ref[...] += dw_sum
        blk = pl.BlockSpec((S, LANES), lambda j: (0, j))
        return pl.pallas_call(
            body, name=name, grid=(AD // LANES,),
            in_specs=[pl.BlockSpec((S, LANES), lambda j: (0, j + col0 // LANES)),
                      pl.BlockSpec((1, LANES), lambda j: (0, 0)), blk],
            out_specs=[blk, pl.BlockSpec((1, LANES), lambda j: (0, 0))],
            out_shape=[jax.ShapeDtypeStruct((S, AD), BF16), jax.ShapeDtypeStruct((1, LANES), F32)],
            compiler_params=_cparams(("arbitrary",)),
        )(proj, w2, g)

    dq, g_qw2 = qk_bwd_call("q_norm_bwd", o_q, qw2, HD ** -0.5, dq_n)
    dk, g_kw2 = qk_bwd_call("k_norm_bwd", o_k, kw2, 1.0, dk_n)
    g_q_norm = g_qw2[:, :HD] + g_qw2[:, HD:]
    g_k_norm = g_kw2[:, :HD] + g_kw2[:, HD:]

    dxbc, dz, ddtr, g_dtb, g_alog, g_dsk, g_ssd_norm = _ssd_bwd(
        xbc, proj, o_dt // LANES, dtb, alog, dsk, ssd_norm_w, hsave, dy_ssd, "ssd_bwd")
    dxbc_pre, g_conv_w, g_conv_b = _conv_bwd(proj, o_xbc, CC, conv_w_f, conv_b, dxbc, "conv_bwd")
    dproj = jnp.concatenate([dz.astype(BF16), dxbc_pre.astype(BF16), dq, dk, dv.astype(BF16), ddtr.astype(BF16)], axis=1)
    gw_proj = _matmul(h1, dproj, "tn", F32, "in_proj_dw", tn=896, tk=S)
    gw_halves = gw_proj.reshape(2, D // 2, PW)
    sum_p = _pair_add(gw_halves, _pair_swap([gw_halves], "pair_swap_in")[0], ci, "pair_add_in")
    sum_in = jnp.concatenate([sum_p[:, :n_zx], sum_p[:, o_dt:o_dt + NH_SSD], sum_p[:, n_zx:o_dt]], axis=1)
    in_handle, in_token = _spread_start([sum_in.reshape(D // 2, 4, wsh).transpose(1, 0, 2)], True, False, dev, chip,
                                        "scatter_in_start")
    dh1 = _matmul(dproj, w_proj, "nt", F32, "in_proj_dx", tm=512, tk=PW)

    def norm1_bwd(r, k):
        x_, dh_, dres = r
        _, vjp = jax.vjp(_normmod, x_, *k)
        dx, dnw, dsc, dsh = vjp(dh_)
        return [dx + dres], [dnw, dsc, dsh]

    grad_x, g_norm1, dscale1, dshift1 = _rows("norm1_bwd", norm1_bwd, [(xs, 0, D), (dh1, 0, D), (dx2, 0, D)],
                                              [norm1_w + in_token[0, 0], scale1, shift1], [(D, F32)], [(1, D)] * 3, S)
    dmod =jnp.concatenate([dshift1, dscale1, dgate1, dshift2, dscale2, dgate2], axis=1)

    small = [g_norm1, g_norm2, dmod, g_conv_b, g_dtb, g_alog, g_dsk, g_ssd_norm, _pad_lanes(g_q_norm),
             _pad_lanes(g_k_norm), g_attn_norm, g_conv_w.reshape(1, KCONV * CC)]
    sizes = [t.shape[1] for t in small]
    packed = jnp.concatenate(small, axis=1)
    nrow = -(-packed.shape[1] // LANES // 8) * 8
    packed = jnp.pad(packed, ((0, 0), (0, nrow * LANES - packed.shape[1]))).reshape(nrow, LANES)
    packed_all = _exchange8([packed], False, "gather_small_grads")[0]
    tot = _sum_parts(packed_all, "sum_small_grads").reshape(1, nrow * LANES)
    offs = [sum(sizes[:i]) for i in range(len(sizes))]
    (g_norm1, g_norm2, g_b_ada, g_conv_b, g_dtb, g_alog, g_dsk, g_ssd_norm, g_q_norm, g_k_norm, g_attn_norm,
     g_conv_w) = [tot[:, o:o + n] for o, n in zip(offs, sizes)]
    g_dtb, g_alog, g_dsk = g_dtb[:, :NH_SSD], g_alog[:, :NH_SSD], g_dsk[:, :NH_SSD]
    g_q_norm, g_k_norm = g_q_norm[:, :HD], g_k_norm[:, :HD]
    ccs = CC // 4
    g_conv_w = lax.dynamic_slice_in_dim(g_conv_w.reshape(KCONV, CC), chip * ccs, ccs, 1)

    dmod_all = packed_all.reshape(8, nrow * LANES)[:, offs[2]:offs[2] + 6 * D]
    dmod_sh = jnp.pad(lax.dynamic_slice_in_dim(dmod_all, chip * nmod, nmod, 1), ((0, 8), (0, 0)))
    gw_ada = _mod_wgrad(c_all, dmod_sh, "mod_wgrad")

    parts_ff1, parts_ff2 = _spread_wait(ff_handle, True, True, in_token, "scatter_ff_wait")
    res_ff1 = _adamw(parts_ff1, w_ff1[0], m_w_ff1[0], v_w_ff1[0], "adamw_w_ff1")
    res_ff2 = _adamw(parts_ff2, w_ff2[0], m_w_ff2[0], v_w_ff2[0], "adamw_w_ff2")
    parts_out = _spread_wait(out_handle, True, True, in_token, "scatter_out_wait")[0]
    res_out = _adamw(parts_out, w_out[0], m_w_out[0], v_w_out[0], "adamw_w_out")
    res_ada = _adamw(gw_ada[None], w_ada[0], m_w_ada[0], v_w_ada[0], "adamw_w_ada")
    lands_in = _sibling_fill(_spread_wait(in_handle, True, False, res_ada[0], "scatter_in_wait"), "scatter_in_fill")[0]
    res_in = _adamw(lands_in.reshape(4, D, wsh), w_in[0], m_w_in[0], v_w_in[0], "adamw_w_in")

    small_names = ["norm1_w", "norm2_w", "b_ada", "conv_w", "conv_b", "dt_bias", "a_log", "d_skip", "ssd_norm_w",
                   "q_norm_w", "k_norm_w", "attn_norm_w"]
    small_g = dict(norm1_w=g_norm1, norm2_w=g_norm2, b_ada=g_b_ada, conv_w=g_conv_w.reshape(1, KCONV * ccs),
                   conv_b=g_conv_b, dt_bias=g_dtb, a_log=g_alog, d_skip=g_dsk, ssd_norm_w=g_ssd_norm, q_norm_w=g_q_norm,
                   k_norm_w=g_k_norm, attn_norm_w=g_attn_norm)
    small_w = dict(norm1_w=(norm1_w, m_norm1_w, v_norm1_w), norm2_w=(norm2_w, m_norm2_w, v_norm2_w),
                   b_ada=(b_ada, m_b_ada, v_b_ada),
                   conv_w=tuple(t.reshape(1, KCONV * ccs) for t in (conv_w, m_conv_w, v_conv_w)),
                   conv_b=(conv_b, m_conv_b, v_conv_b), dt_bias=(dt_bias, m_dt_bias, v_dt_bias),
                   a_log=(a_log, m_a_log, v_a_log), d_skip=(d_skip, m_d_skip, v_d_skip),
                   ssd_norm_w=(ssd_norm_w, m_ssd_norm_w, v_ssd_norm_w), q_norm_w=(q_norm_w, m_q_norm_w, v_q_norm_w),
                   k_norm_w=(k_norm_w, m_k_norm_w, v_k_norm_w), attn_norm_w=(attn_norm_w, m_attn_norm_w, v_attn_norm_w))
    ssz = [_pad_lanes(small_g[n]).shape[1] for n in small_names]
    soff = [sum(ssz[:i]) for i in range(len(ssz))]
    srow = -(-sum(ssz) // LANES // 8) * 8

    def pack(ts, fill):
        t = jnp.concatenate([jnp.pad(t, ((0, 0), (0, (-t.shape[1]) % LANES)), constant_values=fill) for t in ts], axis=1)
        return jnp.pad(t, ((0, 0), (0, srow * LANES - t.shape[1])), constant_values=fill).reshape(srow, LANES)

    sg = pack([small_g[n] for n in small_names], 0.0)
    sw = pack([small_w[n][0] for n in small_names], 0.0)
    sm_ = pack([small_w[n][1] for n in small_names], 0.0)
    sv = pack([small_w[n][2] for n in small_names], 1.0)
    _, s_delta, s_m, s_v = _adamw(sg[None], sw, sm_, sv, "adamw_small", tm=srow)

    def unpack(t, n):
        i = small_names.index(n)
        return t.reshape(1, srow * LANES)[:, soff[i]:soff[i] + small_g[n].shape[1]].reshape(small_w[n][0].shape)

    loss = lax.psum(loss_p[0, 0], ("x", "y", "c"))
    big_res = dict(w_ada=res_ada, w_in=res_in, w_out=res_out, w_ff1=res_ff1, w_ff2=res_ff2)
    order = ["norm1_w", "norm2_w", "w_ada", "b_ada", "w_in", "conv_w", "conv_b", "dt_bias", "a_log", "d_skip",
             "ssd_norm_w", "q_norm_w", "k_norm_w", "attn_norm_w", "w_out", "w_ff1", "w_ff2"]
    grads, deltas, new_m, new_v = [], [], [], []
    for n in order:
        if n in big_res:
            g_, d_, m_, v_ = [t[None] for t in big_res[n]]
        else:
            g_ = small_g[n].reshape(small_w[n][0].shape)
            d_, m_, v_ = unpack(s_delta, n), unpack(s_m, n), unpack(s_v, n)
            if n == "conv_w":
                g_, d_, m_, v_ = [t.reshape(conv_w.shape) for t in (g_, d_, m_, v_)]
        grads.append(g_)
        deltas.append(d_)
        new_m.append(m_)
        new_v.append(v_)
    return (loss, grad_x[None], *grads, *deltas, *new_m, *new_v)
```
